```python
import jax, jax.numpy as jnp
from jax import lax
import numpy as np

D_MODEL = 1024
BATCH = 8
SEQ = 4096
DEPTH = 1

CHUNK = 64
N_MEM = 256
DN_HEADS = 4
DN_HEAD_DIM = 128
DN_WIDTH = DN_HEADS * DN_HEAD_DIM
CONV_K = 4
POOL_GROUPS = 4
POOL_WINDOWS = (2, 4, 8, 16)
POOL_WIDTH = D_MODEL // 2
POOL_GROUP_DIM = POOL_WIDTH // POOL_GROUPS
D_FF = 2816
XA_HEADS = 4
XA_HEAD_DIM = D_MODEL // XA_HEADS
LN_EPS = 1e-5
RMS_EPS = 1e-6
L2_EPS = 1e-6
ALPHA = (2.0 * DEPTH) ** 0.25
BETA_INIT = (8.0 * DEPTH) ** -0.25

OFF_QKV = 3 * DN_WIDTH
OFF_Z = OFF_QKV + DN_WIDTH
OFF_A = OFF_Z + DN_HEADS
OFF_B = OFF_A + DN_HEADS
OFF_POOL = OFF_B + POOL_WIDTH
OFF_GDN = OFF_POOL + D_MODEL
IN_COLS = OFF_GDN + D_MODEL
IN_SPLITS = (OFF_QKV, OFF_Z, OFF_A, OFF_B, OFF_POOL, OFF_GDN)

kernel_name = "hybrid_deltanet_pool_macaron_deepnorm"


def _layernorm(x, g, b):
    xf = x.astype(jnp.float32)
    mu = jnp.mean(xf, axis=-1, keepdims=True)
    var = jnp.mean(jnp.square(xf - mu), axis=-1, keepdims=True)
    y = (xf - mu) * lax.rsqrt(var + LN_EPS) * g.astype(jnp.float32) + b.astype(jnp.float32)
    return y.astype(x.dtype)


def _swiglu(x, w_gate, w_up, w_down):
    return (jax.nn.silu(x @ w_gate) * (x @ w_up)) @ w_down


def _causal_dwconv(x, w):
    c = x.shape[-1]
    return lax.conv_general_dilated(
        x, w[:, None, :].astype(x.dtype), window_strides=(1,), padding=[(CONV_K - 1, 0)],
        dimension_numbers=("NWC", "WIO", "NWC"), feature_group_count=c)


def _l2norm(x):
    return x * lax.rsqrt(jnp.sum(jnp.square(x), axis=-1, keepdims=True) + L2_EPS)


def _gated_delta_rule(q, k, v, g, beta):
    B, L, H, D = q.shape
    n = L // CHUNK
    ch = lambda t: t.reshape(B, n, CHUNK, H, D).transpose(1, 0, 3, 2, 4)
    q, k, v = ch(q), ch(k), ch(v)
    g = g.reshape(B, n, CHUNK, H).transpose(1, 0, 3, 2)
    beta = beta.reshape(B, n, CHUNK, H).transpose(1, 0, 3, 2)

    gc = jnp.cumsum(g, axis=-1)
    idx = jnp.arange(CHUNK)
    incl = idx[:, None] >= idx[None, :]
    strict = idx[:, None] > idx[None, :]
    diff = gc[..., :, None] - gc[..., None, :]
    decay = jnp.exp(jnp.where(incl, diff, -jnp.inf))

    kb = k * beta[..., None]
    a_low = jnp.where(strict, jnp.einsum("nbhcd,nbhed->nbhce", kb, k) * decay, 0.0)
    eye = jnp.eye(CHUNK, dtype=jnp.float32)
    rhs = jnp.concatenate([v * beta[..., None], kb * jnp.exp(gc)[..., None]], axis=-1)
    sol = lax.linalg.triangular_solve(a_low + eye, rhs, left_side=True, lower=True,
                                      unit_diagonal=True)
    u, w = sol[..., :D], sol[..., D:]

    attn = jnp.einsum("nbhcd,nbhed->nbhce", q, k) * decay
    q_dec = q * jnp.exp(gc)[..., None]
    g_last = gc[..., -1]
    k_dec = k * jnp.exp(g_last[..., None] - gc)[..., None]

    def step(state, inp):
        qd, kd, ui, wi, ai, gl = inp
        v_new = ui - jnp.einsum("bhcd,bhde->bhce", wi, state)
        o = jnp.einsum("bhcd,bhde->bhce", qd, state) + jnp.einsum("bhce,bhef->bhcf", ai, v_new)
        state = state * jnp.exp(gl)[..., None, None] + jnp.einsum("bhcd,bhce->bhde", kd, v_new)
        return state, o

    s0 = jnp.zeros((B, H, D, D), jnp.float32)
    _, o = lax.scan(step, s0, (q_dec, k_dec, u, w, attn, g_last))
    return o.transpose(1, 0, 3, 2, 4).reshape(B, L, H, D)


def _multiscale_pool(p, pool_w, pool_scale):
    B, L, _ = p.shape
    pf = p.astype(jnp.float32).reshape(B, L, POOL_GROUPS, POOL_GROUP_DIM)
    csum = jnp.cumsum(pf, axis=1)
    t = (jnp.arange(L, dtype=jnp.int32) + 1).astype(jnp.float32)
    means = []
    for gi, win in enumerate(POOL_WINDOWS):
        c = csum[:, :, gi]
        lag = jnp.pad(c, ((0, 0), (win, 0), (0, 0)))[:, :L]
        cnt = jnp.minimum(t, float(win))[None, :, None]
        means.append((c - lag) / cnt)
    mixed = jnp.stack(means, axis=2) - pf
    y = jnp.einsum("blgc,gcd->blgd", mixed, pool_w.astype(jnp.float32)).reshape(B, L, POOL_WIDTH)
    return (y * pool_scale.astype(jnp.float32)).astype(p.dtype)


def _hybrid_mixer(h, w_in, conv_w, a_log, dt_bias, dn_norm_w, w_dn_branch,
                  pool_w, pool_scale, w_pool_branch, w_mix_out):
    B, L, _ = h.shape
    proj = h @ w_in
    qkv, z, a, b, p, gate_dn, gate_pool = jnp.split(proj, IN_SPLITS, axis=-1)

    qkv = jax.nn.silu(_causal_dwconv(qkv, conv_w)).astype(jnp.float32)
    q, k, v = jnp.split(qkv, 3, axis=-1)
    hs = (B, L, DN_HEADS, DN_HEAD_DIM)
    q = _l2norm(q.reshape(hs)) * (DN_HEAD_DIM ** -0.5)
    k = _l2norm(k.reshape(hs))
    v = v.reshape(hs)
    beta = jax.nn.sigmoid(b.astype(jnp.float32))
    g = -jnp.exp(a_log.astype(jnp.float32)) * jax.nn.softplus(
        a.astype(jnp.float32) + dt_bias.astype(jnp.float32))
    o = _gated_delta_rule(q, k, v, g, beta)
    o = o * lax.rsqrt(jnp.mean(jnp.square(o), axis=-1, keepdims=True) + RMS_EPS)
    o = o * dn_norm_w.astype(jnp.float32) * jax.nn.silu(z.astype(jnp.float32).reshape(hs))
    y_dn = o.reshape(B, L, DN_WIDTH).astype(h.dtype) @ w_dn_branch

    y_pool = _multiscale_pool(p, pool_w, pool_scale) @ w_pool_branch

    merged = jax.nn.sigmoid(gate_dn) * y_dn + jax.nn.sigmoid(gate_pool) * y_pool
    return merged @ w_mix_out


def _cross_attention(h, m, wq, wk, wv, wo):
    B, L, _ = h.shape
    q = (h @ wq).reshape(B, L, XA_HEADS, XA_HEAD_DIM)
    k = (m @ wk).reshape(B, N_MEM, XA_HEADS, XA_HEAD_DIM)
    v = (m @ wv).reshape(B, N_MEM, XA_HEADS, XA_HEAD_DIM)
    s = jnp.einsum("bqhd,bkhd->bhqk", q, k).astype(jnp.float32) * (XA_HEAD_DIM ** -0.5)
    pr = jax.nn.softmax(s, axis=-1).astype(h.dtype)
    o = jnp.einsum("bhqk,bkhd->bqhd", pr, v).reshape(B, L, D_MODEL)
    return o @ wo


def _fwd_setup_inputs(seed: int = 0) -> dict:
    key = jax.random.key(seed)
    ks = iter(jax.random.split(key, 40))
    f32 = jnp.float32

    def nrm(shape, scale):
        return jax.random.normal(next(ks), shape, f32) * scale

    def gain(shape):
        return 1.0 + 0.02 * jax.random.normal(next(ks), shape, f32)

    def bias(shape):
        return 0.02 * jax.random.normal(next(ks), shape, f32)

    Dp = DEPTH
    d = D_MODEL
    x = jax.random.normal(next(ks), (BATCH, SEQ, d), f32)
    mem = jax.random.normal(next(ks), (BATCH, N_MEM, d), f32)

    ffn1_w_gate = nrm((Dp, d, D_FF), d ** -0.5)
    ffn1_w_up = nrm((Dp, d, D_FF), d ** -0.5)
    ffn1_w_down = nrm((Dp, D_FF, d), BETA_INIT * D_FF ** -0.5)
    ln1_g, ln1_b = gain((Dp, d)), bias((Dp, d))

    w_in = nrm((Dp, d, IN_COLS), d ** -0.5)
    conv_w = nrm((Dp, CONV_K, 3 * DN_WIDTH), CONV_K ** -0.5)
    a_log = jnp.log(jax.random.uniform(next(ks), (Dp, DN_HEADS), f32, 1.0, 16.0))
    dt = jnp.exp(jax.random.uniform(next(ks), (Dp, DN_HEADS), f32,
                                    float(np.log(1e-3)), float(np.log(1e-1))))
    dt_bias = dt + jnp.log(-jnp.expm1(-dt))
    dn_norm_w = gain((Dp, DN_HEAD_DIM))
    w_dn_branch = nrm((Dp, DN_WIDTH, d), DN_WIDTH ** -0.5)
    pool_w = nrm((Dp, POOL_GROUPS, POOL_GROUP_DIM, POOL_GROUP_DIM), POOL_GROUP_DIM ** -0.5)
    pool_scale = gain((Dp, POOL_WIDTH))
    w_pool_branch = nrm((Dp, POOL_WIDTH, d), POOL_WIDTH ** -0.5)
    w_mix_out = nrm((Dp, d, d), BETA_INIT * d ** -0.5)
    ln2_g, ln2_b = gain((Dp, d)), bias((Dp, d))

    mem_ln_g, mem_ln_b = gain((Dp, d)), bias((Dp, d))
    xa_wq = nrm((Dp, d, d), d ** -0.5)
    xa_wk = nrm((Dp, d, d), d ** -0.5)
    xa_wv = nrm((Dp, d, d), d ** -0.5)
    xa_wo = nrm((Dp, d, d), BETA_INIT * d ** -0.5)
    ln3_g, ln3_b = gain((Dp, d)), bias((Dp, d))

    ffn2_w_gate = nrm((Dp, d, D_FF), d ** -0.5)
    ffn2_w_up = nrm((Dp, d, D_FF), d ** -0.5)
    ffn2_w_down = nrm((Dp, D_FF, d), BETA_INIT * D_FF ** -0.5)
    ln4_g, ln4_b = gain((Dp, d)), bias((Dp, d))

    return {"x": x, "mem": mem,
            "ffn1_w_gate": ffn1_w_gate, "ffn1_w_up": ffn1_w_up, "ffn1_w_down": ffn1_w_down,
            "ln1_g": ln1_g, "ln1_b": ln1_b,
            "w_in": w_in, "conv_w": conv_w, "a_log": a_log, "dt_bias": dt_bias,
            "dn_norm_w": dn_norm_w, "w_dn_branch": w_dn_branch,
            "pool_w": pool_w, "pool_scale": pool_scale, "w_pool_branch": w_pool_branch,
            "w_mix_out": w_mix_out, "ln2_g": ln2_g, "ln2_b": ln2_b,
            "mem_ln_g": mem_ln_g, "mem_ln_b": mem_ln_b,
            "xa_wq": xa_wq, "xa_wk": xa_wk, "xa_wv": xa_wv, "xa_wo": xa_wo,
            "ln3_g": ln3_g, "ln3_b": ln3_b,
            "ffn2_w_gate": ffn2_w_gate, "ffn2_w_up": ffn2_w_up, "ffn2_w_down": ffn2_w_down,
            "ln4_g": ln4_g, "ln4_b": ln4_b}


def _fwd_reference(x, mem, ffn1_w_gate, ffn1_w_up, ffn1_w_down, ln1_g, ln1_b,
              w_in, conv_w, a_log, dt_bias, dn_norm_w, w_dn_branch,
              pool_w, pool_scale, w_pool_branch, w_mix_out, ln2_g, ln2_b,
              mem_ln_g, mem_ln_b, xa_wq, xa_wk, xa_wv, xa_wo, ln3_g, ln3_b,
              ffn2_w_gate, ffn2_w_up, ffn2_w_down, ln4_g, ln4_b):
    for l in range(DEPTH):
        x = _layernorm(ALPHA * x + 0.5 * _swiglu(x, ffn1_w_gate[l], ffn1_w_up[l], ffn1_w_down[l]),
                       ln1_g[l], ln1_b[l])
        mix = _hybrid_mixer(x, w_in[l], conv_w[l], a_log[l], dt_bias[l], dn_norm_w[l],
                            w_dn_branch[l], pool_w[l], pool_scale[l], w_pool_branch[l], w_mix_out[l])
        x = _layernorm(ALPHA * x + mix, ln2_g[l], ln2_b[l])
        m = _layernorm(mem, mem_ln_g[l], mem_ln_b[l])
        xa = _cross_attention(x, m, xa_wq[l], xa_wk[l], xa_wv[l], xa_wo[l])
        x = _layernorm(ALPHA * x + xa, ln3_g[l], ln3_b[l])
        x = _layernorm(ALPHA * x + 0.5 * _swiglu(x, ffn2_w_gate[l], ffn2_w_up[l], ffn2_w_down[l]),
                       ln4_g[l], ln4_b[l])
    return x


import jax as _jax
import jax.numpy as _jnp

TWIN_FORMAT = 'train_step'
FWD_PARAMS = ['x', 'mem', 'ffn1_w_gate', 'ffn1_w_up', 'ffn1_w_down', 'ln1_g', 'ln1_b', 'w_in', 'conv_w', 'a_log', 'dt_bias', 'dn_norm_w', 'w_dn_branch', 'pool_w', 'pool_scale', 'w_pool_branch', 'w_mix_out', 'ln2_g', 'ln2_b', 'mem_ln_g', 'mem_ln_b', 'xa_wq', 'xa_wk', 'xa_wv', 'xa_wo', 'ln3_g', 'ln3_b', 'ffn2_w_gate', 'ffn2_w_up', 'ffn2_w_down', 'ln4_g', 'ln4_b']
TWIN_WEIGHTS = ['ffn1_w_gate', 'ffn1_w_up', 'ffn1_w_down', 'ln1_g', 'ln1_b', 'w_in', 'conv_w', 'a_log', 'dt_bias', 'dn_norm_w', 'w_dn_branch', 'pool_w', 'pool_scale', 'w_pool_branch', 'w_mix_out', 'ln2_g', 'ln2_b', 'mem_ln_g', 'mem_ln_b', 'xa_wq', 'xa_wk', 'xa_wv', 'xa_wo', 'ln3_g', 'ln3_b', 'ffn2_w_gate', 'ffn2_w_up', 'ffn2_w_down', 'ln4_g', 'ln4_b']
TWIN_DIFF_INPUT = 'x'
TWIN_INPUTS = ['x', 'mem', 'ffn1_w_gate', 'ffn1_w_up', 'ffn1_w_down', 'ln1_g', 'ln1_b', 'w_in', 'conv_w', 'a_log', 'dt_bias', 'dn_norm_w', 'w_dn_branch', 'pool_w', 'pool_scale', 'w_pool_branch', 'w_mix_out', 'ln2_g', 'ln2_b', 'mem_ln_g', 'mem_ln_b', 'xa_wq', 'xa_wk', 'xa_wv', 'xa_wo', 'ln3_g', 'ln3_b', 'ffn2_w_gate', 'ffn2_w_up', 'ffn2_w_down', 'ln4_g', 'ln4_b', 'loss_target', 'm_ffn1_w_gate', 'm_ffn1_w_up', 'm_ffn1_w_down', 'm_ln1_g', 'm_ln1_b', 'm_w_in', 'm_conv_w', 'm_a_log', 'm_dt_bias', 'm_dn_norm_w', 'm_w_dn_branch', 'm_pool_w', 'm_pool_scale', 'm_w_pool_branch', 'm_w_mix_out', 'm_ln2_g', 'm_ln2_b', 'm_mem_ln_g', 'm_mem_ln_b', 'm_xa_wq', 'm_xa_wk', 'm_xa_wv', 'm_xa_wo', 'm_ln3_g', 'm_ln3_b', 'm_ffn2_w_gate', 'm_ffn2_w_up', 'm_ffn2_w_down', 'm_ln4_g', 'm_ln4_b', 'v_ffn1_w_gate', 'v_ffn1_w_up', 'v_ffn1_w_down', 'v_ln1_g', 'v_ln1_b', 'v_w_in', 'v_conv_w', 'v_a_log', 'v_dt_bias', 'v_dn_norm_w', 'v_w_dn_branch', 'v_pool_w', 'v_pool_scale', 'v_w_pool_branch', 'v_w_mix_out', 'v_ln2_g', 'v_ln2_b', 'v_mem_ln_g', 'v_mem_ln_b', 'v_xa_wq', 'v_xa_wk', 'v_xa_wv', 'v_xa_wo', 'v_ln3_g', 'v_ln3_b', 'v_ffn2_w_gate', 'v_ffn2_w_up', 'v_ffn2_w_down', 'v_ln4_g', 'v_ln4_b']
TWIN_OUTPUTS = ['loss', 'grad_x', 'grad_ffn1_w_gate', 'grad_ffn1_w_up', 'grad_ffn1_w_down', 'grad_ln1_g', 'grad_ln1_b', 'grad_w_in', 'grad_conv_w', 'grad_a_log', 'grad_dt_bias', 'grad_dn_norm_w', 'grad_w_dn_branch', 'grad_pool_w', 'grad_pool_scale', 'grad_w_pool_branch', 'grad_w_mix_out', 'grad_ln2_g', 'grad_ln2_b', 'grad_mem_ln_g', 'grad_mem_ln_b', 'grad_xa_wq', 'grad_xa_wk', 'grad_xa_wv', 'grad_xa_wo', 'grad_ln3_g', 'grad_ln3_b', 'grad_ffn2_w_gate', 'grad_ffn2_w_up', 'grad_ffn2_w_down', 'grad_ln4_g', 'grad_ln4_b', 'delta_ffn1_w_gate', 'delta_ffn1_w_up', 'delta_ffn1_w_down', 'delta_ln1_g', 'delta_ln1_b', 'delta_w_in', 'delta_conv_w', 'delta_a_log', 'delta_dt_bias', 'delta_dn_norm_w', 'delta_w_dn_branch', 'delta_pool_w', 'delta_pool_scale', 'delta_w_pool_branch', 'delta_w_mix_out', 'delta_ln2_g', 'delta_ln2_b', 'delta_mem_ln_g', 'delta_mem_ln_b', 'delta_xa_wq', 'delta_xa_wk', 'delta_xa_wv', 'delta_xa_wo', 'delta_ln3_g', 'delta_ln3_b', 'delta_ffn2_w_gate', 'delta_ffn2_w_up', 'delta_ffn2_w_down', 'delta_ln4_g', 'delta_ln4_b', 'new_m_ffn1_w_gate', 'new_m_ffn1_w_up', 'new_m_ffn1_w_down', 'new_m_ln1_g', 'new_m_ln1_b', 'new_m_w_in', 'new_m_conv_w', 'new_m_a_log', 'new_m_dt_bias', 'new_m_dn_norm_w', 'new_m_w_dn_branch', 'new_m_pool_w', 'new_m_pool_scale', 'new_m_w_pool_branch', 'new_m_w_mix_out', 'new_m_ln2_g', 'new_m_ln2_b', 'new_m_mem_ln_g', 'new_m_mem_ln_b', 'new_m_xa_wq', 'new_m_xa_wk', 'new_m_xa_wv', 'new_m_xa_wo', 'new_m_ln3_g', 'new_m_ln3_b', 'new_m_ffn2_w_gate', 'new_m_ffn2_w_up', 'new_m_ffn2_w_down', 'new_m_ln4_g', 'new_m_ln4_b', 'new_v_ffn1_w_gate', 'new_v_ffn1_w_up', 'new_v_ffn1_w_down', 'new_v_ln1_g', 'new_v_ln1_b', 'new_v_w_in', 'new_v_conv_w', 'new_v_a_log', 'new_v_dt_bias', 'new_v_dn_norm_w', 'new_v_w_dn_branch', 'new_v_pool_w', 'new_v_pool_scale', 'new_v_w_pool_branch', 'new_v_w_mix_out', 'new_v_ln2_g', 'new_v_ln2_b', 'new_v_mem_ln_g', 'new_v_mem_ln_b', 'new_v_xa_wq', 'new_v_xa_wk', 'new_v_xa_wv', 'new_v_xa_wo', 'new_v_ln3_g', 'new_v_ln3_b', 'new_v_ffn2_w_gate', 'new_v_ffn2_w_up', 'new_v_ffn2_w_down', 'new_v_ln4_g', 'new_v_ln4_b']
TWIN_LEAF_KINDS = {'loss': 'loss', 'grad_x': 'grad_x', 'grad_ffn1_w_gate': 'grad_w', 'grad_ffn1_w_up': 'grad_w', 'grad_ffn1_w_down': 'grad_w', 'grad_ln1_g': 'grad_w', 'grad_ln1_b': 'grad_w', 'grad_w_in': 'grad_w', 'grad_conv_w': 'grad_w', 'grad_a_log': 'grad_w', 'grad_dt_bias': 'grad_w', 'grad_dn_norm_w': 'grad_w', 'grad_w_dn_branch': 'grad_w', 'grad_pool_w': 'grad_w', 'grad_pool_scale': 'grad_w', 'grad_w_pool_branch': 'grad_w', 'grad_w_mix_out': 'grad_w', 'grad_ln2_g': 'grad_w', 'grad_ln2_b': 'grad_w', 'grad_mem_ln_g': 'grad_w', 'grad_mem_ln_b': 'grad_w', 'grad_xa_wq': 'grad_w', 'grad_xa_wk': 'grad_w', 'grad_xa_wv': 'grad_w', 'grad_xa_wo': 'grad_w', 'grad_ln3_g': 'grad_w', 'grad_ln3_b': 'grad_w', 'grad_ffn2_w_gate': 'grad_w', 'grad_ffn2_w_up': 'grad_w', 'grad_ffn2_w_down': 'grad_w', 'grad_ln4_g': 'grad_w', 'grad_ln4_b': 'grad_w', 'delta_ffn1_w_gate': 'delta_w', 'delta_ffn1_w_up': 'delta_w', 'delta_ffn1_w_down': 'delta_w', 'delta_ln1_g': 'delta_w', 'delta_ln1_b': 'delta_w', 'delta_w_in': 'delta_w', 'delta_conv_w': 'delta_w', 'delta_a_log': 'delta_w', 'delta_dt_bias': 'delta_w', 'delta_dn_norm_w': 'delta_w', 'delta_w_dn_branch': 'delta_w', 'delta_pool_w': 'delta_w', 'delta_pool_scale': 'delta_w', 'delta_w_pool_branch': 'delta_w', 'delta_w_mix_out': 'delta_w', 'delta_ln2_g': 'delta_w', 'delta_ln2_b': 'delta_w', 'delta_mem_ln_g': 'delta_w', 'delta_mem_ln_b': 'delta_w', 'delta_xa_wq': 'delta_w', 'delta_xa_wk': 'delta_w', 'delta_xa_wv': 'delta_w', 'delta_xa_wo': 'delta_w', 'delta_ln3_g': 'delta_w', 'delta_ln3_b': 'delta_w', 'delta_ffn2_w_gate': 'delta_w', 'delta_ffn2_w_up': 'delta_w', 'delta_ffn2_w_down': 'delta_w', 'delta_ln4_g': 'delta_w', 'delta_ln4_b': 'delta_w', 'new_m_ffn1_w_gate': 'new_m', 'new_m_ffn1_w_up': 'new_m', 'new_m_ffn1_w_down': 'new_m', 'new_m_ln1_g': 'new_m', 'new_m_ln1_b': 'new_m', 'new_m_w_in': 'new_m', 'new_m_conv_w': 'new_m', 'new_m_a_log': 'new_m', 'new_m_dt_bias': 'new_m', 'new_m_dn_norm_w': 'new_m', 'new_m_w_dn_branch': 'new_m', 'new_m_pool_w': 'new_m', 'new_m_pool_scale': 'new_m', 'new_m_w_pool_branch': 'new_m', 'new_m_w_mix_out': 'new_m', 'new_m_ln2_g': 'new_m', 'new_m_ln2_b': 'new_m', 'new_m_mem_ln_g': 'new_m', 'new_m_mem_ln_b': 'new_m', 'new_m_xa_wq': 'new_m', 'new_m_xa_wk': 'new_m', 'new_m_xa_wv': 'new_m', 'new_m_xa_wo': 'new_m', 'new_m_ln3_g': 'new_m', 'new_m_ln3_b': 'new_m', 'new_m_ffn2_w_gate': 'new_m', 'new_m_ffn2_w_up': 'new_m', 'new_m_ffn2_w_down': 'new_m', 'new_m_ln4_g': 'new_m', 'new_m_ln4_b': 'new_m', 'new_v_ffn1_w_gate': 'new_v', 'new_v_ffn1_w_up': 'new_v', 'new_v_ffn1_w_down': 'new_v', 'new_v_ln1_g': 'new_v', 'new_v_ln1_b': 'new_v', 'new_v_w_in': 'new_v', 'new_v_conv_w': 'new_v', 'new_v_a_log': 'new_v', 'new_v_dt_bias': 'new_v', 'new_v_dn_norm_w': 'new_v', 'new_v_w_dn_branch': 'new_v', 'new_v_pool_w': 'new_v', 'new_v_pool_scale': 'new_v', 'new_v_w_pool_branch': 'new_v', 'new_v_w_mix_out': 'new_v', 'new_v_ln2_g': 'new_v', 'new_v_ln2_b': 'new_v', 'new_v_mem_ln_g': 'new_v', 'new_v_mem_ln_b': 'new_v', 'new_v_xa_wq': 'new_v', 'new_v_xa_wk': 'new_v', 'new_v_xa_wv': 'new_v', 'new_v_xa_wo': 'new_v', 'new_v_ln3_g': 'new_v', 'new_v_ln3_b': 'new_v', 'new_v_ffn2_w_gate': 'new_v', 'new_v_ffn2_w_up': 'new_v', 'new_v_ffn2_w_down': 'new_v', 'new_v_ln4_g': 'new_v', 'new_v_ln4_b': 'new_v'}


def _forward(args):
    return _fwd_reference(*[args[k] for k in FWD_PARAMS])


def _output_shape():
    def fwd():
        inp = _fwd_setup_inputs(0)
        return _fwd_reference(*[inp[k] for k in FWD_PARAMS])
    out = _jax.eval_shape(fwd)
    return out.shape, out.dtype

N_MICROBATCH = 1
ADAM_LR = 0.001
ADAM_B1 = 0.9
ADAM_B2 = 0.999
ADAM_EPS = 1e-08
ADAM_WD = 0.01
ADAM_STEP = 10
PER_EXAMPLE_BATCH_AXIS = {'x': 0, 'mem': 0, 'loss_target': 0}
SHARED_INPUTS = []
_WEIGHT_DTYPES = {'ffn1_w_gate': _jnp.float32, 'ffn1_w_up': _jnp.float32, 'ffn1_w_down': _jnp.float32, 'ln1_g': _jnp.float32, 'ln1_b': _jnp.float32, 'w_in': _jnp.float32, 'conv_w': _jnp.float32, 'a_log': _jnp.float32, 'dt_bias': _jnp.float32, 'dn_norm_w': _jnp.float32, 'w_dn_branch': _jnp.float32, 'pool_w': _jnp.float32, 'pool_scale': _jnp.float32, 'w_pool_branch': _jnp.float32, 'w_mix_out': _jnp.float32, 'ln2_g': _jnp.float32, 'ln2_b': _jnp.float32, 'mem_ln_g': _jnp.float32, 'mem_ln_b': _jnp.float32, 'xa_wq': _jnp.float32, 'xa_wk': _jnp.float32, 'xa_wv': _jnp.float32, 'xa_wo': _jnp.float32, 'ln3_g': _jnp.float32, 'ln3_b': _jnp.float32, 'ffn2_w_gate': _jnp.float32, 'ffn2_w_up': _jnp.float32, 'ffn2_w_down': _jnp.float32, 'ln4_g': _jnp.float32, 'ln4_b': _jnp.float32}
MOMENT_SCALE = {'ffn1_w_gate': 1.686968e-02, 'ffn1_w_up': 1.630003e-02, 'ffn1_w_down': 4.550489e-02, 'ln1_g': 1.066967e+00, 'ln1_b': 5.360262e-01, 'w_in': 3.026463e-02, 'conv_w': 3.117153e-02, 'a_log': 1.341320e-01, 'dt_bias': 1.244397e-01, 'dn_norm_w': 1.034547e-01, 'w_dn_branch': 3.018829e-02, 'pool_w': 5.863010e-02, 'pool_scale': 6.139835e-02, 'w_pool_branch': 4.161701e-02, 'w_mix_out': 8.682019e-02, 'ln2_g': 1.161570e+00, 'ln2_b': 5.590433e-01, 'mem_ln_g': 1.367030e-02, 'mem_ln_b': 3.104302e-01, 'xa_wq': 9.346958e-03, 'xa_wk': 9.372888e-03, 'xa_wv': 1.156928e-02, 'xa_wo': 1.916889e-02, 'ln3_g': 1.164775e+00, 'ln3_b': 5.615386e-01, 'ffn2_w_gate': 1.627499e-02, 'ffn2_w_up': 1.578057e-02, 'ffn2_w_down': 4.409635e-02, 'ln4_g': 3.206046e+01, 'ln4_b': 1.593896e+00}


def _to_microbatches(a, axis):
    t = _jnp.moveaxis(a, axis, 0)
    t = t.reshape((N_MICROBATCH, t.shape[0] // N_MICROBATCH) + t.shape[1:])
    return _jnp.moveaxis(t, 1, axis + 1)


def setup_inputs(seed: int = 0) -> dict:
    inp = _fwd_setup_inputs(seed)
    key = _jax.random.fold_in(_jax.random.key(seed), 7919)
    shape, _ = _output_shape()
    out = dict(inp)
    out["loss_target"] = _jax.random.normal(_jax.random.fold_in(key, 0), shape, _jnp.float32)
    for i, name in enumerate(TWIN_WEIGHTS):
        w = inp[name].astype(_jnp.float32)
        if MOMENT_SCALE is None:
            s = _jnp.sqrt(_jnp.mean(_jnp.square(w)) + 1e-30)
        else:
            s = MOMENT_SCALE[name]
        km, kv = _jax.random.split(_jax.random.fold_in(key, i + 1))
        out[name] = w
        out["m_" + name] = s * _jax.random.normal(km, w.shape, _jnp.float32)
        out["v_" + name] = (s * s) * _jax.random.uniform(kv, w.shape, _jnp.float32, 0.5, 1.5)
    if N_MICROBATCH > 1:
        for name, axis in PER_EXAMPLE_BATCH_AXIS.items():
            out[name] = _to_microbatches(out[name], axis)
    return {'x': out['x'], 'mem': out['mem'], 'ffn1_w_gate': out['ffn1_w_gate'], 'ffn1_w_up': out['ffn1_w_up'], 'ffn1_w_down': out['ffn1_w_down'], 'ln1_g': out['ln1_g'], 'ln1_b': out['ln1_b'], 'w_in': out['w_in'], 'conv_w': out['conv_w'], 'a_log': out['a_log'], 'dt_bias': out['dt_bias'], 'dn_norm_w': out['dn_norm_w'], 'w_dn_branch': out['w_dn_branch'], 'pool_w': out['pool_w'], 'pool_scale': out['pool_scale'], 'w_pool_branch': out['w_pool_branch'], 'w_mix_out': out['w_mix_out'], 'ln2_g': out['ln2_g'], 'ln2_b': out['ln2_b'], 'mem_ln_g': out['mem_ln_g'], 'mem_ln_b': out['mem_ln_b'], 'xa_wq': out['xa_wq'], 'xa_wk': out['xa_wk'], 'xa_wv': out['xa_wv'], 'xa_wo': out['xa_wo'], 'ln3_g': out['ln3_g'], 'ln3_b': out['ln3_b'], 'ffn2_w_gate': out['ffn2_w_gate'], 'ffn2_w_up': out['ffn2_w_up'], 'ffn2_w_down': out['ffn2_w_down'], 'ln4_g': out['ln4_g'], 'ln4_b': out['ln4_b'], 'loss_target': out['loss_target'], 'm_ffn1_w_gate': out['m_ffn1_w_gate'], 'm_ffn1_w_up': out['m_ffn1_w_up'], 'm_ffn1_w_down': out['m_ffn1_w_down'], 'm_ln1_g': out['m_ln1_g'], 'm_ln1_b': out['m_ln1_b'], 'm_w_in': out['m_w_in'], 'm_conv_w': out['m_conv_w'], 'm_a_log': out['m_a_log'], 'm_dt_bias': out['m_dt_bias'], 'm_dn_norm_w': out['m_dn_norm_w'], 'm_w_dn_branch': out['m_w_dn_branch'], 'm_pool_w': out['m_pool_w'], 'm_pool_scale': out['m_pool_scale'], 'm_w_pool_branch': out['m_w_pool_branch'], 'm_w_mix_out': out['m_w_mix_out'], 'm_ln2_g': out['m_ln2_g'], 'm_ln2_b': out['m_ln2_b'], 'm_mem_ln_g': out['m_mem_ln_g'], 'm_mem_ln_b': out['m_mem_ln_b'], 'm_xa_wq': out['m_xa_wq'], 'm_xa_wk': out['m_xa_wk'], 'm_xa_wv': out['m_xa_wv'], 'm_xa_wo': out['m_xa_wo'], 'm_ln3_g': out['m_ln3_g'], 'm_ln3_b': out['m_ln3_b'], 'm_ffn2_w_gate': out['m_ffn2_w_gate'], 'm_ffn2_w_up': out['m_ffn2_w_up'], 'm_ffn2_w_down': out['m_ffn2_w_down'], 'm_ln4_g': out['m_ln4_g'], 'm_ln4_b': out['m_ln4_b'], 'v_ffn1_w_gate': out['v_ffn1_w_gate'], 'v_ffn1_w_up': out['v_ffn1_w_up'], 'v_ffn1_w_down': out['v_ffn1_w_down'], 'v_ln1_g': out['v_ln1_g'], 'v_ln1_b': out['v_ln1_b'], 'v_w_in': out['v_w_in'], 'v_conv_w': out['v_conv_w'], 'v_a_log': out['v_a_log'], 'v_dt_bias': out['v_dt_bias'], 'v_dn_norm_w': out['v_dn_norm_w'], 'v_w_dn_branch': out['v_w_dn_branch'], 'v_pool_w': out['v_pool_w'], 'v_pool_scale': out['v_pool_scale'], 'v_w_pool_branch': out['v_w_pool_branch'], 'v_w_mix_out': out['v_w_mix_out'], 'v_ln2_g': out['v_ln2_g'], 'v_ln2_b': out['v_ln2_b'], 'v_mem_ln_g': out['v_mem_ln_g'], 'v_mem_ln_b': out['v_mem_ln_b'], 'v_xa_wq': out['v_xa_wq'], 'v_xa_wk': out['v_xa_wk'], 'v_xa_wv': out['v_xa_wv'], 'v_xa_wo': out['v_xa_wo'], 'v_ln3_g': out['v_ln3_g'], 'v_ln3_b': out['v_ln3_b'], 'v_ffn2_w_gate': out['v_ffn2_w_gate'], 'v_ffn2_w_up': out['v_ffn2_w_up'], 'v_ffn2_w_down': out['v_ffn2_w_down'], 'v_ln4_g': out['v_ln4_g'], 'v_ln4_b': out['v_ln4_b']}


def _loss(weights, diff, rest, loss_target):
    with _jax.named_scope("forward"):
        args = {**rest, TWIN_DIFF_INPUT: diff, **{k: w.astype(_WEIGHT_DTYPES[k]) for k, w in weights.items()}}
        y = _forward(args)
    with _jax.named_scope("loss_head"):
        err = _jnp.square(y.astype(_jnp.float32) - loss_target)
        return 0.5 * _jnp.sum(_jnp.mean(err, axis=-1)) if err.ndim else 0.5 * err


def _adamw(w, g, m, v):
    m = ADAM_B1 * m + (1.0 - ADAM_B1) * g
    v = ADAM_B2 * v + (1.0 - ADAM_B2) * _jnp.square(g)
    m_hat = m / (1.0 - ADAM_B1 ** ADAM_STEP)
    v_hat = v / (1.0 - ADAM_B2 ** ADAM_STEP)
    delta = -ADAM_LR * (m_hat / (_jnp.sqrt(v_hat) + ADAM_EPS) + ADAM_WD * w)
    return delta, m, v


def reference(x, mem, ffn1_w_gate, ffn1_w_up, ffn1_w_down, ln1_g, ln1_b, w_in, conv_w, a_log, dt_bias, dn_norm_w, w_dn_branch, pool_w, pool_scale, w_pool_branch, w_mix_out, ln2_g, ln2_b, mem_ln_g, mem_ln_b, xa_wq, xa_wk, xa_wv, xa_wo, ln3_g, ln3_b, ffn2_w_gate, ffn2_w_up, ffn2_w_down, ln4_g, ln4_b, loss_target, m_ffn1_w_gate, m_ffn1_w_up, m_ffn1_w_down, m_ln1_g, m_ln1_b, m_w_in, m_conv_w, m_a_log, m_dt_bias, m_dn_norm_w, m_w_dn_branch, m_pool_w, m_pool_scale, m_w_pool_branch, m_w_mix_out, m_ln2_g, m_ln2_b, m_mem_ln_g, m_mem_ln_b, m_xa_wq, m_xa_wk, m_xa_wv, m_xa_wo, m_ln3_g, m_ln3_b, m_ffn2_w_gate, m_ffn2_w_up, m_ffn2_w_down, m_ln4_g, m_ln4_b, v_ffn1_w_gate, v_ffn1_w_up, v_ffn1_w_down, v_ln1_g, v_ln1_b, v_w_in, v_conv_w, v_a_log, v_dt_bias, v_dn_norm_w, v_w_dn_branch, v_pool_w, v_pool_scale, v_w_pool_branch, v_w_mix_out, v_ln2_g, v_ln2_b, v_mem_ln_g, v_mem_ln_b, v_xa_wq, v_xa_wk, v_xa_wv, v_xa_wo, v_ln3_g, v_ln3_b, v_ffn2_w_gate, v_ffn2_w_up, v_ffn2_w_down, v_ln4_g, v_ln4_b):
    given = dict(x=x, mem=mem, ffn1_w_gate=ffn1_w_gate, ffn1_w_up=ffn1_w_up, ffn1_w_down=ffn1_w_down, ln1_g=ln1_g, ln1_b=ln1_b, w_in=w_in, conv_w=conv_w, a_log=a_log, dt_bias=dt_bias, dn_norm_w=dn_norm_w, w_dn_branch=w_dn_branch, pool_w=pool_w, pool_scale=pool_scale, w_pool_branch=w_pool_branch, w_mix_out=w_mix_out, ln2_g=ln2_g, ln2_b=ln2_b, mem_ln_g=mem_ln_g, mem_ln_b=mem_ln_b, xa_wq=xa_wq, xa_wk=xa_wk, xa_wv=xa_wv, xa_wo=xa_wo, ln3_g=ln3_g, ln3_b=ln3_b, ffn2_w_gate=ffn2_w_gate, ffn2_w_up=ffn2_w_up, ffn2_w_down=ffn2_w_down, ln4_g=ln4_g, ln4_b=ln4_b, loss_target=loss_target, m_ffn1_w_gate=m_ffn1_w_gate, m_ffn1_w_up=m_ffn1_w_up, m_ffn1_w_down=m_ffn1_w_down, m_ln1_g=m_ln1_g, m_ln1_b=m_ln1_b, m_w_in=m_w_in, m_conv_w=m_conv_w, m_a_log=m_a_log, m_dt_bias=m_dt_bias, m_dn_norm_w=m_dn_norm_w, m_w_dn_branch=m_w_dn_branch, m_pool_w=m_pool_w, m_pool_scale=m_pool_scale, m_w_pool_branch=m_w_pool_branch, m_w_mix_out=m_w_mix_out, m_ln2_g=m_ln2_g, m_ln2_b=m_ln2_b, m_mem_ln_g=m_mem_ln_g, m_mem_ln_b=m_mem_ln_b, m_xa_wq=m_xa_wq, m_xa_wk=m_xa_wk, m_xa_wv=m_xa_wv, m_xa_wo=m_xa_wo, m_ln3_g=m_ln3_g, m_ln3_b=m_ln3_b, m_ffn2_w_gate=m_ffn2_w_gate, m_ffn2_w_up=m_ffn2_w_up, m_ffn2_w_down=m_ffn2_w_down, m_ln4_g=m_ln4_g, m_ln4_b=m_ln4_b, v_ffn1_w_gate=v_ffn1_w_gate, v_ffn1_w_up=v_ffn1_w_up, v_ffn1_w_down=v_ffn1_w_down, v_ln1_g=v_ln1_g, v_ln1_b=v_ln1_b, v_w_in=v_w_in, v_conv_w=v_conv_w, v_a_log=v_a_log, v_dt_bias=v_dt_bias, v_dn_norm_w=v_dn_norm_w, v_w_dn_branch=v_w_dn_branch, v_pool_w=v_pool_w, v_pool_scale=v_pool_scale, v_w_pool_branch=v_w_pool_branch, v_w_mix_out=v_w_mix_out, v_ln2_g=v_ln2_g, v_ln2_b=v_ln2_b, v_mem_ln_g=v_mem_ln_g, v_mem_ln_b=v_mem_ln_b, v_xa_wq=v_xa_wq, v_xa_wk=v_xa_wk, v_xa_wv=v_xa_wv, v_xa_wo=v_xa_wo, v_ln3_g=v_ln3_g, v_ln3_b=v_ln3_b, v_ffn2_w_gate=v_ffn2_w_gate, v_ffn2_w_up=v_ffn2_w_up, v_ffn2_w_down=v_ffn2_w_down, v_ln4_g=v_ln4_g, v_ln4_b=v_ln4_b)
    weights = {n: given[n] for n in TWIN_WEIGHTS}
    shared = {n: given[n] for n in SHARED_INPUTS}
    per_example = {n: given[n] for n in ['x', 'mem']}
    grad_fn = _jax.value_and_grad(_loss, argnums=(0, 1))

    def one_microbatch(ex, loss_target):
        ex = dict(ex)
        diff = ex.pop(TWIN_DIFF_INPUT)
        return grad_fn(weights, diff, {**shared, **ex}, loss_target)

    if N_MICROBATCH == 1:
        loss, (grad_w, grad_x) = one_microbatch(per_example, given["loss_target"])
    else:
        def body(carry, xs):
            loss_sum, grad_sum = carry
            l_k, (gw_k, gx_k) = one_microbatch(xs[0], xs[1])
            with _jax.named_scope("update"):
                return (loss_sum + l_k, _jax.tree.map(_jnp.add, grad_sum, gw_k)), gx_k

        init = (_jnp.zeros((), _jnp.float32), _jax.tree.map(_jnp.zeros_like, weights))
        (loss, grad_w), grad_x = _jax.lax.scan(body, init, (per_example, given["loss_target"]))
    with _jax.named_scope("update"):
        delta_w, new_m, new_v = {}, {}, {}
        for n in TWIN_WEIGHTS:
            delta_w[n], new_m[n], new_v[n] = _adamw(weights[n], grad_w[n], given["m_" + n], given["v_" + n])
    return (loss, grad_x, *[grad_w[n] for n in TWIN_WEIGHTS], *[delta_w[n] for n in TWIN_WEIGHTS],
            *[new_m[n] for n in TWIN_WEIGHTS], *[new_v[n] for n in TWIN_WEIGHTS])
```

```python
import functools

import jax
import jax.numpy as jnp
from jax import lax
from jax.experimental import pallas as pl
from jax.experimental.pallas import tpu as pltpu

F32 = jnp.float32
BF16 = jnp.bfloat16
MMD = BF16
WIRE = BF16
HI = lax.Precision.HIGHEST
VMEM_LIMIT_BYTES = 48 * 1024 * 1024

D_MODEL = 1024
D_FF = 2816
CHUNK = 64
N_MEM = 256
DN_HEADS = 4
HD = 128
DN_WIDTH = 512
POOL_WINDOWS = (2, 4, 8, 16)
POOL_WIDTH = 512
XA_HEADS = 4
XA_HD = 256
LN_EPS = 1e-5
RMS_EPS = 1e-6
L2_EPS = 1e-6
ALPHA = 2.0 ** 0.25
HALO = 16

ADAM_LR = 0.001
ADAM_B1 = 0.9
ADAM_B2 = 0.999
ADAM_EPS = 1e-08
ADAM_WD = 0.01
ADAM_STEP = 10

N_DEV = 8
LANES = 1024


def _dot(a, b, ca, cb, hi):
    dn = (((ca,), (cb,)), ((), ()))
    if hi:
        return lax.dot_general(a.astype(F32), b.astype(F32), dn, precision=HI, preferred_element_type=F32)
    return lax.dot_general(a.astype(MMD), b.astype(MMD), dn, preferred_element_type=F32)


def dnn(a, b, hi=False):
    return _dot(a, b, 1, 0, hi)


def dnt(a, b, hi=False):
    return _dot(a, b, 1, 1, hi)


def dtn(a, b, hi=False):
    return _dot(a, b, 0, 0, hi)


def _sigmoid(x):
    return jax.nn.sigmoid(x)


def _silu(x):
    return x * _sigmoid(x)


def _dsilu(x):
    s = _sigmoid(x)
    return s * (1.0 + x * (1.0 - s))


def _softplus(x):
    return jnp.maximum(x, 0.0) + jnp.log1p(jnp.exp(-jnp.abs(x)))


def _iota(shape, dim):
    return lax.broadcasted_iota(jnp.int32, shape, dim)


def _rsum(x):
    return jnp.sum(x, axis=1, keepdims=True)


def _csum(x):
    return jnp.sum(x, axis=0, keepdims=True)


def _pick(n, cands):
    for c in cands:
        if n % c == 0:
            return c
    return n


def _params(sem):
    return pltpu.CompilerParams(dimension_semantics=sem, vmem_limit_bytes=VMEM_LIMIT_BYTES)


def mm(name, a, b, *, ta=False, tb=False, out_dtype=F32, add=None, scale=None):
    if ta:
        kc, m = a.shape
    else:
        m, kc = a.shape
    if tb:
        n, kb = b.shape
    else:
        kb, n = b.shape
    assert kc == kb, (name, a.shape, b.shape)
    tm = m if m <= 512 else _pick(m, (512, 256, 128))
    tn = n if n <= 1024 else _pick(n, (1024, 1408, 768, 512))
    tk = kc if kc <= 1024 else _pick(kc, (1024, 1408, 768, 512))
    nk = kc // tk
    grid = (m // tm, n // tn, nk)
    a_spec = pl.BlockSpec((tk, tm), lambda i, j, k: (k, i)) if ta else pl.BlockSpec((tm, tk), lambda i, j, k: (i, k))
    b_spec = pl.BlockSpec((tn, tk), lambda i, j, k: (j, k)) if tb else pl.BlockSpec((tk, tn), lambda i, j, k: (k, j))
    o_spec = pl.BlockSpec((tm, tn), lambda i, j, k: (i, j))
    ca, cb = (0 if ta else 1), (1 if tb else 0)
    has_add = add is not None

    def body(*refs):
        if has_add:
            a_ref, b_ref, add_ref, o_ref, acc_ref = refs
        else:
            a_ref, b_ref, o_ref, acc_ref = refs
        k = pl.program_id(2)

        @pl.when(k == 0)
        def _():
            acc_ref[...] = jnp.zeros_like(acc_ref)

        acc_ref[...] += _dot(a_ref[...], b_ref[...], ca, cb, False)

        @pl.when(k == nk - 1)
        def _():
            r = acc_ref[...]
            if scale is not None:
                r = r * scale
            if has_add:
                r = r + add_ref[...]
            o_ref[...] = r.astype(o_ref.dtype)

    ins = [a, b] + ([add] if has_add else [])
    specs = [a_spec, b_spec] + ([o_spec] if has_add else [])
    return pl.pallas_call(
        body, name=name, grid=grid, in_specs=specs, out_specs=o_spec,
        out_shape=jax.ShapeDtypeStruct((m, n), out_dtype),
        scratch_shapes=[pltpu.VMEM((tm, tn), F32)],
        compiler_params=_params(("parallel", "parallel", "arbitrary")),
    )(*ins)


class _Ctx:
    def __init__(self, i, nblk, tl):
        self.i, self.nblk, self.tl = i, nblk, tl


def _norm_item(it):
    if isinstance(it, tuple):
        a, w, j = it[:3]
        rows = it[3] if len(it) > 3 else None
        return a, w, j, rows
    return it, it.shape[-1], 0, None


def rowwise(name, fn, length, tl, *, rows=(), consts=(), prevs=(), nexts=(), out_rows=(), out_accs=()):
    nblk = length // tl
    hb = tl // HALO
    nhalo = length // HALO
    arrays, specs = [], []
    for it in rows:
        a, w, j, r = _norm_item(it)
        if a.ndim == 3:
            specs.append(pl.BlockSpec((a.shape[0], tl, w), lambda i, j=j: (0, i, j)))
        else:
            specs.append(pl.BlockSpec((r or tl, w), lambda i, j=j: (i, j)))
        arrays.append(a)
    for a in consts:
        specs.append(pl.BlockSpec(a.shape, lambda i, nd=a.ndim: (0,) * nd))
        arrays.append(a)
    for it in prevs:
        a, w, j, _ = _norm_item(it)
        specs.append(pl.BlockSpec((HALO, w), lambda i, j=j: (jnp.maximum(i * hb - 1, 0), j)))
        arrays.append(a)
    for it in nexts:
        a, w, j, _ = _norm_item(it)
        specs.append(pl.BlockSpec((HALO, w), lambda i, j=j: (jnp.minimum((i + 1) * hb, nhalo - 1), j)))
        arrays.append(a)
    out_shape, out_specs = [], []
    for spec in out_rows:
        if len(spec) == 3:
            h, w, dt = spec
            out_shape.append(jax.ShapeDtypeStruct((h, length, w), dt))
            out_specs.append(pl.BlockSpec((h, tl, w), lambda i: (0, i, 0)))
        else:
            w, dt = spec
            out_shape.append(jax.ShapeDtypeStruct((length, w), dt))
            out_specs.append(pl.BlockSpec((tl, w), lambda i: (i, 0)))
    for shape, dt in out_accs:
        out_shape.append(jax.ShapeDtypeStruct(shape, dt))
        out_specs.append(pl.BlockSpec(shape, lambda i, nd=len(shape): (0,) * nd))
    n_r, n_c, n_p, n_n = len(rows), len(consts), len(prevs), len(nexts)
    n_in = n_r + n_c + n_p + n_n
    n_or = len(out_rows)

    def body(*refs):
        i = pl.program_id(0)
        vals = [r[...] for r in refs[:n_in]]
        outs = refs[n_in:]
        ctx = _Ctx(i, nblk, tl)
        ro, ao = fn(ctx, vals[:n_r], vals[n_r:n_r + n_c], vals[n_r + n_c:n_r + n_c + n_p], vals[n_r + n_c + n_p:])
        for r, v in zip(outs[:n_or], ro, strict=True):
            r[...] = v.astype(r.dtype)
        for r, v in zip(outs[n_or:], ao, strict=True):
            @pl.when(i == 0)
            def _(r=r, v=v):
                r[...] = v.astype(r.dtype)

            @pl.when(i > 0)
            def _(r=r, v=v):
                r[...] += v.astype(r.dtype)

    res = pl.pallas_call(
        body, name=name, grid=(nblk,), in_specs=specs, out_specs=out_specs, out_shape=out_shape,
        compiler_params=_params(("arbitrary",) if out_accs else ("parallel",)),
    )(*arrays)
    return res


def _heads(x, n, w):
    return [x[:, h * w:(h + 1) * w] for h in range(n)]


def _cat(xs):
    return jnp.concatenate(xs, axis=1)


def _row_index(ctx, nrows, offset=0):
    return ctx.i * ctx.tl + offset + _iota((nrows, 1), 0)


def _ln_stats(r):
    mu = jnp.mean(r, axis=1, keepdims=True)
    d = r - mu
    var = jnp.mean(d * d, axis=1, keepdims=True)
    rstd = lax.rsqrt(var + LN_EPS)
    return d * rstd, rstd


def ln_fwd(name, terms, g, b, tl=256):
    coefs = [c for c, _ in terms]
    length = terms[0][1].shape[0]

    def fn(ctx, rows, consts, prevs, nexts):
        r = sum(c * t for c, t in zip(coefs, rows))
        xh, _ = _ln_stats(r)
        return [xh * consts[0] + consts[1], r], []

    return rowwise(name, fn, length, min(tl, length), rows=[t for _, t in terms], consts=[g, b],
                   out_rows=[(D_MODEL, F32), (D_MODEL, F32)])


def ln_bwd(name, r, terms, g, tl=256):
    coefs = [c for c, _ in terms]
    length = r.shape[0]

    def fn(ctx, rows, consts, prevs, nexts):
        xh, rstd = _ln_stats(rows[0])
        dy = sum(c * t for c, t in zip(coefs, rows[1:]))
        dxh = dy * consts[0]
        dr = rstd * (dxh - jnp.mean(dxh, axis=1, keepdims=True) - xh * jnp.mean(dxh * xh, axis=1, keepdims=True))
        return [dr], [_csum(dy * xh), _csum(dy)]

    return rowwise(name, fn, length, min(tl, length), rows=[r] + [t for _, t in terms], consts=[g],
                   out_rows=[(D_MODEL, F32)], out_accs=[((1, D_MODEL), F32), ((1, D_MODEL), F32)])


def ln_loss(name, terms, g, b, target, tl=256):
    coefs = [c for c, _ in terms]
    length = target.shape[0]
    nt = len(terms)

    def fn(ctx, rows, consts, prevs, nexts):
        r = sum(c * t for c, t in zip(coefs, rows[:nt]))
        xh, _ = _ln_stats(r)
        err = xh * consts[0] + consts[1] - rows[nt]
        tot = _csum(_rsum(err * err)) * (0.5 / D_MODEL)
        return [err * (1.0 / D_MODEL), r], [jnp.broadcast_to(tot, (1, 128))]

    return rowwise(name, fn, length, min(tl, length), rows=[t for _, t in terms] + [target], consts=[g, b],
                   out_rows=[(D_MODEL, F32), (D_MODEL, F32)], out_accs=[((1, 128), F32)])


def axpy(name, terms, tl=256):
    coefs = [c for c, _ in terms]
    length, width = terms[0][1].shape

    def fn(ctx, rows, consts, prevs, nexts):
        return [sum(c * t for c, t in zip(coefs, rows))], []

    return rowwise(name, fn, length, min(tl, length), rows=[t for _, t in terms], out_rows=[(width, F32)])[0]


def ffn_fwd(tag, x, wg, wu, wd):
    length = x.shape[0]
    hg = mm(tag + "_gate", x, wg)
    hu = mm(tag + "_up", x, wu)

    def fn(ctx, rows, consts, prevs, nexts):
        return [_silu(rows[0]) * rows[1]], []

    act = rowwise(tag + "_act", fn, length, min(256, length), rows=[hg, hu], out_rows=[(D_FF, BF16)])[0]
    f = mm(tag + "_down", act, wd)
    return f, (hg, hu, act)


def ffn_bwd(tag, x, res, dr, wg, wu, wd):
    hg, hu, act = res
    length = x.shape[0]
    dact = mm(tag + "_dact", dr, wd, tb=True, scale=0.5)
    dwd = mm(tag + "_dwd", act, dr, ta=True, scale=0.5)

    def fn(ctx, rows, consts, prevs, nexts):
        g, u, da = rows
        return [da * u * _dsilu(g), da * _silu(g)], []

    dhg, dhu = rowwise(tag + "_dactb", fn, length, min(256, length), rows=[hg, hu, dact],
                       out_rows=[(D_FF, BF16), (D_FF, BF16)])
    dwg = mm(tag + "_dwg", x, dhg, ta=True)
    dwu = mm(tag + "_dwu", x, dhu, ta=True)
    dx = mm(tag + "_dxg", dhg, wg, tb=True)
    dx = mm(tag + "_dxu", dhu, wu, tb=True, add=dx)
    return dx, dwg, dwu, dwd


def _conv_taps(ext, taps, n):
    out = taps[3] * ext
    for j in range(3):
        out = out + taps[j] * pltpu.roll(ext, 3 - j, 0)
    return out


def _l2n(x):
    r = lax.rsqrt(_rsum(x * x) + L2_EPS)
    return x * r, r


def conv_fwd(name, pre, taps, tl=256):
    length = pre.shape[0]
    tl = min(tl, length)

    def fn(ctx, rows, consts, prevs, nexts):
        prev = jnp.where(ctx.i > 0, prevs[0], 0.0)
        ext = jnp.concatenate([prev, rows[0]], axis=0)
        s = _silu(_conv_taps(ext, consts, tl + HALO)[HALO:])
        q = _cat([_l2n(x)[0] * (HD ** -0.5) for x in _heads(s[:, :DN_WIDTH], DN_HEADS, HD)])
        k = _cat([_l2n(x)[0] for x in _heads(s[:, DN_WIDTH:2 * DN_WIDTH], DN_HEADS, HD)])
        return [q, k, s[:, 2 * DN_WIDTH:]], []

    return rowwise(name, fn, length, tl, rows=[pre], consts=list(taps), prevs=[pre],
                   out_rows=[(DN_WIDTH, F32)] * 3)


def conv_bwd(name, pre, dq, dk, dv, taps, tl=256):
    length = pre.shape[0]
    tl = min(tl, length)
    n = tl + 2 * HALO

    def fn(ctx, rows, consts, prevs, nexts):
        last = ctx.i == ctx.nblk - 1
        prev = jnp.where(ctx.i > 0, prevs[0], 0.0)
        ext = jnp.concatenate([prev, rows[0], nexts[0]], axis=0)
        c = _conv_taps(ext, consts, n)
        s = _silu(c)
        zero = jnp.zeros((HALO, DN_WIDTH), F32)
        dqe, dke, dve = [jnp.concatenate([zero, rows[1 + t], jnp.where(last, 0.0, nexts[1 + t])], axis=0)
                         for t in range(3)]

        def l2_bwd(x, dy):
            y, r = _l2n(x)
            return r * (dy - y * _rsum(dy * y))

        dsq = _cat([l2_bwd(x, d * (HD ** -0.5)) for x, d in zip(_heads(s[:, :DN_WIDTH], DN_HEADS, HD),
                                                                 _heads(dqe, DN_HEADS, HD))])
        dsk = _cat([l2_bwd(x, d) for x, d in zip(_heads(s[:, DN_WIDTH:2 * DN_WIDTH], DN_HEADS, HD),
                                                  _heads(dke, DN_HEADS, HD))])
        dc = _cat([dsq, dsk, dve]) * _dsilu(c)
        dpre = consts[3] * dc
        for j in range(3):
            dpre = dpre + consts[j] * pltpu.roll(dc, n - (3 - j), 0)
        dc_cur = dc[HALO:HALO + tl]
        dws = [_csum(dc_cur * pltpu.roll(ext, 3 - j, 0)[HALO:HALO + tl]) for j in range(3)]
        dws.append(_csum(dc_cur * ext[HALO:HALO + tl]))
        return [dpre[HALO:HALO + tl]], dws

    return rowwise(name, fn, length, tl, rows=[pre, dq, dk, dv], consts=list(taps), prevs=[pre],
                   nexts=[pre, dq, dk, dv], out_rows=[(3 * DN_WIDTH, BF16)],
                   out_accs=[((1, 3 * DN_WIDTH), F32)] * 4)


def _gate_consts():
    lane = jnp.arange(128)[:, None]
    col = jnp.arange(2 * DN_WIDTH)[None, :]
    sel = ((lane < 2 * DN_HEADS) & (col // HD == lane)).astype(F32)
    pick = ((col.T == lane.T * HD) & (lane.T < 2 * DN_HEADS)).astype(F32)
    return sel, pick


def _gate_math(ab, alog, dtb):
    z = ab + dtb
    g = -jnp.exp(alog) * _softplus(z)
    beta = _sigmoid(ab)
    return z, g, beta


def gates_fwd(name, ab, alog, dtb, sel, tl=256):
    length = ab.shape[0]

    def fn(ctx, rows, consts, prevs, nexts):
        _, g, beta = _gate_math(rows[0], consts[0], consts[1])
        lane = _iota(g.shape, 1)
        small = jnp.where(lane < DN_HEADS, g, jnp.where(lane < 2 * DN_HEADS, beta, 0.0))
        big = dnn(small, consts[2], hi=True)
        return [big[:, :DN_WIDTH], big[:, DN_WIDTH:]], []

    return rowwise(name, fn, length, min(tl, length), rows=[ab], consts=[alog, dtb, sel],
                   out_rows=[(DN_WIDTH, F32)] * 2)


def gates_bwd(name, ab, dgb, dbb, alog, dtb, pick, tl=256):
    length = ab.shape[0]

    def fn(ctx, rows, consts, prevs, nexts):
        z, g, beta = _gate_math(rows[0], consts[0], consts[1])
        dsmall = dnn(_cat([rows[1], rows[2]]), consts[2], hi=True)
        lane = _iota(g.shape, 1)
        is_a = lane < DN_HEADS
        da = jnp.where(is_a, dsmall * (-jnp.exp(consts[0])) * _sigmoid(z), 0.0)
        db = jnp.where((lane >= DN_HEADS) & (lane < 2 * DN_HEADS), dsmall * beta * (1.0 - beta), 0.0)
        return [da + db], [_csum(jnp.where(is_a, dsmall * g, 0.0)), _csum(da)]

    return rowwise(name, fn, length, min(tl, length), rows=[ab, dgb, dbb], consts=[alog, dtb, pick],
                   out_rows=[(128, BF16)], out_accs=[((1, 128), F32)] * 2)


def _delta_consts():
    i = jnp.arange(CHUNK)
    ltri = (i[:, None] >= i[None, :]).astype(F32)
    utri = ltri.T
    p0 = jnp.zeros((HD, CHUNK), F32).at[0, :].set(1.0)
    e_last = jnp.zeros((CHUNK, CHUNK), F32).at[:, CHUNK - 1].set(1.0)
    return ltri, utri, p0, e_last


def _tri_inv(a, eye, bd):
    dg = jnp.where(bd, a, 0.0)
    lo = a - dg
    n1 = -dg
    n2 = dnn(n1, n1, True)
    n4 = dnn(n2, n2, True)
    n8 = dnn(n4, n4, True)
    td = dnn(eye + n1, eye + n2, True)
    td = dnn(td, eye + n4, True)
    td = dnn(td, eye + n8, True)
    m = dnn(td, lo, True)
    m2 = dnn(m, m, True)
    x = dnn(eye - m, eye + m2, True)
    return dnn(x, td, True)


def _chunk_common(q, k, v, gb, bb, ltri, utri, p0, e_last):
    gcb = dnn(ltri, gb, True)
    egb = jnp.exp(gcb)
    gc64 = dnn(gcb, p0, True)
    g64 = dnn(gb, p0, True)
    gcr = dtn(g64, utri, True)
    ii, jj = _iota((CHUNK, CHUNK), 0), _iota((CHUNK, CHUNK), 1)
    incl, strict = ii >= jj, ii > jj
    decay = jnp.exp(jnp.where(incl, gc64 - gcr, -jnp.inf))
    kb = k * bb
    vb = v * bb
    kbe = kb * egb
    pm = dnt(kb, k, True)
    qm = dnt(q, k, True)
    glb = dnn(e_last, gcb, True)
    ekb = jnp.exp(glb - gcb)
    return dict(gcb=gcb, egb=egb, decay=decay, kb=kb, vb=vb, kbe=kbe, pm=pm, qm=qm, ekb=ekb,
                incl=incl, strict=strict, ii=ii, jj=jj)


def delta_prep_fwd(name, q, k, v, gb, bb, dconsts):
    length = q.shape[0]

    def fn(ctx, rows, consts, prevs, nexts):
        ltri, utri, p0, e_last = consts
        us, ws, qds, kds, gcs, attns, ts = [], [], [], [], [], [], []
        for h in range(DN_HEADS):
            qh, kh, vh, gh, bh = [_heads(r, DN_HEADS, HD)[h] for r in rows]
            c = _chunk_common(qh, kh, vh, gh, bh, ltri, utri, p0, e_last)
            eye = (c["ii"] == c["jj"]).astype(F32)
            a = jnp.where(c["strict"], c["pm"] * c["decay"], 0.0)
            t = _tri_inv(a, eye, (c["ii"] >> 4) == (c["jj"] >> 4))
            us.append(dnn(t, c["vb"], True))
            ws.append(dnn(t, c["kbe"], True))
            qds.append(qh * c["egb"])
            kds.append(kh * c["ekb"])
            gcs.append(c["gcb"])
            attns.append(c["qm"] * c["decay"])
            ts.append(t)
        return [_cat(us), _cat(ws), _cat(qds), _cat(kds), _cat(gcs), jnp.stack(attns), jnp.stack(ts)], []

    return rowwise(name, fn, length, CHUNK, rows=[q, k, v, gb, bb], consts=list(dconsts),
                   out_rows=[(DN_WIDTH, F32)] * 5 + [(DN_HEADS, CHUNK, F32)] * 2)


def delta_prep_bwd(name, q, k, v, gb, bb, t3, du, dw, dqd, dkd, dattn3, dgl, dconsts):
    length = q.shape[0]
    ones_hd = jnp.ones((HD, HD), F32)
    ones_c_hd = jnp.ones((CHUNK, HD), F32)
    ones_c = jnp.ones((CHUNK, CHUNK), F32)

    def fn(ctx, rows, consts, prevs, nexts):
        ltri, utri, p0, e_last, o_hd, o_c_hd, o_c = consts
        t3v, da3v, dglv = rows[9], rows[10], rows[11]
        dqs, dks, dvs, dgs, dbs = [], [], [], [], []
        for h in range(DN_HEADS):
            qh, kh, vh, gh, bh, duh, dwh, dqdh, dkdh = [_heads(r, DN_HEADS, HD)[h] for r in rows[:9]]
            c = _chunk_common(qh, kh, vh, gh, bh, ltri, utri, p0, e_last)
            t = t3v[h]
            dattn = jnp.where(c["incl"], da3v[h], 0.0)
            dvb = dtn(t, duh, True)
            dkbe = dtn(t, dwh, True)
            dt = dnt(duh, c["vb"], True) + dnt(dwh, c["kbe"], True)
            da = jnp.where(c["strict"], -dtn(t, dnt(dt, t, True), True), 0.0)
            dpm = da * c["decay"]
            ddecay = da * c["pm"] + dattn * c["qm"]
            dqm = dattn * c["decay"]
            dkb = dnn(dpm, kh, True) + dkbe * c["egb"]
            dk = dtn(dpm, c["kb"], True) + dtn(dqm, qh, True) + dkdh * c["ekb"] + dkb * bh
            dq = dnn(dqm, kh, True) + dqdh * c["egb"]
            dbeta = dnn(dkb * kh + dvb * vh, o_hd, True)
            c1 = dnn(dkbe * c["kb"] + dqdh * qh, o_hd, True) * c["egb"]
            c2 = dnn(dkdh * kh, o_hd, True) * c["ekb"]
            e = ddecay * c["decay"]
            dgc = c1 - c2 + dnn(e, o_c_hd, True) - dtn(e, o_c_hd, True)
            dgl_tot = jnp.max(dglv[:, h * HD:(h + 1) * HD], axis=0, keepdims=True) + dnn(o_c, c2, True)
            dgc = dgc + jnp.where(_iota((CHUNK, HD), 0) == CHUNK - 1, dgl_tot, 0.0)
            dqs.append(dq)
            dks.append(dk)
            dvs.append(dvb * bh)
            dgs.append(dnn(utri, dgc, True))
            dbs.append(dbeta)
        return [_cat(dqs), _cat(dks), _cat(dvs), _cat(dgs), _cat(dbs)], []

    return rowwise(name, fn, length, CHUNK,
                   rows=[q, k, v, gb, bb, du, dw, dqd, dkd, t3, dattn3, (dgl, DN_WIDTH, 0, 8)],
                   consts=list(dconsts) + [ones_hd, ones_c_hd, ones_c], out_rows=[(DN_WIDTH, F32)] * 5)


def delta_scan_fwd(name, qd, kd, u, w, attn3, gcb):
    length = qd.shape[0]
    n = length // CHUNK
    row = pl.BlockSpec((CHUNK, DN_WIDTH), lambda c: (c, 0))
    sq = pl.BlockSpec((DN_HEADS, CHUNK, CHUNK), lambda c: (0, c, 0))

    def body(qd_ref, kd_ref, u_ref, w_ref, attn_ref, gc_ref, o_ref, vn_ref, st_ref, s_ref):
        c = pl.program_id(0)

        @pl.when(c == 0)
        def _():
            s_ref[...] = jnp.zeros_like(s_ref)

        for h in range(DN_HEADS):
            sl = pl.ds(h * HD, HD)
            s = s_ref[h]
            st_ref[0, h] = s
            vn = u_ref[:, sl] - dnn(w_ref[:, sl], s)
            o_ref[:, sl] = dnn(qd_ref[:, sl], s) + dnn(attn_ref[h], vn)
            vn_ref[:, sl] = vn
            egl = jnp.exp(gc_ref[pl.ds(CHUNK - 1, 1), sl])
            s_ref[h] = s * egl + dtn(kd_ref[:, sl], vn)

    return pl.pallas_call(
        body, name=name, grid=(n,), in_specs=[row, row, row, row, sq, row],
        out_specs=[row, row, pl.BlockSpec((1, DN_HEADS, HD, HD), lambda c: (c, 0, 0, 0))],
        out_shape=[jax.ShapeDtypeStruct((length, DN_WIDTH), F32), jax.ShapeDtypeStruct((length, DN_WIDTH), F32),
                   jax.ShapeDtypeStruct((n, DN_HEADS, HD, HD), F32)],
        scratch_shapes=[pltpu.VMEM((DN_HEADS, HD, HD), F32)],
        compiler_params=_params(("arbitrary",)),
    )(qd, kd, u, w, attn3, gcb)


def delta_scan_bwd(name, do, qd, kd, w, attn3, vn, st, gcb):
    length = qd.shape[0]
    n = length // CHUNK
    row = pl.BlockSpec((CHUNK, DN_WIDTH), lambda c: (n - 1 - c, 0))
    sq = pl.BlockSpec((DN_HEADS, CHUNK, CHUNK), lambda c: (0, n - 1 - c, 0))
    stb = pl.BlockSpec((1, DN_HEADS, HD, HD), lambda c: (n - 1 - c, 0, 0, 0))
    glb = pl.BlockSpec((8, DN_WIDTH), lambda c: (n - 1 - c, 0))

    def body(do_ref, qd_ref, kd_ref, w_ref, attn_ref, vn_ref, st_ref, gc_ref,
             dqd_ref, dkd_ref, du_ref, dw_ref, dattn_ref, dgl_ref, ds_ref):
        c = pl.program_id(0)

        @pl.when(c == 0)
        def _():
            ds_ref[...] = jnp.zeros_like(ds_ref)

        for h in range(DN_HEADS):
            sl = pl.ds(h * HD, HD)
            s = st_ref[0, h]
            dsn = ds_ref[h]
            d_o = do_ref[:, sl]
            vnh = vn_ref[:, sl]
            egl = jnp.exp(gc_ref[pl.ds(CHUNK - 1, 1), sl])
            dattn_ref[h] = dnt(d_o, vnh)
            dvn = dtn(attn_ref[h], d_o) + dnn(kd_ref[:, sl], dsn)
            dqd_ref[:, sl] = dnt(d_o, s)
            dkd_ref[:, sl] = dnt(vnh, dsn)
            du_ref[:, sl] = dvn
            dw_ref[:, sl] = -dnt(dvn, s)
            dgl_ref[:, sl] = jnp.broadcast_to(_csum(_rsum(dsn * s)) * egl, (8, HD))
            ds_ref[h] = dsn * egl + dtn(qd_ref[:, sl], d_o) - dtn(w_ref[:, sl], dvn)

    return pl.pallas_call(
        body, name=name, grid=(n,), in_specs=[row, row, row, row, sq, row, stb, row],
        out_specs=[row, row, row, row, sq, glb],
        out_shape=[jax.ShapeDtypeStruct((length, DN_WIDTH), F32)] * 4
        + [jax.ShapeDtypeStruct((DN_HEADS, length, CHUNK), F32), jax.ShapeDtypeStruct((n * 8, DN_WIDTH), F32)],
        scratch_shapes=[pltpu.VMEM((DN_HEADS, HD, HD), F32)],
        compiler_params=_params(("arbitrary",)),
    )(do, qd, kd, w, attn3, vn, st, gcb)


def onorm_fwd(name, o, z, nw, tl=256):
    length = o.shape[0]

    def fn(ctx, rows, consts, prevs, nexts):
        outs = []
        for oh, zh in zip(_heads(rows[0], DN_HEADS, HD), _heads(rows[1], DN_HEADS, HD)):
            r = lax.rsqrt(jnp.mean(oh * oh, axis=1, keepdims=True) + RMS_EPS)
            outs.append(oh * r * consts[0] * _silu(zh))
        return [_cat(outs)], []

    return rowwise(name, fn, length, min(tl, length), rows=[o, z], consts=[nw], out_rows=[(DN_WIDTH, BF16)])[0]


def onorm_bwd(name, o, z, d_on, nw, tl=256):
    length = o.shape[0]

    def fn(ctx, rows, consts, prevs, nexts):
        dos, dzs = [], []
        dnw = jnp.zeros((1, HD), F32)
        for oh, zh, dh in zip(*[_heads(r, DN_HEADS, HD) for r in rows]):
            r = lax.rsqrt(jnp.mean(oh * oh, axis=1, keepdims=True) + RMS_EPS)
            y = oh * r
            sz = _silu(zh)
            t = dh * sz * consts[0]
            dos.append(r * (t - y * jnp.mean(t * y, axis=1, keepdims=True)))
            dzs.append(dh * y * consts[0] * _dsilu(zh))
            dnw = dnw + _csum(dh * y * sz)
        return [_cat(dos), _cat(dzs)], [dnw]

    return rowwise(name, fn, length, min(tl, length), rows=[o, z, d_on], consts=[nw],
                   out_rows=[(DN_WIDTH, F32), (DN_WIDTH, BF16)], out_accs=[((1, HD), F32)])


def merge_fwd(name, gates, ydn, ypool, tl=256):
    length = ydn.shape[0]

    def fn(ctx, rows, consts, prevs, nexts):
        gt = rows[0]
        return [_sigmoid(gt[:, :D_MODEL]) * rows[1] + _sigmoid(gt[:, D_MODEL:]) * rows[2]], []

    return rowwise(name, fn, length, min(tl, length), rows=[gates, ydn, ypool], out_rows=[(D_MODEL, BF16)])[0]


def merge_bwd(name, gates, ydn, ypool, dm, tl=256):
    length = ydn.shape[0]

    def fn(ctx, rows, consts, prevs, nexts):
        gt, yd, yp, d = rows
        sd, sp = _sigmoid(gt[:, :D_MODEL]), _sigmoid(gt[:, D_MODEL:])
        dgates = _cat([d * yd * sd * (1.0 - sd), d * yp * sp * (1.0 - sp)])
        return [d * sd, d * sp, dgates], []

    return rowwise(name, fn, length, min(tl, length), rows=[gates, ydn, ypool, dm],
                   out_rows=[(D_MODEL, BF16), (D_MODEL, BF16), (2 * D_MODEL, BF16)])


def _trailing_sums(ext, upto):
    s, sh = ext, 1
    while sh < upto:
        s = s + pltpu.roll(s, sh, 0)
        sh *= 2
    return s


def _leading_sums(ext, upto, n):
    s, sh = ext, 1
    while sh < upto:
        s = s + pltpu.roll(s, n - sh, 0)
        sh *= 2
    return s


def _pool_mixed(ctx, p, prev, tl):
    prevm = jnp.where(ctx.i > 0, prev, 0.0)
    t1 = (_row_index(ctx, tl) + 1).astype(F32)
    outs = []
    for gi, win in enumerate(POOL_WINDOWS):
        sl = slice(gi * HD, (gi + 1) * HD)
        ext = jnp.concatenate([prevm[:, sl], p[:, sl]], axis=0)
        mean = _trailing_sums(ext, win)[HALO:] / jnp.minimum(t1, float(win))
        outs.append(mean - p[:, sl])
    return outs


def pool_fwd(name, p, pool_w, scale, tl=256):
    length = p.shape[0]
    tl = min(tl, length)

    def fn(ctx, rows, consts, prevs, nexts):
        mixed = _pool_mixed(ctx, rows[0], prevs[0], tl)
        y = _cat([dnn(m, consts[0][gi]) for gi, m in enumerate(mixed)])
        return [y * consts[1]], []

    return rowwise(name, fn, length, tl, rows=[p], consts=[pool_w, scale], prevs=[p],
                   out_rows=[(POOL_WIDTH, BF16)])[0]


def pool_bwd(name, p, dpo, pool_w, scale, tl=256):
    length = p.shape[0]
    tl = min(tl, length)
    n = tl + HALO

    def fn(ctx, rows, consts, prevs, nexts):
        last = ctx.i == ctx.nblk - 1
        mixed = _pool_mixed(ctx, rows[0], prevs[0], tl)
        dext = jnp.concatenate([rows[1], jnp.where(last, 0.0, nexts[0])], axis=0)
        t1 = (_row_index(ctx, n) + 1).astype(F32)
        dps, dws, dscs = [], [], []
        for gi, win in enumerate(POOL_WINDOWS):
            sl = slice(gi * HD, (gi + 1) * HD)
            wg = consts[0][gi]
            dyraw = dext[:, sl] * consts[1][:, sl]
            dmix = dnt(dyraw, wg)
            dws.append(dtn(mixed[gi], dyraw[:tl]))
            dscs.append(_csum(rows[1][:, sl] * dnn(mixed[gi], wg)))
            lead = _leading_sums(dmix / jnp.minimum(t1, float(win)), win, n)
            dps.append(lead[:tl] - dmix[:tl])
        return [_cat(dps)], [jnp.stack(dws), _cat(dscs)]

    return rowwise(name, fn, length, tl, rows=[p, dpo], consts=[pool_w, scale], prevs=[p], nexts=[dpo],
                   out_rows=[(POOL_WIDTH, BF16)],
                   out_accs=[((len(POOL_WINDOWS), HD, HD), F32), ((1, POOL_WIDTH), F32)])


def _xa_probs(qh, kh):
    s = dnt(qh, kh) * (XA_HD ** -0.5)
    e = jnp.exp(s - jnp.max(s, axis=1, keepdims=True))
    return e / _rsum(e)


def xattn_fwd(name, qx, kx, vx, tl=256):
    length = qx.shape[0]

    def fn(ctx, rows, consts, prevs, nexts):
        outs = [dnn(_xa_probs(qh, kh), vh) for qh, kh, vh in
                zip(_heads(rows[0], XA_HEADS, XA_HD), _heads(consts[0], XA_HEADS, XA_HD),
                    _heads(consts[1], XA_HEADS, XA_HD))]
        return [_cat(outs)], []

    return rowwise(name, fn, length, min(tl, length), rows=[qx], consts=[kx, vx], out_rows=[(D_MODEL, BF16)])[0]


def xattn_bwd(name, qx, dox, kx, vx, tl=256):
    length = qx.shape[0]

    def fn(ctx, rows, consts, prevs, nexts):
        dqs, dks, dvs = [], [], []
        for qh, dh, kh, vh in zip(_heads(rows[0], XA_HEADS, XA_HD), _heads(rows[1], XA_HEADS, XA_HD),
                                  _heads(consts[0], XA_HEADS, XA_HD), _heads(consts[1], XA_HEADS, XA_HD)):
            pr = _xa_probs(qh, kh)
            dpr = dnt(dh, vh)
            ds = pr * (dpr - _rsum(dpr * pr)) * (XA_HD ** -0.5)
            dqs.append(dnn(ds, kh))
            dks.append(dtn(ds, qh))
            dvs.append(dtn(pr, dh))
        return [_cat(dqs)], [_cat(dks), _cat(dvs)]

    return rowwise(name, fn, length, min(tl, length), rows=[qx, dox], consts=[kx, vx],
                   out_rows=[(D_MODEL, BF16)], out_accs=[((N_MEM, D_MODEL), F32)] * 2)


def local_step(x, mem, target, w):
    sel, pick = _gate_consts()
    dconsts = _delta_consts()
    taps = [w["conv_w"][j:j + 1] for j in range(4)]
    alog = jnp.pad(w["a_log"], ((0, 0), (0, 128 - DN_HEADS)))
    dtb = jnp.pad(w["dt_bias"], ((0, 0), (0, 128 - DN_HEADS)))

    f1, res1 = ffn_fwd("ffn1", x, w["ffn1_w_gate"], w["ffn1_w_up"], w["ffn1_w_down"])
    x1, r1 = ln_fwd("ln1", [(ALPHA, x), (0.5, f1)], w["ln1_g"], w["ln1_b"])

    pre = mm("in_qkv", x1, w["in_qkv"])
    z = mm("in_z", x1, w["in_z"])
    gates = mm("in_gates", x1, w["in_gates"])
    p = mm("in_p", x1, w["in_p"])
    ab = mm("in_ab", x1, w["in_ab"])
    q, k, v = conv_fwd("conv", pre, taps)
    gb, bb = gates_fwd("gates", ab, alog, dtb, sel)
    u, wd_, qd, kd, gcb, attn3, t3 = delta_prep_fwd("dprep", q, k, v, gb, bb, dconsts)
    o, vn, st = delta_scan_fwd("dscan", qd, kd, u, wd_, attn3, gcb)
    on = onorm_fwd("onorm", o, z, w["dn_norm_w"])
    ydn = mm("dn_branch", on, w["w_dn_branch"])
    po = pool_fwd("pool", p, w["pool_w"], w["pool_scale"])
    ypool = mm("pool_branch", po, w["w_pool_branch"])
    merged = merge_fwd("merge", gates, ydn, ypool)
    mix = mm("mix_out", merged, w["w_mix_out"])
    x2, r2 = ln_fwd("ln2", [(ALPHA, x1), (1.0, mix)], w["ln2_g"], w["ln2_b"])

    m, _ = ln_fwd("ln_mem", [(1.0, mem)], w["mem_ln_g"], w["mem_ln_b"])
    qx = mm("xa_q", x2, w["xa_wq"])
    kx = mm("xa_k", m, w["xa_wk"])
    vx = mm("xa_v", m, w["xa_wv"])
    ox = xattn_fwd("xattn", qx, kx, vx)
    xa = mm("xa_o", ox, w["xa_wo"])
    x3, r3 = ln_fwd("ln3", [(ALPHA, x2), (1.0, xa)], w["ln3_g"], w["ln3_b"])

    f2, res2 = ffn_fwd("ffn2", x3, w["ffn2_w_gate"], w["ffn2_w_up"], w["ffn2_w_down"])
    dy4, r4, loss = ln_loss("ln4_loss", [(ALPHA, x3), (0.5, f2)], w["ln4_g"], w["ln4_b"], target)

    g = {}
    dr4, g["ln4_g"], g["ln4_b"] = ln_bwd("ln4_b", r4, [(1.0, dy4)], w["ln4_g"])
    dx3, g["ffn2_w_gate"], g["ffn2_w_up"], g["ffn2_w_down"] = ffn_bwd(
        "ffn2b", x3, res2, dr4, w["ffn2_w_gate"], w["ffn2_w_up"], w["ffn2_w_down"])
    dr3, g["ln3_g"], g["ln3_b"] = ln_bwd("ln3_b", r3, [(ALPHA, dr4), (1.0, dx3)], w["ln3_g"])

    dox = mm("xa_do", dr3, w["xa_wo"], tb=True)
    g["xa_wo"] = mm("xa_dwo", ox, dr3, ta=True)
    dqx, dkx, dvx = xattn_bwd("xattn_b", qx, dox, kx, vx)
    g["xa_wq"] = mm("xa_dwq", x2, dqx, ta=True)
    dx2 = mm("xa_dx", dqx, w["xa_wq"], tb=True)
    g["xa_wk"] = mm("xa_dwk", m, dkx, ta=True)
    g["xa_wv"] = mm("xa_dwv", m, dvx, ta=True)
    dmm = mm("xa_dmk", dkx, w["xa_wk"], tb=True)
    dmm = mm("xa_dmv", dvx, w["xa_wv"], tb=True, add=dmm)
    _, g["mem_ln_g"], g["mem_ln_b"] = ln_bwd("ln_mem_b", mem, [(1.0, dmm)], w["mem_ln_g"])
    dr2, g["ln2_g"], g["ln2_b"] = ln_bwd("ln2_b", r2, [(ALPHA, dr3), (1.0, dx2)], w["ln2_g"])

    dmerged = mm("mix_dm", dr2, w["w_mix_out"], tb=True)
    g["w_mix_out"] = mm("mix_dw", merged, dr2, ta=True)
    d_ydn, d_ypool, d_gates = merge_bwd("merge_b", gates, ydn, ypool, dmerged)
    g["w_dn_branch"] = mm("dn_dw", on, d_ydn, ta=True)
    d_on = mm("dn_dx", d_ydn, w["w_dn_branch"], tb=True)
    g["w_pool_branch"] = mm("pool_dw", po, d_ypool, ta=True)
    d_po = mm("pool_dx", d_ypool, w["w_pool_branch"], tb=True)
    dp, g["pool_w"], g["pool_scale"] = pool_bwd("pool_b", p, d_po, w["pool_w"], w["pool_scale"])
    d_o, dz, g["dn_norm_w"] = onorm_bwd("onorm_b", o, z, d_on, w["dn_norm_w"])
    dqd, dkd, du, dw_, dattn3, dgl = delta_scan_bwd("dscan_b", d_o, qd, kd, wd_, attn3, vn, st, gcb)
    dq, dk, dv, dgb, dbb = delta_prep_bwd("dprep_b", q, k, v, gb, bb, t3, du, dw_, dqd, dkd, dattn3, dgl, dconsts)
    dpre, dc0, dc1, dc2, dc3 = conv_bwd("conv_b", pre, dq, dk, dv, taps)
    g["conv_w"] = jnp.concatenate([dc0, dc1, dc2, dc3], axis=0)
    d_ab, dalog, ddtb = gates_bwd("gates_b", ab, dgb, dbb, alog, dtb, pick)
    g["a_log"] = dalog[:, :DN_HEADS]
    g["dt_bias"] = ddtb[:, :DN_HEADS]
    g["in_qkv"] = mm("in_dwqkv", x1, dpre, ta=True)
    g["in_z"] = mm("in_dwz", x1, dz, ta=True)
    g["in_gates"] = mm("in_dwgates", x1, d_gates, ta=True)
    g["in_p"] = mm("in_dwp", x1, dp, ta=True)
    g["in_ab"] = mm("in_dwab", x1, d_ab, ta=True)
    dx1 = mm("in_dxqkv", dpre, w["in_qkv"], tb=True)
    dx1 = mm("in_dxz", dz, w["in_z"], tb=True, add=dx1)
    dx1 = mm("in_dxgates", d_gates, w["in_gates"], tb=True, add=dx1)
    dx1 = mm("in_dxp", dp, w["in_p"], tb=True, add=dx1)
    dx1 = mm("in_dxab", d_ab, w["in_ab"], tb=True, add=dx1)
    dr1, g["ln1_g"], g["ln1_b"] = ln_bwd("ln1_b", r1, [(ALPHA, dr2), (1.0, dx1)], w["ln1_g"])
    dx0, g["ffn1_w_gate"], g["ffn1_w_up"], g["ffn1_w_down"] = ffn_bwd(
        "ffn1b", x, res1, dr1, w["ffn1_w_gate"], w["ffn1_w_up"], w["ffn1_w_down"])
    grad_x = axpy("grad_x", [(ALPHA, dr1), (1.0, dx0)])
    return loss, grad_x, g


WEIGHT_NAMES = ['ffn1_w_gate', 'ffn1_w_up', 'ffn1_w_down', 'ln1_g', 'ln1_b', 'w_in', 'conv_w', 'a_log', 'dt_bias',
                'dn_norm_w', 'w_dn_branch', 'pool_w', 'pool_scale', 'w_pool_branch', 'w_mix_out', 'ln2_g', 'ln2_b',
                'mem_ln_g', 'mem_ln_b', 'xa_wq', 'xa_wk', 'xa_wv', 'xa_wo', 'ln3_g', 'ln3_b', 'ffn2_w_gate',
                'ffn2_w_up', 'ffn2_w_down', 'ln4_g', 'ln4_b']
SHARDED = [
    ("ffn1_w_gate", 1, (1024, 352)), ("ffn1_w_up", 1, (1024, 352)), ("ffn1_w_down", 0, (352, 1024)),
    ("w_in", 1, (1024, 577)), ("conv_w", 1, (4, 192)), ("w_dn_branch", 1, (512, 128)),
    ("w_pool_branch", 1, (512, 128)), ("w_mix_out", 0, (128, 1024)), ("xa_wq", 0, (128, 1024)),
    ("xa_wk", 0, (128, 1024)), ("xa_wv", 0, (128, 1024)), ("xa_wo", 0, (128, 1024)),
    ("ffn2_w_gate", 1, (1024, 352)), ("ffn2_w_up", 1, (1024, 352)), ("ffn2_w_down", 0, (352, 1024)),
]
REPLICATED = [n for n in WEIGHT_NAMES if n not in {s[0] for s in SHARDED}]
ROW_ALIGN = 16
PACK_BLOCK = 512


def _round_up(n, m):
    return -(-n // m) * m


def _layout():
    off, table = 0, {}
    for name, axis, shape in SHARDED:
        numel = shape[0] * shape[1]
        rows = _round_up(max(-(-numel // LANES), 2), ROW_ALIGN)
        table[name] = (off, rows, axis, shape, numel)
        off += rows
    return table, _round_up(off, PACK_BLOCK)


LAYOUT, PACK_ROWS = _layout()


def _to_rows(flat, rows):
    return jnp.pad(flat, (0, rows * LANES - flat.shape[0])).reshape(rows, LANES)


def pack_weight_shards(shards):
    parts, used = [], 0
    for name, _, _ in SHARDED:
        off, rows, _, _, numel = LAYOUT[name]
        s = shards[name].reshape(-1)
        if name == "conv_w":
            hi = s.astype(BF16)
            lo = (s - hi.astype(F32)).astype(BF16)
            blk = jnp.concatenate([_to_rows(hi, 1), _to_rows(lo, 1), jnp.zeros((rows - 2, LANES), BF16)], axis=0)
        else:
            blk = _to_rows(s.astype(BF16), rows)
        parts.append(blk)
        used += rows
    parts.append(jnp.zeros((PACK_ROWS - used, LANES), BF16))
    return jnp.concatenate(parts, axis=0)


def _join(stack, axis):
    n, a, b = stack.shape
    if axis == 0:
        return stack.reshape(n * a, b)
    return stack.transpose(1, 0, 2).reshape(a, n * b)


def _split(full, axis):
    if axis == 0:
        return full.reshape(N_DEV, full.shape[0] // N_DEV, full.shape[1])
    return full.reshape(full.shape[0], N_DEV, full.shape[1] // N_DEV).transpose(1, 0, 2)


def unpack_full_weights(gathered):
    out = {}
    for name, _, _ in SHARDED:
        off, rows, axis, shape, numel = LAYOUT[name]
        seg = gathered[:, off:off + rows]
        if name == "conv_w":
            flat = seg[:, 0, :numel].astype(F32) + seg[:, 1, :numel].astype(F32)
        else:
            flat = seg.reshape(N_DEV, rows * LANES)[:, :numel]
        out[name] = _join(flat.reshape((N_DEV,) + shape), axis)
    return out


def pack_full_grads(grads):
    parts, used = [], 0
    for name, _, _ in SHARDED:
        off, rows, axis, shape, numel = LAYOUT[name]
        flat = _split(grads[name], axis).reshape(N_DEV, numel)
        parts.append(jnp.pad(flat, ((0, 0), (0, rows * LANES - numel))).reshape(N_DEV, rows, LANES))
        used += rows
    parts.append(jnp.zeros((N_DEV, PACK_ROWS - used, LANES), F32))
    return jnp.concatenate(parts, axis=1)


def unpack_grad_shards(packed):
    out = {}
    for name, _, _ in SHARDED:
        off, rows, axis, shape, numel = LAYOUT[name]
        out[name] = packed[off:off + rows].reshape(-1)[:numel].reshape(shape)
    return out


SMALL_SHAPES = {n: (1024,) for n in REPLICATED}
SMALL_SHAPES.update(pool_w=(4, 128, 128), pool_scale=(512,), dn_norm_w=(128,), a_log=(4,), dt_bias=(4,))


def _small_layout():
    off, table = 0, {}
    for name in REPLICATED:
        numel = 1
        for d in SMALL_SHAPES[name]:
            numel *= d
        rows = -(-numel // LANES)
        table[name] = (off, rows, numel)
        off += rows
    return table, _round_up(off, 8)


SMALL_LAYOUT, SMALL_ROWS = _small_layout()


def pack_small(grads):
    parts, used = [], 0
    for name in REPLICATED:
        off, rows, numel = SMALL_LAYOUT[name]
        parts.append(_to_rows(grads[name].reshape(-1), rows))
        used += rows
    parts.append(jnp.zeros((SMALL_ROWS - used, LANES), F32))
    return jnp.concatenate(parts, axis=0)


def unpack_small(packed):
    out = {}
    for name in REPLICATED:
        off, rows, numel = SMALL_LAYOUT[name]
        out[name] = packed[off:off + rows].reshape(-1)[:numel].reshape(SMALL_SHAPES[name])
    return out


MESH = pl.DeviceIdType.MESH
ANY = pl.BlockSpec(memory_space=pl.ANY)


def _position():
    return lax.axis_index("x"), lax.axis_index("y"), lax.axis_index("c")


def _other_chips(x, y):
    return [(1 - x, y), (x, 1 - y), (1 - x, 1 - y)]


def all_gather(name, block):
    rows, n = block.shape

    def body(x_ref, out_ref, send_sems, recv_sems, local_sem):
        x, y, c = _position()
        me, sibling = (x, y, c), (x, y, 1 - c)
        chips = _other_chips(x, y)

        def slot(px, py, pc):
            return out_ref.at[4 * px + 2 * py + pc]

        def copy(k, blk, to, src=None):
            return pltpu.make_async_remote_copy(
                src_ref=slot(*blk) if src is None else src, dst_ref=slot(*blk),
                send_sem=send_sems.at[k], recv_sem=recv_sems.at[k], device_id=to, device_id_type=MESH)

        mine = pltpu.make_async_copy(x_ref, slot(*me), local_sem)
        mine.start()
        first = [copy(0, me, sibling, src=x_ref)]
        first += [copy(1 + j, me, (*chip, c), src=x_ref) for j, chip in enumerate(chips)]
        for cp in first:
            cp.start()
        passed = [copy(4 + j, (*chip, c), sibling) for j, chip in enumerate(chips)]
        for j, chip in enumerate(chips):
            copy(1 + j, (*chip, c), me).wait_recv()
            passed[j].start()
        copy(0, sibling, me).wait_recv()
        for j, chip in enumerate(chips):
            copy(4 + j, (*chip, 1 - c), me).wait_recv()
        for cp in first + passed:
            cp.wait_send()
        mine.wait()

    return pl.pallas_call(
        body, name=name, out_shape=jax.ShapeDtypeStruct((N_DEV, rows, n), block.dtype),
        in_specs=[ANY], out_specs=ANY,
        scratch_shapes=[pltpu.SemaphoreType.DMA((7,)), pltpu.SemaphoreType.DMA((7,)), pltpu.SemaphoreType.DMA(())],
    )(block)


def sibling_exchange(name, send):
    def body(s_ref, r_ref, send_sem, recv_sem):
        x, y, c = _position()
        cp = pltpu.make_async_remote_copy(src_ref=s_ref, dst_ref=r_ref, send_sem=send_sem, recv_sem=recv_sem,
                                          device_id=(x, y, 1 - c), device_id_type=MESH)
        cp.start()
        cp.wait()

    return pl.pallas_call(
        body, name=name, out_shape=jax.ShapeDtypeStruct(send.shape, send.dtype), in_specs=[ANY], out_specs=ANY,
        scratch_shapes=[pltpu.SemaphoreType.DMA(()), pltpu.SemaphoreType.DMA(())],
    )(send)


def chip_exchange(name, send):
    def body(s_ref, r_ref, send_sems, recv_sems):
        x, y, c = _position()
        cps = [pltpu.make_async_remote_copy(src_ref=s_ref.at[j], dst_ref=r_ref.at[j], send_sem=send_sems.at[j],
                                            recv_sem=recv_sems.at[j], device_id=(*chip, c), device_id_type=MESH)
               for j, chip in enumerate(_other_chips(x, y))]
        for cp in cps:
            cp.start()
        for cp in cps:
            cp.wait()

    return pl.pallas_call(
        body, name=name, out_shape=jax.ShapeDtypeStruct(send.shape, send.dtype), in_specs=[ANY], out_specs=ANY,
        scratch_shapes=[pltpu.SemaphoreType.DMA((3,)), pltpu.SemaphoreType.DMA((3,))],
    )(send)


def _slot_sum(name, table, first, count, packed, received, out_dtype):
    rows = packed.shape[1]
    blk = (1, PACK_BLOCK, LANES)
    with_recv = received is not None

    def body(tbl_ref, *refs):
        if with_recv:
            g_ref, r_ref, o_ref = refs
            o_ref[...] = (g_ref[...] + r_ref[...].astype(F32)).astype(o_ref.dtype)
        else:
            g_ref, o_ref = refs
            o_ref[...] = g_ref[...].astype(o_ref.dtype)

    in_specs = [pl.BlockSpec(blk, lambda r, i, tbl: (tbl[first + r], i, 0))]
    ins = [packed]
    if with_recv:
        in_specs.append(pl.BlockSpec(blk, lambda r, i, tbl: (first + r, i, 0)))
        ins.append(received)
    return pl.pallas_call(
        body, name=name,
        grid_spec=pltpu.PrefetchScalarGridSpec(
            num_scalar_prefetch=1, grid=(count, rows // PACK_BLOCK), in_specs=in_specs,
            out_specs=pl.BlockSpec(blk, lambda r, i, tbl: (r, i, 0))),
        out_shape=jax.ShapeDtypeStruct((count, rows, LANES), out_dtype),
        compiler_params=_params(("parallel", "parallel")),
    )(table, *ins)


def _final_sum(name, own, received):
    rows = own.shape[0]

    def body(h_ref, r_ref, o_ref):
        acc = h_ref[...]
        for j in range(3):
            acc = acc + r_ref[j].astype(F32)
        o_ref[...] = acc

    return pl.pallas_call(
        body, name=name, grid=(rows // PACK_BLOCK,),
        in_specs=[pl.BlockSpec((PACK_BLOCK, LANES), lambda i: (i, 0)),
                  pl.BlockSpec((3, PACK_BLOCK, LANES), lambda i: (0, i, 0))],
        out_specs=pl.BlockSpec((PACK_BLOCK, LANES), lambda i: (i, 0)),
        out_shape=jax.ShapeDtypeStruct((rows, LANES), F32), compiler_params=_params(("parallel",)),
    )(own, received)


def _sum_slots(name, stack):
    n, rows, _ = stack.shape

    def body(s_ref, o_ref):
        acc = s_ref[0]
        for j in range(1, n):
            acc = acc + s_ref[j]
        o_ref[...] = acc

    return pl.pallas_call(
        body, name=name, in_specs=[pl.BlockSpec(stack.shape, lambda: (0, 0, 0))],
        out_specs=pl.BlockSpec((rows, LANES), lambda: (0, 0)), out_shape=jax.ShapeDtypeStruct((rows, LANES), F32),
    )(stack)


def reduce_scatter(packed):
    x, y, c = _position()
    chips = [(x, y)] + _other_chips(x, y)
    to_me = jnp.stack([4 * px + 2 * py + c for px, py in chips]).astype(jnp.int32)
    to_sibling = jnp.stack([4 * px + 2 * py + (1 - c) for px, py in chips]).astype(jnp.int32)
    send1 = _slot_sum("rs_send1", to_sibling, 0, 4, packed, None, WIRE)
    recv1 = sibling_exchange("rs_sibling", send1)
    own = _slot_sum("rs_own", to_me, 0, 1, packed, recv1, F32)[0]
    send2 = _slot_sum("rs_send2", to_me, 1, 3, packed, recv1, WIRE)
    recv2 = chip_exchange("rs_chips", send2)
    return _final_sum("rs_final", own, recv2)


def adamw(name, w, g, m, v):
    shape = w.shape
    last = shape[-1]
    w2, g2, m2, v2 = [a.reshape(-1, last) for a in (w, g, m, v)]
    rows = w2.shape[0]
    tr = 256 if rows % 256 == 0 else rows

    def body(w_ref, g_ref, m_ref, v_ref, d_ref, nm_ref, nv_ref):
        gg = g_ref[...]
        nm = ADAM_B1 * m_ref[...] + (1.0 - ADAM_B1) * gg
        nv = ADAM_B2 * v_ref[...] + (1.0 - ADAM_B2) * (gg * gg)
        m_hat = nm / (1.0 - ADAM_B1 ** ADAM_STEP)
        v_hat = nv / (1.0 - ADAM_B2 ** ADAM_STEP)
        d_ref[...] = -ADAM_LR * (m_hat / (jnp.sqrt(v_hat) + ADAM_EPS) + ADAM_WD * w_ref[...])
        nm_ref[...] = nm
        nv_ref[...] = nv

    spec = pl.BlockSpec((tr, last), lambda i: (i, 0))
    outs = pl.pallas_call(
        body, name=name, grid=(rows // tr,), in_specs=[spec] * 4, out_specs=[spec] * 3,
        out_shape=[jax.ShapeDtypeStruct((rows, last), F32)] * 3, compiler_params=_params(("parallel",)),
    )(w2, g2, m2, v2)
    return [o.reshape(shape) for o in outs]


def _full_weights(shards):
    gathered = all_gather("ag_weights", pack_weight_shards({n: shards[n][0] for n, _, _ in SHARDED}))
    w = unpack_full_weights(gathered)
    w_in = w.pop("w_in")
    w["in_qkv"] = w_in[:, :1536]
    w["in_z"] = w_in[:, 1536:2048]
    w["in_ab"] = jnp.pad(w_in[:, 2048:2056], ((0, 0), (0, 128 - 2 * DN_HEADS)))
    w["in_p"] = w_in[:, 2056:2568]
    w["in_gates"] = w_in[:, 2568:]
    for n in REPLICATED:
        w[n] = shards[n][0] if n == "pool_w" else shards[n]
    return w


def kernel(x, mem, ffn1_w_gate, ffn1_w_up, ffn1_w_down, ln1_g, ln1_b, w_in, conv_w, a_log, dt_bias, dn_norm_w, w_dn_branch, pool_w, pool_scale, w_pool_branch, w_mix_out, ln2_g, ln2_b, mem_ln_g, mem_ln_b, xa_wq, xa_wk, xa_wv, xa_wo, ln3_g, ln3_b, ffn2_w_gate, ffn2_w_up, ffn2_w_down, ln4_g, ln4_b, loss_target, m_ffn1_w_gate, m_ffn1_w_up, m_ffn1_w_down, m_ln1_g, m_ln1_b, m_w_in, m_conv_w, m_a_log, m_dt_bias, m_dn_norm_w, m_w_dn_branch, m_pool_w, m_pool_scale, m_w_pool_branch, m_w_mix_out, m_ln2_g, m_ln2_b, m_mem_ln_g, m_mem_ln_b, m_xa_wq, m_xa_wk, m_xa_wv, m_xa_wo, m_ln3_g, m_ln3_b, m_ffn2_w_gate, m_ffn2_w_up, m_ffn2_w_down, m_ln4_g, m_ln4_b, v_ffn1_w_gate, v_ffn1_w_up, v_ffn1_w_down, v_ln1_g, v_ln1_b, v_w_in, v_conv_w, v_a_log, v_dt_bias, v_dn_norm_w, v_w_dn_branch, v_pool_w, v_pool_scale, v_w_pool_branch, v_w_mix_out, v_ln2_g, v_ln2_b, v_mem_ln_g, v_mem_ln_b, v_xa_wq, v_xa_wk, v_xa_wv, v_xa_wo, v_ln3_g, v_ln3_b, v_ffn2_w_gate, v_ffn2_w_up, v_ffn2_w_down, v_ln4_g, v_ln4_b):
    given = dict(locals())
    shards = {n: given[n] for n in WEIGHT_NAMES}
    w = _full_weights(shards)
    loss_part, grad_x, g = local_step(x[0], mem[0], loss_target[0], w)

    g["w_in"] = jnp.concatenate([g.pop("in_qkv"), g.pop("in_z"), g.pop("in_ab")[:, :2 * DN_HEADS], g.pop("in_p"),
                                 g.pop("in_gates")], axis=1)
    grad = unpack_grad_shards(reduce_scatter(pack_full_grads(g)))
    grad.update(unpack_small(_sum_slots("small_sum", all_gather("ag_small", pack_small(g)))))
    grad = {n: grad[n].reshape(shards[n].shape) for n in WEIGHT_NAMES}

    loss = lax.psum(loss_part[0, 0], ("x", "y", "c"))
    updates = {n: adamw("adamw_" + n, shards[n], grad[n], given["m_" + n], given["v_" + n]) for n in WEIGHT_NAMES}
    return (loss, grad_x[None], *[grad[n] for n in WEIGHT_NAMES], *[updates[n][0] for n in WEIGHT_NAMES],
            *[updates[n][1] for n in WEIGHT_NAMES], *[updates[n][2] for n in WEIGHT_NAMES])
```

```python
import functools

import jax
import jax.numpy as jnp
from jax import lax
from jax.experimental import pallas as pl
from jax.experimental.pallas import tpu as pltpu

F32 = jnp.float32
BF16 = jnp.bfloat16
MMD = BF16
WIRE = BF16
HI = lax.Precision.HIGHEST
X3 = lax.Precision.HIGH
VMEM_LIMIT_BYTES = 48 * 1024 * 1024

D_MODEL = 1024
D_FF = 2816
CHUNK = 64
N_MEM = 256
DN_HEADS = 4
HD = 128
DN_WIDTH = 512
POOL_WINDOWS = (2, 4, 8, 16)
POOL_WIDTH = 512
XA_HEADS = 4
XA_HD = 256
LN_EPS = 1e-5
RMS_EPS = 1e-6
L2_EPS = 1e-6
ALPHA = 2.0 ** 0.25
HALO = 16

ADAM_LR = 0.001
ADAM_B1 = 0.9
ADAM_B2 = 0.999
ADAM_EPS = 1e-08
ADAM_WD = 0.01
ADAM_STEP = 10

N_DEV = 8
LANES = 1024


def _dot(a, b, ca, cb, prec):
    dn = (((ca,), (cb,)), ((), ()))
    if prec is not None:
        return lax.dot_general(a.astype(F32), b.astype(F32), dn, precision=prec, preferred_element_type=F32)
    return lax.dot_general(a.astype(MMD), b.astype(MMD), dn, preferred_element_type=F32)


def dnn(a, b, prec=None):
    return _dot(a, b, 1, 0, prec)


def dnt(a, b, prec=None):
    return _dot(a, b, 1, 1, prec)


def dtn(a, b, prec=None):
    return _dot(a, b, 0, 0, prec)


def _sigmoid(x):
    return jax.nn.sigmoid(x)


def _silu(x):
    return x * _sigmoid(x)


def _dsilu(x):
    s = _sigmoid(x)
    return s * (1.0 + x * (1.0 - s))


def _softplus(x):
    return jnp.maximum(x, 0.0) + jnp.log1p(jnp.exp(-jnp.abs(x)))


def _iota(shape, dim):
    return lax.broadcasted_iota(jnp.int32, shape, dim)


def _rsum(x):
    return jnp.sum(x, axis=1, keepdims=True)


def _csum(x):
    return jnp.sum(x, axis=0, keepdims=True)


def _pick(n, cands):
    for c in cands:
        if n % c == 0:
            return c
    return n


def _params(sem):
    return pltpu.CompilerParams(dimension_semantics=sem, vmem_limit_bytes=VMEM_LIMIT_BYTES)


def mm(name, a, b, *, ta=False, tb=False, out_dtype=F32, add=None, scale=None):
    if ta:
        kc, m = a.shape
    else:
        m, kc = a.shape
    if tb:
        n, kb = b.shape
    else:
        kb, n = b.shape
    assert kc == kb, (name, a.shape, b.shape)
    tm = m if m <= 512 else _pick(m, (512, 256, 128))
    tn = n if n <= 1024 else _pick(n, (1024, 1408, 768, 512))
    tk = kc if kc <= 1024 else _pick(kc, (1024, 1408, 768, 512))
    nk = kc // tk
    grid = (m // tm, n // tn, nk)
    a_spec = pl.BlockSpec((tk, tm), lambda i, j, k: (k, i)) if ta else pl.BlockSpec((tm, tk), lambda i, j, k: (i, k))
    b_spec = pl.BlockSpec((tn, tk), lambda i, j, k: (j, k)) if tb else pl.BlockSpec((tk, tn), lambda i, j, k: (k, j))
    o_spec = pl.BlockSpec((tm, tn), lambda i, j, k: (i, j))
    ca, cb = (0 if ta else 1), (1 if tb else 0)
    has_add = add is not None

    def body(*refs):
        if has_add:
            a_ref, b_ref, add_ref, o_ref, acc_ref = refs
        else:
            a_ref, b_ref, o_ref, acc_ref = refs
        k = pl.program_id(2)

        @pl.when(k == 0)
        def _():
            acc_ref[...] = jnp.zeros_like(acc_ref)

        acc_ref[...] += _dot(a_ref[...], b_ref[...], ca, cb, None)

        @pl.when(k == nk - 1)
        def _():
            r = acc_ref[...]
            if scale is not None:
                r = r * scale
            if has_add:
                r = r + add_ref[...]
            o_ref[...] = r.astype(o_ref.dtype)

    ins = [a, b] + ([add] if has_add else [])
    specs = [a_spec, b_spec] + ([o_spec] if has_add else [])
    return pl.pallas_call(
        body, name=name, grid=grid, in_specs=specs, out_specs=o_spec,
        out_shape=jax.ShapeDtypeStruct((m, n), out_dtype),
        scratch_shapes=[pltpu.VMEM((tm, tn), F32)],
        compiler_params=_params(("parallel", "parallel", "arbitrary")),
    )(*ins)


class _Ctx:
    def __init__(self, i, nblk, tl):
        self.i, self.nblk, self.tl = i, nblk, tl


def _norm_item(it):
    if isinstance(it, tuple):
        a, w, j = it[:3]
        rows = it[3] if len(it) > 3 else None
        return a, w, j, rows
    return it, it.shape[-1], 0, None


def rowwise(name, fn, length, tl, *, rows=(), consts=(), prevs=(), nexts=(), out_rows=(), out_accs=()):
    nblk = length // tl
    hb = tl // HALO
    nhalo = length // HALO
    arrays, specs = [], []
    for it in rows:
        a, w, j, r = _norm_item(it)
        if a.ndim == 3:
            specs.append(pl.BlockSpec((a.shape[0], tl, w), lambda i, j=j: (0, i, j)))
        else:
            specs.append(pl.BlockSpec((r or tl, w), lambda i, j=j: (i, j)))
        arrays.append(a)
    for a in consts:
        specs.append(pl.BlockSpec(a.shape, lambda i, nd=a.ndim: (0,) * nd))
        arrays.append(a)
    for it in prevs:
        a, w, j, _ = _norm_item(it)
        specs.append(pl.BlockSpec((HALO, w), lambda i, j=j: (jnp.maximum(i * hb - 1, 0), j)))
        arrays.append(a)
    for it in nexts:
        a, w, j, _ = _norm_item(it)
        specs.append(pl.BlockSpec((HALO, w), lambda i, j=j: (jnp.minimum((i + 1) * hb, nhalo - 1), j)))
        arrays.append(a)
    out_shape, out_specs = [], []
    for spec in out_rows:
        if len(spec) == 3:
            h, w, dt = spec
            out_shape.append(jax.ShapeDtypeStruct((h, length, w), dt))
            out_specs.append(pl.BlockSpec((h, tl, w), lambda i: (0, i, 0)))
        else:
            w, dt = spec
            out_shape.append(jax.ShapeDtypeStruct((length, w), dt))
            out_specs.append(pl.BlockSpec((tl, w), lambda i: (i, 0)))
    for shape, dt in out_accs:
        out_shape.append(jax.ShapeDtypeStruct(shape, dt))
        out_specs.append(pl.BlockSpec(shape, lambda i, nd=len(shape): (0,) * nd))
    n_r, n_c, n_p, n_n = len(rows), len(consts), len(prevs), len(nexts)
    n_in = n_r + n_c + n_p + n_n
    n_or = len(out_rows)

    def body(*refs):
        i = pl.program_id(0)
        vals = [r[...] for r in refs[:n_in]]
        outs = refs[n_in:]
        ctx = _Ctx(i, nblk, tl)
        ro, ao = fn(ctx, vals[:n_r], vals[n_r:n_r + n_c], vals[n_r + n_c:n_r + n_c + n_p], vals[n_r + n_c + n_p:])
        for r, v in zip(outs[:n_or], ro, strict=True):
            r[...] = v.astype(r.dtype)
        for r, v in zip(outs[n_or:], ao, strict=True):
            @pl.when(i == 0)
            def _(r=r, v=v):
                r[...] = v.astype(r.dtype)

            @pl.when(i > 0)
            def _(r=r, v=v):
                r[...] += v.astype(r.dtype)

    res = pl.pallas_call(
        body, name=name, grid=(nblk,), in_specs=specs, out_specs=out_specs, out_shape=out_shape,
        compiler_params=_params(("arbitrary",) if out_accs else ("parallel",)),
    )(*arrays)
    return res


def _heads(x, n, w):
    return [x[:, h * w:(h + 1) * w] for h in range(n)]


def _cat(xs):
    return jnp.concatenate(xs, axis=1)


def _row_index(ctx, nrows, offset=0):
    return ctx.i * ctx.tl + offset + _iota((nrows, 1), 0)


def _ln_stats(r):
    mu = jnp.mean(r, axis=1, keepdims=True)
    d = r - mu
    var = jnp.mean(d * d, axis=1, keepdims=True)
    rstd = lax.rsqrt(var + LN_EPS)
    return d * rstd, rstd


def ln_fwd(name, terms, g, b, tl=256):
    coefs = [c for c, _ in terms]
    length = terms[0][1].shape[0]

    def fn(ctx, rows, consts, prevs, nexts):
        r = sum(c * t for c, t in zip(coefs, rows))
        xh, _ = _ln_stats(r)
        return [xh * consts[0] + consts[1], r], []

    return rowwise(name, fn, length, min(tl, length), rows=[t for _, t in terms], consts=[g, b],
                   out_rows=[(D_MODEL, F32), (D_MODEL, F32)])


def ln_bwd(name, r, terms, g, tl=256):
    coefs = [c for c, _ in terms]
    length = r.shape[0]

    def fn(ctx, rows, consts, prevs, nexts):
        xh, rstd = _ln_stats(rows[0])
        dy = sum(c * t for c, t in zip(coefs, rows[1:]))
        dxh = dy * consts[0]
        dr = rstd * (dxh - jnp.mean(dxh, axis=1, keepdims=True) - xh * jnp.mean(dxh * xh, axis=1, keepdims=True))
        return [dr], [_csum(dy * xh), _csum(dy)]

    return rowwise(name, fn, length, min(tl, length), rows=[r] + [t for _, t in terms], consts=[g],
                   out_rows=[(D_MODEL, F32)], out_accs=[((1, D_MODEL), F32), ((1, D_MODEL), F32)])


def ln_loss(name, terms, g, b, target, tl=256):
    coefs = [c for c, _ in terms]
    length = target.shape[0]
    nt = len(terms)

    def fn(ctx, rows, consts, prevs, nexts):
        r = sum(c * t for c, t in zip(coefs, rows[:nt]))
        xh, _ = _ln_stats(r)
        err = xh * consts[0] + consts[1] - rows[nt]
        tot = _csum(_rsum(err * err)) * (0.5 / D_MODEL)
        return [err * (1.0 / D_MODEL), r], [jnp.broadcast_to(tot, (1, 128))]

    return rowwise(name, fn, length, min(tl, length), rows=[t for _, t in terms] + [target], consts=[g, b],
                   out_rows=[(D_MODEL, F32), (D_MODEL, F32)], out_accs=[((1, 128), F32)])


def axpy(name, terms, tl=256):
    coefs = [c for c, _ in terms]
    length, width = terms[0][1].shape

    def fn(ctx, rows, consts, prevs, nexts):
        return [sum(c * t for c, t in zip(coefs, rows))], []

    return rowwise(name, fn, length, min(tl, length), rows=[t for _, t in terms], out_rows=[(width, F32)])[0]


def ffn_fwd(tag, x, wg, wu, wd):
    length = x.shape[0]
    hg = mm(tag + "_gate", x, wg, tb=True)
    hu = mm(tag + "_up", x, wu, tb=True)

    def fn(ctx, rows, consts, prevs, nexts):
        return [_silu(rows[0]) * rows[1]], []

    act = rowwise(tag + "_act", fn, length, min(256, length), rows=[hg, hu], out_rows=[(D_FF, BF16)])[0]
    f = mm(tag + "_down", act, wd)
    return f, (hg, hu, act)


def ffn_bwd(tag, x, res, dr, wg, wu, wd):
    hg, hu, act = res
    length = x.shape[0]
    dact = mm(tag + "_dact", dr, wd, tb=True, scale=0.5)
    dwd = mm(tag + "_dwd", act, dr, ta=True, scale=0.5)

    def fn(ctx, rows, consts, prevs, nexts):
        g, u, da = rows
        return [da * u * _dsilu(g), da * _silu(g)], []

    dhg, dhu = rowwise(tag + "_dactb", fn, length, min(256, length), rows=[hg, hu, dact],
                       out_rows=[(D_FF, BF16), (D_FF, BF16)])
    dwg = mm(tag + "_dwg", dhg, x, ta=True)
    dwu = mm(tag + "_dwu", dhu, x, ta=True)
    dx = mm(tag + "_dxg", dhg, wg)
    dx = mm(tag + "_dxu", dhu, wu, add=dx)
    return dx, dwg, dwu, dwd


def _conv_taps(ext, taps, n):
    out = taps[3] * ext
    for j in range(3):
        out = out + taps[j] * pltpu.roll(ext, 3 - j, 0)
    return out


def _l2n(x):
    r = lax.rsqrt(_rsum(x * x) + L2_EPS)
    return x * r, r


def conv_fwd(name, pre, taps, tl=256):
    length = pre.shape[0]
    tl = min(tl, length)

    def fn(ctx, rows, consts, prevs, nexts):
        prev = jnp.where(ctx.i > 0, prevs[0], 0.0)
        ext = jnp.concatenate([prev, rows[0]], axis=0)
        s = _silu(_conv_taps(ext, consts, tl + HALO)[HALO:])
        q = _cat([_l2n(x)[0] * (HD ** -0.5) for x in _heads(s[:, :DN_WIDTH], DN_HEADS, HD)])
        k = _cat([_l2n(x)[0] for x in _heads(s[:, DN_WIDTH:2 * DN_WIDTH], DN_HEADS, HD)])
        return [q, k, s[:, 2 * DN_WIDTH:]], []

    return rowwise(name, fn, length, tl, rows=[pre], consts=list(taps), prevs=[pre],
                   out_rows=[(DN_WIDTH, F32)] * 3)


def conv_bwd(name, pre, dq, dk, dv, taps, tl=256):
    length = pre.shape[0]
    tl = min(tl, length)
    n = tl + 2 * HALO

    def fn(ctx, rows, consts, prevs, nexts):
        last = ctx.i == ctx.nblk - 1
        prev = jnp.where(ctx.i > 0, prevs[0], 0.0)
        ext = jnp.concatenate([prev, rows[0], nexts[0]], axis=0)
        c = _conv_taps(ext, consts, n)
        s = _silu(c)
        zero = jnp.zeros((HALO, DN_WIDTH), F32)
        dqe, dke, dve = [jnp.concatenate([zero, rows[1 + t], jnp.where(last, 0.0, nexts[1 + t])], axis=0)
                         for t in range(3)]

        def l2_bwd(x, dy):
            y, r = _l2n(x)
            return r * (dy - y * _rsum(dy * y))

        dsq = _cat([l2_bwd(x, d * (HD ** -0.5)) for x, d in zip(_heads(s[:, :DN_WIDTH], DN_HEADS, HD),
                                                                 _heads(dqe, DN_HEADS, HD))])
        dsk = _cat([l2_bwd(x, d) for x, d in zip(_heads(s[:, DN_WIDTH:2 * DN_WIDTH], DN_HEADS, HD),
                                                  _heads(dke, DN_HEADS, HD))])
        dc = _cat([dsq, dsk, dve]) * _dsilu(c)
        dpre = consts[3] * dc
        for j in range(3):
            dpre = dpre + consts[j] * pltpu.roll(dc, n - (3 - j), 0)
        dc_cur = dc[HALO:HALO + tl]
        dws = [_csum(dc_cur * pltpu.roll(ext, 3 - j, 0)[HALO:HALO + tl]) for j in range(3)]
        dws.append(_csum(dc_cur * ext[HALO:HALO + tl]))
        return [dpre[HALO:HALO + tl]], dws

    return rowwise(name, fn, length, tl, rows=[pre, dq, dk, dv], consts=list(taps), prevs=[pre],
                   nexts=[pre, dq, dk, dv], out_rows=[(3 * DN_WIDTH, BF16)],
                   out_accs=[((1, 3 * DN_WIDTH), F32)] * 4)


def _gate_consts():
    lane = jnp.arange(128)[:, None]
    col = jnp.arange(2 * DN_WIDTH)[None, :]
    sel = ((lane < 2 * DN_HEADS) & (col // HD == lane)).astype(F32)
    pick = ((col.T == lane.T * HD) & (lane.T < 2 * DN_HEADS)).astype(F32)
    return sel, pick


def _gate_math(ab, alog, dtb):
    z = ab + dtb
    g = -jnp.exp(alog) * _softplus(z)
    beta = _sigmoid(ab)
    return z, g, beta


def gates_fwd(name, ab, alog, dtb, sel, tl=256):
    length = ab.shape[0]

    def fn(ctx, rows, consts, prevs, nexts):
        _, g, beta = _gate_math(rows[0], consts[0], consts[1])
        lane = _iota(g.shape, 1)
        small = jnp.where(lane < DN_HEADS, g, jnp.where(lane < 2 * DN_HEADS, beta, 0.0))
        big = dnn(small, consts[2], HI)
        return [big[:, :DN_WIDTH], big[:, DN_WIDTH:]], []

    return rowwise(name, fn, length, min(tl, length), rows=[ab], consts=[alog, dtb, sel],
                   out_rows=[(DN_WIDTH, F32)] * 2)


def gates_bwd(name, ab, dgb, dbb, alog, dtb, pick, tl=256):
    length = ab.shape[0]

    def fn(ctx, rows, consts, prevs, nexts):
        z, g, beta = _gate_math(rows[0], consts[0], consts[1])
        dsmall = dnn(_cat([rows[1], rows[2]]), consts[2], HI)
        lane = _iota(g.shape, 1)
        is_a = lane < DN_HEADS
        da = jnp.where(is_a, dsmall * (-jnp.exp(consts[0])) * _sigmoid(z), 0.0)
        db = jnp.where((lane >= DN_HEADS) & (lane < 2 * DN_HEADS), dsmall * beta * (1.0 - beta), 0.0)
        return [da + db], [_csum(jnp.where(is_a, dsmall * g, 0.0)), _csum(da)]

    return rowwise(name, fn, length, min(tl, length), rows=[ab, dgb, dbb], consts=[alog, dtb, pick],
                   out_rows=[(128, BF16)], out_accs=[((1, 128), F32)] * 2)


CPS = 2


def _chunk_scan_rows(x, suffix=False):
    n = x.shape[0]
    rc = _iota(x.shape, 0) & (CHUNK - 1)
    sh = 1
    while sh < CHUNK:
        if suffix:
            x = x + jnp.where(rc < CHUNK - sh, pltpu.roll(x, n - sh, 0), 0.0)
        else:
            x = x + jnp.where(rc >= sh, pltpu.roll(x, sh, 0), 0.0)
        sh *= 2
    return x


def _tri_inv(a_list, eye, bd):
    def each(f, *ls):
        return [f(*xs) for xs in zip(*ls)]

    dg = [jnp.where(bd, a, 0.0) for a in a_list]
    lo = each(lambda a, d: a - d, a_list, dg)
    n1 = [-d for d in dg]
    n2 = each(lambda n: dnn(n, n, X3), n1)
    n4 = each(lambda n: dnn(n, n, X3), n2)
    td = each(lambda p, s: dnn(eye + p, eye + s, X3), n1, n2)
    n8 = each(lambda n: dnn(n, n, X3), n4)
    td = each(lambda t, n: dnn(t, eye + n, X3), td, n4)
    td = each(lambda t, n: dnn(t, eye + n, X3), td, n8)
    m = each(lambda t, l: dnn(t, l, X3), td, lo)
    m2 = each(lambda x: dnn(x, x, X3), m)
    x = each(lambda p, s: dnn(eye - p, eye + s, X3), m, m2)
    return each(lambda p, t: dnn(p, t, X3), x, td)


def _chunk_common(q, k, v, gcb, bb):
    egb = jnp.exp(gcb)
    gc64 = gcb[:, :CHUNK]
    ii, jj = _iota((CHUNK, CHUNK), 0), _iota((CHUNK, CHUNK), 1)
    incl, strict = ii >= jj, ii > jj
    decay = jnp.exp(jnp.where(incl, gc64 - gc64.T, -jnp.inf))
    kb = k * bb
    vb = v * bb
    kbe = kb * egb
    pq = dnt(jnp.concatenate([kb, q], axis=0), k, X3)
    ekb = jnp.exp(gcb[CHUNK - 1:CHUNK, :] - gcb)
    return dict(egb=egb, decay=decay, kb=kb, vb=vb, kbe=kbe, pm=pq[:CHUNK], qm=pq[CHUNK:], ekb=ekb,
                incl=incl, strict=strict, ii=ii, jj=jj)


def _chunk_head(vals, ci, h):
    return [v[ci * CHUNK:(ci + 1) * CHUNK, h * HD:(h + 1) * HD] for v in vals]


def _assemble(per_chunk):
    return jnp.concatenate([_cat(hs) for hs in per_chunk], axis=0)


def _assemble3(per_chunk):
    return jnp.stack([jnp.concatenate([per_chunk[ci][h] for ci in range(CPS)], axis=0) for h in range(DN_HEADS)])


def delta_prep_fwd(name, q, k, v, gb, bb):
    length = q.shape[0]

    def fn(ctx, rows, consts, prevs, nexts):
        gcb_all = _chunk_scan_rows(rows[3])
        vals = [rows[0], rows[1], rows[2], gcb_all, rows[4]]
        units = [(ci, h) for ci in range(CPS) for h in range(DN_HEADS)]
        ins = [_chunk_head(vals, ci, h) for ci, h in units]
        cs = [_chunk_common(*i) for i in ins]
        eye = (cs[0]["ii"] == cs[0]["jj"]).astype(F32)
        ts = _tri_inv([jnp.where(c["strict"], c["pm"] * c["decay"], 0.0) for c in cs], eye,
                      (cs[0]["ii"] >> 4) == (cs[0]["jj"] >> 4))
        uws = [dnn(t, _cat([c["vb"], c["kbe"]]), X3) for t, c in zip(ts, cs)]

        def grid2(xs):
            return [xs[ci * DN_HEADS:(ci + 1) * DN_HEADS] for ci in range(CPS)]

        return [_assemble(grid2([uw[:, :HD] for uw in uws])), _assemble(grid2([uw[:, HD:] for uw in uws])),
                _assemble(grid2([i[0] * c["egb"] for i, c in zip(ins, cs)])),
                _assemble(grid2([i[1] * c["ekb"] for i, c in zip(ins, cs)])), gcb_all,
                _assemble3(grid2([c["qm"] * c["decay"] for c in cs])), _assemble3(grid2(ts))], []

    return rowwise(name, fn, length, CHUNK * CPS, rows=[q, k, v, gb, bb],
                   out_rows=[(DN_WIDTH, F32)] * 5 + [(DN_HEADS, CHUNK, F32)] * 2)


def delta_prep_bwd(name, q, k, v, gb, bb, t3, du, dw, dqd, dkd, dattn3, dgl):
    length = q.shape[0]

    def fn(ctx, rows, consts, prevs, nexts):
        gcb_all = _chunk_scan_rows(rows[3])
        vals = [rows[0], rows[1], rows[2], gcb_all] + list(rows[4:9])
        t3v, da3v, dglv = rows[9], rows[10], rows[11]
        units = [(ci, h) for ci in range(CPS) for h in range(DN_HEADS)]
        ins = [_chunk_head(vals, ci, h) for ci, h in units]
        cs = [_chunk_common(*i[:5]) for i in ins]
        ts = [t3v[h][ci * CHUNK:(ci + 1) * CHUNK] for ci, h in units]
        dattns = [jnp.where(c["incl"], da3v[h][ci * CHUNK:(ci + 1) * CHUNK], 0.0) for (ci, h), c in zip(units, cs)]
        duws = [_cat([i[5], i[6]]) for i in ins]
        dvks = [dtn(t, d, X3) for t, d in zip(ts, duws)]
        dts = [dnt(d, _cat([c["vb"], c["kbe"]]), X3) for d, c in zip(duws, cs)]
        dts = [dnt(d, t, X3) for d, t in zip(dts, ts)]
        das = [jnp.where(c["strict"], -dtn(t, d, X3), 0.0) for c, t, d in zip(cs, ts, dts)]
        dpqs = [jnp.concatenate([da * c["decay"], dat * c["decay"]], axis=0) for da, dat, c in zip(das, dattns, cs)]
        dpqks = [dnn(d, i[1], X3) for d, i in zip(dpqs, ins)]
        dkps = [dtn(d, jnp.concatenate([c["kb"], i[0]], axis=0), X3) for d, c, i in zip(dpqs, cs, ins)]
        dqs, dks, dvs, dgcs, dbs = [], [], [], [], []
        for (ci, h), i, c, dvk, da, dattn, dpqk, dkp in zip(units, ins, cs, dvks, das, dattns, dpqks, dkps):
            qh, kh, vh, _, bh, _, _, dqdh, dkdh = i
            dvb, dkbe = dvk[:, :HD], dvk[:, HD:]
            dkb = dpqk[:CHUNK] + dkbe * c["egb"]
            c1 = _rsum(dkbe * c["kb"] + dqdh * qh) * c["egb"]
            c2 = _rsum(dkdh * kh) * c["ekb"]
            e = (da * c["pm"] + dattn * c["qm"]) * c["decay"]
            dgc = c1 - c2 + _rsum(e) - _rsum(e.T)
            dgl_tot = jnp.max(dglv[ci * 8:(ci + 1) * 8, h * HD:(h + 1) * HD], axis=0, keepdims=True) + _csum(c2)
            dgcs.append(dgc + jnp.where(_iota((CHUNK, HD), 0) == CHUNK - 1, dgl_tot, 0.0))
            dqs.append(dpqk[CHUNK:] + dqdh * c["egb"])
            dks.append(dkp + dkdh * c["ekb"] + dkb * bh)
            dvs.append(dvb * bh)
            dbs.append(jnp.broadcast_to(_rsum(dkb * kh + dvb * vh), (CHUNK, HD)))

        def grid2(xs):
            return [xs[ci * DN_HEADS:(ci + 1) * DN_HEADS] for ci in range(CPS)]

        return [_assemble(grid2(dqs)), _assemble(grid2(dks)), _assemble(grid2(dvs)),
                _chunk_scan_rows(_assemble(grid2(dgcs)), suffix=True), _assemble(grid2(dbs))], []

    return rowwise(name, fn, length, CHUNK * CPS,
                   rows=[q, k, v, gb, bb, du, dw, dqd, dkd, t3, dattn3, (dgl, DN_WIDTH, 0, 8 * CPS)],
                   out_rows=[(DN_WIDTH, F32)] * 5)


def delta_scan_fwd(name, qd, kd, u, w, attn3, gcb):
    length = qd.shape[0]
    n = length // CHUNK
    row = pl.BlockSpec((CHUNK, DN_WIDTH), lambda c: (c, 0))
    sq = pl.BlockSpec((DN_HEADS, CHUNK, CHUNK), lambda c: (0, c, 0))

    def body(qd_ref, kd_ref, u_ref, w_ref, attn_ref, gc_ref, o_ref, vn_ref, st_ref, s_ref):
        c = pl.program_id(0)

        @pl.when(c == 0)
        def _():
            s_ref[...] = jnp.zeros_like(s_ref)

        for h in range(DN_HEADS):
            sl = pl.ds(h * HD, HD)
            s = s_ref[h]
            st_ref[0, h] = s
            vn = u_ref[:, sl] - dnn(w_ref[:, sl], s)
            o_ref[:, sl] = dnn(qd_ref[:, sl], s) + dnn(attn_ref[h], vn)
            vn_ref[:, sl] = vn
            egl = jnp.exp(gc_ref[pl.ds(CHUNK - 1, 1), sl])
            s_ref[h] = s * egl + dtn(kd_ref[:, sl], vn)

    return pl.pallas_call(
        body, name=name, grid=(n,), in_specs=[row, row, row, row, sq, row],
        out_specs=[row, row, pl.BlockSpec((1, DN_HEADS, HD, HD), lambda c: (c, 0, 0, 0))],
        out_shape=[jax.ShapeDtypeStruct((length, DN_WIDTH), F32), jax.ShapeDtypeStruct((length, DN_WIDTH), F32),
                   jax.ShapeDtypeStruct((n, DN_HEADS, HD, HD), F32)],
        scratch_shapes=[pltpu.VMEM((DN_HEADS, HD, HD), F32)],
        compiler_params=_params(("arbitrary",)),
    )(qd, kd, u, w, attn3, gcb)


def delta_scan_bwd(name, do, qd, kd, w, attn3, vn, st, gcb):
    length = qd.shape[0]
    n = length // CHUNK
    row = pl.BlockSpec((CHUNK, DN_WIDTH), lambda c: (n - 1 - c, 0))
    sq = pl.BlockSpec((DN_HEADS, CHUNK, CHUNK), lambda c: (0, n - 1 - c, 0))
    stb = pl.BlockSpec((1, DN_HEADS, HD, HD), lambda c: (n - 1 - c, 0, 0, 0))
    glb = pl.BlockSpec((8, DN_WIDTH), lambda c: (n - 1 - c, 0))

    def body(do_ref, qd_ref, kd_ref, w_ref, attn_ref, vn_ref, st_ref, gc_ref,
             dqd_ref, dkd_ref, du_ref, dw_ref, dattn_ref, dgl_ref, ds_ref):
        c = pl.program_id(0)

        @pl.when(c == 0)
        def _():
            ds_ref[...] = jnp.zeros_like(ds_ref)

        for h in range(DN_HEADS):
            sl = pl.ds(h * HD, HD)
            s = st_ref[0, h]
            dsn = ds_ref[h]
            d_o = do_ref[:, sl]
            vnh = vn_ref[:, sl]
            egl = jnp.exp(gc_ref[pl.ds(CHUNK - 1, 1), sl])
            dattn_ref[h] = dnt(d_o, vnh)
            dvn = dtn(attn_ref[h], d_o) + dnn(kd_ref[:, sl], dsn)
            dqd_ref[:, sl] = dnt(d_o, s)
            dkd_ref[:, sl] = dnt(vnh, dsn)
            du_ref[:, sl] = dvn
            dw_ref[:, sl] = -dnt(dvn, s)
            dgl_ref[:, sl] = jnp.broadcast_to(_csum(_rsum(dsn * s)) * egl, (8, HD))
            ds_ref[h] = dsn * egl + dtn(qd_ref[:, sl], d_o) - dtn(w_ref[:, sl], dvn)

    return pl.pallas_call(
        body, name=name, grid=(n,), in_specs=[row, row, row, row, sq, row, stb, row],
        out_specs=[row, row, row, row, sq, glb],
        out_shape=[jax.ShapeDtypeStruct((length, DN_WIDTH), F32)] * 4
        + [jax.ShapeDtypeStruct((DN_HEADS, length, CHUNK), F32), jax.ShapeDtypeStruct((n * 8, DN_WIDTH), F32)],
        scratch_shapes=[pltpu.VMEM((DN_HEADS, HD, HD), F32)],
        compiler_params=_params(("arbitrary",)),
    )(do, qd, kd, w, attn3, vn, st, gcb)


def onorm_fwd(name, o, z, nw, tl=256):
    length = o.shape[0]

    def fn(ctx, rows, consts, prevs, nexts):
        outs = []
        for oh, zh in zip(_heads(rows[0], DN_HEADS, HD), _heads(rows[1], DN_HEADS, HD)):
            r = lax.rsqrt(jnp.mean(oh * oh, axis=1, keepdims=True) + RMS_EPS)
            outs.append(oh * r * consts[0] * _silu(zh))
        return [_cat(outs)], []

    return rowwise(name, fn, length, min(tl, length), rows=[o, z], consts=[nw], out_rows=[(DN_WIDTH, BF16)])[0]


def onorm_bwd(name, o, z, d_on, nw, tl=256):
    length = o.shape[0]

    def fn(ctx, rows, consts, prevs, nexts):
        dos, dzs = [], []
        dnw = jnp.zeros((1, HD), F32)
        for oh, zh, dh in zip(*[_heads(r, DN_HEADS, HD) for r in rows]):
            r = lax.rsqrt(jnp.mean(oh * oh, axis=1, keepdims=True) + RMS_EPS)
            y = oh * r
            sz = _silu(zh)
            t = dh * sz * consts[0]
            dos.append(r * (t - y * jnp.mean(t * y, axis=1, keepdims=True)))
            dzs.append(dh * y * consts[0] * _dsilu(zh))
            dnw = dnw + _csum(dh * y * sz)
        return [_cat(dos), _cat(dzs)], [dnw]

    return rowwise(name, fn, length, min(tl, length), rows=[o, z, d_on], consts=[nw],
                   out_rows=[(DN_WIDTH, F32), (DN_WIDTH, BF16)], out_accs=[((1, HD), F32)])


def merge_fwd(name, gates, ydn, ypool, tl=256):
    length = ydn.shape[0]

    def fn(ctx, rows, consts, prevs, nexts):
        gt = rows[0]
        return [_sigmoid(gt[:, :D_MODEL]) * rows[1] + _sigmoid(gt[:, D_MODEL:]) * rows[2]], []

    return rowwise(name, fn, length, min(tl, length), rows=[gates, ydn, ypool], out_rows=[(D_MODEL, BF16)])[0]


def merge_bwd(name, gates, ydn, ypool, dm, tl=256):
    length = ydn.shape[0]

    def fn(ctx, rows, consts, prevs, nexts):
        gt, yd, yp, d = rows
        sd, sp = _sigmoid(gt[:, :D_MODEL]), _sigmoid(gt[:, D_MODEL:])
        dgates = _cat([d * yd * sd * (1.0 - sd), d * yp * sp * (1.0 - sp)])
        return [d * sd, d * sp, dgates], []

    return rowwise(name, fn, length, min(tl, length), rows=[gates, ydn, ypool, dm],
                   out_rows=[(D_MODEL, BF16), (D_MODEL, BF16), (2 * D_MODEL, BF16)])


def _trailing_sums(ext, upto):
    s, sh = ext, 1
    while sh < upto:
        s = s + pltpu.roll(s, sh, 0)
        sh *= 2
    return s


def _leading_sums(ext, upto, n):
    s, sh = ext, 1
    while sh < upto:
        s = s + pltpu.roll(s, n - sh, 0)
        sh *= 2
    return s


def _pool_mixed(ctx, p, prev, tl):
    prevm = jnp.where(ctx.i > 0, prev, 0.0)
    t1 = (_row_index(ctx, tl) + 1).astype(F32)
    outs = []
    for gi, win in enumerate(POOL_WINDOWS):
        sl = slice(gi * HD, (gi + 1) * HD)
        ext = jnp.concatenate([prevm[:, sl], p[:, sl]], axis=0)
        mean = _trailing_sums(ext, win)[HALO:] / jnp.minimum(t1, float(win))
        outs.append(mean - p[:, sl])
    return outs


def pool_fwd(name, p, pool_w, scale, tl=256):
    length = p.shape[0]
    tl = min(tl, length)

    def fn(ctx, rows, consts, prevs, nexts):
        mixed = _pool_mixed(ctx, rows[0], prevs[0], tl)
        y = _cat([dnn(m, consts[0][gi]) for gi, m in enumerate(mixed)])
        return [y * consts[1]], []

    return rowwise(name, fn, length, tl, rows=[p], consts=[pool_w, scale], prevs=[p],
                   out_rows=[(POOL_WIDTH, BF16)])[0]


def pool_bwd(name, p, dpo, pool_w, scale, tl=256):
    length = p.shape[0]
    tl = min(tl, length)
    n = tl + HALO

    def fn(ctx, rows, consts, prevs, nexts):
        last = ctx.i == ctx.nblk - 1
        mixed = _pool_mixed(ctx, rows[0], prevs[0], tl)
        dext = jnp.concatenate([rows[1], jnp.where(last, 0.0, nexts[0])], axis=0)
        t1 = (_row_index(ctx, n) + 1).astype(F32)
        dps, dws, dscs = [], [], []
        for gi, win in enumerate(POOL_WINDOWS):
            sl = slice(gi * HD, (gi + 1) * HD)
            wg = consts[0][gi]
            dyraw = dext[:, sl] * consts[1][:, sl]
            dmix = dnt(dyraw, wg)
            dws.append(dtn(mixed[gi], dyraw[:tl]))
            dscs.append(_csum(rows[1][:, sl] * dnn(mixed[gi], wg)))
            lead = _leading_sums(dmix / jnp.minimum(t1, float(win)), win, n)
            dps.append(lead[:tl] - dmix[:tl])
        return [_cat(dps)], [jnp.stack(dws), _cat(dscs)]

    return rowwise(name, fn, length, tl, rows=[p, dpo], consts=[pool_w, scale], prevs=[p], nexts=[dpo],
                   out_rows=[(POOL_WIDTH, BF16)],
                   out_accs=[((len(POOL_WINDOWS), HD, HD), F32), ((1, POOL_WIDTH), F32)])


def _xa_probs(qh, kh):
    s = dnt(qh, kh) * (XA_HD ** -0.5)
    e = jnp.exp(s - jnp.max(s, axis=1, keepdims=True))
    return e / _rsum(e)


def xattn_fwd(name, qx, kx, vx, tl=256):
    length = qx.shape[0]

    def fn(ctx, rows, consts, prevs, nexts):
        outs = [dnn(_xa_probs(qh, kh), vh) for qh, kh, vh in
                zip(_heads(rows[0], XA_HEADS, XA_HD), _heads(consts[0], XA_HEADS, XA_HD),
                    _heads(consts[1], XA_HEADS, XA_HD))]
        return [_cat(outs)], []

    return rowwise(name, fn, length, min(tl, length), rows=[qx], consts=[kx, vx], out_rows=[(D_MODEL, BF16)])[0]


def xattn_bwd(name, qx, dox, kx, vx, tl=256):
    length = qx.shape[0]

    def fn(ctx, rows, consts, prevs, nexts):
        dqs, dks, dvs = [], [], []
        for qh, dh, kh, vh in zip(_heads(rows[0], XA_HEADS, XA_HD), _heads(rows[1], XA_HEADS, XA_HD),
                                  _heads(consts[0], XA_HEADS, XA_HD), _heads(consts[1], XA_HEADS, XA_HD)):
            pr = _xa_probs(qh, kh)
            dpr = dnt(dh, vh)
            ds = pr * (dpr - _rsum(dpr * pr)) * (XA_HD ** -0.5)
            dqs.append(dnn(ds, kh))
            dks.append(dtn(ds, qh))
            dvs.append(dtn(pr, dh))
        return [_cat(dqs)], [_cat(dks), _cat(dvs)]

    return rowwise(name, fn, length, min(tl, length), rows=[qx, dox], consts=[kx, vx],
                   out_rows=[(D_MODEL, BF16)], out_accs=[((N_MEM, D_MODEL), F32)] * 2)


def local_step(x, mem, target, w):
    sel, pick = _gate_consts()
    taps = [w["conv_w"][j:j + 1] for j in range(4)]
    alog = jnp.pad(w["a_log"], ((0, 0), (0, 128 - DN_HEADS)))
    dtb = jnp.pad(w["dt_bias"], ((0, 0), (0, 128 - DN_HEADS)))

    f1, res1 = ffn_fwd("ffn1", x, w["ffn1_w_gate"], w["ffn1_w_up"], w["ffn1_w_down"])
    x1, r1 = ln_fwd("ln1", [(ALPHA, x), (0.5, f1)], w["ln1_g"], w["ln1_b"])

    pre = mm("in_qkv", x1, w["in_qkv"], tb=True)
    z = mm("in_z", x1, w["in_z"], tb=True)
    gates = mm("in_gates", x1, w["in_gates"], tb=True)
    p = mm("in_p", x1, w["in_p"], tb=True)
    ab = mm("in_ab", x1, w["in_ab"], tb=True)
    q, k, v = conv_fwd("conv", pre, taps)
    gb, bb = gates_fwd("gates", ab, alog, dtb, sel)
    u, wd_, qd, kd, gcb, attn3, t3 = delta_prep_fwd("dprep", q, k, v, gb, bb)
    o, vn, st = delta_scan_fwd("dscan", qd, kd, u, wd_, attn3, gcb)
    on = onorm_fwd("onorm", o, z, w["dn_norm_w"])
    ydn = mm("dn_branch", on, w["w_dn_branch"], tb=True)
    po = pool_fwd("pool", p, w["pool_w"], w["pool_scale"])
    ypool = mm("pool_branch", po, w["w_pool_branch"], tb=True)
    merged = merge_fwd("merge", gates, ydn, ypool)
    mix = mm("mix_out", merged, w["w_mix_out"])
    x2, r2 = ln_fwd("ln2", [(ALPHA, x1), (1.0, mix)], w["ln2_g"], w["ln2_b"])

    m, _ = ln_fwd("ln_mem", [(1.0, mem)], w["mem_ln_g"], w["mem_ln_b"])
    qx = mm("xa_q", x2, w["xa_wq"])
    kx = mm("xa_k", m, w["xa_wk"])
    vx = mm("xa_v", m, w["xa_wv"])
    ox = xattn_fwd("xattn", qx, kx, vx)
    xa = mm("xa_o", ox, w["xa_wo"])
    x3, r3 = ln_fwd("ln3", [(ALPHA, x2), (1.0, xa)], w["ln3_g"], w["ln3_b"])

    f2, res2 = ffn_fwd("ffn2", x3, w["ffn2_w_gate"], w["ffn2_w_up"], w["ffn2_w_down"])
    dy4, r4, loss = ln_loss("ln4_loss", [(ALPHA, x3), (0.5, f2)], w["ln4_g"], w["ln4_b"], target)

    g = {}
    dr4, g["ln4_g"], g["ln4_b"] = ln_bwd("ln4_b", r4, [(1.0, dy4)], w["ln4_g"])
    dx3, g["ffn2_w_gate"], g["ffn2_w_up"], g["ffn2_w_down"] = ffn_bwd(
        "ffn2b", x3, res2, dr4, w["ffn2_w_gate"], w["ffn2_w_up"], w["ffn2_w_down"])
    dr3, g["ln3_g"], g["ln3_b"] = ln_bwd("ln3_b", r3, [(ALPHA, dr4), (1.0, dx3)], w["ln3_g"])

    dox = mm("xa_do", dr3, w["xa_wo"], tb=True)
    g["xa_wo"] = mm("xa_dwo", ox, dr3, ta=True)
    dqx, dkx, dvx = xattn_bwd("xattn_b", qx, dox, kx, vx)
    g["xa_wq"] = mm("xa_dwq", x2, dqx, ta=True)
    dx2 = mm("xa_dx", dqx, w["xa_wq"], tb=True)
    g["xa_wk"] = mm("xa_dwk", m, dkx, ta=True)
    g["xa_wv"] = mm("xa_dwv", m, dvx, ta=True)
    dmm = mm("xa_dmk", dkx, w["xa_wk"], tb=True)
    dmm = mm("xa_dmv", dvx, w["xa_wv"], tb=True, add=dmm)
    _, g["mem_ln_g"], g["mem_ln_b"] = ln_bwd("ln_mem_b", mem, [(1.0, dmm)], w["mem_ln_g"])
    dr2, g["ln2_g"], g["ln2_b"] = ln_bwd("ln2_b", r2, [(ALPHA, dr3), (1.0, dx2)], w["ln2_g"])

    dmerged = mm("mix_dm", dr2, w["w_mix_out"], tb=True)
    g["w_mix_out"] = mm("mix_dw", merged, dr2, ta=True)
    d_ydn, d_ypool, d_gates = merge_bwd("merge_b", gates, ydn, ypool, dmerged)
    g["w_dn_branch"] = mm("dn_dw", d_ydn, on, ta=True)
    d_on = mm("dn_dx", d_ydn, w["w_dn_branch"])
    g["w_pool_branch"] = mm("pool_dw", d_ypool, po, ta=True)
    d_po = mm("pool_dx", d_ypool, w["w_pool_branch"])
    dp, g["pool_w"], g["pool_scale"] = pool_bwd("pool_b", p, d_po, w["pool_w"], w["pool_scale"])
    d_o, dz, g["dn_norm_w"] = onorm_bwd("onorm_b", o, z, d_on, w["dn_norm_w"])
    dqd, dkd, du, dw_, dattn3, dgl = delta_scan_bwd("dscan_b", d_o, qd, kd, wd_, attn3, vn, st, gcb)
    dq, dk, dv, dgb, dbb = delta_prep_bwd("dprep_b", q, k, v, gb, bb, t3, du, dw_, dqd, dkd, dattn3, dgl)
    dpre, dc0, dc1, dc2, dc3 = conv_bwd("conv_b", pre, dq, dk, dv, taps)
    g["conv_w"] = jnp.concatenate([dc0, dc1, dc2, dc3], axis=0)
    d_ab, dalog, ddtb = gates_bwd("gates_b", ab, dgb, dbb, alog, dtb, pick)
    g["a_log"] = dalog[:, :DN_HEADS]
    g["dt_bias"] = ddtb[:, :DN_HEADS]
    g["in_qkv"] = mm("in_dwqkv", dpre, x1, ta=True)
    g["in_z"] = mm("in_dwz", dz, x1, ta=True)
    g["in_gates"] = mm("in_dwgates", d_gates, x1, ta=True)
    g["in_p"] = mm("in_dwp", dp, x1, ta=True)
    g["in_ab"] = mm("in_dwab", d_ab, x1, ta=True)
    dx1 = mm("in_dxqkv", dpre, w["in_qkv"])
    dx1 = mm("in_dxz", dz, w["in_z"], add=dx1)
    dx1 = mm("in_dxgates", d_gates, w["in_gates"], add=dx1)
    dx1 = mm("in_dxp", dp, w["in_p"], add=dx1)
    dx1 = mm("in_dxab", d_ab, w["in_ab"], add=dx1)
    dr1, g["ln1_g"], g["ln1_b"] = ln_bwd("ln1_b", r1, [(ALPHA, dr2), (1.0, dx1)], w["ln1_g"])
    dx0, g["ffn1_w_gate"], g["ffn1_w_up"], g["ffn1_w_down"] = ffn_bwd(
        "ffn1b", x, res1, dr1, w["ffn1_w_gate"], w["ffn1_w_up"], w["ffn1_w_down"])
    grad_x = axpy("grad_x", [(ALPHA, dr1), (1.0, dx0)])
    return loss, grad_x, g


WEIGHT_NAMES = ['ffn1_w_gate', 'ffn1_w_up', 'ffn1_w_down', 'ln1_g', 'ln1_b', 'w_in', 'conv_w', 'a_log', 'dt_bias',
                'dn_norm_w', 'w_dn_branch', 'pool_w', 'pool_scale', 'w_pool_branch', 'w_mix_out', 'ln2_g', 'ln2_b',
                'mem_ln_g', 'mem_ln_b', 'xa_wq', 'xa_wk', 'xa_wv', 'xa_wo', 'ln3_g', 'ln3_b', 'ffn2_w_gate',
                'ffn2_w_up', 'ffn2_w_down', 'ln4_g', 'ln4_b']
SHARDED = [
    ("ffn1_w_gate", "cols", (1024, 352)), ("ffn1_w_up", "cols", (1024, 352)), ("ffn1_w_down", "rows", (352, 1024)),
    ("w_in", "cols", (1024, 577)), ("conv_w", "flat", (4, 192)), ("w_dn_branch", "cols", (512, 128)),
    ("w_pool_branch", "cols", (512, 128)), ("w_mix_out", "rows", (128, 1024)), ("xa_wq", "rows", (128, 1024)),
    ("xa_wk", "rows", (128, 1024)), ("xa_wv", "rows", (128, 1024)), ("xa_wo", "rows", (128, 1024)),
    ("ffn2_w_gate", "cols", (1024, 352)), ("ffn2_w_up", "cols", (1024, 352)), ("ffn2_w_down", "rows", (352, 1024)),
]
REPLICATED = [n for n in WEIGHT_NAMES if n not in {s[0] for s in SHARDED}]
ROW_ALIGN = 16
PACK_BLOCK = 192
W_IN_COLS = 577
W_IN_PIECES = (("in_qkv", 0, 1536), ("in_z", 1536, 2048), ("in_ab", 2048, 2056), ("in_p", 2056, 2568),
               ("in_gates", 2568, 4616))


def _round_up(n, m):
    return -(-n // m) * m


def _layout():
    off, table = 0, {}
    for name, form, shape in SHARDED:
        valid = {"rows": shape[0], "cols": shape[1], "flat": 2}[form]
        width = {"rows": shape[1], "cols": shape[0], "flat": shape[0] * shape[1]}[form]
        rows = _round_up(valid, ROW_ALIGN)
        table[name] = (off, rows, valid, width, form, shape)
        off += rows
    return table, _round_up(off, PACK_BLOCK)


LAYOUT, PACK_ROWS = _layout()


def _pad_block(blk, rows):
    return jnp.pad(blk, ((0, rows - blk.shape[0]), (0, LANES - blk.shape[1])))


def pack_weight_shards(shards):
    parts, used = [], 0
    for name, form, _ in SHARDED:
        off, rows, valid, width, _, _ = LAYOUT[name]
        s = shards[name]
        if form == "flat":
            flat = s.reshape(1, -1)
            hi = flat.astype(BF16)
            blk = jnp.concatenate([hi, (flat - hi.astype(F32)).astype(BF16)], axis=0)
        else:
            blk = (s.T if form == "cols" else s).astype(BF16)
        parts.append(_pad_block(blk, rows))
        used += rows
    parts.append(jnp.zeros((PACK_ROWS - used, LANES), BF16))
    return jnp.concatenate(parts, axis=0)


def _w_in_rows(padded, rows, first, last):
    segs = []
    for k in range(N_DEV):
        lo, hi = max(first, k * W_IN_COLS), min(last, (k + 1) * W_IN_COLS)
        if lo < hi:
            segs.append(padded[k * rows + lo - k * W_IN_COLS:k * rows + hi - k * W_IN_COLS])
    return segs[0] if len(segs) == 1 else jnp.concatenate(segs, axis=0)


def unpack_full_weights(gathered):
    out = {}
    for name, form, shape in SHARDED:
        off, rows, valid, width, _, _ = LAYOUT[name]
        seg = gathered[:, off:off + rows]
        if form == "flat":
            flat = seg[:, 0, :width].astype(F32) + seg[:, 1, :width].astype(F32)
            out[name] = flat.reshape((N_DEV,) + shape).transpose(1, 0, 2).reshape(shape[0], N_DEV * shape[1])
        elif name == "w_in":
            padded = seg.reshape(N_DEV * rows, LANES)
            for piece, first, last in W_IN_PIECES:
                out[piece] = _w_in_rows(padded, rows, first, last)
        else:
            out[name] = seg[:, :valid, :width].reshape(N_DEV * valid, width)
    return out


def pack_full_grads(grads):
    parts, used = [], 0
    for name, form, shape in SHARDED:
        off, rows, valid, width, _, _ = LAYOUT[name]
        if form == "flat":
            full = grads[name].reshape(shape[0], N_DEV, shape[1]).transpose(1, 0, 2).reshape(N_DEV, 1, width)
        elif name == "w_in":
            full = jnp.concatenate([grads[piece][:last - first] for piece, first, last in W_IN_PIECES], axis=0)
            full = full.reshape(N_DEV, valid, width)
        else:
            full = grads[name].reshape(N_DEV, valid, width)
        parts.append(jnp.pad(full, ((0, 0), (0, rows - full.shape[1]), (0, LANES - width))))
        used += rows
    parts.append(jnp.zeros((N_DEV, PACK_ROWS - used, LANES), F32))
    return jnp.concatenate(parts, axis=1)


def unpack_grad_shards(packed):
    out = {}
    for name, form, shape in SHARDED:
        off, rows, valid, width, _, _ = LAYOUT[name]
        if form == "flat":
            out[name] = packed[off, :width].reshape(shape)
        elif form == "cols":
            out[name] = packed[off:off + valid, :width].T
        else:
            out[name] = packed[off:off + valid, :width]
    return out


SMALL_SHAPES = {n: (1024,) for n in REPLICATED}
SMALL_SHAPES.update(pool_w=(4, 128, 128), pool_scale=(512,), dn_norm_w=(128,), a_log=(4,), dt_bias=(4,))


def _small_layout():
    off, table = 0, {}
    for name in REPLICATED:
        numel = 1
        for d in SMALL_SHAPES[name]:
            numel *= d
        rows = -(-numel // LANES)
        table[name] = (off, rows, numel)
        off += rows
    return table, _round_up(off, 8)


SMALL_LAYOUT, SMALL_ROWS = _small_layout()


def _to_rows(flat, rows):
    return jnp.pad(flat, (0, rows * LANES - flat.shape[0])).reshape(rows, LANES)


def pack_small(grads):
    parts, used = [], 0
    for name in REPLICATED:
        off, rows, numel = SMALL_LAYOUT[name]
        parts.append(_to_rows(grads[name].reshape(-1), rows))
        used += rows
    parts.append(jnp.zeros((SMALL_ROWS - used, LANES), F32))
    return jnp.concatenate(parts, axis=0)


def unpack_small(packed):
    out = {}
    for name in REPLICATED:
        off, rows, numel = SMALL_LAYOUT[name]
        out[name] = packed[off:off + rows].reshape(-1)[:numel].reshape(SMALL_SHAPES[name])
    return out


MESH = pl.DeviceIdType.MESH
ANY = pl.BlockSpec(memory_space=pl.ANY)


def _position():
    return lax.axis_index("x"), lax.axis_index("y"), lax.axis_index("c")


def _other_chips(x, y):
    return [(1 - x, y), (x, 1 - y), (1 - x, 1 - y)]


def all_gather(name, block):
    rows, n = block.shape

    def body(x_ref, out_ref, send_sems, recv_sems, local_sem):
        x, y, c = _position()
        me, sibling = (x, y, c), (x, y, 1 - c)
        chips = _other_chips(x, y)

        def slot(px, py, pc):
            return out_ref.at[4 * px + 2 * py + pc]

        def copy(k, blk, to, src=None):
            return pltpu.make_async_remote_copy(
                src_ref=slot(*blk) if src is None else src, dst_ref=slot(*blk),
                send_sem=send_sems.at[k], recv_sem=recv_sems.at[k], device_id=to, device_id_type=MESH)

        mine = pltpu.make_async_copy(x_ref, slot(*me), local_sem)
        mine.start()
        first = [copy(0, me, sibling, src=x_ref)]
        first += [copy(1 + j, me, (*chip, c), src=x_ref) for j, chip in enumerate(chips)]
        for cp in first:
            cp.start()
        passed = [copy(4 + j, (*chip, c), sibling) for j, chip in enumerate(chips)]
        for j, chip in enumerate(chips):
            copy(1 + j, (*chip, c), me).wait_recv()
            passed[j].start()
        copy(0, sibling, me).wait_recv()
        for j, chip in enumerate(chips):
            copy(4 + j, (*chip, 1 - c), me).wait_recv()
        for cp in first + passed:
            cp.wait_send()
        mine.wait()

    return pl.pallas_call(
        body, name=name, out_shape=jax.ShapeDtypeStruct((N_DEV, rows, n), block.dtype),
        in_specs=[ANY], out_specs=ANY,
        scratch_shapes=[pltpu.SemaphoreType.DMA((7,)), pltpu.SemaphoreType.DMA((7,)), pltpu.SemaphoreType.DMA(())],
    )(block)


def sibling_exchange(name, send):
    def body(s_ref, r_ref, send_sem, recv_sem):
        x, y, c = _position()
        cp = pltpu.make_async_remote_copy(src_ref=s_ref, dst_ref=r_ref, send_sem=send_sem, recv_sem=recv_sem,
                                          device_id=(x, y, 1 - c), device_id_type=MESH)
        cp.start()
        cp.wait()

    return pl.pallas_call(
        body, name=name, out_shape=jax.ShapeDtypeStruct(send.shape, send.dtype), in_specs=[ANY], out_specs=ANY,
        scratch_shapes=[pltpu.SemaphoreType.DMA(()), pltpu.SemaphoreType.DMA(())],
    )(send)


def chip_exchange(name, send):
    def body(s_ref, r_ref, send_sems, recv_sems):
        x, y, c = _position()
        cps = [pltpu.make_async_remote_copy(src_ref=s_ref.at[j], dst_ref=r_ref.at[j], send_sem=send_sems.at[j],
                                            recv_sem=recv_sems.at[j], device_id=(*chip, c), device_id_type=MESH)
               for j, chip in enumerate(_other_chips(x, y))]
        for cp in cps:
            cp.start()
        for cp in cps:
            cp.wait()

    return pl.pallas_call(
        body, name=name, out_shape=jax.ShapeDtypeStruct(send.shape, send.dtype), in_specs=[ANY], out_specs=ANY,
        scratch_shapes=[pltpu.SemaphoreType.DMA((3,)), pltpu.SemaphoreType.DMA((3,))],
    )(send)


def _slot_sum(name, table, first, count, packed, received, out_dtype):
    rows = packed.shape[1]
    blk = (1, PACK_BLOCK, LANES)
    with_recv = received is not None

    def body(tbl_ref, *refs):
        if with_recv:
            g_ref, r_ref, o_ref = refs
            o_ref[...] = (g_ref[...] + r_ref[...].astype(F32)).astype(o_ref.dtype)
        else:
            g_ref, o_ref = refs
            o_ref[...] = g_ref[...].astype(o_ref.dtype)

    in_specs = [pl.BlockSpec(blk, lambda r, i, tbl: (tbl[first + r], i, 0))]
    ins = [packed]
    if with_recv:
        in_specs.append(pl.BlockSpec(blk, lambda r, i, tbl: (first + r, i, 0)))
        ins.append(received)
    return pl.pallas_call(
        body, name=name,
        grid_spec=pltpu.PrefetchScalarGridSpec(
            num_scalar_prefetch=1, grid=(count, rows // PACK_BLOCK), in_specs=in_specs,
            out_specs=pl.BlockSpec(blk, lambda r, i, tbl: (r, i, 0))),
        out_shape=jax.ShapeDtypeStruct((count, rows, LANES), out_dtype),
        compiler_params=_params(("parallel", "parallel")),
    )(table, *ins)


def _final_sum(name, own, received):
    rows = own.shape[0]

    def body(h_ref, r_ref, o_ref):
        acc = h_ref[...]
        for j in range(3):
            acc = acc + r_ref[j].astype(F32)
        o_ref[...] = acc

    return pl.pallas_call(
        body, name=name, grid=(rows // PACK_BLOCK,),
        in_specs=[pl.BlockSpec((PACK_BLOCK, LANES), lambda i: (i, 0)),
                  pl.BlockSpec((3, PACK_BLOCK, LANES), lambda i: (0, i, 0))],
        out_specs=pl.BlockSpec((PACK_BLOCK, LANES), lambda i: (i, 0)),
        out_shape=jax.ShapeDtypeStruct((rows, LANES), F32), compiler_params=_params(("parallel",)),
    )(own, received)


def _sum_slots(name, stack):
    n, rows, _ = stack.shape

    def body(s_ref, o_ref):
        acc = s_ref[0]
        for j in range(1, n):
            acc = acc + s_ref[j]
        o_ref[...] = acc

    return pl.pallas_call(
        body, name=name, in_specs=[pl.BlockSpec(stack.shape, lambda: (0, 0, 0))],
        out_specs=pl.BlockSpec((rows, LANES), lambda: (0, 0)), out_shape=jax.ShapeDtypeStruct((rows, LANES), F32),
    )(stack)


def reduce_scatter(packed):
    x, y, c = _position()
    chips = [(x, y)] + _other_chips(x, y)
    to_me = jnp.stack([4 * px + 2 * py + c for px, py in chips]).astype(jnp.int32)
    to_sibling = jnp.stack([4 * px + 2 * py + (1 - c) for px, py in chips]).astype(jnp.int32)
    send1 = _slot_sum("rs_send1", to_sibling, 0, 4, packed, None, WIRE)
    recv1 = sibling_exchange("rs_sibling", send1)
    own = _slot_sum("rs_own", to_me, 0, 1, packed, recv1, F32)[0]
    send2 = _slot_sum("rs_send2", to_me, 1, 3, packed, recv1, WIRE)
    recv2 = chip_exchange("rs_chips", send2)
    return _final_sum("rs_final", own, recv2)


def adamw(name, w, g, m, v):
    shape = w.shape
    last = shape[-1]
    w2, g2, m2, v2 = [a.reshape(-1, last) for a in (w, g, m, v)]
    rows = w2.shape[0]
    tr = 256 if rows % 256 == 0 else rows

    def body(w_ref, g_ref, m_ref, v_ref, d_ref, nm_ref, nv_ref):
        gg = g_ref[...]
        nm = ADAM_B1 * m_ref[...] + (1.0 - ADAM_B1) * gg
        nv = ADAM_B2 * v_ref[...] + (1.0 - ADAM_B2) * (gg * gg)
        m_hat = nm / (1.0 - ADAM_B1 ** ADAM_STEP)
        v_hat = nv / (1.0 - ADAM_B2 ** ADAM_STEP)
        d_ref[...] = -ADAM_LR * (m_hat / (jnp.sqrt(v_hat) + ADAM_EPS) + ADAM_WD * w_ref[...])
        nm_ref[...] = nm
        nv_ref[...] = nv

    spec = pl.BlockSpec((tr, last), lambda i: (i, 0))
    outs = pl.pallas_call(
        body, name=name, grid=(rows // tr,), in_specs=[spec] * 4, out_specs=[spec] * 3,
        out_shape=[jax.ShapeDtypeStruct((rows, last), F32)] * 3, compiler_params=_params(("parallel",)),
    )(w2, g2, m2, v2)
    return [o.reshape(shape) for o in outs]


def _full_weights(shards):
    gathered = all_gather("ag_weights", pack_weight_shards({n: shards[n][0] for n, _, _ in SHARDED}))
    w = unpack_full_weights(gathered)
    w["in_ab"] = jnp.pad(w["in_ab"], ((0, 128 - 2 * DN_HEADS), (0, 0)))
    for n in REPLICATED:
        w[n] = shards[n][0] if n == "pool_w" else shards[n]
    return w


def kernel(x, mem, ffn1_w_gate, ffn1_w_up, ffn1_w_down, ln1_g, ln1_b, w_in, conv_w, a_log, dt_bias, dn_norm_w, w_dn_branch, pool_w, pool_scale, w_pool_branch, w_mix_out, ln2_g, ln2_b, mem_ln_g, mem_ln_b, xa_wq, xa_wk, xa_wv, xa_wo, ln3_g, ln3_b, ffn2_w_gate, ffn2_w_up, ffn2_w_down, ln4_g, ln4_b, loss_target, m_ffn1_w_gate, m_ffn1_w_up, m_ffn1_w_down, m_ln1_g, m_ln1_b, m_w_in, m_conv_w, m_a_log, m_dt_bias, m_dn_norm_w, m_w_dn_branch, m_pool_w, m_pool_scale, m_w_pool_branch, m_w_mix_out, m_ln2_g, m_ln2_b, m_mem_ln_g, m_mem_ln_b, m_xa_wq, m_xa_wk, m_xa_wv, m_xa_wo, m_ln3_g, m_ln3_b, m_ffn2_w_gate, m_ffn2_w_up, m_ffn2_w_down, m_ln4_g, m_ln4_b, v_ffn1_w_gate, v_ffn1_w_up, v_ffn1_w_down, v_ln1_g, v_ln1_b, v_w_in, v_conv_w, v_a_log, v_dt_bias, v_dn_norm_w, v_w_dn_branch, v_pool_w, v_pool_scale, v_w_pool_branch, v_w_mix_out, v_ln2_g, v_ln2_b, v_mem_ln_g, v_mem_ln_b, v_xa_wq, v_xa_wk, v_xa_wv, v_xa_wo, v_ln3_g, v_ln3_b, v_ffn2_w_gate, v_ffn2_w_up, v_ffn2_w_down, v_ln4_g, v_ln4_b):
    given = dict(locals())
    shards = {n: given[n] for n in WEIGHT_NAMES}
    w = _full_weights(shards)
    loss_part, grad_x, g = local_step(x[0], mem[0], loss_target[0], w)

    grad =unpack_grad_shards(reduce_scatter(pack_full_grads(g)))
    grad.update(unpack_small(_sum_slots("small_sum", all_gather("ag_small", pack_small(g)))))
    grad = {n: grad[n].reshape(shards[n].shape) for n in WEIGHT_NAMES}

    loss = lax.psum(loss_part[0, 0], ("x", "y", "c"))
    updates = {n: adamw("adamw_" + n, shards[n], grad[n], given["m_" + n], given["v_" + n]) for n in WEIGHT_NAMES}
    return (loss, grad_x[None], *[grad[n] for n in WEIGHT_NAMES], *[updates[n][0] for n in WEIGHT_NAMES],
            *[updates[n][1] for n in WEIGHT_NAMES], *[updates[n][2] for n in WEIGHT_NAMES])
```

```python
import functools

import jax
import jax.numpy as jnp
from jax import lax
from jax.experimental import pallas as pl
from jax.experimental.pallas import tpu as pltpu

F32 = jnp.float32
BF16 = jnp.bfloat16
MMD = BF16
WIRE = BF16
HI = lax.Precision.HIGHEST
X3 = lax.Precision.HIGH
VMEM_LIMIT_BYTES = 48 * 1024 * 1024

D_MODEL = 1024
D_FF = 2816
CHUNK = 64
N_MEM = 256
DN_HEADS = 4
HD = 128
DN_WIDTH = 512
POOL_WINDOWS = (2, 4, 8, 16)
POOL_WIDTH = 512
XA_HEADS = 4
XA_HD = 256
LN_EPS = 1e-5
RMS_EPS = 1e-6
L2_EPS = 1e-6
ALPHA = 2.0 ** 0.25
HALO = 16

ADAM_LR = 0.001
ADAM_B1 = 0.9
ADAM_B2 = 0.999
ADAM_EPS = 1e-08
ADAM_WD = 0.01
ADAM_STEP = 10

N_DEV = 8
LANES = 1024
ANY = pl.BlockSpec(memory_space=pl.ANY)


def _dot(a, b, ca, cb, prec):
    dn = (((ca,), (cb,)), ((), ()))
    if prec is not None:
        return lax.dot_general(a.astype(F32), b.astype(F32), dn, precision=prec, preferred_element_type=F32)
    return lax.dot_general(a.astype(MMD), b.astype(MMD), dn, preferred_element_type=F32)


def dnn(a, b, prec=None):
    return _dot(a, b, 1, 0, prec)


def dnt(a, b, prec=None):
    return _dot(a, b, 1, 1, prec)


def dtn(a, b, prec=None):
    return _dot(a, b, 0, 0, prec)


def _sigmoid(x):
    return jax.nn.sigmoid(x)


def _silu(x):
    return x * _sigmoid(x)


def _dsilu(x):
    s = _sigmoid(x)
    return s * (1.0 + x * (1.0 - s))


def _softplus(x):
    return jnp.maximum(x, 0.0) + jnp.log1p(jnp.exp(-jnp.abs(x)))


def _iota(shape, dim):
    return lax.broadcasted_iota(jnp.int32, shape, dim)


def _rsum(x):
    return jnp.sum(x, axis=1, keepdims=True)


def _csum(x):
    return jnp.sum(x, axis=0, keepdims=True)


def _pick(n, cands):
    for c in cands:
        if n % c == 0:
            return c
    return n


def _params(sem):
    return pltpu.CompilerParams(dimension_semantics=sem, vmem_limit_bytes=VMEM_LIMIT_BYTES)


def mm(name, a, b, *, ta=False, tb=False, out_dtype=F32, add=None, scale=None, deps=()):
    if ta:
        kc, m = a.shape
    else:
        m, kc = a.shape
    if tb:
        n, kb = b.shape
    else:
        kb, n = b.shape
    assert kc == kb, (name, a.shape, b.shape)
    tm = m if m <= 512 else _pick(m, (512, 256, 128))
    tn = n if n <= 1024 else _pick(n, (1024, 1408, 768, 512))
    tk = kc if kc <= 1024 else _pick(kc, (1024, 1408, 768, 512))
    nk = kc // tk
    grid = (m // tm, n // tn, nk)
    a_spec = pl.BlockSpec((tk, tm), lambda i, j, k: (k, i)) if ta else pl.BlockSpec((tm, tk), lambda i, j, k: (i, k))
    b_spec = pl.BlockSpec((tn, tk), lambda i, j, k: (j, k)) if tb else pl.BlockSpec((tk, tn), lambda i, j, k: (k, j))
    o_spec = pl.BlockSpec((tm, tn), lambda i, j, k: (i, j))
    ca, cb = (0 if ta else 1), (1 if tb else 0)
    has_add = add is not None

    def body(*refs):
        a_ref, b_ref, o_ref, acc_ref = refs[0], refs[1], refs[-2], refs[-1]
        add_ref = refs[2] if has_add else None
        k = pl.program_id(2)

        @pl.when(k == 0)
        def _():
            acc_ref[...] = jnp.zeros_like(acc_ref)

        acc_ref[...] += _dot(a_ref[...], b_ref[...], ca, cb, None)

        @pl.when(k == nk - 1)
        def _():
            r = acc_ref[...]
            if scale is not None:
                r = r * scale
            if has_add:
                r = r + add_ref[...]
            o_ref[...] = r.astype(o_ref.dtype)

    ins = [a, b] + ([add] if has_add else []) + list(deps)
    specs = [a_spec, b_spec] + ([o_spec] if has_add else []) + [ANY] * len(deps)
    return pl.pallas_call(
        body, name=name, grid=grid, in_specs=specs, out_specs=o_spec,
        out_shape=jax.ShapeDtypeStruct((m, n), out_dtype),
        scratch_shapes=[pltpu.VMEM((tm, tn), F32)],
        compiler_params=_params(("parallel", "parallel", "arbitrary")),
    )(*ins)


class _Ctx:
    def __init__(self, i, nblk, tl):
        self.i, self.nblk, self.tl = i, nblk, tl


def _norm_item(it):
    if isinstance(it, tuple):
        a, w, j = it[:3]
        rows = it[3] if len(it) > 3 else None
        return a, w, j, rows
    return it, it.shape[-1], 0, None


def rowwise(name, fn, length, tl, *, rows=(), consts=(), prevs=(), nexts=(), out_rows=(), out_accs=(), deps=()):
    nblk = length // tl
    hb = tl // HALO
    nhalo = length // HALO
    arrays, specs = [], []
    for it in rows:
        a, w, j, r = _norm_item(it)
        if a.ndim == 3:
            specs.append(pl.BlockSpec((a.shape[0], tl, w), lambda i, j=j: (0, i, j)))
        else:
            specs.append(pl.BlockSpec((r or tl, w), lambda i, j=j: (i, j)))
        arrays.append(a)
    for a in consts:
        specs.append(pl.BlockSpec(a.shape, lambda i, nd=a.ndim: (0,) * nd))
        arrays.append(a)
    for it in prevs:
        a, w, j, _ = _norm_item(it)
        specs.append(pl.BlockSpec((HALO, w), lambda i, j=j: (jnp.maximum(i * hb - 1, 0), j)))
        arrays.append(a)
    for it in nexts:
        a, w, j, _ = _norm_item(it)
        specs.append(pl.BlockSpec((HALO, w), lambda i, j=j: (jnp.minimum((i + 1) * hb, nhalo - 1), j)))
        arrays.append(a)
    out_shape, out_specs = [], []
    for spec in out_rows:
        if len(spec) == 3:
            h, w, dt = spec
            out_shape.append(jax.ShapeDtypeStruct((h, length, w), dt))
            out_specs.append(pl.BlockSpec((h, tl, w), lambda i: (0, i, 0)))
        else:
            w, dt = spec
            out_shape.append(jax.ShapeDtypeStruct((length, w), dt))
            out_specs.append(pl.BlockSpec((tl, w), lambda i: (i, 0)))
    for shape, dt in out_accs:
        out_shape.append(jax.ShapeDtypeStruct(shape, dt))
        out_specs.append(pl.BlockSpec(shape, lambda i, nd=len(shape): (0,) * nd))
    n_r, n_c, n_p, n_n = len(rows), len(consts), len(prevs), len(nexts)
    n_in = n_r + n_c + n_p + n_n
    n_or = len(out_rows)
    arrays, specs = arrays + list(deps), specs + [ANY] * len(deps)

    def body(*refs):
        i = pl.program_id(0)
        vals = [r[...] for r in refs[:n_in]]
        outs = refs[n_in + len(deps):]
        ctx = _Ctx(i, nblk, tl)
        ro, ao = fn(ctx, vals[:n_r], vals[n_r:n_r + n_c], vals[n_r + n_c:n_r + n_c + n_p], vals[n_r + n_c + n_p:])
        for r, v in zip(outs[:n_or], ro, strict=True):
            r[...] = v.astype(r.dtype)
        for r, v in zip(outs[n_or:], ao, strict=True):
            @pl.when(i == 0)
            def _(r=r, v=v):
                r[...] = v.astype(r.dtype)

            @pl.when(i > 0)
            def _(r=r, v=v):
                r[...] += v.astype(r.dtype)

    res = pl.pallas_call(
        body, name=name, grid=(nblk,), in_specs=specs, out_specs=out_specs, out_shape=out_shape,
        compiler_params=_params(("arbitrary",) if out_accs else ("parallel",)),
    )(*arrays)
    return res


def _heads(x, n, w):
    return [x[:, h * w:(h + 1) * w] for h in range(n)]


def _cat(xs):
    return jnp.concatenate(xs, axis=1)


def _row_index(ctx, nrows, offset=0):
    return ctx.i * ctx.tl + offset + _iota((nrows, 1), 0)


def _ln_stats(r):
    mu = jnp.mean(r, axis=1, keepdims=True)
    d = r - mu
    var = jnp.mean(d * d, axis=1, keepdims=True)
    rstd = lax.rsqrt(var + LN_EPS)
    return d * rstd, rstd


def ln_fwd(name, terms, g, b, tl=256, deps=()):
    coefs = [c for c, _ in terms]
    length = terms[0][1].shape[0]

    def fn(ctx, rows, consts, prevs, nexts):
        r = sum(c * t for c, t in zip(coefs, rows))
        xh, _ = _ln_stats(r)
        return [xh * consts[0] + consts[1], r], []

    return rowwise(name, fn, length, min(tl, length), rows=[t for _, t in terms], consts=[g, b],
                   out_rows=[(D_MODEL, F32), (D_MODEL, F32)], deps=deps)


def ln_bwd(name, r, terms, g, tl=256, deps=()):
    coefs = [c for c, _ in terms]
    length = r.shape[0]

    def fn(ctx, rows, consts, prevs, nexts):
        xh, rstd = _ln_stats(rows[0])
        dy = sum(c * t for c, t in zip(coefs, rows[1:]))
        dxh = dy * consts[0]
        dr = rstd * (dxh - jnp.mean(dxh, axis=1, keepdims=True) - xh * jnp.mean(dxh * xh, axis=1, keepdims=True))
        return [dr], [_csum(dy * xh), _csum(dy)]

    return rowwise(name, fn, length, min(tl, length), rows=[r] + [t for _, t in terms], consts=[g],
                   out_rows=[(D_MODEL, F32)], out_accs=[((1, D_MODEL), F32), ((1, D_MODEL), F32)], deps=deps)


def ln_loss(name, terms, g, b, target, tl=256):
    coefs = [c for c, _ in terms]
    length = target.shape[0]
    nt = len(terms)

    def fn(ctx, rows, consts, prevs, nexts):
        r = sum(c * t for c, t in zip(coefs, rows[:nt]))
        xh, _ = _ln_stats(r)
        err = xh * consts[0] + consts[1] - rows[nt]
        tot = _csum(_rsum(err * err)) * (0.5 / D_MODEL)
        return [err * (1.0 / D_MODEL), r], [jnp.broadcast_to(tot, (1, 128))]

    return rowwise(name, fn, length, min(tl, length), rows=[t for _, t in terms] + [target], consts=[g, b],
                   out_rows=[(D_MODEL, F32), (D_MODEL, F32)], out_accs=[((1, 128), F32)])


def axpy(name, terms, tl=256):
    coefs = [c for c, _ in terms]
    length, width = terms[0][1].shape

    def fn(ctx, rows, consts, prevs, nexts):
        return [sum(c * t for c, t in zip(coefs, rows))], []

    return rowwise(name, fn, length, min(tl, length), rows=[t for _, t in terms], out_rows=[(width, F32)])[0]


def ffn_fwd(tag, x, wg, wu, wd, deps=()):
    length = x.shape[0]
    hg = mm(tag + "_gate", x, wg, tb=True, deps=deps)
    hu = mm(tag + "_up", x, wu, tb=True)

    def fn(ctx, rows, consts, prevs, nexts):
        return [_silu(rows[0]) * rows[1]], []

    act = rowwise(tag + "_act", fn, length, min(256, length), rows=[hg, hu], out_rows=[(D_FF, BF16)])[0]
    f = mm(tag + "_down", act, wd)
    return f, (hg, hu, act)


def ffn_bwd(tag, x, res, dr, wg, wu, wd):
    hg, hu, act = res
    length = x.shape[0]
    dact = mm(tag + "_dact", dr, wd, tb=True, scale=0.5)
    dwd = mm(tag + "_dwd", act, dr, ta=True, scale=0.5)

    def fn(ctx, rows, consts, prevs, nexts):
        g, u, da = rows
        return [da * u * _dsilu(g), da * _silu(g)], []

    dhg, dhu = rowwise(tag + "_dactb", fn, length, min(256, length), rows=[hg, hu, dact],
                       out_rows=[(D_FF, BF16), (D_FF, BF16)])
    dwg = mm(tag + "_dwg", dhg, x, ta=True)
    dwu = mm(tag + "_dwu", dhu, x, ta=True)
    dx = mm(tag + "_dxg", dhg, wg)
    dx = mm(tag + "_dxu", dhu, wu, add=dx)
    return dx, dwg, dwu, dwd


def _conv_taps(ext, taps, n):
    out = taps[3] * ext
    for j in range(3):
        out = out + taps[j] * pltpu.roll(ext, 3 - j, 0)
    return out


def _l2n(x):
    r = lax.rsqrt(_rsum(x * x) + L2_EPS)
    return x * r, r


def conv_fwd(name, pre, taps, tl=256):
    length = pre.shape[0]
    tl = min(tl, length)

    def fn(ctx, rows, consts, prevs, nexts):
        prev = jnp.where(ctx.i > 0, prevs[0], 0.0)
        ext = jnp.concatenate([prev, rows[0]], axis=0)
        s = _silu(_conv_taps(ext, consts, tl + HALO)[HALO:])
        q = _cat([_l2n(x)[0] * (HD ** -0.5) for x in _heads(s[:, :DN_WIDTH], DN_HEADS, HD)])
        k = _cat([_l2n(x)[0] for x in _heads(s[:, DN_WIDTH:2 * DN_WIDTH], DN_HEADS, HD)])
        return [q, k, s[:, 2 * DN_WIDTH:]], []

    return rowwise(name, fn, length, tl, rows=[pre], consts=list(taps), prevs=[pre],
                   out_rows=[(DN_WIDTH, F32)] * 3)


def conv_bwd(name, pre, dq, dk, dv, taps, tl=256):
    length = pre.shape[0]
    tl = min(tl, length)
    n = tl + 2 * HALO

    def fn(ctx, rows, consts, prevs, nexts):
        last = ctx.i == ctx.nblk - 1
        prev = jnp.where(ctx.i > 0, prevs[0], 0.0)
        ext = jnp.concatenate([prev, rows[0], nexts[0]], axis=0)
        c = _conv_taps(ext, consts, n)
        s = _silu(c)
        zero = jnp.zeros((HALO, DN_WIDTH), F32)
        dqe, dke, dve = [jnp.concatenate([zero, rows[1 + t], jnp.where(last, 0.0, nexts[1 + t])], axis=0)
                         for t in range(3)]

        def l2_bwd(x, dy):
            y, r = _l2n(x)
            return r * (dy - y * _rsum(dy * y))

        dsq = _cat([l2_bwd(x, d * (HD ** -0.5)) for x, d in zip(_heads(s[:, :DN_WIDTH], DN_HEADS, HD),
                                                                 _heads(dqe, DN_HEADS, HD))])
        dsk = _cat([l2_bwd(x, d) for x, d in zip(_heads(s[:, DN_WIDTH:2 * DN_WIDTH], DN_HEADS, HD),
                                                  _heads(dke, DN_HEADS, HD))])
        dc = _cat([dsq, dsk, dve]) * _dsilu(c)
        dpre = consts[3] * dc
        for j in range(3):
            dpre = dpre + consts[j] * pltpu.roll(dc, n - (3 - j), 0)
        dc_cur = dc[HALO:HALO + tl]
        dws = [_csum(dc_cur * pltpu.roll(ext, 3 - j, 0)[HALO:HALO + tl]) for j in range(3)]
        dws.append(_csum(dc_cur * ext[HALO:HALO + tl]))
        return [dpre[HALO:HALO + tl]], dws

    return rowwise(name, fn, length, tl, rows=[pre, dq, dk, dv], consts=list(taps), prevs=[pre],
                   nexts=[pre, dq, dk, dv], out_rows=[(3 * DN_WIDTH, BF16)],
                   out_accs=[((1, 3 * DN_WIDTH), F32)] * 4)


def _gate_consts():
    lane = jnp.arange(128)[:, None]
    col = jnp.arange(2 * DN_WIDTH)[None, :]
    sel = ((lane < 2 * DN_HEADS) & (col // HD == lane)).astype(F32)
    pick = ((col.T == lane.T * HD) & (lane.T < 2 * DN_HEADS)).astype(F32)
    return sel, pick


def _gate_math(ab, alog, dtb):
    z = ab + dtb
    g = -jnp.exp(alog) * _softplus(z)
    beta = _sigmoid(ab)
    return z, g, beta


def gates_fwd(name, ab, alog, dtb, sel, tl=256):
    length = ab.shape[0]

    def fn(ctx, rows, consts, prevs, nexts):
        _, g, beta = _gate_math(rows[0], consts[0], consts[1])
        lane = _iota(g.shape, 1)
        small = jnp.where(lane < DN_HEADS, g, jnp.where(lane < 2 * DN_HEADS, beta, 0.0))
        big = dnn(small, consts[2], HI)
        return [big[:, :DN_WIDTH], big[:, DN_WIDTH:]], []

    return rowwise(name, fn, length, min(tl, length), rows=[ab], consts=[alog, dtb, sel],
                   out_rows=[(DN_WIDTH, F32)] * 2)


def gates_bwd(name, ab, dgb, dbb, alog, dtb, pick, tl=256):
    length = ab.shape[0]

    def fn(ctx, rows, consts, prevs, nexts):
        z, g, beta = _gate_math(rows[0], consts[0], consts[1])
        dsmall = dnn(_cat([rows[1], rows[2]]), consts[2], HI)
        lane = _iota(g.shape, 1)
        is_a = lane < DN_HEADS
        da = jnp.where(is_a, dsmall * (-jnp.exp(consts[0])) * _sigmoid(z), 0.0)
        db = jnp.where((lane >= DN_HEADS) & (lane < 2 * DN_HEADS), dsmall * beta * (1.0 - beta), 0.0)
        return [da + db], [_csum(jnp.where(is_a, dsmall * g, 0.0)), _csum(da)]

    return rowwise(name, fn, length, min(tl, length), rows=[ab, dgb, dbb], consts=[alog, dtb, pick],
                   out_rows=[(128, BF16)], out_accs=[((1, 128), F32)] * 2)


CPS = 2


def _chunk_scan_rows(x, suffix=False):
    n = x.shape[0]
    rc = _iota(x.shape, 0) & (CHUNK - 1)
    sh = 1
    while sh < CHUNK:
        if suffix:
            x = x + jnp.where(rc < CHUNK - sh, pltpu.roll(x, n - sh, 0), 0.0)
        else:
            x = x + jnp.where(rc >= sh, pltpu.roll(x, sh, 0), 0.0)
        sh *= 2
    return x


def _tri_inv(a_list, eye, bd):
    def each(f, *ls):
        return [f(*xs) for xs in zip(*ls)]

    dg = [jnp.where(bd, a, 0.0) for a in a_list]
    lo = each(lambda a, d: a - d, a_list, dg)
    n1 = [-d for d in dg]
    n2 = each(lambda n: dnn(n, n, X3), n1)
    n4 = each(lambda n: dnn(n, n, X3), n2)
    td = each(lambda p, s: dnn(eye + p, eye + s, X3), n1, n2)
    n8 = each(lambda n: dnn(n, n, X3), n4)
    td = each(lambda t, n: dnn(t, eye + n, X3), td, n4)
    td = each(lambda t, n: dnn(t, eye + n, X3), td, n8)
    m = each(lambda t, l: dnn(t, l, X3), td, lo)
    m2 = each(lambda x: dnn(x, x, X3), m)
    x = each(lambda p, s: dnn(eye - p, eye + s, X3), m, m2)
    return each(lambda p, t: dnn(p, t, X3), x, td)


def _chunk_common(q, k, v, gcb, bb):
    egb = jnp.exp(gcb)
    gc64 = gcb[:, :CHUNK]
    ii, jj = _iota((CHUNK, CHUNK), 0), _iota((CHUNK, CHUNK), 1)
    incl, strict = ii >= jj, ii > jj
    decay = jnp.exp(jnp.where(incl, gc64 - gc64.T, -jnp.inf))
    kb = k * bb
    vb = v * bb
    kbe = kb * egb
    pq = dnt(jnp.concatenate([kb, q], axis=0), k, X3)
    ekb = jnp.exp(gcb[CHUNK - 1:CHUNK, :] - gcb)
    return dict(egb=egb, decay=decay, kb=kb, vb=vb, kbe=kbe, pm=pq[:CHUNK], qm=pq[CHUNK:], ekb=ekb,
                incl=incl, strict=strict, ii=ii, jj=jj)


def _chunk_head(vals, ci, h):
    return [v[ci * CHUNK:(ci + 1) * CHUNK, h * HD:(h + 1) * HD] for v in vals]


def _assemble(per_chunk):
    return jnp.concatenate([_cat(hs) for hs in per_chunk], axis=0)


def _assemble3(per_chunk):
    return jnp.stack([jnp.concatenate([per_chunk[ci][h] for ci in range(CPS)], axis=0) for h in range(DN_HEADS)])


def delta_prep_fwd(name, q, k, v, gb, bb):
    length = q.shape[0]

    def fn(ctx, rows, consts, prevs, nexts):
        gcb_all = _chunk_scan_rows(rows[3])
        vals = [rows[0], rows[1], rows[2], gcb_all, rows[4]]
        units = [(ci, h) for ci in range(CPS) for h in range(DN_HEADS)]
        ins = [_chunk_head(vals, ci, h) for ci, h in units]
        cs = [_chunk_common(*i) for i in ins]
        eye = (cs[0]["ii"] == cs[0]["jj"]).astype(F32)
        ts = _tri_inv([jnp.where(c["strict"], c["pm"] * c["decay"], 0.0) for c in cs], eye,
                      (cs[0]["ii"] >> 4) == (cs[0]["jj"] >> 4))
        uws = [dnn(t, _cat([c["vb"], c["kbe"]]), X3) for t, c in zip(ts, cs)]

        def grid2(xs):
            return [xs[ci * DN_HEADS:(ci + 1) * DN_HEADS] for ci in range(CPS)]

        return [_assemble(grid2([uw[:, :HD] for uw in uws])), _assemble(grid2([uw[:, HD:] for uw in uws])),
                _assemble(grid2([i[0] * c["egb"] for i, c in zip(ins, cs)])),
                _assemble(grid2([i[1] * c["ekb"] for i, c in zip(ins, cs)])), gcb_all,
                _assemble3(grid2([c["qm"] * c["decay"] for c in cs])), _assemble3(grid2(ts))], []

    return rowwise(name, fn, length, CHUNK * CPS, rows=[q, k, v, gb, bb],
                   out_rows=[(DN_WIDTH, F32)] * 5 + [(DN_HEADS, CHUNK, F32)] * 2)


def delta_prep_bwd(name, q, k, v, gb, bb, t3, du, dw, dqd, dkd, dattn3, dgl):
    length = q.shape[0]

    def fn(ctx, rows, consts, prevs, nexts):
        gcb_all = _chunk_scan_rows(rows[3])
        vals = [rows[0], rows[1], rows[2], gcb_all] + list(rows[4:9])
        t3v, da3v, dglv = rows[9], rows[10], rows[11]
        units = [(ci, h) for ci in range(CPS) for h in range(DN_HEADS)]
        ins = [_chunk_head(vals, ci, h) for ci, h in units]
        cs = [_chunk_common(*i[:5]) for i in ins]
        ts = [t3v[h][ci * CHUNK:(ci + 1) * CHUNK] for ci, h in units]
        dattns = [jnp.where(c["incl"], da3v[h][ci * CHUNK:(ci + 1) * CHUNK], 0.0) for (ci, h), c in zip(units, cs)]
        duws = [_cat([i[5], i[6]]) for i in ins]
        dvks = [dtn(t, d, X3) for t, d in zip(ts, duws)]
        dts = [dnt(d, _cat([c["vb"], c["kbe"]]), X3) for d, c in zip(duws, cs)]
        dts = [dnt(d, t, X3) for d, t in zip(dts, ts)]
        das = [jnp.where(c["strict"], -dtn(t, d, X3), 0.0) for c, t, d in zip(cs, ts, dts)]
        dpqs = [jnp.concatenate([da * c["decay"], dat * c["decay"]], axis=0) for da, dat, c in zip(das, dattns, cs)]
        dpqks = [dnn(d, i[1], X3) for d, i in zip(dpqs, ins)]
        dkps = [dtn(d, jnp.concatenate([c["kb"], i[0]], axis=0), X3) for d, c, i in zip(dpqs, cs, ins)]
        dqs, dks, dvs, dgcs, dbs = [], [], [], [], []
        for (ci, h), i, c, dvk, da, dattn, dpqk, dkp in zip(units, ins, cs, dvks, das, dattns, dpqks, dkps):
            qh, kh, vh, _, bh, _, _, dqdh, dkdh = i
            dvb, dkbe = dvk[:, :HD], dvk[:, HD:]
            dkb = dpqk[:CHUNK] + dkbe * c["egb"]
            c1 = _rsum(dkbe * c["kb"] + dqdh * qh) * c["egb"]
            c2 = _rsum(dkdh * kh) * c["ekb"]
            e = (da * c["pm"] + dattn * c["qm"]) * c["decay"]
            dgc = c1 - c2 + _rsum(e) - _rsum(e.T)
            dgl_tot = jnp.max(dglv[ci * 8:(ci + 1) * 8, h * HD:(h + 1) * HD], axis=0, keepdims=True) + _csum(c2)
            dgcs.append(dgc + jnp.where(_iota((CHUNK, HD), 0) == CHUNK - 1, dgl_tot, 0.0))
            dqs.append(dpqk[CHUNK:] + dqdh * c["egb"])
            dks.append(dkp + dkdh * c["ekb"] + dkb * bh)
            dvs.append(dvb * bh)
            dbs.append(jnp.broadcast_to(_rsum(dkb * kh + dvb * vh), (CHUNK, HD)))

        def grid2(xs):
            return [xs[ci * DN_HEADS:(ci + 1) * DN_HEADS] for ci in range(CPS)]

        return [_assemble(grid2(dqs)), _assemble(grid2(dks)), _assemble(grid2(dvs)),
                _chunk_scan_rows(_assemble(grid2(dgcs)), suffix=True), _assemble(grid2(dbs))], []

    return rowwise(name, fn, length, CHUNK * CPS,
                   rows=[q, k, v, gb, bb, du, dw, dqd, dkd, t3, dattn3, (dgl, DN_WIDTH, 0, 8 * CPS)],
                   out_rows=[(DN_WIDTH, F32)] * 5)


def delta_scan_fwd(name, qd, kd, u, w, attn3, gcb):
    length = qd.shape[0]
    n = length // CHUNK
    row = pl.BlockSpec((CHUNK, DN_WIDTH), lambda c: (c, 0))
    sq = pl.BlockSpec((DN_HEADS, CHUNK, CHUNK), lambda c: (0, c, 0))

    def body(qd_ref, kd_ref, u_ref, w_ref, attn_ref, gc_ref, o_ref, vn_ref, st_ref, s_ref):
        c = pl.program_id(0)

        @pl.when(c == 0)
        def _():
            s_ref[...] = jnp.zeros_like(s_ref)

        for h in range(DN_HEADS):
            sl = pl.ds(h * HD, HD)
            s = s_ref[h]
            st_ref[0, h] = s
            vn = u_ref[:, sl] - dnn(w_ref[:, sl], s)
            o_ref[:, sl] = dnn(qd_ref[:, sl], s) + dnn(attn_ref[h], vn)
            vn_ref[:, sl] = vn
            egl = jnp.exp(gc_ref[pl.ds(CHUNK - 1, 1), sl])
            s_ref[h] = s * egl + dtn(kd_ref[:, sl], vn)

    return pl.pallas_call(
        body, name=name, grid=(n,), in_specs=[row, row, row, row, sq, row],
        out_specs=[row, row, pl.BlockSpec((1, DN_HEADS, HD, HD), lambda c: (c, 0, 0, 0))],
        out_shape=[jax.ShapeDtypeStruct((length, DN_WIDTH), F32), jax.ShapeDtypeStruct((length, DN_WIDTH), F32),
                   jax.ShapeDtypeStruct((n, DN_HEADS, HD, HD), F32)],
        scratch_shapes=[pltpu.VMEM((DN_HEADS, HD, HD), F32)],
        compiler_params=_params(("arbitrary",)),
    )(qd, kd, u, w, attn3, gcb)


def delta_scan_bwd(name, do, qd, kd, w, attn3, vn, st, gcb):
    length = qd.shape[0]
    n = length // CHUNK
    row = pl.BlockSpec((CHUNK, DN_WIDTH), lambda c: (n - 1 - c, 0))
    sq = pl.BlockSpec((DN_HEADS, CHUNK, CHUNK), lambda c: (0, n - 1 - c, 0))
    stb = pl.BlockSpec((1, DN_HEADS, HD, HD), lambda c: (n - 1 - c, 0, 0, 0))
    glb = pl.BlockSpec((8, DN_WIDTH), lambda c: (n - 1 - c, 0))

    def body(do_ref, qd_ref, kd_ref, w_ref, attn_ref, vn_ref, st_ref, gc_ref,
             dqd_ref, dkd_ref, du_ref, dw_ref, dattn_ref, dgl_ref, ds_ref):
        c = pl.program_id(0)

        @pl.when(c == 0)
        def _():
            ds_ref[...] = jnp.zeros_like(ds_ref)

        for h in range(DN_HEADS):
            sl = pl.ds(h * HD, HD)
            s = st_ref[0, h]
            dsn = ds_ref[h]
            d_o = do_ref[:, sl]
            vnh = vn_ref[:, sl]
            egl = jnp.exp(gc_ref[pl.ds(CHUNK - 1, 1), sl])
            dattn_ref[h] = dnt(d_o, vnh)
            dvn = dtn(attn_ref[h], d_o) + dnn(kd_ref[:, sl], dsn)
            dqd_ref[:, sl] = dnt(d_o, s)
            dkd_ref[:, sl] = dnt(vnh, dsn)
            du_ref[:, sl] = dvn
            dw_ref[:, sl] = -dnt(dvn, s)
            dgl_ref[:, sl] = jnp.broadcast_to(_csum(_rsum(dsn * s)) * egl, (8, HD))
            ds_ref[h] = dsn * egl + dtn(qd_ref[:, sl], d_o) - dtn(w_ref[:, sl], dvn)

    return pl.pallas_call(
        body, name=name, grid=(n,), in_specs=[row, row, row, row, sq, row, stb, row],
        out_specs=[row, row, row, row, sq, glb],
        out_shape=[jax.ShapeDtypeStruct((length, DN_WIDTH), F32)] * 4
        + [jax.ShapeDtypeStruct((DN_HEADS, length, CHUNK), F32), jax.ShapeDtypeStruct((n * 8, DN_WIDTH), F32)],
        scratch_shapes=[pltpu.VMEM((DN_HEADS, HD, HD), F32)],
        compiler_params=_params(("arbitrary",)),
    )(do, qd, kd, w, attn3, vn, st, gcb)


def onorm_fwd(name, o, z, nw, tl=256):
    length = o.shape[0]

    def fn(ctx, rows, consts, prevs, nexts):
        outs = []
        for oh, zh in zip(_heads(rows[0], DN_HEADS, HD), _heads(rows[1], DN_HEADS, HD)):
            r = lax.rsqrt(jnp.mean(oh * oh, axis=1, keepdims=True) + RMS_EPS)
            outs.append(oh * r * consts[0] * _silu(zh))
        return [_cat(outs)], []

    return rowwise(name, fn, length, min(tl, length), rows=[o, z], consts=[nw], out_rows=[(DN_WIDTH, BF16)])[0]


def onorm_bwd(name, o, z, d_on, nw, tl=256):
    length = o.shape[0]

    def fn(ctx, rows, consts, prevs, nexts):
        dos, dzs = [], []
        dnw = jnp.zeros((1, HD), F32)
        for oh, zh, dh in zip(*[_heads(r, DN_HEADS, HD) for r in rows]):
            r = lax.rsqrt(jnp.mean(oh * oh, axis=1, keepdims=True) + RMS_EPS)
            y = oh * r
            sz = _silu(zh)
            t = dh * sz * consts[0]
            dos.append(r * (t - y * jnp.mean(t * y, axis=1, keepdims=True)))
            dzs.append(dh * y * consts[0] * _dsilu(zh))
            dnw = dnw + _csum(dh * y * sz)
        return [_cat(dos), _cat(dzs)], [dnw]

    return rowwise(name, fn, length, min(tl, length), rows=[o, z, d_on], consts=[nw],
                   out_rows=[(DN_WIDTH, F32), (DN_WIDTH, BF16)], out_accs=[((1, HD), F32)])


def merge_fwd(name, gates, ydn, ypool, tl=256):
    length = ydn.shape[0]

    def fn(ctx, rows, consts, prevs, nexts):
        gt = rows[0]
        return [_sigmoid(gt[:, :D_MODEL]) * rows[1] + _sigmoid(gt[:, D_MODEL:]) * rows[2]], []

    return rowwise(name, fn, length, min(tl, length), rows=[gates, ydn, ypool], out_rows=[(D_MODEL, BF16)])[0]


def merge_bwd(name, gates, ydn, ypool, dm, tl=256):
    length = ydn.shape[0]

    def fn(ctx, rows, consts, prevs, nexts):
        gt, yd, yp, d = rows
        sd, sp = _sigmoid(gt[:, :D_MODEL]), _sigmoid(gt[:, D_MODEL:])
        dgates = _cat([d * yd * sd * (1.0 - sd), d * yp * sp * (1.0 - sp)])
        return [d * sd, d * sp, dgates], []

    return rowwise(name, fn, length, min(tl, length), rows=[gates, ydn, ypool, dm],
                   out_rows=[(D_MODEL, BF16), (D_MODEL, BF16), (2 * D_MODEL, BF16)])


def _trailing_sums(ext, upto):
    s, sh = ext, 1
    while sh < upto:
        s = s + pltpu.roll(s, sh, 0)
        sh *= 2
    return s


def _leading_sums(ext, upto, n):
    s, sh = ext, 1
    while sh < upto:
        s = s + pltpu.roll(s, n - sh, 0)
        sh *= 2
    return s


def _pool_mixed(ctx, p, prev, tl):
    prevm = jnp.where(ctx.i > 0, prev, 0.0)
    t1 = (_row_index(ctx, tl) + 1).astype(F32)
    outs = []
    for gi, win in enumerate(POOL_WINDOWS):
        sl = slice(gi * HD, (gi + 1) * HD)
        ext = jnp.concatenate([prevm[:, sl], p[:, sl]], axis=0)
        mean = _trailing_sums(ext, win)[HALO:] / jnp.minimum(t1, float(win))
        outs.append(mean - p[:, sl])
    return outs


def pool_fwd(name, p, pool_w, scale, tl=256):
    length = p.shape[0]
    tl = min(tl, length)

    def fn(ctx, rows, consts, prevs, nexts):
        mixed = _pool_mixed(ctx, rows[0], prevs[0], tl)
        y = _cat([dnn(m, consts[0][gi]) for gi, m in enumerate(mixed)])
        return [y * consts[1]], []

    return rowwise(name, fn, length, tl, rows=[p], consts=[pool_w, scale], prevs=[p],
                   out_rows=[(POOL_WIDTH, BF16)])[0]


def pool_bwd(name, p, dpo, pool_w, scale, tl=256):
    length = p.shape[0]
    tl = min(tl, length)
    n = tl + HALO

    def fn(ctx, rows, consts, prevs, nexts):
        last = ctx.i == ctx.nblk - 1
        mixed = _pool_mixed(ctx, rows[0], prevs[0], tl)
        dext = jnp.concatenate([rows[1], jnp.where(last, 0.0, nexts[0])], axis=0)
        t1 = (_row_index(ctx, n) + 1).astype(F32)
        dps, dws, dscs = [], [], []
        for gi, win in enumerate(POOL_WINDOWS):
            sl = slice(gi * HD, (gi + 1) * HD)
            wg = consts[0][gi]
            dyraw = dext[:, sl] * consts[1][:, sl]
            dmix = dnt(dyraw, wg)
            dws.append(dtn(mixed[gi], dyraw[:tl]))
            dscs.append(_csum(rows[1][:, sl] * dnn(mixed[gi], wg)))
            lead = _leading_sums(dmix / jnp.minimum(t1, float(win)), win, n)
            dps.append(lead[:tl] - dmix[:tl])
        return [_cat(dps)], [jnp.stack(dws), _cat(dscs)]

    return rowwise(name, fn, length, tl, rows=[p, dpo], consts=[pool_w, scale], prevs=[p], nexts=[dpo],
                   out_rows=[(POOL_WIDTH, BF16)],
                   out_accs=[((len(POOL_WINDOWS), HD, HD), F32), ((1, POOL_WIDTH), F32)])


def _xa_probs(qh, kh):
    s = dnt(qh, kh) * (XA_HD ** -0.5)
    e = jnp.exp(s - jnp.max(s, axis=1, keepdims=True))
    return e / _rsum(e)


def xattn_fwd(name, qx, kx, vx, tl=256):
    length = qx.shape[0]

    def fn(ctx, rows, consts, prevs, nexts):
        outs = [dnn(_xa_probs(qh, kh), vh) for qh, kh, vh in
                zip(_heads(rows[0], XA_HEADS, XA_HD), _heads(consts[0], XA_HEADS, XA_HD),
                    _heads(consts[1], XA_HEADS, XA_HD))]
        return [_cat(outs)], []

    return rowwise(name, fn, length, min(tl, length), rows=[qx], consts=[kx, vx], out_rows=[(D_MODEL, BF16)])[0]


def xattn_bwd(name, qx, dox, kx, vx, tl=256):
    length = qx.shape[0]

    def fn(ctx, rows, consts, prevs, nexts):
        dqs, dks, dvs = [], [], []
        for qh, dh, kh, vh in zip(_heads(rows[0], XA_HEADS, XA_HD), _heads(rows[1], XA_HEADS, XA_HD),
                                  _heads(consts[0], XA_HEADS, XA_HD), _heads(consts[1], XA_HEADS, XA_HD)):
            pr = _xa_probs(qh, kh)
            dpr = dnt(dh, vh)
            ds = pr * (dpr - _rsum(dpr * pr)) * (XA_HD ** -0.5)
            dqs.append(dnn(ds, kh))
            dks.append(dtn(ds, qh))
            dvs.append(dtn(pr, dh))
        return [_cat(dqs)], [_cat(dks), _cat(dvs)]

    return rowwise(name, fn, length, min(tl, length), rows=[qx, dox], consts=[kx, vx],
                   out_rows=[(D_MODEL, BF16)], out_accs=[((N_MEM, D_MODEL), F32)] * 2)


def local_step(x, mem, target, w, io):
    sel, pick = _gate_consts()
    alog = jnp.pad(w["a_log"], ((0, 0), (0, 128 - DN_HEADS)))
    dtb = jnp.pad(w["dt_bias"], ((0, 0), (0, 128 - DN_HEADS)))

    f1, res1 = ffn_fwd("ffn1", x, w["ffn1_w_gate"], w["ffn1_w_up"], w["ffn1_w_down"], deps=io.rest_started())
    x1, r1 = ln_fwd("ln1", [(ALPHA, x), (0.5, f1)], w["ln1_g"], w["ln1_b"], deps=io.rest_halfway(f1))
    w = dict(w, **io.rest_weights(x1))
    taps = [w["conv_w"][j:j + 1] for j in range(4)]

    pre = mm("in_qkv", x1, w["in_qkv"], tb=True)
    z = mm("in_z", x1, w["in_z"], tb=True)
    gates = mm("in_gates", x1, w["in_gates"], tb=True)
    p = mm("in_p", x1, w["in_p"], tb=True)
    ab = mm("in_ab", x1, w["in_ab"], tb=True)
    q, k, v = conv_fwd("conv", pre, taps)
    gb, bb = gates_fwd("gates", ab, alog, dtb, sel)
    u, wd_, qd, kd, gcb, attn3, t3 = delta_prep_fwd("dprep", q, k, v, gb, bb)
    o, vn, st = delta_scan_fwd("dscan", qd, kd, u, wd_, attn3, gcb)
    on = onorm_fwd("onorm", o, z, w["dn_norm_w"])
    ydn = mm("dn_branch", on, w["w_dn_branch"], tb=True)
    po = pool_fwd("pool", p, w["pool_w"], w["pool_scale"])
    ypool = mm("pool_branch", po, w["w_pool_branch"], tb=True)
    merged = merge_fwd("merge", gates, ydn, ypool)
    mix = mm("mix_out", merged, w["w_mix_out"])
    x2, r2 = ln_fwd("ln2", [(ALPHA, x1), (1.0, mix)], w["ln2_g"], w["ln2_b"])

    m, _ = ln_fwd("ln_mem", [(1.0, mem)], w["mem_ln_g"], w["mem_ln_b"])
    qx = mm("xa_q", x2, w["xa_wq"])
    kx = mm("xa_k", m, w["xa_wk"])
    vx = mm("xa_v", m, w["xa_wv"])
    ox = xattn_fwd("xattn", qx, kx, vx)
    xa = mm("xa_o", ox, w["xa_wo"])
    x3, r3 = ln_fwd("ln3", [(ALPHA, x2), (1.0, xa)], w["ln3_g"], w["ln3_b"])

    f2, res2 = ffn_fwd("ffn2", x3, w["ffn2_w_gate"], w["ffn2_w_up"], w["ffn2_w_down"])
    dy4, r4, loss = ln_loss("ln4_loss", [(ALPHA, x3), (0.5, f2)], w["ln4_g"], w["ln4_b"], target)

    g = {}
    dr4, g["ln4_g"], g["ln4_b"] = ln_bwd("ln4_b", r4, [(1.0, dy4)], w["ln4_g"])
    dx3, g["ffn2_w_gate"], g["ffn2_w_up"], g["ffn2_w_down"] = ffn_bwd(
        "ffn2b", x3, res2, dr4, w["ffn2_w_gate"], w["ffn2_w_up"], w["ffn2_w_down"])
    dep = io.grads_out(0, g)
    dr3, g["ln3_g"], g["ln3_b"] = ln_bwd("ln3_b", r3, [(ALPHA, dr4), (1.0, dx3)], w["ln3_g"], deps=dep)

    dox = mm("xa_do", dr3, w["xa_wo"], tb=True)
    g["xa_wo"] = mm("xa_dwo", ox, dr3, ta=True)
    dqx, dkx, dvx = xattn_bwd("xattn_b", qx, dox, kx, vx)
    g["xa_wq"] = mm("xa_dwq", x2, dqx, ta=True)
    dx2 = mm("xa_dx", dqx, w["xa_wq"], tb=True)
    g["xa_wk"] = mm("xa_dwk", m, dkx, ta=True)
    g["xa_wv"] = mm("xa_dwv", m, dvx, ta=True)
    dmm = mm("xa_dmk", dkx, w["xa_wk"], tb=True)
    dmm = mm("xa_dmv", dvx, w["xa_wv"], tb=True, add=dmm)
    _, g["mem_ln_g"], g["mem_ln_b"] = ln_bwd("ln_mem_b", mem, [(1.0, dmm)], w["mem_ln_g"])
    dr2, g["ln2_g"], g["ln2_b"] = ln_bwd("ln2_b", r2, [(ALPHA, dr3), (1.0, dx2)], w["ln2_g"])
    io.grads_in(0, dr2)

    dmerged = mm("mix_dm", dr2, w["w_mix_out"], tb=True)
    g["w_mix_out"] = mm("mix_dw", merged, dr2, ta=True)
    d_ydn, d_ypool, d_gates = merge_bwd("merge_b", gates, ydn, ypool, dmerged)
    g["w_dn_branch"] = mm("dn_dw", d_ydn, on, ta=True)
    d_on = mm("dn_dx", d_ydn, w["w_dn_branch"])
    g["w_pool_branch"] = mm("pool_dw", d_ypool, po, ta=True)
    d_po = mm("pool_dx", d_ypool, w["w_pool_branch"])
    dp, g["pool_w"], g["pool_scale"] = pool_bwd("pool_b", p, d_po, w["pool_w"], w["pool_scale"])
    d_o, dz, g["dn_norm_w"] = onorm_bwd("onorm_b", o, z, d_on, w["dn_norm_w"])
    dqd, dkd, du, dw_, dattn3, dgl = delta_scan_bwd("dscan_b", d_o, qd, kd, wd_, attn3, vn, st, gcb)
    dq, dk, dv, dgb, dbb = delta_prep_bwd("dprep_b", q, k, v, gb, bb, t3, du, dw_, dqd, dkd, dattn3, dgl)
    dpre, dc0, dc1, dc2, dc3 = conv_bwd("conv_b", pre, dq, dk, dv, taps)
    g["conv_w"] = jnp.concatenate([dc0, dc1, dc2, dc3], axis=0)
    d_ab, dalog, ddtb = gates_bwd("gates_b", ab, dgb, dbb, alog, dtb, pick)
    g["a_log"] = dalog[:, :DN_HEADS]
    g["dt_bias"] = ddtb[:, :DN_HEADS]
    g["in_qkv"] = mm("in_dwqkv", dpre, x1, ta=True)
    g["in_z"] = mm("in_dwz", dz, x1, ta=True)
    g["in_gates"] = mm("in_dwgates", d_gates, x1, ta=True)
    g["in_p"] = mm("in_dwp", dp, x1, ta=True)
    g["in_ab"] = mm("in_dwab", d_ab, x1, ta=True)
    dx1 = mm("in_dxqkv", dpre, w["in_qkv"], deps=io.grads_out(1, g))
    dx1 = mm("in_dxz", dz, w["in_z"], add=dx1)
    dx1 = mm("in_dxgates", d_gates, w["in_gates"], add=dx1)
    dx1 = mm("in_dxp", dp, w["in_p"], add=dx1)
    dx1 = mm("in_dxab", d_ab, w["in_ab"], add=dx1)
    dr1, g["ln1_g"], g["ln1_b"] = ln_bwd("ln1_b", r1, [(ALPHA, dr2), (1.0, dx1)], w["ln1_g"])
    dx0, g["ffn1_w_gate"], g["ffn1_w_up"], g["ffn1_w_down"] = ffn_bwd(
        "ffn1b", x, res1, dr1, w["ffn1_w_gate"], w["ffn1_w_up"], w["ffn1_w_down"])
    io.grads_in(1, dx0)
    grad_x = axpy("grad_x", [(ALPHA, dr1), (1.0, dx0)])
    return loss, grad_x, g


WEIGHT_NAMES = ['ffn1_w_gate', 'ffn1_w_up', 'ffn1_w_down', 'ln1_g', 'ln1_b', 'w_in', 'conv_w', 'a_log', 'dt_bias',
                'dn_norm_w', 'w_dn_branch', 'pool_w', 'pool_scale', 'w_pool_branch', 'w_mix_out', 'ln2_g', 'ln2_b',
                'mem_ln_g', 'mem_ln_b', 'xa_wq', 'xa_wk', 'xa_wv', 'xa_wo', 'ln3_g', 'ln3_b', 'ffn2_w_gate',
                'ffn2_w_up', 'ffn2_w_down', 'ln4_g', 'ln4_b']
SHARDED = [
    ("ffn1_w_gate", "cols", (1024, 352)), ("ffn1_w_up", "cols", (1024, 352)), ("ffn1_w_down", "rows", (352, 1024)),
    ("w_in", "cols", (1024, 577)), ("conv_w", "flat", (4, 192)), ("w_dn_branch", "cols", (512, 128)),
    ("w_pool_branch", "cols", (512, 128)), ("w_mix_out", "rows", (128, 1024)), ("xa_wq", "rows", (128, 1024)),
    ("xa_wk", "rows", (128, 1024)), ("xa_wv", "rows", (128, 1024)), ("xa_wo", "rows", (128, 1024)),
    ("ffn2_w_gate", "cols", (1024, 352)), ("ffn2_w_up", "cols", (1024, 352)), ("ffn2_w_down", "rows", (352, 1024)),
]
REPLICATED = [n for n in WEIGHT_NAMES if n not in {s[0] for s in SHARDED}]
ROW_ALIGN = 16
GROUP_ALIGN = 96
GROUPS = {"ffn1": ("ffn1_w_gate", "ffn1_w_up", "ffn1_w_down"),
          "mid": ("w_in", "conv_w", "w_dn_branch", "w_pool_branch", "w_mix_out", "xa_wq", "xa_wk", "xa_wv", "xa_wo"),
          "ffn2": ("ffn2_w_gate", "ffn2_w_up", "ffn2_w_down")}
W_IN_COLS = 577
W_IN_PIECES = (("in_qkv", 0, 1536), ("in_z", 1536, 2048), ("in_ab", 2048, 2056), ("in_p", 2056, 2568),
               ("in_gates", 2568, 4616))


def _round_up(n, m):
    return -(-n // m) * m


def _layout():
    off, table = 0, {}
    for name, form, shape in SHARDED:
        valid = {"rows": shape[0], "cols": shape[1], "flat": 2}[form]
        width = {"rows": shape[1], "cols": shape[0], "flat": shape[0] * shape[1]}[form]
        rows = _round_up(valid, ROW_ALIGN)
        table[name] = (off, rows, valid, width, form, shape)
        off += rows
    return table


LAYOUT = _layout()


def _group_span(names):
    base = LAYOUT[names[0]][0]
    return base, _round_up(LAYOUT[names[-1]][0] + LAYOUT[names[-1]][1] - base, GROUP_ALIGN)


def _row_block(rows):
    return _pick(rows, (512, 384, 352, 256, 192, 176, 128, 96, 64, 32, 16))


def _pad_block(blk, rows):
    return jnp.pad(blk, ((0, rows - blk.shape[0]), (0, LANES - blk.shape[1])))


def pack_weight_shards(shards, names):
    parts, used = [], 0
    for name in names:
        off, rows, valid, width, form, _ = LAYOUT[name]
        s = shards[name]
        if form == "flat":
            flat = s.reshape(1, -1)
            hi = flat.astype(BF16)
            blk = jnp.concatenate([hi, (flat - hi.astype(F32)).astype(BF16)], axis=0)
        else:
            blk = (s.T if form == "cols" else s).astype(BF16)
        parts.append(_pad_block(blk, rows))
        used += rows
    if _group_span(names)[1] > used:
        parts.append(jnp.zeros((_group_span(names)[1] - used, LANES), BF16))
    return jnp.concatenate(parts, axis=0)


def _w_in_rows(padded, rows, first, last):
    segs = []
    for k in range(N_DEV):
        lo, hi = max(first, k * W_IN_COLS), min(last, (k + 1) * W_IN_COLS)
        if lo < hi:
            segs.append(padded[k * rows + lo - k * W_IN_COLS:k * rows + hi - k * W_IN_COLS])
    return segs[0] if len(segs) == 1 else jnp.concatenate(segs, axis=0)


def unpack_full_weights(gathered, names):
    out, base = {}, _group_span(names)[0]
    for name in names:
        off, rows, valid, width, form, shape = LAYOUT[name]
        seg = gathered[:, off - base:off - base + rows]
        if form == "flat":
            flat = seg[:, 0, :width].astype(F32) + seg[:, 1, :width].astype(F32)
            out[name] = flat.reshape((N_DEV,) + shape).transpose(1, 0, 2).reshape(shape[0], N_DEV * shape[1])
        elif name == "w_in":
            padded = seg.reshape(N_DEV * rows, LANES)
            for piece, first, last in W_IN_PIECES:
                out[piece] = _w_in_rows(padded, rows, first, last)
        else:
            out[name] = seg[:, :valid, :width].reshape(N_DEV * valid, width)
    return out


def pack_full_grads(grads, names):
    parts, used = [], 0
    for name in names:
        off, rows, valid, width, form, shape = LAYOUT[name]
        if form == "flat":
            full = grads[name].reshape(shape[0], N_DEV, shape[1]).transpose(1, 0, 2).reshape(N_DEV, 1, width)
        elif name == "w_in":
            full = jnp.concatenate([grads[piece][:last - first] for piece, first, last in W_IN_PIECES], axis=0)
            full = full.reshape(N_DEV, valid, width)
        else:
            full = grads[name].reshape(N_DEV, valid, width)
        parts.append(jnp.pad(full, ((0, 0), (0, rows - full.shape[1]), (0, LANES - width))))
        used += rows
    if _group_span(names)[1] > used:
        parts.append(jnp.zeros((N_DEV, _group_span(names)[1] - used, LANES), F32))
    return jnp.concatenate(parts, axis=1)


def unpack_grad_shards(packed, names):
    out, base = {}, _group_span(names)[0]
    for name in names:
        off, rows, valid, width, form, shape = LAYOUT[name]
        off -= base
        if form == "flat":
            out[name] = packed[off, :width].reshape(shape)
        elif form == "cols":
            out[name] = packed[off:off + valid, :width].T
        else:
            out[name] = packed[off:off + valid, :width]
    return out


SMALL_SHAPES = {n: (1024,) for n in REPLICATED}
SMALL_SHAPES.update(pool_w=(4, 128, 128), pool_scale=(512,), dn_norm_w=(128,), a_log=(4,), dt_bias=(4,))


SMALL_SHAPES["loss"] = (1,)
SMALL_NAMES = REPLICATED + ["loss"]


def _small_layout():
    off, table = 0, {}
    for name in SMALL_NAMES:
        numel = 1
        for d in SMALL_SHAPES[name]:
            numel *= d
        rows = _round_up(-(-numel // LANES), 8)
        table[name] = (off, rows, numel)
        off += rows
    return table, off


SMALL_LAYOUT, SMALL_ROWS = _small_layout()


def _to_rows(flat, rows):
    return jnp.pad(flat, (0, rows * LANES - flat.shape[0])).reshape(rows, LANES)


def pack_small(values):
    return jnp.concatenate([_to_rows(values[name].reshape(-1), SMALL_LAYOUT[name][1]) for name in SMALL_NAMES], axis=0)


def unpack_small(packed):
    out = {}
    for name in SMALL_NAMES:
        off, rows, numel = SMALL_LAYOUT[name]
        out[name] = packed[off:off + rows].reshape(-1)[:numel].reshape(SMALL_SHAPES[name])
    return out


MESH = pl.DeviceIdType.MESH


def _position():
    return lax.axis_index("x"), lax.axis_index("y"), lax.axis_index("c")


def _other_chips(x, y):
    return [(1 - x, y), (x, 1 - y), (1 - x, 1 - y)]


def all_gather(name, block):
    rows, n = block.shape

    def body(x_ref, out_ref, send_sems, recv_sems, local_sem):
        x, y, c = _position()
        me, sibling = (x, y, c), (x, y, 1 - c)
        chips = _other_chips(x, y)

        def slot(px, py, pc):
            return out_ref.at[4 * px + 2 * py + pc]

        def copy(k, blk, to, src=None):
            return pltpu.make_async_remote_copy(
                src_ref=slot(*blk) if src is None else src, dst_ref=slot(*blk),
                send_sem=send_sems.at[k], recv_sem=recv_sems.at[k], device_id=to, device_id_type=MESH)

        mine = pltpu.make_async_copy(x_ref, slot(*me), local_sem)
        mine.start()
        first = [copy(0, me, sibling, src=x_ref)]
        first += [copy(1 + j, me, (*chip, c), src=x_ref) for j, chip in enumerate(chips)]
        for cp in first:
            cp.start()
        passed = [copy(4 + j, (*chip, c), sibling) for j, chip in enumerate(chips)]
        for j, chip in enumerate(chips):
            copy(1 + j, (*chip, c), me).wait_recv()
            passed[j].start()
        copy(0, sibling, me).wait_recv()
        for j, chip in enumerate(chips):
            copy(4 + j, (*chip, 1 - c), me).wait_recv()
        for cp in first + passed:
            cp.wait_send()
        mine.wait()

    return pl.pallas_call(
        body, name=name, out_shape=jax.ShapeDtypeStruct((N_DEV, rows, n), block.dtype),
        in_specs=[ANY], out_specs=ANY,
        scratch_shapes=[pltpu.SemaphoreType.DMA((7,)), pltpu.SemaphoreType.DMA((7,)), pltpu.SemaphoreType.DMA(())],
    )(block)


def sibling_exchange(name, send):
    def body(s_ref, r_ref, send_sem, recv_sem):
        x, y, c = _position()
        cp = pltpu.make_async_remote_copy(src_ref=s_ref, dst_ref=r_ref, send_sem=send_sem, recv_sem=recv_sem,
                                          device_id=(x, y, 1 - c), device_id_type=MESH)
        cp.start()
        cp.wait()

    return pl.pallas_call(
        body, name=name, out_shape=jax.ShapeDtypeStruct(send.shape, send.dtype), in_specs=[ANY], out_specs=ANY,
        scratch_shapes=[pltpu.SemaphoreType.DMA(()), pltpu.SemaphoreType.DMA(())],
    )(send)


def chip_exchange(name, send):
    def body(s_ref, r_ref, send_sems, recv_sems):
        x, y, c = _position()
        cps = [pltpu.make_async_remote_copy(src_ref=s_ref.at[j], dst_ref=r_ref.at[j], send_sem=send_sems.at[j],
                                            recv_sem=recv_sems.at[j], device_id=(*chip, c), device_id_type=MESH)
               for j, chip in enumerate(_other_chips(x, y))]
        for cp in cps:
            cp.start()
        for cp in cps:
            cp.wait()

    return pl.pallas_call(
        body, name=name, out_shape=jax.ShapeDtypeStruct(send.shape, send.dtype), in_specs=[ANY], out_specs=ANY,
        scratch_shapes=[pltpu.SemaphoreType.DMA((3,)), pltpu.SemaphoreType.DMA((3,))],
    )(send)


HBM = pl.BlockSpec(memory_space=pltpu.HBM)
SEM = pl.BlockSpec(memory_space=pltpu.SEMAPHORE)
EFFECT = pltpu.SideEffectType.DATAFLOW_SIDE_EFFECTING


def _remote(src, dst, send_sem, recv_sem, to):
    return pltpu.make_async_remote_copy(src_ref=src, dst_ref=dst, send_sem=send_sem, recv_sem=recv_sem,
                                        device_id=to, device_id_type=MESH)


def split_start(name, bufs, n, make_copies):
    nb = len(bufs)

    def body(*refs):
        for out_cp, _ in make_copies(refs[:nb], refs[nb:nb + n], refs[nb + n:nb + 2 * n]):
            out_cp.start()
        refs[-1][...] = jnp.zeros_like(refs[-1])

    outs = pl.pallas_call(
        body, name=name,
        out_shape=tuple([pltpu.SemaphoreType.DMA(())] * (2 * n)) + tuple(pltpu.HBM(b.shape, b.dtype) for b in bufs)
        + (jax.ShapeDtypeStruct((8, 128), F32),),
        in_specs=[HBM] * nb,
        out_specs=tuple([SEM] * (2 * n) + [HBM] * nb + [pl.BlockSpec(memory_space=pltpu.VMEM)]),
        input_output_aliases={i: 2 * n + i for i in range(nb)},
        compiler_params=pltpu.CompilerParams(has_side_effects=EFFECT),
    )(*[pltpu.with_memory_space_constraint(b, pltpu.HBM) for b in bufs])
    return list(outs[:2 * n]), list(outs[2 * n:2 * n + nb]), outs[-1]


def split_wait(name, bufs, sems, n, make_copies, after):
    nb = len(bufs)

    def body(*refs):
        for out_cp, in_cp in make_copies(refs[:nb], refs[nb:nb + n], refs[nb + n:nb + 2 * n]):
            out_cp.wait_send()
            in_cp.wait_recv()

    outs = pl.pallas_call(
        body, name=name, out_shape=tuple(pltpu.HBM(b.shape, b.dtype) for b in bufs),
        in_specs=[HBM] * nb + [SEM] * (2 * n) + [ANY], out_specs=tuple([HBM] * nb),
        input_output_aliases={i: i for i in range(nb)},
        compiler_params=pltpu.CompilerParams(has_side_effects=EFFECT),
    )(*bufs, *sems, after)
    return list(outs)


def _gather_stage1(refs, send, recv):
    src, land = refs
    x, y, c = _position()
    peers = [(x, y, 1 - c)] + [(*chip, c) for chip in _other_chips(x, y)]
    return [(_remote(src, land.at[4 * x + 2 * y + c], send[k], recv[k], p),
             _remote(src, land.at[4 * p[0] + 2 * p[1] + p[2]], send[k], recv[k], p)) for k, p in enumerate(peers)]


def _gather_stage2(refs, send, recv):
    (land,) = refs
    x, y, c = _position()
    out = []
    for j, (px, py) in enumerate(_other_chips(x, y)):
        mine, theirs = land.at[4 * px + 2 * py + c], land.at[4 * px + 2 * py + 1 - c]
        out.append((_remote(mine, mine, send[j], recv[j], (x, y, 1 - c)),
                    _remote(theirs, theirs, send[j], recv[j], (x, y, 1 - c))))
    return out


def _scatter_direct(refs, send, recv):
    sendbuf, land = refs
    x, y, c = _position()
    me = 4 * x + 2 * y + c
    out = []
    for k, (fx, fy, fc) in enumerate([(a, b, d) for a in (0, 1) for b in (0, 1) for d in (0, 1) if a | b | d]):
        p = (1 - x if fx else x, 1 - y if fy else y, 1 - c if fc else c)
        peer = 4 * p[0] + 2 * p[1] + p[2]
        out.append((_remote(sendbuf.at[peer], land.at[me], send[k], recv[k], p),
                    _remote(sendbuf.at[peer], land.at[peer], send[k], recv[k], p)))
    return out


def _slot_sum(name, table, first, count, packed, received, out_dtype):
    rows = packed.shape[1]
    blk = (1, _row_block(rows), LANES)
    with_recv = received is not None

    def body(tbl_ref, *refs):
        if with_recv:
            g_ref, r_ref, o_ref = refs
            o_ref[...] = (g_ref[...] + r_ref[...].astype(F32)).astype(o_ref.dtype)
        else:
            g_ref, o_ref = refs
            o_ref[...] = g_ref[...].astype(o_ref.dtype)

    in_specs = [pl.BlockSpec(blk, lambda r, i, tbl: (tbl[first + r], i, 0))]
    ins = [packed]
    if with_recv:
        in_specs.append(pl.BlockSpec(blk, lambda r, i, tbl: (first + r, i, 0)))
        ins.append(received)
    return pl.pallas_call(
        body, name=name,
        grid_spec=pltpu.PrefetchScalarGridSpec(
            num_scalar_prefetch=1, grid=(count, rows // blk[1]), in_specs=in_specs,
            out_specs=pl.BlockSpec(blk, lambda r, i, tbl: (r, i, 0))),
        out_shape=jax.ShapeDtypeStruct((count, rows, LANES), out_dtype),
        compiler_params=_params(("parallel", "parallel")),
    )(table, *ins)


def _final_sum(name, own, received):
    rows = own.shape[0]

    def body(h_ref, r_ref, o_ref):
        acc = h_ref[...]
        for j in range(3):
            acc = acc + r_ref[j].astype(F32)
        o_ref[...] = acc

    tr = _row_block(rows)
    return pl.pallas_call(
        body, name=name, grid=(rows // tr,),
        in_specs=[pl.BlockSpec((tr, LANES), lambda i: (i, 0)), pl.BlockSpec((3, tr, LANES), lambda i: (0, i, 0))],
        out_specs=pl.BlockSpec((tr, LANES), lambda i: (i, 0)),
        out_shape=jax.ShapeDtypeStruct((rows, LANES), F32), compiler_params=_params(("parallel",)),
    )(own, received)


def _own_plus_slots(name, me, packed, landed):
    n, rows, _ = landed.shape
    tr = _row_block(rows)

    def body(me_ref, g_ref, l_ref, o_ref):
        acc = g_ref[0]
        for j in range(n):
            acc = acc + l_ref[j].astype(F32)
        o_ref[...] = acc

    return pl.pallas_call(
        body, name=name,
        grid_spec=pltpu.PrefetchScalarGridSpec(
            num_scalar_prefetch=1, grid=(rows // tr,),
            in_specs=[pl.BlockSpec((1, tr, LANES), lambda i, me: (me[0], i, 0)),
                      pl.BlockSpec((n, tr, LANES), lambda i, me: (0, i, 0))],
            out_specs=pl.BlockSpec((tr, LANES), lambda i, me: (i, 0))),
        out_shape=jax.ShapeDtypeStruct((rows, LANES), F32), compiler_params=_params(("parallel",)),
    )(me, packed, landed)


def _sum_slots(name, stack):
    n, rows, _ = stack.shape

    def body(s_ref, o_ref):
        acc = s_ref[0]
        for j in range(1, n):
            acc = acc + s_ref[j]
        o_ref[...] = acc

    return pl.pallas_call(
        body, name=name, in_specs=[pl.BlockSpec(stack.shape, lambda: (0, 0, 0))],
        out_specs=pl.BlockSpec((rows, LANES), lambda: (0, 0)), out_shape=jax.ShapeDtypeStruct((rows, LANES), F32),
    )(stack)


def reduce_scatter(packed):
    x, y, c = _position()
    chips = [(x, y)] + _other_chips(x, y)
    to_me = jnp.stack([4 * px + 2 * py + c for px, py in chips]).astype(jnp.int32)
    to_sibling = jnp.stack([4 * px + 2 * py + (1 - c) for px, py in chips]).astype(jnp.int32)
    send1 = _slot_sum("rs_send1", to_sibling, 0, 4, packed, None, WIRE)
    recv1 = sibling_exchange("rs_sibling", send1)
    own = _slot_sum("rs_own", to_me, 0, 1, packed, recv1, F32)[0]
    send2 = _slot_sum("rs_send2", to_me, 1, 3, packed, recv1, WIRE)
    recv2 = chip_exchange("rs_chips", send2)
    return _final_sum("rs_final", own, recv2)


def adamw(name, w, g, m, v):
    shape = w.shape
    last = shape[-1]
    w2, g2, m2, v2 = [a.reshape(-1, last) for a in (w, g, m, v)]
    rows = w2.shape[0]
    tr = 256 if rows % 256 == 0 else rows

    def body(w_ref, g_ref, m_ref, v_ref, d_ref, nm_ref, nv_ref):
        gg = g_ref[...]
        nm = ADAM_B1 * m_ref[...] + (1.0 - ADAM_B1) * gg
        nv = ADAM_B2 * v_ref[...] + (1.0 - ADAM_B2) * (gg * gg)
        m_hat = nm / (1.0 - ADAM_B1 ** ADAM_STEP)
        v_hat = nv / (1.0 - ADAM_B2 ** ADAM_STEP)
        d_ref[...] = -ADAM_LR * (m_hat / (jnp.sqrt(v_hat) + ADAM_EPS) + ADAM_WD * w_ref[...])
        nm_ref[...] = nm
        nv_ref[...] = nv

    spec = pl.BlockSpec((tr, last), lambda i: (i, 0))
    outs = pl.pallas_call(
        body, name=name, grid=(rows // tr,), in_specs=[spec] * 4, out_specs=[spec] * 3,
        out_shape=[jax.ShapeDtypeStruct((rows, last), F32)] * 3, compiler_params=_params(("parallel",)),
    )(w2, g2, m2, v2)
    return [o.reshape(shape) for o in outs]


def _landing(block_shape, dtype, own):
    x, y, c = _position()
    return lax.dynamic_update_slice(lax.empty((N_DEV,) + block_shape, dtype), own[None], (4 * x + 2 * y + c, 0, 0))


class _Exchanges:
    LATER = GROUPS["mid"] + GROUPS["ffn2"]
    EARLY_GRADS = (GROUPS["ffn2"], GROUPS["mid"])

    def __init__(self, shards):
        self.shards = shards
        self.pending = {}
        self.reduced = {}

    def first_weights(self):
        names = GROUPS["ffn1"]
        return unpack_full_weights(all_gather("ag_ffn1", pack_weight_shards(self.shards, names)), names)

    def rest_started(self):
        block = jnp.concatenate([pack_weight_shards(self.shards, GROUPS["mid"]),
                                 pack_weight_shards(self.shards, GROUPS["ffn2"])], axis=0)
        sems, bufs, token = split_start("ag_rest_s1", [block, _landing(block.shape, block.dtype, block)], 4,
                                        _gather_stage1)
        self.pending["w"] = (sems, bufs)
        return (token,)

    def rest_halfway(self, after):
        sems, bufs = self.pending.pop("w")
        _, land = split_wait("ag_rest_w1", bufs, sems, 4, _gather_stage1, after)
        sems, bufs, token = split_start("ag_rest_s2", [land], 3, _gather_stage2)
        self.pending["w"] = (sems, bufs)
        return (token,)

    def rest_weights(self, after):
        sems, bufs = self.pending.pop("w")
        (gathered,) = split_wait("ag_rest_w2", bufs, sems, 3, _gather_stage2, after)
        mid_rows = _group_span(GROUPS["mid"])[1]
        w = unpack_full_weights(gathered[:, :mid_rows], GROUPS["mid"])
        w.update(unpack_full_weights(gathered[:, mid_rows:], GROUPS["ffn2"]))
        w["in_ab"] = jnp.pad(w["in_ab"], ((0, 128 - 2 * DN_HEADS), (0, 0)))
        return w

    def grads_out(self, k, grads):
        names = self.EARLY_GRADS[k]
        x, y, c = _position()
        packed = pack_full_grads(grads, names)
        wire = _slot_sum(f"rs{k}_wire", jnp.arange(N_DEV, dtype=jnp.int32), 0, N_DEV, packed, None, WIRE)
        land = _landing(wire.shape[1:], WIRE, jnp.zeros(wire.shape[1:], WIRE))
        sems, bufs, token = split_start(f"rs{k}_start", [wire, land], N_DEV - 1, _scatter_direct)
        self.pending[k] = (sems, bufs, packed)
        return (token,)

    def grads_in(self, k, after):
        sems, bufs, packed = self.pending.pop(k)
        x, y, c = _position()
        _, landed = split_wait(f"rs{k}_wait", bufs, sems, N_DEV - 1, _scatter_direct, after)
        me = jnp.reshape(4 * x + 2 * y + c, (1,)).astype(jnp.int32)
        self.reduced.update(unpack_grad_shards(_own_plus_slots(f"rs{k}_sum", me, packed, landed), self.EARLY_GRADS[k]))

    def last_grads(self, grads):
        names = GROUPS["ffn1"]
        self.reduced.update(unpack_grad_shards(reduce_scatter(pack_full_grads(grads, names)), names))


def kernel(x, mem, ffn1_w_gate, ffn1_w_up, ffn1_w_down, ln1_g, ln1_b, w_in, conv_w, a_log, dt_bias, dn_norm_w, w_dn_branch, pool_w, pool_scale, w_pool_branch, w_mix_out, ln2_g, ln2_b, mem_ln_g, mem_ln_b, xa_wq, xa_wk, xa_wv, xa_wo, ln3_g, ln3_b, ffn2_w_gate, ffn2_w_up, ffn2_w_down, ln4_g, ln4_b, loss_target, m_ffn1_w_gate, m_ffn1_w_up, m_ffn1_w_down, m_ln1_g, m_ln1_b, m_w_in, m_conv_w, m_a_log, m_dt_bias, m_dn_norm_w, m_w_dn_branch, m_pool_w, m_pool_scale, m_w_pool_branch, m_w_mix_out, m_ln2_g, m_ln2_b, m_mem_ln_g, m_mem_ln_b, m_xa_wq, m_xa_wk, m_xa_wv, m_xa_wo, m_ln3_g, m_ln3_b, m_ffn2_w_gate, m_ffn2_w_up, m_ffn2_w_down, m_ln4_g, m_ln4_b, v_ffn1_w_gate, v_ffn1_w_up, v_ffn1_w_down, v_ln1_g, v_ln1_b, v_w_in, v_conv_w, v_a_log, v_dt_bias, v_dn_norm_w, v_w_dn_branch, v_pool_w, v_pool_scale, v_w_pool_branch, v_w_mix_out, v_ln2_g, v_ln2_b, v_mem_ln_g, v_mem_ln_b, v_xa_wq, v_xa_wk, v_xa_wv, v_xa_wo, v_ln3_g, v_ln3_b, v_ffn2_w_gate, v_ffn2_w_up, v_ffn2_w_down, v_ln4_g, v_ln4_b):
    given = dict(locals())
    shards = {n: given[n] for n in WEIGHT_NAMES}
    io = _Exchanges({n: shards[n][0] for n, _, _ in SHARDED})
    w = io.first_weights()
    for n in REPLICATED:
        w[n] = shards[n][0] if n == "pool_w" else shards[n]
    loss_part, grad_x, g = local_step(x[0], mem[0], loss_target[0], w, io)

    io.last_grads(g)
    small = unpack_small(_sum_slots("small_sum", all_gather("ag_small", pack_small(dict(g, loss=loss_part[0, :1])))))
    loss = small.pop("loss")[0]
    grad = dict(io.reduced, **small)
    grad = {n: grad[n].reshape(shards[n].shape) for n in WEIGHT_NAMES}
    updates = {n: adamw("adamw_" + n, shards[n], grad[n], given["m_" + n], given["v_" + n]) for n in WEIGHT_NAMES}
    return (loss, grad_x[None], *[grad[n] for n in WEIGHT_NAMES], *[updates[n][0] for n in WEIGHT_NAMES],
            *[updates[n][1] for n in WEIGHT_NAMES], *[updates[n][2] for n in WEIGHT_NAMES])
```

```python
import functools

import jax
import jax.numpy as jnp
from jax import lax
from jax.experimental import pallas as pl
from jax.experimental.pallas import tpu as pltpu

F32 = jnp.float32
BF16 = jnp.bfloat16
MMD = BF16
WIRE = BF16
HI = lax.Precision.HIGHEST
X3 = lax.Precision.HIGH
VMEM_LIMIT_BYTES = 48 * 1024 * 1024

D_MODEL = 1024
D_FF = 2816
CHUNK = 64
N_MEM = 256
DN_HEADS = 4
HD = 128
DN_WIDTH = 512
POOL_WINDOWS = (2, 4, 8, 16)
POOL_WIDTH = 512
XA_HEADS = 4
XA_HD = 256
LN_EPS = 1e-5
RMS_EPS = 1e-6
L2_EPS = 1e-6
ALPHA = 2.0 ** 0.25
HALO = 16

ADAM_LR = 0.001
ADAM_B1 = 0.9
ADAM_B2 = 0.999
ADAM_EPS = 1e-08
ADAM_WD = 0.01
ADAM_STEP = 10

N_DEV = 8
LANES = 1024
ANY = pl.BlockSpec(memory_space=pl.ANY)


def _dot(a, b, ca, cb, prec):
    dn = (((ca,), (cb,)), ((), ()))
    if prec is not None:
        return lax.dot_general(a.astype(F32), b.astype(F32), dn, precision=prec, preferred_element_type=F32)
    return lax.dot_general(a.astype(MMD), b.astype(MMD), dn, preferred_element_type=F32)


def dnn(a, b, prec=None):
    return _dot(a, b, 1, 0, prec)


def dnt(a, b, prec=None):
    return _dot(a, b, 1, 1, prec)


def dtn(a, b, prec=None):
    return _dot(a, b, 0, 0, prec)


def _sigmoid(x):
    return jax.nn.sigmoid(x)


def _silu(x):
    return x * _sigmoid(x)


def _dsilu(x):
    s = _sigmoid(x)
    return s * (1.0 + x * (1.0 - s))


def _softplus(x):
    return jnp.maximum(x, 0.0) + jnp.log1p(jnp.exp(-jnp.abs(x)))


def _iota(shape, dim):
    return lax.broadcasted_iota(jnp.int32, shape, dim)


def _rsum(x):
    return jnp.sum(x, axis=1, keepdims=True)


def _csum(x):
    return jnp.sum(x, axis=0, keepdims=True)


def _pick(n, cands):
    for c in cands:
        if n % c == 0:
            return c
    return n


def _params(sem):
    return pltpu.CompilerParams(dimension_semantics=sem, vmem_limit_bytes=VMEM_LIMIT_BYTES)


def mm(name, a, b, *, ta=False, tb=False, out_dtype=F32, add=None, scale=None, deps=()):
    if ta:
        kc, m = a.shape
    else:
        m, kc = a.shape
    if tb:
        n, kb = b.shape
    else:
        kb, n = b.shape
    assert kc == kb, (name, a.shape, b.shape)
    tm = m if m <= 512 else _pick(m, (512, 256, 128))
    tn = n if n <= 1024 else _pick(n, (1024, 1408, 768, 512))
    tk = kc if kc <= 1024 else _pick(kc, (1024, 1408, 768, 512))
    nk = kc // tk
    grid = (m // tm, n // tn, nk)
    a_spec = pl.BlockSpec((tk, tm), lambda i, j, k: (k, i)) if ta else pl.BlockSpec((tm, tk), lambda i, j, k: (i, k))
    b_spec = pl.BlockSpec((tn, tk), lambda i, j, k: (j, k)) if tb else pl.BlockSpec((tk, tn), lambda i, j, k: (k, j))
    o_spec = pl.BlockSpec((tm, tn), lambda i, j, k: (i, j))
    ca, cb = (0 if ta else 1), (1 if tb else 0)
    has_add = add is not None

    def body(*refs):
        a_ref, b_ref, o_ref, acc_ref = refs[0], refs[1], refs[-2], refs[-1]
        add_ref = refs[2] if has_add else None
        k = pl.program_id(2)

        @pl.when(k == 0)
        def _():
            acc_ref[...] = jnp.zeros_like(acc_ref)

        acc_ref[...] += _dot(a_ref[...], b_ref[...], ca, cb, None)

        @pl.when(k == nk - 1)
        def _():
            r = acc_ref[...]
            if scale is not None:
                r = r * scale
            if has_add:
                r = r + add_ref[...]
            o_ref[...] = r.astype(o_ref.dtype)

    ins = [a, b] + ([add] if has_add else []) + list(deps)
    specs = [a_spec, b_spec] + ([o_spec] if has_add else []) + [ANY] * len(deps)
    return pl.pallas_call(
        body, name=name, grid=grid, in_specs=specs, out_specs=o_spec,
        out_shape=jax.ShapeDtypeStruct((m, n), out_dtype),
        scratch_shapes=[pltpu.VMEM((tm, tn), F32)],
        compiler_params=_params(("parallel", "parallel", "arbitrary")),
    )(*ins)


class _Ctx:
    def __init__(self, i, nblk, tl):
        self.i, self.nblk, self.tl = i, nblk, tl


def _norm_item(it):
    if isinstance(it, tuple):
        a, w, j = it[:3]
        rows = it[3] if len(it) > 3 else None
        return a, w, j, rows
    return it, it.shape[-1], 0, None


def rowwise(name, fn, length, tl, *, rows=(), consts=(), prevs=(), nexts=(), out_rows=(), out_accs=(), deps=()):
    nblk = length // tl
    hb = tl // HALO
    nhalo = length // HALO
    arrays, specs = [], []
    for it in rows:
        a, w, j, r = _norm_item(it)
        if a.ndim == 3:
            specs.append(pl.BlockSpec((a.shape[0], tl, w), lambda i, j=j: (0, i, j)))
        else:
            specs.append(pl.BlockSpec((r or tl, w), lambda i, j=j: (i, j)))
        arrays.append(a)
    for a in consts:
        specs.append(pl.BlockSpec(a.shape, lambda i, nd=a.ndim: (0,) * nd))
        arrays.append(a)
    for it in prevs:
        a, w, j, _ = _norm_item(it)
        specs.append(pl.BlockSpec((HALO, w), lambda i, j=j: (jnp.maximum(i * hb - 1, 0), j)))
        arrays.append(a)
    for it in nexts:
        a, w, j, _ = _norm_item(it)
        specs.append(pl.BlockSpec((HALO, w), lambda i, j=j: (jnp.minimum((i + 1) * hb, nhalo - 1), j)))
        arrays.append(a)
    out_shape, out_specs = [], []
    for spec in out_rows:
        if len(spec) == 3:
            h, w, dt = spec
            out_shape.append(jax.ShapeDtypeStruct((h, length, w), dt))
            out_specs.append(pl.BlockSpec((h, tl, w), lambda i: (0, i, 0)))
        else:
            w, dt = spec
            out_shape.append(jax.ShapeDtypeStruct((length, w), dt))
            out_specs.append(pl.BlockSpec((tl, w), lambda i: (i, 0)))
    for shape, dt in out_accs:
        out_shape.append(jax.ShapeDtypeStruct(shape, dt))
        out_specs.append(pl.BlockSpec(shape, lambda i, nd=len(shape): (0,) * nd))
    n_r, n_c, n_p, n_n = len(rows), len(consts), len(prevs), len(nexts)
    n_in = n_r + n_c + n_p + n_n
    n_or = len(out_rows)
    arrays, specs = arrays + list(deps), specs + [ANY] * len(deps)

    def body(*refs):
        i = pl.program_id(0)
        vals = [r[...] for r in refs[:n_in]]
        outs = refs[n_in + len(deps):]
        ctx = _Ctx(i, nblk, tl)
        ro, ao = fn(ctx, vals[:n_r], vals[n_r:n_r + n_c], vals[n_r + n_c:n_r + n_c + n_p], vals[n_r + n_c + n_p:])
        for r, v in zip(outs[:n_or], ro, strict=True):
            r[...] = v.astype(r.dtype)
        for r, v in zip(outs[n_or:], ao, strict=True):
            @pl.when(i == 0)
            def _(r=r, v=v):
                r[...] = v.astype(r.dtype)

            @pl.when(i > 0)
            def _(r=r, v=v):
                r[...] += v.astype(r.dtype)

    res = pl.pallas_call(
        body, name=name, grid=(nblk,), in_specs=specs, out_specs=out_specs, out_shape=out_shape,
        compiler_params=_params(("arbitrary",) if out_accs else ("parallel",)),
    )(*arrays)
    return res


def _heads(x, n, w):
    return [x[:, h * w:(h + 1) * w] for h in range(n)]


def _cat(xs):
    return jnp.concatenate(xs, axis=1)


def _row_index(ctx, nrows, offset=0):
    return ctx.i * ctx.tl + offset + _iota((nrows, 1), 0)


def _ln_stats(r):
    mu = jnp.mean(r, axis=1, keepdims=True)
    d = r - mu
    var = jnp.mean(d * d, axis=1, keepdims=True)
    rstd = lax.rsqrt(var + LN_EPS)
    return d * rstd, rstd


def ln_fwd(name, terms, g, b, tl=256, deps=()):
    coefs = [c for c, _ in terms]
    length = terms[0][1].shape[0]

    def fn(ctx, rows, consts, prevs, nexts):
        r = sum(c * t for c, t in zip(coefs, rows))
        xh, _ = _ln_stats(r)
        return [xh * consts[0] + consts[1], r], []

    return rowwise(name, fn, length, min(tl, length), rows=[t for _, t in terms], consts=[g, b],
                   out_rows=[(D_MODEL, F32), (D_MODEL, F32)], deps=deps)


def ln_bwd(name, r, terms, g, tl=256, deps=()):
    coefs = [c for c, _ in terms]
    length = r.shape[0]

    def fn(ctx, rows, consts, prevs, nexts):
        xh, rstd = _ln_stats(rows[0])
        dy = sum(c * t for c, t in zip(coefs, rows[1:]))
        dxh = dy * consts[0]
        dr = rstd * (dxh - jnp.mean(dxh, axis=1, keepdims=True) - xh * jnp.mean(dxh * xh, axis=1, keepdims=True))
        return [dr], [_csum(dy * xh), _csum(dy)]

    return rowwise(name, fn, length, min(tl, length), rows=[r] + [t for _, t in terms], consts=[g],
                   out_rows=[(D_MODEL, F32)], out_accs=[((1, D_MODEL), F32), ((1, D_MODEL), F32)], deps=deps)


def ln_loss(name, terms, g, b, target, tl=256):
    coefs = [c for c, _ in terms]
    length = target.shape[0]
    nt = len(terms)

    def fn(ctx, rows, consts, prevs, nexts):
        r = sum(c * t for c, t in zip(coefs, rows[:nt]))
        xh, _ = _ln_stats(r)
        err = xh * consts[0] + consts[1] - rows[nt]
        tot = _csum(_rsum(err * err)) * (0.5 / D_MODEL)
        return [err * (1.0 / D_MODEL), r], [jnp.broadcast_to(tot, (1, 128))]

    return rowwise(name, fn, length, min(tl, length), rows=[t for _, t in terms] + [target], consts=[g, b],
                   out_rows=[(D_MODEL, F32), (D_MODEL, F32)], out_accs=[((1, 128), F32)])


def axpy(name, terms, tl=256):
    coefs = [c for c, _ in terms]
    length, width = terms[0][1].shape

    def fn(ctx, rows, consts, prevs, nexts):
        return [sum(c * t for c, t in zip(coefs, rows))], []

    return rowwise(name, fn, length, min(tl, length), rows=[t for _, t in terms], out_rows=[(width, F32)])[0]


def ffn_fwd(tag, x, wg, wu, wd, deps=()):
    length = x.shape[0]
    hg = mm(tag + "_gate", x, wg, tb=True, deps=deps)
    hu = mm(tag + "_up", x, wu, tb=True)
    if callable(wd):
        wd = wd(hu)

    def fn(ctx, rows, consts, prevs, nexts):
        return [_silu(rows[0]) * rows[1]], []

    act = rowwise(tag + "_act", fn, length, min(256, length), rows=[hg, hu], out_rows=[(D_FF, BF16)])[0]
    f = mm(tag + "_down", act, wd)
    return f, (hg, hu, act), wd


def ffn_bwd(tag, x, res, dr, wg, wu, wd, deps=(), on_dwd=None, on_dwgu=None):
    hg, hu, act = res
    length = x.shape[0]
    dwd = mm(tag + "_dwd", act, dr, ta=True, scale=0.5, deps=deps)
    dact = mm(tag + "_dact", dr, wd, tb=True, scale=0.5, deps=on_dwd(dwd) if on_dwd else ())

    def fn(ctx, rows, consts, prevs, nexts):
        g, u, da = rows
        return [da * u * _dsilu(g), da * _silu(g)], []

    dhg, dhu = rowwise(tag + "_dactb", fn, length, min(256, length), rows=[hg, hu, dact],
                       out_rows=[(D_FF, BF16), (D_FF, BF16)])
    dwg = mm(tag + "_dwg", dhg, x, ta=True)
    dwu = mm(tag + "_dwu", dhu, x, ta=True)
    dx = mm(tag + "_dxg", dhg, wg, deps=on_dwgu(dwg, dwu) if on_dwgu else ())
    dx = mm(tag + "_dxu", dhu, wu, add=dx)
    return dx, dwg, dwu, dwd


def _conv_taps(ext, taps, n):
    out = taps[3] * ext
    for j in range(3):
        out = out + taps[j] * pltpu.roll(ext, 3 - j, 0)
    return out


def _l2n(x):
    r = lax.rsqrt(_rsum(x * x) + L2_EPS)
    return x * r, r


def conv_fwd(name, pre, taps, tl=256):
    length = pre.shape[0]
    tl = min(tl, length)

    def fn(ctx, rows, consts, prevs, nexts):
        prev = jnp.where(ctx.i > 0, prevs[0], 0.0)
        ext = jnp.concatenate([prev, rows[0]], axis=0)
        s = _silu(_conv_taps(ext, consts, tl + HALO)[HALO:])
        q = _cat([_l2n(x)[0] * (HD ** -0.5) for x in _heads(s[:, :DN_WIDTH], DN_HEADS, HD)])
        k = _cat([_l2n(x)[0] for x in _heads(s[:, DN_WIDTH:2 * DN_WIDTH], DN_HEADS, HD)])
        return [q, k, s[:, 2 * DN_WIDTH:]], []

    return rowwise(name, fn, length, tl, rows=[pre], consts=list(taps), prevs=[pre],
                   out_rows=[(DN_WIDTH, F32)] * 3)


def conv_bwd(name, pre, dq, dk, dv, taps, tl=256):
    length = pre.shape[0]
    tl = min(tl, length)
    n = tl + 2 * HALO

    def fn(ctx, rows, consts, prevs, nexts):
        last = ctx.i == ctx.nblk - 1
        prev = jnp.where(ctx.i > 0, prevs[0], 0.0)
        ext = jnp.concatenate([prev, rows[0], nexts[0]], axis=0)
        c = _conv_taps(ext, consts, n)
        s = _silu(c)
        zero = jnp.zeros((HALO, DN_WIDTH), F32)
        dqe, dke, dve = [jnp.concatenate([zero, rows[1 + t], jnp.where(last, 0.0, nexts[1 + t])], axis=0)
                         for t in range(3)]

        def l2_bwd(x, dy):
            y, r = _l2n(x)
            return r * (dy - y * _rsum(dy * y))

        dsq = _cat([l2_bwd(x, d * (HD ** -0.5)) for x, d in zip(_heads(s[:, :DN_WIDTH], DN_HEADS, HD),
                                                                 _heads(dqe, DN_HEADS, HD))])
        dsk = _cat([l2_bwd(x, d) for x, d in zip(_heads(s[:, DN_WIDTH:2 * DN_WIDTH], DN_HEADS, HD),
                                                  _heads(dke, DN_HEADS, HD))])
        dc = _cat([dsq, dsk, dve]) * _dsilu(c)
        dpre = consts[3] * dc
        for j in range(3):
            dpre = dpre + consts[j] * pltpu.roll(dc, n - (3 - j), 0)
        dc_cur = dc[HALO:HALO + tl]
        dws = [_csum(dc_cur * pltpu.roll(ext, 3 - j, 0)[HALO:HALO + tl]) for j in range(3)]
        dws.append(_csum(dc_cur * ext[HALO:HALO + tl]))
        return [dpre[HALO:HALO + tl]], dws

    return rowwise(name, fn, length, tl, rows=[pre, dq, dk, dv], consts=list(taps), prevs=[pre],
                   nexts=[pre, dq, dk, dv], out_rows=[(3 * DN_WIDTH, BF16)],
                   out_accs=[((1, 3 * DN_WIDTH), F32)] * 4)


def _gate_consts():
    lane = jnp.arange(128)[:, None]
    col = jnp.arange(2 * DN_WIDTH)[None, :]
    sel = ((lane < 2 * DN_HEADS) & (col // HD == lane)).astype(F32)
    pick = ((col.T == lane.T * HD) & (lane.T < 2 * DN_HEADS)).astype(F32)
    return sel, pick


def _gate_math(ab, alog, dtb):
    z = ab + dtb
    g = -jnp.exp(alog) * _softplus(z)
    beta = _sigmoid(ab)
    return z, g, beta


def gates_fwd(name, ab, alog, dtb, sel, tl=256):
    length = ab.shape[0]

    def fn(ctx, rows, consts, prevs, nexts):
        _, g, beta = _gate_math(rows[0], consts[0], consts[1])
        lane = _iota(g.shape, 1)
        small = jnp.where(lane < DN_HEADS, g, jnp.where(lane < 2 * DN_HEADS, beta, 0.0))
        big = dnn(small, consts[2], HI)
        return [big[:, :DN_WIDTH], big[:, DN_WIDTH:]], []

    return rowwise(name, fn, length, min(tl, length), rows=[ab], consts=[alog, dtb, sel],
                   out_rows=[(DN_WIDTH, F32)] * 2)


def gates_bwd(name, ab, dgb, dbb, alog, dtb, pick, tl=256):
    length = ab.shape[0]

    def fn(ctx, rows, consts, prevs, nexts):
        z, g, beta = _gate_math(rows[0], consts[0], consts[1])
        dsmall = dnn(_cat([rows[1], rows[2]]), consts[2], HI)
        lane = _iota(g.shape, 1)
        is_a = lane < DN_HEADS
        da = jnp.where(is_a, dsmall * (-jnp.exp(consts[0])) * _sigmoid(z), 0.0)
        db = jnp.where((lane >= DN_HEADS) & (lane < 2 * DN_HEADS), dsmall * beta * (1.0 - beta), 0.0)
        return [da + db], [_csum(jnp.where(is_a, dsmall * g, 0.0)), _csum(da)]

    return rowwise(name, fn, length, min(tl, length), rows=[ab, dgb, dbb], consts=[alog, dtb, pick],
                   out_rows=[(128, BF16)], out_accs=[((1, 128), F32)] * 2)


CPS = 2


def _chunk_scan_rows(x, suffix=False):
    n = x.shape[0]
    rc = _iota(x.shape, 0) & (CHUNK - 1)
    sh = 1
    while sh < CHUNK:
        if suffix:
            x = x + jnp.where(rc < CHUNK - sh, pltpu.roll(x, n - sh, 0), 0.0)
        else:
            x = x + jnp.where(rc >= sh, pltpu.roll(x, sh, 0), 0.0)
        sh *= 2
    return x


def _tri_inv(a_list, eye, bd):
    def each(f, *ls):
        return [f(*xs) for xs in zip(*ls)]

    dg = [jnp.where(bd, a, 0.0) for a in a_list]
    lo = each(lambda a, d: a - d, a_list, dg)
    n1 = [-d for d in dg]
    n2 = each(lambda n: dnn(n, n, X3), n1)
    n4 = each(lambda n: dnn(n, n, X3), n2)
    td = each(lambda p, s: dnn(eye + p, eye + s, X3), n1, n2)
    n8 = each(lambda n: dnn(n, n, X3), n4)
    td = each(lambda t, n: dnn(t, eye + n, X3), td, n4)
    td = each(lambda t, n: dnn(t, eye + n, X3), td, n8)
    m = each(lambda t, l: dnn(t, l, X3), td, lo)
    m2 = each(lambda x: dnn(x, x, X3), m)
    x = each(lambda p, s: dnn(eye - p, eye + s, X3), m, m2)
    return each(lambda p, t: dnn(p, t, X3), x, td)


def _chunk_common(q, k, v, gcb, bb):
    egb = jnp.exp(gcb)
    gc64 = gcb[:, :CHUNK]
    ii, jj = _iota((CHUNK, CHUNK), 0), _iota((CHUNK, CHUNK), 1)
    incl, strict = ii >= jj, ii > jj
    decay = jnp.exp(jnp.where(incl, gc64 - gc64.T, -jnp.inf))
    kb = k * bb
    vb = v * bb
    kbe = kb * egb
    pq = dnt(jnp.concatenate([kb, q], axis=0), k, X3)
    ekb = jnp.exp(gcb[CHUNK - 1:CHUNK, :] - gcb)
    return dict(egb=egb, decay=decay, kb=kb, vb=vb, kbe=kbe, pm=pq[:CHUNK], qm=pq[CHUNK:], ekb=ekb,
                incl=incl, strict=strict, ii=ii, jj=jj)


def _chunk_head(vals, ci, h):
    return [v[ci * CHUNK:(ci + 1) * CHUNK, h * HD:(h + 1) * HD] for v in vals]


def _assemble(per_chunk):
    return jnp.concatenate([_cat(hs) for hs in per_chunk], axis=0)


def _assemble3(per_chunk):
    return jnp.stack([jnp.concatenate([per_chunk[ci][h] for ci in range(CPS)], axis=0) for h in range(DN_HEADS)])


def delta_prep_fwd(name, q, k, v, gb, bb):
    length = q.shape[0]

    def fn(ctx, rows, consts, prevs, nexts):
        gcb_all = _chunk_scan_rows(rows[3])
        vals = [rows[0], rows[1], rows[2], gcb_all, rows[4]]
        units = [(ci, h) for ci in range(CPS) for h in range(DN_HEADS)]
        ins = [_chunk_head(vals, ci, h) for ci, h in units]
        cs = [_chunk_common(*i) for i in ins]
        eye = (cs[0]["ii"] == cs[0]["jj"]).astype(F32)
        ts = _tri_inv([jnp.where(c["strict"], c["pm"] * c["decay"], 0.0) for c in cs], eye,
                      (cs[0]["ii"] >> 4) == (cs[0]["jj"] >> 4))
        uws = [dnn(t, _cat([c["vb"], c["kbe"]]), X3) for t, c in zip(ts, cs)]

        def grid2(xs):
            return [xs[ci * DN_HEADS:(ci + 1) * DN_HEADS] for ci in range(CPS)]

        return [_assemble(grid2([uw[:, :HD] for uw in uws])), _assemble(grid2([uw[:, HD:] for uw in uws])),
                _assemble(grid2([i[0] * c["egb"] for i, c in zip(ins, cs)])),
                _assemble(grid2([i[1] * c["ekb"] for i, c in zip(ins, cs)])), gcb_all,
                _assemble3(grid2([c["qm"] * c["decay"] for c in cs])), _assemble3(grid2(ts))], []

    return rowwise(name, fn, length, CHUNK * CPS, rows=[q, k, v, gb, bb],
                   out_rows=[(DN_WIDTH, F32)] * 5 + [(DN_HEADS, CHUNK, F32)] * 2)


def delta_prep_bwd(name, q, k, v, gb, bb, t3, du, dw, dqd, dkd, dattn3, dgl):
    length = q.shape[0]

    def fn(ctx, rows, consts, prevs, nexts):
        gcb_all = _chunk_scan_rows(rows[3])
        vals = [rows[0], rows[1], rows[2], gcb_all] + list(rows[4:9])
        t3v, da3v, dglv = rows[9], rows[10], rows[11]
        units = [(ci, h) for ci in range(CPS) for h in range(DN_HEADS)]
        ins = [_chunk_head(vals, ci, h) for ci, h in units]
        cs = [_chunk_common(*i[:5]) for i in ins]
        ts = [t3v[h][ci * CHUNK:(ci + 1) * CHUNK] for ci, h in units]
        dattns = [jnp.where(c["incl"], da3v[h][ci * CHUNK:(ci + 1) * CHUNK], 0.0) for (ci, h), c in zip(units, cs)]
        duws = [_cat([i[5], i[6]]) for i in ins]
        dvks = [dtn(t, d, X3) for t, d in zip(ts, duws)]
        dts = [dnt(d, _cat([c["vb"], c["kbe"]]), X3) for d, c in zip(duws, cs)]
        dts = [dnt(d, t, X3) for d, t in zip(dts, ts)]
        das = [jnp.where(c["strict"], -dtn(t, d, X3), 0.0) for c, t, d in zip(cs, ts, dts)]
        dpqs = [jnp.concatenate([da * c["decay"], dat * c["decay"]], axis=0) for da, dat, c in zip(das, dattns, cs)]
        dpqks = [dnn(d, i[1], X3) for d, i in zip(dpqs, ins)]
        dkps = [dtn(d, jnp.concatenate([c["kb"], i[0]], axis=0), X3) for d, c, i in zip(dpqs, cs, ins)]
        dqs, dks, dvs, dgcs, dbs = [], [], [], [], []
        for (ci, h), i, c, dvk, da, dattn, dpqk, dkp in zip(units, ins, cs, dvks, das, dattns, dpqks, dkps):
            qh, kh, vh, _, bh, _, _, dqdh, dkdh = i
            dvb, dkbe = dvk[:, :HD], dvk[:, HD:]
            dkb = dpqk[:CHUNK] + dkbe * c["egb"]
            c1 = _rsum(dkbe * c["kb"] + dqdh * qh) * c["egb"]
            c2 = _rsum(dkdh * kh) * c["ekb"]
            e = (da * c["pm"] + dattn * c["qm"]) * c["decay"]
            dgc = c1 - c2 + _rsum(e) - _rsum(e.T)
            dgl_tot = jnp.max(dglv[ci * 8:(ci + 1) * 8, h * HD:(h + 1) * HD], axis=0, keepdims=True) + _csum(c2)
            dgcs.append(dgc + jnp.where(_iota((CHUNK, HD), 0) == CHUNK - 1, dgl_tot, 0.0))
            dqs.append(dpqk[CHUNK:] + dqdh * c["egb"])
            dks.append(dkp + dkdh * c["ekb"] + dkb * bh)
            dvs.append(dvb * bh)
            dbs.append(jnp.broadcast_to(_rsum(dkb * kh + dvb * vh), (CHUNK, HD)))

        def grid2(xs):
            return [xs[ci * DN_HEADS:(ci + 1) * DN_HEADS] for ci in range(CPS)]

        return [_assemble(grid2(dqs)), _assemble(grid2(dks)), _assemble(grid2(dvs)),
                _chunk_scan_rows(_assemble(grid2(dgcs)), suffix=True), _assemble(grid2(dbs))], []

    return rowwise(name, fn, length, CHUNK * CPS,
                   rows=[q, k, v, gb, bb, du, dw, dqd, dkd, t3, dattn3, (dgl, DN_WIDTH, 0, 8 * CPS)],
                   out_rows=[(DN_WIDTH, F32)] * 5)


def delta_scan_fwd(name, qd, kd, u, w, attn3, gcb):
    length = qd.shape[0]
    n = length // CHUNK
    row = pl.BlockSpec((CHUNK, DN_WIDTH), lambda c: (c, 0))
    sq = pl.BlockSpec((DN_HEADS, CHUNK, CHUNK), lambda c: (0, c, 0))

    def body(qd_ref, kd_ref, u_ref, w_ref, attn_ref, gc_ref, o_ref, vn_ref, st_ref, s_ref):
        c = pl.program_id(0)

        @pl.when(c == 0)
        def _():
            s_ref[...] = jnp.zeros_like(s_ref)

        for h in range(DN_HEADS):
            sl = pl.ds(h * HD, HD)
            s = s_ref[h]
            st_ref[0, h] = s
            vn = u_ref[:, sl] - dnn(w_ref[:, sl], s)
            o_ref[:, sl] = dnn(qd_ref[:, sl], s) + dnn(attn_ref[h], vn)
            vn_ref[:, sl] = vn
            egl = jnp.exp(gc_ref[pl.ds(CHUNK - 1, 1), sl])
            s_ref[h] = s * egl + dtn(kd_ref[:, sl], vn)

    return pl.pallas_call(
        body, name=name, grid=(n,), in_specs=[row, row, row, row, sq, row],
        out_specs=[row, row, pl.BlockSpec((1, DN_HEADS, HD, HD), lambda c: (c, 0, 0, 0))],
        out_shape=[jax.ShapeDtypeStruct((length, DN_WIDTH), F32), jax.ShapeDtypeStruct((length, DN_WIDTH), F32),
                   jax.ShapeDtypeStruct((n, DN_HEADS, HD, HD), F32)],
        scratch_shapes=[pltpu.VMEM((DN_HEADS, HD, HD), F32)],
        compiler_params=_params(("arbitrary",)),
    )(qd, kd, u, w, attn3, gcb)


def delta_scan_bwd(name, do, qd, kd, w, attn3, vn, st, gcb):
    length = qd.shape[0]
    n = length // CHUNK
    row = pl.BlockSpec((CHUNK, DN_WIDTH), lambda c: (n - 1 - c, 0))
    sq = pl.BlockSpec((DN_HEADS, CHUNK, CHUNK), lambda c: (0, n - 1 - c, 0))
    stb = pl.BlockSpec((1, DN_HEADS, HD, HD), lambda c: (n - 1 - c, 0, 0, 0))
    glb = pl.BlockSpec((8, DN_WIDTH), lambda c: (n - 1 - c, 0))

    def body(do_ref, qd_ref, kd_ref, w_ref, attn_ref, vn_ref, st_ref, gc_ref,
             dqd_ref, dkd_ref, du_ref, dw_ref, dattn_ref, dgl_ref, ds_ref):
        c = pl.program_id(0)

        @pl.when(c == 0)
        def _():
            ds_ref[...] = jnp.zeros_like(ds_ref)

        for h in range(DN_HEADS):
            sl = pl.ds(h * HD, HD)
            s = st_ref[0, h]
            dsn = ds_ref[h]
            d_o = do_ref[:, sl]
            vnh = vn_ref[:, sl]
            egl = jnp.exp(gc_ref[pl.ds(CHUNK - 1, 1), sl])
            dattn_ref[h] = dnt(d_o, vnh)
            dvn = dtn(attn_ref[h], d_o) + dnn(kd_ref[:, sl], dsn)
            dqd_ref[:, sl] = dnt(d_o, s)
            dkd_ref[:, sl] = dnt(vnh, dsn)
            du_ref[:, sl] = dvn
            dw_ref[:, sl] = -dnt(dvn, s)
            dgl_ref[:, sl] = jnp.broadcast_to(_csum(_rsum(dsn * s)) * egl, (8, HD))
            ds_ref[h] = dsn * egl + dtn(qd_ref[:, sl], d_o) - dtn(w_ref[:, sl], dvn)

    return pl.pallas_call(
        body, name=name, grid=(n,), in_specs=[row, row, row, row, sq, row, stb, row],
        out_specs=[row, row, row, row, sq, glb],
        out_shape=[jax.ShapeDtypeStruct((length, DN_WIDTH), F32)] * 4
        + [jax.ShapeDtypeStruct((DN_HEADS, length, CHUNK), F32), jax.ShapeDtypeStruct((n * 8, DN_WIDTH), F32)],
        scratch_shapes=[pltpu.VMEM((DN_HEADS, HD, HD), F32)],
        compiler_params=_params(("arbitrary",)),
    )(do, qd, kd, w, attn3, vn, st, gcb)


def onorm_fwd(name, o, z, nw, tl=256):
    length = o.shape[0]

    def fn(ctx, rows, consts, prevs, nexts):
        outs = []
        for oh, zh in zip(_heads(rows[0], DN_HEADS, HD), _heads(rows[1], DN_HEADS, HD)):
            r = lax.rsqrt(jnp.mean(oh * oh, axis=1, keepdims=True) + RMS_EPS)
            outs.append(oh * r * consts[0] * _silu(zh))
        return [_cat(outs)], []

    return rowwise(name, fn, length, min(tl, length), rows=[o, z], consts=[nw], out_rows=[(DN_WIDTH, BF16)])[0]


def onorm_bwd(name, o, z, d_on, nw, tl=256):
    length = o.shape[0]

    def fn(ctx, rows, consts, prevs, nexts):
        dos, dzs = [], []
        dnw = jnp.zeros((1, HD), F32)
        for oh, zh, dh in zip(*[_heads(r, DN_HEADS, HD) for r in rows]):
            r = lax.rsqrt(jnp.mean(oh * oh, axis=1, keepdims=True) + RMS_EPS)
            y = oh * r
            sz = _silu(zh)
            t = dh * sz * consts[0]
            dos.append(r * (t - y * jnp.mean(t * y, axis=1, keepdims=True)))
            dzs.append(dh * y * consts[0] * _dsilu(zh))
            dnw = dnw + _csum(dh * y * sz)
        return [_cat(dos), _cat(dzs)], [dnw]

    return rowwise(name, fn, length, min(tl, length), rows=[o, z, d_on], consts=[nw],
                   out_rows=[(DN_WIDTH, F32), (DN_WIDTH, BF16)], out_accs=[((1, HD), F32)])


def merge_fwd(name, gates, ydn, ypool, tl=256):
    length = ydn.shape[0]

    def fn(ctx, rows, consts, prevs, nexts):
        gt = rows[0]
        return [_sigmoid(gt[:, :D_MODEL]) * rows[1] + _sigmoid(gt[:, D_MODEL:]) * rows[2]], []

    return rowwise(name, fn, length, min(tl, length), rows=[gates, ydn, ypool], out_rows=[(D_MODEL, BF16)])[0]


def merge_bwd(name, gates, ydn, ypool, dm, tl=256):
    length = ydn.shape[0]

    def fn(ctx, rows, consts, prevs, nexts):
        gt, yd, yp, d = rows
        sd, sp = _sigmoid(gt[:, :D_MODEL]), _sigmoid(gt[:, D_MODEL:])
        dgates = _cat([d * yd * sd * (1.0 - sd), d * yp * sp * (1.0 - sp)])
        return [d * sd, d * sp, dgates], []

    return rowwise(name, fn, length, min(tl, length), rows=[gates, ydn, ypool, dm],
                   out_rows=[(D_MODEL, BF16), (D_MODEL, BF16), (2 * D_MODEL, BF16)])


def _trailing_sums(ext, upto):
    s, sh = ext, 1
    while sh < upto:
        s = s + pltpu.roll(s, sh, 0)
        sh *= 2
    return s


def _leading_sums(ext, upto, n):
    s, sh = ext, 1
    while sh < upto:
        s = s + pltpu.roll(s, n - sh, 0)
        sh *= 2
    return s


def _pool_mixed(ctx, p, prev, tl):
    prevm = jnp.where(ctx.i > 0, prev, 0.0)
    t1 = (_row_index(ctx, tl) + 1).astype(F32)
    outs = []
    for gi, win in enumerate(POOL_WINDOWS):
        sl = slice(gi * HD, (gi + 1) * HD)
        ext = jnp.concatenate([prevm[:, sl], p[:, sl]], axis=0)
        mean = _trailing_sums(ext, win)[HALO:] / jnp.minimum(t1, float(win))
        outs.append(mean - p[:, sl])
    return outs


def pool_fwd(name, p, pool_w, scale, tl=256):
    length = p.shape[0]
    tl = min(tl, length)

    def fn(ctx, rows, consts, prevs, nexts):
        mixed = _pool_mixed(ctx, rows[0], prevs[0], tl)
        y = _cat([dnn(m, consts[0][gi]) for gi, m in enumerate(mixed)])
        return [y * consts[1]], []

    return rowwise(name, fn, length, tl, rows=[p], consts=[pool_w, scale], prevs=[p],
                   out_rows=[(POOL_WIDTH, BF16)])[0]


def pool_bwd(name, p, dpo, pool_w, scale, tl=256):
    length = p.shape[0]
    tl = min(tl, length)
    n = tl + HALO

    def fn(ctx, rows, consts, prevs, nexts):
        last = ctx.i == ctx.nblk - 1
        mixed = _pool_mixed(ctx, rows[0], prevs[0], tl)
        dext = jnp.concatenate([rows[1], jnp.where(last, 0.0, nexts[0])], axis=0)
        t1 = (_row_index(ctx, n) + 1).astype(F32)
        dps, dws, dscs = [], [], []
        for gi, win in enumerate(POOL_WINDOWS):
            sl = slice(gi * HD, (gi + 1) * HD)
            wg = consts[0][gi]
            dyraw = dext[:, sl] * consts[1][:, sl]
            dmix = dnt(dyraw, wg)
            dws.append(dtn(mixed[gi], dyraw[:tl]))
            dscs.append(_csum(rows[1][:, sl] * dnn(mixed[gi], wg)))
            lead = _leading_sums(dmix / jnp.minimum(t1, float(win)), win, n)
            dps.append(lead[:tl] - dmix[:tl])
        return [_cat(dps)], [jnp.stack(dws), _cat(dscs)]

    return rowwise(name, fn, length, tl, rows=[p, dpo], consts=[pool_w, scale], prevs=[p], nexts=[dpo],
                   out_rows=[(POOL_WIDTH, BF16)],
                   out_accs=[((len(POOL_WINDOWS), HD, HD), F32), ((1, POOL_WIDTH), F32)])


def _xa_probs(qh, kh):
    s = dnt(qh, kh) * (XA_HD ** -0.5)
    e = jnp.exp(s - jnp.max(s, axis=1, keepdims=True))
    return e / _rsum(e)


def xattn_fwd(name, qx, kx, vx, tl=256):
    length = qx.shape[0]

    def fn(ctx, rows, consts, prevs, nexts):
        outs = [dnn(_xa_probs(qh, kh), vh) for qh, kh, vh in
                zip(_heads(rows[0], XA_HEADS, XA_HD), _heads(consts[0], XA_HEADS, XA_HD),
                    _heads(consts[1], XA_HEADS, XA_HD))]
        return [_cat(outs)], []

    return rowwise(name, fn, length, min(tl, length), rows=[qx], consts=[kx, vx], out_rows=[(D_MODEL, BF16)])[0]


def xattn_bwd(name, qx, dox, kx, vx, tl=256):
    length = qx.shape[0]

    def fn(ctx, rows, consts, prevs, nexts):
        dqs, dks, dvs = [], [], []
        for qh, dh, kh, vh in zip(_heads(rows[0], XA_HEADS, XA_HD), _heads(rows[1], XA_HEADS, XA_HD),
                                  _heads(consts[0], XA_HEADS, XA_HD), _heads(consts[1], XA_HEADS, XA_HD)):
            pr = _xa_probs(qh, kh)
            dpr = dnt(dh, vh)
            ds = pr * (dpr - _rsum(dpr * pr)) * (XA_HD ** -0.5)
            dqs.append(dnn(ds, kh))
            dks.append(dtn(ds, qh))
            dvs.append(dtn(pr, dh))
        return [_cat(dqs)], [_cat(dks), _cat(dvs)]

    return rowwise(name, fn, length, min(tl, length), rows=[qx, dox], consts=[kx, vx],
                   out_rows=[(D_MODEL, BF16)], out_accs=[((N_MEM, D_MODEL), F32)] * 2)


def local_step(x, mem, target, w, io):
    sel, pick = _gate_consts()
    alog = jnp.pad(w["a_log"], ((0, 0), (0, 128 - DN_HEADS)))
    dtb = jnp.pad(w["dt_bias"], ((0, 0), (0, 128 - DN_HEADS)))

    f1, res1, w_down1 = ffn_fwd("ffn1", x, w["ffn1_w_gate"], w["ffn1_w_up"], io.ffn1_down, deps=io.rest_started())
    x1, r1 = ln_fwd("ln1", [(ALPHA, x), (0.5, f1)], w["ln1_g"], w["ln1_b"], deps=io.halfway("mid", f1))
    w = dict(w, ffn1_w_down=w_down1, **io.weights("mid", x1))
    taps = [w["conv_w"][j:j + 1] for j in range(4)]

    pre = mm("in_qkv", x1, w["in_qkv"], tb=True)
    z = mm("in_z", x1, w["in_z"], tb=True)
    gates = mm("in_gates", x1, w["in_gates"], tb=True)
    p = mm("in_p", x1, w["in_p"], tb=True)
    ab = mm("in_ab", x1, w["in_ab"], tb=True)
    q, k, v = conv_fwd("conv", pre, taps)
    gb, bb = gates_fwd("gates", ab, alog, dtb, sel)
    u, wd_, qd, kd, gcb, attn3, t3 = delta_prep_fwd("dprep", q, k, v, gb, bb)
    o, vn, st = delta_scan_fwd("dscan", qd, kd, u, wd_, attn3, gcb)
    on = onorm_fwd("onorm", o, z, w["dn_norm_w"])
    ydn = mm("dn_branch", on, w["w_dn_branch"], tb=True)
    po = pool_fwd("pool", p, w["pool_w"], w["pool_scale"])
    ypool = mm("pool_branch", po, w["w_pool_branch"], tb=True)
    merged = merge_fwd("merge", gates, ydn, ypool)
    mix = mm("mix_out", merged, w["w_mix_out"])
    x2, r2 = ln_fwd("ln2", [(ALPHA, x1), (1.0, mix)], w["ln2_g"], w["ln2_b"])

    m, _ = ln_fwd("ln_mem", [(1.0, mem)], w["mem_ln_g"], w["mem_ln_b"])
    qx = mm("xa_q", x2, w["xa_wq"], deps=io.halfway("ffn2", x2))
    kx = mm("xa_k", m, w["xa_wk"])
    vx = mm("xa_v", m, w["xa_wv"])
    ox = xattn_fwd("xattn", qx, kx, vx)
    xa = mm("xa_o", ox, w["xa_wo"])
    x3, r3 = ln_fwd("ln3", [(ALPHA, x2), (1.0, xa)], w["ln3_g"], w["ln3_b"])
    w = dict(w, **io.weights("ffn2", x3))

    f2, res2, _ = ffn_fwd("ffn2", x3, w["ffn2_w_gate"], w["ffn2_w_up"], w["ffn2_w_down"])
    dy4, r4, loss = ln_loss("ln4_loss", [(ALPHA, x3), (0.5, f2)], w["ln4_g"], w["ln4_b"], target)

    g = {}
    dr4, g["ln4_g"], g["ln4_b"] = ln_bwd("ln4_b", r4, [(1.0, dy4)], w["ln4_g"])
    dx3, g["ffn2_w_gate"], g["ffn2_w_up"], g["ffn2_w_down"] = ffn_bwd(
        "ffn2b", x3, res2, dr4, w["ffn2_w_gate"], w["ffn2_w_up"], w["ffn2_w_down"])
    dep = io.grads_out("ffn2", g)
    dr3, g["ln3_g"], g["ln3_b"] = ln_bwd("ln3_b", r3, [(ALPHA, dr4), (1.0, dx3)], w["ln3_g"], deps=dep)

    dox = mm("xa_do", dr3, w["xa_wo"], tb=True)
    g["xa_wo"] = mm("xa_dwo", ox, dr3, ta=True)
    dqx, dkx, dvx = xattn_bwd("xattn_b", qx, dox, kx, vx)
    g["xa_wq"] = mm("xa_dwq", x2, dqx, ta=True)
    dx2 = mm("xa_dx", dqx, w["xa_wq"], tb=True)
    g["xa_wk"] = mm("xa_dwk", m, dkx, ta=True)
    g["xa_wv"] = mm("xa_dwv", m, dvx, ta=True)
    dmm = mm("xa_dmk", dkx, w["xa_wk"], tb=True, deps=io.grads_out("xa", g))
    dmm = mm("xa_dmv", dvx, w["xa_wv"], tb=True, add=dmm)
    _, g["mem_ln_g"], g["mem_ln_b"] = ln_bwd("ln_mem_b", mem, [(1.0, dmm)], w["mem_ln_g"])
    dr2, g["ln2_g"], g["ln2_b"] = ln_bwd("ln2_b", r2, [(ALPHA, dr3), (1.0, dx2)], w["ln2_g"])
    io.grads_in("ffn2", dr2)

    dmerged = mm("mix_dm", dr2, w["w_mix_out"], tb=True)
    g["w_mix_out"] = mm("mix_dw", merged, dr2, ta=True)
    d_ydn, d_ypool, d_gates = merge_bwd("merge_b", gates, ydn, ypool, dmerged)
    g["w_dn_branch"] = mm("dn_dw", d_ydn, on, ta=True)
    d_on = mm("dn_dx", d_ydn, w["w_dn_branch"])
    g["w_pool_branch"] = mm("pool_dw", d_ypool, po, ta=True)
    d_po = mm("pool_dx", d_ypool, w["w_pool_branch"])
    dp, g["pool_w"], g["pool_scale"] = pool_bwd("pool_b", p, d_po, w["pool_w"], w["pool_scale"])
    d_o, dz, g["dn_norm_w"] = onorm_bwd("onorm_b", o, z, d_on, w["dn_norm_w"])
    dqd, dkd, du, dw_, dattn3, dgl = delta_scan_bwd("dscan_b", d_o, qd, kd, wd_, attn3, vn, st, gcb)
    dq, dk, dv, dgb, dbb = delta_prep_bwd("dprep_b", q, k, v, gb, bb, t3, du, dw_, dqd, dkd, dattn3, dgl)
    dpre, dc0, dc1, dc2, dc3 = conv_bwd("conv_b", pre, dq, dk, dv, taps)
    g["conv_w"] = jnp.concatenate([dc0, dc1, dc2, dc3], axis=0)
    d_ab, dalog, ddtb = gates_bwd("gates_b", ab, dgb, dbb, alog, dtb, pick)
    g["a_log"] = dalog[:, :DN_HEADS]
    g["dt_bias"] = ddtb[:, :DN_HEADS]
    g["in_qkv"] = mm("in_dwqkv", dpre, x1, ta=True)
    g["in_z"] = mm("in_dwz", dz, x1, ta=True)
    g["in_gates"] = mm("in_dwgates", d_gates, x1, ta=True)
    g["in_p"] = mm("in_dwp", dp, x1, ta=True)
    g["in_ab"] = mm("in_dwab", d_ab, x1, ta=True)
    io.grads_in("xa", g["in_ab"])
    dx1 = mm("in_dxqkv", dpre, w["in_qkv"], deps=io.grads_out("mixer", g))
    dx1 = mm("in_dxz", dz, w["in_z"], add=dx1)
    dx1 = mm("in_dxgates", d_gates, w["in_gates"], add=dx1)
    dx1 = mm("in_dxp", dp, w["in_p"], add=dx1)
    dx1 = mm("in_dxab", d_ab, w["in_ab"], add=dx1)
    dr1, g["ln1_g"], g["ln1_b"] = ln_bwd("ln1_b", r1, [(ALPHA, dr2), (1.0, dx1)], w["ln1_g"])

    def on_dwd(dwd):
        return io.small_out(dict(g, loss=loss[0, :1])) + io.grads_out("ffn1_d", dict(ffn1_w_down=dwd))

    def on_dwgu(dwg, dwu):
        return io.grads_out("ffn1_gu", dict(ffn1_w_gate=dwg, ffn1_w_up=dwu))

    dx0, g["ffn1_w_gate"], g["ffn1_w_up"], g["ffn1_w_down"] = ffn_bwd(
        "ffn1b", x, res1, dr1, w["ffn1_w_gate"], w["ffn1_w_up"], w["ffn1_w_down"], on_dwd=on_dwd, on_dwgu=on_dwgu)
    grad_x = axpy("grad_x", [(ALPHA, dr1), (1.0, dx0)])
    return loss, grad_x, g


WEIGHT_NAMES = ['ffn1_w_gate', 'ffn1_w_up', 'ffn1_w_down', 'ln1_g', 'ln1_b', 'w_in', 'conv_w', 'a_log', 'dt_bias',
                'dn_norm_w', 'w_dn_branch', 'pool_w', 'pool_scale', 'w_pool_branch', 'w_mix_out', 'ln2_g', 'ln2_b',
                'mem_ln_g', 'mem_ln_b', 'xa_wq', 'xa_wk', 'xa_wv', 'xa_wo', 'ln3_g', 'ln3_b', 'ffn2_w_gate',
                'ffn2_w_up', 'ffn2_w_down', 'ln4_g', 'ln4_b']
SHARDED = [
    ("ffn1_w_gate", "cols", (1024, 352)), ("ffn1_w_up", "cols", (1024, 352)), ("ffn1_w_down", "rows", (352, 1024)),
    ("w_in", "cols", (1024, 577)), ("conv_w", "flat", (4, 192)), ("w_dn_branch", "cols", (512, 128)),
    ("w_pool_branch", "cols", (512, 128)), ("w_mix_out", "rows", (128, 1024)), ("xa_wq", "rows", (128, 1024)),
    ("xa_wk", "rows", (128, 1024)), ("xa_wv", "rows", (128, 1024)), ("xa_wo", "rows", (128, 1024)),
    ("ffn2_w_gate", "cols", (1024, 352)), ("ffn2_w_up", "cols", (1024, 352)), ("ffn2_w_down", "rows", (352, 1024)),
]
REPLICATED = [n for n in WEIGHT_NAMES if n not in {s[0] for s in SHARDED}]
ROW_ALIGN = 16
ROW_BLOCKS = (512, 384, 352, 256, 192, 176, 128)
GROUPS = {"ffn1_gu": ("ffn1_w_gate", "ffn1_w_up"), "ffn1_d": ("ffn1_w_down",),
          "mixer": ("w_in", "conv_w", "w_dn_branch", "w_pool_branch", "w_mix_out"),
          "xa": ("xa_wq", "xa_wk", "xa_wv", "xa_wo"),
          "ffn2": ("ffn2_w_gate", "ffn2_w_up", "ffn2_w_down")}
GROUPS["mid"] = GROUPS["mixer"] + GROUPS["xa"]
W_IN_COLS = 577
W_IN_PIECES = (("in_qkv", 0, 1536), ("in_z", 1536, 2048), ("in_ab", 2048, 2056), ("in_p", 2056, 2568),
               ("in_gates", 2568, 4616))


def _round_up(n, m):
    return -(-n // m) * m


def _layout():
    off, table = 0, {}
    for name, form, shape in SHARDED:
        valid = {"rows": shape[0], "cols": shape[1], "flat": 2}[form]
        width = {"rows": shape[1], "cols": shape[0], "flat": shape[0] * shape[1]}[form]
        rows = _round_up(valid, ROW_ALIGN)
        table[name] = (off, rows, valid, width, form, shape)
        off += rows
    return table


LAYOUT = _layout()


def _group_span(names):
    base = LAYOUT[names[0]][0]
    rows = LAYOUT[names[-1]][0] + LAYOUT[names[-1]][1] - base
    while not any(rows % b == 0 for b in ROW_BLOCKS):
        rows += ROW_ALIGN
    return base, rows


def _row_block(rows):
    return _pick(rows, ROW_BLOCKS)


def _pad_block(blk, rows):
    return jnp.pad(blk, ((0, rows - blk.shape[0]), (0, LANES - blk.shape[1])))


def pack_weight_shards(shards, names):
    parts, used = [], 0
    for name in names:
        off, rows, valid, width, form, _ = LAYOUT[name]
        s = shards[name]
        if form == "flat":
            flat = s.reshape(1, -1)
            hi = flat.astype(BF16)
            blk = jnp.concatenate([hi, (flat - hi.astype(F32)).astype(BF16)], axis=0)
        else:
            blk = (s.T if form == "cols" else s).astype(BF16)
        parts.append(_pad_block(blk, rows))
        used += rows
    if _group_span(names)[1] > used:
        parts.append(jnp.zeros((_group_span(names)[1] - used, LANES), BF16))
    return jnp.concatenate(parts, axis=0)


def _w_in_rows(padded, rows, first, last):
    segs = []
    for k in range(N_DEV):
        lo, hi = max(first, k * W_IN_COLS), min(last, (k + 1) * W_IN_COLS)
        if lo < hi:
            segs.append(padded[k * rows + lo - k * W_IN_COLS:k * rows + hi - k * W_IN_COLS])
    return segs[0] if len(segs) == 1 else jnp.concatenate(segs, axis=0)


def unpack_full_weights(gathered, names):
    out, base = {}, _group_span(names)[0]
    for name in names:
        off, rows, valid, width, form, shape = LAYOUT[name]
        seg = gathered[:, off - base:off - base + rows]
        if form == "flat":
            flat = seg[:, 0, :width].astype(F32) + seg[:, 1, :width].astype(F32)
            out[name] = flat.reshape((N_DEV,) + shape).transpose(1, 0, 2).reshape(shape[0], N_DEV * shape[1])
        elif name == "w_in":
            padded = seg.reshape(N_DEV * rows, LANES)
            for piece, first, last in W_IN_PIECES:
                out[piece] = _w_in_rows(padded, rows, first, last)
        else:
            out[name] = seg[:, :valid, :width].reshape(N_DEV * valid, width)
    return out


def pack_full_grads(grads, names):
    parts, used = [], 0
    for name in names:
        off, rows, valid, width, form, shape = LAYOUT[name]
        if form == "flat":
            full = grads[name].reshape(shape[0], N_DEV, shape[1]).transpose(1, 0, 2).reshape(N_DEV, 1, width)
        elif name == "w_in":
            full = jnp.concatenate([grads[piece][:last - first] for piece, first, last in W_IN_PIECES], axis=0)
            full = full.reshape(N_DEV, valid, width)
        else:
            full = grads[name].reshape(N_DEV, valid, width)
        parts.append(jnp.pad(full, ((0, 0), (0, rows - full.shape[1]), (0, LANES - width))))
        used += rows
    if _group_span(names)[1] > used:
        parts.append(jnp.zeros((N_DEV, _group_span(names)[1] - used, LANES), F32))
    return jnp.concatenate(parts, axis=1)


def unpack_grad_shards(packed, names):
    out, base = {}, _group_span(names)[0]
    for name in names:
        off, rows, valid, width, form, shape = LAYOUT[name]
        off -= base
        if form == "flat":
            out[name] = packed[off, :width].reshape(shape)
        elif form == "cols":
            out[name] = packed[off:off + valid, :width].T
        else:
            out[name] = packed[off:off + valid, :width]
    return out


SMALL_SHAPES = {n: (1024,) for n in REPLICATED}
SMALL_SHAPES.update(pool_w=(4, 128, 128), pool_scale=(512,), dn_norm_w=(128,), a_log=(4,), dt_bias=(4,))


SMALL_SHAPES["loss"] = (1,)
SMALL_NAMES = REPLICATED + ["loss"]


def _small_layout():
    off, table = 0, {}
    for name in SMALL_NAMES:
        numel = 1
        for d in SMALL_SHAPES[name]:
            numel *= d
        rows = _round_up(-(-numel // LANES), 8)
        table[name] = (off, rows, numel)
        off += rows
    return table, off


SMALL_LAYOUT, SMALL_ROWS = _small_layout()


def _to_rows(flat, rows):
    return jnp.pad(flat, (0, rows * LANES - flat.shape[0])).reshape(rows, LANES)


def pack_small(values):
    return jnp.concatenate([_to_rows(values[name].reshape(-1), SMALL_LAYOUT[name][1]) for name in SMALL_NAMES], axis=0)


def unpack_small(packed):
    out = {}
    for name in SMALL_NAMES:
        off, rows, numel = SMALL_LAYOUT[name]
        out[name] = packed[off:off + rows].reshape(-1)[:numel].reshape(SMALL_SHAPES[name])
    return out


MESH = pl.DeviceIdType.MESH


def _position():
    return lax.axis_index("x"), lax.axis_index("y"), lax.axis_index("c")


def _other_chips(x, y):
    return [(1 - x, y), (x, 1 - y), (1 - x, 1 - y)]


def all_gather(name, block):
    rows, n = block.shape

    def body(x_ref, out_ref, send_sems, recv_sems, local_sem):
        x, y, c = _position()
        me, sibling = (x, y, c), (x, y, 1 - c)
        chips = _other_chips(x, y)

        def slot(px, py, pc):
            return out_ref.at[4 * px + 2 * py + pc]

        def copy(k, blk, to, src=None):
            return pltpu.make_async_remote_copy(
                src_ref=slot(*blk) if src is None else src, dst_ref=slot(*blk),
                send_sem=send_sems.at[k], recv_sem=recv_sems.at[k], device_id=to, device_id_type=MESH)

        mine = pltpu.make_async_copy(x_ref, slot(*me), local_sem)
        mine.start()
        first = [copy(0, me, sibling, src=x_ref)]
        first += [copy(1 + j, me, (*chip, c), src=x_ref) for j, chip in enumerate(chips)]
        for cp in first:
            cp.start()
        passed = [copy(4 + j, (*chip, c), sibling) for j, chip in enumerate(chips)]
        for j, chip in enumerate(chips):
            copy(1 + j, (*chip, c), me).wait_recv()
            passed[j].start()
        copy(0, sibling, me).wait_recv()
        for j, chip in enumerate(chips):
            copy(4 + j, (*chip, 1 - c), me).wait_recv()
        for cp in first + passed:
            cp.wait_send()
        mine.wait()

    return pl.pallas_call(
        body, name=name, out_shape=jax.ShapeDtypeStruct((N_DEV, rows, n), block.dtype),
        in_specs=[ANY], out_specs=ANY,
        scratch_shapes=[pltpu.SemaphoreType.DMA((7,)), pltpu.SemaphoreType.DMA((7,)), pltpu.SemaphoreType.DMA(())],
    )(block)


HBM = pl.BlockSpec(memory_space=pltpu.HBM)
SEM = pl.BlockSpec(memory_space=pltpu.SEMAPHORE)
EFFECT = pltpu.SideEffectType.DATAFLOW_SIDE_EFFECTING


def _remote(src, dst, send_sem, recv_sem, to):
    return pltpu.make_async_remote_copy(src_ref=src, dst_ref=dst, send_sem=send_sem, recv_sem=recv_sem,
                                        device_id=to, device_id_type=MESH)


def split_start(name, bufs, n, make_copies):
    nb = len(bufs)

    def body(*refs):
        for out_cp, _ in make_copies(refs[:nb], refs[nb:nb + n], refs[nb + n:nb + 2 * n]):
            out_cp.start()
        refs[-1][...] = jnp.zeros_like(refs[-1])

    outs = pl.pallas_call(
        body, name=name,
        out_shape=tuple([pltpu.SemaphoreType.DMA(())] * (2 * n)) + tuple(pltpu.HBM(b.shape, b.dtype) for b in bufs)
        + (jax.ShapeDtypeStruct((8, 128), F32),),
        in_specs=[HBM] * nb,
        out_specs=tuple([SEM] * (2 * n) + [HBM] * nb + [pl.BlockSpec(memory_space=pltpu.VMEM)]),
        input_output_aliases={i: 2 * n + i for i in range(nb)},
        compiler_params=pltpu.CompilerParams(has_side_effects=EFFECT),
    )(*[pltpu.with_memory_space_constraint(b, pltpu.HBM) for b in bufs])
    return list(outs[:2 * n]), list(outs[2 * n:2 * n + nb]), outs[-1]


def split_wait(name, bufs, sems, n, make_copies, after):
    nb = len(bufs)

    def body(*refs):
        for out_cp, in_cp in make_copies(refs[:nb], refs[nb:nb + n], refs[nb + n:nb + 2 * n]):
            out_cp.wait_send()
            in_cp.wait_recv()

    outs = pl.pallas_call(
        body, name=name, out_shape=tuple(pltpu.HBM(b.shape, b.dtype) for b in bufs),
        in_specs=[HBM] * nb + [SEM] * (2 * n) + [ANY], out_specs=tuple([HBM] * nb),
        input_output_aliases={i: i for i in range(nb)},
        compiler_params=pltpu.CompilerParams(has_side_effects=EFFECT),
    )(*bufs, *sems, after)
    return list(outs)


def _gather_stage1(refs, send, recv):
    src, land = refs
    x, y, c = _position()
    peers = [(x, y, 1 - c)] + [(*chip, c) for chip in _other_chips(x, y)]
    return [(_remote(src, land.at[4 * x + 2 * y + c], send[k], recv[k], p),
             _remote(src, land.at[4 * p[0] + 2 * p[1] + p[2]], send[k], recv[k], p)) for k, p in enumerate(peers)]


def _gather_stage2(refs, send, recv):
    (land,) = refs
    x, y, c = _position()
    out = []
    for j, (px, py) in enumerate(_other_chips(x, y)):
        mine, theirs = land.at[4 * px + 2 * py + c], land.at[4 * px + 2 * py + 1 - c]
        out.append((_remote(mine, mine, send[j], recv[j], (x, y, 1 - c)),
                    _remote(theirs, theirs, send[j], recv[j], (x, y, 1 - c))))
    return out


def _flips():
    return [(a, b, d) for a in (0, 1) for b in (0, 1) for d in (0, 1) if a | b | d]


def _gather_direct(refs, send, recv):
    src, land = refs
    x, y, c = _position()
    out = []
    for k, (fx, fy, fc) in enumerate(_flips()):
        p = (1 - x if fx else x, 1 - y if fy else y, 1 - c if fc else c)
        out.append((_remote(src, land.at[4 * x + 2 * y + c], send[k], recv[k], p),
                    _remote(src, land.at[4 * p[0] + 2 * p[1] + p[2]], send[k], recv[k], p)))
    return out


def _scatter_direct(refs, send, recv):
    sendbuf, land = refs
    x, y, c = _position()
    me = 4 * x + 2 * y + c
    out = []
    for k, (fx, fy, fc) in enumerate(_flips()):
        p = (1 - x if fx else x, 1 - y if fy else y, 1 - c if fc else c)
        peer = 4 * p[0] + 2 * p[1] + p[2]
        out.append((_remote(sendbuf.at[peer], land.at[me], send[k], recv[k], p),
                    _remote(sendbuf.at[peer], land.at[peer], send[k], recv[k], p)))
    return out


def _slot_sum(name, table, first, count, packed, received, out_dtype):
    rows = packed.shape[1]
    blk = (1, _row_block(rows), LANES)
    with_recv = received is not None

    def body(tbl_ref, *refs):
        if with_recv:
            g_ref, r_ref, o_ref = refs
            o_ref[...] = (g_ref[...] + r_ref[...].astype(F32)).astype(o_ref.dtype)
        else:
            g_ref, o_ref = refs
            o_ref[...] = g_ref[...].astype(o_ref.dtype)

    in_specs = [pl.BlockSpec(blk, lambda r, i, tbl: (tbl[first + r], i, 0))]
    ins = [packed]
    if with_recv:
        in_specs.append(pl.BlockSpec(blk, lambda r, i, tbl: (first + r, i, 0)))
        ins.append(received)
    return pl.pallas_call(
        body, name=name,
        grid_spec=pltpu.PrefetchScalarGridSpec(
            num_scalar_prefetch=1, grid=(count, rows // blk[1]), in_specs=in_specs,
            out_specs=pl.BlockSpec(blk, lambda r, i, tbl: (r, i, 0))),
        out_shape=jax.ShapeDtypeStruct((count, rows, LANES), out_dtype),
        compiler_params=_params(("parallel", "parallel")),
    )(table, *ins)


def _own_plus_slots(name, me, packed, landed):
    n, rows, _ = landed.shape
    tr = _row_block(rows)

    def body(me_ref, g_ref, l_ref, o_ref):
        acc = g_ref[0]
        for j in range(n):
            acc = acc + l_ref[j].astype(F32)
        o_ref[...] = acc

    return pl.pallas_call(
        body, name=name,
        grid_spec=pltpu.PrefetchScalarGridSpec(
            num_scalar_prefetch=1, grid=(rows // tr,),
            in_specs=[pl.BlockSpec((1, tr, LANES), lambda i, me: (me[0], i, 0)),
                      pl.BlockSpec((n, tr, LANES), lambda i, me: (0, i, 0))],
            out_specs=pl.BlockSpec((tr, LANES), lambda i, me: (i, 0))),
        out_shape=jax.ShapeDtypeStruct((rows, LANES), F32), compiler_params=_params(("parallel",)),
    )(me, packed, landed)


def _sum_slots(name, stack):
    n, rows, _ = stack.shape

    def body(s_ref, o_ref):
        acc = s_ref[0]
        for j in range(1, n):
            acc = acc + s_ref[j]
        o_ref[...] = acc

    return pl.pallas_call(
        body, name=name, in_specs=[pl.BlockSpec(stack.shape, lambda: (0, 0, 0))],
        out_specs=pl.BlockSpec((rows, LANES), lambda: (0, 0)), out_shape=jax.ShapeDtypeStruct((rows, LANES), F32),
    )(stack)


def adamw(name, w, g, m, v):
    shape = w.shape
    last = shape[-1]
    w2, g2, m2, v2 = [a.reshape(-1, last) for a in (w, g, m, v)]
    rows = w2.shape[0]
    tr = 256 if rows % 256 == 0 else rows

    def body(w_ref, g_ref, m_ref, v_ref, d_ref, nm_ref, nv_ref):
        gg = g_ref[...]
        nm = ADAM_B1 * m_ref[...] + (1.0 - ADAM_B1) * gg
        nv = ADAM_B2 * v_ref[...] + (1.0 - ADAM_B2) * (gg * gg)
        m_hat = nm / (1.0 - ADAM_B1 ** ADAM_STEP)
        v_hat = nv / (1.0 - ADAM_B2 ** ADAM_STEP)
        d_ref[...] = -ADAM_LR * (m_hat / (jnp.sqrt(v_hat) + ADAM_EPS) + ADAM_WD * w_ref[...])
        nm_ref[...] = nm
        nv_ref[...] = nv

    spec = pl.BlockSpec((tr, last), lambda i: (i, 0))
    outs = pl.pallas_call(
        body, name=name, grid=(rows // tr,), in_specs=[spec] * 4, out_specs=[spec] * 3,
        out_shape=[jax.ShapeDtypeStruct((rows, last), F32)] * 3, compiler_params=_params(("parallel",)),
    )(w2, g2, m2, v2)
    return [o.reshape(shape) for o in outs]


def _landing(block_shape, dtype, own):
    x, y, c = _position()
    return lax.dynamic_update_slice(lax.empty((N_DEV,) + block_shape, dtype), own[None], (4 * x + 2 * y + c, 0, 0))


class _Exchanges:
    def __init__(self, shards):
        self.shards = shards
        self.pending = {}
        self.reduced = {}

    def first_weights(self):
        names = GROUPS["ffn1_gu"]
        return unpack_full_weights(all_gather("ag_ffn1_gu", pack_weight_shards(self.shards, names)), names)

    def rest_started(self):
        tokens = []
        block = pack_weight_shards(self.shards, GROUPS["ffn1_d"])
        sems, bufs, token = split_start("ag_ffn1_d_s", [block, _landing(block.shape, block.dtype, block)], N_DEV - 1,
                                        _gather_direct)
        self.pending["ffn1_d"] = (sems, bufs)
        tokens.append(token)
        for key in ("mid", "ffn2"):
            block = pack_weight_shards(self.shards, GROUPS[key])
            sems, bufs, token = split_start(f"ag_{key}_s1", [block, _landing(block.shape, block.dtype, block)], 4,
                                            _gather_stage1)
            self.pending[key] = (sems, bufs)
            tokens.append(token)
        return tuple(tokens)

    def ffn1_down(self, after):
        sems, bufs = self.pending.pop("ffn1_d")
        _, gathered = split_wait("ag_ffn1_d_w", bufs, sems, N_DEV - 1, _gather_direct, after)
        return unpack_full_weights(gathered, GROUPS["ffn1_d"])["ffn1_w_down"]

    def halfway(self, key, after):
        sems, bufs = self.pending.pop(key)
        _, land = split_wait(f"ag_{key}_w1", bufs, sems, 4, _gather_stage1, after)
        sems, bufs, token = split_start(f"ag_{key}_s2", [land], 3, _gather_stage2)
        self.pending[key] = (sems, bufs)
        return (token,)

    def weights(self, key, after):
        sems, bufs = self.pending.pop(key)
        (gathered,) = split_wait(f"ag_{key}_w2", bufs, sems, 3, _gather_stage2, after)
        w = unpack_full_weights(gathered, GROUPS[key])
        if "in_ab" in w:
            w["in_ab"] = jnp.pad(w["in_ab"], ((0, 128 - 2 * DN_HEADS), (0, 0)))
        return w

    def grads_out(self, key, grads):
        packed = pack_full_grads(grads, GROUPS[key])
        wire = _slot_sum(f"rs_{key}_wire", jnp.arange(N_DEV, dtype=jnp.int32), 0, N_DEV, packed, None, WIRE)
        land = _landing(wire.shape[1:], WIRE, jnp.zeros(wire.shape[1:], WIRE))
        sems, bufs, token = split_start(f"rs_{key}_start", [wire, land], N_DEV - 1, _scatter_direct)
        self.pending[key] = (sems, bufs, packed)
        return (token,)

    def grads_in(self, key, after):
        sems, bufs, packed = self.pending.pop(key)
        x, y, c = _position()
        _, landed = split_wait(f"rs_{key}_wait", bufs, sems, N_DEV - 1, _scatter_direct, after)
        me = jnp.reshape(4 * x + 2 * y + c, (1,)).astype(jnp.int32)
        total = _own_plus_slots(f"rs_{key}_sum", me, packed, landed)
        self.reduced.update(unpack_grad_shards(total, GROUPS[key]))
        return total

    def small_out(self, values):
        block = pack_small(values)
        sems, bufs, token = split_start("ag_small_s", [block, _landing(block.shape, block.dtype, block)], N_DEV - 1,
                                        _gather_direct)
        self.pending["small"] = (sems, bufs)
        return (token,)

    def small_in(self, after):
        sems, bufs = self.pending.pop("small")
        _, gathered = split_wait("ag_small_w", bufs, sems, N_DEV - 1, _gather_direct, after)
        return unpack_small(_sum_slots("small_sum", gathered))


def kernel(x, mem, ffn1_w_gate, ffn1_w_up, ffn1_w_down, ln1_g, ln1_b, w_in, conv_w, a_log, dt_bias, dn_norm_w, w_dn_branch, pool_w, pool_scale, w_pool_branch, w_mix_out, ln2_g, ln2_b, mem_ln_g, mem_ln_b, xa_wq, xa_wk, xa_wv, xa_wo, ln3_g, ln3_b, ffn2_w_gate, ffn2_w_up, ffn2_w_down, ln4_g, ln4_b, loss_target, m_ffn1_w_gate, m_ffn1_w_up, m_ffn1_w_down, m_ln1_g, m_ln1_b, m_w_in, m_conv_w, m_a_log, m_dt_bias, m_dn_norm_w, m_w_dn_branch, m_pool_w, m_pool_scale, m_w_pool_branch, m_w_mix_out, m_ln2_g, m_ln2_b, m_mem_ln_g, m_mem_ln_b, m_xa_wq, m_xa_wk, m_xa_wv, m_xa_wo, m_ln3_g, m_ln3_b, m_ffn2_w_gate, m_ffn2_w_up, m_ffn2_w_down, m_ln4_g, m_ln4_b, v_ffn1_w_gate, v_ffn1_w_up, v_ffn1_w_down, v_ln1_g, v_ln1_b, v_w_in, v_conv_w, v_a_log, v_dt_bias, v_dn_norm_w, v_w_dn_branch, v_pool_w, v_pool_scale, v_w_pool_branch, v_w_mix_out, v_ln2_g, v_ln2_b, v_mem_ln_g, v_mem_ln_b, v_xa_wq, v_xa_wk, v_xa_wv, v_xa_wo, v_ln3_g, v_ln3_b, v_ffn2_w_gate, v_ffn2_w_up, v_ffn2_w_down, v_ln4_g, v_ln4_b):
    given = dict(locals())
    shards = {n: given[n] for n in WEIGHT_NAMES}
    io = _Exchanges({n: shards[n][0] for n, _, _ in SHARDED})
    w = io.first_weights()
    for n in REPLICATED:
        w[n] = shards[n][0] if n == "pool_w" else shards[n]
    loss_part, grad_x, g = local_step(x[0], mem[0], loss_target[0], w, io)

    grad, updates = {}, {}

    def update(names, reduced):
        for n in names:
            grad[n] = reduced[n].reshape(shards[n].shape)
            updates[n] = adamw("adamw_" + n, shards[n], grad[n], given["m_" + n], given["v_" + n])
        return updates[names[-1]][0]

    update(GROUPS["ffn2"] + GROUPS["xa"], io.reduced)
    io.grads_in("mixer", grad_x)
    done = update(GROUPS["mixer"], io.reduced)
    small = io.small_in(done)
    loss = small.pop("loss")[0]
    done = update(REPLICATED, small)
    io.grads_in("ffn1_d", done)
    done = update(GROUPS["ffn1_d"], io.reduced)
    io.grads_in("ffn1_gu", done)
    update(GROUPS["ffn1_gu"], io.reduced)
    return (loss, grad_x[None], *[grad[n] for n in WEIGHT_NAMES], *[updates[n][0] for n in WEIGHT_NAMES],
            *[updates[n][1] for n in WEIGHT_NAMES], *[updates[n][2] for n in WEIGHT_NAMES])
```

```python
import functools

import jax
import jax.numpy as jnp
from jax import lax
from jax.experimental import pallas as pl
from jax.experimental.pallas import tpu as pltpu

F32 = jnp.float32
BF16 = jnp.bfloat16
MMD = BF16
WIRE = BF16
HI = lax.Precision.HIGHEST
X3 = lax.Precision.HIGH
VMEM_LIMIT_BYTES = 48 * 1024 * 1024

D_MODEL = 1024
D_FF = 2816
CHUNK = 64
N_MEM = 256
DN_HEADS = 4
HD = 128
DN_WIDTH = 512
POOL_WINDOWS = (2, 4, 8, 16)
POOL_WIDTH = 512
XA_HEADS = 4
XA_HD = 256
LN_EPS = 1e-5
RMS_EPS = 1e-6
L2_EPS = 1e-6
ALPHA = 2.0 ** 0.25
HALO = 16

ADAM_LR = 0.001
ADAM_B1 = 0.9
ADAM_B2 = 0.999
ADAM_EPS = 1e-08
ADAM_WD = 0.01
ADAM_STEP = 10

N_DEV = 8
LANES = 1024
ANY = pl.BlockSpec(memory_space=pl.ANY)


def _dot(a, b, ca, cb, prec):
    dn = (((ca,), (cb,)), ((), ()))
    if prec is not None:
        return lax.dot_general(a.astype(F32), b.astype(F32), dn, precision=prec, preferred_element_type=F32)
    return lax.dot_general(a.astype(MMD), b.astype(MMD), dn, preferred_element_type=F32)


def dnn(a, b, prec=None):
    return _dot(a, b, 1, 0, prec)


def dnt(a, b, prec=None):
    return _dot(a, b, 1, 1, prec)


def dtn(a, b, prec=None):
    return _dot(a, b, 0, 0, prec)


def _sigmoid(x):
    return jax.nn.sigmoid(x)


def _silu(x):
    return x * _sigmoid(x)


def _dsilu(x):
    s = _sigmoid(x)
    return s * (1.0 + x * (1.0 - s))


def _softplus(x):
    return jnp.maximum(x, 0.0) + jnp.log1p(jnp.exp(-jnp.abs(x)))


def _iota(shape, dim):
    return lax.broadcasted_iota(jnp.int32, shape, dim)


def _rsum(x):
    return jnp.sum(x, axis=1, keepdims=True)


def _csum(x):
    return jnp.sum(x, axis=0, keepdims=True)


def _pick(n, cands):
    for c in cands:
        if n % c == 0:
            return c
    return n


def _params(sem):
    return pltpu.CompilerParams(dimension_semantics=sem, vmem_limit_bytes=VMEM_LIMIT_BYTES)


def mm(name, a, b, *, ta=False, tb=False, out_dtype=F32, add=None, scale=None, deps=()):
    if ta:
        a, ta = a.astype(MMD).T, False
    if ta:
        kc, m = a.shape
    else:
        m, kc = a.shape
    if tb:
        n, kb = b.shape
    else:
        kb, n = b.shape
    assert kc == kb, (name, a.shape, b.shape)
    tm = m if m <= 512 else _pick(m, (512, 256, 128))
    tn = n if n <= 1024 else _pick(n, (1024, 1408, 768, 512))
    tk = kc if kc <= 1024 else _pick(kc, (1024, 1408, 768, 512))
    nk = kc // tk
    grid = (m // tm, n // tn, nk)
    a_spec = pl.BlockSpec((tk, tm), lambda i, j, k: (k, i)) if ta else pl.BlockSpec((tm, tk), lambda i, j, k: (i, k))
    b_spec = pl.BlockSpec((tn, tk), lambda i, j, k: (j, k)) if tb else pl.BlockSpec((tk, tn), lambda i, j, k: (k, j))
    o_spec = pl.BlockSpec((tm, tn), lambda i, j, k: (i, j))
    ca, cb = (0 if ta else 1), (1 if tb else 0)
    has_add = add is not None

    def body(*refs):
        a_ref, b_ref, o_ref, acc_ref = refs[0], refs[1], refs[-2], refs[-1]
        add_ref = refs[2] if has_add else None
        k = pl.program_id(2)

        @pl.when(k == 0)
        def _():
            acc_ref[...] = jnp.zeros_like(acc_ref)

        acc_ref[...] += _dot(a_ref[...], b_ref[...], ca, cb, None)

        @pl.when(k == nk - 1)
        def _():
            r = acc_ref[...]
            if scale is not None:
                r = r * scale
            if has_add:
                r = r + add_ref[...]
            o_ref[...] = r.astype(o_ref.dtype)

    ins = [a, b] + ([add] if has_add else []) + list(deps)
    specs = [a_spec, b_spec] + ([o_spec] if has_add else []) + [ANY] * len(deps)
    return pl.pallas_call(
        body, name=name, grid=grid, in_specs=specs, out_specs=o_spec,
        out_shape=jax.ShapeDtypeStruct((m, n), out_dtype),
        scratch_shapes=[pltpu.VMEM((tm, tn), F32)],
        compiler_params=_params(("parallel", "parallel", "arbitrary")),
    )(*ins)


class _Ctx:
    def __init__(self, i, nblk, tl):
        self.i, self.nblk, self.tl = i, nblk, tl


def _norm_item(it):
    if isinstance(it, tuple):
        a, w, j = it[:3]
        rows = it[3] if len(it) > 3 else None
        return a, w, j, rows
    return it, it.shape[-1], 0, None


def rowwise(name, fn, length, tl, *, rows=(), consts=(), prevs=(), nexts=(), out_rows=(), out_accs=(), deps=()):
    nblk = length // tl
    hb = tl // HALO
    nhalo = length // HALO
    arrays, specs = [], []
    for it in rows:
        a, w, j, r = _norm_item(it)
        if a.ndim == 3:
            specs.append(pl.BlockSpec((a.shape[0], tl, w), lambda i, j=j: (0, i, j)))
        else:
            specs.append(pl.BlockSpec((r or tl, w), lambda i, j=j: (i, j)))
        arrays.append(a)
    for a in consts:
        specs.append(pl.BlockSpec(a.shape, lambda i, nd=a.ndim: (0,) * nd))
        arrays.append(a)
    for it in prevs:
        a, w, j, _ = _norm_item(it)
        specs.append(pl.BlockSpec((HALO, w), lambda i, j=j: (jnp.maximum(i * hb - 1, 0), j)))
        arrays.append(a)
    for it in nexts:
        a, w, j, _ = _norm_item(it)
        specs.append(pl.BlockSpec((HALO, w), lambda i, j=j: (jnp.minimum((i + 1) * hb, nhalo - 1), j)))
        arrays.append(a)
    out_shape, out_specs = [], []
    for spec in out_rows:
        if len(spec) == 3:
            h, w, dt = spec
            out_shape.append(jax.ShapeDtypeStruct((h, length, w), dt))
            out_specs.append(pl.BlockSpec((h, tl, w), lambda i: (0, i, 0)))
        else:
            w, dt = spec
            out_shape.append(jax.ShapeDtypeStruct((length, w), dt))
            out_specs.append(pl.BlockSpec((tl, w), lambda i: (i, 0)))
    for shape, dt in out_accs:
        out_shape.append(jax.ShapeDtypeStruct(shape, dt))
        out_specs.append(pl.BlockSpec(shape, lambda i, nd=len(shape): (0,) * nd))
    n_r, n_c, n_p, n_n = len(rows), len(consts), len(prevs), len(nexts)
    n_in = n_r + n_c + n_p + n_n
    n_or = len(out_rows)
    arrays, specs = arrays + list(deps), specs + [ANY] * len(deps)

    def body(*refs):
        i = pl.program_id(0)
        vals = [r[...] for r in refs[:n_in]]
        outs = refs[n_in + len(deps):]
        ctx = _Ctx(i, nblk, tl)
        ro, ao = fn(ctx, vals[:n_r], vals[n_r:n_r + n_c], vals[n_r + n_c:n_r + n_c + n_p], vals[n_r + n_c + n_p:])
        for r, v in zip(outs[:n_or], ro, strict=True):
            r[...] = v.astype(r.dtype)
        for r, v in zip(outs[n_or:], ao, strict=True):
            @pl.when(i == 0)
            def _(r=r, v=v):
                r[...] = v.astype(r.dtype)

            @pl.when(i > 0)
            def _(r=r, v=v):
                r[...] += v.astype(r.dtype)

    res = pl.pallas_call(
        body, name=name, grid=(nblk,), in_specs=specs, out_specs=out_specs, out_shape=out_shape,
        compiler_params=_params(("arbitrary",) if out_accs else ("parallel",)),
    )(*arrays)
    return res


def _heads(x, n, w):
    return [x[:, h * w:(h + 1) * w] for h in range(n)]


def _cat(xs):
    return jnp.concatenate(xs, axis=1)


def _row_index(ctx, nrows, offset=0):
    return ctx.i * ctx.tl + offset + _iota((nrows, 1), 0)


def _ln_stats(r):
    mu = jnp.mean(r, axis=1, keepdims=True)
    d = r - mu
    var = jnp.mean(d * d, axis=1, keepdims=True)
    rstd = lax.rsqrt(var + LN_EPS)
    return d * rstd, rstd


def ln_fwd(name, terms, g, b, tl=256, deps=()):
    coefs = [c for c, _ in terms]
    length = terms[0][1].shape[0]

    def fn(ctx, rows, consts, prevs, nexts):
        r = sum(c * t for c, t in zip(coefs, rows))
        xh, _ = _ln_stats(r)
        return [xh * consts[0] + consts[1], r], []

    return rowwise(name, fn, length, min(tl, length), rows=[t for _, t in terms], consts=[g, b],
                   out_rows=[(D_MODEL, F32), (D_MODEL, F32)], deps=deps)


def ln_bwd(name, r, terms, g, tl=256, deps=()):
    coefs = [c for c, _ in terms]
    length = r.shape[0]

    def fn(ctx, rows, consts, prevs, nexts):
        xh, rstd = _ln_stats(rows[0])
        dy = sum(c * t for c, t in zip(coefs, rows[1:]))
        dxh = dy * consts[0]
        dr = rstd * (dxh - jnp.mean(dxh, axis=1, keepdims=True) - xh * jnp.mean(dxh * xh, axis=1, keepdims=True))
        return [dr], [_csum(dy * xh), _csum(dy)]

    return rowwise(name, fn, length, min(tl, length), rows=[r] + [t for _, t in terms], consts=[g],
                   out_rows=[(D_MODEL, F32)], out_accs=[((1, D_MODEL), F32), ((1, D_MODEL), F32)], deps=deps)


def ln_loss(name, terms, g, b, target, tl=256):
    coefs = [c for c, _ in terms]
    length = target.shape[0]
    nt = len(terms)

    def fn(ctx, rows, consts, prevs, nexts):
        r = sum(c * t for c, t in zip(coefs, rows[:nt]))
        xh, _ = _ln_stats(r)
        err = xh * consts[0] + consts[1] - rows[nt]
        tot = _csum(_rsum(err * err)) * (0.5 / D_MODEL)
        return [err * (1.0 / D_MODEL), r], [jnp.broadcast_to(tot, (1, 128))]

    return rowwise(name, fn, length, min(tl, length), rows=[t for _, t in terms] + [target], consts=[g, b],
                   out_rows=[(D_MODEL, F32), (D_MODEL, F32)], out_accs=[((1, 128), F32)])


def axpy(name, terms, tl=256):
    coefs = [c for c, _ in terms]
    length, width = terms[0][1].shape

    def fn(ctx, rows, consts, prevs, nexts):
        return [sum(c * t for c, t in zip(coefs, rows))], []

    return rowwise(name, fn, length, min(tl, length), rows=[t for _, t in terms], out_rows=[(width, F32)])[0]


def ffn_fwd(tag, x, wg, wu, wd, deps=()):
    length = x.shape[0]
    hg = mm(tag + "_gate", x, wg, tb=True, deps=deps)
    hu = mm(tag + "_up", x, wu, tb=True)
    if callable(wd):
        wd = wd(hu)

    def fn(ctx, rows, consts, prevs, nexts):
        return [_silu(rows[0]) * rows[1]], []

    act = rowwise(tag + "_act", fn, length, min(256, length), rows=[hg, hu], out_rows=[(D_FF, BF16)])[0]
    f = mm(tag + "_down", act, wd)
    return f, (hg, hu, act), wd


def ffn_bwd(tag, x, res, dr, wg, wu, wd, deps=(), on_dwd=None, on_dwgu=None):
    hg, hu, act = res
    length = x.shape[0]
    dwd = mm(tag + "_dwd", act, dr, ta=True, scale=0.5, deps=deps)
    dact = mm(tag + "_dact", dr, wd, tb=True, scale=0.5, deps=on_dwd(dwd) if on_dwd else ())

    def fn(ctx, rows, consts, prevs, nexts):
        g, u, da = rows
        return [da * u * _dsilu(g), da * _silu(g)], []

    dhg, dhu = rowwise(tag + "_dactb", fn, length, min(256, length), rows=[hg, hu, dact],
                       out_rows=[(D_FF, BF16), (D_FF, BF16)])
    dwg = mm(tag + "_dwg", dhg, x, ta=True)
    dwu = mm(tag + "_dwu", dhu, x, ta=True)
    dx = mm(tag + "_dxg", dhg, wg, deps=on_dwgu(dwg, dwu) if on_dwgu else ())
    dx = mm(tag + "_dxu", dhu, wu, add=dx)
    return dx, dwg, dwu, dwd


def _conv_taps(ext, taps, n):
    out = taps[3] * ext
    for j in range(3):
        out = out + taps[j] * pltpu.roll(ext, 3 - j, 0)
    return out


def _l2n(x):
    r = lax.rsqrt(_rsum(x * x) + L2_EPS)
    return x * r, r


def conv_fwd(name, pre, taps, tl=256):
    length = pre.shape[0]
    tl = min(tl, length)

    def fn(ctx, rows, consts, prevs, nexts):
        prev = jnp.where(ctx.i > 0, prevs[0], 0.0)
        ext = jnp.concatenate([prev, rows[0]], axis=0)
        s = _silu(_conv_taps(ext, consts, tl + HALO)[HALO:])
        q = _cat([_l2n(x)[0] * (HD ** -0.5) for x in _heads(s[:, :DN_WIDTH], DN_HEADS, HD)])
        k = _cat([_l2n(x)[0] for x in _heads(s[:, DN_WIDTH:2 * DN_WIDTH], DN_HEADS, HD)])
        return [q, k, s[:, 2 * DN_WIDTH:]], []

    return rowwise(name, fn, length, tl, rows=[pre], consts=list(taps), prevs=[pre],
                   out_rows=[(DN_WIDTH, F32)] * 3)


def conv_bwd(name, pre, dq, dk, dv, taps, tl=256):
    length = pre.shape[0]
    tl = min(tl, length)
    n = tl + 2 * HALO

    def fn(ctx, rows, consts, prevs, nexts):
        last = ctx.i == ctx.nblk - 1
        prev = jnp.where(ctx.i > 0, prevs[0], 0.0)
        ext = jnp.concatenate([prev, rows[0], nexts[0]], axis=0)
        c = _conv_taps(ext, consts, n)
        s = _silu(c)
        zero = jnp.zeros((HALO, DN_WIDTH), F32)
        dqe, dke, dve = [jnp.concatenate([zero, rows[1 + t], jnp.where(last, 0.0, nexts[1 + t])], axis=0)
                         for t in range(3)]

        def l2_bwd(x, dy):
            y, r = _l2n(x)
            return r * (dy - y * _rsum(dy * y))

        dsq = _cat([l2_bwd(x, d * (HD ** -0.5)) for x, d in zip(_heads(s[:, :DN_WIDTH], DN_HEADS, HD),
                                                                 _heads(dqe, DN_HEADS, HD))])
        dsk = _cat([l2_bwd(x, d) for x, d in zip(_heads(s[:, DN_WIDTH:2 * DN_WIDTH], DN_HEADS, HD),
                                                  _heads(dke, DN_HEADS, HD))])
        dc = _cat([dsq, dsk, dve]) * _dsilu(c)
        dpre = consts[3] * dc
        for j in range(3):
            dpre = dpre + consts[j] * pltpu.roll(dc, n - (3 - j), 0)
        dc_cur = dc[HALO:HALO + tl]
        dws = [_csum(dc_cur * pltpu.roll(ext, 3 - j, 0)[HALO:HALO + tl]) for j in range(3)]
        dws.append(_csum(dc_cur * ext[HALO:HALO + tl]))
        return [dpre[HALO:HALO + tl]], dws

    return rowwise(name, fn, length, tl, rows=[pre, dq, dk, dv], consts=list(taps), prevs=[pre],
                   nexts=[pre, dq, dk, dv], out_rows=[(3 * DN_WIDTH, BF16)],
                   out_accs=[((1, 3 * DN_WIDTH), F32)] * 4)


def _gate_consts():
    lane = jnp.arange(128)[:, None]
    col = jnp.arange(2 * DN_WIDTH)[None, :]
    sel = ((lane < 2 * DN_HEADS) & (col // HD == lane)).astype(F32)
    pick = ((col.T == lane.T * HD) & (lane.T < 2 * DN_HEADS)).astype(F32)
    return sel, pick


def _gate_math(ab, alog, dtb):
    z = ab + dtb
    g = -jnp.exp(alog) * _softplus(z)
    beta = _sigmoid(ab)
    return z, g, beta


def gates_fwd(name, ab, alog, dtb, sel, tl=256):
    length = ab.shape[0]

    def fn(ctx, rows, consts, prevs, nexts):
        _, g, beta = _gate_math(rows[0], consts[0], consts[1])
        lane = _iota(g.shape, 1)
        small = jnp.where(lane < DN_HEADS, g, jnp.where(lane < 2 * DN_HEADS, beta, 0.0))
        big = dnn(small, consts[2], HI)
        return [big[:, :DN_WIDTH], big[:, DN_WIDTH:]], []

    return rowwise(name, fn, length, min(tl, length), rows=[ab], consts=[alog, dtb, sel],
                   out_rows=[(DN_WIDTH, F32)] * 2)


def gates_bwd(name, ab, dgb, dbb, alog, dtb, pick, tl=256):
    length = ab.shape[0]

    def fn(ctx, rows, consts, prevs, nexts):
        z, g, beta = _gate_math(rows[0], consts[0], consts[1])
        dsmall = dnn(_cat([rows[1], rows[2]]), consts[2], HI)
        lane = _iota(g.shape, 1)
        is_a = lane < DN_HEADS
        da = jnp.where(is_a, dsmall * (-jnp.exp(consts[0])) * _sigmoid(z), 0.0)
        db = jnp.where((lane >= DN_HEADS) & (lane < 2 * DN_HEADS), dsmall * beta * (1.0 - beta), 0.0)
        return [da + db], [_csum(jnp.where(is_a, dsmall * g, 0.0)), _csum(da)]

    return rowwise(name, fn, length, min(tl, length), rows=[ab, dgb, dbb], consts=[alog, dtb, pick],
                   out_rows=[(128, BF16)], out_accs=[((1, 128), F32)] * 2)


CPS = 2


def _chunk_scan_rows(x, suffix=False):
    n = x.shape[0]
    rc = _iota(x.shape, 0) & (CHUNK - 1)
    sh = 1
    while sh < CHUNK:
        if suffix:
            x = x + jnp.where(rc < CHUNK - sh, pltpu.roll(x, n - sh, 0), 0.0)
        else:
            x = x + jnp.where(rc >= sh, pltpu.roll(x, sh, 0), 0.0)
        sh *= 2
    return x


def _tri_inv(a_list, eye, bd):
    def each(f, *ls):
        return [f(*xs) for xs in zip(*ls)]

    dg = [jnp.where(bd, a, 0.0) for a in a_list]
    lo = each(lambda a, d: a - d, a_list, dg)
    n1 = [-d for d in dg]
    n2 = each(lambda n: dnn(n, n, X3), n1)
    n4 = each(lambda n: dnn(n, n, X3), n2)
    td = each(lambda p, s: dnn(eye + p, eye + s, X3), n1, n2)
    n8 = each(lambda n: dnn(n, n, X3), n4)
    td = each(lambda t, n: dnn(t, eye + n, X3), td, n4)
    td = each(lambda t, n: dnn(t, eye + n, X3), td, n8)
    m = each(lambda t, l: dnn(t, l, X3), td, lo)
    m2 = each(lambda x: dnn(x, x, X3), m)
    x = each(lambda p, s: dnn(eye - p, eye + s, X3), m, m2)
    return each(lambda p, t: dnn(p, t, X3), x, td)


def _chunk_common(q, k, v, gcb, bb):
    egb = jnp.exp(gcb)
    gc64 = gcb[:, :CHUNK]
    ii, jj = _iota((CHUNK, CHUNK), 0), _iota((CHUNK, CHUNK), 1)
    incl, strict = ii >= jj, ii > jj
    decay = jnp.exp(jnp.where(incl, gc64 - gc64.T, -jnp.inf))
    kb = k * bb
    vb = v * bb
    kbe = kb * egb
    pq = dnt(jnp.concatenate([kb, q], axis=0), k, X3)
    ekb = jnp.exp(gcb[CHUNK - 1:CHUNK, :] - gcb)
    return dict(egb=egb, decay=decay, kb=kb, vb=vb, kbe=kbe, pm=pq[:CHUNK], qm=pq[CHUNK:], ekb=ekb,
                incl=incl, strict=strict, ii=ii, jj=jj)


def _chunk_head(vals, ci, h):
    return [v[ci * CHUNK:(ci + 1) * CHUNK, h * HD:(h + 1) * HD] for v in vals]


def _assemble(per_chunk):
    return jnp.concatenate([_cat(hs) for hs in per_chunk], axis=0)


def _assemble3(per_chunk):
    return jnp.stack([jnp.concatenate([per_chunk[ci][h] for ci in range(CPS)], axis=0) for h in range(DN_HEADS)])


def delta_prep_fwd(name, q, k, v, gb, bb):
    length = q.shape[0]

    def fn(ctx, rows, consts, prevs, nexts):
        gcb_all = _chunk_scan_rows(rows[3])
        vals = [rows[0], rows[1], rows[2], gcb_all, rows[4]]
        units = [(ci, h) for ci in range(CPS) for h in range(DN_HEADS)]
        ins = [_chunk_head(vals, ci, h) for ci, h in units]
        cs = [_chunk_common(*i) for i in ins]
        eye = (cs[0]["ii"] == cs[0]["jj"]).astype(F32)
        ts = _tri_inv([jnp.where(c["strict"], c["pm"] * c["decay"], 0.0) for c in cs], eye,
                      (cs[0]["ii"] >> 4) == (cs[0]["jj"] >> 4))
        uws = [dnn(t, _cat([c["vb"], c["kbe"]]), X3) for t, c in zip(ts, cs)]

        def grid2(xs):
            return [xs[ci * DN_HEADS:(ci + 1) * DN_HEADS] for ci in range(CPS)]

        return [_assemble(grid2([uw[:, :HD] for uw in uws])), _assemble(grid2([uw[:, HD:] for uw in uws])),
                _assemble(grid2([i[0] * c["egb"] for i, c in zip(ins, cs)])),
                _assemble(grid2([i[1] * c["ekb"] for i, c in zip(ins, cs)])), gcb_all,
                _assemble3(grid2([c["qm"] * c["decay"] for c in cs])), _assemble3(grid2(ts))], []

    return rowwise(name, fn, length, CHUNK * CPS, rows=[q, k, v, gb, bb],
                   out_rows=[(DN_WIDTH, F32)] * 5 + [(DN_HEADS, CHUNK, F32)] * 2)


def delta_prep_bwd(name, q, k, v, gb, bb, t3, du, dw, dqd, dkd, dattn3, dgl):
    length = q.shape[0]

    def fn(ctx, rows, consts, prevs, nexts):
        gcb_all = _chunk_scan_rows(rows[3])
        vals = [rows[0], rows[1], rows[2], gcb_all] + list(rows[4:9])
        t3v, da3v, dglv = rows[9], rows[10], rows[11]
        units = [(ci, h) for ci in range(CPS) for h in range(DN_HEADS)]
        ins = [_chunk_head(vals, ci, h) for ci, h in units]
        cs = [_chunk_common(*i[:5]) for i in ins]
        ts = [t3v[h][ci * CHUNK:(ci + 1) * CHUNK] for ci, h in units]
        dattns = [jnp.where(c["incl"], da3v[h][ci * CHUNK:(ci + 1) * CHUNK], 0.0) for (ci, h), c in zip(units, cs)]
        duws = [_cat([i[5], i[6]]) for i in ins]
        dvks = [dtn(t, d, X3) for t, d in zip(ts, duws)]
        dts = [dnt(d, _cat([c["vb"], c["kbe"]]), X3) for d, c in zip(duws, cs)]
        dts = [dnt(d, t, X3) for d, t in zip(dts, ts)]
        das = [jnp.where(c["strict"], -dtn(t, d, X3), 0.0) for c, t, d in zip(cs, ts, dts)]
        dpqs = [jnp.concatenate([da * c["decay"], dat * c["decay"]], axis=0) for da, dat, c in zip(das, dattns, cs)]
        dpqks = [dnn(d, i[1], X3) for d, i in zip(dpqs, ins)]
        dkps = [dtn(d, jnp.concatenate([c["kb"], i[0]], axis=0), X3) for d, c, i in zip(dpqs, cs, ins)]
        dqs, dks, dvs, dgcs, dbs = [], [], [], [], []
        for (ci, h), i, c, dvk, da, dattn, dpqk, dkp in zip(units, ins, cs, dvks, das, dattns, dpqks, dkps):
            qh, kh, vh, _, bh, _, _, dqdh, dkdh = i
            dvb, dkbe = dvk[:, :HD], dvk[:, HD:]
            dkb = dpqk[:CHUNK] + dkbe * c["egb"]
            c1 = _rsum(dkbe * c["kb"] + dqdh * qh) * c["egb"]
            c2 = _rsum(dkdh * kh) * c["ekb"]
            e = (da * c["pm"] + dattn * c["qm"]) * c["decay"]
            dgc = c1 - c2 + _rsum(e) - _rsum(e.T)
            dgl_tot = jnp.max(dglv[ci * 8:(ci + 1) * 8, h * HD:(h + 1) * HD], axis=0, keepdims=True) + _csum(c2)
            dgcs.append(dgc + jnp.where(_iota((CHUNK, HD), 0) == CHUNK - 1, dgl_tot, 0.0))
            dqs.append(dpqk[CHUNK:] + dqdh * c["egb"])
            dks.append(dkp + dkdh * c["ekb"] + dkb * bh)
            dvs.append(dvb * bh)
            dbs.append(jnp.broadcast_to(_rsum(dkb * kh + dvb * vh), (CHUNK, HD)))

        def grid2(xs):
            return [xs[ci * DN_HEADS:(ci + 1) * DN_HEADS] for ci in range(CPS)]

        return [_assemble(grid2(dqs)), _assemble(grid2(dks)), _assemble(grid2(dvs)),
                _chunk_scan_rows(_assemble(grid2(dgcs)), suffix=True), _assemble(grid2(dbs))], []

    return rowwise(name, fn, length, CHUNK * CPS,
                   rows=[q, k, v, gb, bb, du, dw, dqd, dkd, t3, dattn3, (dgl, DN_WIDTH, 0, 8 * CPS)],
                   out_rows=[(DN_WIDTH, F32)] * 5)


def delta_scan_fwd(name, qd, kd, u, w, attn3, gcb):
    length = qd.shape[0]
    n = length // CHUNK
    row = pl.BlockSpec((CHUNK, DN_WIDTH), lambda c: (c, 0))
    sq = pl.BlockSpec((DN_HEADS, CHUNK, CHUNK), lambda c: (0, c, 0))

    def body(qd_ref, kd_ref, u_ref, w_ref, attn_ref, gc_ref, o_ref, vn_ref, st_ref, s_ref):
        c = pl.program_id(0)

        @pl.when(c == 0)
        def _():
            s_ref[...] = jnp.zeros_like(s_ref)

        for h in range(DN_HEADS):
            sl = pl.ds(h * HD, HD)
            s = s_ref[h]
            st_ref[0, h] = s
            vn = u_ref[:, sl] - dnn(w_ref[:, sl], s)
            o_ref[:, sl] = dnn(qd_ref[:, sl], s) + dnn(attn_ref[h], vn)
            vn_ref[:, sl] = vn
            egl = jnp.exp(gc_ref[pl.ds(CHUNK - 1, 1), sl])
            s_ref[h] = s * egl + dtn(kd_ref[:, sl], vn)

    return pl.pallas_call(
        body, name=name, grid=(n,), in_specs=[row, row, row, row, sq, row],
        out_specs=[row, row, pl.BlockSpec((1, DN_HEADS, HD, HD), lambda c: (c, 0, 0, 0))],
        out_shape=[jax.ShapeDtypeStruct((length, DN_WIDTH), F32), jax.ShapeDtypeStruct((length, DN_WIDTH), F32),
                   jax.ShapeDtypeStruct((n, DN_HEADS, HD, HD), F32)],
        scratch_shapes=[pltpu.VMEM((DN_HEADS, HD, HD), F32)],
        compiler_params=_params(("arbitrary",)),
    )(qd, kd, u, w, attn3, gcb)


def delta_scan_bwd(name, do, qd, kd, w, attn3, vn, st, gcb):
    length = qd.shape[0]
    n = length // CHUNK
    row = pl.BlockSpec((CHUNK, DN_WIDTH), lambda c: (n - 1 - c, 0))
    sq = pl.BlockSpec((DN_HEADS, CHUNK, CHUNK), lambda c: (0, n - 1 - c, 0))
    stb = pl.BlockSpec((1, DN_HEADS, HD, HD), lambda c: (n - 1 - c, 0, 0, 0))
    glb = pl.BlockSpec((8, DN_WIDTH), lambda c: (n - 1 - c, 0))

    def body(do_ref, qd_ref, kd_ref, w_ref, attn_ref, vn_ref, st_ref, gc_ref,
             dqd_ref, dkd_ref, du_ref, dw_ref, dattn_ref, dgl_ref, ds_ref):
        c = pl.program_id(0)

        @pl.when(c == 0)
        def _():
            ds_ref[...] = jnp.zeros_like(ds_ref)

        for h in range(DN_HEADS):
            sl = pl.ds(h * HD, HD)
            s = st_ref[0, h]
            dsn = ds_ref[h]
            d_o = do_ref[:, sl]
            vnh = vn_ref[:, sl]
            egl = jnp.exp(gc_ref[pl.ds(CHUNK - 1, 1), sl])
            dattn_ref[h] = dnt(d_o, vnh)
            dvn = dtn(attn_ref[h], d_o) + dnn(kd_ref[:, sl], dsn)
            dqd_ref[:, sl] = dnt(d_o, s)
            dkd_ref[:, sl] = dnt(vnh, dsn)
            du_ref[:, sl] = dvn
            dw_ref[:, sl] = -dnt(dvn, s)
            dgl_ref[:, sl] = jnp.broadcast_to(_csum(_rsum(dsn * s)) * egl, (8, HD))
            ds_ref[h] = dsn * egl + dtn(qd_ref[:, sl], d_o) - dtn(w_ref[:, sl], dvn)

    return pl.pallas_call(
        body, name=name, grid=(n,), in_specs=[row, row, row, row, sq, row, stb, row],
        out_specs=[row, row, row, row, sq, glb],
        out_shape=[jax.ShapeDtypeStruct((length, DN_WIDTH), F32)] * 4
        + [jax.ShapeDtypeStruct((DN_HEADS, length, CHUNK), F32), jax.ShapeDtypeStruct((n * 8, DN_WIDTH), F32)],
        scratch_shapes=[pltpu.VMEM((DN_HEADS, HD, HD), F32)],
        compiler_params=_params(("arbitrary",)),
    )(do, qd, kd, w, attn3, vn, st, gcb)


def onorm_fwd(name, o, z, nw, tl=256):
    length = o.shape[0]

    def fn(ctx, rows, consts, prevs, nexts):
        outs = []
        for oh, zh in zip(_heads(rows[0], DN_HEADS, HD), _heads(rows[1], DN_HEADS, HD)):
            r = lax.rsqrt(jnp.mean(oh * oh, axis=1, keepdims=True) + RMS_EPS)
            outs.append(oh * r * consts[0] * _silu(zh))
        return [_cat(outs)], []

    return rowwise(name, fn, length, min(tl, length), rows=[o, z], consts=[nw], out_rows=[(DN_WIDTH, BF16)])[0]


def onorm_bwd(name, o, z, d_on, nw, tl=256):
    length = o.shape[0]

    def fn(ctx, rows, consts, prevs, nexts):
        dos, dzs = [], []
        dnw = jnp.zeros((1, HD), F32)
        for oh, zh, dh in zip(*[_heads(r, DN_HEADS, HD) for r in rows]):
            r = lax.rsqrt(jnp.mean(oh * oh, axis=1, keepdims=True) + RMS_EPS)
            y = oh * r
            sz = _silu(zh)
            t = dh * sz * consts[0]
            dos.append(r * (t - y * jnp.mean(t * y, axis=1, keepdims=True)))
            dzs.append(dh * y * consts[0] * _dsilu(zh))
            dnw = dnw + _csum(dh * y * sz)
        return [_cat(dos), _cat(dzs)], [dnw]

    return rowwise(name, fn, length, min(tl, length), rows=[o, z, d_on], consts=[nw],
                   out_rows=[(DN_WIDTH, F32), (DN_WIDTH, BF16)], out_accs=[((1, HD), F32)])


def merge_fwd(name, gates, ydn, ypool, tl=256):
    length = ydn.shape[0]

    def fn(ctx, rows, consts, prevs, nexts):
        gt = rows[0]
        return [_sigmoid(gt[:, :D_MODEL]) * rows[1] + _sigmoid(gt[:, D_MODEL:]) * rows[2]], []

    return rowwise(name, fn, length, min(tl, length), rows=[gates, ydn, ypool], out_rows=[(D_MODEL, BF16)])[0]


def merge_bwd(name, gates, ydn, ypool, dm, tl=256):
    length = ydn.shape[0]

    def fn(ctx, rows, consts, prevs, nexts):
        gt, yd, yp, d = rows
        sd, sp = _sigmoid(gt[:, :D_MODEL]), _sigmoid(gt[:, D_MODEL:])
        dgates = _cat([d * yd * sd * (1.0 - sd), d * yp * sp * (1.0 - sp)])
        return [d * sd, d * sp, dgates], []

    return rowwise(name, fn, length, min(tl, length), rows=[gates, ydn, ypool, dm],
                   out_rows=[(D_MODEL, BF16), (D_MODEL, BF16), (2 * D_MODEL, BF16)])


def _trailing_sums(ext, upto):
    s, sh = ext, 1
    while sh < upto:
        s = s + pltpu.roll(s, sh, 0)
        sh *= 2
    return s


def _leading_sums(ext, upto, n):
    s, sh = ext, 1
    while sh < upto:
        s = s + pltpu.roll(s, n - sh, 0)
        sh *= 2
    return s


def _pool_mixed(ctx, p, prev, tl):
    prevm = jnp.where(ctx.i > 0, prev, 0.0)
    t1 = (_row_index(ctx, tl) + 1).astype(F32)
    outs = []
    for gi, win in enumerate(POOL_WINDOWS):
        sl = slice(gi * HD, (gi + 1) * HD)
        ext = jnp.concatenate([prevm[:, sl], p[:, sl]], axis=0)
        mean = _trailing_sums(ext, win)[HALO:] / jnp.minimum(t1, float(win))
        outs.append(mean - p[:, sl])
    return outs


def pool_fwd(name, p, pool_w, scale, tl=256):
    length = p.shape[0]
    tl = min(tl, length)

    def fn(ctx, rows, consts, prevs, nexts):
        mixed = _pool_mixed(ctx, rows[0], prevs[0], tl)
        y = _cat([dnn(m, consts[0][gi]) for gi, m in enumerate(mixed)])
        return [y * consts[1]], []

    return rowwise(name, fn, length, tl, rows=[p], consts=[pool_w, scale], prevs=[p],
                   out_rows=[(POOL_WIDTH, BF16)])[0]


def pool_bwd(name, p, dpo, pool_w, scale, tl=256):
    length = p.shape[0]
    tl = min(tl, length)
    n = tl + HALO

    def fn(ctx, rows, consts, prevs, nexts):
        last = ctx.i == ctx.nblk - 1
        mixed = _pool_mixed(ctx, rows[0], prevs[0], tl)
        dext = jnp.concatenate([rows[1], jnp.where(last, 0.0, nexts[0])], axis=0)
        t1 = (_row_index(ctx, n) + 1).astype(F32)
        dps, dws, dscs = [], [], []
        for gi, win in enumerate(POOL_WINDOWS):
            sl = slice(gi * HD, (gi + 1) * HD)
            wg = consts[0][gi]
            dyraw = dext[:, sl] * consts[1][:, sl]
            dmix = dnt(dyraw, wg)
            dws.append(dtn(mixed[gi], dyraw[:tl]))
            dscs.append(_csum(rows[1][:, sl] * dnn(mixed[gi], wg)))
            lead = _leading_sums(dmix / jnp.minimum(t1, float(win)), win, n)
            dps.append(lead[:tl] - dmix[:tl])
        return [_cat(dps)], [jnp.stack(dws), _cat(dscs)]

    return rowwise(name, fn, length, tl, rows=[p, dpo], consts=[pool_w, scale], prevs=[p], nexts=[dpo],
                   out_rows=[(POOL_WIDTH, BF16)],
                   out_accs=[((len(POOL_WINDOWS), HD, HD), F32), ((1, POOL_WIDTH), F32)])


def _xa_probs(qh, kh):
    s = dnt(qh, kh) * (XA_HD ** -0.5)
    e = jnp.exp(s - jnp.max(s, axis=1, keepdims=True))
    return e / _rsum(e)


def xattn_fwd(name, qx, kx, vx, tl=256):
    length = qx.shape[0]

    def fn(ctx, rows, consts, prevs, nexts):
        outs = [dnn(_xa_probs(qh, kh), vh) for qh, kh, vh in
                zip(_heads(rows[0], XA_HEADS, XA_HD), _heads(consts[0], XA_HEADS, XA_HD),
                    _heads(consts[1], XA_HEADS, XA_HD))]
        return [_cat(outs)], []

    return rowwise(name, fn, length, min(tl, length), rows=[qx], consts=[kx, vx], out_rows=[(D_MODEL, BF16)])[0]


def xattn_bwd(name, qx, dox, kx, vx, tl=256):
    length = qx.shape[0]

    def fn(ctx, rows, consts, prevs, nexts):
        dqs, dks, dvs = [], [], []
        for qh, dh, kh, vh in zip(_heads(rows[0], XA_HEADS, XA_HD), _heads(rows[1], XA_HEADS, XA_HD),
                                  _heads(consts[0], XA_HEADS, XA_HD), _heads(consts[1], XA_HEADS, XA_HD)):
            pr = _xa_probs(qh, kh)
            dpr = dnt(dh, vh)
            ds = pr * (dpr - _rsum(dpr * pr)) * (XA_HD ** -0.5)
            dqs.append(dnn(ds, kh))
            dks.append(dtn(ds, qh))
            dvs.append(dtn(pr, dh))
        return [_cat(dqs)], [_cat(dks), _cat(dvs)]

    return rowwise(name, fn, length, min(tl, length), rows=[qx, dox], consts=[kx, vx],
                   out_rows=[(D_MODEL, BF16)], out_accs=[((N_MEM, D_MODEL), F32)] * 2)


def local_step(x, mem, target, w, io):
    sel, pick = _gate_consts()
    alog = jnp.pad(w["a_log"], ((0, 0), (0, 128 - DN_HEADS)))
    dtb = jnp.pad(w["dt_bias"], ((0, 0), (0, 128 - DN_HEADS)))

    f1, res1, w_down1 = ffn_fwd("ffn1", x, w["ffn1_w_gate"], w["ffn1_w_up"], io.ffn1_down, deps=io.rest_started())
    x1, r1 = ln_fwd("ln1", [(ALPHA, x), (0.5, f1)], w["ln1_g"], w["ln1_b"], deps=io.halfway("mid", f1))
    w = dict(w, ffn1_w_down=w_down1, **io.weights("mid", x1))
    taps = [w["conv_w"][j:j + 1] for j in range(4)]

    pre = mm("in_qkv", x1, w["in_qkv"], tb=True)
    z = mm("in_z", x1, w["in_z"], tb=True)
    gates = mm("in_gates", x1, w["in_gates"], tb=True)
    p = mm("in_p", x1, w["in_p"], tb=True)
    ab = mm("in_ab", x1, w["in_ab"], tb=True)
    q, k, v = conv_fwd("conv", pre, taps)
    gb, bb = gates_fwd("gates", ab, alog, dtb, sel)
    u, wd_, qd, kd, gcb, attn3, t3 = delta_prep_fwd("dprep", q, k, v, gb, bb)
    o, vn, st = delta_scan_fwd("dscan", qd, kd, u, wd_, attn3, gcb)
    on = onorm_fwd("onorm", o, z, w["dn_norm_w"])
    ydn = mm("dn_branch", on, w["w_dn_branch"], tb=True)
    po = pool_fwd("pool", p, w["pool_w"], w["pool_scale"])
    ypool = mm("pool_branch", po, w["w_pool_branch"], tb=True)
    merged = merge_fwd("merge", gates, ydn, ypool)
    mix = mm("mix_out", merged, w["w_mix_out"])
    x2, r2 = ln_fwd("ln2", [(ALPHA, x1), (1.0, mix)], w["ln2_g"], w["ln2_b"])

    m, _ = ln_fwd("ln_mem", [(1.0, mem)], w["mem_ln_g"], w["mem_ln_b"])
    qx = mm("xa_q", x2, w["xa_wq"], deps=io.halfway("ffn2", x2))
    kx = mm("xa_k", m, w["xa_wk"])
    vx = mm("xa_v", m, w["xa_wv"])
    ox = xattn_fwd("xattn", qx, kx, vx)
    xa = mm("xa_o", ox, w["xa_wo"])
    x3, r3 = ln_fwd("ln3", [(ALPHA, x2), (1.0, xa)], w["ln3_g"], w["ln3_b"])
    w = dict(w, **io.weights("ffn2", x3))

    f2, res2, _ = ffn_fwd("ffn2", x3, w["ffn2_w_gate"], w["ffn2_w_up"], w["ffn2_w_down"])
    dy4, r4, loss = ln_loss("ln4_loss", [(ALPHA, x3), (0.5, f2)], w["ln4_g"], w["ln4_b"], target)

    g = {}
    dr4, g["ln4_g"], g["ln4_b"] = ln_bwd("ln4_b", r4, [(1.0, dy4)], w["ln4_g"])
    dx3, g["ffn2_w_gate"], g["ffn2_w_up"], g["ffn2_w_down"] = ffn_bwd(
        "ffn2b", x3, res2, dr4, w["ffn2_w_gate"], w["ffn2_w_up"], w["ffn2_w_down"])
    dep = io.grads_out("ffn2", g)
    dr3, g["ln3_g"], g["ln3_b"] = ln_bwd("ln3_b", r3, [(ALPHA, dr4), (1.0, dx3)], w["ln3_g"], deps=dep)

    dox = mm("xa_do", dr3, w["xa_wo"], tb=True)
    g["xa_wo"] = mm("xa_dwo", ox, dr3, ta=True)
    dqx, dkx, dvx = xattn_bwd("xattn_b", qx, dox, kx, vx)
    g["xa_wq"] = mm("xa_dwq", x2, dqx, ta=True)
    dx2 = mm("xa_dx", dqx, w["xa_wq"], tb=True)
    g["xa_wk"] = mm("xa_dwk", m, dkx, ta=True)
    g["xa_wv"] = mm("xa_dwv", m, dvx, ta=True)
    dmm = mm("xa_dmk", dkx, w["xa_wk"], tb=True, deps=io.grads_out("xa", g))
    dmm = mm("xa_dmv", dvx, w["xa_wv"], tb=True, add=dmm)
    _, g["mem_ln_g"], g["mem_ln_b"] = ln_bwd("ln_mem_b", mem, [(1.0, dmm)], w["mem_ln_g"])
    dr2, g["ln2_g"], g["ln2_b"] = ln_bwd("ln2_b", r2, [(ALPHA, dr3), (1.0, dx2)], w["ln2_g"])
    io.grads_in("ffn2", dr2)

    dmerged = mm("mix_dm", dr2, w["w_mix_out"], tb=True)
    g["w_mix_out"] = mm("mix_dw", merged, dr2, ta=True)
    d_ydn, d_ypool, d_gates = merge_bwd("merge_b", gates, ydn, ypool, dmerged)
    g["w_dn_branch"] = mm("dn_dw", d_ydn, on, ta=True)
    d_on = mm("dn_dx", d_ydn, w["w_dn_branch"])
    g["w_pool_branch"] = mm("pool_dw", d_ypool, po, ta=True)
    d_po = mm("pool_dx", d_ypool, w["w_pool_branch"])
    dp, g["pool_w"], g["pool_scale"] = pool_bwd("pool_b", p, d_po, w["pool_w"], w["pool_scale"])
    d_o, dz, g["dn_norm_w"] = onorm_bwd("onorm_b", o, z, d_on, w["dn_norm_w"])
    dqd, dkd, du, dw_, dattn3, dgl = delta_scan_bwd("dscan_b", d_o, qd, kd, wd_, attn3, vn, st, gcb)
    dq, dk, dv, dgb, dbb = delta_prep_bwd("dprep_b", q, k, v, gb, bb, t3, du, dw_, dqd, dkd, dattn3, dgl)
    dpre, dc0, dc1, dc2, dc3 = conv_bwd("conv_b", pre, dq, dk, dv, taps)
    g["conv_w"] = jnp.concatenate([dc0, dc1, dc2, dc3], axis=0)
    d_ab, dalog, ddtb = gates_bwd("gates_b", ab, dgb, dbb, alog, dtb, pick)
    g["a_log"] = dalog[:, :DN_HEADS]
    g["dt_bias"] = ddtb[:, :DN_HEADS]
    g["in_qkv"] = mm("in_dwqkv", dpre, x1, ta=True)
    g["in_z"] = mm("in_dwz", dz, x1, ta=True)
    g["in_gates"] = mm("in_dwgates", d_gates, x1, ta=True)
    g["in_p"] = mm("in_dwp", dp, x1, ta=True)
    g["in_ab"] = mm("in_dwab", d_ab, x1, ta=True)
    io.grads_in("xa", g["in_ab"])
    dx1 = mm("in_dxqkv", dpre, w["in_qkv"], deps=io.grads_out("mixer", g))
    dx1 = mm("in_dxz", dz, w["in_z"], add=dx1)
    dx1 = mm("in_dxgates", d_gates, w["in_gates"], add=dx1)
    dx1 = mm("in_dxp", dp, w["in_p"], add=dx1)
    dx1 = mm("in_dxab", d_ab, w["in_ab"], add=dx1)
    dr1, g["ln1_g"], g["ln1_b"] = ln_bwd("ln1_b", r1, [(ALPHA, dr2), (1.0, dx1)], w["ln1_g"])

    def on_dwd(dwd):
        return io.small_out(dict(g, loss=loss[0, :1])) + io.grads_out("ffn1_d", dict(ffn1_w_down=dwd))

    def on_dwgu(dwg, dwu):
        return io.grads_out("ffn1_gu", dict(ffn1_w_gate=dwg, ffn1_w_up=dwu))

    dx0, g["ffn1_w_gate"], g["ffn1_w_up"], g["ffn1_w_down"] = ffn_bwd(
        "ffn1b", x, res1, dr1, w["ffn1_w_gate"], w["ffn1_w_up"], w["ffn1_w_down"], on_dwd=on_dwd, on_dwgu=on_dwgu)
    grad_x = axpy("grad_x", [(ALPHA, dr1), (1.0, dx0)])
    return loss, grad_x, g


WEIGHT_NAMES = ['ffn1_w_gate', 'ffn1_w_up', 'ffn1_w_down', 'ln1_g', 'ln1_b', 'w_in', 'conv_w', 'a_log', 'dt_bias',
                'dn_norm_w', 'w_dn_branch', 'pool_w', 'pool_scale', 'w_pool_branch', 'w_mix_out', 'ln2_g', 'ln2_b',
                'mem_ln_g', 'mem_ln_b', 'xa_wq', 'xa_wk', 'xa_wv', 'xa_wo', 'ln3_g', 'ln3_b', 'ffn2_w_gate',
                'ffn2_w_up', 'ffn2_w_down', 'ln4_g', 'ln4_b']
SHARDED = [
    ("ffn1_w_gate", "cols", (1024, 352)), ("ffn1_w_up", "cols", (1024, 352)), ("ffn1_w_down", "rows", (352, 1024)),
    ("w_in", "cols", (1024, 577)), ("conv_w", "flat", (4, 192)), ("w_dn_branch", "cols", (512, 128)),
    ("w_pool_branch", "cols", (512, 128)), ("w_mix_out", "rows", (128, 1024)), ("xa_wq", "rows", (128, 1024)),
    ("xa_wk", "rows", (128, 1024)), ("xa_wv", "rows", (128, 1024)), ("xa_wo", "rows", (128, 1024)),
    ("ffn2_w_gate", "cols", (1024, 352)), ("ffn2_w_up", "cols", (1024, 352)), ("ffn2_w_down", "rows", (352, 1024)),
]
REPLICATED = [n for n in WEIGHT_NAMES if n not in {s[0] for s in SHARDED}]
ROW_ALIGN = 16
ROW_BLOCKS = (512, 384, 352, 256, 192, 176, 128)
GROUPS = {"ffn1_gu": ("ffn1_w_gate", "ffn1_w_up"), "ffn1_d": ("ffn1_w_down",),
          "mixer": ("w_in", "conv_w", "w_dn_branch", "w_pool_branch", "w_mix_out"),
          "xa": ("xa_wq", "xa_wk", "xa_wv", "xa_wo"),
          "ffn2": ("ffn2_w_gate", "ffn2_w_up", "ffn2_w_down")}
GROUPS["mid"] = GROUPS["mixer"] + GROUPS["xa"]
W_IN_COLS = 577
W_IN_PIECES = (("in_qkv", 0, 1536), ("in_z", 1536, 2048), ("in_ab", 2048, 2056), ("in_p", 2056, 2568),
               ("in_gates", 2568, 4616))


def _round_up(n, m):
    return -(-n // m) * m


def _layout():
    off, table = 0, {}
    for name, form, shape in SHARDED:
        valid = {"rows": shape[0], "cols": shape[1], "flat": 2}[form]
        width = {"rows": shape[1], "cols": shape[0], "flat": shape[0] * shape[1]}[form]
        rows = _round_up(valid, ROW_ALIGN)
        table[name] = (off, rows, valid, width, form, shape)
        off += rows
    return table


LAYOUT = _layout()


def _group_span(names):
    base = LAYOUT[names[0]][0]
    rows = LAYOUT[names[-1]][0] + LAYOUT[names[-1]][1] - base
    while not any(rows % b == 0 for b in ROW_BLOCKS):
        rows += ROW_ALIGN
    return base, rows


def _row_block(rows):
    return _pick(rows, ROW_BLOCKS)


def _pad_block(blk, rows):
    return jnp.pad(blk, ((0, rows - blk.shape[0]), (0, LANES - blk.shape[1])))


def pack_weight_shards(shards, names):
    parts, used = [], 0
    for name in names:
        off, rows, valid, width, form, _ = LAYOUT[name]
        s = shards[name]
        if form == "flat":
            flat = s.reshape(1, -1)
            hi = flat.astype(BF16)
            blk = jnp.concatenate([hi, (flat - hi.astype(F32)).astype(BF16)], axis=0)
        else:
            blk = (s.T if form == "cols" else s).astype(BF16)
        parts.append(_pad_block(blk, rows))
        used += rows
    if _group_span(names)[1] > used:
        parts.append(jnp.zeros((_group_span(names)[1] - used, LANES), BF16))
    return jnp.concatenate(parts, axis=0)


def _w_in_rows(padded, rows, first, last):
    segs = []
    for k in range(N_DEV):
        lo, hi = max(first, k * W_IN_COLS), min(last, (k + 1) * W_IN_COLS)
        if lo < hi:
            segs.append(padded[k * rows + lo - k * W_IN_COLS:k * rows + hi - k * W_IN_COLS])
    return segs[0] if len(segs) == 1 else jnp.concatenate(segs, axis=0)


def unpack_full_weights(gathered, names):
    out, base = {}, _group_span(names)[0]
    for name in names:
        off, rows, valid, width, form, shape = LAYOUT[name]
        seg = gathered[:, off - base:off - base + rows]
        if form == "flat":
            flat = seg[:, 0, :width].astype(F32) + seg[:, 1, :width].astype(F32)
            out[name] = flat.reshape((N_DEV,) + shape).transpose(1, 0, 2).reshape(shape[0], N_DEV * shape[1])
        elif name == "w_in":
            padded = seg.reshape(N_DEV * rows, LANES)
            for piece, first, last in W_IN_PIECES:
                out[piece] = _w_in_rows(padded, rows, first, last)
        else:
            out[name] = seg[:, :valid, :width].reshape(N_DEV * valid, width)
    return out


def pack_full_grads(grads, names):
    parts, used = [], 0
    for name in names:
        off, rows, valid, width, form, shape = LAYOUT[name]
        if form == "flat":
            full = grads[name].reshape(shape[0], N_DEV, shape[1]).transpose(1, 0, 2).reshape(N_DEV, 1, width)
        elif name == "w_in":
            full = jnp.concatenate([grads[piece][:last - first] for piece, first, last in W_IN_PIECES], axis=0)
            full = full.reshape(N_DEV, valid, width)
        else:
            full = grads[name].reshape(N_DEV, valid, width)
        parts.append(jnp.pad(full, ((0, 0), (0, rows - full.shape[1]), (0, LANES - width))))
        used += rows
    if _group_span(names)[1] > used:
        parts.append(jnp.zeros((N_DEV, _group_span(names)[1] - used, LANES), F32))
    return jnp.concatenate(parts, axis=1)


def unpack_grad_shards(packed, names):
    out, base = {}, _group_span(names)[0]
    for name in names:
        off, rows, valid, width, form, shape = LAYOUT[name]
        off -= base
        if form == "flat":
            out[name] = packed[off, :width].reshape(shape)
        elif form == "cols":
            out[name] = packed[off:off + valid, :width].T
        else:
            out[name] = packed[off:off + valid, :width]
    return out


SMALL_SHAPES = {n: (1024,) for n in REPLICATED}
SMALL_SHAPES.update(pool_w=(4, 128, 128), pool_scale=(512,), dn_norm_w=(128,), a_log=(4,), dt_bias=(4,))


SMALL_SHAPES["loss"] = (1,)
SMALL_NAMES = REPLICATED + ["loss"]


def _small_layout():
    off, table = 0, {}
    for name in SMALL_NAMES:
        numel = 1
        for d in SMALL_SHAPES[name]:
            numel *= d
        rows = _round_up(-(-numel // LANES), 8)
        table[name] = (off, rows, numel)
        off += rows
    return table, off


SMALL_LAYOUT, SMALL_ROWS = _small_layout()


def _to_rows(flat, rows):
    return jnp.pad(flat, (0, rows * LANES - flat.shape[0])).reshape(rows, LANES)


def pack_small(values):
    return jnp.concatenate([_to_rows(values[name].reshape(-1), SMALL_LAYOUT[name][1]) for name in SMALL_NAMES], axis=0)


def unpack_small(packed):
    out = {}
    for name in SMALL_NAMES:
        off, rows, numel = SMALL_LAYOUT[name]
        out[name] = packed[off:off + rows].reshape(-1)[:numel].reshape(SMALL_SHAPES[name])
    return out


MESH = pl.DeviceIdType.MESH


def _position():
    return lax.axis_index("x"), lax.axis_index("y"), lax.axis_index("c")


def _other_chips(x, y):
    return [(1 - x, y), (x, 1 - y), (1 - x, 1 - y)]


def all_gather(name, block):
    rows, n = block.shape

    def body(x_ref, out_ref, send_sems, recv_sems, local_sem):
        x, y, c = _position()
        me, sibling = (x, y, c), (x, y, 1 - c)
        chips = _other_chips(x, y)

        def slot(px, py, pc):
            return out_ref.at[4 * px + 2 * py + pc]

        def copy(k, blk, to, src=None):
            return pltpu.make_async_remote_copy(
                src_ref=slot(*blk) if src is None else src, dst_ref=slot(*blk),
                send_sem=send_sems.at[k], recv_sem=recv_sems.at[k], device_id=to, device_id_type=MESH)

        mine = pltpu.make_async_copy(x_ref, slot(*me), local_sem)
        mine.start()
        first = [copy(0, me, sibling, src=x_ref)]
        first += [copy(1 + j, me, (*chip, c), src=x_ref) for j, chip in enumerate(chips)]
        for cp in first:
            cp.start()
        passed = [copy(4 + j, (*chip, c), sibling) for j, chip in enumerate(chips)]
        for j, chip in enumerate(chips):
            copy(1 + j, (*chip, c), me).wait_recv()
            passed[j].start()
        copy(0, sibling, me).wait_recv()
        for j, chip in enumerate(chips):
            copy(4 + j, (*chip, 1 - c), me).wait_recv()
        for cp in first + passed:
            cp.wait_send()
        mine.wait()

    return pl.pallas_call(
        body, name=name, out_shape=jax.ShapeDtypeStruct((N_DEV, rows, n), block.dtype),
        in_specs=[ANY], out_specs=ANY,
        scratch_shapes=[pltpu.SemaphoreType.DMA((7,)), pltpu.SemaphoreType.DMA((7,)), pltpu.SemaphoreType.DMA(())],
    )(block)


HBM = pl.BlockSpec(memory_space=pltpu.HBM)
SEM = pl.BlockSpec(memory_space=pltpu.SEMAPHORE)
EFFECT = pltpu.SideEffectType.DATAFLOW_SIDE_EFFECTING


def _remote(src, dst, send_sem, recv_sem, to):
    return pltpu.make_async_remote_copy(src_ref=src, dst_ref=dst, send_sem=send_sem, recv_sem=recv_sem,
                                        device_id=to, device_id_type=MESH)


def split_start(name, bufs, n, make_copies):
    nb = len(bufs)

    def body(*refs):
        for out_cp, _ in make_copies(refs[:nb], refs[nb:nb + n], refs[nb + n:nb + 2 * n]):
            out_cp.start()
        refs[-1][...] = jnp.zeros_like(refs[-1])

    outs = pl.pallas_call(
        body, name=name,
        out_shape=tuple([pltpu.SemaphoreType.DMA(())] * (2 * n)) + tuple(pltpu.HBM(b.shape, b.dtype) for b in bufs)
        + (jax.ShapeDtypeStruct((8, 128), F32),),
        in_specs=[HBM] * nb,
        out_specs=tuple([SEM] * (2 * n) + [HBM] * nb + [pl.BlockSpec(memory_space=pltpu.VMEM)]),
        input_output_aliases={i: 2 * n + i for i in range(nb)},
        compiler_params=pltpu.CompilerParams(has_side_effects=EFFECT),
    )(*[pltpu.with_memory_space_constraint(b, pltpu.HBM) for b in bufs])
    return list(outs[:2 * n]), list(outs[2 * n:2 * n + nb]), outs[-1]


def split_wait(name, bufs, sems, n, make_copies, after):
    nb = len(bufs)

    def body(*refs):
        for out_cp, in_cp in make_copies(refs[:nb], refs[nb:nb + n], refs[nb + n:nb + 2 * n]):
            out_cp.wait_send()
            in_cp.wait_recv()

    outs = pl.pallas_call(
        body, name=name, out_shape=tuple(pltpu.HBM(b.shape, b.dtype) for b in bufs),
        in_specs=[HBM] * nb + [SEM] * (2 * n) + [ANY], out_specs=tuple([HBM] * nb),
        input_output_aliases={i: i for i in range(nb)},
        compiler_params=pltpu.CompilerParams(has_side_effects=EFFECT),
    )(*bufs, *sems, after)
    return list(outs)


def _gather_stage1(refs, send, recv):
    src, land = refs
    x, y, c = _position()
    peers = [(x, y, 1 - c)] + [(*chip, c) for chip in _other_chips(x, y)]
    return [(_remote(src, land.at[4 * x + 2 * y + c], send[k], recv[k], p),
             _remote(src, land.at[4 * p[0] + 2 * p[1] + p[2]], send[k], recv[k], p)) for k, p in enumerate(peers)]


def _gather_stage2(refs, send, recv):
    (land,) = refs
    x, y, c = _position()
    out = []
    for j, (px, py) in enumerate(_other_chips(x, y)):
        mine, theirs = land.at[4 * px + 2 * py + c], land.at[4 * px + 2 * py + 1 - c]
        out.append((_remote(mine, mine, send[j], recv[j], (x, y, 1 - c)),
                    _remote(theirs, theirs, send[j], recv[j], (x, y, 1 - c))))
    return out


def _flips():
    return [(a, b, d) for a in (0, 1) for b in (0, 1) for d in (0, 1) if a | b | d]


def _gather_direct(refs, send, recv):
    src, land = refs
    x, y, c = _position()
    out = []
    for k, (fx, fy, fc) in enumerate(_flips()):
        p = (1 - x if fx else x, 1 - y if fy else y, 1 - c if fc else c)
        out.append((_remote(src, land.at[4 * x + 2 * y + c], send[k], recv[k], p),
                    _remote(src, land.at[4 * p[0] + 2 * p[1] + p[2]], send[k], recv[k], p)))
    return out


def _scatter_direct(refs, send, recv):
    sendbuf, land = refs
    x, y, c = _position()
    me = 4 * x + 2 * y + c
    out = []
    for k, (fx, fy, fc) in enumerate(_flips()):
        p = (1 - x if fx else x, 1 - y if fy else y, 1 - c if fc else c)
        peer = 4 * p[0] + 2 * p[1] + p[2]
        out.append((_remote(sendbuf.at[peer], land.at[me], send[k], recv[k], p),
                    _remote(sendbuf.at[peer], land.at[peer], send[k], recv[k], p)))
    return out


def _slot_sum(name, table, first, count, packed, received, out_dtype):
    rows = packed.shape[1]
    blk = (1, _row_block(rows), LANES)
    with_recv = received is not None

    def body(tbl_ref, *refs):
        if with_recv:
            g_ref, r_ref, o_ref = refs
            o_ref[...] = (g_ref[...] + r_ref[...].astype(F32)).astype(o_ref.dtype)
        else:
            g_ref, o_ref = refs
            o_ref[...] = g_ref[...].astype(o_ref.dtype)

    in_specs = [pl.BlockSpec(blk, lambda r, i, tbl: (tbl[first + r], i, 0))]
    ins = [packed]
    if with_recv:
        in_specs.append(pl.BlockSpec(blk, lambda r, i, tbl: (first + r, i, 0)))
        ins.append(received)
    return pl.pallas_call(
        body, name=name,
        grid_spec=pltpu.PrefetchScalarGridSpec(
            num_scalar_prefetch=1, grid=(count, rows // blk[1]), in_specs=in_specs,
            out_specs=pl.BlockSpec(blk, lambda r, i, tbl: (r, i, 0))),
        out_shape=jax.ShapeDtypeStruct((count, rows, LANES), out_dtype),
        compiler_params=_params(("parallel", "parallel")),
    )(table, *ins)


def _own_plus_slots(name, me, packed, landed):
    n, rows, _ = landed.shape
    tr = _row_block(rows)

    def body(me_ref, g_ref, l_ref, o_ref):
        acc = g_ref[0]
        for j in range(n):
            acc = acc + l_ref[j].astype(F32)
        o_ref[...] = acc

    return pl.pallas_call(
        body, name=name,
        grid_spec=pltpu.PrefetchScalarGridSpec(
            num_scalar_prefetch=1, grid=(rows // tr,),
            in_specs=[pl.BlockSpec((1, tr, LANES), lambda i, me: (me[0], i, 0)),
                      pl.BlockSpec((n, tr, LANES), lambda i, me: (0, i, 0))],
            out_specs=pl.BlockSpec((tr, LANES), lambda i, me: (i, 0))),
        out_shape=jax.ShapeDtypeStruct((rows, LANES), F32), compiler_params=_params(("parallel",)),
    )(me, packed, landed)


def _sum_slots(name, stack):
    n, rows, _ = stack.shape

    def body(s_ref, o_ref):
        acc = s_ref[0]
        for j in range(1, n):
            acc = acc + s_ref[j]
        o_ref[...] = acc

    return pl.pallas_call(
        body, name=name, in_specs=[pl.BlockSpec(stack.shape, lambda: (0, 0, 0))],
        out_specs=pl.BlockSpec((rows, LANES), lambda: (0, 0)), out_shape=jax.ShapeDtypeStruct((rows, LANES), F32),
    )(stack)


def adamw(name, w, g, m, v):
    shape = w.shape
    last = shape[-1]
    w2, g2, m2, v2 = [a.reshape(-1, last) for a in (w, g, m, v)]
    rows = w2.shape[0]
    tr = 256 if rows % 256 == 0 else rows

    def body(w_ref, g_ref, m_ref, v_ref, d_ref, nm_ref, nv_ref):
        gg = g_ref[...]
        nm = ADAM_B1 * m_ref[...] + (1.0 - ADAM_B1) * gg
        nv = ADAM_B2 * v_ref[...] + (1.0 - ADAM_B2) * (gg * gg)
        m_hat = nm / (1.0 - ADAM_B1 ** ADAM_STEP)
        v_hat = nv / (1.0 - ADAM_B2 ** ADAM_STEP)
        d_ref[...] = -ADAM_LR * (m_hat / (jnp.sqrt(v_hat) + ADAM_EPS) + ADAM_WD * w_ref[...])
        nm_ref[...] = nm
        nv_ref[...] = nv

    spec = pl.BlockSpec((tr, last), lambda i: (i, 0))
    outs = pl.pallas_call(
        body, name=name, grid=(rows // tr,), in_specs=[spec] * 4, out_specs=[spec] * 3,
        out_shape=[jax.ShapeDtypeStruct((rows, last), F32)] * 3, compiler_params=_params(("parallel",)),
    )(w2, g2, m2, v2)
    return [o.reshape(shape) for o in outs]


def _landing(block_shape, dtype, own):
    x, y, c = _position()
    return lax.dynamic_update_slice(lax.empty((N_DEV,) + block_shape, dtype), own[None], (4 * x + 2 * y + c, 0, 0))


class _Exchanges:
    def __init__(self, shards):
        self.shards = shards
        self.pending = {}
        self.reduced = {}

    def first_weights(self):
        names = GROUPS["ffn1_gu"]
        return unpack_full_weights(all_gather("ag_ffn1_gu", pack_weight_shards(self.shards, names)), names)

    def rest_started(self):
        tokens = []
        block = pack_weight_shards(self.shards, GROUPS["ffn1_d"])
        sems, bufs, token = split_start("ag_ffn1_d_s", [block, _landing(block.shape, block.dtype, block)], N_DEV - 1,
                                        _gather_direct)
        self.pending["ffn1_d"] = (sems, bufs)
        tokens.append(token)
        for key in ("mid", "ffn2"):
            block = pack_weight_shards(self.shards, GROUPS[key])
            sems, bufs, token = split_start(f"ag_{key}_s1", [block, _landing(block.shape, block.dtype, block)], 4,
                                            _gather_stage1)
            self.pending[key] = (sems, bufs)
            tokens.append(token)
        return tuple(tokens)

    def ffn1_down(self, after):
        sems, bufs = self.pending.pop("ffn1_d")
        _, gathered = split_wait("ag_ffn1_d_w", bufs, sems, N_DEV - 1, _gather_direct, after)
        return unpack_full_weights(gathered, GROUPS["ffn1_d"])["ffn1_w_down"]

    def halfway(self, key, after):
        sems, bufs = self.pending.pop(key)
        _, land = split_wait(f"ag_{key}_w1", bufs, sems, 4, _gather_stage1, after)
        sems, bufs, token = split_start(f"ag_{key}_s2", [land], 3, _gather_stage2)
        self.pending[key] = (sems, bufs)
        return (token,)

    def weights(self, key, after):
        sems, bufs = self.pending.pop(key)
        (gathered,) = split_wait(f"ag_{key}_w2", bufs, sems, 3, _gather_stage2, after)
        w = unpack_full_weights(gathered, GROUPS[key])
        if "in_ab" in w:
            w["in_ab"] = jnp.pad(w["in_ab"], ((0, 128 - 2 * DN_HEADS), (0, 0)))
        return w

    def grads_out(self, key, grads):
        packed = pack_full_grads(grads, GROUPS[key])
        wire = _slot_sum(f"rs_{key}_wire", jnp.arange(N_DEV, dtype=jnp.int32), 0, N_DEV, packed, None, WIRE)
        land = _landing(wire.shape[1:], WIRE, jnp.zeros(wire.shape[1:], WIRE))
        sems, bufs, token = split_start(f"rs_{key}_start", [wire, land], N_DEV - 1, _scatter_direct)
        self.pending[key] = (sems, bufs, packed)
        return (token,)

    def grads_in(self, key, after):
        sems, bufs, packed = self.pending.pop(key)
        x, y, c = _position()
        _, landed = split_wait(f"rs_{key}_wait", bufs, sems, N_DEV - 1, _scatter_direct, after)
        me = jnp.reshape(4 * x + 2 * y + c, (1,)).astype(jnp.int32)
        total = _own_plus_slots(f"rs_{key}_sum", me, packed, landed)
        self.reduced.update(unpack_grad_shards(total, GROUPS[key]))
        return total

    def small_out(self, values):
        block = pack_small(values)
        sems, bufs, token = split_start("ag_small_s", [block, _landing(block.shape, block.dtype, block)], N_DEV - 1,
                                        _gather_direct)
        self.pending["small"] = (sems, bufs)
        return (token,)

    def small_in(self, after):
        sems, bufs = self.pending.pop("small")
        _, gathered = split_wait("ag_small_w", bufs, sems, N_DEV - 1, _gather_direct, after)
        return unpack_small(_sum_slots("small_sum", gathered))


def kernel(x, mem, ffn1_w_gate, ffn1_w_up, ffn1_w_down, ln1_g, ln1_b, w_in, conv_w, a_log, dt_bias, dn_norm_w, w_dn_branch, pool_w, pool_scale, w_pool_branch, w_mix_out, ln2_g, ln2_b, mem_ln_g, mem_ln_b, xa_wq, xa_wk, xa_wv, xa_wo, ln3_g, ln3_b, ffn2_w_gate, ffn2_w_up, ffn2_w_down, ln4_g, ln4_b, loss_target, m_ffn1_w_gate, m_ffn1_w_up, m_ffn1_w_down, m_ln1_g, m_ln1_b, m_w_in, m_conv_w, m_a_log, m_dt_bias, m_dn_norm_w, m_w_dn_branch, m_pool_w, m_pool_scale, m_w_pool_branch, m_w_mix_out, m_ln2_g, m_ln2_b, m_mem_ln_g, m_mem_ln_b, m_xa_wq, m_xa_wk, m_xa_wv, m_xa_wo, m_ln3_g, m_ln3_b, m_ffn2_w_gate, m_ffn2_w_up, m_ffn2_w_down, m_ln4_g, m_ln4_b, v_ffn1_w_gate, v_ffn1_w_up, v_ffn1_w_down, v_ln1_g, v_ln1_b, v_w_in, v_conv_w, v_a_log, v_dt_bias, v_dn_norm_w, v_w_dn_branch, v_pool_w, v_pool_scale, v_w_pool_branch, v_w_mix_out, v_ln2_g, v_ln2_b, v_mem_ln_g, v_mem_ln_b, v_xa_wq, v_xa_wk, v_xa_wv, v_xa_wo, v_ln3_g, v_ln3_b, v_ffn2_w_gate, v_ffn2_w_up, v_ffn2_w_down, v_ln4_g, v_ln4_b):
    given = dict(locals())
    shards = {n: given[n] for n in WEIGHT_NAMES}
    io = _Exchanges({n: shards[n][0] for n, _, _ in SHARDED})
    w = io.first_weights()
    for n in REPLICATED:
        w[n] = shards[n][0] if n == "pool_w" else shards[n]
    loss_part, grad_x, g = local_step(x[0], mem[0], loss_target[0], w, io)

    grad, updates = {}, {}

    def update(names, reduced):
        for n in names:
            grad[n] = reduced[n].reshape(shards[n].shape)
            updates[n] = adamw("adamw_" + n, shards[n], grad[n], given["m_" + n], given["v_" + n])
        return updates[names[-1]][0]

    update(GROUPS["ffn2"] + GROUPS["xa"], io.reduced)
    io.grads_in("mixer", grad_x)
    done = update(GROUPS["mixer"], io.reduced)
    small = io.small_in(done)
    loss = small.pop("loss")[0]
    done = update(REPLICATED, small)
    io.grads_in("ffn1_d", done)
    done = update(GROUPS["ffn1_d"], io.reduced)
    io.grads_in("ffn1_gu", done)
    update(GROUPS["ffn1_gu"], io.reduced)
    return (loss, grad_x[None], *[grad[n] for n in WEIGHT_NAMES], *[updates[n][0] for n in WEIGHT_NAMES],
            *[updates[n][1] for n in WEIGHT_NAMES], *[updates[n][2] for n in WEIGHT_NAMES])
```

```python
import functools

import jax
import jax.numpy as jnp
from jax import lax
from jax.experimental import pallas as pl
from jax.experimental.pallas import tpu as pltpu

F32 = jnp.float32
BF16 = jnp.bfloat16
MMD = BF16
WIRE = BF16
HI = lax.Precision.HIGHEST
X3 = lax.Precision.HIGH
VMEM_LIMIT_BYTES = 48 * 1024 * 1024

D_MODEL = 1024
D_FF = 2816
CHUNK = 64
N_MEM = 256
DN_HEADS = 4
HD = 128
DN_WIDTH = 512
POOL_WINDOWS = (2, 4, 8, 16)
POOL_WIDTH = 512
XA_HEADS = 4
XA_HD = 256
LN_EPS = 1e-5
RMS_EPS = 1e-6
L2_EPS = 1e-6
ALPHA = 2.0 ** 0.25
HALO = 16

ADAM_LR = 0.001
ADAM_B1 = 0.9
ADAM_B2 = 0.999
ADAM_EPS = 1e-08
ADAM_WD = 0.01
ADAM_STEP = 10

N_DEV = 8
LANES = 1024
ANY = pl.BlockSpec(memory_space=pl.ANY)


def _dot(a, b, ca, cb, prec):
    dn = (((ca,), (cb,)), ((), ()))
    if prec is not None:
        return lax.dot_general(a.astype(F32), b.astype(F32), dn, precision=prec, preferred_element_type=F32)
    return lax.dot_general(a.astype(MMD), b.astype(MMD), dn, preferred_element_type=F32)


def dnn(a, b, prec=None):
    return _dot(a, b, 1, 0, prec)


def dnt(a, b, prec=None):
    return _dot(a, b, 1, 1, prec)


def dtn(a, b, prec=None):
    return _dot(a, b, 0, 0, prec)


def _sigmoid(x):
    return jax.nn.sigmoid(x)


def _silu(x):
    return x * _sigmoid(x)


def _dsilu(x):
    s = _sigmoid(x)
    return s * (1.0 + x * (1.0 - s))


def _softplus(x):
    return jnp.maximum(x, 0.0) + jnp.log1p(jnp.exp(-jnp.abs(x)))


def _iota(shape, dim):
    return lax.broadcasted_iota(jnp.int32, shape, dim)


def _rsum(x):
    return jnp.sum(x, axis=1, keepdims=True)


def _csum(x):
    return jnp.sum(x, axis=0, keepdims=True)


def _pick(n, cands):
    for c in cands:
        if n % c == 0:
            return c
    return n


def _params(sem):
    return pltpu.CompilerParams(dimension_semantics=sem, vmem_limit_bytes=VMEM_LIMIT_BYTES)


MM_TILE_SIZES = (4096, 2816, 2048, 1536, 1408, 1024, 768, 512, 384, 256, 128)
MM_VMEM_BUDGET = 36 * 1024 * 1024
HBM_BYTES_PER_US = 3.0e6
GRID_STEP_US = 0.35


def _mm_tiles(m, n, kc, a_bytes, b_bytes, o_bytes):
    def sizes(d):
        return [d] if d <= 512 else [t for t in MM_TILE_SIZES if d % t == 0]

    best = None
    for tm in sizes(m):
        for tn in sizes(n):
            for tk in sizes(kc):
                vmem = 2 * (tm * tk * a_bytes + tk * tn * b_bytes + tm * tn * o_bytes) + tm * tn * 4
                if vmem > MM_VMEM_BUDGET:
                    continue
                steps = (m // tm) * (n // tn) * (kc // tk)
                traffic = m * kc * a_bytes * (n // tn) + kc * n * b_bytes * (m // tm) + m * n * o_bytes
                edge = tm * tk * a_bytes + tk * tn * b_bytes + tm * tn * o_bytes
                cost = (traffic + edge) / HBM_BYTES_PER_US + steps * GRID_STEP_US
                if best is None or cost < best[0]:
                    best = (cost, tm, tn, tk)
    return best[1:]


def mm(name, a, b, *, ta=False, tb=False, out_dtype=F32, add=None, scale=None, deps=()):
    if ta:
        kc, m = a.shape
    else:
        m, kc = a.shape
    if tb:
        n, kb = b.shape
    else:
        kb, n = b.shape
    assert kc == kb, (name, a.shape, b.shape)
    tm, tn, tk = _mm_tiles(m, n, kc, a.dtype.itemsize, b.dtype.itemsize,
                           jnp.dtype(out_dtype).itemsize * (2 if add is not None else 1))
    nk = kc // tk
    grid = (m // tm, n // tn, nk)
    a_spec = pl.BlockSpec((tk, tm), lambda i, j, k: (k, i)) if ta else pl.BlockSpec((tm, tk), lambda i, j, k: (i, k))
    b_spec = pl.BlockSpec((tn, tk), lambda i, j, k: (j, k)) if tb else pl.BlockSpec((tk, tn), lambda i, j, k: (k, j))
    o_spec = pl.BlockSpec((tm, tn), lambda i, j, k: (i, j))
    ca, cb = (0 if ta else 1), (1 if tb else 0)
    has_add = add is not None

    def body(*refs):
        a_ref, b_ref, o_ref, acc_ref = refs[0], refs[1], refs[-2], refs[-1]
        add_ref = refs[2] if has_add else None
        k = pl.program_id(2)

        @pl.when(k == 0)
        def _():
            acc_ref[...] = jnp.zeros_like(acc_ref)

        acc_ref[...] += _dot(a_ref[...], b_ref[...], ca, cb, None)

        @pl.when(k == nk - 1)
        def _():
            r = acc_ref[...]
            if scale is not None:
                r = r * scale
            if has_add:
                r = r + add_ref[...]
            o_ref[...] = r.astype(o_ref.dtype)

    ins = [a, b] + ([add] if has_add else []) + list(deps)
    specs = [a_spec, b_spec] + ([o_spec] if has_add else []) + [ANY] * len(deps)
    return pl.pallas_call(
        body, name=name, grid=grid, in_specs=specs, out_specs=o_spec,
        out_shape=jax.ShapeDtypeStruct((m, n), out_dtype),
        scratch_shapes=[pltpu.VMEM((tm, tn), F32)],
        compiler_params=_params(("parallel", "parallel", "arbitrary")),
    )(*ins)


class _Ctx:
    def __init__(self, i, nblk, tl):
        self.i, self.nblk, self.tl = i, nblk, tl


def _norm_item(it):
    if isinstance(it, tuple):
        a, w, j = it[:3]
        rows = it[3] if len(it) > 3 else None
        return a, w, j, rows
    return it, it.shape[-1], 0, None


def rowwise(name, fn, length, tl, *, rows=(), consts=(), prevs=(), nexts=(), out_rows=(), out_accs=(), deps=()):
    nblk = length // tl
    hb = tl // HALO
    nhalo = length // HALO
    arrays, specs = [], []
    for it in rows:
        a, w, j, r = _norm_item(it)
        if a.ndim == 3:
            specs.append(pl.BlockSpec((a.shape[0], tl, w), lambda i, j=j: (0, i, j)))
        else:
            specs.append(pl.BlockSpec((r or tl, w), lambda i, j=j: (i, j)))
        arrays.append(a)
    for a in consts:
        specs.append(pl.BlockSpec(a.shape, lambda i, nd=a.ndim: (0,) * nd))
        arrays.append(a)
    for it in prevs:
        a, w, j, _ = _norm_item(it)
        specs.append(pl.BlockSpec((HALO, w), lambda i, j=j: (jnp.maximum(i * hb - 1, 0), j)))
        arrays.append(a)
    for it in nexts:
        a, w, j, _ = _norm_item(it)
        specs.append(pl.BlockSpec((HALO, w), lambda i, j=j: (jnp.minimum((i + 1) * hb, nhalo - 1), j)))
        arrays.append(a)
    out_shape, out_specs = [], []
    for spec in out_rows:
        if len(spec) == 3:
            h, w, dt = spec
            out_shape.append(jax.ShapeDtypeStruct((h, length, w), dt))
            out_specs.append(pl.BlockSpec((h, tl, w), lambda i: (0, i, 0)))
        else:
            w, dt = spec
            out_shape.append(jax.ShapeDtypeStruct((length, w), dt))
            out_specs.append(pl.BlockSpec((tl, w), lambda i: (i, 0)))
    for shape, dt in out_accs:
        out_shape.append(jax.ShapeDtypeStruct(shape, dt))
        out_specs.append(pl.BlockSpec(shape, lambda i, nd=len(shape): (0,) * nd))
    n_r, n_c, n_p, n_n = len(rows), len(consts), len(prevs), len(nexts)
    n_in = n_r + n_c + n_p + n_n
    n_or = len(out_rows)
    arrays, specs = arrays + list(deps), specs + [ANY] * len(deps)

    def body(*refs):
        i = pl.program_id(0)
        vals = [r[...] for r in refs[:n_in]]
        outs = refs[n_in + len(deps):]
        ctx = _Ctx(i, nblk, tl)
        ro, ao = fn(ctx, vals[:n_r], vals[n_r:n_r + n_c], vals[n_r + n_c:n_r + n_c + n_p], vals[n_r + n_c + n_p:])
        for r, v in zip(outs[:n_or], ro, strict=True):
            r[...] = v.astype(r.dtype)
        for r, v in zip(outs[n_or:], ao, strict=True):
            @pl.when(i == 0)
            def _(r=r, v=v):
                r[...] = v.astype(r.dtype)

            @pl.when(i > 0)
            def _(r=r, v=v):
                r[...] += v.astype(r.dtype)

    res = pl.pallas_call(
        body, name=name, grid=(nblk,), in_specs=specs, out_specs=out_specs, out_shape=out_shape,
        compiler_params=_params(("arbitrary",) if out_accs else ("parallel",)),
    )(*arrays)
    return res


def _heads(x, n, w):
    return [x[:, h * w:(h + 1) * w] for h in range(n)]


def _cat(xs):
    return jnp.concatenate(xs, axis=1)


def _row_index(ctx, nrows, offset=0):
    return ctx.i * ctx.tl + offset + _iota((nrows, 1), 0)


def _ln_stats(r):
    mu = jnp.mean(r, axis=1, keepdims=True)
    d = r - mu
    var = jnp.mean(d * d, axis=1, keepdims=True)
    rstd = lax.rsqrt(var + LN_EPS)
    return d * rstd, rstd


def ln_fwd(name, terms, g, b, tl=256, deps=()):
    coefs = [c for c, _ in terms]
    length = terms[0][1].shape[0]

    def fn(ctx, rows, consts, prevs, nexts):
        r = sum(c * t for c, t in zip(coefs, rows))
        xh, _ = _ln_stats(r)
        return [xh * consts[0] + consts[1], r], []

    return rowwise(name, fn, length, min(tl, length), rows=[t for _, t in terms], consts=[g, b],
                   out_rows=[(D_MODEL, F32), (D_MODEL, F32)], deps=deps)


def ln_bwd(name, r, terms, g, tl=256, deps=()):
    coefs = [c for c, _ in terms]
    length = r.shape[0]

    def fn(ctx, rows, consts, prevs, nexts):
        xh, rstd = _ln_stats(rows[0])
        dy = sum(c * t for c, t in zip(coefs, rows[1:]))
        dxh = dy * consts[0]
        dr = rstd * (dxh - jnp.mean(dxh, axis=1, keepdims=True) - xh * jnp.mean(dxh * xh, axis=1, keepdims=True))
        return [dr], [_csum(dy * xh), _csum(dy)]

    return rowwise(name, fn, length, min(tl, length), rows=[r] + [t for _, t in terms], consts=[g],
                   out_rows=[(D_MODEL, F32)], out_accs=[((1, D_MODEL), F32), ((1, D_MODEL), F32)], deps=deps)


def ln_loss(name, terms, g, b, target, tl=256):
    coefs = [c for c, _ in terms]
    length = target.shape[0]
    nt = len(terms)

    def fn(ctx, rows, consts, prevs, nexts):
        r = sum(c * t for c, t in zip(coefs, rows[:nt]))
        xh, _ = _ln_stats(r)
        err = xh * consts[0] + consts[1] - rows[nt]
        tot = _csum(_rsum(err * err)) * (0.5 / D_MODEL)
        return [err * (1.0 / D_MODEL), r], [jnp.broadcast_to(tot, (1, 128))]

    return rowwise(name, fn, length, min(tl, length), rows=[t for _, t in terms] + [target], consts=[g, b],
                   out_rows=[(D_MODEL, F32), (D_MODEL, F32)], out_accs=[((1, 128), F32)])


def axpy(name, terms, tl=256):
    coefs = [c for c, _ in terms]
    length, width = terms[0][1].shape

    def fn(ctx, rows, consts, prevs, nexts):
        return [sum(c * t for c, t in zip(coefs, rows))], []

    return rowwise(name, fn, length, min(tl, length), rows=[t for _, t in terms], out_rows=[(width, F32)])[0]


def ffn_fwd(tag, x, wg, wu, wd, deps=()):
    length = x.shape[0]
    hg = mm(tag + "_gate", x, wg, tb=True, deps=deps)
    hu = mm(tag + "_up", x, wu, tb=True)
    if callable(wd):
        wd = wd(hu)

    def fn(ctx, rows, consts, prevs, nexts):
        return [_silu(rows[0]) * rows[1]], []

    act = rowwise(tag + "_act", fn, length, min(256, length), rows=[hg, hu], out_rows=[(D_FF, BF16)])[0]
    f = mm(tag + "_down", act, wd)
    return f, (hg, hu, act), wd


def ffn_bwd(tag, x, res, dr, wg, wu, wd, deps=(), on_dwd=None, on_dwgu=None):
    hg, hu, act = res
    length = x.shape[0]
    dwd = mm(tag + "_dwd", act, dr, ta=True, scale=0.5, deps=deps)
    dact = mm(tag + "_dact", dr, wd, tb=True, scale=0.5, deps=on_dwd(dwd) if on_dwd else ())

    def fn(ctx, rows, consts, prevs, nexts):
        g, u, da = rows
        return [da * u * _dsilu(g), da * _silu(g)], []

    dhg, dhu = rowwise(tag + "_dactb", fn, length, min(256, length), rows=[hg, hu, dact],
                       out_rows=[(D_FF, BF16), (D_FF, BF16)])
    dwg = mm(tag + "_dwg", dhg, x, ta=True)
    dwu = mm(tag + "_dwu", dhu, x, ta=True)
    dx = mm(tag + "_dxg", dhg, wg, deps=on_dwgu(dwg, dwu) if on_dwgu else ())
    dx = mm(tag + "_dxu", dhu, wu, add=dx)
    return dx, dwg, dwu, dwd


def _conv_taps(ext, taps, n):
    out = taps[3] * ext
    for j in range(3):
        out = out + taps[j] * pltpu.roll(ext, 3 - j, 0)
    return out


def _l2n(x):
    r = lax.rsqrt(_rsum(x * x) + L2_EPS)
    return x * r, r


def conv_fwd(name, pre, taps, tl=256):
    length = pre.shape[0]
    tl = min(tl, length)

    def fn(ctx, rows, consts, prevs, nexts):
        prev = jnp.where(ctx.i > 0, prevs[0], 0.0)
        ext = jnp.concatenate([prev, rows[0]], axis=0)
        s = _silu(_conv_taps(ext, consts, tl + HALO)[HALO:])
        q = _cat([_l2n(x)[0] * (HD ** -0.5) for x in _heads(s[:, :DN_WIDTH], DN_HEADS, HD)])
        k = _cat([_l2n(x)[0] for x in _heads(s[:, DN_WIDTH:2 * DN_WIDTH], DN_HEADS, HD)])
        return [q, k, s[:, 2 * DN_WIDTH:]], []

    return rowwise(name, fn, length, tl, rows=[pre], consts=list(taps), prevs=[pre],
                   out_rows=[(DN_WIDTH, F32)] * 3)


def conv_bwd(name, pre, dq, dk, dv, taps, tl=256):
    length = pre.shape[0]
    tl = min(tl, length)
    n = tl + 2 * HALO

    def fn(ctx, rows, consts, prevs, nexts):
        last = ctx.i == ctx.nblk - 1
        prev = jnp.where(ctx.i > 0, prevs[0], 0.0)
        ext = jnp.concatenate([prev, rows[0], nexts[0]], axis=0)
        c = _conv_taps(ext, consts, n)
        s = _silu(c)
        zero = jnp.zeros((HALO, DN_WIDTH), F32)
        dqe, dke, dve = [jnp.concatenate([zero, rows[1 + t], jnp.where(last, 0.0, nexts[1 + t])], axis=0)
                         for t in range(3)]

        def l2_bwd(x, dy):
            y, r = _l2n(x)
            return r * (dy - y * _rsum(dy * y))

        dsq = _cat([l2_bwd(x, d * (HD ** -0.5)) for x, d in zip(_heads(s[:, :DN_WIDTH], DN_HEADS, HD),
                                                                 _heads(dqe, DN_HEADS, HD))])
        dsk = _cat([l2_bwd(x, d) for x, d in zip(_heads(s[:, DN_WIDTH:2 * DN_WIDTH], DN_HEADS, HD),
                                                  _heads(dke, DN_HEADS, HD))])
        dc = _cat([dsq, dsk, dve]) * _dsilu(c)
        dpre = consts[3] * dc
        for j in range(3):
            dpre = dpre + consts[j] * pltpu.roll(dc, n - (3 - j), 0)
        dc_cur = dc[HALO:HALO + tl]
        dws = [_csum(dc_cur * pltpu.roll(ext, 3 - j, 0)[HALO:HALO + tl]) for j in range(3)]
        dws.append(_csum(dc_cur * ext[HALO:HALO + tl]))
        return [dpre[HALO:HALO + tl]], dws

    return rowwise(name, fn, length, tl, rows=[pre, dq, dk, dv], consts=list(taps), prevs=[pre],
                   nexts=[pre, dq, dk, dv], out_rows=[(3 * DN_WIDTH, BF16)],
                   out_accs=[((1, 3 * DN_WIDTH), F32)] * 4)


def _gate_consts():
    lane = jnp.arange(128)[:, None]
    col = jnp.arange(2 * DN_WIDTH)[None, :]
    sel = ((lane < 2 * DN_HEADS) & (col // HD == lane)).astype(F32)
    pick = ((col.T == lane.T * HD) & (lane.T < 2 * DN_HEADS)).astype(F32)
    return sel, pick


def _gate_math(ab, alog, dtb):
    z = ab + dtb
    g = -jnp.exp(alog) * _softplus(z)
    beta = _sigmoid(ab)
    return z, g, beta


def gates_fwd(name, ab, alog, dtb, sel, tl=256):
    length = ab.shape[0]

    def fn(ctx, rows, consts, prevs, nexts):
        _, g, beta = _gate_math(rows[0], consts[0], consts[1])
        lane = _iota(g.shape, 1)
        small = jnp.where(lane < DN_HEADS, g, jnp.where(lane < 2 * DN_HEADS, beta, 0.0))
        big = dnn(small, consts[2], HI)
        return [big[:, :DN_WIDTH], big[:, DN_WIDTH:]], []

    return rowwise(name, fn, length, min(tl, length), rows=[ab], consts=[alog, dtb, sel],
                   out_rows=[(DN_WIDTH, F32)] * 2)


def gates_bwd(name, ab, dgb, dbb, alog, dtb, pick, tl=256):
    length = ab.shape[0]

    def fn(ctx, rows, consts, prevs, nexts):
        z, g, beta = _gate_math(rows[0], consts[0], consts[1])
        dsmall = dnn(_cat([rows[1], rows[2]]), consts[2], HI)
        lane = _iota(g.shape, 1)
        is_a = lane < DN_HEADS
        da = jnp.where(is_a, dsmall * (-jnp.exp(consts[0])) * _sigmoid(z), 0.0)
        db = jnp.where((lane >= DN_HEADS) & (lane < 2 * DN_HEADS), dsmall * beta * (1.0 - beta), 0.0)
        return [da + db], [_csum(jnp.where(is_a, dsmall * g, 0.0)), _csum(da)]

    return rowwise(name, fn, length, min(tl, length), rows=[ab, dgb, dbb], consts=[alog, dtb, pick],
                   out_rows=[(128, BF16)], out_accs=[((1, 128), F32)] * 2)


CPS = 2


def _chunk_scan_rows(x, suffix=False):
    n = x.shape[0]
    rc = _iota(x.shape, 0) & (CHUNK - 1)
    sh = 1
    while sh < CHUNK:
        if suffix:
            x = x + jnp.where(rc < CHUNK - sh, pltpu.roll(x, n - sh, 0), 0.0)
        else:
            x = x + jnp.where(rc >= sh, pltpu.roll(x, sh, 0), 0.0)
        sh *= 2
    return x


def _tri_inv(a_list, eye, bd):
    def each(f, *ls):
        return [f(*xs) for xs in zip(*ls)]

    dg = [jnp.where(bd, a, 0.0) for a in a_list]
    lo = each(lambda a, d: a - d, a_list, dg)
    n1 = [-d for d in dg]
    n2 = each(lambda n: dnn(n, n, X3), n1)
    n4 = each(lambda n: dnn(n, n, X3), n2)
    td = each(lambda p, s: dnn(eye + p, eye + s, X3), n1, n2)
    n8 = each(lambda n: dnn(n, n, X3), n4)
    td = each(lambda t, n: dnn(t, eye + n, X3), td, n4)
    td = each(lambda t, n: dnn(t, eye + n, X3), td, n8)
    m = each(lambda t, l: dnn(t, l, X3), td, lo)
    m2 = each(lambda x: dnn(x, x, X3), m)
    x = each(lambda p, s: dnn(eye - p, eye + s, X3), m, m2)
    return each(lambda p, t: dnn(p, t, X3), x, td)


def _chunk_common(q, k, v, gcb, bb):
    egb = jnp.exp(gcb)
    gc64 = gcb[:, :CHUNK]
    ii, jj = _iota((CHUNK, CHUNK), 0), _iota((CHUNK, CHUNK), 1)
    incl, strict = ii >= jj, ii > jj
    decay = jnp.exp(jnp.where(incl, gc64 - gc64.T, -jnp.inf))
    kb = k * bb
    vb = v * bb
    kbe = kb * egb
    pq = dnt(jnp.concatenate([kb, q], axis=0), k, X3)
    ekb = jnp.exp(gcb[CHUNK - 1:CHUNK, :] - gcb)
    return dict(egb=egb, decay=decay, kb=kb, vb=vb, kbe=kbe, pm=pq[:CHUNK], qm=pq[CHUNK:], ekb=ekb,
                incl=incl, strict=strict, ii=ii, jj=jj)


def _chunk_head(vals, ci, h):
    return [v[ci * CHUNK:(ci + 1) * CHUNK, h * HD:(h + 1) * HD] for v in vals]


def _assemble(per_chunk):
    return jnp.concatenate([_cat(hs) for hs in per_chunk], axis=0)


def _assemble3(per_chunk):
    return jnp.stack([jnp.concatenate([per_chunk[ci][h] for ci in range(CPS)], axis=0) for h in range(DN_HEADS)])


def delta_prep_fwd(name, q, k, v, gb, bb):
    length = q.shape[0]

    def fn(ctx, rows, consts, prevs, nexts):
        gcb_all = _chunk_scan_rows(rows[3])
        vals = [rows[0], rows[1], rows[2], gcb_all, rows[4]]
        units = [(ci, h) for ci in range(CPS) for h in range(DN_HEADS)]
        ins = [_chunk_head(vals, ci, h) for ci, h in units]
        cs = [_chunk_common(*i) for i in ins]
        eye = (cs[0]["ii"] == cs[0]["jj"]).astype(F32)
        ts = _tri_inv([jnp.where(c["strict"], c["pm"] * c["decay"], 0.0) for c in cs], eye,
                      (cs[0]["ii"] >> 4) == (cs[0]["jj"] >> 4))
        uws = [dnn(t, _cat([c["vb"], c["kbe"]]), X3) for t, c in zip(ts, cs)]

        def grid2(xs):
            return [xs[ci * DN_HEADS:(ci + 1) * DN_HEADS] for ci in range(CPS)]

        return [_assemble(grid2([uw[:, :HD] for uw in uws])), _assemble(grid2([uw[:, HD:] for uw in uws])),
                _assemble(grid2([i[0] * c["egb"] for i, c in zip(ins, cs)])),
                _assemble(grid2([i[1] * c["ekb"] for i, c in zip(ins, cs)])), gcb_all,
                _assemble3(grid2([c["qm"] * c["decay"] for c in cs])), _assemble3(grid2(ts))], []

    return rowwise(name, fn, length, CHUNK * CPS, rows=[q, k, v, gb, bb],
                   out_rows=[(DN_WIDTH, F32)] * 5 + [(DN_HEADS, CHUNK, F32)] * 2)


def delta_prep_bwd(name, q, k, v, gb, bb, t3, du, dw, dqd, dkd, dattn3, dgl):
    length = q.shape[0]

    def fn(ctx, rows, consts, prevs, nexts):
        gcb_all = _chunk_scan_rows(rows[3])
        vals = [rows[0], rows[1], rows[2], gcb_all] + list(rows[4:9])
        t3v, da3v, dglv = rows[9], rows[10], rows[11]
        units = [(ci, h) for ci in range(CPS) for h in range(DN_HEADS)]
        ins = [_chunk_head(vals, ci, h) for ci, h in units]
        cs = [_chunk_common(*i[:5]) for i in ins]
        ts = [t3v[h][ci * CHUNK:(ci + 1) * CHUNK] for ci, h in units]
        dattns = [jnp.where(c["incl"], da3v[h][ci * CHUNK:(ci + 1) * CHUNK], 0.0) for (ci, h), c in zip(units, cs)]
        duws = [_cat([i[5], i[6]]) for i in ins]
        dvks = [dtn(t, d, X3) for t, d in zip(ts, duws)]
        dts = [dnt(d, _cat([c["vb"], c["kbe"]]), X3) for d, c in zip(duws, cs)]
        dts = [dnt(d, t, X3) for d, t in zip(dts, ts)]
        das = [jnp.where(c["strict"], -dtn(t, d, X3), 0.0) for c, t, d in zip(cs, ts, dts)]
        dpqs = [jnp.concatenate([da * c["decay"], dat * c["decay"]], axis=0) for da, dat, c in zip(das, dattns, cs)]
        dpqks = [dnn(d, i[1], X3) for d, i in zip(dpqs, ins)]
        dkps = [dtn(d, jnp.concatenate([c["kb"], i[0]], axis=0), X3) for d, c, i in zip(dpqs, cs, ins)]
        dqs, dks, dvs, dgcs, dbs = [], [], [], [], []
        for (ci, h), i, c, dvk, da, dattn, dpqk, dkp in zip(units, ins, cs, dvks, das, dattns, dpqks, dkps):
            qh, kh, vh, _, bh, _, _, dqdh, dkdh = i
            dvb, dkbe = dvk[:, :HD], dvk[:, HD:]
            dkb = dpqk[:CHUNK] + dkbe * c["egb"]
            c1 = _rsum(dkbe * c["kb"] + dqdh * qh) * c["egb"]
            c2 = _rsum(dkdh * kh) * c["ekb"]
            e = (da * c["pm"] + dattn * c["qm"]) * c["decay"]
            dgc = c1 - c2 + _rsum(e) - _rsum(e.T)
            dgl_tot = jnp.max(dglv[ci * 8:(ci + 1) * 8, h * HD:(h + 1) * HD], axis=0, keepdims=True) + _csum(c2)
            dgcs.append(dgc + jnp.where(_iota((CHUNK, HD), 0) == CHUNK - 1, dgl_tot, 0.0))
            dqs.append(dpqk[CHUNK:] + dqdh * c["egb"])
            dks.append(dkp + dkdh * c["ekb"] + dkb * bh)
            dvs.append(dvb * bh)
            dbs.append(jnp.broadcast_to(_rsum(dkb * kh + dvb * vh), (CHUNK, HD)))

        def grid2(xs):
            return [xs[ci * DN_HEADS:(ci + 1) * DN_HEADS] for ci in range(CPS)]

        return [_assemble(grid2(dqs)), _assemble(grid2(dks)), _assemble(grid2(dvs)),
                _chunk_scan_rows(_assemble(grid2(dgcs)), suffix=True), _assemble(grid2(dbs))], []

    return rowwise(name, fn, length, CHUNK * CPS,
                   rows=[q, k, v, gb, bb, du, dw, dqd, dkd, t3, dattn3, (dgl, DN_WIDTH, 0, 8 * CPS)],
                   out_rows=[(DN_WIDTH, F32)] * 5)


def delta_scan_fwd(name, qd, kd, u, w, attn3, gcb):
    length = qd.shape[0]
    n = length // CHUNK
    row = pl.BlockSpec((CHUNK, DN_WIDTH), lambda c: (c, 0))
    sq = pl.BlockSpec((DN_HEADS, CHUNK, CHUNK), lambda c: (0, c, 0))

    def body(qd_ref, kd_ref, u_ref, w_ref, attn_ref, gc_ref, o_ref, vn_ref, st_ref, s_ref):
        c = pl.program_id(0)

        @pl.when(c == 0)
        def _():
            s_ref[...] = jnp.zeros_like(s_ref)

        heads = range(DN_HEADS)
        sls = [pl.ds(h * HD, HD) for h in heads]
        ss = [s_ref[h] for h in heads]
        ws = [dnn(w_ref[:, sl], s) for sl, s in zip(sls, ss)]
        qs = [dnn(qd_ref[:, sl], s) for sl, s in zip(sls, ss)]
        vns = [u_ref[:, sl] - x for sl, x in zip(sls, ws)]
        avs = [dnn(attn_ref[h], vn) for h, vn in zip(heads, vns)]
        kvs = [dtn(kd_ref[:, sl], vn) for sl, vn in zip(sls, vns)]
        for h, sl in zip(heads, sls):
            st_ref[0, h] = ss[h]
            o_ref[:, sl] = qs[h] + avs[h]
            vn_ref[:, sl] = vns[h]
            s_ref[h] = ss[h] * jnp.exp(gc_ref[pl.ds(CHUNK - 1, 1), sl]) + kvs[h]

    return pl.pallas_call(
        body, name=name, grid=(n,), in_specs=[row, row, row, row, sq, row],
        out_specs=[row, row, pl.BlockSpec((1, DN_HEADS, HD, HD), lambda c: (c, 0, 0, 0))],
        out_shape=[jax.ShapeDtypeStruct((length, DN_WIDTH), F32), jax.ShapeDtypeStruct((length, DN_WIDTH), F32),
                   jax.ShapeDtypeStruct((n, DN_HEADS, HD, HD), F32)],
        scratch_shapes=[pltpu.VMEM((DN_HEADS, HD, HD), F32)],
        compiler_params=_params(("arbitrary",)),
    )(qd, kd, u, w, attn3, gcb)


def delta_scan_bwd(name, do, qd, kd, w, attn3, vn, st, gcb):
    length = qd.shape[0]
    n = length // CHUNK
    row = pl.BlockSpec((CHUNK, DN_WIDTH), lambda c: (n - 1 - c, 0))
    sq = pl.BlockSpec((DN_HEADS, CHUNK, CHUNK), lambda c: (0, n - 1 - c, 0))
    stb = pl.BlockSpec((1, DN_HEADS, HD, HD), lambda c: (n - 1 - c, 0, 0, 0))
    glb = pl.BlockSpec((8, DN_WIDTH), lambda c: (n - 1 - c, 0))

    def body(do_ref, qd_ref, kd_ref, w_ref, attn_ref, vn_ref, st_ref, gc_ref,
             dqd_ref, dkd_ref, du_ref, dw_ref, dattn_ref, dgl_ref, ds_ref):
        c = pl.program_id(0)

        @pl.when(c == 0)
        def _():
            ds_ref[...] = jnp.zeros_like(ds_ref)

        heads = range(DN_HEADS)
        sls = [pl.ds(h * HD, HD) for h in heads]
        ss = [st_ref[0, h] for h in heads]
        dsns = [ds_ref[h] for h in heads]
        dos = [do_ref[:, sl] for sl in sls]
        vns = [vn_ref[:, sl] for sl in sls]
        dvns = [dtn(attn_ref[h], d) for h, d in zip(heads, dos)]
        dvns = [x + dnn(kd_ref[:, sl], dsn) for x, sl, dsn in zip(dvns, sls, dsns)]
        qdos = [dtn(qd_ref[:, sl], d) for sl, d in zip(sls, dos)]
        for h, sl in zip(heads, sls):
            dattn_ref[h] = dnt(dos[h], vns[h])
            dqd_ref[:, sl] = dnt(dos[h], ss[h])
            dkd_ref[:, sl] = dnt(vns[h], dsns[h])
            du_ref[:, sl] = dvns[h]
        dws = [dnt(dvn, s) for dvn, s in zip(dvns, ss)]
        wdvs = [dtn(w_ref[:, sl], dvn) for sl, dvn in zip(sls, dvns)]
        for h, sl in zip(heads, sls):
            egl = jnp.exp(gc_ref[pl.ds(CHUNK - 1, 1), sl])
            dw_ref[:, sl] = -dws[h]
            dgl_ref[:, sl] = jnp.broadcast_to(_csum(_rsum(dsns[h] * ss[h])) * egl, (8, HD))
            ds_ref[h] = dsns[h] * egl + qdos[h] - wdvs[h]

    return pl.pallas_call(
        body, name=name, grid=(n,), in_specs=[row, row, row, row, sq, row, stb, row],
        out_specs=[row, row, row, row, sq, glb],
        out_shape=[jax.ShapeDtypeStruct((length, DN_WIDTH), F32)] * 4
        + [jax.ShapeDtypeStruct((DN_HEADS, length, CHUNK), F32), jax.ShapeDtypeStruct((n * 8, DN_WIDTH), F32)],
        scratch_shapes=[pltpu.VMEM((DN_HEADS, HD, HD), F32)],
        compiler_params=_params(("arbitrary",)),
    )(do, qd, kd, w, attn3, vn, st, gcb)


def onorm_fwd(name, o, z, nw, tl=256):
    length = o.shape[0]

    def fn(ctx, rows, consts, prevs, nexts):
        outs = []
        for oh, zh in zip(_heads(rows[0], DN_HEADS, HD), _heads(rows[1], DN_HEADS, HD)):
            r = lax.rsqrt(jnp.mean(oh * oh, axis=1, keepdims=True) + RMS_EPS)
            outs.append(oh * r * consts[0] * _silu(zh))
        return [_cat(outs)], []

    return rowwise(name, fn, length, min(tl, length), rows=[o, z], consts=[nw], out_rows=[(DN_WIDTH, BF16)])[0]


def onorm_bwd(name, o, z, d_on, nw, tl=256):
    length = o.shape[0]

    def fn(ctx, rows, consts, prevs, nexts):
        dos, dzs = [], []
        dnw = jnp.zeros((1, HD), F32)
        for oh, zh, dh in zip(*[_heads(r, DN_HEADS, HD) for r in rows]):
            r = lax.rsqrt(jnp.mean(oh * oh, axis=1, keepdims=True) + RMS_EPS)
            y = oh * r
            sz = _silu(zh)
            t = dh * sz * consts[0]
            dos.append(r * (t - y * jnp.mean(t * y, axis=1, keepdims=True)))
            dzs.append(dh * y * consts[0] * _dsilu(zh))
            dnw = dnw + _csum(dh * y * sz)
        return [_cat(dos), _cat(dzs)], [dnw]

    return rowwise(name, fn, length, min(tl, length), rows=[o, z, d_on], consts=[nw],
                   out_rows=[(DN_WIDTH, F32), (DN_WIDTH, BF16)], out_accs=[((1, HD), F32)])


def merge_fwd(name, gates, ydn, ypool, tl=256):
    length = ydn.shape[0]

    def fn(ctx, rows, consts, prevs, nexts):
        gt = rows[0]
        return [_sigmoid(gt[:, :D_MODEL]) * rows[1] + _sigmoid(gt[:, D_MODEL:]) * rows[2]], []

    return rowwise(name, fn, length, min(tl, length), rows=[gates, ydn, ypool], out_rows=[(D_MODEL, BF16)])[0]


def merge_bwd(name, gates, ydn, ypool, dm, tl=256):
    length = ydn.shape[0]

    def fn(ctx, rows, consts, prevs, nexts):
        gt, yd, yp, d = rows
        sd, sp = _sigmoid(gt[:, :D_MODEL]), _sigmoid(gt[:, D_MODEL:])
        dgates = _cat([d * yd * sd * (1.0 - sd), d * yp * sp * (1.0 - sp)])
        return [d * sd, d * sp, dgates], []

    return rowwise(name, fn, length, min(tl, length), rows=[gates, ydn, ypool, dm],
                   out_rows=[(D_MODEL, BF16), (D_MODEL, BF16), (2 * D_MODEL, BF16)])


def _trailing_sums(ext, upto):
    s, sh = ext, 1
    while sh < upto:
        s = s + pltpu.roll(s, sh, 0)
        sh *= 2
    return s


def _leading_sums(ext, upto, n):
    s, sh = ext, 1
    while sh < upto:
        s = s + pltpu.roll(s, n - sh, 0)
        sh *= 2
    return s


def _pool_mixed(ctx, p, prev, tl):
    prevm = jnp.where(ctx.i > 0, prev, 0.0)
    t1 = (_row_index(ctx, tl) + 1).astype(F32)
    outs = []
    for gi, win in enumerate(POOL_WINDOWS):
        sl = slice(gi * HD, (gi + 1) * HD)
        ext = jnp.concatenate([prevm[:, sl], p[:, sl]], axis=0)
        mean = _trailing_sums(ext, win)[HALO:] / jnp.minimum(t1, float(win))
        outs.append(mean - p[:, sl])
    return outs


def pool_fwd(name, p, pool_w, scale, tl=256):
    length = p.shape[0]
    tl = min(tl, length)

    def fn(ctx, rows, consts, prevs, nexts):
        mixed = _pool_mixed(ctx, rows[0], prevs[0], tl)
        y = _cat([dnn(m, consts[0][gi]) for gi, m in enumerate(mixed)])
        return [y * consts[1]], []

    return rowwise(name, fn, length, tl, rows=[p], consts=[pool_w, scale], prevs=[p],
                   out_rows=[(POOL_WIDTH, BF16)])[0]


def pool_bwd(name, p, dpo, pool_w, scale, tl=256):
    length = p.shape[0]
    tl = min(tl, length)
    n = tl + HALO

    def fn(ctx, rows, consts, prevs, nexts):
        last = ctx.i == ctx.nblk - 1
        mixed = _pool_mixed(ctx, rows[0], prevs[0], tl)
        dext = jnp.concatenate([rows[1], jnp.where(last, 0.0, nexts[0])], axis=0)
        t1 = (_row_index(ctx, n) + 1).astype(F32)
        dps, dws, dscs = [], [], []
        for gi, win in enumerate(POOL_WINDOWS):
            sl = slice(gi * HD, (gi + 1) * HD)
            wg = consts[0][gi]
            dyraw = dext[:, sl] * consts[1][:, sl]
            dmix = dnt(dyraw, wg)
            dws.append(dtn(mixed[gi], dyraw[:tl]))
            dscs.append(_csum(rows[1][:, sl] * dnn(mixed[gi], wg)))
            lead = _leading_sums(dmix / jnp.minimum(t1, float(win)), win, n)
            dps.append(lead[:tl] - dmix[:tl])
        return [_cat(dps)], [jnp.stack(dws), _cat(dscs)]

    return rowwise(name, fn, length, tl, rows=[p, dpo], consts=[pool_w, scale], prevs=[p], nexts=[dpo],
                   out_rows=[(POOL_WIDTH, BF16)],
                   out_accs=[((len(POOL_WINDOWS), HD, HD), F32), ((1, POOL_WIDTH), F32)])


def _xa_probs(qh, kh):
    s = dnt(qh, kh) * (XA_HD ** -0.5)
    e = jnp.exp(s - jnp.max(s, axis=1, keepdims=True))
    return e / _rsum(e)


def xattn_fwd(name, qx, kx, vx, tl=256):
    length = qx.shape[0]

    def fn(ctx, rows, consts, prevs, nexts):
        outs = [dnn(_xa_probs(qh, kh), vh) for qh, kh, vh in
                zip(_heads(rows[0], XA_HEADS, XA_HD), _heads(consts[0], XA_HEADS, XA_HD),
                    _heads(consts[1], XA_HEADS, XA_HD))]
        return [_cat(outs)], []

    return rowwise(name, fn, length, min(tl, length), rows=[qx], consts=[kx, vx], out_rows=[(D_MODEL, BF16)])[0]


def xattn_bwd(name, qx, dox, kx, vx, tl=256):
    length = qx.shape[0]

    def fn(ctx, rows, consts, prevs, nexts):
        dqs, dks, dvs = [], [], []
        for qh, dh, kh, vh in zip(_heads(rows[0], XA_HEADS, XA_HD), _heads(rows[1], XA_HEADS, XA_HD),
                                  _heads(consts[0], XA_HEADS, XA_HD), _heads(consts[1], XA_HEADS, XA_HD)):
            pr = _xa_probs(qh, kh)
            dpr = dnt(dh, vh)
            ds = pr * (dpr - _rsum(dpr * pr)) * (XA_HD ** -0.5)
            dqs.append(dnn(ds, kh))
            dks.append(dtn(ds, qh))
            dvs.append(dtn(pr, dh))
        return [_cat(dqs)], [_cat(dks), _cat(dvs)]

    return rowwise(name, fn, length, min(tl, length), rows=[qx, dox], consts=[kx, vx],
                   out_rows=[(D_MODEL, BF16)], out_accs=[((N_MEM, D_MODEL), F32)] * 2)


def local_step(x, mem, target, w, io):
    sel, pick = _gate_consts()
    alog = jnp.pad(w["a_log"], ((0, 0), (0, 128 - DN_HEADS)))
    dtb = jnp.pad(w["dt_bias"], ((0, 0), (0, 128 - DN_HEADS)))

    f1, res1, w_down1 = ffn_fwd("ffn1", x, w["ffn1_w_gate"], w["ffn1_w_up"], io.ffn1_down, deps=io.rest_started())
    x1, r1 = ln_fwd("ln1", [(ALPHA, x), (0.5, f1)], w["ln1_g"], w["ln1_b"], deps=io.halfway("mid", f1))
    w = dict(w, ffn1_w_down=w_down1, **io.weights("mid", x1))
    taps = [w["conv_w"][j:j + 1] for j in range(4)]

    pre = mm("in_qkv", x1, w["in_qkv"], tb=True)
    z = mm("in_z", x1, w["in_z"], tb=True)
    gates = mm("in_gates", x1, w["in_gates"], tb=True)
    p = mm("in_p", x1, w["in_p"], tb=True)
    ab = mm("in_ab", x1, w["in_ab"], tb=True)
    q, k, v = conv_fwd("conv", pre, taps)
    gb, bb = gates_fwd("gates", ab, alog, dtb, sel)
    u, wd_, qd, kd, gcb, attn3, t3 = delta_prep_fwd("dprep", q, k, v, gb, bb)
    o, vn, st = delta_scan_fwd("dscan", qd, kd, u, wd_, attn3, gcb)
    on = onorm_fwd("onorm", o, z, w["dn_norm_w"])
    ydn = mm("dn_branch", on, w["w_dn_branch"], tb=True)
    po = pool_fwd("pool", p, w["pool_w"], w["pool_scale"])
    ypool = mm("pool_branch", po, w["w_pool_branch"], tb=True)
    merged = merge_fwd("merge", gates, ydn, ypool)
    mix = mm("mix_out", merged, w["w_mix_out"])
    x2, r2 = ln_fwd("ln2", [(ALPHA, x1), (1.0, mix)], w["ln2_g"], w["ln2_b"])

    m, _ = ln_fwd("ln_mem", [(1.0, mem)], w["mem_ln_g"], w["mem_ln_b"])
    qx = mm("xa_q", x2, w["xa_wq"], deps=io.halfway("ffn2", x2))
    kx = mm("xa_k", m, w["xa_wk"])
    vx = mm("xa_v", m, w["xa_wv"])
    ox = xattn_fwd("xattn", qx, kx, vx)
    xa = mm("xa_o", ox, w["xa_wo"])
    x3, r3 = ln_fwd("ln3", [(ALPHA, x2), (1.0, xa)], w["ln3_g"], w["ln3_b"])
    w = dict(w, **io.weights("ffn2", x3))

    f2, res2, _ = ffn_fwd("ffn2", x3, w["ffn2_w_gate"], w["ffn2_w_up"], w["ffn2_w_down"])
    dy4, r4, loss = ln_loss("ln4_loss", [(ALPHA, x3), (0.5, f2)], w["ln4_g"], w["ln4_b"], target)

    g = {}
    dr4, g["ln4_g"], g["ln4_b"] = ln_bwd("ln4_b", r4, [(1.0, dy4)], w["ln4_g"])
    dx3, g["ffn2_w_gate"], g["ffn2_w_up"], g["ffn2_w_down"] = ffn_bwd(
        "ffn2b", x3, res2, dr4, w["ffn2_w_gate"], w["ffn2_w_up"], w["ffn2_w_down"])
    dep = io.grads_out("ffn2", g)
    dr3, g["ln3_g"], g["ln3_b"] = ln_bwd("ln3_b", r3, [(ALPHA, dr4), (1.0, dx3)], w["ln3_g"], deps=dep)

    dox = mm("xa_do", dr3, w["xa_wo"], tb=True)
    g["xa_wo"] = mm("xa_dwo", ox, dr3, ta=True)
    dqx, dkx, dvx = xattn_bwd("xattn_b", qx, dox, kx, vx)
    g["xa_wq"] = mm("xa_dwq", x2, dqx, ta=True)
    dx2 = mm("xa_dx", dqx, w["xa_wq"], tb=True)
    g["xa_wk"] = mm("xa_dwk", m, dkx, ta=True)
    g["xa_wv"] = mm("xa_dwv", m, dvx, ta=True)
    dmm = mm("xa_dmk", dkx, w["xa_wk"], tb=True, deps=io.grads_out("xa", g))
    dmm = mm("xa_dmv", dvx, w["xa_wv"], tb=True, add=dmm)
    _, g["mem_ln_g"], g["mem_ln_b"] = ln_bwd("ln_mem_b", mem, [(1.0, dmm)], w["mem_ln_g"])
    dr2, g["ln2_g"], g["ln2_b"] = ln_bwd("ln2_b", r2, [(ALPHA, dr3), (1.0, dx2)], w["ln2_g"])
    io.grads_in("ffn2", dr2)

    dmerged = mm("mix_dm", dr2, w["w_mix_out"], tb=True)
    g["w_mix_out"] = mm("mix_dw", merged, dr2, ta=True)
    d_ydn, d_ypool, d_gates = merge_bwd("merge_b", gates, ydn, ypool, dmerged)
    g["w_dn_branch"] = mm("dn_dw", d_ydn, on, ta=True)
    d_on = mm("dn_dx", d_ydn, w["w_dn_branch"])
    g["w_pool_branch"] = mm("pool_dw", d_ypool, po, ta=True)
    d_po = mm("pool_dx", d_ypool, w["w_pool_branch"])
    dp, g["pool_w"], g["pool_scale"] = pool_bwd("pool_b", p, d_po, w["pool_w"], w["pool_scale"])
    d_o, dz, g["dn_norm_w"] = onorm_bwd("onorm_b", o, z, d_on, w["dn_norm_w"])
    dqd, dkd, du, dw_, dattn3, dgl = delta_scan_bwd("dscan_b", d_o, qd, kd, wd_, attn3, vn, st, gcb)
    dq, dk, dv, dgb, dbb = delta_prep_bwd("dprep_b", q, k, v, gb, bb, t3, du, dw_, dqd, dkd, dattn3, dgl)
    dpre, dc0, dc1, dc2, dc3 = conv_bwd("conv_b", pre, dq, dk, dv, taps)
    g["conv_w"] = jnp.concatenate([dc0, dc1, dc2, dc3], axis=0)
    d_ab, dalog, ddtb = gates_bwd("gates_b", ab, dgb, dbb, alog, dtb, pick)
    g["a_log"] = dalog[:, :DN_HEADS]
    g["dt_bias"] = ddtb[:, :DN_HEADS]
    g["in_qkv"] = mm("in_dwqkv", dpre, x1, ta=True)
    g["in_z"] = mm("in_dwz", dz, x1, ta=True)
    g["in_gates"] = mm("in_dwgates", d_gates, x1, ta=True)
    g["in_p"] = mm("in_dwp", dp, x1, ta=True)
    g["in_ab"] = mm("in_dwab", d_ab, x1, ta=True)
    io.grads_in("xa", g["in_ab"])
    dx1 = mm("in_dxqkv", dpre, w["in_qkv"], deps=io.grads_out("mixer", g))
    dx1 = mm("in_dxz", dz, w["in_z"], add=dx1)
    dx1 = mm("in_dxgates", d_gates, w["in_gates"], add=dx1)
    dx1 = mm("in_dxp", dp, w["in_p"], add=dx1)
    dx1 = mm("in_dxab", d_ab, w["in_ab"], add=dx1)
    dr1, g["ln1_g"], g["ln1_b"] = ln_bwd("ln1_b", r1, [(ALPHA, dr2), (1.0, dx1)], w["ln1_g"])

    def on_dwd(dwd):
        return io.small_out(dict(g, loss=loss[0, :1])) + io.grads_out("ffn1_d", dict(ffn1_w_down=dwd))

    def on_dwgu(dwg, dwu):
        return io.grads_out("ffn1_gu", dict(ffn1_w_gate=dwg, ffn1_w_up=dwu))

    dx0, g["ffn1_w_gate"], g["ffn1_w_up"], g["ffn1_w_down"] = ffn_bwd(
        "ffn1b", x, res1, dr1, w["ffn1_w_gate"], w["ffn1_w_up"], w["ffn1_w_down"], on_dwd=on_dwd, on_dwgu=on_dwgu)
    grad_x = axpy("grad_x", [(ALPHA, dr1), (1.0, dx0)])
    return loss, grad_x, g


WEIGHT_NAMES = ['ffn1_w_gate', 'ffn1_w_up', 'ffn1_w_down', 'ln1_g', 'ln1_b', 'w_in', 'conv_w', 'a_log', 'dt_bias',
                'dn_norm_w', 'w_dn_branch', 'pool_w', 'pool_scale', 'w_pool_branch', 'w_mix_out', 'ln2_g', 'ln2_b',
                'mem_ln_g', 'mem_ln_b', 'xa_wq', 'xa_wk', 'xa_wv', 'xa_wo', 'ln3_g', 'ln3_b', 'ffn2_w_gate',
                'ffn2_w_up', 'ffn2_w_down', 'ln4_g', 'ln4_b']
SHARDED = [
    ("ffn1_w_gate", "cols", (1024, 352)), ("ffn1_w_up", "cols", (1024, 352)), ("ffn1_w_down", "rows", (352, 1024)),
    ("w_in", "cols", (1024, 577)), ("conv_w", "flat", (4, 192)), ("w_dn_branch", "cols", (512, 128)),
    ("w_pool_branch", "cols", (512, 128)), ("w_mix_out", "rows", (128, 1024)), ("xa_wq", "rows", (128, 1024)),
    ("xa_wk", "rows", (128, 1024)), ("xa_wv", "rows", (128, 1024)), ("xa_wo", "rows", (128, 1024)),
    ("ffn2_w_gate", "cols", (1024, 352)), ("ffn2_w_up", "cols", (1024, 352)), ("ffn2_w_down", "rows", (352, 1024)),
]
REPLICATED = [n for n in WEIGHT_NAMES if n not in {s[0] for s in SHARDED}]
ROW_ALIGN = 16
ROW_BLOCKS = (512, 384, 352, 256, 192, 176, 128)
GROUPS = {"ffn1_gu": ("ffn1_w_gate", "ffn1_w_up"), "ffn1_d": ("ffn1_w_down",),
          "mixer": ("w_in", "conv_w", "w_dn_branch", "w_pool_branch", "w_mix_out"),
          "xa": ("xa_wq", "xa_wk", "xa_wv", "xa_wo"),
          "ffn2": ("ffn2_w_gate", "ffn2_w_up", "ffn2_w_down")}
GROUPS["mid"] = GROUPS["mixer"] + GROUPS["xa"]
W_IN_COLS = 577
W_IN_PIECES = (("in_qkv", 0, 1536), ("in_z", 1536, 2048), ("in_ab", 2048, 2056), ("in_p", 2056, 2568),
               ("in_gates", 2568, 4616))


def _round_up(n, m):
    return -(-n // m) * m


def _layout():
    off, table = 0, {}
    for name, form, shape in SHARDED:
        valid = {"rows": shape[0], "cols": shape[1], "flat": 2}[form]
        width = {"rows": shape[1], "cols": shape[0], "flat": shape[0] * shape[1]}[form]
        rows = _round_up(valid, ROW_ALIGN)
        table[name] = (off, rows, valid, width, form, shape)
        off += rows
    return table


LAYOUT = _layout()


def _group_span(names):
    base = LAYOUT[names[0]][0]
    rows = LAYOUT[names[-1]][0] + LAYOUT[names[-1]][1] - base
    while not any(rows % b == 0 for b in ROW_BLOCKS):
        rows += ROW_ALIGN
    return base, rows


def _row_block(rows):
    return _pick(rows, ROW_BLOCKS)


def _pad_block(blk, rows):
    return jnp.pad(blk, ((0, rows - blk.shape[0]), (0, LANES - blk.shape[1])))


def pack_weight_shards(shards, names):
    parts, used = [], 0
    for name in names:
        off, rows, valid, width, form, _ = LAYOUT[name]
        s = shards[name]
        if form == "flat":
            flat = s.reshape(1, -1)
            hi = flat.astype(BF16)
            blk = jnp.concatenate([hi, (flat - hi.astype(F32)).astype(BF16)], axis=0)
        else:
            blk = (s.T if form == "cols" else s).astype(BF16)
        parts.append(_pad_block(blk, rows))
        used += rows
    if _group_span(names)[1] > used:
        parts.append(jnp.zeros((_group_span(names)[1] - used, LANES), BF16))
    return jnp.concatenate(parts, axis=0)


def _w_in_rows(padded, rows, first, last):
    segs = []
    for k in range(N_DEV):
        lo, hi = max(first, k * W_IN_COLS), min(last, (k + 1) * W_IN_COLS)
        if lo < hi:
            segs.append(padded[k * rows + lo - k * W_IN_COLS:k * rows + hi - k * W_IN_COLS])
    return segs[0] if len(segs) == 1 else jnp.concatenate(segs, axis=0)


def unpack_full_weights(gathered, names):
    out, base = {}, _group_span(names)[0]
    for name in names:
        off, rows, valid, width, form, shape = LAYOUT[name]
        seg = gathered[:, off - base:off - base + rows]
        if form == "flat":
            flat = seg[:, 0, :width].astype(F32) + seg[:, 1, :width].astype(F32)
            out[name] = flat.reshape((N_DEV,) + shape).transpose(1, 0, 2).reshape(shape[0], N_DEV * shape[1])
        elif name == "w_in":
            padded = seg.reshape(N_DEV * rows, LANES)
            for piece, first, last in W_IN_PIECES:
                out[piece] = _w_in_rows(padded, rows, first, last)
        else:
            out[name] = seg[:, :valid, :width].reshape(N_DEV * valid, width)
    return out


def pack_full_grads(grads, names, me):
    wire, own, used = [], [], 0
    for name in names:
        off, rows, valid, width, form, shape = LAYOUT[name]
        if form == "flat":
            full = grads[name].reshape(shape[0], N_DEV, shape[1]).transpose(1, 0, 2).reshape(N_DEV, 1, width)
        elif name == "w_in":
            full = jnp.concatenate([grads[piece][:last - first] for piece, first, last in W_IN_PIECES], axis=0)
            full = full.reshape(N_DEV, valid, width)
        else:
            full = grads[name].reshape(N_DEV, valid, width)
        pad = ((0, rows - full.shape[1]), (0, LANES - width))
        wire.append(jnp.pad(full.astype(WIRE), ((0, 0),) + pad))
        own.append(jnp.pad(lax.dynamic_index_in_dim(full, me, 0, keepdims=False), pad))
        used += rows
    if _group_span(names)[1] > used:
        wire.append(jnp.zeros((N_DEV, _group_span(names)[1] - used, LANES), WIRE))
        own.append(jnp.zeros((_group_span(names)[1] - used, LANES), F32))
    return jnp.concatenate(wire, axis=1), jnp.concatenate(own, axis=0)


def unpack_grad_shards(packed, names):
    out, base = {}, _group_span(names)[0]
    for name in names:
        off, rows, valid, width, form, shape = LAYOUT[name]
        off -= base
        if form == "flat":
            out[name] = packed[off, :width].reshape(shape)
        elif form == "cols":
            out[name] = packed[off:off + valid, :width].T
        else:
            out[name] = packed[off:off + valid, :width]
    return out


SMALL_SHAPES = {n: (1024,) for n in REPLICATED}
SMALL_SHAPES.update(pool_w=(4, 128, 128), pool_scale=(512,), dn_norm_w=(128,), a_log=(4,), dt_bias=(4,))


SMALL_SHAPES["loss"] = (1,)
SMALL_NAMES = REPLICATED + ["loss"]


def _small_layout():
    off, table = 0, {}
    for name in SMALL_NAMES:
        numel = 1
        for d in SMALL_SHAPES[name]:
            numel *= d
        rows = _round_up(-(-numel // LANES), 8)
        table[name] = (off, rows, numel)
        off += rows
    return table, off


SMALL_LAYOUT, SMALL_ROWS = _small_layout()


def _to_rows(flat, rows):
    return jnp.pad(flat, (0, rows * LANES - flat.shape[0])).reshape(rows, LANES)


def pack_small(values):
    return jnp.concatenate([_to_rows(values[name].reshape(-1), SMALL_LAYOUT[name][1]) for name in SMALL_NAMES], axis=0)


def unpack_small(packed):
    out = {}
    for name in SMALL_NAMES:
        off, rows, numel = SMALL_LAYOUT[name]
        out[name] = packed[off:off + rows].reshape(-1)[:numel].reshape(SMALL_SHAPES[name])
    return out


MESH = pl.DeviceIdType.MESH


def _position():
    return lax.axis_index("x"), lax.axis_index("y"), lax.axis_index("c")


def _other_chips(x, y):
    return [(1 - x, y), (x, 1 - y), (1 - x, 1 - y)]


def all_gather(name, block):
    rows, n = block.shape

    def body(x_ref, out_ref, send_sems, recv_sems, local_sem):
        x, y, c = _position()
        me, sibling = (x, y, c), (x, y, 1 - c)
        chips = _other_chips(x, y)

        def slot(px, py, pc):
            return out_ref.at[4 * px + 2 * py + pc]

        def copy(k, blk, to, src=None):
            return pltpu.make_async_remote_copy(
                src_ref=slot(*blk) if src is None else src, dst_ref=slot(*blk),
                send_sem=send_sems.at[k], recv_sem=recv_sems.at[k], device_id=to, device_id_type=MESH)

        mine = pltpu.make_async_copy(x_ref, slot(*me), local_sem)
        mine.start()
        first = [copy(0, me, sibling, src=x_ref)]
        first += [copy(1 + j, me, (*chip, c), src=x_ref) for j, chip in enumerate(chips)]
        for cp in first:
            cp.start()
        passed = [copy(4 + j, (*chip, c), sibling) for j, chip in enumerate(chips)]
        for j, chip in enumerate(chips):
            copy(1 + j, (*chip, c), me).wait_recv()
            passed[j].start()
        copy(0, sibling, me).wait_recv()
        for j, chip in enumerate(chips):
            copy(4 + j, (*chip, 1 - c), me).wait_recv()
        for cp in first + passed:
            cp.wait_send()
        mine.wait()

    return pl.pallas_call(
        body, name=name, out_shape=jax.ShapeDtypeStruct((N_DEV, rows, n), block.dtype),
        in_specs=[ANY], out_specs=ANY,
        scratch_shapes=[pltpu.SemaphoreType.DMA((7,)), pltpu.SemaphoreType.DMA((7,)), pltpu.SemaphoreType.DMA(())],
    )(block)


HBM = pl.BlockSpec(memory_space=pltpu.HBM)
SEM = pl.BlockSpec(memory_space=pltpu.SEMAPHORE)
EFFECT = pltpu.SideEffectType.DATAFLOW_SIDE_EFFECTING


def _remote(src, dst, send_sem, recv_sem, to):
    return pltpu.make_async_remote_copy(src_ref=src, dst_ref=dst, send_sem=send_sem, recv_sem=recv_sem,
                                        device_id=to, device_id_type=MESH)


def split_start(name, bufs, n, make_copies):
    nb = len(bufs)

    def body(*refs):
        for out_cp, _ in make_copies(refs[:nb], refs[nb:nb + n], refs[nb + n:nb + 2 * n]):
            out_cp.start()
        refs[-1][...] = jnp.zeros_like(refs[-1])

    outs = pl.pallas_call(
        body, name=name,
        out_shape=tuple([pltpu.SemaphoreType.DMA(())] * (2 * n)) + tuple(pltpu.HBM(b.shape, b.dtype) for b in bufs)
        + (jax.ShapeDtypeStruct((8, 128), F32),),
        in_specs=[HBM] * nb,
        out_specs=tuple([SEM] * (2 * n) + [HBM] * nb + [pl.BlockSpec(memory_space=pltpu.VMEM)]),
        input_output_aliases={i: 2 * n + i for i in range(nb)},
        compiler_params=pltpu.CompilerParams(has_side_effects=EFFECT),
    )(*[pltpu.with_memory_space_constraint(b, pltpu.HBM) for b in bufs])
    return list(outs[:2 * n]), list(outs[2 * n:2 * n + nb]), outs[-1]


def split_wait(name, bufs, sems, n, make_copies, after):
    nb = len(bufs)

    def body(*refs):
        for out_cp, in_cp in make_copies(refs[:nb], refs[nb:nb + n], refs[nb + n:nb + 2 * n]):
            out_cp.wait_send()
            in_cp.wait_recv()

    outs = pl.pallas_call(
        body, name=name, out_shape=tuple(pltpu.HBM(b.shape, b.dtype) for b in bufs),
        in_specs=[HBM] * nb + [SEM] * (2 * n) + [ANY], out_specs=tuple([HBM] * nb),
        input_output_aliases={i: i for i in range(nb)},
        compiler_params=pltpu.CompilerParams(has_side_effects=EFFECT),
    )(*bufs, *sems, after)
    return list(outs)


def _gather_stage1(refs, send, recv):
    src, land = refs
    x, y, c = _position()
    peers = [(x, y, 1 - c)] + [(*chip, c) for chip in _other_chips(x, y)]
    return [(_remote(src, land.at[4 * x + 2 * y + c], send[k], recv[k], p),
             _remote(src, land.at[4 * p[0] + 2 * p[1] + p[2]], send[k], recv[k], p)) for k, p in enumerate(peers)]


def _gather_stage2(refs, send, recv):
    (land,) = refs
    x, y, c = _position()
    out = []
    for j, (px, py) in enumerate(_other_chips(x, y)):
        mine, theirs = land.at[4 * px + 2 * py + c], land.at[4 * px + 2 * py + 1 - c]
        out.append((_remote(mine, mine, send[j], recv[j], (x, y, 1 - c)),
                    _remote(theirs, theirs, send[j], recv[j], (x, y, 1 - c))))
    return out


def _flips():
    return [(a, b, d) for a in (0, 1) for b in (0, 1) for d in (0, 1) if a | b | d]


def _gather_direct(refs, send, recv):
    src, land = refs
    x, y, c = _position()
    out = []
    for k, (fx, fy, fc) in enumerate(_flips()):
        p = (1 - x if fx else x, 1 - y if fy else y, 1 - c if fc else c)
        out.append((_remote(src, land.at[4 * x + 2 * y + c], send[k], recv[k], p),
                    _remote(src, land.at[4 * p[0] + 2 * p[1] + p[2]], send[k], recv[k], p)))
    return out


def _scatter_direct(refs, send, recv):
    sendbuf, land = refs
    x, y, c = _position()
    me = 4 * x + 2 * y + c
    out = []
    for k, (fx, fy, fc) in enumerate(_flips()):
        p = (1 - x if fx else x, 1 - y if fy else y, 1 - c if fc else c)
        peer = 4 * p[0] + 2 * p[1] + p[2]
        out.append((_remote(sendbuf.at[peer], land.at[me], send[k], recv[k], p),
                    _remote(sendbuf.at[peer], land.at[peer], send[k], recv[k], p)))
    return out


def _own_plus_slots(name, own, landed):
    n, rows, _ = landed.shape
    tr = _row_block(rows)

    def body(g_ref, l_ref, o_ref):
        acc = g_ref[...]
        for j in range(n):
            acc = acc + l_ref[j].astype(F32)
        o_ref[...] = acc

    return pl.pallas_call(
        body, name=name, grid=(rows // tr,),
        in_specs=[pl.BlockSpec((tr, LANES), lambda i: (i, 0)), pl.BlockSpec((n, tr, LANES), lambda i: (0, i, 0))],
        out_specs=pl.BlockSpec((tr, LANES), lambda i: (i, 0)),
        out_shape=jax.ShapeDtypeStruct((rows, LANES), F32), compiler_params=_params(("parallel",)),
    )(own, landed)


def _sum_slots(name, stack):
    n, rows, _ = stack.shape

    def body(s_ref, o_ref):
        acc = s_ref[0]
        for j in range(1, n):
            acc = acc + s_ref[j]
        o_ref[...] = acc

    return pl.pallas_call(
        body, name=name, in_specs=[pl.BlockSpec(stack.shape, lambda: (0, 0, 0))],
        out_specs=pl.BlockSpec((rows, LANES), lambda: (0, 0)), out_shape=jax.ShapeDtypeStruct((rows, LANES), F32),
    )(stack)


def adamw(name, w, g, m, v):
    shape = w.shape
    last = shape[-1]
    w2, g2, m2, v2 = [a.reshape(-1, last) for a in (w, g, m, v)]
    rows = w2.shape[0]
    tr = 256 if rows % 256 == 0 else rows

    def body(w_ref, g_ref, m_ref, v_ref, d_ref, nm_ref, nv_ref):
        gg = g_ref[...]
        nm = ADAM_B1 * m_ref[...] + (1.0 - ADAM_B1) * gg
        nv = ADAM_B2 * v_ref[...] + (1.0 - ADAM_B2) * (gg * gg)
        m_hat = nm / (1.0 - ADAM_B1 ** ADAM_STEP)
        v_hat = nv / (1.0 - ADAM_B2 ** ADAM_STEP)
        d_ref[...] = -ADAM_LR * (m_hat / (jnp.sqrt(v_hat) + ADAM_EPS) + ADAM_WD * w_ref[...])
        nm_ref[...] = nm
        nv_ref[...] = nv

    spec = pl.BlockSpec((tr, last), lambda i: (i, 0))
    outs = pl.pallas_call(
        body, name=name, grid=(rows // tr,), in_specs=[spec] * 4, out_specs=[spec] * 3,
        out_shape=[jax.ShapeDtypeStruct((rows, last), F32)] * 3, compiler_params=_params(("parallel",)),
    )(w2, g2, m2, v2)
    return [o.reshape(shape) for o in outs]


def _landing(block_shape, dtype, own):
    x, y, c = _position()
    return lax.dynamic_update_slice(lax.empty((N_DEV,) + block_shape, dtype), own[None], (4 * x + 2 * y + c, 0, 0))


class _Exchanges:
    def __init__(self, shards):
        self.shards = shards
        self.pending = {}
        self.reduced = {}

    def first_weights(self):
        names = GROUPS["ffn1_gu"]
        return unpack_full_weights(all_gather("ag_ffn1_gu", pack_weight_shards(self.shards, names)), names)

    def rest_started(self):
        tokens = []
        block = pack_weight_shards(self.shards, GROUPS["ffn1_d"])
        sems, bufs, token = split_start("ag_ffn1_d_s", [block, _landing(block.shape, block.dtype, block)], N_DEV - 1,
                                        _gather_direct)
        self.pending["ffn1_d"] = (sems, bufs)
        tokens.append(token)
        for key in ("mid", "ffn2"):
            block = pack_weight_shards(self.shards, GROUPS[key])
            sems, bufs, token = split_start(f"ag_{key}_s1", [block, _landing(block.shape, block.dtype, block)], 4,
                                            _gather_stage1)
            self.pending[key] = (sems, bufs)
            tokens.append(token)
        return tuple(tokens)

    def ffn1_down(self, after):
        sems, bufs = self.pending.pop("ffn1_d")
        _, gathered = split_wait("ag_ffn1_d_w", bufs, sems, N_DEV - 1, _gather_direct, after)
        return unpack_full_weights(gathered, GROUPS["ffn1_d"])["ffn1_w_down"]

    def halfway(self, key, after):
        sems, bufs = self.pending.pop(key)
        _, land = split_wait(f"ag_{key}_w1", bufs, sems, 4, _gather_stage1, after)
        sems, bufs, token = split_start(f"ag_{key}_s2", [land], 3, _gather_stage2)
        self.pending[key] = (sems, bufs)
        return (token,)

    def weights(self, key, after):
        sems, bufs = self.pending.pop(key)
        (gathered,) = split_wait(f"ag_{key}_w2", bufs, sems, 3, _gather_stage2, after)
        w = unpack_full_weights(gathered, GROUPS[key])
        if "in_ab" in w:
            w["in_ab"] = jnp.pad(w["in_ab"], ((0, 128 - 2 * DN_HEADS), (0, 0)))
        return w

    def grads_out(self, key, grads):
        x, y, c = _position()
        wire, own = pack_full_grads(grads, GROUPS[key], 4 * x + 2 * y + c)
        land = _landing(wire.shape[1:], WIRE, jnp.zeros(wire.shape[1:], WIRE))
        sems, bufs, token = split_start(f"rs_{key}_start", [wire, land], N_DEV - 1, _scatter_direct)
        self.pending[key] = (sems, bufs, own)
        return (token,)

    def grads_in(self, key, after):
        sems, bufs, own = self.pending.pop(key)
        _, landed = split_wait(f"rs_{key}_wait", bufs, sems, N_DEV - 1, _scatter_direct, after)
        self.reduced.update(unpack_grad_shards(_own_plus_slots(f"rs_{key}_sum", own, landed), GROUPS[key]))

    def small_out(self, values):
        block = pack_small(values)
        sems, bufs, token = split_start("ag_small_s", [block, _landing(block.shape, block.dtype, block)], N_DEV - 1,
                                        _gather_direct)
        self.pending["small"] = (sems, bufs)
        return (token,)

    def small_in(self, after):
        sems, bufs = self.pending.pop("small")
        _, gathered = split_wait("ag_small_w", bufs, sems, N_DEV - 1, _gather_direct, after)
        return unpack_small(_sum_slots("small_sum", gathered))


def kernel(x, mem, ffn1_w_gate, ffn1_w_up, ffn1_w_down, ln1_g, ln1_b, w_in, conv_w, a_log, dt_bias, dn_norm_w, w_dn_branch, pool_w, pool_scale, w_pool_branch, w_mix_out, ln2_g, ln2_b, mem_ln_g, mem_ln_b, xa_wq, xa_wk, xa_wv, xa_wo, ln3_g, ln3_b, ffn2_w_gate, ffn2_w_up, ffn2_w_down, ln4_g, ln4_b, loss_target, m_ffn1_w_gate, m_ffn1_w_up, m_ffn1_w_down, m_ln1_g, m_ln1_b, m_w_in, m_conv_w, m_a_log, m_dt_bias, m_dn_norm_w, m_w_dn_branch, m_pool_w, m_pool_scale, m_w_pool_branch, m_w_mix_out, m_ln2_g, m_ln2_b, m_mem_ln_g, m_mem_ln_b, m_xa_wq, m_xa_wk, m_xa_wv, m_xa_wo, m_ln3_g, m_ln3_b, m_ffn2_w_gate, m_ffn2_w_up, m_ffn2_w_down, m_ln4_g, m_ln4_b, v_ffn1_w_gate, v_ffn1_w_up, v_ffn1_w_down, v_ln1_g, v_ln1_b, v_w_in, v_conv_w, v_a_log, v_dt_bias, v_dn_norm_w, v_w_dn_branch, v_pool_w, v_pool_scale, v_w_pool_branch, v_w_mix_out, v_ln2_g, v_ln2_b, v_mem_ln_g, v_mem_ln_b, v_xa_wq, v_xa_wk, v_xa_wv, v_xa_wo, v_ln3_g, v_ln3_b, v_ffn2_w_gate, v_ffn2_w_up, v_ffn2_w_down, v_ln4_g, v_ln4_b):
    given = dict(locals())
    shards = {n: given[n] for n in WEIGHT_NAMES}
    io = _Exchanges({n: shards[n][0] for n, _, _ in SHARDED})
    w = io.first_weights()
    for n in REPLICATED:
        w[n] = shards[n][0] if n == "pool_w" else shards[n]
    loss_part, grad_x, g = local_step(x[0], mem[0], loss_target[0], w, io)

    grad, updates = {}, {}

    def update(names, reduced):
        for n in names:
            grad[n] = reduced[n].reshape(shards[n].shape)
            updates[n] = adamw("adamw_" + n, shards[n], grad[n], given["m_" + n], given["v_" + n])
        return updates[names[-1]][0]

    update(GROUPS["ffn2"] + GROUPS["xa"], io.reduced)
    io.grads_in("mixer", grad_x)
    done = update(GROUPS["mixer"], io.reduced)
    small = io.small_in(done)
    loss = small.pop("loss")[0]
    done = update(REPLICATED, small)
    io.grads_in("ffn1_d", done)
    done = update(GROUPS["ffn1_d"], io.reduced)
    io.grads_in("ffn1_gu", done)
    update(GROUPS["ffn1_gu"], io.reduced)
    return (loss, grad_x[None], *[grad[n] for n in WEIGHT_NAMES], *[updates[n][0] for n in WEIGHT_NAMES],
            *[updates[n][1] for n in WEIGHT_NAMES], *[updates[n][2] for n in WEIGHT_NAMES])
```

```python
import functools

import jax
import jax.numpy as jnp
from jax import lax
from jax.experimental import pallas as pl
from jax.experimental.pallas import tpu as pltpu

F32 = jnp.float32
BF16 = jnp.bfloat16
MMD = BF16
WIRE = BF16
HI = lax.Precision.HIGHEST
X3 = lax.Precision.HIGH
VMEM_LIMIT_BYTES = 48 * 1024 * 1024

D_MODEL = 1024
D_FF = 2816
CHUNK = 64
N_MEM = 256
DN_HEADS = 4
HD = 128
DN_WIDTH = 512
POOL_WINDOWS = (2, 4, 8, 16)
POOL_WIDTH = 512
XA_HEADS = 4
XA_HD = 256
LN_EPS = 1e-5
RMS_EPS = 1e-6
L2_EPS = 1e-6
ALPHA = 2.0 ** 0.25
HALO = 16

ADAM_LR = 0.001
ADAM_B1 = 0.9
ADAM_B2 = 0.999
ADAM_EPS = 1e-08
ADAM_WD = 0.01
ADAM_STEP = 10

N_DEV = 8
LANES = 1024
ANY = pl.BlockSpec(memory_space=pl.ANY)


def _dot(a, b, ca, cb, prec):
    dn = (((ca,), (cb,)), ((), ()))
    if prec is not None:
        return lax.dot_general(a.astype(F32), b.astype(F32), dn, precision=prec, preferred_element_type=F32)
    return lax.dot_general(a.astype(MMD), b.astype(MMD), dn, preferred_element_type=F32)


def dnn(a, b, prec=None):
    return _dot(a, b, 1, 0, prec)


def dnt(a, b, prec=None):
    return _dot(a, b, 1, 1, prec)


def dtn(a, b, prec=None):
    return _dot(a, b, 0, 0, prec)


def _sigmoid(x):
    return jax.nn.sigmoid(x)


def _silu(x):
    return x * _sigmoid(x)


def _dsilu(x):
    s = _sigmoid(x)
    return s * (1.0 + x * (1.0 - s))


def _softplus(x):
    return jnp.maximum(x, 0.0) + jnp.log1p(jnp.exp(-jnp.abs(x)))


def _iota(shape, dim):
    return lax.broadcasted_iota(jnp.int32, shape, dim)


def _rsum(x):
    return jnp.sum(x, axis=1, keepdims=True)


def _csum(x):
    return jnp.sum(x, axis=0, keepdims=True)


def _pick(n, cands):
    for c in cands:
        if n % c == 0:
            return c
    return n


def _params(sem):
    return pltpu.CompilerParams(dimension_semantics=sem, vmem_limit_bytes=VMEM_LIMIT_BYTES)


MM_TILE_SIZES = (4096, 2816, 2048, 1536, 1408, 1024, 768, 512, 384, 256, 128)
MM_VMEM_BUDGET = 36 * 1024 * 1024
HBM_BYTES_PER_US = 3.0e6
GRID_STEP_US = 0.35


def _mm_tiles(m, n, kc, a_bytes, b_bytes, o_bytes):
    def sizes(d):
        return [d] if d <= 512 else [t for t in MM_TILE_SIZES if d % t == 0]

    best = None
    for tm in sizes(m):
        for tn in sizes(n):
            for tk in sizes(kc):
                vmem = 2 * (tm * tk * a_bytes + tk * tn * b_bytes + tm * tn * o_bytes) + tm * tn * 4
                if vmem > MM_VMEM_BUDGET:
                    continue
                steps = (m // tm) * (n // tn) * (kc // tk)
                traffic = m * kc * a_bytes * (n // tn) + kc * n * b_bytes * (m // tm) + m * n * o_bytes
                edge = tm * tk * a_bytes + tk * tn * b_bytes + tm * tn * o_bytes
                cost = (traffic + edge) / HBM_BYTES_PER_US + steps * GRID_STEP_US
                if best is None or cost < best[0]:
                    best = (cost, tm, tn, tk)
    return best[1:]


def mm(name, a, b, *, ta=False, tb=False, out_dtype=F32, add=None, scale=None, deps=()):
    adds = [] if add is None else (list(add) if isinstance(add, (list, tuple)) else [(1.0, add)])
    if ta:
        kc, m = a.shape
    else:
        m, kc = a.shape
    if tb:
        n, kb = b.shape
    else:
        kb, n = b.shape
    assert kc == kb, (name, a.shape, b.shape)
    tm, tn, tk = _mm_tiles(m, n, kc, a.dtype.itemsize, b.dtype.itemsize,
                           jnp.dtype(out_dtype).itemsize * (1 + len(adds)))
    nk = kc // tk
    grid = (m // tm, n // tn, nk)
    a_spec = pl.BlockSpec((tk, tm), lambda i, j, k: (k, i)) if ta else pl.BlockSpec((tm, tk), lambda i, j, k: (i, k))
    b_spec = pl.BlockSpec((tn, tk), lambda i, j, k: (j, k)) if tb else pl.BlockSpec((tk, tn), lambda i, j, k: (k, j))
    o_spec = pl.BlockSpec((tm, tn), lambda i, j, k: (i, j))
    ca, cb = (0 if ta else 1), (1 if tb else 0)

    def body(*refs):
        a_ref, b_ref, o_ref, acc_ref = refs[0], refs[1], refs[-2], refs[-1]
        k = pl.program_id(2)

        @pl.when(k == 0)
        def _():
            acc_ref[...] = jnp.zeros_like(acc_ref)

        acc_ref[...] += _dot(a_ref[...], b_ref[...], ca, cb, None)

        @pl.when(k == nk - 1)
        def _():
            r = acc_ref[...]
            if scale is not None:
                r = r * scale
            for (coef, _), add_ref in zip(adds, refs[2:2 + len(adds)]):
                r = r + (add_ref[...] if coef == 1.0 else coef * add_ref[...])
            o_ref[...] = r.astype(o_ref.dtype)

    ins = [a, b] + [t for _, t in adds] + list(deps)
    specs = [a_spec, b_spec] + [o_spec] * len(adds) + [ANY] * len(deps)
    return pl.pallas_call(
        body, name=name, grid=grid, in_specs=specs, out_specs=o_spec,
        out_shape=jax.ShapeDtypeStruct((m, n), out_dtype),
        scratch_shapes=[pltpu.VMEM((tm, tn), F32)],
        compiler_params=_params(("parallel", "parallel", "arbitrary")),
    )(*ins)


class _Ctx:
    def __init__(self, i, nblk, tl):
        self.i, self.nblk, self.tl = i, nblk, tl


def _norm_item(it):
    if isinstance(it, tuple):
        a, w, j = it[:3]
        rows = it[3] if len(it) > 3 else None
        return a, w, j, rows
    return it, it.shape[-1], 0, None


def rowwise(name, fn, length, tl, *, rows=(), consts=(), prevs=(), nexts=(), out_rows=(), out_accs=(), deps=()):
    nblk = length // tl
    hb = tl // HALO
    nhalo = length // HALO
    arrays, specs = [], []
    for it in rows:
        a, w, j, r = _norm_item(it)
        if a.ndim == 3:
            specs.append(pl.BlockSpec((a.shape[0], tl, w), lambda i, j=j: (0, i, j)))
        else:
            specs.append(pl.BlockSpec((r or tl, w), lambda i, j=j: (i, j)))
        arrays.append(a)
    for a in consts:
        specs.append(pl.BlockSpec(a.shape, lambda i, nd=a.ndim: (0,) * nd))
        arrays.append(a)
    for it in prevs:
        a, w, j, _ = _norm_item(it)
        specs.append(pl.BlockSpec((HALO, w), lambda i, j=j: (jnp.maximum(i * hb - 1, 0), j)))
        arrays.append(a)
    for it in nexts:
        a, w, j, _ = _norm_item(it)
        specs.append(pl.BlockSpec((HALO, w), lambda i, j=j: (jnp.minimum((i + 1) * hb, nhalo - 1), j)))
        arrays.append(a)
    out_shape, out_specs = [], []
    for spec in out_rows:
        if len(spec) == 3:
            h, w, dt = spec
            out_shape.append(jax.ShapeDtypeStruct((h, length, w), dt))
            out_specs.append(pl.BlockSpec((h, tl, w), lambda i: (0, i, 0)))
        else:
            w, dt = spec
            out_shape.append(jax.ShapeDtypeStruct((length, w), dt))
            out_specs.append(pl.BlockSpec((tl, w), lambda i: (i, 0)))
    for shape, dt in out_accs:
        out_shape.append(jax.ShapeDtypeStruct(shape, dt))
        out_specs.append(pl.BlockSpec(shape, lambda i, nd=len(shape): (0,) * nd))
    n_r, n_c, n_p, n_n = len(rows), len(consts), len(prevs), len(nexts)
    n_in = n_r + n_c + n_p + n_n
    n_or = len(out_rows)
    arrays, specs = arrays + list(deps), specs + [ANY] * len(deps)

    def body(*refs):
        i = pl.program_id(0)
        vals = [r[...] for r in refs[:n_in]]
        outs = refs[n_in + len(deps):]
        ctx = _Ctx(i, nblk, tl)
        ro, ao = fn(ctx, vals[:n_r], vals[n_r:n_r + n_c], vals[n_r + n_c:n_r + n_c + n_p], vals[n_r + n_c + n_p:])
        for r, v in zip(outs[:n_or], ro, strict=True):
            r[...] = v.astype(r.dtype)
        for r, v in zip(outs[n_or:], ao, strict=True):
            @pl.when(i == 0)
            def _(r=r, v=v):
                r[...] = v.astype(r.dtype)

            @pl.when(i > 0)
            def _(r=r, v=v):
                r[...] += v.astype(r.dtype)

    res = pl.pallas_call(
        body, name=name, grid=(nblk,), in_specs=specs, out_specs=out_specs, out_shape=out_shape,
        compiler_params=_params(("arbitrary",) if out_accs else ("parallel",)),
    )(*arrays)
    return res


def _heads(x, n, w):
    return [x[:, h * w:(h + 1) * w] for h in range(n)]


def _cat(xs):
    return jnp.concatenate(xs, axis=1)


def _row_index(ctx, nrows, offset=0):
    return ctx.i * ctx.tl + offset + _iota((nrows, 1), 0)


def _ln_stats(r):
    mu = jnp.mean(r, axis=1, keepdims=True)
    d = r - mu
    var = jnp.mean(d * d, axis=1, keepdims=True)
    rstd = lax.rsqrt(var + LN_EPS)
    return d * rstd, rstd


def ln_fwd(name, terms, g, b, tl=256, deps=()):
    coefs = [c for c, _ in terms]
    length = terms[0][1].shape[0]

    def fn(ctx, rows, consts, prevs, nexts):
        r = sum(c * t for c, t in zip(coefs, rows))
        xh, _ = _ln_stats(r)
        return [xh * consts[0] + consts[1], r], []

    return rowwise(name, fn, length, min(tl, length), rows=[t for _, t in terms], consts=[g, b],
                   out_rows=[(D_MODEL, F32), (D_MODEL, F32)], deps=deps)


def ln_bwd(name, r, terms, g, tl=256, deps=()):
    coefs = [c for c, _ in terms]
    length = r.shape[0]

    def fn(ctx, rows, consts, prevs, nexts):
        xh, rstd = _ln_stats(rows[0])
        dy = sum(c * t for c, t in zip(coefs, rows[1:]))
        dxh = dy * consts[0]
        dr = rstd * (dxh - jnp.mean(dxh, axis=1, keepdims=True) - xh * jnp.mean(dxh * xh, axis=1, keepdims=True))
        return [dr], [_csum(dy * xh), _csum(dy)]

    return rowwise(name, fn, length, min(tl, length), rows=[r] + [t for _, t in terms], consts=[g],
                   out_rows=[(D_MODEL, F32)], out_accs=[((1, D_MODEL), F32), ((1, D_MODEL), F32)], deps=deps)


def ln_loss(name, terms, g, b, target, tl=256):
    coefs = [c for c, _ in terms]
    length = target.shape[0]
    nt = len(terms)

    def fn(ctx, rows, consts, prevs, nexts):
        r = sum(c * t for c, t in zip(coefs, rows[:nt]))
        xh, _ = _ln_stats(r)
        err = xh * consts[0] + consts[1] - rows[nt]
        tot = _csum(_rsum(err * err)) * (0.5 / D_MODEL)
        return [err * (1.0 / D_MODEL), r], [jnp.broadcast_to(tot, (1, 128))]

    return rowwise(name, fn, length, min(tl, length), rows=[t for _, t in terms] + [target], consts=[g, b],
                   out_rows=[(D_MODEL, F32), (D_MODEL, F32)], out_accs=[((1, 128), F32)])


def _ffn_blocks(length):
    return min(512, length), D_FF // 2


def ffn_gate_up_act(name, x, wg, wu, deps=()):
    length = x.shape[0]
    tm, tn = _ffn_blocks(length)

    def body(x_ref, wg_ref, wu_ref, *rest):
        hg_ref, hu_ref, act_ref = rest[-3:]
        xb = x_ref[...].astype(MMD)
        hg = dnt(xb, wg_ref[...])
        hu = dnt(xb, wu_ref[...])
        hg_ref[...] = hg
        hu_ref[...] = hu
        act_ref[...] = (_silu(hg) * hu).astype(act_ref.dtype)

    row = pl.BlockSpec((tm, D_MODEL), lambda i, j: (i, 0))
    wsp = pl.BlockSpec((tn, D_MODEL), lambda i, j: (j, 0))
    osp = pl.BlockSpec((tm, tn), lambda i, j: (i, j))
    return pl.pallas_call(
        body, name=name, grid=(length // tm, D_FF // tn), in_specs=[row, wsp, wsp] + [ANY] * len(deps),
        out_specs=[osp] * 3,
        out_shape=[jax.ShapeDtypeStruct((length, D_FF), F32)] * 2 + [jax.ShapeDtypeStruct((length, D_FF), BF16)],
        compiler_params=_params(("parallel", "parallel")),
    )(x, wg, wu, *deps)


def ffn_dact(name, dr, wd, hg, hu, deps=()):
    length = dr.shape[0]
    tm, tn = _ffn_blocks(length)

    def body(dr_ref, wd_ref, hg_ref, hu_ref, *rest):
        dhg_ref, dhu_ref = rest[-2:]
        da = 0.5 * dnt(dr_ref[...], wd_ref[...])
        g = hg_ref[...]
        s = _sigmoid(g)
        dhg_ref[...] = (da * hu_ref[...] * (s * (1.0 + g * (1.0 - s)))).astype(dhg_ref.dtype)
        dhu_ref[...] = (da * (g * s)).astype(dhu_ref.dtype)

    row = pl.BlockSpec((tm, D_MODEL), lambda i, j: (i, 0))
    wsp = pl.BlockSpec((tn, D_MODEL), lambda i, j: (j, 0))
    osp = pl.BlockSpec((tm, tn), lambda i, j: (i, j))
    return pl.pallas_call(
        body, name=name, grid=(length // tm, D_FF // tn), in_specs=[row, wsp, osp, osp] + [ANY] * len(deps),
        out_specs=[osp] * 2, out_shape=[jax.ShapeDtypeStruct((length, D_FF), BF16)] * 2,
        compiler_params=_params(("parallel", "parallel")),
    )(dr, wd, hg, hu, *deps)


def ffn_fwd(tag, x, wg, wu, wd, deps=()):
    hg, hu, act = ffn_gate_up_act(tag + "_gate_up", x, wg, wu, deps)
    if callable(wd):
        wd = wd(act)
    f = mm(tag + "_down", act, wd)
    return f, (hg, hu, act), wd


def ffn_bwd(tag, x, res, dr, wg, wu, wd, deps=(), on_dwd=None, on_dwgu=None, also=None):
    hg, hu, act = res
    dwd = mm(tag + "_dwd", act, dr, ta=True, scale=0.5, deps=deps)
    dhg, dhu = ffn_dact(tag + "_dact", dr, wd, hg, hu, deps=on_dwd(dwd) if on_dwd else ())
    dwg = mm(tag + "_dwg", dhg, x, ta=True)
    dwu = mm(tag + "_dwu", dhu, x, ta=True)
    dx = mm(tag + "_dxg", dhg, wg, deps=on_dwgu(dwg, dwu) if on_dwgu else ())
    dx = mm(tag + "_dxu", dhu, wu, add=[(1.0, dx)] + ([also] if also else []))
    return dx, dwg, dwu, dwd


def _conv_taps(ext, taps, n):
    out = taps[3] * ext
    for j in range(3):
        out = out + taps[j] * pltpu.roll(ext, 3 - j, 0)
    return out


def _l2n(x):
    r = lax.rsqrt(_rsum(x * x) + L2_EPS)
    return x * r, r


def conv_fwd(name, pre, taps, tl=256):
    length = pre.shape[0]
    tl = min(tl, length)

    def fn(ctx, rows, consts, prevs, nexts):
        prev = jnp.where(ctx.i > 0, prevs[0], 0.0)
        ext = jnp.concatenate([prev, rows[0]], axis=0)
        s = _silu(_conv_taps(ext, consts, tl + HALO)[HALO:])
        q = _cat([_l2n(x)[0] * (HD ** -0.5) for x in _heads(s[:, :DN_WIDTH], DN_HEADS, HD)])
        k = _cat([_l2n(x)[0] for x in _heads(s[:, DN_WIDTH:2 * DN_WIDTH], DN_HEADS, HD)])
        return [q, k, s[:, 2 * DN_WIDTH:]], []

    return rowwise(name, fn, length, tl, rows=[pre], consts=list(taps), prevs=[pre],
                   out_rows=[(DN_WIDTH, F32)] * 3)


def conv_bwd(name, pre, dq, dk, dv, taps, tl=256):
    length = pre.shape[0]
    tl = min(tl, length)
    n = tl + 2 * HALO

    def fn(ctx, rows, consts, prevs, nexts):
        last = ctx.i == ctx.nblk - 1
        prev = jnp.where(ctx.i > 0, prevs[0], 0.0)
        ext = jnp.concatenate([prev, rows[0], nexts[0]], axis=0)
        c = _conv_taps(ext, consts, n)
        sg = _sigmoid(c)
        s = c * sg
        zero = jnp.zeros((HALO, DN_WIDTH), F32)
        dqe, dke, dve = [jnp.concatenate([zero, rows[1 + t], jnp.where(last, 0.0, nexts[1 + t])], axis=0)
                         for t in range(3)]

        def l2_bwd(x, dy):
            y, r = _l2n(x)
            return r * (dy - y * _rsum(dy * y))

        dsq = _cat([l2_bwd(x, d * (HD ** -0.5)) for x, d in zip(_heads(s[:, :DN_WIDTH], DN_HEADS, HD),
                                                                 _heads(dqe, DN_HEADS, HD))])
        dsk = _cat([l2_bwd(x, d) for x, d in zip(_heads(s[:, DN_WIDTH:2 * DN_WIDTH], DN_HEADS, HD),
                                                  _heads(dke, DN_HEADS, HD))])
        dc = _cat([dsq, dsk, dve]) * (sg * (1.0 + c * (1.0 - sg)))
        dpre = consts[3] * dc
        for j in range(3):
            dpre = dpre + consts[j] * pltpu.roll(dc, n - (3 - j), 0)
        dc_cur = dc[HALO:HALO + tl]
        dws = [_csum(dc_cur * pltpu.roll(ext, 3 - j, 0)[HALO:HALO + tl]) for j in range(3)]
        dws.append(_csum(dc_cur * ext[HALO:HALO + tl]))
        return [dpre[HALO:HALO + tl]], dws

    return rowwise(name, fn, length, tl, rows=[pre, dq, dk, dv], consts=list(taps), prevs=[pre],
                   nexts=[pre, dq, dk, dv], out_rows=[(3 * DN_WIDTH, BF16)],
                   out_accs=[((1, 3 * DN_WIDTH), F32)] * 4)


def _gate_consts():
    lane = jnp.arange(128)[:, None]
    col = jnp.arange(2 * DN_WIDTH)[None, :]
    sel = ((lane < 2 * DN_HEADS) & (col // HD == lane)).astype(F32)
    pick = ((col.T == lane.T * HD) & (lane.T < 2 * DN_HEADS)).astype(F32)
    return sel, pick


def _gate_math(ab, alog, dtb):
    z = ab + dtb
    g = -jnp.exp(alog) * _softplus(z)
    beta = _sigmoid(ab)
    return z, g, beta


def gates_fwd(name, ab, alog, dtb, sel, tl=256):
    length = ab.shape[0]

    def fn(ctx, rows, consts, prevs, nexts):
        _, g, beta = _gate_math(rows[0], consts[0], consts[1])
        lane = _iota(g.shape, 1)
        small = jnp.where(lane < DN_HEADS, g, jnp.where(lane < 2 * DN_HEADS, beta, 0.0))
        big = dnn(small, consts[2], HI)
        return [big[:, :DN_WIDTH], big[:, DN_WIDTH:]], []

    return rowwise(name, fn, length, min(tl, length), rows=[ab], consts=[alog, dtb, sel],
                   out_rows=[(DN_WIDTH, F32)] * 2)


def gates_bwd(name, ab, dgb, dbb, alog, dtb, pick, tl=256):
    length = ab.shape[0]

    def fn(ctx, rows, consts, prevs, nexts):
        z, g, beta = _gate_math(rows[0], consts[0], consts[1])
        dsmall = dnn(_cat([rows[1], rows[2]]), consts[2], HI)
        lane = _iota(g.shape, 1)
        is_a = lane < DN_HEADS
        da = jnp.where(is_a, dsmall * (-jnp.exp(consts[0])) * _sigmoid(z), 0.0)
        db = jnp.where((lane >= DN_HEADS) & (lane < 2 * DN_HEADS), dsmall * beta * (1.0 - beta), 0.0)
        return [da + db], [_csum(jnp.where(is_a, dsmall * g, 0.0)), _csum(da)]

    return rowwise(name, fn, length, min(tl, length), rows=[ab, dgb, dbb], consts=[alog, dtb, pick],
                   out_rows=[(128, BF16)], out_accs=[((1, 128), F32)] * 2)


CPS = 2


def _chunk_scan_rows(x, suffix=False):
    n = x.shape[0]
    rc = _iota(x.shape, 0) & (CHUNK - 1)
    sh = 1
    while sh < CHUNK:
        if suffix:
            x = x + jnp.where(rc < CHUNK - sh, pltpu.roll(x, n - sh, 0), 0.0)
        else:
            x = x + jnp.where(rc >= sh, pltpu.roll(x, sh, 0), 0.0)
        sh *= 2
    return x


def _tri_inv(a_list, eye, bd):
    def each(f, *ls):
        return [f(*xs) for xs in zip(*ls)]

    dg = [jnp.where(bd, a, 0.0) for a in a_list]
    lo = each(lambda a, d: a - d, a_list, dg)
    n1 = [-d for d in dg]
    n2 = each(lambda n: dnn(n, n, X3), n1)
    n4 = each(lambda n: dnn(n, n, X3), n2)
    td = each(lambda p, s: dnn(eye + p, eye + s, X3), n1, n2)
    n8 = each(lambda n: dnn(n, n, X3), n4)
    td = each(lambda t, n: dnn(t, eye + n, X3), td, n4)
    td = each(lambda t, n: dnn(t, eye + n, X3), td, n8)
    m = each(lambda t, l: dnn(t, l, X3), td, lo)
    m2 = each(lambda x: dnn(x, x, X3), m)
    x = each(lambda p, s: dnn(eye - p, eye + s, X3), m, m2)
    return each(lambda p, t: dnn(p, t, X3), x, td)


def _chunk_common(q, k, v, gcb, bb):
    egb = jnp.exp(gcb)
    gc64 = gcb[:, :CHUNK]
    ii, jj = _iota((CHUNK, CHUNK), 0), _iota((CHUNK, CHUNK), 1)
    incl, strict = ii >= jj, ii > jj
    decay = jnp.exp(jnp.where(incl, gc64 - gc64.T, -jnp.inf))
    kb = k * bb
    vb = v * bb
    kbe = kb * egb
    pq = dnt(jnp.concatenate([kb, q], axis=0), k, X3)
    ekb = jnp.exp(gcb[CHUNK - 1:CHUNK, :] - gcb)
    return dict(egb=egb, decay=decay, kb=kb, vb=vb, kbe=kbe, pm=pq[:CHUNK], qm=pq[CHUNK:], ekb=ekb,
                incl=incl, strict=strict, ii=ii, jj=jj)


def _chunk_head(vals, ci, h):
    return [v[ci * CHUNK:(ci + 1) * CHUNK, h * HD:(h + 1) * HD] for v in vals]


def _assemble(per_chunk):
    return jnp.concatenate([_cat(hs) for hs in per_chunk], axis=0)


def _assemble3(per_chunk):
    return jnp.stack([jnp.concatenate([per_chunk[ci][h] for ci in range(CPS)], axis=0) for h in range(DN_HEADS)])


def delta_prep_fwd(name, q, k, v, gb, bb):
    length = q.shape[0]

    def fn(ctx, rows, consts, prevs, nexts):
        gcb_all = _chunk_scan_rows(rows[3])
        vals = [rows[0], rows[1], rows[2], gcb_all, rows[4]]
        units = [(ci, h) for ci in range(CPS) for h in range(DN_HEADS)]
        ins = [_chunk_head(vals, ci, h) for ci, h in units]
        cs = [_chunk_common(*i) for i in ins]
        eye = (cs[0]["ii"] == cs[0]["jj"]).astype(F32)
        ts = _tri_inv([jnp.where(c["strict"], c["pm"] * c["decay"], 0.0) for c in cs], eye,
                      (cs[0]["ii"] >> 4) == (cs[0]["jj"] >> 4))
        uws = [dnn(t, _cat([c["vb"], c["kbe"]]), X3) for t, c in zip(ts, cs)]

        def grid2(xs):
            return [xs[ci * DN_HEADS:(ci + 1) * DN_HEADS] for ci in range(CPS)]

        return [_assemble(grid2([uw[:, :HD] for uw in uws])), _assemble(grid2([uw[:, HD:] for uw in uws])),
                _assemble(grid2([i[0] * c["egb"] for i, c in zip(ins, cs)])),
                _assemble(grid2([i[1] * c["ekb"] for i, c in zip(ins, cs)])), gcb_all,
                _assemble3(grid2([c["qm"] * c["decay"] for c in cs])), _assemble3(grid2(ts))], []

    return rowwise(name, fn, length, CHUNK * CPS, rows=[q, k, v, gb, bb],
                   out_rows=[(DN_WIDTH, F32)] * 5 + [(DN_HEADS, CHUNK, F32)] * 2)


def delta_prep_bwd(name, q, k, v, gb, bb, t3, du, dw, dqd, dkd, dattn3, dgl):
    length = q.shape[0]

    def fn(ctx, rows, consts, prevs, nexts):
        gcb_all = _chunk_scan_rows(rows[3])
        vals = [rows[0], rows[1], rows[2], gcb_all] + list(rows[4:9])
        t3v, da3v, dglv = rows[9], rows[10], rows[11]
        units = [(ci, h) for ci in range(CPS) for h in range(DN_HEADS)]
        ins = [_chunk_head(vals, ci, h) for ci, h in units]
        cs = [_chunk_common(*i[:5]) for i in ins]
        ts = [t3v[h][ci * CHUNK:(ci + 1) * CHUNK] for ci, h in units]
        dattns = [jnp.where(c["incl"], da3v[h][ci * CHUNK:(ci + 1) * CHUNK], 0.0) for (ci, h), c in zip(units, cs)]
        duws = [_cat([i[5], i[6]]) for i in ins]
        dvks = [dtn(t, d, X3) for t, d in zip(ts, duws)]
        dts = [dnt(d, _cat([c["vb"], c["kbe"]]), X3) for d, c in zip(duws, cs)]
        dts = [dnt(d, t, X3) for d, t in zip(dts, ts)]
        das = [jnp.where(c["strict"], -dtn(t, d, X3), 0.0) for c, t, d in zip(cs, ts, dts)]
        dpqs = [jnp.concatenate([da * c["decay"], dat * c["decay"]], axis=0) for da, dat, c in zip(das, dattns, cs)]
        dpqks = [dnn(d, i[1], X3) for d, i in zip(dpqs, ins)]
        dkps = [dtn(d, jnp.concatenate([c["kb"], i[0]], axis=0), X3) for d, c, i in zip(dpqs, cs, ins)]
        dqs, dks, dvs, dgcs, dbs = [], [], [], [], []
        for (ci, h), i, c, dvk, da, dattn, dpqk, dkp in zip(units, ins, cs, dvks, das, dattns, dpqks, dkps):
            qh, kh, vh, _, bh, _, _, dqdh, dkdh = i
            dvb, dkbe = dvk[:, :HD], dvk[:, HD:]
            dkb = dpqk[:CHUNK] + dkbe * c["egb"]
            c1 = _rsum(dkbe * c["kb"] + dqdh * qh) * c["egb"]
            c2 = _rsum(dkdh * kh) * c["ekb"]
            e = (da * c["pm"] + dattn * c["qm"]) * c["decay"]
            dgc = c1 - c2 + _rsum(e) - _rsum(e.T)
            dgl_tot = jnp.max(dglv[ci * 8:(ci + 1) * 8, h * HD:(h + 1) * HD], axis=0, keepdims=True) + _csum(c2)
            dgcs.append(dgc + jnp.where(_iota((CHUNK, HD), 0) == CHUNK - 1, dgl_tot, 0.0))
            dqs.append(dpqk[CHUNK:] + dqdh * c["egb"])
            dks.append(dkp + dkdh * c["ekb"] + dkb * bh)
            dvs.append(dvb * bh)
            dbs.append(jnp.broadcast_to(_rsum(dkb * kh + dvb * vh), (CHUNK, HD)))

        def grid2(xs):
            return [xs[ci * DN_HEADS:(ci + 1) * DN_HEADS] for ci in range(CPS)]

        return [_assemble(grid2(dqs)), _assemble(grid2(dks)), _assemble(grid2(dvs)),
                _chunk_scan_rows(_assemble(grid2(dgcs)), suffix=True), _assemble(grid2(dbs))], []

    return rowwise(name, fn, length, CHUNK * CPS,
                   rows=[q, k, v, gb, bb, du, dw, dqd, dkd, t3, dattn3, (dgl, DN_WIDTH, 0, 8 * CPS)],
                   out_rows=[(DN_WIDTH, F32)] * 5)


def delta_scan_fwd(name, qd, kd, u, w, attn3, gcb):
    length = qd.shape[0]
    n = length // CHUNK
    row = pl.BlockSpec((CHUNK, DN_WIDTH), lambda c: (c, 0))
    sq = pl.BlockSpec((DN_HEADS, CHUNK, CHUNK), lambda c: (0, c, 0))

    def body(qd_ref, kd_ref, u_ref, w_ref, attn_ref, gc_ref, o_ref, vn_ref, st_ref, s_ref):
        c = pl.program_id(0)

        @pl.when(c == 0)
        def _():
            s_ref[...] = jnp.zeros_like(s_ref)

        heads = range(DN_HEADS)
        sls = [pl.ds(h * HD, HD) for h in heads]
        ss = [s_ref[h] for h in heads]
        ws = [dnn(w_ref[:, sl], s) for sl, s in zip(sls, ss)]
        qs = [dnn(qd_ref[:, sl], s) for sl, s in zip(sls, ss)]
        vns = [u_ref[:, sl] - x for sl, x in zip(sls, ws)]
        avs = [dnn(attn_ref[h], vn) for h, vn in zip(heads, vns)]
        kvs = [dtn(kd_ref[:, sl], vn) for sl, vn in zip(sls, vns)]
        for h, sl in zip(heads, sls):
            st_ref[0, h] = ss[h]
            o_ref[:, sl] = qs[h] + avs[h]
            vn_ref[:, sl] = vns[h]
            s_ref[h] = ss[h] * jnp.exp(gc_ref[pl.ds(CHUNK - 1, 1), sl]) + kvs[h]

    return pl.pallas_call(
        body, name=name, grid=(n,), in_specs=[row, row, row, row, sq, row],
        out_specs=[row, row, pl.BlockSpec((1, DN_HEADS, HD, HD), lambda c: (c, 0, 0, 0))],
        out_shape=[jax.ShapeDtypeStruct((length, DN_WIDTH), F32), jax.ShapeDtypeStruct((length, DN_WIDTH), F32),
                   jax.ShapeDtypeStruct((n, DN_HEADS, HD, HD), F32)],
        scratch_shapes=[pltpu.VMEM((DN_HEADS, HD, HD), F32)],
        compiler_params=_params(("arbitrary",)),
    )(qd, kd, u, w, attn3, gcb)


def delta_scan_bwd(name, do, qd, kd, w, attn3, vn, st, gcb):
    length = qd.shape[0]
    n = length // CHUNK
    row = pl.BlockSpec((CHUNK, DN_WIDTH), lambda c: (n - 1 - c, 0))
    sq = pl.BlockSpec((DN_HEADS, CHUNK, CHUNK), lambda c: (0, n - 1 - c, 0))
    stb = pl.BlockSpec((1, DN_HEADS, HD, HD), lambda c: (n - 1 - c, 0, 0, 0))
    glb = pl.BlockSpec((8, DN_WIDTH), lambda c: (n - 1 - c, 0))

    def body(do_ref, qd_ref, kd_ref, w_ref, attn_ref, vn_ref, st_ref, gc_ref,
             dqd_ref, dkd_ref, du_ref, dw_ref, dattn_ref, dgl_ref, ds_ref):
        c = pl.program_id(0)

        @pl.when(c == 0)
        def _():
            ds_ref[...] = jnp.zeros_like(ds_ref)

        heads = range(DN_HEADS)
        sls = [pl.ds(h * HD, HD) for h in heads]
        ss = [st_ref[0, h] for h in heads]
        dsns = [ds_ref[h] for h in heads]
        dos = [do_ref[:, sl] for sl in sls]
        vns = [vn_ref[:, sl] for sl in sls]
        dvns = [dtn(attn_ref[h], d) for h, d in zip(heads, dos)]
        dvns = [x + dnn(kd_ref[:, sl], dsn) for x, sl, dsn in zip(dvns, sls, dsns)]
        qdos = [dtn(qd_ref[:, sl], d) for sl, d in zip(sls, dos)]
        for h, sl in zip(heads, sls):
            dattn_ref[h] = dnt(dos[h], vns[h])
            dqd_ref[:, sl] = dnt(dos[h], ss[h])
            dkd_ref[:, sl] = dnt(vns[h], dsns[h])
            du_ref[:, sl] = dvns[h]
        dws = [dnt(dvn, s) for dvn, s in zip(dvns, ss)]
        wdvs = [dtn(w_ref[:, sl], dvn) for sl, dvn in zip(sls, dvns)]
        for h, sl in zip(heads, sls):
            egl = jnp.exp(gc_ref[pl.ds(CHUNK - 1, 1), sl])
            dw_ref[:, sl] = -dws[h]
            dgl_ref[:, sl] = jnp.broadcast_to(_csum(_rsum(dsns[h] * ss[h])) * egl, (8, HD))
            ds_ref[h] = dsns[h] * egl + qdos[h] - wdvs[h]

    return pl.pallas_call(
        body, name=name, grid=(n,), in_specs=[row, row, row, row, sq, row, stb, row],
        out_specs=[row, row, row, row, sq, glb],
        out_shape=[jax.ShapeDtypeStruct((length, DN_WIDTH), F32)] * 4
        + [jax.ShapeDtypeStruct((DN_HEADS, length, CHUNK), F32), jax.ShapeDtypeStruct((n * 8, DN_WIDTH), F32)],
        scratch_shapes=[pltpu.VMEM((DN_HEADS, HD, HD), F32)],
        compiler_params=_params(("arbitrary",)),
    )(do, qd, kd, w, attn3, vn, st, gcb)


def onorm_fwd(name, o, z, nw, tl=256):
    length = o.shape[0]

    def fn(ctx, rows, consts, prevs, nexts):
        outs = []
        for oh, zh in zip(_heads(rows[0], DN_HEADS, HD), _heads(rows[1], DN_HEADS, HD)):
            r = lax.rsqrt(jnp.mean(oh * oh, axis=1, keepdims=True) + RMS_EPS)
            outs.append(oh * r * consts[0] * _silu(zh))
        return [_cat(outs)], []

    return rowwise(name, fn, length, min(tl, length), rows=[o, z], consts=[nw], out_rows=[(DN_WIDTH, BF16)])[0]


def onorm_bwd(name, o, z, d_on, nw, tl=256):
    length = o.shape[0]

    def fn(ctx, rows, consts, prevs, nexts):
        dos, dzs = [], []
        dnw = jnp.zeros((1, HD), F32)
        for oh, zh, dh in zip(*[_heads(r, DN_HEADS, HD) for r in rows]):
            r = lax.rsqrt(jnp.mean(oh * oh, axis=1, keepdims=True) + RMS_EPS)
            y = oh * r
            sz = _silu(zh)
            t = dh * sz * consts[0]
            dos.append(r * (t - y * jnp.mean(t * y, axis=1, keepdims=True)))
            dzs.append(dh * y * consts[0] * _dsilu(zh))
            dnw = dnw + _csum(dh * y * sz)
        return [_cat(dos), _cat(dzs)], [dnw]

    return rowwise(name, fn, length, min(tl, length), rows=[o, z, d_on], consts=[nw],
                   out_rows=[(DN_WIDTH, F32), (DN_WIDTH, BF16)], out_accs=[((1, HD), F32)])


def merge_fwd(name, gates, ydn, ypool, tl=256):
    length = ydn.shape[0]

    def fn(ctx, rows, consts, prevs, nexts):
        gt = rows[0]
        return [_sigmoid(gt[:, :D_MODEL]) * rows[1] + _sigmoid(gt[:, D_MODEL:]) * rows[2]], []

    return rowwise(name, fn, length, min(tl, length), rows=[gates, ydn, ypool], out_rows=[(D_MODEL, BF16)])[0]


def merge_bwd(name, gates, ydn, ypool, dm, tl=256):
    length = ydn.shape[0]

    def fn(ctx, rows, consts, prevs, nexts):
        gt, yd, yp, d = rows
        sd, sp = _sigmoid(gt[:, :D_MODEL]), _sigmoid(gt[:, D_MODEL:])
        dgates = _cat([d * yd * sd * (1.0 - sd), d * yp * sp * (1.0 - sp)])
        return [d * sd, d * sp, dgates], []

    return rowwise(name, fn, length, min(tl, length), rows=[gates, ydn, ypool, dm],
                   out_rows=[(D_MODEL, BF16), (D_MODEL, BF16), (2 * D_MODEL, BF16)])


def _trailing_sums(ext, upto):
    s, sh = ext, 1
    while sh < upto:
        s = s + pltpu.roll(s, sh, 0)
        sh *= 2
    return s


def _leading_sums(ext, upto, n):
    s, sh = ext, 1
    while sh < upto:
        s = s + pltpu.roll(s, n - sh, 0)
        sh *= 2
    return s


def _pool_mixed(ctx, p, prev, tl):
    prevm = jnp.where(ctx.i > 0, prev, 0.0)
    t1 = (_row_index(ctx, tl) + 1).astype(F32)
    outs = []
    for gi, win in enumerate(POOL_WINDOWS):
        sl = slice(gi * HD, (gi + 1) * HD)
        ext = jnp.concatenate([prevm[:, sl], p[:, sl]], axis=0)
        mean = _trailing_sums(ext, win)[HALO:] / jnp.minimum(t1, float(win))
        outs.append(mean - p[:, sl])
    return outs


def pool_fwd(name, p, pool_w, scale, tl=256):
    length = p.shape[0]
    tl = min(tl, length)

    def fn(ctx, rows, consts, prevs, nexts):
        mixed = _pool_mixed(ctx, rows[0], prevs[0], tl)
        y = _cat([dnn(m, consts[0][gi]) for gi, m in enumerate(mixed)])
        return [y * consts[1]], []

    return rowwise(name, fn, length, tl, rows=[p], consts=[pool_w, scale], prevs=[p],
                   out_rows=[(POOL_WIDTH, BF16)])[0]


def pool_bwd(name, p, dpo, pool_w, scale, tl=256):
    length = p.shape[0]
    tl = min(tl, length)
    n = tl + HALO

    def fn(ctx, rows, consts, prevs, nexts):
        last = ctx.i == ctx.nblk - 1
        mixed = _pool_mixed(ctx, rows[0], prevs[0], tl)
        dext = jnp.concatenate([rows[1], jnp.where(last, 0.0, nexts[0])], axis=0)
        t1 = (_row_index(ctx, n) + 1).astype(F32)
        dps, dws, dscs = [], [], []
        for gi, win in enumerate(POOL_WINDOWS):
            sl = slice(gi * HD, (gi + 1) * HD)
            wg = consts[0][gi]
            dyraw = dext[:, sl] * consts[1][:, sl]
            dmix = dnt(dyraw, wg)
            dws.append(dtn(mixed[gi], dyraw[:tl]))
            dscs.append(_csum(rows[1][:, sl] * dnn(mixed[gi], wg)))
            lead = _leading_sums(dmix / jnp.minimum(t1, float(win)), win, n)
            dps.append(lead[:tl] - dmix[:tl])
        return [_cat(dps)], [jnp.stack(dws), _cat(dscs)]

    return rowwise(name, fn, length, tl, rows=[p, dpo], consts=[pool_w, scale], prevs=[p], nexts=[dpo],
                   out_rows=[(POOL_WIDTH, BF16)],
                   out_accs=[((len(POOL_WINDOWS), HD, HD), F32), ((1, POOL_WIDTH), F32)])


def _xa_probs(qh, kh):
    s = dnt(qh, kh) * (XA_HD ** -0.5)
    e = jnp.exp(s - jnp.max(s, axis=1, keepdims=True))
    return e / _rsum(e)


def xattn_fwd(name, qx, kx, vx, tl=256):
    length = qx.shape[0]

    def fn(ctx, rows, consts, prevs, nexts):
        outs = [dnn(_xa_probs(qh, kh), vh) for qh, kh, vh in
                zip(_heads(rows[0], XA_HEADS, XA_HD), _heads(consts[0], XA_HEADS, XA_HD),
                    _heads(consts[1], XA_HEADS, XA_HD))]
        return [_cat(outs)], []

    return rowwise(name, fn, length, min(tl, length), rows=[qx], consts=[kx, vx], out_rows=[(D_MODEL, BF16)])[0]


def xattn_bwd(name, qx, dox, kx, vx, tl=256):
    length = qx.shape[0]

    def fn(ctx, rows, consts, prevs, nexts):
        dqs, dks, dvs = [], [], []
        for qh, dh, kh, vh in zip(_heads(rows[0], XA_HEADS, XA_HD), _heads(rows[1], XA_HEADS, XA_HD),
                                  _heads(consts[0], XA_HEADS, XA_HD), _heads(consts[1], XA_HEADS, XA_HD)):
            pr = _xa_probs(qh, kh)
            dpr = dnt(dh, vh)
            ds = pr * (dpr - _rsum(dpr * pr)) * (XA_HD ** -0.5)
            dqs.append(dnn(ds, kh))
            dks.append(dtn(ds, qh))
            dvs.append(dtn(pr, dh))
        return [_cat(dqs)], [_cat(dks), _cat(dvs)]

    return rowwise(name, fn, length, min(tl, length), rows=[qx, dox], consts=[kx, vx],
                   out_rows=[(D_MODEL, BF16)], out_accs=[((N_MEM, D_MODEL), F32)] * 2)


def local_step(x, mem, target, w, io):
    sel, pick = _gate_consts()
    alog = jnp.pad(w["a_log"], ((0, 0), (0, 128 - DN_HEADS)))
    dtb = jnp.pad(w["dt_bias"], ((0, 0), (0, 128 - DN_HEADS)))

    f1, res1, w_down1 = ffn_fwd("ffn1", x, w["ffn1_w_gate"], w["ffn1_w_up"], io.ffn1_down, deps=io.rest_started())
    x1, r1 = ln_fwd("ln1", [(ALPHA, x), (0.5, f1)], w["ln1_g"], w["ln1_b"], deps=io.halfway("mid", f1))
    w = dict(w, ffn1_w_down=w_down1, **io.weights("mid", x1))
    taps = [w["conv_w"][j:j + 1] for j in range(4)]

    pre = mm("in_qkv", x1, w["in_qkv"], tb=True)
    z = mm("in_z", x1, w["in_z"], tb=True)
    gates = mm("in_gates", x1, w["in_gates"], tb=True)
    p = mm("in_p", x1, w["in_p"], tb=True)
    ab = mm("in_ab", x1, w["in_ab"], tb=True)
    q, k, v = conv_fwd("conv", pre, taps)
    gb, bb = gates_fwd("gates", ab, alog, dtb, sel)
    u, wd_, qd, kd, gcb, attn3, t3 = delta_prep_fwd("dprep", q, k, v, gb, bb)
    o, vn, st = delta_scan_fwd("dscan", qd, kd, u, wd_, attn3, gcb)
    on = onorm_fwd("onorm", o, z, w["dn_norm_w"])
    ydn = mm("dn_branch", on, w["w_dn_branch"], tb=True)
    po = pool_fwd("pool", p, w["pool_w"], w["pool_scale"])
    ypool = mm("pool_branch", po, w["w_pool_branch"], tb=True)
    merged = merge_fwd("merge", gates, ydn, ypool)
    mix = mm("mix_out", merged, w["w_mix_out"])
    x2, r2 = ln_fwd("ln2", [(ALPHA, x1), (1.0, mix)], w["ln2_g"], w["ln2_b"])

    m, _ = ln_fwd("ln_mem", [(1.0, mem)], w["mem_ln_g"], w["mem_ln_b"])
    qx = mm("xa_q", x2, w["xa_wq"], deps=io.halfway("ffn2", x2))
    kx = mm("xa_k", m, w["xa_wk"])
    vx = mm("xa_v", m, w["xa_wv"])
    ox = xattn_fwd("xattn", qx, kx, vx)
    xa = mm("xa_o", ox, w["xa_wo"])
    x3, r3 = ln_fwd("ln3", [(ALPHA, x2), (1.0, xa)], w["ln3_g"], w["ln3_b"])
    w = dict(w, **io.weights("ffn2", x3))

    f2, res2, _ = ffn_fwd("ffn2", x3, w["ffn2_w_gate"], w["ffn2_w_up"], w["ffn2_w_down"])
    dy4, r4, loss = ln_loss("ln4_loss", [(ALPHA, x3), (0.5, f2)], w["ln4_g"], w["ln4_b"], target)

    g = {}
    dr4, g["ln4_g"], g["ln4_b"] = ln_bwd("ln4_b", r4, [(1.0, dy4)], w["ln4_g"])
    dx3, g["ffn2_w_gate"], g["ffn2_w_up"], g["ffn2_w_down"] = ffn_bwd(
        "ffn2b", x3, res2, dr4, w["ffn2_w_gate"], w["ffn2_w_up"], w["ffn2_w_down"])
    dep = io.grads_out("ffn2", g)
    dr3, g["ln3_g"], g["ln3_b"] = ln_bwd("ln3_b", r3, [(ALPHA, dr4), (1.0, dx3)], w["ln3_g"], deps=dep)

    dox = mm("xa_do", dr3, w["xa_wo"], tb=True)
    g["xa_wo"] = mm("xa_dwo", ox, dr3, ta=True)
    dqx, dkx, dvx = xattn_bwd("xattn_b", qx, dox, kx, vx)
    g["xa_wq"] = mm("xa_dwq", x2, dqx, ta=True)
    dx2 = mm("xa_dx", dqx, w["xa_wq"], tb=True)
    g["xa_wk"] = mm("xa_dwk", m, dkx, ta=True)
    g["xa_wv"] = mm("xa_dwv", m, dvx, ta=True)
    dmm = mm("xa_dmk", dkx, w["xa_wk"], tb=True, deps=io.grads_out("xa", g))
    dmm = mm("xa_dmv", dvx, w["xa_wv"], tb=True, add=dmm)
    _, g["mem_ln_g"], g["mem_ln_b"] = ln_bwd("ln_mem_b", mem, [(1.0, dmm)], w["mem_ln_g"])
    dr2, g["ln2_g"], g["ln2_b"] = ln_bwd("ln2_b", r2, [(ALPHA, dr3), (1.0, dx2)], w["ln2_g"])
    io.grads_in("ffn2", dr2)

    dmerged = mm("mix_dm", dr2, w["w_mix_out"], tb=True)
    g["w_mix_out"] = mm("mix_dw", merged, dr2, ta=True)
    d_ydn, d_ypool, d_gates = merge_bwd("merge_b", gates, ydn, ypool, dmerged)
    g["w_dn_branch"] = mm("dn_dw", d_ydn, on, ta=True)
    d_on = mm("dn_dx", d_ydn, w["w_dn_branch"])
    g["w_pool_branch"] = mm("pool_dw", d_ypool, po, ta=True)
    d_po = mm("pool_dx", d_ypool, w["w_pool_branch"])
    dp, g["pool_w"], g["pool_scale"] = pool_bwd("pool_b", p, d_po, w["pool_w"], w["pool_scale"])
    d_o, dz, g["dn_norm_w"] = onorm_bwd("onorm_b", o, z, d_on, w["dn_norm_w"])
    dqd, dkd, du, dw_, dattn3, dgl = delta_scan_bwd("dscan_b", d_o, qd, kd, wd_, attn3, vn, st, gcb)
    dq, dk, dv, dgb, dbb = delta_prep_bwd("dprep_b", q, k, v, gb, bb, t3, du, dw_, dqd, dkd, dattn3, dgl)
    dpre, dc0, dc1, dc2, dc3 = conv_bwd("conv_b", pre, dq, dk, dv, taps)
    g["conv_w"] = jnp.concatenate([dc0, dc1, dc2, dc3], axis=0)
    d_ab, dalog, ddtb = gates_bwd("gates_b", ab, dgb, dbb, alog, dtb, pick)
    g["a_log"] = dalog[:, :DN_HEADS]
    g["dt_bias"] = ddtb[:, :DN_HEADS]
    g["in_qkv"] = mm("in_dwqkv", dpre, x1, ta=True)
    g["in_z"] = mm("in_dwz", dz, x1, ta=True)
    g["in_gates"] = mm("in_dwgates", d_gates, x1, ta=True)
    g["in_p"] = mm("in_dwp", dp, x1, ta=True)
    g["in_ab"] = mm("in_dwab", d_ab, x1, ta=True)
    io.grads_in("xa", g["in_ab"])
    dx1 = mm("in_dxqkv", dpre, w["in_qkv"], deps=io.grads_out("mixer", g))
    dx1 = mm("in_dxz", dz, w["in_z"], add=dx1)
    dx1 = mm("in_dxgates", d_gates, w["in_gates"], add=dx1)
    dx1 = mm("in_dxp", dp, w["in_p"], add=dx1)
    dx1 = mm("in_dxab", d_ab, w["in_ab"], add=dx1)
    dr1, g["ln1_g"], g["ln1_b"] = ln_bwd("ln1_b", r1, [(ALPHA, dr2), (1.0, dx1)], w["ln1_g"])

    def on_dwd(dwd):
        return io.small_out(dict(g, loss=loss[0, :1])) + io.grads_out("ffn1_d", dict(ffn1_w_down=dwd))

    def on_dwgu(dwg, dwu):
        return io.grads_out("ffn1_gu", dict(ffn1_w_gate=dwg, ffn1_w_up=dwu))

    grad_x, g["ffn1_w_gate"], g["ffn1_w_up"], g["ffn1_w_down"] = ffn_bwd(
        "ffn1b", x, res1, dr1, w["ffn1_w_gate"], w["ffn1_w_up"], w["ffn1_w_down"], on_dwd=on_dwd, on_dwgu=on_dwgu,
        also=(ALPHA, dr1))
    return loss, grad_x, g


WEIGHT_NAMES = ['ffn1_w_gate', 'ffn1_w_up', 'ffn1_w_down', 'ln1_g', 'ln1_b', 'w_in', 'conv_w', 'a_log', 'dt_bias',
                'dn_norm_w', 'w_dn_branch', 'pool_w', 'pool_scale', 'w_pool_branch', 'w_mix_out', 'ln2_g', 'ln2_b',
                'mem_ln_g', 'mem_ln_b', 'xa_wq', 'xa_wk', 'xa_wv', 'xa_wo', 'ln3_g', 'ln3_b', 'ffn2_w_gate',
                'ffn2_w_up', 'ffn2_w_down', 'ln4_g', 'ln4_b']
SHARDED = [
    ("ffn1_w_gate", "cols", (1024, 352)), ("ffn1_w_up", "cols", (1024, 352)), ("ffn1_w_down", "rows", (352, 1024)),
    ("w_in", "cols", (1024, 577)), ("conv_w", "flat", (4, 192)), ("w_dn_branch", "cols", (512, 128)),
    ("w_pool_branch", "cols", (512, 128)), ("w_mix_out", "rows", (128, 1024)), ("xa_wq", "rows", (128, 1024)),
    ("xa_wk", "rows", (128, 1024)), ("xa_wv", "rows", (128, 1024)), ("xa_wo", "rows", (128, 1024)),
    ("ffn2_w_gate", "cols", (1024, 352)), ("ffn2_w_up", "cols", (1024, 352)), ("ffn2_w_down", "rows", (352, 1024)),
]
REPLICATED = [n for n in WEIGHT_NAMES if n not in {s[0] for s in SHARDED}]
ROW_ALIGN = 16
ROW_BLOCKS = (512, 384, 352, 256, 192, 176, 128)
GROUPS = {"ffn1_gu": ("ffn1_w_gate", "ffn1_w_up"), "ffn1_d": ("ffn1_w_down",),
          "mixer": ("w_in", "conv_w", "w_dn_branch", "w_pool_branch", "w_mix_out"),
          "xa": ("xa_wq", "xa_wk", "xa_wv", "xa_wo"),
          "ffn2": ("ffn2_w_gate", "ffn2_w_up", "ffn2_w_down")}
GROUPS["mid"] = GROUPS["mixer"] + GROUPS["xa"]
W_IN_COLS = 577
W_IN_PIECES = (("in_qkv", 0, 1536), ("in_z", 1536, 2048), ("in_ab", 2048, 2056), ("in_p", 2056, 2568),
               ("in_gates", 2568, 4616))


def _round_up(n, m):
    return -(-n // m) * m


def _layout():
    off, table = 0, {}
    for name, form, shape in SHARDED:
        valid = {"rows": shape[0], "cols": shape[1], "flat": 2}[form]
        width = {"rows": shape[1], "cols": shape[0], "flat": shape[0] * shape[1]}[form]
        rows = _round_up(valid, ROW_ALIGN)
        table[name] = (off, rows, valid, width, form, shape)
        off += rows
    return table


LAYOUT = _layout()


def _group_span(names):
    base = LAYOUT[names[0]][0]
    rows = LAYOUT[names[-1]][0] + LAYOUT[names[-1]][1] - base
    while not any(rows % b == 0 for b in ROW_BLOCKS):
        rows += ROW_ALIGN
    return base, rows


def _row_block(rows):
    return _pick(rows, ROW_BLOCKS)


def _pad_block(blk, rows):
    return jnp.pad(blk, ((0, rows - blk.shape[0]), (0, LANES - blk.shape[1])))


def pack_weight_shards(shards, names):
    parts, used = [], 0
    for name in names:
        off, rows, valid, width, form, _ = LAYOUT[name]
        s = shards[name]
        if form == "flat":
            flat = s.reshape(1, -1)
            hi = flat.astype(BF16)
            blk = jnp.concatenate([hi, (flat - hi.astype(F32)).astype(BF16)], axis=0)
        else:
            blk = (s.T if form == "cols" else s).astype(BF16)
        parts.append(_pad_block(blk, rows))
        used += rows
    if _group_span(names)[1] > used:
        parts.append(jnp.zeros((_group_span(names)[1] - used, LANES), BF16))
    return jnp.concatenate(parts, axis=0)


def _w_in_rows(padded, rows, first, last):
    segs = []
    for k in range(N_DEV):
        lo, hi = max(first, k * W_IN_COLS), min(last, (k + 1) * W_IN_COLS)
        if lo < hi:
            segs.append(padded[k * rows + lo - k * W_IN_COLS:k * rows + hi - k * W_IN_COLS])
    return segs[0] if len(segs) == 1 else jnp.concatenate(segs, axis=0)


def unpack_full_weights(gathered, names):
    out, base = {}, _group_span(names)[0]
    for name in names:
        off, rows, valid, width, form, shape = LAYOUT[name]
        seg = gathered[:, off - base:off - base + rows]
        if form == "flat":
            flat = seg[:, 0, :width].astype(F32) + seg[:, 1, :width].astype(F32)
            out[name] = flat.reshape((N_DEV,) + shape).transpose(1, 0, 2).reshape(shape[0], N_DEV * shape[1])
        elif name == "w_in":
            padded = seg.reshape(N_DEV * rows, LANES)
            for piece, first, last in W_IN_PIECES:
                out[piece] = _w_in_rows(padded, rows, first, last)
        else:
            out[name] = seg[:, :valid, :width].reshape(N_DEV * valid, width)
    return out


def pack_full_grads(grads, names, me):
    wire, own, used = [], [], 0
    for name in names:
        off, rows, valid, width, form, shape = LAYOUT[name]
        if form == "flat":
            full = grads[name].reshape(shape[0], N_DEV, shape[1]).transpose(1, 0, 2).reshape(N_DEV, 1, width)
        elif name == "w_in":
            full = jnp.concatenate([grads[piece][:last - first] for piece, first, last in W_IN_PIECES], axis=0)
            full = full.reshape(N_DEV, valid, width)
        else:
            full = grads[name].reshape(N_DEV, valid, width)
        pad = ((0, rows - full.shape[1]), (0, LANES - width))
        wire.append(jnp.pad(full.astype(WIRE), ((0, 0),) + pad))
        own.append(jnp.pad(lax.dynamic_index_in_dim(full, me, 0, keepdims=False), pad))
        used += rows
    if _group_span(names)[1] > used:
        wire.append(jnp.zeros((N_DEV, _group_span(names)[1] - used, LANES), WIRE))
        own.append(jnp.zeros((_group_span(names)[1] - used, LANES), F32))
    return jnp.concatenate(wire, axis=1), jnp.concatenate(own, axis=0)


def unpack_grad_shards(packed, names):
    out, base = {}, _group_span(names)[0]
    for name in names:
        off, rows, valid, width, form, shape = LAYOUT[name]
        off -= base
        if form == "flat":
            out[name] = packed[off, :width].reshape(shape)
        elif form == "cols":
            out[name] = packed[off:off + valid, :width].T
        else:
            out[name] = packed[off:off + valid, :width]
    return out


SMALL_SHAPES = {n: (1024,) for n in REPLICATED}
SMALL_SHAPES.update(pool_w=(4, 128, 128), pool_scale=(512,), dn_norm_w=(128,), a_log=(4,), dt_bias=(4,))


SMALL_SHAPES["loss"] = (1,)
SMALL_NAMES = REPLICATED + ["loss"]


def _small_layout():
    off, table = 0, {}
    for name in SMALL_NAMES:
        numel = 1
        for d in SMALL_SHAPES[name]:
            numel *= d
        rows = _round_up(-(-numel // LANES), 8)
        table[name] = (off, rows, numel)
        off += rows
    return table, off


SMALL_LAYOUT, SMALL_ROWS = _small_layout()


def _to_rows(flat, rows):
    return jnp.pad(flat, (0, rows * LANES - flat.shape[0])).reshape(rows, LANES)


def pack_small(values):
    return jnp.concatenate([_to_rows(values[name].reshape(-1), SMALL_LAYOUT[name][1]) for name in SMALL_NAMES], axis=0)


def unpack_small(packed):
    out = {}
    for name in SMALL_NAMES:
        off, rows, numel = SMALL_LAYOUT[name]
        out[name] = packed[off:off + rows].reshape(-1)[:numel].reshape(SMALL_SHAPES[name])
    return out


MESH = pl.DeviceIdType.MESH


def _position():
    return lax.axis_index("x"), lax.axis_index("y"), lax.axis_index("c")


def _other_chips(x, y):
    return [(1 - x, y), (x, 1 - y), (1 - x, 1 - y)]


def all_gather(name, block):
    rows, n = block.shape

    def body(x_ref, out_ref, send_sems, recv_sems, local_sem):
        x, y, c = _position()
        me, sibling = (x, y, c), (x, y, 1 - c)
        chips = _other_chips(x, y)

        def slot(px, py, pc):
            return out_ref.at[4 * px + 2 * py + pc]

        def copy(k, blk, to, src=None):
            return pltpu.make_async_remote_copy(
                src_ref=slot(*blk) if src is None else src, dst_ref=slot(*blk),
                send_sem=send_sems.at[k], recv_sem=recv_sems.at[k], device_id=to, device_id_type=MESH)

        mine = pltpu.make_async_copy(x_ref, slot(*me), local_sem)
        mine.start()
        first = [copy(0, me, sibling, src=x_ref)]
        first += [copy(1 + j, me, (*chip, c), src=x_ref) for j, chip in enumerate(chips)]
        for cp in first:
            cp.start()
        passed = [copy(4 + j, (*chip, c), sibling) for j, chip in enumerate(chips)]
        for j, chip in enumerate(chips):
            copy(1 + j, (*chip, c), me).wait_recv()
            passed[j].start()
        copy(0, sibling, me).wait_recv()
        for j, chip in enumerate(chips):
            copy(4 + j, (*chip, 1 - c), me).wait_recv()
        for cp in first + passed:
            cp.wait_send()
        mine.wait()

    return pl.pallas_call(
        body, name=name, out_shape=jax.ShapeDtypeStruct((N_DEV, rows, n), block.dtype),
        in_specs=[ANY], out_specs=ANY,
        scratch_shapes=[pltpu.SemaphoreType.DMA((7,)), pltpu.SemaphoreType.DMA((7,)), pltpu.SemaphoreType.DMA(())],
    )(block)


HBM = pl.BlockSpec(memory_space=pltpu.HBM)
SEM = pl.BlockSpec(memory_space=pltpu.SEMAPHORE)
EFFECT = pltpu.SideEffectType.DATAFLOW_SIDE_EFFECTING


def _remote(src, dst, send_sem, recv_sem, to):
    return pltpu.make_async_remote_copy(src_ref=src, dst_ref=dst, send_sem=send_sem, recv_sem=recv_sem,
                                        device_id=to, device_id_type=MESH)


def split_start(name, bufs, n, make_copies):
    nb = len(bufs)

    def body(*refs):
        for out_cp, _ in make_copies(refs[:nb], refs[nb:nb + n], refs[nb + n:nb + 2 * n]):
            out_cp.start()
        refs[-1][...] = jnp.zeros_like(refs[-1])

    outs = pl.pallas_call(
        body, name=name,
        out_shape=tuple([pltpu.SemaphoreType.DMA(())] * (2 * n)) + tuple(pltpu.HBM(b.shape, b.dtype) for b in bufs)
        + (jax.ShapeDtypeStruct((8, 128), F32),),
        in_specs=[HBM] * nb,
        out_specs=tuple([SEM] * (2 * n) + [HBM] * nb + [pl.BlockSpec(memory_space=pltpu.VMEM)]),
        input_output_aliases={i: 2 * n + i for i in range(nb)},
        compiler_params=pltpu.CompilerParams(has_side_effects=EFFECT),
    )(*[pltpu.with_memory_space_constraint(b, pltpu.HBM) for b in bufs])
    return list(outs[:2 * n]), list(outs[2 * n:2 * n + nb]), outs[-1]


def split_wait(name, bufs, sems, n, make_copies, after):
    nb = len(bufs)

    def body(*refs):
        for out_cp, in_cp in make_copies(refs[:nb], refs[nb:nb + n], refs[nb + n:nb + 2 * n]):
            out_cp.wait_send()
            in_cp.wait_recv()

    outs = pl.pallas_call(
        body, name=name, out_shape=tuple(pltpu.HBM(b.shape, b.dtype) for b in bufs),
        in_specs=[HBM] * nb + [SEM] * (2 * n) + [ANY], out_specs=tuple([HBM] * nb),
        input_output_aliases={i: i for i in range(nb)},
        compiler_params=pltpu.CompilerParams(has_side_effects=EFFECT),
    )(*bufs, *sems, after)
    return list(outs)


def _gather_stage1(refs, send, recv):
    src, land = refs
    x, y, c = _position()
    peers = [(x, y, 1 - c)] + [(*chip, c) for chip in _other_chips(x, y)]
    return [(_remote(src, land.at[4 * x + 2 * y + c], send[k], recv[k], p),
             _remote(src, land.at[4 * p[0] + 2 * p[1] + p[2]], send[k], recv[k], p)) for k, p in enumerate(peers)]


def _gather_stage2(refs, send, recv):
    (land,) = refs
    x, y, c = _position()
    out = []
    for j, (px, py) in enumerate(_other_chips(x, y)):
        mine, theirs = land.at[4 * px + 2 * py + c], land.at[4 * px + 2 * py + 1 - c]
        out.append((_remote(mine, mine, send[j], recv[j], (x, y, 1 - c)),
                    _remote(theirs, theirs, send[j], recv[j], (x, y, 1 - c))))
    return out


def _flips():
    return [(a, b, d) for a in (0, 1) for b in (0, 1) for d in (0, 1) if a | b | d]


def _gather_direct(refs, send, recv):
    src, land = refs
    x, y, c = _position()
    out = []
    for k, (fx, fy, fc) in enumerate(_flips()):
        p = (1 - x if fx else x, 1 - y if fy else y, 1 - c if fc else c)
        out.append((_remote(src, land.at[4 * x + 2 * y + c], send[k], recv[k], p),
                    _remote(src, land.at[4 * p[0] + 2 * p[1] + p[2]], send[k], recv[k], p)))
    return out


def _scatter_direct(refs, send, recv):
    sendbuf, land = refs
    x, y, c = _position()
    me = 4 * x + 2 * y + c
    out = []
    for k, (fx, fy, fc) in enumerate(_flips()):
        p = (1 - x if fx else x, 1 - y if fy else y, 1 - c if fc else c)
        peer = 4 * p[0] + 2 * p[1] + p[2]
        out.append((_remote(sendbuf.at[peer], land.at[me], send[k], recv[k], p),
                    _remote(sendbuf.at[peer], land.at[peer], send[k], recv[k], p)))
    return out


def _own_plus_slots(name, own, landed):
    n, rows, _ = landed.shape
    tr = _row_block(rows)

    def body(g_ref, l_ref, o_ref):
        acc = g_ref[...]
        for j in range(n):
            acc = acc + l_ref[j].astype(F32)
        o_ref[...] = acc

    return pl.pallas_call(
        body, name=name, grid=(rows // tr,),
        in_specs=[pl.BlockSpec((tr, LANES), lambda i: (i, 0)), pl.BlockSpec((n, tr, LANES), lambda i: (0, i, 0))],
        out_specs=pl.BlockSpec((tr, LANES), lambda i: (i, 0)),
        out_shape=jax.ShapeDtypeStruct((rows, LANES), F32), compiler_params=_params(("parallel",)),
    )(own, landed)


def _sum_slots(name, stack):
    n, rows, _ = stack.shape

    def body(s_ref, o_ref):
        acc = s_ref[0]
        for j in range(1, n):
            acc = acc + s_ref[j]
        o_ref[...] = acc

    return pl.pallas_call(
        body, name=name, in_specs=[pl.BlockSpec(stack.shape, lambda: (0, 0, 0))],
        out_specs=pl.BlockSpec((rows, LANES), lambda: (0, 0)), out_shape=jax.ShapeDtypeStruct((rows, LANES), F32),
    )(stack)


def adamw(name, w, g, m, v):
    shape = w.shape
    last = shape[-1]
    w2, g2, m2, v2 = [a.reshape(-1, last) for a in (w, g, m, v)]
    rows = w2.shape[0]
    tr = 256 if rows % 256 == 0 else rows

    def body(w_ref, g_ref, m_ref, v_ref, d_ref, nm_ref, nv_ref):
        gg = g_ref[...]
        nm = ADAM_B1 * m_ref[...] + (1.0 - ADAM_B1) * gg
        nv = ADAM_B2 * v_ref[...] + (1.0 - ADAM_B2) * (gg * gg)
        m_hat = nm / (1.0 - ADAM_B1 ** ADAM_STEP)
        v_hat = nv / (1.0 - ADAM_B2 ** ADAM_STEP)
        d_ref[...] = -ADAM_LR * (m_hat / (jnp.sqrt(v_hat) + ADAM_EPS) + ADAM_WD * w_ref[...])
        nm_ref[...] = nm
        nv_ref[...] = nv

    spec = pl.BlockSpec((tr, last), lambda i: (i, 0))
    outs = pl.pallas_call(
        body, name=name, grid=(rows // tr,), in_specs=[spec] * 4, out_specs=[spec] * 3,
        out_shape=[jax.ShapeDtypeStruct((rows, last), F32)] * 3, compiler_params=_params(("parallel",)),
    )(w2, g2, m2, v2)
    return [o.reshape(shape) for o in outs]


def _landing(block_shape, dtype, own):
    x, y, c = _position()
    return lax.dynamic_update_slice(lax.empty((N_DEV,) + block_shape, dtype), own[None], (4 * x + 2 * y + c, 0, 0))


class _Exchanges:
    def __init__(self, shards):
        self.shards = shards
        self.pending = {}
        self.reduced = {}

    def first_weights(self):
        names = GROUPS["ffn1_gu"]
        return unpack_full_weights(all_gather("ag_ffn1_gu", pack_weight_shards(self.shards, names)), names)

    def rest_started(self):
        tokens = []
        block = pack_weight_shards(self.shards, GROUPS["ffn1_d"])
        sems, bufs, token = split_start("ag_ffn1_d_s", [block, _landing(block.shape, block.dtype, block)], N_DEV - 1,
                                        _gather_direct)
        self.pending["ffn1_d"] = (sems, bufs)
        tokens.append(token)
        for key in ("mid", "ffn2"):
            block = pack_weight_shards(self.shards, GROUPS[key])
            sems, bufs, token = split_start(f"ag_{key}_s1", [block, _landing(block.shape, block.dtype, block)], 4,
                                            _gather_stage1)
            self.pending[key] = (sems, bufs)
            tokens.append(token)
        return tuple(tokens)

    def ffn1_down(self, after):
        sems, bufs = self.pending.pop("ffn1_d")
        _, gathered = split_wait("ag_ffn1_d_w", bufs, sems, N_DEV - 1, _gather_direct, after)
        return unpack_full_weights(gathered, GROUPS["ffn1_d"])["ffn1_w_down"]

    def halfway(self, key, after):
        sems, bufs = self.pending.pop(key)
        _, land = split_wait(f"ag_{key}_w1", bufs, sems, 4, _gather_stage1, after)
        sems, bufs, token = split_start(f"ag_{key}_s2", [land], 3, _gather_stage2)
        self.pending[key] = (sems, bufs)
        return (token,)

    def weights(self, key, after):
        sems, bufs = self.pending.pop(key)
        (gathered,) = split_wait(f"ag_{key}_w2", bufs, sems, 3, _gather_stage2, after)
        w = unpack_full_weights(gathered, GROUPS[key])
        if "in_ab" in w:
            w["in_ab"] = jnp.pad(w["in_ab"], ((0, 128 - 2 * DN_HEADS), (0, 0)))
        return w

    def grads_out(self, key, grads):
        x, y, c = _position()
        wire, own = pack_full_grads(grads, GROUPS[key], 4 * x + 2 * y + c)
        land = _landing(wire.shape[1:], WIRE, jnp.zeros(wire.shape[1:], WIRE))
        sems, bufs, token = split_start(f"rs_{key}_start", [wire, land], N_DEV - 1, _scatter_direct)
        self.pending[key] = (sems, bufs, own)
        return (token,)

    def grads_in(self, key, after):
        sems, bufs, own = self.pending.pop(key)
        _, landed = split_wait(f"rs_{key}_wait", bufs, sems, N_DEV - 1, _scatter_direct, after)
        self.reduced.update(unpack_grad_shards(_own_plus_slots(f"rs_{key}_sum", own, landed), GROUPS[key]))

    def small_out(self, values):
        block = pack_small(values)
        sems, bufs, token = split_start("ag_small_s", [block, _landing(block.shape, block.dtype, block)], N_DEV - 1,
                                        _gather_direct)
        self.pending["small"] = (sems, bufs)
        return (token,)

    def small_in(self, after):
        sems, bufs = self.pending.pop("small")
        _, gathered = split_wait("ag_small_w", bufs, sems, N_DEV - 1, _gather_direct, after)
        return unpack_small(_sum_slots("small_sum", gathered))


def kernel(x, mem, ffn1_w_gate, ffn1_w_up, ffn1_w_down, ln1_g, ln1_b, w_in, conv_w, a_log, dt_bias, dn_norm_w, w_dn_branch, pool_w, pool_scale, w_pool_branch, w_mix_out, ln2_g, ln2_b, mem_ln_g, mem_ln_b, xa_wq, xa_wk, xa_wv, xa_wo, ln3_g, ln3_b, ffn2_w_gate, ffn2_w_up, ffn2_w_down, ln4_g, ln4_b, loss_target, m_ffn1_w_gate, m_ffn1_w_up, m_ffn1_w_down, m_ln1_g, m_ln1_b, m_w_in, m_conv_w, m_a_log, m_dt_bias, m_dn_norm_w, m_w_dn_branch, m_pool_w, m_pool_scale, m_w_pool_branch, m_w_mix_out, m_ln2_g, m_ln2_b, m_mem_ln_g, m_mem_ln_b, m_xa_wq, m_xa_wk, m_xa_wv, m_xa_wo, m_ln3_g, m_ln3_b, m_ffn2_w_gate, m_ffn2_w_up, m_ffn2_w_down, m_ln4_g, m_ln4_b, v_ffn1_w_gate, v_ffn1_w_up, v_ffn1_w_down, v_ln1_g, v_ln1_b, v_w_in, v_conv_w, v_a_log, v_dt_bias, v_dn_norm_w, v_w_dn_branch, v_pool_w, v_pool_scale, v_w_pool_branch, v_w_mix_out, v_ln2_g, v_ln2_b, v_mem_ln_g, v_mem_ln_b, v_xa_wq, v_xa_wk, v_xa_wv, v_xa_wo, v_ln3_g, v_ln3_b, v_ffn2_w_gate, v_ffn2_w_up, v_ffn2_w_down, v_ln4_g, v_ln4_b):
    given = dict(locals())
    shards = {n: given[n] for n in WEIGHT_NAMES}
    io = _Exchanges({n: shards[n][0] for n, _, _ in SHARDED})
    w = io.first_weights()
    for n in REPLICATED:
        w[n] = shards[n][0] if n == "pool_w" else shards[n]
    loss_part, grad_x, g = local_step(x[0], mem[0], loss_target[0], w, io)

    grad, updates = {}, {}

    def update(names, reduced):
        for n in names:
            grad[n] = reduced[n].reshape(shards[n].shape)
            updates[n] = adamw("adamw_" + n, shards[n], grad[n], given["m_" + n], given["v_" + n])
        return updates[names[-1]][0]

    update(GROUPS["ffn2"] + GROUPS["xa"], io.reduced)
    io.grads_in("mixer", grad_x)
    done = update(GROUPS["mixer"], io.reduced)
    small = io.small_in(done)
    loss = small.pop("loss")[0]
    done = update(REPLICATED, small)
    io.grads_in("ffn1_d", done)
    done = update(GROUPS["ffn1_d"], io.reduced)
    io.grads_in("ffn1_gu", done)
    update(GROUPS["ffn1_gu"], io.reduced)
    return (loss, grad_x[None], *[grad[n] for n in WEIGHT_NAMES], *[updates[n][0] for n in WEIGHT_NAMES],
            *[updates[n][1] for n in WEIGHT_NAMES], *[updates[n][2] for n in WEIGHT_NAMES])
```

```python
import functools

import jax
import jax.numpy as jnp
from jax import lax
from jax.experimental import pallas as pl
from jax.experimental.pallas import tpu as pltpu

F32 = jnp.float32
BF16 = jnp.bfloat16
MMD = BF16
WIRE = BF16
HI = lax.Precision.HIGHEST
X3 = lax.Precision.HIGH
VMEM_LIMIT_BYTES = 48 * 1024 * 1024

D_MODEL = 1024
D_FF = 2816
CHUNK = 64
N_MEM = 256
DN_HEADS = 4
HD = 128
DN_WIDTH = 512
POOL_WINDOWS = (2, 4, 8, 16)
POOL_WIDTH = 512
XA_HEADS = 4
XA_HD = 256
LN_EPS = 1e-5
RMS_EPS = 1e-6
L2_EPS = 1e-6
ALPHA = 2.0 ** 0.25
HALO = 16
ROWS = 512
ROWS_WIDE = 256

ADAM_LR = 0.001
ADAM_B1 = 0.9
ADAM_B2 = 0.999
ADAM_EPS = 1e-08
ADAM_WD = 0.01
ADAM_STEP = 10

N_DEV = 8
LANES = 1024
ANY = pl.BlockSpec(memory_space=pl.ANY)


def _dot(a, b, ca, cb, prec):
    dn = (((ca,), (cb,)), ((), ()))
    if prec is not None:
        return lax.dot_general(a.astype(F32), b.astype(F32), dn, precision=prec, preferred_element_type=F32)
    return lax.dot_general(a.astype(MMD), b.astype(MMD), dn, preferred_element_type=F32)


def dnn(a, b, prec=None):
    return _dot(a, b, 1, 0, prec)


def dnt(a, b, prec=None):
    return _dot(a, b, 1, 1, prec)


def dtn(a, b, prec=None):
    return _dot(a, b, 0, 0, prec)


def _sigmoid(x):
    return jax.nn.sigmoid(x)


def _silu(x):
    return x * _sigmoid(x)


def _dsilu(x):
    s = _sigmoid(x)
    return s * (1.0 + x * (1.0 - s))


def _softplus(x):
    return jnp.maximum(x, 0.0) + jnp.log1p(jnp.exp(-jnp.abs(x)))


def _iota(shape, dim):
    return lax.broadcasted_iota(jnp.int32, shape, dim)


def _rsum(x):
    return jnp.sum(x, axis=1, keepdims=True)


def _csum(x):
    return jnp.sum(x, axis=0, keepdims=True)


def _pick(n, cands):
    for c in cands:
        if n % c == 0:
            return c
    return n


def _params(sem):
    return pltpu.CompilerParams(dimension_semantics=sem, vmem_limit_bytes=VMEM_LIMIT_BYTES)


MM_TILE_SIZES = (4096, 2816, 2048, 1536, 1408, 1024, 768, 512, 384, 256, 128)
MM_VMEM_BUDGET = 36 * 1024 * 1024
HBM_BYTES_PER_US = 3.0e6
GRID_STEP_US = 0.35


def _mm_tiles(m, n, kc, a_bytes, b_bytes, o_bytes):
    def sizes(d):
        return [d] if d <= 512 else [t for t in MM_TILE_SIZES if d % t == 0]

    best = None
    for tm in sizes(m):
        for tn in sizes(n):
            for tk in sizes(kc):
                vmem = 2 * (tm * tk * a_bytes + tk * tn * b_bytes + tm * tn * o_bytes) + tm * tn * 4
                if vmem > MM_VMEM_BUDGET:
                    continue
                steps = (m // tm) * (n // tn) * (kc // tk)
                traffic = m * kc * a_bytes * (n // tn) + kc * n * b_bytes * (m // tm) + m * n * o_bytes
                edge = tm * tk * a_bytes + tk * tn * b_bytes + tm * tn * o_bytes
                cost = (traffic + edge) / HBM_BYTES_PER_US + steps * GRID_STEP_US
                if best is None or cost < best[0]:
                    best = (cost, tm, tn, tk)
    return best[1:]


def mm(name, a, b, *, ta=False, tb=False, out_dtype=F32, add=None, scale=None, deps=()):
    adds = [] if add is None else (list(add) if isinstance(add, (list, tuple)) else [(1.0, add)])
    if ta:
        kc, m = a.shape
    else:
        m, kc = a.shape
    if tb:
        n, kb = b.shape
    else:
        kb, n = b.shape
    assert kc == kb, (name, a.shape, b.shape)
    tm, tn, tk = _mm_tiles(m, n, kc, a.dtype.itemsize, b.dtype.itemsize,
                           jnp.dtype(out_dtype).itemsize * (1 + len(adds)))
    nk = kc // tk
    grid = (m // tm, n // tn, nk)
    a_spec = pl.BlockSpec((tk, tm), lambda i, j, k: (k, i)) if ta else pl.BlockSpec((tm, tk), lambda i, j, k: (i, k))
    b_spec = pl.BlockSpec((tn, tk), lambda i, j, k: (j, k)) if tb else pl.BlockSpec((tk, tn), lambda i, j, k: (k, j))
    o_spec = pl.BlockSpec((tm, tn), lambda i, j, k: (i, j))
    ca, cb = (0 if ta else 1), (1 if tb else 0)

    def body(*refs):
        a_ref, b_ref = refs[0], refs[1]
        o_ref = refs[-1] if nk == 1 else refs[-2]
        k = pl.program_id(2)
        part = _dot(a_ref[...], b_ref[...], ca, cb, None)

        def finish(r):
            if scale is not None:
                r = r * scale
            for (coef, _), add_ref in zip(adds, refs[2:2 + len(adds)]):
                r = r + (add_ref[...] if coef == 1.0 else coef * add_ref[...])
            o_ref[...] = r.astype(o_ref.dtype)

        if nk == 1:
            finish(part)
            return
        acc_ref = refs[-1]

        @pl.when(k == 0)
        def _():
            acc_ref[...] = part

        if nk > 2:
            @pl.when((k > 0) & (k < nk - 1))
            def _():
                acc_ref[...] += part

        @pl.when(k == nk - 1)
        def _():
            finish(acc_ref[...] + part)

    ins = [a, b] + [t for _, t in adds] + list(deps)
    specs = [a_spec, b_spec] + [o_spec] * len(adds) + [ANY] * len(deps)
    return pl.pallas_call(
        body, name=name, grid=grid, in_specs=specs, out_specs=o_spec,
        out_shape=jax.ShapeDtypeStruct((m, n), out_dtype),
        scratch_shapes=[pltpu.VMEM((tm, tn), F32)] if nk > 1 else [],
        compiler_params=_params(("parallel", "parallel", "arbitrary")),
    )(*ins)


class _Ctx:
    def __init__(self, i, nblk, tl):
        self.i, self.nblk, self.tl = i, nblk, tl


def _norm_item(it):
    if isinstance(it, tuple):
        a, w, j = it[:3]
        rows = it[3] if len(it) > 3 else None
        return a, w, j, rows
    return it, it.shape[-1], 0, None


def rowwise(name, fn, length, tl, *, rows=(), consts=(), prevs=(), nexts=(), out_rows=(), out_accs=(), deps=()):
    nblk = length // tl
    hb = tl // HALO
    nhalo = length // HALO
    arrays, specs = [], []
    for it in rows:
        a, w, j, r = _norm_item(it)
        if a.ndim == 3:
            specs.append(pl.BlockSpec((a.shape[0], tl, w), lambda i, j=j: (0, i, j)))
        else:
            specs.append(pl.BlockSpec((r or tl, w), lambda i, j=j: (i, j)))
        arrays.append(a)
    for a in consts:
        specs.append(pl.BlockSpec(a.shape, lambda i, nd=a.ndim: (0,) * nd))
        arrays.append(a)
    for it in prevs:
        a, w, j, _ = _norm_item(it)
        specs.append(pl.BlockSpec((HALO, w), lambda i, j=j: (jnp.maximum(i * hb - 1, 0), j)))
        arrays.append(a)
    for it in nexts:
        a, w, j, _ = _norm_item(it)
        specs.append(pl.BlockSpec((HALO, w), lambda i, j=j: (jnp.minimum((i + 1) * hb, nhalo - 1), j)))
        arrays.append(a)
    out_shape, out_specs = [], []
    for spec in out_rows:
        if len(spec) == 3:
            h, w, dt = spec
            out_shape.append(jax.ShapeDtypeStruct((h, length, w), dt))
            out_specs.append(pl.BlockSpec((h, tl, w), lambda i: (0, i, 0)))
        else:
            w, dt = spec
            out_shape.append(jax.ShapeDtypeStruct((length, w), dt))
            out_specs.append(pl.BlockSpec((tl, w), lambda i: (i, 0)))
    for shape, dt in out_accs:
        out_shape.append(jax.ShapeDtypeStruct(shape, dt))
        out_specs.append(pl.BlockSpec(shape, lambda i, nd=len(shape): (0,) * nd))
    n_r, n_c, n_p, n_n = len(rows), len(consts), len(prevs), len(nexts)
    n_in = n_r + n_c + n_p + n_n
    n_or = len(out_rows)
    arrays, specs = arrays + list(deps), specs + [ANY] * len(deps)

    def body(*refs):
        i = pl.program_id(0)
        vals = [r[...] for r in refs[:n_in]]
        outs = refs[n_in + len(deps):]
        ctx = _Ctx(i, nblk, tl)
        ro, ao = fn(ctx, vals[:n_r], vals[n_r:n_r + n_c], vals[n_r + n_c:n_r + n_c + n_p], vals[n_r + n_c + n_p:])
        for r, v in zip(outs[:n_or], ro, strict=True):
            r[...] = v.astype(r.dtype)
        for r, v in zip(outs[n_or:], ao, strict=True):
            @pl.when(i == 0)
            def _(r=r, v=v):
                r[...] = v.astype(r.dtype)

            @pl.when(i > 0)
            def _(r=r, v=v):
                r[...] += v.astype(r.dtype)

    res = pl.pallas_call(
        body, name=name, grid=(nblk,), in_specs=specs, out_specs=out_specs, out_shape=out_shape,
        compiler_params=_params(("arbitrary",) if out_accs else ("parallel",)),
    )(*arrays)
    return res


def _heads(x, n, w):
    return [x[:, h * w:(h + 1) * w] for h in range(n)]


def _cat(xs):
    return jnp.concatenate(xs, axis=1)


def _row_index(ctx, nrows, offset=0):
    return ctx.i * ctx.tl + offset + _iota((nrows, 1), 0)


def _ln_stats(r):
    mu = jnp.mean(r, axis=1, keepdims=True)
    d = r - mu
    var = jnp.mean(d * d, axis=1, keepdims=True)
    rstd = lax.rsqrt(var + LN_EPS)
    return d * rstd, rstd


def ln_fwd(name, terms, g, b, tl=ROWS, deps=()):
    coefs = [c for c, _ in terms]
    length = terms[0][1].shape[0]

    def fn(ctx, rows, consts, prevs, nexts):
        r = sum(c * t for c, t in zip(coefs, rows))
        xh, _ = _ln_stats(r)
        return [xh * consts[0] + consts[1], r], []

    return rowwise(name, fn, length, min(tl, length), rows=[t for _, t in terms], consts=[g, b],
                   out_rows=[(D_MODEL, F32), (D_MODEL, F32)], deps=deps)


def ln_bwd(name, r, terms, g, tl=ROWS, deps=()):
    coefs = [c for c, _ in terms]
    length = r.shape[0]

    def fn(ctx, rows, consts, prevs, nexts):
        xh, rstd = _ln_stats(rows[0])
        dy = sum(c * t for c, t in zip(coefs, rows[1:]))
        dxh = dy * consts[0]
        dr = rstd * (dxh - jnp.mean(dxh, axis=1, keepdims=True) - xh * jnp.mean(dxh * xh, axis=1, keepdims=True))
        return [dr], [_csum(dy * xh), _csum(dy)]

    return rowwise(name, fn, length, min(tl, length), rows=[r] + [t for _, t in terms], consts=[g],
                   out_rows=[(D_MODEL, F32)], out_accs=[((1, D_MODEL), F32), ((1, D_MODEL), F32)], deps=deps)


def ln_loss(name, terms, g, b, target, tl=ROWS):
    coefs = [c for c, _ in terms]
    length = target.shape[0]
    nt = len(terms)

    def fn(ctx, rows, consts, prevs, nexts):
        r = sum(c * t for c, t in zip(coefs, rows[:nt]))
        xh, _ = _ln_stats(r)
        err = xh * consts[0] + consts[1] - rows[nt]
        tot = _csum(_rsum(err * err)) * (0.5 / D_MODEL)
        return [err * (1.0 / D_MODEL), r], [jnp.broadcast_to(tot, (1, 128))]

    return rowwise(name, fn, length, min(tl, length), rows=[t for _, t in terms] + [target], consts=[g, b],
                   out_rows=[(D_MODEL, F32), (D_MODEL, F32)], out_accs=[((1, 128), F32)])


def _ffn_blocks(length):
    return min(512, length), D_FF // 2


def ffn_gate_up_act(name, x, wg, wu, deps=()):
    length = x.shape[0]
    tm, tn = _ffn_blocks(length)

    def body(x_ref, wg_ref, wu_ref, *rest):
        hg_ref, hu_ref, act_ref = rest[-3:]
        xb = x_ref[...].astype(MMD)
        hg = dnt(xb, wg_ref[...])
        hu = dnt(xb, wu_ref[...])
        hg_ref[...] = hg
        hu_ref[...] = hu
        act_ref[...] = (_silu(hg) * hu).astype(act_ref.dtype)

    row = pl.BlockSpec((tm, D_MODEL), lambda i, j: (i, 0))
    wsp = pl.BlockSpec((tn, D_MODEL), lambda i, j: (j, 0))
    osp = pl.BlockSpec((tm, tn), lambda i, j: (i, j))
    return pl.pallas_call(
        body, name=name, grid=(length // tm, D_FF // tn), in_specs=[row, wsp, wsp] + [ANY] * len(deps),
        out_specs=[osp] * 3,
        out_shape=[jax.ShapeDtypeStruct((length, D_FF), F32)] * 2 + [jax.ShapeDtypeStruct((length, D_FF), BF16)],
        compiler_params=_params(("parallel", "parallel")),
    )(x, wg, wu, *deps)


def ffn_dact(name, dr, wd, hg, hu, deps=()):
    length = dr.shape[0]
    tm, tn = _ffn_blocks(length)

    def body(dr_ref, wd_ref, hg_ref, hu_ref, *rest):
        dhg_ref, dhu_ref = rest[-2:]
        da = 0.5 * dnt(dr_ref[...], wd_ref[...])
        g = hg_ref[...]
        s = _sigmoid(g)
        dhg_ref[...] = (da * hu_ref[...] * (s * (1.0 + g * (1.0 - s)))).astype(dhg_ref.dtype)
        dhu_ref[...] = (da * (g * s)).astype(dhu_ref.dtype)

    row = pl.BlockSpec((tm, D_MODEL), lambda i, j: (i, 0))
    wsp = pl.BlockSpec((tn, D_MODEL), lambda i, j: (j, 0))
    osp = pl.BlockSpec((tm, tn), lambda i, j: (i, j))
    return pl.pallas_call(
        body, name=name, grid=(length // tm, D_FF // tn), in_specs=[row, wsp, osp, osp] + [ANY] * len(deps),
        out_specs=[osp] * 2, out_shape=[jax.ShapeDtypeStruct((length, D_FF), BF16)] * 2,
        compiler_params=_params(("parallel", "parallel")),
    )(dr, wd, hg, hu, *deps)


def ffn_fwd(tag, x, wg, wu, wd, deps=()):
    hg, hu, act = ffn_gate_up_act(tag + "_gate_up", x, wg, wu, deps)
    if callable(wd):
        wd = wd(act)
    f = mm(tag + "_down", act, wd)
    return f, (hg, hu, act), wd


def ffn_bwd(tag, x, res, dr, wg, wu, wd, deps=(), on_dwd=None, on_dwgu=None, also=None):
    hg, hu, act = res
    dwd = mm(tag + "_dwd", act, dr, ta=True, scale=0.5, deps=deps)
    dhg, dhu = ffn_dact(tag + "_dact", dr, wd, hg, hu, deps=on_dwd(dwd) if on_dwd else ())
    dwg = mm(tag + "_dwg", dhg, x, ta=True)
    dwu = mm(tag + "_dwu", dhu, x, ta=True)
    dx = mm(tag + "_dxg", dhg, wg, deps=on_dwgu(dwg, dwu) if on_dwgu else ())
    dx = mm(tag + "_dxu", dhu, wu, add=[(1.0, dx)] + ([also] if also else []))
    return dx, dwg, dwu, dwd


def _conv_taps(ext, taps, n):
    out = taps[3] * ext
    for j in range(3):
        out = out + taps[j] * pltpu.roll(ext, 3 - j, 0)
    return out


def _l2n(x):
    r = lax.rsqrt(_rsum(x * x) + L2_EPS)
    return x * r, r


def conv_fwd(name, pre, taps, tl=ROWS_WIDE):
    length = pre.shape[0]
    tl = min(tl, length)

    def fn(ctx, rows, consts, prevs, nexts):
        prev = jnp.where(ctx.i > 0, prevs[0], 0.0)
        ext = jnp.concatenate([prev, rows[0]], axis=0)
        s = _silu(_conv_taps(ext, consts, tl + HALO)[HALO:])
        q = _cat([_l2n(x)[0] * (HD ** -0.5) for x in _heads(s[:, :DN_WIDTH], DN_HEADS, HD)])
        k = _cat([_l2n(x)[0] for x in _heads(s[:, DN_WIDTH:2 * DN_WIDTH], DN_HEADS, HD)])
        return [q, k, s[:, 2 * DN_WIDTH:]], []

    return rowwise(name, fn, length, tl, rows=[pre], consts=list(taps), prevs=[pre],
                   out_rows=[(DN_WIDTH, F32)] * 3)


def conv_bwd(name, pre, dq, dk, dv, taps, tl=ROWS_WIDE):
    length = pre.shape[0]
    tl = min(tl, length)
    n = tl + 2 * HALO

    def fn(ctx, rows, consts, prevs, nexts):
        last = ctx.i == ctx.nblk - 1
        prev = jnp.where(ctx.i > 0, prevs[0], 0.0)
        ext = jnp.concatenate([prev, rows[0], nexts[0]], axis=0)
        c = _conv_taps(ext, consts, n)
        sg = _sigmoid(c)
        s = c * sg
        zero = jnp.zeros((HALO, DN_WIDTH), F32)
        dqe, dke, dve = [jnp.concatenate([zero, rows[1 + t], jnp.where(last, 0.0, nexts[1 + t])], axis=0)
                         for t in range(3)]

        def l2_bwd(x, dy):
            y, r = _l2n(x)
            return r * (dy - y * _rsum(dy * y))

        dsq = _cat([l2_bwd(x, d * (HD ** -0.5)) for x, d in zip(_heads(s[:, :DN_WIDTH], DN_HEADS, HD),
                                                                 _heads(dqe, DN_HEADS, HD))])
        dsk = _cat([l2_bwd(x, d) for x, d in zip(_heads(s[:, DN_WIDTH:2 * DN_WIDTH], DN_HEADS, HD),
                                                  _heads(dke, DN_HEADS, HD))])
        dc = _cat([dsq, dsk, dve]) * (sg * (1.0 + c * (1.0 - sg)))
        dpre = consts[3] * dc
        for j in range(3):
            dpre = dpre + consts[j] * pltpu.roll(dc, n - (3 - j), 0)
        dc_cur = dc[HALO:HALO + tl]
        dws = [_csum(dc_cur * pltpu.roll(ext, 3 - j, 0)[HALO:HALO + tl]) for j in range(3)]
        dws.append(_csum(dc_cur * ext[HALO:HALO + tl]))
        return [dpre[HALO:HALO + tl]], dws

    return rowwise(name, fn, length, tl, rows=[pre, dq, dk, dv], consts=list(taps), prevs=[pre],
                   nexts=[pre, dq, dk, dv], out_rows=[(3 * DN_WIDTH, BF16)],
                   out_accs=[((1, 3 * DN_WIDTH), F32)] * 4)


def _gate_consts():
    lane = jnp.arange(128)[:, None]
    col = jnp.arange(2 * DN_WIDTH)[None, :]
    sel = ((lane < 2 * DN_HEADS) & (col // HD == lane)).astype(F32)
    pick = ((col.T == lane.T * HD) & (lane.T < 2 * DN_HEADS)).astype(F32)
    return sel, pick


def _gate_math(ab, alog, dtb):
    z = ab + dtb
    g = -jnp.exp(alog) * _softplus(z)
    beta = _sigmoid(ab)
    return z, g, beta


def gates_fwd(name, ab, alog, dtb, sel, tl=ROWS):
    length = ab.shape[0]

    def fn(ctx, rows, consts, prevs, nexts):
        _, g, beta = _gate_math(rows[0], consts[0], consts[1])
        lane = _iota(g.shape, 1)
        small = jnp.where(lane < DN_HEADS, g, jnp.where(lane < 2 * DN_HEADS, beta, 0.0))
        big = dnn(small, consts[2], HI)
        return [big[:, :DN_WIDTH], big[:, DN_WIDTH:]], []

    return rowwise(name, fn, length, min(tl, length), rows=[ab], consts=[alog, dtb, sel],
                   out_rows=[(DN_WIDTH, F32)] * 2)


def gates_bwd(name, ab, dgb, dbb, alog, dtb, pick, tl=ROWS):
    length = ab.shape[0]

    def fn(ctx, rows, consts, prevs, nexts):
        z, g, beta = _gate_math(rows[0], consts[0], consts[1])
        dsmall = dnn(_cat([rows[1], rows[2]]), consts[2], HI)
        lane = _iota(g.shape, 1)
        is_a = lane < DN_HEADS
        da = jnp.where(is_a, dsmall * (-jnp.exp(consts[0])) * _sigmoid(z), 0.0)
        db = jnp.where((lane >= DN_HEADS) & (lane < 2 * DN_HEADS), dsmall * beta * (1.0 - beta), 0.0)
        return [da + db], [_csum(jnp.where(is_a, dsmall * g, 0.0)), _csum(da)]

    return rowwise(name, fn, length, min(tl, length), rows=[ab, dgb, dbb], consts=[alog, dtb, pick],
                   out_rows=[(128, BF16)], out_accs=[((1, 128), F32)] * 2)


CPS = 2


def _chunk_scan_rows(x, suffix=False):
    n = x.shape[0]
    rc = _iota(x.shape, 0) & (CHUNK - 1)
    sh = 1
    while sh < CHUNK:
        if suffix:
            x = x + jnp.where(rc < CHUNK - sh, pltpu.roll(x, n - sh, 0), 0.0)
        else:
            x = x + jnp.where(rc >= sh, pltpu.roll(x, sh, 0), 0.0)
        sh *= 2
    return x


def _tri_inv(a_list, eye, bd):
    def each(f, *ls):
        return [f(*xs) for xs in zip(*ls)]

    dg = [jnp.where(bd, a, 0.0) for a in a_list]
    lo = each(lambda a, d: a - d, a_list, dg)
    n1 = [-d for d in dg]
    n2 = each(lambda n: dnn(n, n, X3), n1)
    n4 = each(lambda n: dnn(n, n, X3), n2)
    td = each(lambda p, s: dnn(eye + p, eye + s, X3), n1, n2)
    n8 = each(lambda n: dnn(n, n, X3), n4)
    td = each(lambda t, n: dnn(t, eye + n, X3), td, n4)
    td = each(lambda t, n: dnn(t, eye + n, X3), td, n8)
    m = each(lambda t, l: dnn(t, l, X3), td, lo)
    m2 = each(lambda x: dnn(x, x, X3), m)
    x = each(lambda p, s: dnn(eye - p, eye + s, X3), m, m2)
    return each(lambda p, t: dnn(p, t, X3), x, td)


def _chunk_common(q, k, v, gcb, bb):
    egb = jnp.exp(gcb)
    gc64 = gcb[:, :CHUNK]
    ii, jj = _iota((CHUNK, CHUNK), 0), _iota((CHUNK, CHUNK), 1)
    incl, strict = ii >= jj, ii > jj
    decay = jnp.exp(jnp.where(incl, gc64 - gc64.T, -jnp.inf))
    kb = k * bb
    vb = v * bb
    kbe = kb * egb
    pq = dnt(jnp.concatenate([kb, q], axis=0), k, X3)
    ekb = jnp.exp(gcb[CHUNK - 1:CHUNK, :] - gcb)
    return dict(egb=egb, decay=decay, kb=kb, vb=vb, kbe=kbe, pm=pq[:CHUNK], qm=pq[CHUNK:], ekb=ekb,
                incl=incl, strict=strict, ii=ii, jj=jj)


def _chunk_head(vals, ci, h):
    return [v[ci * CHUNK:(ci + 1) * CHUNK, h * HD:(h + 1) * HD] for v in vals]


def _assemble(per_chunk):
    return jnp.concatenate([_cat(hs) for hs in per_chunk], axis=0)


def _assemble3(per_chunk):
    return jnp.stack([jnp.concatenate([per_chunk[ci][h] for ci in range(CPS)], axis=0) for h in range(DN_HEADS)])


def delta_prep_fwd(name, q, k, v, gb, bb):
    length = q.shape[0]

    def fn(ctx, rows, consts, prevs, nexts):
        gcb_all = _chunk_scan_rows(rows[3])
        vals = [rows[0], rows[1], rows[2], gcb_all, rows[4]]
        units = [(ci, h) for ci in range(CPS) for h in range(DN_HEADS)]
        ins = [_chunk_head(vals, ci, h) for ci, h in units]
        cs = [_chunk_common(*i) for i in ins]
        eye = (cs[0]["ii"] == cs[0]["jj"]).astype(F32)
        ts = _tri_inv([jnp.where(c["strict"], c["pm"] * c["decay"], 0.0) for c in cs], eye,
                      (cs[0]["ii"] >> 4) == (cs[0]["jj"] >> 4))
        uws = [dnn(t, _cat([c["vb"], c["kbe"]]), X3) for t, c in zip(ts, cs)]

        def grid2(xs):
            return [xs[ci * DN_HEADS:(ci + 1) * DN_HEADS] for ci in range(CPS)]

        return [_assemble(grid2([uw[:, :HD] for uw in uws])), _assemble(grid2([uw[:, HD:] for uw in uws])),
                _assemble(grid2([i[0] * c["egb"] for i, c in zip(ins, cs)])),
                _assemble(grid2([i[1] * c["ekb"] for i, c in zip(ins, cs)])), gcb_all,
                _assemble3(grid2([c["qm"] * c["decay"] for c in cs])), _assemble3(grid2(ts))], []

    return rowwise(name, fn, length, CHUNK * CPS, rows=[q, k, v, gb, bb],
                   out_rows=[(DN_WIDTH, F32)] * 5 + [(DN_HEADS, CHUNK, F32)] * 2)


def delta_prep_bwd(name, q, k, v, gb, bb, t3, du, dw, dqd, dkd, dattn3, dgl):
    length = q.shape[0]

    def fn(ctx, rows, consts, prevs, nexts):
        gcb_all = _chunk_scan_rows(rows[3])
        vals = [rows[0], rows[1], rows[2], gcb_all] + list(rows[4:9])
        t3v, da3v, dglv = rows[9], rows[10], rows[11]
        units = [(ci, h) for ci in range(CPS) for h in range(DN_HEADS)]
        ins = [_chunk_head(vals, ci, h) for ci, h in units]
        cs = [_chunk_common(*i[:5]) for i in ins]
        ts = [t3v[h][ci * CHUNK:(ci + 1) * CHUNK] for ci, h in units]
        dattns = [jnp.where(c["incl"], da3v[h][ci * CHUNK:(ci + 1) * CHUNK], 0.0) for (ci, h), c in zip(units, cs)]
        duws = [_cat([i[5], i[6]]) for i in ins]
        dvks = [dtn(t, d, X3) for t, d in zip(ts, duws)]
        dts = [dnt(d, _cat([c["vb"], c["kbe"]]), X3) for d, c in zip(duws, cs)]
        dts = [dnt(d, t, X3) for d, t in zip(dts, ts)]
        das = [jnp.where(c["strict"], -dtn(t, d, X3), 0.0) for c, t, d in zip(cs, ts, dts)]
        dpqs = [jnp.concatenate([da * c["decay"], dat * c["decay"]], axis=0) for da, dat, c in zip(das, dattns, cs)]
        dpqks = [dnn(d, i[1], X3) for d, i in zip(dpqs, ins)]
        dkps = [dtn(d, jnp.concatenate([c["kb"], i[0]], axis=0), X3) for d, c, i in zip(dpqs, cs, ins)]
        dqs, dks, dvs, dgcs, dbs = [], [], [], [], []
        for (ci, h), i, c, dvk, da, dattn, dpqk, dkp in zip(units, ins, cs, dvks, das, dattns, dpqks, dkps):
            qh, kh, vh, _, bh, _, _, dqdh, dkdh = i
            dvb, dkbe = dvk[:, :HD], dvk[:, HD:]
            dkb = dpqk[:CHUNK] + dkbe * c["egb"]
            c1 = _rsum(dkbe * c["kb"] + dqdh * qh) * c["egb"]
            c2 = _rsum(dkdh * kh) * c["ekb"]
            e = (da * c["pm"] + dattn * c["qm"]) * c["decay"]
            dgc = c1 - c2 + _rsum(e) - _rsum(e.T)
            dgl_tot = jnp.max(dglv[ci * 8:(ci + 1) * 8, h * HD:(h + 1) * HD], axis=0, keepdims=True) + _csum(c2)
            dgcs.append(dgc + jnp.where(_iota((CHUNK, HD), 0) == CHUNK - 1, dgl_tot, 0.0))
            dqs.append(dpqk[CHUNK:] + dqdh * c["egb"])
            dks.append(dkp + dkdh * c["ekb"] + dkb * bh)
            dvs.append(dvb * bh)
            dbs.append(jnp.broadcast_to(_rsum(dkb * kh + dvb * vh), (CHUNK, HD)))

        def grid2(xs):
            return [xs[ci * DN_HEADS:(ci + 1) * DN_HEADS] for ci in range(CPS)]

        return [_assemble(grid2(dqs)), _assemble(grid2(dks)), _assemble(grid2(dvs)),
                _chunk_scan_rows(_assemble(grid2(dgcs)), suffix=True), _assemble(grid2(dbs))], []

    return rowwise(name, fn, length, CHUNK * CPS,
                   rows=[q, k, v, gb, bb, du, dw, dqd, dkd, t3, dattn3, (dgl, DN_WIDTH, 0, 8 * CPS)],
                   out_rows=[(DN_WIDTH, F32)] * 5)


SCAN_CHUNKS = 4


def _scan_chunks(n):
    return SCAN_CHUNKS if n % SCAN_CHUNKS == 0 else 1


def delta_scan_fwd(name, qd, kd, u, w, attn3, gcb):
    length = qd.shape[0]
    n = length // CHUNK
    sc = _scan_chunks(n)
    row = pl.BlockSpec((sc * CHUNK, DN_WIDTH), lambda c: (c, 0))
    sq = pl.BlockSpec((DN_HEADS, sc * CHUNK, CHUNK), lambda c: (0, c, 0))

    def body(qd_ref, kd_ref, u_ref, w_ref, attn_ref, gc_ref, o_ref, vn_ref, st_ref, s_ref):
        c = pl.program_id(0)

        @pl.when(c == 0)
        def _():
            s_ref[...] = jnp.zeros_like(s_ref)

        heads = range(DN_HEADS)
        sls = [pl.ds(h * HD, HD) for h in heads]
        ss = [s_ref[h] for h in heads]
        for ci in range(sc):
            rs = pl.ds(ci * CHUNK, CHUNK)
            ws = [dnn(w_ref[rs, sl], s) for sl, s in zip(sls, ss)]
            qs = [dnn(qd_ref[rs, sl], s) for sl, s in zip(sls, ss)]
            vns = [u_ref[rs, sl] - x for sl, x in zip(sls, ws)]
            avs = [dnn(attn_ref[h, rs, :], vn) for h, vn in zip(heads, vns)]
            kvs = [dtn(kd_ref[rs, sl], vn) for sl, vn in zip(sls, vns)]
            for h, sl in zip(heads, sls):
                st_ref[ci, h] = ss[h]
                o_ref[rs, sl] = qs[h] + avs[h]
                vn_ref[rs, sl] = vns[h]
            ss = [s * jnp.exp(gc_ref[pl.ds(ci * CHUNK + CHUNK - 1, 1), sl]) + kv for s, sl, kv in zip(ss, sls, kvs)]
        for h in heads:
            s_ref[h] = ss[h]

    return pl.pallas_call(
        body, name=name, grid=(n // sc,), in_specs=[row, row, row, row, sq, row],
        out_specs=[row, row, pl.BlockSpec((sc, DN_HEADS, HD, HD), lambda c: (c, 0, 0, 0))],
        out_shape=[jax.ShapeDtypeStruct((length, DN_WIDTH), F32), jax.ShapeDtypeStruct((length, DN_WIDTH), F32),
                   jax.ShapeDtypeStruct((n, DN_HEADS, HD, HD), F32)],
        scratch_shapes=[pltpu.VMEM((DN_HEADS, HD, HD), F32)],
        compiler_params=_params(("arbitrary",)),
    )(qd, kd, u, w, attn3, gcb)


def delta_scan_bwd(name, do, qd, kd, w, attn3, vn, st, gcb):
    length = qd.shape[0]
    n = length // CHUNK
    sc = _scan_chunks(n)
    nb = n // sc
    row = pl.BlockSpec((sc * CHUNK, DN_WIDTH), lambda c: (nb - 1 - c, 0))
    sq = pl.BlockSpec((DN_HEADS, sc * CHUNK, CHUNK), lambda c: (0, nb - 1 - c, 0))
    stb = pl.BlockSpec((sc, DN_HEADS, HD, HD), lambda c: (nb - 1 - c, 0, 0, 0))
    glb = pl.BlockSpec((sc * 8, DN_WIDTH), lambda c: (nb - 1 - c, 0))

    def body(do_ref, qd_ref, kd_ref, w_ref, attn_ref, vn_ref, st_ref, gc_ref,
             dqd_ref, dkd_ref, du_ref, dw_ref, dattn_ref, dgl_ref, ds_ref):
        c = pl.program_id(0)

        @pl.when(c == 0)
        def _():
            ds_ref[...] = jnp.zeros_like(ds_ref)

        heads = range(DN_HEADS)
        sls = [pl.ds(h * HD, HD) for h in heads]
        dsns = [ds_ref[h] for h in heads]
        for ci in reversed(range(sc)):
            rs = pl.ds(ci * CHUNK, CHUNK)
            ss = [st_ref[ci, h] for h in heads]
            dos = [do_ref[rs, sl] for sl in sls]
            vns = [vn_ref[rs, sl] for sl in sls]
            dvns = [dtn(attn_ref[h, rs, :], d) for h, d in zip(heads, dos)]
            dvns = [x + dnn(kd_ref[rs, sl], dsn) for x, sl, dsn in zip(dvns, sls, dsns)]
            qdos = [dtn(qd_ref[rs, sl], d) for sl, d in zip(sls, dos)]
            for h, sl in zip(heads, sls):
                dattn_ref[h, rs, :] = dnt(dos[h], vns[h])
                dqd_ref[rs, sl] = dnt(dos[h], ss[h])
                dkd_ref[rs, sl] = dnt(vns[h], dsns[h])
                du_ref[rs, sl] = dvns[h]
            dws = [dnt(dvn, s) for dvn, s in zip(dvns, ss)]
            wdvs = [dtn(w_ref[rs, sl], dvn) for sl, dvn in zip(sls, dvns)]
            nxt = []
            for h, sl in zip(heads, sls):
                egl = jnp.exp(gc_ref[pl.ds(ci * CHUNK + CHUNK - 1, 1), sl])
                dw_ref[rs, sl] = -dws[h]
                dgl_ref[pl.ds(ci * 8, 8), sl] = jnp.broadcast_to(_csum(_rsum(dsns[h] * ss[h])) * egl, (8, HD))
                nxt.append(dsns[h] * egl + qdos[h] - wdvs[h])
            dsns = nxt
        for h in heads:
            ds_ref[h] = dsns[h]

    return pl.pallas_call(
        body, name=name, grid=(nb,), in_specs=[row, row, row, row, sq, row, stb, row],
        out_specs=[row, row, row, row, sq, glb],
        out_shape=[jax.ShapeDtypeStruct((length, DN_WIDTH), F32)] * 4
        + [jax.ShapeDtypeStruct((DN_HEADS, length, CHUNK), F32), jax.ShapeDtypeStruct((n * 8, DN_WIDTH), F32)],
        scratch_shapes=[pltpu.VMEM((DN_HEADS, HD, HD), F32)],
        compiler_params=_params(("arbitrary",)),
    )(do, qd, kd, w, attn3, vn, st, gcb)


def onorm_fwd(name, o, z, nw, tl=ROWS):
    length = o.shape[0]

    def fn(ctx, rows, consts, prevs, nexts):
        outs = []
        for oh, zh in zip(_heads(rows[0], DN_HEADS, HD), _heads(rows[1], DN_HEADS, HD)):
            r = lax.rsqrt(jnp.mean(oh * oh, axis=1, keepdims=True) + RMS_EPS)
            outs.append(oh * r * consts[0] * _silu(zh))
        return [_cat(outs)], []

    return rowwise(name, fn, length, min(tl, length), rows=[o, z], consts=[nw], out_rows=[(DN_WIDTH, BF16)])[0]


def onorm_bwd(name, o, z, d_on, nw, tl=ROWS):
    length = o.shape[0]

    def fn(ctx, rows, consts, prevs, nexts):
        dos, dzs = [], []
        dnw = jnp.zeros((1, HD), F32)
        for oh, zh, dh in zip(*[_heads(r, DN_HEADS, HD) for r in rows]):
            r = lax.rsqrt(jnp.mean(oh * oh, axis=1, keepdims=True) + RMS_EPS)
            y = oh * r
            sz = _silu(zh)
            t = dh * sz * consts[0]
            dos.append(r * (t - y * jnp.mean(t * y, axis=1, keepdims=True)))
            dzs.append(dh * y * consts[0] * _dsilu(zh))
            dnw = dnw + _csum(dh * y * sz)
        return [_cat(dos), _cat(dzs)], [dnw]

    return rowwise(name, fn, length, min(tl, length), rows=[o, z, d_on], consts=[nw],
                   out_rows=[(DN_WIDTH, F32), (DN_WIDTH, BF16)], out_accs=[((1, HD), F32)])


def merge_fwd(name, gates, ydn, ypool, tl=ROWS_WIDE):
    length = ydn.shape[0]

    def fn(ctx, rows, consts, prevs, nexts):
        gt = rows[0]
        return [_sigmoid(gt[:, :D_MODEL]) * rows[1] + _sigmoid(gt[:, D_MODEL:]) * rows[2]], []

    return rowwise(name, fn, length, min(tl, length), rows=[gates, ydn, ypool], out_rows=[(D_MODEL, BF16)])[0]


def merge_bwd(name, gates, ydn, ypool, dm, tl=ROWS_WIDE):
    length = ydn.shape[0]

    def fn(ctx, rows, consts, prevs, nexts):
        gt, yd, yp, d = rows
        sd, sp = _sigmoid(gt[:, :D_MODEL]), _sigmoid(gt[:, D_MODEL:])
        dgates = _cat([d * yd * sd * (1.0 - sd), d * yp * sp * (1.0 - sp)])
        return [d * sd, d * sp, dgates], []

    return rowwise(name, fn, length, min(tl, length), rows=[gates, ydn, ypool, dm],
                   out_rows=[(D_MODEL, BF16), (D_MODEL, BF16), (2 * D_MODEL, BF16)])


def _trailing_sums(ext, upto):
    s, sh = ext, 1
    while sh < upto:
        s = s + pltpu.roll(s, sh, 0)
        sh *= 2
    return s


def _leading_sums(ext, upto, n):
    s, sh = ext, 1
    while sh < upto:
        s = s + pltpu.roll(s, n - sh, 0)
        sh *= 2
    return s


def _pool_mixed(ctx, p, prev, tl):
    prevm = jnp.where(ctx.i > 0, prev, 0.0)
    t1 = (_row_index(ctx, tl) + 1).astype(F32)
    outs = []
    for gi, win in enumerate(POOL_WINDOWS):
        sl = slice(gi * HD, (gi + 1) * HD)
        ext = jnp.concatenate([prevm[:, sl], p[:, sl]], axis=0)
        mean = _trailing_sums(ext, win)[HALO:] / jnp.minimum(t1, float(win))
        outs.append(mean - p[:, sl])
    return outs


def pool_fwd(name, p, pool_w, scale, tl=ROWS):
    length = p.shape[0]
    tl = min(tl, length)

    def fn(ctx, rows, consts, prevs, nexts):
        mixed = _pool_mixed(ctx, rows[0], prevs[0], tl)
        y = _cat([dnn(m, consts[0][gi]) for gi, m in enumerate(mixed)])
        return [y * consts[1]], []

    return rowwise(name, fn, length, tl, rows=[p], consts=[pool_w, scale], prevs=[p],
                   out_rows=[(POOL_WIDTH, BF16)])[0]


def pool_bwd(name, p, dpo, pool_w, scale, tl=ROWS):
    length = p.shape[0]
    tl = min(tl, length)
    n = tl + HALO

    def fn(ctx, rows, consts, prevs, nexts):
        last = ctx.i == ctx.nblk - 1
        mixed = _pool_mixed(ctx, rows[0], prevs[0], tl)
        dext = jnp.concatenate([rows[1], jnp.where(last, 0.0, nexts[0])], axis=0)
        t1 = (_row_index(ctx, n) + 1).astype(F32)
        dps, dws, dscs = [], [], []
        for gi, win in enumerate(POOL_WINDOWS):
            sl = slice(gi * HD, (gi + 1) * HD)
            wg = consts[0][gi]
            dyraw = dext[:, sl] * consts[1][:, sl]
            dmix = dnt(dyraw, wg)
            dws.append(dtn(mixed[gi], dyraw[:tl]))
            dscs.append(_csum(rows[1][:, sl] * dnn(mixed[gi], wg)))
            lead = _leading_sums(dmix / jnp.minimum(t1, float(win)), win, n)
            dps.append(lead[:tl] - dmix[:tl])
        return [_cat(dps)], [jnp.stack(dws), _cat(dscs)]

    return rowwise(name, fn, length, tl, rows=[p, dpo], consts=[pool_w, scale], prevs=[p], nexts=[dpo],
                   out_rows=[(POOL_WIDTH, BF16)],
                   out_accs=[((len(POOL_WINDOWS), HD, HD), F32), ((1, POOL_WIDTH), F32)])


def _xa_probs(qh, kh):
    s = dnt(qh, kh) * (XA_HD ** -0.5)
    e = jnp.exp(s - jnp.max(s, axis=1, keepdims=True))
    return e / _rsum(e)


def xattn_fwd(name, qx, kx, vx, tl=ROWS):
    length = qx.shape[0]

    def fn(ctx, rows, consts, prevs, nexts):
        outs = [dnn(_xa_probs(qh, kh), vh) for qh, kh, vh in
                zip(_heads(rows[0], XA_HEADS, XA_HD), _heads(consts[0], XA_HEADS, XA_HD),
                    _heads(consts[1], XA_HEADS, XA_HD))]
        return [_cat(outs)], []

    return rowwise(name, fn, length, min(tl, length), rows=[qx], consts=[kx, vx], out_rows=[(D_MODEL, BF16)])[0]


def xattn_bwd(name, qx, dox, kx, vx, tl=ROWS):
    length = qx.shape[0]

    def fn(ctx, rows, consts, prevs, nexts):
        dqs, dks, dvs = [], [], []
        for qh, dh, kh, vh in zip(_heads(rows[0], XA_HEADS, XA_HD), _heads(rows[1], XA_HEADS, XA_HD),
                                  _heads(consts[0], XA_HEADS, XA_HD), _heads(consts[1], XA_HEADS, XA_HD)):
            pr = _xa_probs(qh, kh)
            dpr = dnt(dh, vh)
            ds = pr * (dpr - _rsum(dpr * pr)) * (XA_HD ** -0.5)
            dqs.append(dnn(ds, kh))
            dks.append(dtn(ds, qh))
            dvs.append(dtn(pr, dh))
        return [_cat(dqs)], [_cat(dks), _cat(dvs)]

    return rowwise(name, fn, length, min(tl, length), rows=[qx, dox], consts=[kx, vx],
                   out_rows=[(D_MODEL, BF16)], out_accs=[((N_MEM, D_MODEL), F32)] * 2)


def local_step(x, mem, target, w, io):
    sel, pick = _gate_consts()
    alog = jnp.pad(w["a_log"], ((0, 0), (0, 128 - DN_HEADS)))
    dtb = jnp.pad(w["dt_bias"], ((0, 0), (0, 128 - DN_HEADS)))

    f1, res1, w_down1 = ffn_fwd("ffn1", x, w["ffn1_w_gate"], w["ffn1_w_up"], io.ffn1_down, deps=io.rest_started())
    x1, r1 = ln_fwd("ln1", [(ALPHA, x), (0.5, f1)], w["ln1_g"], w["ln1_b"], deps=io.halfway("mid", f1))
    w = dict(w, ffn1_w_down=w_down1, **io.weights("mid", x1))
    taps = [w["conv_w"][j:j + 1] for j in range(4)]

    pre = mm("in_qkv", x1, w["in_qkv"], tb=True)
    z = mm("in_z", x1, w["in_z"], tb=True)
    gates = mm("in_gates", x1, w["in_gates"], tb=True)
    p = mm("in_p", x1, w["in_p"], tb=True)
    ab = mm("in_ab", x1, w["in_ab"], tb=True)
    q, k, v = conv_fwd("conv", pre, taps)
    gb, bb = gates_fwd("gates", ab, alog, dtb, sel)
    u, wd_, qd, kd, gcb, attn3, t3 = delta_prep_fwd("dprep", q, k, v, gb, bb)
    o, vn, st = delta_scan_fwd("dscan", qd, kd, u, wd_, attn3, gcb)
    on = onorm_fwd("onorm", o, z, w["dn_norm_w"])
    ydn = mm("dn_branch", on, w["w_dn_branch"], tb=True)
    po = pool_fwd("pool", p, w["pool_w"], w["pool_scale"])
    ypool = mm("pool_branch", po, w["w_pool_branch"], tb=True)
    merged = merge_fwd("merge", gates, ydn, ypool)
    mix = mm("mix_out", merged, w["w_mix_out"])
    x2, r2 = ln_fwd("ln2", [(ALPHA, x1), (1.0, mix)], w["ln2_g"], w["ln2_b"])

    m, _ = ln_fwd("ln_mem", [(1.0, mem)], w["mem_ln_g"], w["mem_ln_b"])
    qx = mm("xa_q", x2, w["xa_wq"], deps=io.halfway("ffn2", x2))
    kx = mm("xa_k", m, w["xa_wk"])
    vx = mm("xa_v", m, w["xa_wv"])
    ox = xattn_fwd("xattn", qx, kx, vx)
    xa = mm("xa_o", ox, w["xa_wo"])
    x3, r3 = ln_fwd("ln3", [(ALPHA, x2), (1.0, xa)], w["ln3_g"], w["ln3_b"])
    w = dict(w, **io.weights("ffn2", x3))

    f2, res2, _ = ffn_fwd("ffn2", x3, w["ffn2_w_gate"], w["ffn2_w_up"], w["ffn2_w_down"])
    dy4, r4, loss = ln_loss("ln4_loss", [(ALPHA, x3), (0.5, f2)], w["ln4_g"], w["ln4_b"], target)

    g = {}
    dr4, g["ln4_g"], g["ln4_b"] = ln_bwd("ln4_b", r4, [(1.0, dy4)], w["ln4_g"])
    dx3, g["ffn2_w_gate"], g["ffn2_w_up"], g["ffn2_w_down"] = ffn_bwd(
        "ffn2b", x3, res2, dr4, w["ffn2_w_gate"], w["ffn2_w_up"], w["ffn2_w_down"])
    dep = io.grads_out("ffn2", g)
    dr3, g["ln3_g"], g["ln3_b"] = ln_bwd("ln3_b", r3, [(ALPHA, dr4), (1.0, dx3)], w["ln3_g"], deps=dep)

    dox = mm("xa_do", dr3, w["xa_wo"], tb=True)
    g["xa_wo"] = mm("xa_dwo", ox, dr3, ta=True)
    dqx, dkx, dvx = xattn_bwd("xattn_b", qx, dox, kx, vx)
    g["xa_wq"] = mm("xa_dwq", x2, dqx, ta=True)
    dx2 = mm("xa_dx", dqx, w["xa_wq"], tb=True)
    g["xa_wk"] = mm("xa_dwk", m, dkx, ta=True)
    g["xa_wv"] = mm("xa_dwv", m, dvx, ta=True)
    dmm = mm("xa_dmk", dkx, w["xa_wk"], tb=True, deps=io.grads_out("xa", g))
    dmm = mm("xa_dmv", dvx, w["xa_wv"], tb=True, add=dmm)
    _, g["mem_ln_g"], g["mem_ln_b"] = ln_bwd("ln_mem_b", mem, [(1.0, dmm)], w["mem_ln_g"])
    dr2, g["ln2_g"], g["ln2_b"] = ln_bwd("ln2_b", r2, [(ALPHA, dr3), (1.0, dx2)], w["ln2_g"])
    io.grads_in("ffn2", dr2)

    dmerged = mm("mix_dm", dr2, w["w_mix_out"], tb=True)
    g["w_mix_out"] = mm("mix_dw", merged, dr2, ta=True)
    d_ydn, d_ypool, d_gates = merge_bwd("merge_b", gates, ydn, ypool, dmerged)
    g["w_dn_branch"] = mm("dn_dw", d_ydn, on, ta=True)
    d_on = mm("dn_dx", d_ydn, w["w_dn_branch"])
    g["w_pool_branch"] = mm("pool_dw", d_ypool, po, ta=True)
    d_po = mm("pool_dx", d_ypool, w["w_pool_branch"])
    dp, g["pool_w"], g["pool_scale"] = pool_bwd("pool_b", p, d_po, w["pool_w"], w["pool_scale"])
    d_o, dz, g["dn_norm_w"] = onorm_bwd("onorm_b", o, z, d_on, w["dn_norm_w"])
    dqd, dkd, du, dw_, dattn3, dgl = delta_scan_bwd("dscan_b", d_o, qd, kd, wd_, attn3, vn, st, gcb)
    dq, dk, dv, dgb, dbb = delta_prep_bwd("dprep_b", q, k, v, gb, bb, t3, du, dw_, dqd, dkd, dattn3, dgl)
    dpre, dc0, dc1, dc2, dc3 = conv_bwd("conv_b", pre, dq, dk, dv, taps)
    g["conv_w"] = jnp.concatenate([dc0, dc1, dc2, dc3], axis=0)
    d_ab, dalog, ddtb = gates_bwd("gates_b", ab, dgb, dbb, alog, dtb, pick)
    g["a_log"] = dalog[:, :DN_HEADS]
    g["dt_bias"] = ddtb[:, :DN_HEADS]
    g["in_qkv"] = mm("in_dwqkv", dpre, x1, ta=True)
    g["in_z"] = mm("in_dwz", dz, x1, ta=True)
    g["in_gates"] = mm("in_dwgates", d_gates, x1, ta=True)
    g["in_p"] = mm("in_dwp", dp, x1, ta=True)
    g["in_ab"] = mm("in_dwab", d_ab, x1, ta=True)
    io.grads_in("xa", g["in_ab"])
    dx1 = mm("in_dxqkv", dpre, w["in_qkv"], deps=io.grads_out("mixer", g))
    dx1 = mm("in_dxz", dz, w["in_z"], add=dx1)
    dx1 = mm("in_dxgates", d_gates, w["in_gates"], add=dx1)
    dx1 = mm("in_dxp", dp, w["in_p"], add=dx1)
    dx1 = mm("in_dxab", d_ab, w["in_ab"], add=dx1)
    dr1, g["ln1_g"], g["ln1_b"] = ln_bwd("ln1_b", r1, [(ALPHA, dr2), (1.0, dx1)], w["ln1_g"])

    def on_dwd(dwd):
        return io.small_out(dict(g, loss=loss[0, :1])) + io.grads_out("ffn1_d", dict(ffn1_w_down=dwd))

    def on_dwgu(dwg, dwu):
        return io.grads_out("ffn1_gu", dict(ffn1_w_gate=dwg, ffn1_w_up=dwu))

    grad_x, g["ffn1_w_gate"], g["ffn1_w_up"], g["ffn1_w_down"] = ffn_bwd(
        "ffn1b", x, res1, dr1, w["ffn1_w_gate"], w["ffn1_w_up"], w["ffn1_w_down"], on_dwd=on_dwd, on_dwgu=on_dwgu,
        also=(ALPHA, dr1))
    return loss, grad_x, g


WEIGHT_NAMES = ['ffn1_w_gate', 'ffn1_w_up', 'ffn1_w_down', 'ln1_g', 'ln1_b', 'w_in', 'conv_w', 'a_log', 'dt_bias',
                'dn_norm_w', 'w_dn_branch', 'pool_w', 'pool_scale', 'w_pool_branch', 'w_mix_out', 'ln2_g', 'ln2_b',
                'mem_ln_g', 'mem_ln_b', 'xa_wq', 'xa_wk', 'xa_wv', 'xa_wo', 'ln3_g', 'ln3_b', 'ffn2_w_gate',
                'ffn2_w_up', 'ffn2_w_down', 'ln4_g', 'ln4_b']
SHARDED = [
    ("ffn1_w_gate", "cols", (1024, 352)), ("ffn1_w_up", "cols", (1024, 352)), ("ffn1_w_down", "rows", (352, 1024)),
    ("w_in", "cols", (1024, 577)), ("conv_w", "flat", (4, 192)), ("w_dn_branch", "cols", (512, 128)),
    ("w_pool_branch", "cols", (512, 128)), ("w_mix_out", "rows", (128, 1024)), ("xa_wq", "rows", (128, 1024)),
    ("xa_wk", "rows", (128, 1024)), ("xa_wv", "rows", (128, 1024)), ("xa_wo", "rows", (128, 1024)),
    ("ffn2_w_gate", "cols", (1024, 352)), ("ffn2_w_up", "cols", (1024, 352)), ("ffn2_w_down", "rows", (352, 1024)),
]
REPLICATED = [n for n in WEIGHT_NAMES if n not in {s[0] for s in SHARDED}]
ROW_ALIGN = 16
ROW_BLOCKS = (512, 384, 352, 256, 192, 176, 128)
GROUPS = {"ffn1_gu": ("ffn1_w_gate", "ffn1_w_up"), "ffn1_d": ("ffn1_w_down",),
          "mixer": ("w_in", "conv_w", "w_dn_branch", "w_pool_branch", "w_mix_out"),
          "xa": ("xa_wq", "xa_wk", "xa_wv", "xa_wo"),
          "ffn2": ("ffn2_w_gate", "ffn2_w_up", "ffn2_w_down")}
GROUPS["mid"] = GROUPS["mixer"] + GROUPS["xa"]
W_IN_COLS = 577
W_IN_PIECES = (("in_qkv", 0, 1536), ("in_z", 1536, 2048), ("in_ab", 2048, 2056), ("in_p", 2056, 2568),
               ("in_gates", 2568, 4616))


def _round_up(n, m):
    return -(-n // m) * m


def _layout():
    off, table = 0, {}
    for name, form, shape in SHARDED:
        valid = {"rows": shape[0], "cols": shape[1], "flat": 2}[form]
        width = {"rows": shape[1], "cols": shape[0], "flat": shape[0] * shape[1]}[form]
        rows = _round_up(valid, ROW_ALIGN)
        table[name] = (off, rows, valid, width, form, shape)
        off += rows
    return table


LAYOUT = _layout()


def _group_span(names):
    base = LAYOUT[names[0]][0]
    rows = LAYOUT[names[-1]][0] + LAYOUT[names[-1]][1] - base
    while not any(rows % b == 0 for b in ROW_BLOCKS):
        rows += ROW_ALIGN
    return base, rows


def _row_block(rows):
    return _pick(rows, ROW_BLOCKS)


def _pad_block(blk, rows):
    return jnp.pad(blk, ((0, rows - blk.shape[0]), (0, LANES - blk.shape[1])))


def pack_weight_shards(shards, names):
    parts, used = [], 0
    for name in names:
        off, rows, valid, width, form, _ = LAYOUT[name]
        s = shards[name]
        if form == "flat":
            flat = s.reshape(1, -1)
            hi = flat.astype(BF16)
            blk = jnp.concatenate([hi, (flat - hi.astype(F32)).astype(BF16)], axis=0)
        else:
            blk = (s.T if form == "cols" else s).astype(BF16)
        parts.append(_pad_block(blk, rows))
        used += rows
    if _group_span(names)[1] > used:
        parts.append(jnp.zeros((_group_span(names)[1] - used, LANES), BF16))
    return jnp.concatenate(parts, axis=0)


def _w_in_rows(padded, rows, first, last):
    segs = []
    for k in range(N_DEV):
        lo, hi = max(first, k * W_IN_COLS), min(last, (k + 1) * W_IN_COLS)
        if lo < hi:
            segs.append(padded[k * rows + lo - k * W_IN_COLS:k * rows + hi - k * W_IN_COLS])
    return segs[0] if len(segs) == 1 else jnp.concatenate(segs, axis=0)


def unpack_full_weights(gathered, names):
    out, base = {}, _group_span(names)[0]
    for name in names:
        off, rows, valid, width, form, shape = LAYOUT[name]
        seg = gathered[:, off - base:off - base + rows]
        if form == "flat":
            flat = seg[:, 0, :width].astype(F32) + seg[:, 1, :width].astype(F32)
            out[name] = flat.reshape((N_DEV,) + shape).transpose(1, 0, 2).reshape(shape[0], N_DEV * shape[1])
        elif name == "w_in":
            padded = seg.reshape(N_DEV * rows, LANES)
            for piece, first, last in W_IN_PIECES:
                out[piece] = _w_in_rows(padded, rows, first, last)
        else:
            out[name] = seg[:, :valid, :width].reshape(N_DEV * valid, width)
    return out


def pack_full_grads(grads, names, me):
    wire, own, used = [], [], 0
    for name in names:
        off, rows, valid, width, form, shape = LAYOUT[name]
        if form == "flat":
            full = grads[name].reshape(shape[0], N_DEV, shape[1]).transpose(1, 0, 2).reshape(N_DEV, 1, width)
        elif name == "w_in":
            full = jnp.concatenate([grads[piece][:last - first] for piece, first, last in W_IN_PIECES], axis=0)
            full = full.reshape(N_DEV, valid, width)
        else:
            full = grads[name].reshape(N_DEV, valid, width)
        pad = ((0, rows - full.shape[1]), (0, LANES - width))
        wire.append(jnp.pad(full.astype(WIRE), ((0, 0),) + pad))
        own.append(jnp.pad(lax.dynamic_index_in_dim(full, me, 0, keepdims=False), pad))
        used += rows
    if _group_span(names)[1] > used:
        wire.append(jnp.zeros((N_DEV, _group_span(names)[1] - used, LANES), WIRE))
        own.append(jnp.zeros((_group_span(names)[1] - used, LANES), F32))
    return jnp.concatenate(wire, axis=1), jnp.concatenate(own, axis=0)


def unpack_grad_shards(packed, names):
    out, base = {}, _group_span(names)[0]
    for name in names:
        off, rows, valid, width, form, shape = LAYOUT[name]
        off -= base
        if form == "flat":
            out[name] = packed[off, :width].reshape(shape)
        elif form == "cols":
            out[name] = packed[off:off + valid, :width].T
        else:
            out[name] = packed[off:off + valid, :width]
    return out


SMALL_SHAPES = {n: (1024,) for n in REPLICATED}
SMALL_SHAPES.update(pool_w=(4, 128, 128), pool_scale=(512,), dn_norm_w=(128,), a_log=(4,), dt_bias=(4,))


SMALL_SHAPES["loss"] = (1,)
SMALL_NAMES = REPLICATED + ["loss"]


def _small_layout():
    off, table = 0, {}
    for name in SMALL_NAMES:
        numel = 1
        for d in SMALL_SHAPES[name]:
            numel *= d
        rows = _round_up(-(-numel // LANES), 8)
        table[name] = (off, rows, numel)
        off += rows
    return table, off


SMALL_LAYOUT, SMALL_ROWS = _small_layout()


def _to_rows(flat, rows):
    return jnp.pad(flat, (0, rows * LANES - flat.shape[0])).reshape(rows, LANES)


def pack_small(values):
    return jnp.concatenate([_to_rows(values[name].reshape(-1), SMALL_LAYOUT[name][1]) for name in SMALL_NAMES], axis=0)


def unpack_small(packed):
    out = {}
    for name in SMALL_NAMES:
        off, rows, numel = SMALL_LAYOUT[name]
        out[name] = packed[off:off + rows].reshape(-1)[:numel].reshape(SMALL_SHAPES[name])
    return out


MESH = pl.DeviceIdType.MESH


def _position():
    return lax.axis_index("x"), lax.axis_index("y"), lax.axis_index("c")


def _other_chips(x, y):
    return [(1 - x, y), (x, 1 - y), (1 - x, 1 - y)]


def all_gather(name, block):
    rows, n = block.shape

    def body(x_ref, out_ref, send_sems, recv_sems, local_sem):
        x, y, c = _position()
        me, sibling = (x, y, c), (x, y, 1 - c)
        chips = _other_chips(x, y)

        def slot(px, py, pc):
            return out_ref.at[4 * px + 2 * py + pc]

        def copy(k, blk, to, src=None):
            return pltpu.make_async_remote_copy(
                src_ref=slot(*blk) if src is None else src, dst_ref=slot(*blk),
                send_sem=send_sems.at[k], recv_sem=recv_sems.at[k], device_id=to, device_id_type=MESH)

        mine = pltpu.make_async_copy(x_ref, slot(*me), local_sem)
        mine.start()
        first = [copy(0, me, sibling, src=x_ref)]
        first += [copy(1 + j, me, (*chip, c), src=x_ref) for j, chip in enumerate(chips)]
        for cp in first:
            cp.start()
        passed = [copy(4 + j, (*chip, c), sibling) for j, chip in enumerate(chips)]
        for j, chip in enumerate(chips):
            copy(1 + j, (*chip, c), me).wait_recv()
            passed[j].start()
        copy(0, sibling, me).wait_recv()
        for j, chip in enumerate(chips):
            copy(4 + j, (*chip, 1 - c), me).wait_recv()
        for cp in first + passed:
            cp.wait_send()
        mine.wait()

    return pl.pallas_call(
        body, name=name, out_shape=jax.ShapeDtypeStruct((N_DEV, rows, n), block.dtype),
        in_specs=[ANY], out_specs=ANY,
        scratch_shapes=[pltpu.SemaphoreType.DMA((7,)), pltpu.SemaphoreType.DMA((7,)), pltpu.SemaphoreType.DMA(())],
    )(block)


HBM = pl.BlockSpec(memory_space=pltpu.HBM)
SEM = pl.BlockSpec(memory_space=pltpu.SEMAPHORE)
EFFECT = pltpu.SideEffectType.DATAFLOW_SIDE_EFFECTING


def _remote(src, dst, send_sem, recv_sem, to):
    return pltpu.make_async_remote_copy(src_ref=src, dst_ref=dst, send_sem=send_sem, recv_sem=recv_sem,
                                        device_id=to, device_id_type=MESH)


def split_start(name, bufs, n, make_copies):
    nb = len(bufs)

    def body(*refs):
        for out_cp, _ in make_copies(refs[:nb], refs[nb:nb + n], refs[nb + n:nb + 2 * n]):
            out_cp.start()
        refs[-1][...] = jnp.zeros_like(refs[-1])

    outs = pl.pallas_call(
        body, name=name,
        out_shape=tuple([pltpu.SemaphoreType.DMA(())] * (2 * n)) + tuple(pltpu.HBM(b.shape, b.dtype) for b in bufs)
        + (jax.ShapeDtypeStruct((8, 128), F32),),
        in_specs=[HBM] * nb,
        out_specs=tuple([SEM] * (2 * n) + [HBM] * nb + [pl.BlockSpec(memory_space=pltpu.VMEM)]),
        input_output_aliases={i: 2 * n + i for i in range(nb)},
        compiler_params=pltpu.CompilerParams(has_side_effects=EFFECT),
    )(*[pltpu.with_memory_space_constraint(b, pltpu.HBM) for b in bufs])
    return list(outs[:2 * n]), list(outs[2 * n:2 * n + nb]), outs[-1]


def split_wait(name, bufs, sems, n, make_copies, after):
    nb = len(bufs)

    def body(*refs):
        for out_cp, in_cp in make_copies(refs[:nb], refs[nb:nb + n], refs[nb + n:nb + 2 * n]):
            out_cp.wait_send()
            in_cp.wait_recv()

    outs = pl.pallas_call(
        body, name=name, out_shape=tuple(pltpu.HBM(b.shape, b.dtype) for b in bufs),
        in_specs=[HBM] * nb + [SEM] * (2 * n) + [ANY], out_specs=tuple([HBM] * nb),
        input_output_aliases={i: i for i in range(nb)},
        compiler_params=pltpu.CompilerParams(has_side_effects=EFFECT),
    )(*bufs, *sems, after)
    return list(outs)


def _gather_stage1(refs, send, recv):
    src, land = refs
    x, y, c = _position()
    peers = [(x, y, 1 - c)] + [(*chip, c) for chip in _other_chips(x, y)]
    return [(_remote(src, land.at[4 * x + 2 * y + c], send[k], recv[k], p),
             _remote(src, land.at[4 * p[0] + 2 * p[1] + p[2]], send[k], recv[k], p)) for k, p in enumerate(peers)]


def _gather_stage2(refs, send, recv):
    (land,) = refs
    x, y, c = _position()
    out = []
    for j, (px, py) in enumerate(_other_chips(x, y)):
        mine, theirs = land.at[4 * px + 2 * py + c], land.at[4 * px + 2 * py + 1 - c]
        out.append((_remote(mine, mine, send[j], recv[j], (x, y, 1 - c)),
                    _remote(theirs, theirs, send[j], recv[j], (x, y, 1 - c))))
    return out


def _flips():
    return [(a, b, d) for a in (0, 1) for b in (0, 1) for d in (0, 1) if a | b | d]


def _gather_direct(refs, send, recv):
    src, land = refs
    x, y, c = _position()
    out = []
    for k, (fx, fy, fc) in enumerate(_flips()):
        p = (1 - x if fx else x, 1 - y if fy else y, 1 - c if fc else c)
        out.append((_remote(src, land.at[4 * x + 2 * y + c], send[k], recv[k], p),
                    _remote(src, land.at[4 * p[0] + 2 * p[1] + p[2]], send[k], recv[k], p)))
    return out


def _scatter_direct(refs, send, recv):
    sendbuf, land = refs
    x, y, c = _position()
    me = 4 * x + 2 * y + c
    out = []
    for k, (fx, fy, fc) in enumerate(_flips()):
        p = (1 - x if fx else x, 1 - y if fy else y, 1 - c if fc else c)
        peer = 4 * p[0] + 2 * p[1] + p[2]
        out.append((_remote(sendbuf.at[peer], land.at[me], send[k], recv[k], p),
                    _remote(sendbuf.at[peer], land.at[peer], send[k], recv[k], p)))
    return out


def _own_plus_slots(name, own, landed):
    n, rows, _ = landed.shape
    tr = _row_block(rows)

    def body(g_ref, l_ref, o_ref):
        acc = g_ref[...]
        for j in range(n):
            acc = acc + l_ref[j].astype(F32)
        o_ref[...] = acc

    return pl.pallas_call(
        body, name=name, grid=(rows // tr,),
        in_specs=[pl.BlockSpec((tr, LANES), lambda i: (i, 0)), pl.BlockSpec((n, tr, LANES), lambda i: (0, i, 0))],
        out_specs=pl.BlockSpec((tr, LANES), lambda i: (i, 0)),
        out_shape=jax.ShapeDtypeStruct((rows, LANES), F32), compiler_params=_params(("parallel",)),
    )(own, landed)


def _sum_slots(name, stack):
    n, rows, _ = stack.shape

    def body(s_ref, o_ref):
        acc = s_ref[0]
        for j in range(1, n):
            acc = acc + s_ref[j]
        o_ref[...] = acc

    return pl.pallas_call(
        body, name=name, in_specs=[pl.BlockSpec(stack.shape, lambda: (0, 0, 0))],
        out_specs=pl.BlockSpec((rows, LANES), lambda: (0, 0)), out_shape=jax.ShapeDtypeStruct((rows, LANES), F32),
    )(stack)


def adamw(name, w, g, m, v):
    shape = w.shape
    last = shape[-1]
    w2, g2, m2, v2 = [a.reshape(-1, last) for a in (w, g, m, v)]
    rows = w2.shape[0]
    tr = 256 if rows % 256 == 0 else rows

    def body(w_ref, g_ref, m_ref, v_ref, d_ref, nm_ref, nv_ref):
        gg = g_ref[...]
        nm = ADAM_B1 * m_ref[...] + (1.0 - ADAM_B1) * gg
        nv = ADAM_B2 * v_ref[...] + (1.0 - ADAM_B2) * (gg * gg)
        m_hat = nm / (1.0 - ADAM_B1 ** ADAM_STEP)
        v_hat = nv / (1.0 - ADAM_B2 ** ADAM_STEP)
        d_ref[...] = -ADAM_LR * (m_hat / (jnp.sqrt(v_hat) + ADAM_EPS) + ADAM_WD * w_ref[...])
        nm_ref[...] = nm
        nv_ref[...] = nv

    spec = pl.BlockSpec((tr, last), lambda i: (i, 0))
    outs = pl.pallas_call(
        body, name=name, grid=(rows // tr,), in_specs=[spec] * 4, out_specs=[spec] * 3,
        out_shape=[jax.ShapeDtypeStruct((rows, last), F32)] * 3, compiler_params=_params(("parallel",)),
    )(w2, g2, m2, v2)
    return [o.reshape(shape) for o in outs]


def _landing(block_shape, dtype, own):
    x, y, c = _position()
    return lax.dynamic_update_slice(lax.empty((N_DEV,) + block_shape, dtype), own[None], (4 * x + 2 * y + c, 0, 0))


class _Exchanges:
    def __init__(self, shards):
        self.shards = shards
        self.pending = {}
        self.reduced = {}

    def first_weights(self):
        names = GROUPS["ffn1_gu"]
        return unpack_full_weights(all_gather("ag_ffn1_gu", pack_weight_shards(self.shards, names)), names)

    def rest_started(self):
        tokens = []
        block = pack_weight_shards(self.shards, GROUPS["ffn1_d"])
        sems, bufs, token = split_start("ag_ffn1_d_s", [block, _landing(block.shape, block.dtype, block)], N_DEV - 1,
                                        _gather_direct)
        self.pending["ffn1_d"] = (sems, bufs)
        tokens.append(token)
        for key in ("mid", "ffn2"):
            block = pack_weight_shards(self.shards, GROUPS[key])
            sems, bufs, token = split_start(f"ag_{key}_s1", [block, _landing(block.shape, block.dtype, block)], 4,
                                            _gather_stage1)
            self.pending[key] = (sems, bufs)
            tokens.append(token)
        return tuple(tokens)

    def ffn1_down(self, after):
        sems, bufs = self.pending.pop("ffn1_d")
        _, gathered = split_wait("ag_ffn1_d_w", bufs, sems, N_DEV - 1, _gather_direct, after)
        return unpack_full_weights(gathered, GROUPS["ffn1_d"])["ffn1_w_down"]

    def halfway(self, key, after):
        sems, bufs = self.pending.pop(key)
        _, land = split_wait(f"ag_{key}_w1", bufs, sems, 4, _gather_stage1, after)
        sems, bufs, token = split_start(f"ag_{key}_s2", [land], 3, _gather_stage2)
        self.pending[key] = (sems, bufs)
        return (token,)

    def weights(self, key, after):
        sems, bufs = self.pending.pop(key)
        (gathered,) = split_wait(f"ag_{key}_w2", bufs, sems, 3, _gather_stage2, after)
        w = unpack_full_weights(gathered, GROUPS[key])
        if "in_ab" in w:
            w["in_ab"] = jnp.pad(w["in_ab"], ((0, 128 - 2 * DN_HEADS), (0, 0)))
        return w

    def grads_out(self, key, grads):
        x, y, c = _position()
        wire, own = pack_full_grads(grads, GROUPS[key], 4 * x + 2 * y + c)
        land = _landing(wire.shape[1:], WIRE, jnp.zeros(wire.shape[1:], WIRE))
        sems, bufs, token = split_start(f"rs_{key}_start", [wire, land], N_DEV - 1, _scatter_direct)
        self.pending[key] = (sems, bufs, own)
        return (token,)

    def grads_in(self, key, after):
        sems, bufs, own = self.pending.pop(key)
        _, landed = split_wait(f"rs_{key}_wait", bufs, sems, N_DEV - 1, _scatter_direct, after)
        self.reduced.update(unpack_grad_shards(_own_plus_slots(f"rs_{key}_sum", own, landed), GROUPS[key]))

    def small_out(self, values):
        block = pack_small(values)
        sems, bufs, token = split_start("ag_small_s", [block, _landing(block.shape, block.dtype, block)], N_DEV - 1,
                                        _gather_direct)
        self.pending["small"] = (sems, bufs)
        return (token,)

    def small_in(self, after):
        sems, bufs = self.pending.pop("small")
        _, gathered = split_wait("ag_small_w", bufs, sems, N_DEV - 1, _gather_direct, after)
        return unpack_small(_sum_slots("small_sum", gathered))


def kernel(x, mem, ffn1_w_gate, ffn1_w_up, ffn1_w_down, ln1_g, ln1_b, w_in, conv_w, a_log, dt_bias, dn_norm_w, w_dn_branch, pool_w, pool_scale, w_pool_branch, w_mix_out, ln2_g, ln2_b, mem_ln_g, mem_ln_b, xa_wq, xa_wk, xa_wv, xa_wo, ln3_g, ln3_b, ffn2_w_gate, ffn2_w_up, ffn2_w_down, ln4_g, ln4_b, loss_target, m_ffn1_w_gate, m_ffn1_w_up, m_ffn1_w_down, m_ln1_g, m_ln1_b, m_w_in, m_conv_w, m_a_log, m_dt_bias, m_dn_norm_w, m_w_dn_branch, m_pool_w, m_pool_scale, m_w_pool_branch, m_w_mix_out, m_ln2_g, m_ln2_b, m_mem_ln_g, m_mem_ln_b, m_xa_wq, m_xa_wk, m_xa_wv, m_xa_wo, m_ln3_g, m_ln3_b, m_ffn2_w_gate, m_ffn2_w_up, m_ffn2_w_down, m_ln4_g, m_ln4_b, v_ffn1_w_gate, v_ffn1_w_up, v_ffn1_w_down, v_ln1_g, v_ln1_b, v_w_in, v_conv_w, v_a_log, v_dt_bias, v_dn_norm_w, v_w_dn_branch, v_pool_w, v_pool_scale, v_w_pool_branch, v_w_mix_out, v_ln2_g, v_ln2_b, v_mem_ln_g, v_mem_ln_b, v_xa_wq, v_xa_wk, v_xa_wv, v_xa_wo, v_ln3_g, v_ln3_b, v_ffn2_w_gate, v_ffn2_w_up, v_ffn2_w_down, v_ln4_g, v_ln4_b):
    given = dict(locals())
    shards = {n: given[n] for n in WEIGHT_NAMES}
    io = _Exchanges({n: shards[n][0] for n, _, _ in SHARDED})
    w = io.first_weights()
    for n in REPLICATED:
        w[n] = shards[n][0] if n == "pool_w" else shards[n]
    loss_part, grad_x, g = local_step(x[0], mem[0], loss_target[0], w, io)

    grad, updates = {}, {}

    def update(names, reduced):
        for n in names:
            grad[n] = reduced[n].reshape(shards[n].shape)
            updates[n] = adamw("adamw_" + n, shards[n], grad[n], given["m_" + n], given["v_" + n])
        return updates[names[-1]][0]

    update(GROUPS["ffn2"] + GROUPS["xa"], io.reduced)
    io.grads_in("mixer", grad_x)
    done = update(GROUPS["mixer"], io.reduced)
    small = io.small_in(done)
    loss = small.pop("loss")[0]
    done = update(REPLICATED, small)
    io.grads_in("ffn1_d", done)
    done = update(GROUPS["ffn1_d"], io.reduced)
    io.grads_in("ffn1_gu", done)
    update(GROUPS["ffn1_gu"], io.reduced)
    return (loss, grad_x[None], *[grad[n] for n in WEIGHT_NAMES], *[updates[n][0] for n in WEIGHT_NAMES],
            *[updates[n][1] for n in WEIGHT_NAMES], *[updates[n][2] for n in WEIGHT_NAMES])
```

```python
import functools

import jax
import jax.numpy as jnp
from jax import lax
from jax.experimental import pallas as pl
from jax.experimental.pallas import tpu as pltpu

F32 = jnp.float32
BF16 = jnp.bfloat16
MMD = BF16
WIRE = BF16
HI = lax.Precision.HIGHEST
X3 = lax.Precision.HIGH
VMEM_LIMIT_BYTES = 48 * 1024 * 1024

D_MODEL = 1024
D_FF = 2816
CHUNK = 64
N_MEM = 256
DN_HEADS = 4
HD = 128
DN_WIDTH = 512
POOL_WINDOWS = (2, 4, 8, 16)
POOL_WIDTH = 512
XA_HEADS = 4
XA_HD = 256
LN_EPS = 1e-5
RMS_EPS = 1e-6
L2_EPS = 1e-6
ALPHA = 2.0 ** 0.25
HALO = 16
ROWS = 512
ROWS_WIDE = 256

ADAM_LR = 0.001
ADAM_B1 = 0.9
ADAM_B2 = 0.999
ADAM_EPS = 1e-08
ADAM_WD = 0.01
ADAM_STEP = 10

N_DEV = 8
LANES = 1024
ANY = pl.BlockSpec(memory_space=pl.ANY)


def _dot(a, b, ca, cb, prec):
    dn = (((ca,), (cb,)), ((), ()))
    if prec is not None:
        return lax.dot_general(a.astype(F32), b.astype(F32), dn, precision=prec, preferred_element_type=F32)
    return lax.dot_general(a.astype(MMD), b.astype(MMD), dn, preferred_element_type=F32)


def dnn(a, b, prec=None):
    return _dot(a, b, 1, 0, prec)


def dnt(a, b, prec=None):
    return _dot(a, b, 1, 1, prec)


def dtn(a, b, prec=None):
    return _dot(a, b, 0, 0, prec)


def _sigmoid(x):
    return jax.nn.sigmoid(x)


def _silu(x):
    return x * _sigmoid(x)


def _dsilu(x):
    s = _sigmoid(x)
    return s * (1.0 + x * (1.0 - s))


def _softplus(x):
    return jnp.maximum(x, 0.0) + jnp.log1p(jnp.exp(-jnp.abs(x)))


def _iota(shape, dim):
    return lax.broadcasted_iota(jnp.int32, shape, dim)


def _rsum(x):
    return jnp.sum(x, axis=1, keepdims=True)


def _csum(x):
    return jnp.sum(x, axis=0, keepdims=True)


def _pick(n, cands):
    for c in cands:
        if n % c == 0:
            return c
    return n


def _params(sem):
    return pltpu.CompilerParams(dimension_semantics=sem, vmem_limit_bytes=VMEM_LIMIT_BYTES)


MM_TILE_SIZES = (4096, 2816, 2048, 1536, 1408, 1024, 768, 512, 384, 256, 128)
MM_VMEM_BUDGET = 36 * 1024 * 1024
HBM_BYTES_PER_US = 3.0e6
GRID_STEP_US = 0.35


def _mm_tiles(m, n, kc, a_bytes, b_bytes, o_bytes):
    def sizes(d):
        return [d] if d <= 512 else [t for t in MM_TILE_SIZES if d % t == 0]

    best = None
    for tm in sizes(m):
        for tn in sizes(n):
            for tk in sizes(kc):
                vmem = 2 * (tm * tk * a_bytes + tk * tn * b_bytes + tm * tn * o_bytes) + tm * tn * 4
                if vmem > MM_VMEM_BUDGET:
                    continue
                steps = (m // tm) * (n // tn) * (kc // tk)
                traffic = m * kc * a_bytes * (n // tn) + kc * n * b_bytes * (m // tm) + m * n * o_bytes
                edge = tm * tk * a_bytes + tk * tn * b_bytes + tm * tn * o_bytes
                cost = (traffic + edge) / HBM_BYTES_PER_US + steps * GRID_STEP_US
                if best is None or cost < best[0]:
                    best = (cost, tm, tn, tk)
    return best[1:]


def mm(name, a, b, *, ta=False, tb=False, out_dtype=F32, add=None, scale=None, deps=()):
    adds = [] if add is None else (list(add) if isinstance(add, (list, tuple)) else [(1.0, add)])
    if ta:
        kc, m = a.shape
    else:
        m, kc = a.shape
    if tb:
        n, kb = b.shape
    else:
        kb, n = b.shape
    assert kc == kb, (name, a.shape, b.shape)
    tm, tn, tk = _mm_tiles(m, n, kc, a.dtype.itemsize, b.dtype.itemsize,
                           jnp.dtype(out_dtype).itemsize * (1 + len(adds)))
    nk = kc // tk
    grid = (m // tm, n // tn, nk)
    a_spec = pl.BlockSpec((tk, tm), lambda i, j, k: (k, i)) if ta else pl.BlockSpec((tm, tk), lambda i, j, k: (i, k))
    b_spec = pl.BlockSpec((tn, tk), lambda i, j, k: (j, k)) if tb else pl.BlockSpec((tk, tn), lambda i, j, k: (k, j))
    o_spec = pl.BlockSpec((tm, tn), lambda i, j, k: (i, j))
    ca, cb = (0 if ta else 1), (1 if tb else 0)

    def body(*refs):
        a_ref, b_ref = refs[0], refs[1]
        o_ref = refs[-1] if nk == 1 else refs[-2]
        k = pl.program_id(2)
        part = _dot(a_ref[...], b_ref[...], ca, cb, None)

        def finish(r):
            if scale is not None:
                r = r * scale
            for (coef, _), add_ref in zip(adds, refs[2:2 + len(adds)]):
                r = r + (add_ref[...] if coef == 1.0 else coef * add_ref[...])
            o_ref[...] = r.astype(o_ref.dtype)

        if nk == 1:
            finish(part)
            return
        acc_ref = refs[-1]

        @pl.when(k == 0)
        def _():
            acc_ref[...] = part

        if nk > 2:
            @pl.when((k > 0) & (k < nk - 1))
            def _():
                acc_ref[...] += part

        @pl.when(k == nk - 1)
        def _():
            finish(acc_ref[...] + part)

    ins = [a, b] + [t for _, t in adds] + list(deps)
    specs = [a_spec, b_spec] + [o_spec] * len(adds) + [ANY] * len(deps)
    return pl.pallas_call(
        body, name=name, grid=grid, in_specs=specs, out_specs=o_spec,
        out_shape=jax.ShapeDtypeStruct((m, n), out_dtype),
        scratch_shapes=[pltpu.VMEM((tm, tn), F32)] if nk > 1 else [],
        compiler_params=_params(("parallel", "parallel", "arbitrary")),
    )(*ins)


class _Ctx:
    def __init__(self, i, nblk, tl):
        self.i, self.nblk, self.tl = i, nblk, tl


def _norm_item(it):
    if isinstance(it, tuple):
        a, w, j = it[:3]
        rows = it[3] if len(it) > 3 else None
        return a, w, j, rows
    return it, it.shape[-1], 0, None


def rowwise(name, fn, length, tl, *, rows=(), consts=(), prevs=(), nexts=(), out_rows=(), out_accs=(), deps=()):
    nblk = length // tl
    hb = tl // HALO
    nhalo = length // HALO
    arrays, specs = [], []
    for it in rows:
        a, w, j, r = _norm_item(it)
        if a.ndim == 3:
            specs.append(pl.BlockSpec((a.shape[0], tl, w), lambda i, j=j: (0, i, j)))
        else:
            specs.append(pl.BlockSpec((r or tl, w), lambda i, j=j: (i, j)))
        arrays.append(a)
    for a in consts:
        specs.append(pl.BlockSpec(a.shape, lambda i, nd=a.ndim: (0,) * nd))
        arrays.append(a)
    for it in prevs:
        a, w, j, _ = _norm_item(it)
        specs.append(pl.BlockSpec((HALO, w), lambda i, j=j: (jnp.maximum(i * hb - 1, 0), j)))
        arrays.append(a)
    for it in nexts:
        a, w, j, _ = _norm_item(it)
        specs.append(pl.BlockSpec((HALO, w), lambda i, j=j: (jnp.minimum((i + 1) * hb, nhalo - 1), j)))
        arrays.append(a)
    out_shape, out_specs = [], []
    for spec in out_rows:
        if len(spec) == 3:
            h, w, dt = spec
            out_shape.append(jax.ShapeDtypeStruct((h, length, w), dt))
            out_specs.append(pl.BlockSpec((h, tl, w), lambda i: (0, i, 0)))
        else:
            w, dt = spec
            out_shape.append(jax.ShapeDtypeStruct((length, w), dt))
            out_specs.append(pl.BlockSpec((tl, w), lambda i: (i, 0)))
    for shape, dt in out_accs:
        out_shape.append(jax.ShapeDtypeStruct(shape, dt))
        out_specs.append(pl.BlockSpec(shape, lambda i, nd=len(shape): (0,) * nd))
    n_r, n_c, n_p, n_n = len(rows), len(consts), len(prevs), len(nexts)
    n_in = n_r + n_c + n_p + n_n
    n_or = len(out_rows)
    arrays, specs = arrays + list(deps), specs + [ANY] * len(deps)

    def body(*refs):
        i = pl.program_id(0)
        vals = [r[...] for r in refs[:n_in]]
        outs = refs[n_in + len(deps):]
        ctx = _Ctx(i, nblk, tl)
        ro, ao = fn(ctx, vals[:n_r], vals[n_r:n_r + n_c], vals[n_r + n_c:n_r + n_c + n_p], vals[n_r + n_c + n_p:])
        for r, v in zip(outs[:n_or], ro, strict=True):
            r[...] = v.astype(r.dtype)
        for r, v in zip(outs[n_or:], ao, strict=True):
            @pl.when(i == 0)
            def _(r=r, v=v):
                r[...] = v.astype(r.dtype)

            @pl.when(i > 0)
            def _(r=r, v=v):
                r[...] += v.astype(r.dtype)

    res = pl.pallas_call(
        body, name=name, grid=(nblk,), in_specs=specs, out_specs=out_specs, out_shape=out_shape,
        compiler_params=_params(("arbitrary",) if out_accs else ("parallel",)),
    )(*arrays)
    return res


def _heads(x, n, w):
    return [x[:, h * w:(h + 1) * w] for h in range(n)]


def _cat(xs):
    return jnp.concatenate(xs, axis=1)


def _row_index(ctx, nrows, offset=0):
    return ctx.i * ctx.tl + offset + _iota((nrows, 1), 0)


def _ln_stats(r):
    mu = jnp.mean(r, axis=1, keepdims=True)
    d = r - mu
    var = jnp.mean(d * d, axis=1, keepdims=True)
    rstd = lax.rsqrt(var + LN_EPS)
    return d * rstd, rstd


def ln_fwd(name, terms, g, b, tl=ROWS, deps=()):
    coefs = [c for c, _ in terms]
    length = terms[0][1].shape[0]

    def fn(ctx, rows, consts, prevs, nexts):
        r = sum(c * t for c, t in zip(coefs, rows))
        xh, _ = _ln_stats(r)
        return [xh * consts[0] + consts[1], r], []

    return rowwise(name, fn, length, min(tl, length), rows=[t for _, t in terms], consts=[g, b],
                   out_rows=[(D_MODEL, F32), (D_MODEL, F32)], deps=deps)


def ln_bwd(name, r, terms, g, tl=ROWS, deps=()):
    coefs = [c for c, _ in terms]
    length = r.shape[0]

    def fn(ctx, rows, consts, prevs, nexts):
        xh, rstd = _ln_stats(rows[0])
        dy = sum(c * t for c, t in zip(coefs, rows[1:]))
        dxh = dy * consts[0]
        dr = rstd * (dxh - jnp.mean(dxh, axis=1, keepdims=True) - xh * jnp.mean(dxh * xh, axis=1, keepdims=True))
        return [dr], [_csum(dy * xh), _csum(dy)]

    return rowwise(name, fn, length, min(tl, length), rows=[r] + [t for _, t in terms], consts=[g],
                   out_rows=[(D_MODEL, F32)], out_accs=[((1, D_MODEL), F32), ((1, D_MODEL), F32)], deps=deps)


def ln_loss(name, terms, g, b, target, tl=ROWS):
    coefs = [c for c, _ in terms]
    length = target.shape[0]
    nt = len(terms)

    def fn(ctx, rows, consts, prevs, nexts):
        r = sum(c * t for c, t in zip(coefs, rows[:nt]))
        xh, _ = _ln_stats(r)
        err = xh * consts[0] + consts[1] - rows[nt]
        tot = _csum(_rsum(err * err)) * (0.5 / D_MODEL)
        return [err * (1.0 / D_MODEL), r], [jnp.broadcast_to(tot, (1, 128))]

    return rowwise(name, fn, length, min(tl, length), rows=[t for _, t in terms] + [target], consts=[g, b],
                   out_rows=[(D_MODEL, F32), (D_MODEL, F32)], out_accs=[((1, 128), F32)])


def _ffn_blocks(length):
    return min(512, length), D_FF // 2


def ffn_gate_up_act(name, x, wg, wu, deps=()):
    length = x.shape[0]
    tm, tn = _ffn_blocks(length)

    def body(x_ref, wg_ref, wu_ref, *rest):
        hg_ref, hu_ref, act_ref = rest[-3:]
        xb = x_ref[...].astype(MMD)
        hg = dnt(xb, wg_ref[...])
        hu = dnt(xb, wu_ref[...])
        hg_ref[...] = hg
        hu_ref[...] = hu
        act_ref[...] = (_silu(hg) * hu).astype(act_ref.dtype)

    row = pl.BlockSpec((tm, D_MODEL), lambda i, j: (i, 0))
    wsp = pl.BlockSpec((tn, D_MODEL), lambda i, j: (j, 0))
    osp = pl.BlockSpec((tm, tn), lambda i, j: (i, j))
    return pl.pallas_call(
        body, name=name, grid=(length // tm, D_FF // tn), in_specs=[row, wsp, wsp] + [ANY] * len(deps),
        out_specs=[osp] * 3,
        out_shape=[jax.ShapeDtypeStruct((length, D_FF), F32)] * 2 + [jax.ShapeDtypeStruct((length, D_FF), BF16)],
        compiler_params=_params(("parallel", "parallel")),
    )(x, wg, wu, *deps)


def ffn_dact(name, dr, wd, hg, hu, deps=()):
    length = dr.shape[0]
    tm, tn = _ffn_blocks(length)

    def body(dr_ref, wd_ref, hg_ref, hu_ref, *rest):
        dhg_ref, dhu_ref = rest[-2:]
        da = 0.5 * dnt(dr_ref[...], wd_ref[...])
        g = hg_ref[...]
        s = _sigmoid(g)
        dhg_ref[...] = (da * hu_ref[...] * (s * (1.0 + g * (1.0 - s)))).astype(dhg_ref.dtype)
        dhu_ref[...] = (da * (g * s)).astype(dhu_ref.dtype)

    row = pl.BlockSpec((tm, D_MODEL), lambda i, j: (i, 0))
    wsp = pl.BlockSpec((tn, D_MODEL), lambda i, j: (j, 0))
    osp = pl.BlockSpec((tm, tn), lambda i, j: (i, j))
    return pl.pallas_call(
        body, name=name, grid=(length // tm, D_FF // tn), in_specs=[row, wsp, osp, osp] + [ANY] * len(deps),
        out_specs=[osp] * 2, out_shape=[jax.ShapeDtypeStruct((length, D_FF), BF16)] * 2,
        compiler_params=_params(("parallel", "parallel")),
    )(dr, wd, hg, hu, *deps)


def ffn_fwd(tag, x, wg, wu, wd, deps=()):
    hg, hu, act = ffn_gate_up_act(tag + "_gate_up", x, wg, wu, deps)
    if callable(wd):
        wd = wd(act)
    f = mm(tag + "_down", act, wd)
    return f, (hg, hu, act), wd


def ffn_bwd(tag, x, res, dr, wg, wu, wd, deps=(), on_dwd=None, on_dwgu=None, also=None):
    hg, hu, act = res
    dwd = mm(tag + "_dwd", act, dr, ta=True, scale=0.5, deps=deps)
    dhg, dhu = ffn_dact(tag + "_dact", dr, wd, hg, hu, deps=on_dwd(dwd) if on_dwd else ())
    dwg = mm(tag + "_dwg", dhg, x, ta=True)
    dwu = mm(tag + "_dwu", dhu, x, ta=True)
    dx = mm(tag + "_dxg", dhg, wg, deps=on_dwgu(dwg, dwu) if on_dwgu else ())
    dx = mm(tag + "_dxu", dhu, wu, add=[(1.0, dx)] + ([also] if also else []))
    return dx, dwg, dwu, dwd


def _conv_taps(ext, taps, n):
    out = taps[3] * ext
    for j in range(3):
        out = out + taps[j] * pltpu.roll(ext, 3 - j, 0)
    return out


def _l2n(x):
    r = lax.rsqrt(_rsum(x * x) + L2_EPS)
    return x * r, r


def conv_fwd(name, pre, taps, tl=ROWS_WIDE, deps=()):
    length = pre.shape[0]
    tl = min(tl, length)

    def fn(ctx, rows, consts, prevs, nexts):
        prev = jnp.where(ctx.i > 0, prevs[0], 0.0)
        ext = jnp.concatenate([prev, rows[0]], axis=0)
        s = _silu(_conv_taps(ext, consts, tl + HALO)[HALO:])
        q = _cat([_l2n(x)[0] * (HD ** -0.5) for x in _heads(s[:, :DN_WIDTH], DN_HEADS, HD)])
        k = _cat([_l2n(x)[0] for x in _heads(s[:, DN_WIDTH:2 * DN_WIDTH], DN_HEADS, HD)])
        return [q, k, s[:, 2 * DN_WIDTH:]], []

    return rowwise(name, fn, length, tl, rows=[pre], consts=list(taps), prevs=[pre],
                   out_rows=[(DN_WIDTH, F32)] * 3, deps=deps)


def conv_bwd(name, pre, dq, dk, dv, taps, tl=ROWS_WIDE):
    length = pre.shape[0]
    tl = min(tl, length)
    n = tl + 2 * HALO

    def fn(ctx, rows, consts, prevs, nexts):
        last = ctx.i == ctx.nblk - 1
        prev = jnp.where(ctx.i > 0, prevs[0], 0.0)
        ext = jnp.concatenate([prev, rows[0], nexts[0]], axis=0)
        c = _conv_taps(ext, consts, n)
        sg = _sigmoid(c)
        s = c * sg
        zero = jnp.zeros((HALO, DN_WIDTH), F32)
        dqe, dke, dve = [jnp.concatenate([zero, rows[1 + t], jnp.where(last, 0.0, nexts[1 + t])], axis=0)
                         for t in range(3)]

        def l2_bwd(x, dy):
            y, r = _l2n(x)
            return r * (dy - y * _rsum(dy * y))

        dsq = _cat([l2_bwd(x, d * (HD ** -0.5)) for x, d in zip(_heads(s[:, :DN_WIDTH], DN_HEADS, HD),
                                                                 _heads(dqe, DN_HEADS, HD))])
        dsk = _cat([l2_bwd(x, d) for x, d in zip(_heads(s[:, DN_WIDTH:2 * DN_WIDTH], DN_HEADS, HD),
                                                  _heads(dke, DN_HEADS, HD))])
        dc = _cat([dsq, dsk, dve]) * (sg * (1.0 + c * (1.0 - sg)))
        dpre = consts[3] * dc
        for j in range(3):
            dpre = dpre + consts[j] * pltpu.roll(dc, n - (3 - j), 0)
        dc_cur = dc[HALO:HALO + tl]
        dws = [_csum(dc_cur * pltpu.roll(ext, 3 - j, 0)[HALO:HALO + tl]) for j in range(3)]
        dws.append(_csum(dc_cur * ext[HALO:HALO + tl]))
        return [dpre[HALO:HALO + tl]], dws

    return rowwise(name, fn, length, tl, rows=[pre, dq, dk, dv], consts=list(taps), prevs=[pre],
                   nexts=[pre, dq, dk, dv], out_rows=[(3 * DN_WIDTH, BF16)],
                   out_accs=[((1, 3 * DN_WIDTH), F32)] * 4)


def _gate_consts():
    lane = jnp.arange(128)[:, None]
    col = jnp.arange(2 * DN_WIDTH)[None, :]
    sel = ((lane < 2 * DN_HEADS) & (col // HD == lane)).astype(F32)
    pick = ((col.T == lane.T * HD) & (lane.T < 2 * DN_HEADS)).astype(F32)
    return sel, pick


def _gate_math(ab, alog, dtb):
    z = ab + dtb
    g = -jnp.exp(alog) * _softplus(z)
    beta = _sigmoid(ab)
    return z, g, beta


def gates_fwd(name, ab, alog, dtb, sel, tl=ROWS):
    length = ab.shape[0]

    def fn(ctx, rows, consts, prevs, nexts):
        _, g, beta = _gate_math(rows[0], consts[0], consts[1])
        lane = _iota(g.shape, 1)
        small = jnp.where(lane < DN_HEADS, g, jnp.where(lane < 2 * DN_HEADS, beta, 0.0))
        big = dnn(small, consts[2], HI)
        return [big[:, :DN_WIDTH], big[:, DN_WIDTH:]], []

    return rowwise(name, fn, length, min(tl, length), rows=[ab], consts=[alog, dtb, sel],
                   out_rows=[(DN_WIDTH, F32)] * 2)


def gates_bwd(name, ab, dgb, dbb, alog, dtb, pick, tl=ROWS):
    length = ab.shape[0]

    def fn(ctx, rows, consts, prevs, nexts):
        z, g, beta = _gate_math(rows[0], consts[0], consts[1])
        dsmall = dnn(_cat([rows[1], rows[2]]), consts[2], HI)
        lane = _iota(g.shape, 1)
        is_a = lane < DN_HEADS
        da = jnp.where(is_a, dsmall * (-jnp.exp(consts[0])) * _sigmoid(z), 0.0)
        db = jnp.where((lane >= DN_HEADS) & (lane < 2 * DN_HEADS), dsmall * beta * (1.0 - beta), 0.0)
        return [da + db], [_csum(jnp.where(is_a, dsmall * g, 0.0)), _csum(da)]

    return rowwise(name, fn, length, min(tl, length), rows=[ab, dgb, dbb], consts=[alog, dtb, pick],
                   out_rows=[(128, BF16)], out_accs=[((1, 128), F32)] * 2)


CPS = 2


def _chunk_scan_rows(x, suffix=False):
    n = x.shape[0]
    rc = _iota(x.shape, 0) & (CHUNK - 1)
    sh = 1
    while sh < CHUNK:
        if suffix:
            x = x + jnp.where(rc < CHUNK - sh, pltpu.roll(x, n - sh, 0), 0.0)
        else:
            x = x + jnp.where(rc >= sh, pltpu.roll(x, sh, 0), 0.0)
        sh *= 2
    return x


def _tri_inv(a_list, eye, bd):
    def each(f, *ls):
        return [f(*xs) for xs in zip(*ls)]

    dg = [jnp.where(bd, a, 0.0) for a in a_list]
    lo = each(lambda a, d: a - d, a_list, dg)
    n1 = [-d for d in dg]
    n2 = each(lambda n: dnn(n, n, X3), n1)
    n4 = each(lambda n: dnn(n, n, X3), n2)
    td = each(lambda p, s: dnn(eye + p, eye + s, X3), n1, n2)
    n8 = each(lambda n: dnn(n, n, X3), n4)
    td = each(lambda t, n: dnn(t, eye + n, X3), td, n4)
    td = each(lambda t, n: dnn(t, eye + n, X3), td, n8)
    m = each(lambda t, l: dnn(t, l, X3), td, lo)
    m2 = each(lambda x: dnn(x, x, X3), m)
    x = each(lambda p, s: dnn(eye - p, eye + s, X3), m, m2)
    return each(lambda p, t: dnn(p, t, X3), x, td)


def _chunk_common(q, k, v, gcb, bb):
    egb = jnp.exp(gcb)
    gc64 = gcb[:, :CHUNK]
    ii, jj = _iota((CHUNK, CHUNK), 0), _iota((CHUNK, CHUNK), 1)
    incl, strict = ii >= jj, ii > jj
    decay = jnp.exp(jnp.where(incl, gc64 - gc64.T, -jnp.inf))
    kb = k * bb
    vb = v * bb
    kbe = kb * egb
    pq = dnt(jnp.concatenate([kb, q], axis=0), k, X3)
    ekb = jnp.exp(gcb[CHUNK - 1:CHUNK, :] - gcb)
    return dict(egb=egb, decay=decay, kb=kb, vb=vb, kbe=kbe, pm=pq[:CHUNK], qm=pq[CHUNK:], ekb=ekb,
                incl=incl, strict=strict, ii=ii, jj=jj)


def _chunk_head(vals, ci, h):
    return [v[ci * CHUNK:(ci + 1) * CHUNK, h * HD:(h + 1) * HD] for v in vals]


def _assemble(per_chunk):
    return jnp.concatenate([_cat(hs) for hs in per_chunk], axis=0)


def _assemble3(per_chunk):
    return jnp.stack([jnp.concatenate([per_chunk[ci][h] for ci in range(CPS)], axis=0) for h in range(DN_HEADS)])


def delta_prep_fwd(name, q, k, v, gb, bb):
    length = q.shape[0]

    def fn(ctx, rows, consts, prevs, nexts):
        gcb_all = _chunk_scan_rows(rows[3])
        vals = [rows[0], rows[1], rows[2], gcb_all, rows[4]]
        units = [(ci, h) for ci in range(CPS) for h in range(DN_HEADS)]
        ins = [_chunk_head(vals, ci, h) for ci, h in units]
        cs = [_chunk_common(*i) for i in ins]
        eye = (cs[0]["ii"] == cs[0]["jj"]).astype(F32)
        ts = _tri_inv([jnp.where(c["strict"], c["pm"] * c["decay"], 0.0) for c in cs], eye,
                      (cs[0]["ii"] >> 4) == (cs[0]["jj"] >> 4))
        uws = [dnn(t, _cat([c["vb"], c["kbe"]]), X3) for t, c in zip(ts, cs)]

        def grid2(xs):
            return [xs[ci * DN_HEADS:(ci + 1) * DN_HEADS] for ci in range(CPS)]

        return [_assemble(grid2([uw[:, :HD] for uw in uws])), _assemble(grid2([uw[:, HD:] for uw in uws])),
                _assemble(grid2([i[0] * c["egb"] for i, c in zip(ins, cs)])),
                _assemble(grid2([i[1] * c["ekb"] for i, c in zip(ins, cs)])), gcb_all,
                _assemble3(grid2([c["qm"] * c["decay"] for c in cs])), _assemble3(grid2(ts))], []

    return rowwise(name, fn, length, CHUNK * CPS, rows=[q, k, v, gb, bb],
                   out_rows=[(DN_WIDTH, F32)] * 5 + [(DN_HEADS, CHUNK, F32)] * 2)


def delta_prep_bwd(name, q, k, v, gb, bb, t3, du, dw, dqd, dkd, dattn3, dgl):
    length = q.shape[0]

    def fn(ctx, rows, consts, prevs, nexts):
        gcb_all = _chunk_scan_rows(rows[3])
        vals = [rows[0], rows[1], rows[2], gcb_all] + list(rows[4:9])
        t3v, da3v, dglv = rows[9], rows[10], rows[11]
        units = [(ci, h) for ci in range(CPS) for h in range(DN_HEADS)]
        ins = [_chunk_head(vals, ci, h) for ci, h in units]
        cs = [_chunk_common(*i[:5]) for i in ins]
        ts = [t3v[h][ci * CHUNK:(ci + 1) * CHUNK] for ci, h in units]
        dattns = [jnp.where(c["incl"], da3v[h][ci * CHUNK:(ci + 1) * CHUNK], 0.0) for (ci, h), c in zip(units, cs)]
        duws = [_cat([i[5], i[6]]) for i in ins]
        dvks = [dtn(t, d, X3) for t, d in zip(ts, duws)]
        dts = [dnt(d, _cat([c["vb"], c["kbe"]]), X3) for d, c in zip(duws, cs)]
        dts = [dnt(d, t, X3) for d, t in zip(dts, ts)]
        das = [jnp.where(c["strict"], -dtn(t, d, X3), 0.0) for c, t, d in zip(cs, ts, dts)]
        dpqs = [jnp.concatenate([da * c["decay"], dat * c["decay"]], axis=0) for da, dat, c in zip(das, dattns, cs)]
        dpqks = [dnn(d, i[1], X3) for d, i in zip(dpqs, ins)]
        dkps = [dtn(d, jnp.concatenate([c["kb"], i[0]], axis=0), X3) for d, c, i in zip(dpqs, cs, ins)]
        dqs, dks, dvs, dgcs, dbs = [], [], [], [], []
        for (ci, h), i, c, dvk, da, dattn, dpqk, dkp in zip(units, ins, cs, dvks, das, dattns, dpqks, dkps):
            qh, kh, vh, _, bh, _, _, dqdh, dkdh = i
            dvb, dkbe = dvk[:, :HD], dvk[:, HD:]
            dkb = dpqk[:CHUNK] + dkbe * c["egb"]
            c1 = _rsum(dkbe * c["kb"] + dqdh * qh) * c["egb"]
            c2 = _rsum(dkdh * kh) * c["ekb"]
            e = (da * c["pm"] + dattn * c["qm"]) * c["decay"]
            dgc = c1 - c2 + _rsum(e) - _rsum(e.T)
            dgl_tot = jnp.max(dglv[ci * 8:(ci + 1) * 8, h * HD:(h + 1) * HD], axis=0, keepdims=True) + _csum(c2)
            dgcs.append(dgc + jnp.where(_iota((CHUNK, HD), 0) == CHUNK - 1, dgl_tot, 0.0))
            dqs.append(dpqk[CHUNK:] + dqdh * c["egb"])
            dks.append(dkp + dkdh * c["ekb"] + dkb * bh)
            dvs.append(dvb * bh)
            dbs.append(jnp.broadcast_to(_rsum(dkb * kh + dvb * vh), (CHUNK, HD)))

        def grid2(xs):
            return [xs[ci * DN_HEADS:(ci + 1) * DN_HEADS] for ci in range(CPS)]

        return [_assemble(grid2(dqs)), _assemble(grid2(dks)), _assemble(grid2(dvs)),
                _chunk_scan_rows(_assemble(grid2(dgcs)), suffix=True), _assemble(grid2(dbs))], []

    return rowwise(name, fn, length, CHUNK * CPS,
                   rows=[q, k, v, gb, bb, du, dw, dqd, dkd, t3, dattn3, (dgl, DN_WIDTH, 0, 8 * CPS)],
                   out_rows=[(DN_WIDTH, F32)] * 5)


SCAN_CHUNKS = 4


def _scan_chunks(n):
    return SCAN_CHUNKS if n % SCAN_CHUNKS == 0 else 1


def delta_scan_fwd(name, qd, kd, u, w, attn3, gcb):
    length = qd.shape[0]
    n = length // CHUNK
    sc = _scan_chunks(n)
    row = pl.BlockSpec((sc * CHUNK, DN_WIDTH), lambda c: (c, 0))
    sq = pl.BlockSpec((DN_HEADS, sc * CHUNK, CHUNK), lambda c: (0, c, 0))

    def body(qd_ref, kd_ref, u_ref, w_ref, attn_ref, gc_ref, o_ref, vn_ref, st_ref, s_ref):
        c = pl.program_id(0)

        @pl.when(c == 0)
        def _():
            s_ref[...] = jnp.zeros_like(s_ref)

        heads = range(DN_HEADS)
        sls = [pl.ds(h * HD, HD) for h in heads]
        ss = [s_ref[h] for h in heads]
        for ci in range(sc):
            rs = pl.ds(ci * CHUNK, CHUNK)
            ws = [dnn(w_ref[rs, sl], s) for sl, s in zip(sls, ss)]
            qs = [dnn(qd_ref[rs, sl], s) for sl, s in zip(sls, ss)]
            vns = [u_ref[rs, sl] - x for sl, x in zip(sls, ws)]
            avs = [dnn(attn_ref[h, rs, :], vn) for h, vn in zip(heads, vns)]
            kvs = [dtn(kd_ref[rs, sl], vn) for sl, vn in zip(sls, vns)]
            for h, sl in zip(heads, sls):
                st_ref[ci, h] = ss[h]
                o_ref[rs, sl] = qs[h] + avs[h]
                vn_ref[rs, sl] = vns[h]
            ss = [s * jnp.exp(gc_ref[pl.ds(ci * CHUNK + CHUNK - 1, 1), sl]) + kv for s, sl, kv in zip(ss, sls, kvs)]
        for h in heads:
            s_ref[h] = ss[h]

    return pl.pallas_call(
        body, name=name, grid=(n // sc,), in_specs=[row, row, row, row, sq, row],
        out_specs=[row, row, pl.BlockSpec((sc, DN_HEADS, HD, HD), lambda c: (c, 0, 0, 0))],
        out_shape=[jax.ShapeDtypeStruct((length, DN_WIDTH), F32), jax.ShapeDtypeStruct((length, DN_WIDTH), F32),
                   jax.ShapeDtypeStruct((n, DN_HEADS, HD, HD), F32)],
        scratch_shapes=[pltpu.VMEM((DN_HEADS, HD, HD), F32)],
        compiler_params=_params(("arbitrary",)),
    )(qd, kd, u, w, attn3, gcb)


def delta_scan_bwd(name, do, qd, kd, w, attn3, vn, st, gcb):
    length = qd.shape[0]
    n = length // CHUNK
    sc = _scan_chunks(n)
    nb = n // sc
    row = pl.BlockSpec((sc * CHUNK, DN_WIDTH), lambda c: (nb - 1 - c, 0))
    sq = pl.BlockSpec((DN_HEADS, sc * CHUNK, CHUNK), lambda c: (0, nb - 1 - c, 0))
    stb = pl.BlockSpec((sc, DN_HEADS, HD, HD), lambda c: (nb - 1 - c, 0, 0, 0))
    glb = pl.BlockSpec((sc * 8, DN_WIDTH), lambda c: (nb - 1 - c, 0))

    def body(do_ref, qd_ref, kd_ref, w_ref, attn_ref, vn_ref, st_ref, gc_ref,
             dqd_ref, dkd_ref, du_ref, dw_ref, dattn_ref, dgl_ref, ds_ref):
        c = pl.program_id(0)

        @pl.when(c == 0)
        def _():
            ds_ref[...] = jnp.zeros_like(ds_ref)

        heads = range(DN_HEADS)
        sls = [pl.ds(h * HD, HD) for h in heads]
        dsns = [ds_ref[h] for h in heads]
        for ci in reversed(range(sc)):
            rs = pl.ds(ci * CHUNK, CHUNK)
            ss = [st_ref[ci, h] for h in heads]
            dos = [do_ref[rs, sl] for sl in sls]
            vns = [vn_ref[rs, sl] for sl in sls]
            dvns = [dtn(attn_ref[h, rs, :], d) for h, d in zip(heads, dos)]
            dvns = [x + dnn(kd_ref[rs, sl], dsn) for x, sl, dsn in zip(dvns, sls, dsns)]
            qdos = [dtn(qd_ref[rs, sl], d) for sl, d in zip(sls, dos)]
            for h, sl in zip(heads, sls):
                dattn_ref[h, rs, :] = dnt(dos[h], vns[h])
                dqd_ref[rs, sl] = dnt(dos[h], ss[h])
                dkd_ref[rs, sl] = dnt(vns[h], dsns[h])
                du_ref[rs, sl] = dvns[h]
            dws = [dnt(dvn, s) for dvn, s in zip(dvns, ss)]
            wdvs = [dtn(w_ref[rs, sl], dvn) for sl, dvn in zip(sls, dvns)]
            nxt = []
            for h, sl in zip(heads, sls):
                egl = jnp.exp(gc_ref[pl.ds(ci * CHUNK + CHUNK - 1, 1), sl])
                dw_ref[rs, sl] = -dws[h]
                dgl_ref[pl.ds(ci * 8, 8), sl] = jnp.broadcast_to(_csum(_rsum(dsns[h] * ss[h])) * egl, (8, HD))
                nxt.append(dsns[h] * egl + qdos[h] - wdvs[h])
            dsns = nxt
        for h in heads:
            ds_ref[h] = dsns[h]

    return pl.pallas_call(
        body, name=name, grid=(nb,), in_specs=[row, row, row, row, sq, row, stb, row],
        out_specs=[row, row, row, row, sq, glb],
        out_shape=[jax.ShapeDtypeStruct((length, DN_WIDTH), F32)] * 4
        + [jax.ShapeDtypeStruct((DN_HEADS, length, CHUNK), F32), jax.ShapeDtypeStruct((n * 8, DN_WIDTH), F32)],
        scratch_shapes=[pltpu.VMEM((DN_HEADS, HD, HD), F32)],
        compiler_params=_params(("arbitrary",)),
    )(do, qd, kd, w, attn3, vn, st, gcb)


def onorm_fwd(name, o, z, nw, tl=ROWS):
    length = o.shape[0]

    def fn(ctx, rows, consts, prevs, nexts):
        outs = []
        for oh, zh in zip(_heads(rows[0], DN_HEADS, HD), _heads(rows[1], DN_HEADS, HD)):
            r = lax.rsqrt(jnp.mean(oh * oh, axis=1, keepdims=True) + RMS_EPS)
            outs.append(oh * r * consts[0] * _silu(zh))
        return [_cat(outs)], []

    return rowwise(name, fn, length, min(tl, length), rows=[o, z], consts=[nw], out_rows=[(DN_WIDTH, BF16)])[0]


def onorm_bwd(name, o, z, d_on, nw, tl=ROWS):
    length = o.shape[0]

    def fn(ctx, rows, consts, prevs, nexts):
        dos, dzs = [], []
        dnw = jnp.zeros((1, HD), F32)
        for oh, zh, dh in zip(*[_heads(r, DN_HEADS, HD) for r in rows]):
            r = lax.rsqrt(jnp.mean(oh * oh, axis=1, keepdims=True) + RMS_EPS)
            y = oh * r
            sz = _silu(zh)
            t = dh * sz * consts[0]
            dos.append(r * (t - y * jnp.mean(t * y, axis=1, keepdims=True)))
            dzs.append(dh * y * consts[0] * _dsilu(zh))
            dnw = dnw + _csum(dh * y * sz)
        return [_cat(dos), _cat(dzs)], [dnw]

    return rowwise(name, fn, length, min(tl, length), rows=[o, z, d_on], consts=[nw],
                   out_rows=[(DN_WIDTH, F32), (DN_WIDTH, BF16)], out_accs=[((1, HD), F32)])


def merge_fwd(name, gates, ydn, ypool, tl=ROWS_WIDE):
    length = ydn.shape[0]

    def fn(ctx, rows, consts, prevs, nexts):
        gt = rows[0]
        return [_sigmoid(gt[:, :D_MODEL]) * rows[1] + _sigmoid(gt[:, D_MODEL:]) * rows[2]], []

    return rowwise(name, fn, length, min(tl, length), rows=[gates, ydn, ypool], out_rows=[(D_MODEL, BF16)])[0]


def merge_bwd(name, gates, ydn, ypool, dm, tl=ROWS_WIDE):
    length = ydn.shape[0]

    def fn(ctx, rows, consts, prevs, nexts):
        gt, yd, yp, d = rows
        sd, sp = _sigmoid(gt[:, :D_MODEL]), _sigmoid(gt[:, D_MODEL:])
        dgates = _cat([d * yd * sd * (1.0 - sd), d * yp * sp * (1.0 - sp)])
        return [d * sd, d * sp, dgates], []

    return rowwise(name, fn, length, min(tl, length), rows=[gates, ydn, ypool, dm],
                   out_rows=[(D_MODEL, BF16), (D_MODEL, BF16), (2 * D_MODEL, BF16)])


def _trailing_sums(ext, upto):
    s, sh = ext, 1
    while sh < upto:
        s = s + pltpu.roll(s, sh, 0)
        sh *= 2
    return s


def _leading_sums(ext, upto, n):
    s, sh = ext, 1
    while sh < upto:
        s = s + pltpu.roll(s, n - sh, 0)
        sh *= 2
    return s


def _pool_mixed(ctx, p, prev, tl):
    prevm = jnp.where(ctx.i > 0, prev, 0.0)
    t1 = (_row_index(ctx, tl) + 1).astype(F32)
    outs = []
    for gi, win in enumerate(POOL_WINDOWS):
        sl = slice(gi * HD, (gi + 1) * HD)
        ext = jnp.concatenate([prevm[:, sl], p[:, sl]], axis=0)
        mean = _trailing_sums(ext, win)[HALO:] / jnp.minimum(t1, float(win))
        outs.append(mean - p[:, sl])
    return outs


def pool_fwd(name, p, pool_w, scale, tl=ROWS):
    length = p.shape[0]
    tl = min(tl, length)

    def fn(ctx, rows, consts, prevs, nexts):
        mixed = _pool_mixed(ctx, rows[0], prevs[0], tl)
        y = _cat([dnn(m, consts[0][gi]) for gi, m in enumerate(mixed)])
        return [y * consts[1]], []

    return rowwise(name, fn, length, tl, rows=[p], consts=[pool_w, scale], prevs=[p],
                   out_rows=[(POOL_WIDTH, BF16)])[0]


def pool_bwd(name, p, dpo, pool_w, scale, tl=ROWS):
    length = p.shape[0]
    tl = min(tl, length)
    n = tl + HALO

    def fn(ctx, rows, consts, prevs, nexts):
        last = ctx.i == ctx.nblk - 1
        mixed = _pool_mixed(ctx, rows[0], prevs[0], tl)
        dext = jnp.concatenate([rows[1], jnp.where(last, 0.0, nexts[0])], axis=0)
        t1 = (_row_index(ctx, n) + 1).astype(F32)
        dps, dws, dscs = [], [], []
        for gi, win in enumerate(POOL_WINDOWS):
            sl = slice(gi * HD, (gi + 1) * HD)
            wg = consts[0][gi]
            dyraw = dext[:, sl] * consts[1][:, sl]
            dmix = dnt(dyraw, wg)
            dws.append(dtn(mixed[gi], dyraw[:tl]))
            dscs.append(_csum(rows[1][:, sl] * dnn(mixed[gi], wg)))
            lead = _leading_sums(dmix / jnp.minimum(t1, float(win)), win, n)
            dps.append(lead[:tl] - dmix[:tl])
        return [_cat(dps)], [jnp.stack(dws), _cat(dscs)]

    return rowwise(name, fn, length, tl, rows=[p, dpo], consts=[pool_w, scale], prevs=[p], nexts=[dpo],
                   out_rows=[(POOL_WIDTH, BF16)],
                   out_accs=[((len(POOL_WINDOWS), HD, HD), F32), ((1, POOL_WIDTH), F32)])


def _xa_probs(qh, kh):
    s = dnt(qh, kh) * (XA_HD ** -0.5)
    e = jnp.exp(s - jnp.max(s, axis=1, keepdims=True))
    return e / _rsum(e)


def xattn_fwd(name, qx, kx, vx, tl=ROWS):
    length = qx.shape[0]

    def fn(ctx, rows, consts, prevs, nexts):
        outs = [dnn(_xa_probs(qh, kh), vh) for qh, kh, vh in
                zip(_heads(rows[0], XA_HEADS, XA_HD), _heads(consts[0], XA_HEADS, XA_HD),
                    _heads(consts[1], XA_HEADS, XA_HD))]
        return [_cat(outs)], []

    return rowwise(name, fn, length, min(tl, length), rows=[qx], consts=[kx, vx], out_rows=[(D_MODEL, BF16)])[0]


def xattn_bwd(name, qx, dox, kx, vx, tl=ROWS):
    length = qx.shape[0]

    def fn(ctx, rows, consts, prevs, nexts):
        dqs, dks, dvs = [], [], []
        for qh, dh, kh, vh in zip(_heads(rows[0], XA_HEADS, XA_HD), _heads(rows[1], XA_HEADS, XA_HD),
                                  _heads(consts[0], XA_HEADS, XA_HD), _heads(consts[1], XA_HEADS, XA_HD)):
            pr = _xa_probs(qh, kh)
            dpr = dnt(dh, vh)
            ds = pr * (dpr - _rsum(dpr * pr)) * (XA_HD ** -0.5)
            dqs.append(dnn(ds, kh))
            dks.append(dtn(ds, qh))
            dvs.append(dtn(pr, dh))
        return [_cat(dqs)], [_cat(dks), _cat(dvs)]

    return rowwise(name, fn, length, min(tl, length), rows=[qx, dox], consts=[kx, vx],
                   out_rows=[(D_MODEL, BF16)], out_accs=[((N_MEM, D_MODEL), F32)] * 2)


def local_step(x, mem, target, w, io):
    sel, pick = _gate_consts()
    alog = jnp.pad(w["a_log"], ((0, 0), (0, 128 - DN_HEADS)))
    dtb = jnp.pad(w["dt_bias"], ((0, 0), (0, 128 - DN_HEADS)))

    f1, res1, w_down1 = ffn_fwd("ffn1", x, w["ffn1_w_gate"], w["ffn1_w_up"], io.ffn1_down, deps=io.rest_started())
    x1, r1 = ln_fwd("ln1", [(ALPHA, x), (0.5, f1)], w["ln1_g"], w["ln1_b"], deps=io.halfway("mixer", f1))
    w = dict(w, ffn1_w_down=w_down1, **io.weights("mixer", x1))
    taps = [w["conv_w"][j:j + 1] for j in range(4)]

    pre = mm("in_qkv", x1, w["in_qkv"], tb=True)
    z = mm("in_z", x1, w["in_z"], tb=True)
    gates = mm("in_gates", x1, w["in_gates"], tb=True)
    p = mm("in_p", x1, w["in_p"], tb=True)
    ab = mm("in_ab", x1, w["in_ab"], tb=True)
    q, k, v = conv_fwd("conv", pre, taps, deps=io.halfway("xa", pre))
    gb, bb = gates_fwd("gates", ab, alog, dtb, sel)
    u, wd_, qd, kd, gcb, attn3, t3 = delta_prep_fwd("dprep", q, k, v, gb, bb)
    o, vn, st = delta_scan_fwd("dscan", qd, kd, u, wd_, attn3, gcb)
    on = onorm_fwd("onorm", o, z, w["dn_norm_w"])
    ydn = mm("dn_branch", on, w["w_dn_branch"], tb=True)
    po = pool_fwd("pool", p, w["pool_w"], w["pool_scale"])
    ypool = mm("pool_branch", po, w["w_pool_branch"], tb=True)
    merged = merge_fwd("merge", gates, ydn, ypool)
    mix = mm("mix_out", merged, w["w_mix_out"])
    x2, r2 = ln_fwd("ln2", [(ALPHA, x1), (1.0, mix)], w["ln2_g"], w["ln2_b"])

    w = dict(w, **io.weights("xa", x2))
    m, _ = ln_fwd("ln_mem", [(1.0, mem)], w["mem_ln_g"], w["mem_ln_b"])
    qx = mm("xa_q", x2, w["xa_wq"], deps=io.halfway("ffn2", x2))
    kx = mm("xa_k", m, w["xa_wk"])
    vx = mm("xa_v", m, w["xa_wv"])
    ox = xattn_fwd("xattn", qx, kx, vx)
    xa = mm("xa_o", ox, w["xa_wo"])
    x3, r3 = ln_fwd("ln3", [(ALPHA, x2), (1.0, xa)], w["ln3_g"], w["ln3_b"])
    w = dict(w, **io.weights("ffn2", x3))

    f2, res2, _ = ffn_fwd("ffn2", x3, w["ffn2_w_gate"], w["ffn2_w_up"], w["ffn2_w_down"])
    dy4, r4, loss = ln_loss("ln4_loss", [(ALPHA, x3), (0.5, f2)], w["ln4_g"], w["ln4_b"], target)

    g = {}
    dr4, g["ln4_g"], g["ln4_b"] = ln_bwd("ln4_b", r4, [(1.0, dy4)], w["ln4_g"])
    dx3, g["ffn2_w_gate"], g["ffn2_w_up"], g["ffn2_w_down"] = ffn_bwd(
        "ffn2b", x3, res2, dr4, w["ffn2_w_gate"], w["ffn2_w_up"], w["ffn2_w_down"])
    dep = io.grads_out("ffn2", g)
    dr3, g["ln3_g"], g["ln3_b"] = ln_bwd("ln3_b", r3, [(ALPHA, dr4), (1.0, dx3)], w["ln3_g"], deps=dep)

    dox = mm("xa_do", dr3, w["xa_wo"], tb=True)
    g["xa_wo"] = mm("xa_dwo", ox, dr3, ta=True)
    dqx, dkx, dvx = xattn_bwd("xattn_b", qx, dox, kx, vx)
    g["xa_wq"] = mm("xa_dwq", x2, dqx, ta=True)
    dx2 = mm("xa_dx", dqx, w["xa_wq"], tb=True)
    g["xa_wk"] = mm("xa_dwk", m, dkx, ta=True)
    g["xa_wv"] = mm("xa_dwv", m, dvx, ta=True)
    dmm = mm("xa_dmk", dkx, w["xa_wk"], tb=True, deps=io.grads_out("xa", g))
    dmm = mm("xa_dmv", dvx, w["xa_wv"], tb=True, add=dmm)
    _, g["mem_ln_g"], g["mem_ln_b"] = ln_bwd("ln_mem_b", mem, [(1.0, dmm)], w["mem_ln_g"])
    dr2, g["ln2_g"], g["ln2_b"] = ln_bwd("ln2_b", r2, [(ALPHA, dr3), (1.0, dx2)], w["ln2_g"])
    io.grads_in("ffn2", dr2)

    dmerged = mm("mix_dm", dr2, w["w_mix_out"], tb=True)
    g["w_mix_out"] = mm("mix_dw", merged, dr2, ta=True)
    d_ydn, d_ypool, d_gates = merge_bwd("merge_b", gates, ydn, ypool, dmerged)
    g["w_dn_branch"] = mm("dn_dw", d_ydn, on, ta=True)
    d_on = mm("dn_dx", d_ydn, w["w_dn_branch"])
    g["w_pool_branch"] = mm("pool_dw", d_ypool, po, ta=True)
    d_po = mm("pool_dx", d_ypool, w["w_pool_branch"])
    dp, g["pool_w"], g["pool_scale"] = pool_bwd("pool_b", p, d_po, w["pool_w"], w["pool_scale"])
    d_o, dz, g["dn_norm_w"] = onorm_bwd("onorm_b", o, z, d_on, w["dn_norm_w"])
    dqd, dkd, du, dw_, dattn3, dgl = delta_scan_bwd("dscan_b", d_o, qd, kd, wd_, attn3, vn, st, gcb)
    dq, dk, dv, dgb, dbb = delta_prep_bwd("dprep_b", q, k, v, gb, bb, t3, du, dw_, dqd, dkd, dattn3, dgl)
    dpre, dc0, dc1, dc2, dc3 = conv_bwd("conv_b", pre, dq, dk, dv, taps)
    g["conv_w"] = jnp.concatenate([dc0, dc1, dc2, dc3], axis=0)
    d_ab, dalog, ddtb = gates_bwd("gates_b", ab, dgb, dbb, alog, dtb, pick)
    g["a_log"] = dalog[:, :DN_HEADS]
    g["dt_bias"] = ddtb[:, :DN_HEADS]
    g["in_qkv"] = mm("in_dwqkv", dpre, x1, ta=True)
    g["in_z"] = mm("in_dwz", dz, x1, ta=True)
    g["in_gates"] = mm("in_dwgates", d_gates, x1, ta=True)
    g["in_p"] = mm("in_dwp", dp, x1, ta=True)
    g["in_ab"] = mm("in_dwab", d_ab, x1, ta=True)
    io.grads_in("xa", g["in_ab"])
    dx1 = mm("in_dxqkv", dpre, w["in_qkv"], deps=io.grads_out("mixer", g))
    dx1 = mm("in_dxz", dz, w["in_z"], add=dx1)
    dx1 = mm("in_dxgates", d_gates, w["in_gates"], add=dx1)
    dx1 = mm("in_dxp", dp, w["in_p"], add=dx1)
    dx1 = mm("in_dxab", d_ab, w["in_ab"], add=dx1)
    dr1, g["ln1_g"], g["ln1_b"] = ln_bwd("ln1_b", r1, [(ALPHA, dr2), (1.0, dx1)], w["ln1_g"])

    def on_dwd(dwd):
        return io.small_out(dict(g, loss=loss[0, :1])) + io.grads_out("ffn1_d", dict(ffn1_w_down=dwd))

    def on_dwgu(dwg, dwu):
        return io.grads_out("ffn1_gu", dict(ffn1_w_gate=dwg, ffn1_w_up=dwu))

    grad_x, g["ffn1_w_gate"], g["ffn1_w_up"], g["ffn1_w_down"] = ffn_bwd(
        "ffn1b", x, res1, dr1, w["ffn1_w_gate"], w["ffn1_w_up"], w["ffn1_w_down"], on_dwd=on_dwd, on_dwgu=on_dwgu,
        also=(ALPHA, dr1))
    return loss, grad_x, g


WEIGHT_NAMES = ['ffn1_w_gate', 'ffn1_w_up', 'ffn1_w_down', 'ln1_g', 'ln1_b', 'w_in', 'conv_w', 'a_log', 'dt_bias',
                'dn_norm_w', 'w_dn_branch', 'pool_w', 'pool_scale', 'w_pool_branch', 'w_mix_out', 'ln2_g', 'ln2_b',
                'mem_ln_g', 'mem_ln_b', 'xa_wq', 'xa_wk', 'xa_wv', 'xa_wo', 'ln3_g', 'ln3_b', 'ffn2_w_gate',
                'ffn2_w_up', 'ffn2_w_down', 'ln4_g', 'ln4_b']
SHARDED = [
    ("ffn1_w_gate", "cols", (1024, 352)), ("ffn1_w_up", "cols", (1024, 352)), ("ffn1_w_down", "rows", (352, 1024)),
    ("w_in", "cols", (1024, 577)), ("conv_w", "flat", (4, 192)), ("w_dn_branch", "cols", (512, 128)),
    ("w_pool_branch", "cols", (512, 128)), ("w_mix_out", "rows", (128, 1024)), ("xa_wq", "rows", (128, 1024)),
    ("xa_wk", "rows", (128, 1024)), ("xa_wv", "rows", (128, 1024)), ("xa_wo", "rows", (128, 1024)),
    ("ffn2_w_gate", "cols", (1024, 352)), ("ffn2_w_up", "cols", (1024, 352)), ("ffn2_w_down", "rows", (352, 1024)),
]
REPLICATED = [n for n in WEIGHT_NAMES if n not in {s[0] for s in SHARDED}]
ROW_ALIGN = 16
ROW_BLOCKS = (512, 384, 352, 256, 192, 176, 128)
GROUPS = {"ffn1_gu": ("ffn1_w_gate", "ffn1_w_up"), "ffn1_d": ("ffn1_w_down",),
          "mixer": ("w_in", "conv_w", "w_dn_branch", "w_pool_branch", "w_mix_out"),
          "xa": ("xa_wq", "xa_wk", "xa_wv", "xa_wo"),
          "ffn2": ("ffn2_w_gate", "ffn2_w_up", "ffn2_w_down")}
W_IN_COLS = 577
W_IN_PIECES = (("in_qkv", 0, 1536), ("in_z", 1536, 2048), ("in_ab", 2048, 2056), ("in_p", 2056, 2568),
               ("in_gates", 2568, 4616))


def _round_up(n, m):
    return -(-n // m) * m


def _layout():
    off, table = 0, {}
    for name, form, shape in SHARDED:
        valid = {"rows": shape[0], "cols": shape[1], "flat": 2}[form]
        width = {"rows": shape[1], "cols": shape[0], "flat": shape[0] * shape[1]}[form]
        rows = _round_up(valid, ROW_ALIGN)
        table[name] = (off, rows, valid, width, form, shape)
        off += rows
    return table


LAYOUT = _layout()


def _group_span(names):
    base = LAYOUT[names[0]][0]
    rows = LAYOUT[names[-1]][0] + LAYOUT[names[-1]][1] - base
    while not any(rows % b == 0 for b in ROW_BLOCKS):
        rows += ROW_ALIGN
    return base, rows


def _row_block(rows):
    return _pick(rows, ROW_BLOCKS)


def _pad_block(blk, rows):
    return jnp.pad(blk, ((0, rows - blk.shape[0]), (0, LANES - blk.shape[1])))


def pack_weight_shards(shards, names):
    parts, used = [], 0
    for name in names:
        off, rows, valid, width, form, _ = LAYOUT[name]
        s = shards[name]
        if form == "flat":
            flat = s.reshape(1, -1)
            hi = flat.astype(BF16)
            blk = jnp.concatenate([hi, (flat - hi.astype(F32)).astype(BF16)], axis=0)
        else:
            blk = (s.T if form == "cols" else s).astype(BF16)
        parts.append(_pad_block(blk, rows))
        used += rows
    if _group_span(names)[1] > used:
        parts.append(jnp.zeros((_group_span(names)[1] - used, LANES), BF16))
    return jnp.concatenate(parts, axis=0)


IN_AB_ROWS = 128


def _w_in_segments(rows, first, last):
    segs = []
    for k in range(N_DEV):
        lo, hi = max(first, k * W_IN_COLS), min(last, (k + 1) * W_IN_COLS)
        if lo < hi:
            segs.append((k * rows + lo - k * W_IN_COLS, lo - first, hi - lo))
    return segs


def w_in_pieces(name, padded, rows):
    sizes = [IN_AB_ROWS if piece == "in_ab" else last - first for piece, first, last in W_IN_PIECES]

    def body(src_ref, *outs):
        for o_ref, (piece, first, last) in zip(outs, W_IN_PIECES):
            if piece == "in_ab":
                o_ref[...] = jnp.zeros_like(o_ref)
            for src, dst, count in _w_in_segments(rows, first, last):
                o_ref[pl.ds(dst, count), :] = src_ref[pl.ds(src, count), :]

    outs = pl.pallas_call(
        body, name=name, out_shape=[jax.ShapeDtypeStruct((n, LANES), padded.dtype) for n in sizes],
        compiler_params=pltpu.CompilerParams(vmem_limit_bytes=VMEM_LIMIT_BYTES),
    )(padded)
    return {piece: o for (piece, _, _), o in zip(W_IN_PIECES, outs)}


def unpack_full_weights(gathered, names):
    out, base = {}, _group_span(names)[0]
    for name in names:
        off, rows, valid, width, form, shape = LAYOUT[name]
        seg = gathered[:, off - base:off - base + rows]
        if form == "flat":
            flat = seg[:, 0, :width].astype(F32) + seg[:, 1, :width].astype(F32)
            out[name] = flat.reshape((N_DEV,) + shape).transpose(1, 0, 2).reshape(shape[0], N_DEV * shape[1])
        elif name == "w_in":
            out.update(w_in_pieces("w_in_pieces", seg.reshape(N_DEV * rows, LANES), rows))
        else:
            out[name] = seg[:, :valid, :width].reshape(N_DEV * valid, width)
    return out


def pack_full_grads(grads, names, me):
    wire, own, used = [], [], 0
    for name in names:
        off, rows, valid, width, form, shape = LAYOUT[name]
        if form == "flat":
            full = grads[name].reshape(shape[0], N_DEV, shape[1]).transpose(1, 0, 2).reshape(N_DEV, 1, width)
        elif name == "w_in":
            full = jnp.concatenate([grads[piece][:last - first] for piece, first, last in W_IN_PIECES], axis=0)
            full = full.reshape(N_DEV, valid, width)
        else:
            full = grads[name].reshape(N_DEV, valid, width)
        pad = ((0, rows - full.shape[1]), (0, LANES - width))
        wire.append(jnp.pad(full.astype(WIRE), ((0, 0),) + pad))
        own.append(jnp.pad(lax.dynamic_index_in_dim(full, me, 0, keepdims=False), pad))
        used += rows
    if _group_span(names)[1] > used:
        wire.append(jnp.zeros((N_DEV, _group_span(names)[1] - used, LANES), WIRE))
        own.append(jnp.zeros((_group_span(names)[1] - used, LANES), F32))
    return jnp.concatenate(wire, axis=1), jnp.concatenate(own, axis=0)


def unpack_grad_shards(packed, names):
    out, base = {}, _group_span(names)[0]
    for name in names:
        off, rows, valid, width, form, shape = LAYOUT[name]
        off -= base
        if form == "flat":
            out[name] = packed[off, :width].reshape(shape)
        elif form == "cols":
            out[name] = packed[off:off + valid, :width].T
        else:
            out[name] = packed[off:off + valid, :width]
    return out


SMALL_SHAPES = {n: (1024,) for n in REPLICATED}
SMALL_SHAPES.update(pool_w=(4, 128, 128), pool_scale=(512,), dn_norm_w=(128,), a_log=(4,), dt_bias=(4,))


SMALL_SHAPES["loss"] = (1,)
SMALL_NAMES = REPLICATED + ["loss"]


def _small_layout():
    off, table = 0, {}
    for name in SMALL_NAMES:
        numel = 1
        for d in SMALL_SHAPES[name]:
            numel *= d
        rows = _round_up(-(-numel // LANES), 8)
        table[name] = (off, rows, numel)
        off += rows
    return table, off


SMALL_LAYOUT, SMALL_ROWS = _small_layout()


def _to_rows(flat, rows):
    return jnp.pad(flat, (0, rows * LANES - flat.shape[0])).reshape(rows, LANES)


def pack_small(values):
    return jnp.concatenate([_to_rows(values[name].reshape(-1), SMALL_LAYOUT[name][1]) for name in SMALL_NAMES], axis=0)


def unpack_small(packed):
    out = {}
    for name in SMALL_NAMES:
        off, rows, numel = SMALL_LAYOUT[name]
        out[name] = packed[off:off + rows].reshape(-1)[:numel].reshape(SMALL_SHAPES[name])
    return out


MESH = pl.DeviceIdType.MESH


def _position():
    return lax.axis_index("x"), lax.axis_index("y"), lax.axis_index("c")


def _other_chips(x, y):
    return [(1 - x, y), (x, 1 - y), (1 - x, 1 - y)]


def all_gather(name, block):
    rows, n = block.shape

    def body(x_ref, out_ref, send_sems, recv_sems, local_sem):
        x, y, c = _position()
        me, sibling = (x, y, c), (x, y, 1 - c)
        chips = _other_chips(x, y)

        def slot(px, py, pc):
            return out_ref.at[4 * px + 2 * py + pc]

        def copy(k, blk, to, src=None):
            return pltpu.make_async_remote_copy(
                src_ref=slot(*blk) if src is None else src, dst_ref=slot(*blk),
                send_sem=send_sems.at[k], recv_sem=recv_sems.at[k], device_id=to, device_id_type=MESH)

        mine = pltpu.make_async_copy(x_ref, slot(*me), local_sem)
        mine.start()
        first = [copy(0, me, sibling, src=x_ref)]
        first += [copy(1 + j, me, (*chip, c), src=x_ref) for j, chip in enumerate(chips)]
        for cp in first:
            cp.start()
        passed = [copy(4 + j, (*chip, c), sibling) for j, chip in enumerate(chips)]
        for j, chip in enumerate(chips):
            copy(1 + j, (*chip, c), me).wait_recv()
            passed[j].start()
        copy(0, sibling, me).wait_recv()
        for j, chip in enumerate(chips):
            copy(4 + j, (*chip, 1 - c), me).wait_recv()
        for cp in first + passed:
            cp.wait_send()
        mine.wait()

    return pl.pallas_call(
        body, name=name, out_shape=jax.ShapeDtypeStruct((N_DEV, rows, n), block.dtype),
        in_specs=[ANY], out_specs=ANY,
        scratch_shapes=[pltpu.SemaphoreType.DMA((7,)), pltpu.SemaphoreType.DMA((7,)), pltpu.SemaphoreType.DMA(())],
    )(block)


HBM = pl.BlockSpec(memory_space=pltpu.HBM)
SEM = pl.BlockSpec(memory_space=pltpu.SEMAPHORE)
EFFECT = pltpu.SideEffectType.DATAFLOW_SIDE_EFFECTING


def _remote(src, dst, send_sem, recv_sem, to):
    return pltpu.make_async_remote_copy(src_ref=src, dst_ref=dst, send_sem=send_sem, recv_sem=recv_sem,
                                        device_id=to, device_id_type=MESH)


def split_start(name, bufs, n, make_copies):
    nb = len(bufs)

    def body(*refs):
        for out_cp, _ in make_copies(refs[:nb], refs[nb:nb + n], refs[nb + n:nb + 2 * n]):
            out_cp.start()
        refs[-1][...] = jnp.zeros_like(refs[-1])

    outs = pl.pallas_call(
        body, name=name,
        out_shape=tuple([pltpu.SemaphoreType.DMA(())] * (2 * n)) + tuple(pltpu.HBM(b.shape, b.dtype) for b in bufs)
        + (jax.ShapeDtypeStruct((8, 128), F32),),
        in_specs=[HBM] * nb,
        out_specs=tuple([SEM] * (2 * n) + [HBM] * nb + [pl.BlockSpec(memory_space=pltpu.VMEM)]),
        input_output_aliases={i: 2 * n + i for i in range(nb)},
        compiler_params=pltpu.CompilerParams(has_side_effects=EFFECT),
    )(*[pltpu.with_memory_space_constraint(b, pltpu.HBM) for b in bufs])
    return list(outs[:2 * n]), list(outs[2 * n:2 * n + nb]), outs[-1]


def split_wait(name, bufs, sems, n, make_copies, after):
    nb = len(bufs)

    def body(*refs):
        for out_cp, in_cp in make_copies(refs[:nb], refs[nb:nb + n], refs[nb + n:nb + 2 * n]):
            out_cp.wait_send()
            in_cp.wait_recv()

    outs = pl.pallas_call(
        body, name=name, out_shape=tuple(pltpu.HBM(b.shape, b.dtype) for b in bufs),
        in_specs=[HBM] * nb + [SEM] * (2 * n) + [ANY], out_specs=tuple([HBM] * nb),
        input_output_aliases={i: i for i in range(nb)},
        compiler_params=pltpu.CompilerParams(has_side_effects=EFFECT),
    )(*bufs, *sems, after)
    return list(outs)


def _gather_stage1(refs, send, recv):
    src, land = refs
    x, y, c = _position()
    peers = [(x, y, 1 - c)] + [(*chip, c) for chip in _other_chips(x, y)]
    return [(_remote(src, land.at[4 * x + 2 * y + c], send[k], recv[k], p),
             _remote(src, land.at[4 * p[0] + 2 * p[1] + p[2]], send[k], recv[k], p)) for k, p in enumerate(peers)]


def _gather_stage2(refs, send, recv):
    (land,) = refs
    x, y, c = _position()
    out = []
    for j, (px, py) in enumerate(_other_chips(x, y)):
        mine, theirs = land.at[4 * px + 2 * py + c], land.at[4 * px + 2 * py + 1 - c]
        out.append((_remote(mine, mine, send[j], recv[j], (x, y, 1 - c)),
                    _remote(theirs, theirs, send[j], recv[j], (x, y, 1 - c))))
    return out


def _flips():
    return [(a, b, d) for a in (0, 1) for b in (0, 1) for d in (0, 1) if a | b | d]


def _gather_direct(refs, send, recv):
    src, land = refs
    x, y, c = _position()
    out = []
    for k, (fx, fy, fc) in enumerate(_flips()):
        p = (1 - x if fx else x, 1 - y if fy else y, 1 - c if fc else c)
        out.append((_remote(src, land.at[4 * x + 2 * y + c], send[k], recv[k], p),
                    _remote(src, land.at[4 * p[0] + 2 * p[1] + p[2]], send[k], recv[k], p)))
    return out


def _scatter_direct(refs, send, recv):
    sendbuf, land = refs
    x, y, c = _position()
    me = 4 * x + 2 * y + c
    out = []
    for k, (fx, fy, fc) in enumerate(_flips()):
        p = (1 - x if fx else x, 1 - y if fy else y, 1 - c if fc else c)
        peer = 4 * p[0] + 2 * p[1] + p[2]
        out.append((_remote(sendbuf.at[peer], land.at[me], send[k], recv[k], p),
                    _remote(sendbuf.at[peer], land.at[peer], send[k], recv[k], p)))
    return out


def _own_plus_slots(name, own, landed):
    n, rows, _ = landed.shape
    tr = _row_block(rows)

    def body(g_ref, l_ref, o_ref):
        acc = g_ref[...]
        for j in range(n):
            acc = acc + l_ref[j].astype(F32)
        o_ref[...] = acc

    return pl.pallas_call(
        body, name=name, grid=(rows // tr,),
        in_specs=[pl.BlockSpec((tr, LANES), lambda i: (i, 0)), pl.BlockSpec((n, tr, LANES), lambda i: (0, i, 0))],
        out_specs=pl.BlockSpec((tr, LANES), lambda i: (i, 0)),
        out_shape=jax.ShapeDtypeStruct((rows, LANES), F32), compiler_params=_params(("parallel",)),
    )(own, landed)


def _sum_slots(name, stack):
    n, rows, _ = stack.shape

    def body(s_ref, o_ref):
        acc = s_ref[0]
        for j in range(1, n):
            acc = acc + s_ref[j]
        o_ref[...] = acc

    return pl.pallas_call(
        body, name=name, in_specs=[pl.BlockSpec(stack.shape, lambda: (0, 0, 0))],
        out_specs=pl.BlockSpec((rows, LANES), lambda: (0, 0)), out_shape=jax.ShapeDtypeStruct((rows, LANES), F32),
    )(stack)


def adamw(name, w, g, m, v):
    shape = w.shape
    last = shape[-1]
    w2, g2, m2, v2 = [a.reshape(-1, last) for a in (w, g, m, v)]
    rows = w2.shape[0]
    tr = 256 if rows % 256 == 0 else rows

    def body(w_ref, g_ref, m_ref, v_ref, d_ref, nm_ref, nv_ref):
        gg = g_ref[...]
        nm = ADAM_B1 * m_ref[...] + (1.0 - ADAM_B1) * gg
        nv = ADAM_B2 * v_ref[...] + (1.0 - ADAM_B2) * (gg * gg)
        m_hat = nm / (1.0 - ADAM_B1 ** ADAM_STEP)
        v_hat = nv / (1.0 - ADAM_B2 ** ADAM_STEP)
        d_ref[...] = -ADAM_LR * (m_hat / (jnp.sqrt(v_hat) + ADAM_EPS) + ADAM_WD * w_ref[...])
        nm_ref[...] = nm
        nv_ref[...] = nv

    spec = pl.BlockSpec((tr, last), lambda i: (i, 0))
    outs = pl.pallas_call(
        body, name=name, grid=(rows // tr,), in_specs=[spec] * 4, out_specs=[spec] * 3,
        out_shape=[jax.ShapeDtypeStruct((rows, last), F32)] * 3, compiler_params=_params(("parallel",)),
    )(w2, g2, m2, v2)
    return [o.reshape(shape) for o in outs]


def _landing(block_shape, dtype, own):
    x, y, c = _position()
    return lax.dynamic_update_slice(lax.empty((N_DEV,) + block_shape, dtype), own[None], (4 * x + 2 * y + c, 0, 0))


class _Exchanges:
    def __init__(self, shards):
        self.shards = shards
        self.pending = {}
        self.reduced = {}

    def first_weights(self):
        names = GROUPS["ffn1_gu"]
        return unpack_full_weights(all_gather("ag_ffn1_gu", pack_weight_shards(self.shards, names)), names)

    def rest_started(self):
        tokens = []
        block = pack_weight_shards(self.shards, GROUPS["ffn1_d"])
        sems, bufs, token = split_start("ag_ffn1_d_s", [block, _landing(block.shape, block.dtype, block)], N_DEV - 1,
                                        _gather_direct)
        self.pending["ffn1_d"] = (sems, bufs)
        tokens.append(token)
        for key in ("mixer", "xa", "ffn2"):
            block = pack_weight_shards(self.shards, GROUPS[key])
            sems, bufs, token = split_start(f"ag_{key}_s1", [block, _landing(block.shape, block.dtype, block)], 4,
                                            _gather_stage1)
            self.pending[key] = (sems, bufs)
            tokens.append(token)
        return tuple(tokens)

    def ffn1_down(self, after):
        sems, bufs = self.pending.pop("ffn1_d")
        _, gathered = split_wait("ag_ffn1_d_w", bufs, sems, N_DEV - 1, _gather_direct, after)
        return unpack_full_weights(gathered, GROUPS["ffn1_d"])["ffn1_w_down"]

    def halfway(self, key, after):
        sems, bufs = self.pending.pop(key)
        _, land = split_wait(f"ag_{key}_w1", bufs, sems, 4, _gather_stage1, after)
        sems, bufs, token = split_start(f"ag_{key}_s2", [land], 3, _gather_stage2)
        self.pending[key] = (sems, bufs)
        return (token,)

    def weights(self, key, after):
        sems, bufs = self.pending.pop(key)
        (gathered,) = split_wait(f"ag_{key}_w2", bufs, sems, 3, _gather_stage2, after)
        return unpack_full_weights(gathered, GROUPS[key])

    def grads_out(self, key, grads):
        x, y, c = _position()
        wire, own = pack_full_grads(grads, GROUPS[key], 4 * x + 2 * y + c)
        land = _landing(wire.shape[1:], WIRE, jnp.zeros(wire.shape[1:], WIRE))
        sems, bufs, token = split_start(f"rs_{key}_start", [wire, land], N_DEV - 1, _scatter_direct)
        self.pending[key] = (sems, bufs, own)
        return (token,)

    def grads_in(self, key, after):
        sems, bufs, own = self.pending.pop(key)
        _, landed = split_wait(f"rs_{key}_wait", bufs, sems, N_DEV - 1, _scatter_direct, after)
        self.reduced.update(unpack_grad_shards(_own_plus_slots(f"rs_{key}_sum", own, landed), GROUPS[key]))

    def small_out(self, values):
        block = pack_small(values)
        sems, bufs, token = split_start("ag_small_s", [block, _landing(block.shape, block.dtype, block)], N_DEV - 1,
                                        _gather_direct)
        self.pending["small"] = (sems, bufs)
        return (token,)

    def small_in(self, after):
        sems, bufs = self.pending.pop("small")
        _, gathered = split_wait("ag_small_w", bufs, sems, N_DEV - 1, _gather_direct, after)
        return unpack_small(_sum_slots("small_sum", gathered))


def kernel(x, mem, ffn1_w_gate, ffn1_w_up, ffn1_w_down, ln1_g, ln1_b, w_in, conv_w, a_log, dt_bias, dn_norm_w, w_dn_branch, pool_w, pool_scale, w_pool_branch, w_mix_out, ln2_g, ln2_b, mem_ln_g, mem_ln_b, xa_wq, xa_wk, xa_wv, xa_wo, ln3_g, ln3_b, ffn2_w_gate, ffn2_w_up, ffn2_w_down, ln4_g, ln4_b, loss_target, m_ffn1_w_gate, m_ffn1_w_up, m_ffn1_w_down, m_ln1_g, m_ln1_b, m_w_in, m_conv_w, m_a_log, m_dt_bias, m_dn_norm_w, m_w_dn_branch, m_pool_w, m_pool_scale, m_w_pool_branch, m_w_mix_out, m_ln2_g, m_ln2_b, m_mem_ln_g, m_mem_ln_b, m_xa_wq, m_xa_wk, m_xa_wv, m_xa_wo, m_ln3_g, m_ln3_b, m_ffn2_w_gate, m_ffn2_w_up, m_ffn2_w_down, m_ln4_g, m_ln4_b, v_ffn1_w_gate, v_ffn1_w_up, v_ffn1_w_down, v_ln1_g, v_ln1_b, v_w_in, v_conv_w, v_a_log, v_dt_bias, v_dn_norm_w, v_w_dn_branch, v_pool_w, v_pool_scale, v_w_pool_branch, v_w_mix_out, v_ln2_g, v_ln2_b, v_mem_ln_g, v_mem_ln_b, v_xa_wq, v_xa_wk, v_xa_wv, v_xa_wo, v_ln3_g, v_ln3_b, v_ffn2_w_gate, v_ffn2_w_up, v_ffn2_w_down, v_ln4_g, v_ln4_b):
    given = dict(locals())
    shards = {n: given[n] for n in WEIGHT_NAMES}
    io = _Exchanges({n: shards[n][0] for n, _, _ in SHARDED})
    w = io.first_weights()
    for n in REPLICATED:
        w[n] = shards[n][0] if n == "pool_w" else shards[n]
    loss_part, grad_x, g = local_step(x[0], mem[0], loss_target[0], w, io)

    grad, updates = {}, {}

    def update(names, reduced):
        for n in names:
            grad[n] = reduced[n].reshape(shards[n].shape)
            updates[n] = adamw("adamw_" + n, shards[n], grad[n], given["m_" + n], given["v_" + n])
        return updates[names[-1]][0]

    update(GROUPS["ffn2"] + GROUPS["xa"], io.reduced)
    io.grads_in("mixer", grad_x)
    done = update(GROUPS["mixer"], io.reduced)
    small = io.small_in(done)
    loss = small.pop("loss")[0]
    done = update(REPLICATED, small)
    io.grads_in("ffn1_d", done)
    done = update(GROUPS["ffn1_d"], io.reduced)
    io.grads_in("ffn1_gu", done)
    update(GROUPS["ffn1_gu"], io.reduced)
    return (loss, grad_x[None], *[grad[n] for n in WEIGHT_NAMES], *[updates[n][0] for n in WEIGHT_NAMES],
            *[updates[n][1] for n in WEIGHT_NAMES], *[updates[n][2] for n in WEIGHT_NAMES])
```

```python
import functools

import jax
import jax.numpy as jnp
from jax import lax
from jax.experimental import pallas as pl
from jax.experimental.pallas import tpu as pltpu

F32 = jnp.float32
BF16 = jnp.bfloat16
MMD = BF16
WIRE = BF16
HI = lax.Precision.HIGHEST
X3 = lax.Precision.HIGH
VMEM_LIMIT_BYTES = 48 * 1024 * 1024

D_MODEL = 1024
D_FF = 2816
CHUNK = 64
N_MEM = 256
DN_HEADS = 4
HD = 128
DN_WIDTH = 512
POOL_WINDOWS = (2, 4, 8, 16)
POOL_WIDTH = 512
XA_HEADS = 4
XA_HD = 256
LN_EPS = 1e-5
RMS_EPS = 1e-6
L2_EPS = 1e-6
ALPHA = 2.0 ** 0.25
HALO = 16
ROWS = 512
ROWS_WIDE = 256

ADAM_LR = 0.001
ADAM_B1 = 0.9
ADAM_B2 = 0.999
ADAM_EPS = 1e-08
ADAM_WD = 0.01
ADAM_STEP = 10

N_DEV = 8
LANES = 1024
ANY = pl.BlockSpec(memory_space=pl.ANY)


def _dot(a, b, ca, cb, prec):
    dn = (((ca,), (cb,)), ((), ()))
    if prec is not None:
        return lax.dot_general(a.astype(F32), b.astype(F32), dn, precision=prec, preferred_element_type=F32)
    return lax.dot_general(a.astype(MMD), b.astype(MMD), dn, preferred_element_type=F32)


def dnn(a, b, prec=None):
    return _dot(a, b, 1, 0, prec)


def dnt(a, b, prec=None):
    return _dot(a, b, 1, 1, prec)


def dtn(a, b, prec=None):
    return _dot(a, b, 0, 0, prec)


def _sigmoid(x):
    return jax.nn.sigmoid(x)


def _silu(x):
    return x * _sigmoid(x)


def _dsilu(x):
    s = _sigmoid(x)
    return s * (1.0 + x * (1.0 - s))


def _softplus(x):
    return jnp.maximum(x, 0.0) + jnp.log1p(jnp.exp(-jnp.abs(x)))


def _iota(shape, dim):
    return lax.broadcasted_iota(jnp.int32, shape, dim)


def _rsum(x):
    return jnp.sum(x, axis=1, keepdims=True)


def _csum(x):
    return jnp.sum(x, axis=0, keepdims=True)


def _pick(n, cands):
    for c in cands:
        if n % c == 0:
            return c
    return n


def _params(sem):
    return pltpu.CompilerParams(dimension_semantics=sem, vmem_limit_bytes=VMEM_LIMIT_BYTES)


MM_TILE_SIZES = (4096, 2816, 2048, 1536, 1408, 1024, 768, 512, 384, 256, 128)
MM_VMEM_BUDGET = 36 * 1024 * 1024
HBM_BYTES_PER_US = 3.0e6
GRID_STEP_US = 0.35


def _mm_tiles(m, n, kc, a_bytes, b_bytes, o_bytes):
    def sizes(d):
        return [d] if d <= 512 else [t for t in MM_TILE_SIZES if d % t == 0]

    best = None
    for tm in sizes(m):
        for tn in sizes(n):
            for tk in sizes(kc):
                vmem = 2 * (tm * tk * a_bytes + tk * tn * b_bytes + tm * tn * o_bytes) + tm * tn * 4
                if vmem > MM_VMEM_BUDGET:
                    continue
                steps = (m // tm) * (n // tn) * (kc // tk)
                traffic = m * kc * a_bytes * (n // tn) + kc * n * b_bytes * (m // tm) + m * n * o_bytes
                edge = tm * tk * a_bytes + tk * tn * b_bytes + tm * tn * o_bytes
                cost = (traffic + edge) / HBM_BYTES_PER_US + steps * GRID_STEP_US
                if best is None or cost < best[0]:
                    best = (cost, tm, tn, tk)
    return best[1:]


def mm(name, a, b, *, ta=False, tb=False, out_dtype=F32, add=None, scale=None, deps=()):
    adds = [] if add is None else (list(add) if isinstance(add, (list, tuple)) else [(1.0, add)])
    if ta:
        kc, m = a.shape
    else:
        m, kc = a.shape
    if tb:
        n, kb = b.shape
    else:
        kb, n = b.shape
    assert kc == kb, (name, a.shape, b.shape)
    tm, tn, tk = _mm_tiles(m, n, kc, a.dtype.itemsize, b.dtype.itemsize,
                           jnp.dtype(out_dtype).itemsize * (1 + len(adds)))
    nk = kc // tk
    grid = (m // tm, n // tn, nk)
    a_spec = pl.BlockSpec((tk, tm), lambda i, j, k: (k, i)) if ta else pl.BlockSpec((tm, tk), lambda i, j, k: (i, k))
    b_spec = pl.BlockSpec((tn, tk), lambda i, j, k: (j, k)) if tb else pl.BlockSpec((tk, tn), lambda i, j, k: (k, j))
    o_spec = pl.BlockSpec((tm, tn), lambda i, j, k: (i, j))
    ca, cb = (0 if ta else 1), (1 if tb else 0)

    def body(*refs):
        a_ref, b_ref = refs[0], refs[1]
        o_ref = refs[-1] if nk == 1 else refs[-2]
        k = pl.program_id(2)
        part = _dot(a_ref[...], b_ref[...], ca, cb, None)

        def finish(r):
            if scale is not None:
                r = r * scale
            for (coef, _), add_ref in zip(adds, refs[2:2 + len(adds)]):
                r = r + (add_ref[...] if coef == 1.0 else coef * add_ref[...])
            o_ref[...] = r.astype(o_ref.dtype)

        if nk == 1:
            finish(part)
            return
        acc_ref = refs[-1]

        @pl.when(k == 0)
        def _():
            acc_ref[...] = part

        if nk > 2:
            @pl.when((k > 0) & (k < nk - 1))
            def _():
                acc_ref[...] += part

        @pl.when(k == nk - 1)
        def _():
            finish(acc_ref[...] + part)

    ins = [a, b] + [t for _, t in adds] + list(deps)
    specs = [a_spec, b_spec] + [o_spec] * len(adds) + [ANY] * len(deps)
    return pl.pallas_call(
        body, name=name, grid=grid, in_specs=specs, out_specs=o_spec,
        out_shape=jax.ShapeDtypeStruct((m, n), out_dtype),
        scratch_shapes=[pltpu.VMEM((tm, tn), F32)] if nk > 1 else [],
        compiler_params=_params(("parallel", "parallel", "arbitrary")),
    )(*ins)


class _Ctx:
    def __init__(self, i, nblk, tl):
        self.i, self.nblk, self.tl = i, nblk, tl


def _norm_item(it):
    if isinstance(it, tuple):
        a, w, j = it[:3]
        rows = it[3] if len(it) > 3 else None
        return a, w, j, rows
    return it, it.shape[-1], 0, None


def rowwise(name, fn, length, tl, *, rows=(), consts=(), prevs=(), nexts=(), out_rows=(), out_accs=(), deps=()):
    nblk = length // tl
    hb = tl // HALO
    nhalo = length // HALO
    arrays, specs = [], []
    for it in rows:
        a, w, j, r = _norm_item(it)
        if a.ndim == 3:
            specs.append(pl.BlockSpec((a.shape[0], tl, w), lambda i, j=j: (0, i, j)))
        else:
            specs.append(pl.BlockSpec((r or tl, w), lambda i, j=j: (i, j)))
        arrays.append(a)
    for a in consts:
        specs.append(pl.BlockSpec(a.shape, lambda i, nd=a.ndim: (0,) * nd))
        arrays.append(a)
    for it in prevs:
        a, w, j, _ = _norm_item(it)
        specs.append(pl.BlockSpec((HALO, w), lambda i, j=j: (jnp.maximum(i * hb - 1, 0), j)))
        arrays.append(a)
    for it in nexts:
        a, w, j, _ = _norm_item(it)
        specs.append(pl.BlockSpec((HALO, w), lambda i, j=j: (jnp.minimum((i + 1) * hb, nhalo - 1), j)))
        arrays.append(a)
    out_shape, out_specs = [], []
    for spec in out_rows:
        if len(spec) == 3:
            h, w, dt = spec
            out_shape.append(jax.ShapeDtypeStruct((h, length, w), dt))
            out_specs.append(pl.BlockSpec((h, tl, w), lambda i: (0, i, 0)))
        else:
            w, dt = spec
            out_shape.append(jax.ShapeDtypeStruct((length, w), dt))
            out_specs.append(pl.BlockSpec((tl, w), lambda i: (i, 0)))
    for shape, dt in out_accs:
        out_shape.append(jax.ShapeDtypeStruct(shape, dt))
        out_specs.append(pl.BlockSpec(shape, lambda i, nd=len(shape): (0,) * nd))
    n_r, n_c, n_p, n_n = len(rows), len(consts), len(prevs), len(nexts)
    n_in = n_r + n_c + n_p + n_n
    n_or = len(out_rows)
    arrays, specs = arrays + list(deps), specs + [ANY] * len(deps)

    def body(*refs):
        i = pl.program_id(0)
        vals = [r[...] for r in refs[:n_in]]
        outs = refs[n_in + len(deps):]
        ctx = _Ctx(i, nblk, tl)
        ro, ao = fn(ctx, vals[:n_r], vals[n_r:n_r + n_c], vals[n_r + n_c:n_r + n_c + n_p], vals[n_r + n_c + n_p:])
        for r, v in zip(outs[:n_or], ro, strict=True):
            r[...] = v.astype(r.dtype)
        for r, v in zip(outs[n_or:], ao, strict=True):
            @pl.when(i == 0)
            def _(r=r, v=v):
                r[...] = v.astype(r.dtype)

            @pl.when(i > 0)
            def _(r=r, v=v):
                r[...] += v.astype(r.dtype)

    res = pl.pallas_call(
        body, name=name, grid=(nblk,), in_specs=specs, out_specs=out_specs, out_shape=out_shape,
        compiler_params=_params(("arbitrary",) if out_accs else ("parallel",)),
    )(*arrays)
    return res


def _heads(x, n, w):
    return [x[:, h * w:(h + 1) * w] for h in range(n)]


def _cat(xs):
    return jnp.concatenate(xs, axis=1)


def _row_index(ctx, nrows, offset=0):
    return ctx.i * ctx.tl + offset + _iota((nrows, 1), 0)


def _ln_stats(r):
    mu = jnp.mean(r, axis=1, keepdims=True)
    d = r - mu
    var = jnp.mean(d * d, axis=1, keepdims=True)
    rstd = lax.rsqrt(var + LN_EPS)
    return d * rstd, rstd


def ln_fwd(name, terms, g, b, tl=ROWS, deps=()):
    coefs = [c for c, _ in terms]
    length = terms[0][1].shape[0]

    def fn(ctx, rows, consts, prevs, nexts):
        r = sum(c * t for c, t in zip(coefs, rows))
        xh, _ = _ln_stats(r)
        return [xh * consts[0] + consts[1], r], []

    return rowwise(name, fn, length, min(tl, length), rows=[t for _, t in terms], consts=[g, b],
                   out_rows=[(D_MODEL, F32), (D_MODEL, F32)], deps=deps)


def ln_bwd(name, r, terms, g, tl=ROWS, deps=()):
    coefs = [c for c, _ in terms]
    length = r.shape[0]

    def fn(ctx, rows, consts, prevs, nexts):
        xh, rstd = _ln_stats(rows[0])
        dy = sum(c * t for c, t in zip(coefs, rows[1:]))
        dxh = dy * consts[0]
        dr = rstd * (dxh - jnp.mean(dxh, axis=1, keepdims=True) - xh * jnp.mean(dxh * xh, axis=1, keepdims=True))
        return [dr], [_csum(dy * xh), _csum(dy)]

    return rowwise(name, fn, length, min(tl, length), rows=[r] + [t for _, t in terms], consts=[g],
                   out_rows=[(D_MODEL, F32)], out_accs=[((1, D_MODEL), F32), ((1, D_MODEL), F32)], deps=deps)


def ln_loss(name, terms, g, b, target, tl=ROWS):
    coefs = [c for c, _ in terms]
    length = target.shape[0]
    nt = len(terms)

    def fn(ctx, rows, consts, prevs, nexts):
        r = sum(c * t for c, t in zip(coefs, rows[:nt]))
        xh, _ = _ln_stats(r)
        err = xh * consts[0] + consts[1] - rows[nt]
        tot = _csum(_rsum(err * err)) * (0.5 / D_MODEL)
        return [err * (1.0 / D_MODEL), r], [jnp.broadcast_to(tot, (1, 128))]

    return rowwise(name, fn, length, min(tl, length), rows=[t for _, t in terms] + [target], consts=[g, b],
                   out_rows=[(D_MODEL, F32), (D_MODEL, F32)], out_accs=[((1, 128), F32)])


def _ffn_blocks(length):
    return min(512, length), D_FF // 2


def ffn_gate_up_act(name, x, wg, wu, deps=()):
    length = x.shape[0]
    tm, tn = _ffn_blocks(length)

    def body(x_ref, wg_ref, wu_ref, *rest):
        hg_ref, hu_ref, act_ref = rest[-3:]
        xb = x_ref[...].astype(MMD)
        hg = dnt(xb, wg_ref[...])
        hu = dnt(xb, wu_ref[...])
        hg_ref[...] = hg
        hu_ref[...] = hu
        act_ref[...] = (_silu(hg) * hu).astype(act_ref.dtype)

    row = pl.BlockSpec((tm, D_MODEL), lambda i, j: (i, 0))
    wsp = pl.BlockSpec((tn, D_MODEL), lambda i, j: (j, 0))
    osp = pl.BlockSpec((tm, tn), lambda i, j: (i, j))
    return pl.pallas_call(
        body, name=name, grid=(length // tm, D_FF // tn), in_specs=[row, wsp, wsp] + [ANY] * len(deps),
        out_specs=[osp] * 3,
        out_shape=[jax.ShapeDtypeStruct((length, D_FF), F32)] * 2 + [jax.ShapeDtypeStruct((length, D_FF), BF16)],
        compiler_params=_params(("parallel", "parallel")),
    )(x, wg, wu, *deps)


def ffn_dact(name, dr, wd, hg, hu, deps=()):
    length = dr.shape[0]
    tm, tn = _ffn_blocks(length)

    def body(dr_ref, wd_ref, hg_ref, hu_ref, *rest):
        dhg_ref, dhu_ref = rest[-2:]
        da = 0.5 * dnt(dr_ref[...], wd_ref[...])
        g = hg_ref[...]
        s = _sigmoid(g)
        dhg_ref[...] = (da * hu_ref[...] * (s * (1.0 + g * (1.0 - s)))).astype(dhg_ref.dtype)
        dhu_ref[...] = (da * (g * s)).astype(dhu_ref.dtype)

    row = pl.BlockSpec((tm, D_MODEL), lambda i, j: (i, 0))
    wsp = pl.BlockSpec((tn, D_MODEL), lambda i, j: (j, 0))
    osp = pl.BlockSpec((tm, tn), lambda i, j: (i, j))
    return pl.pallas_call(
        body, name=name, grid=(length // tm, D_FF // tn), in_specs=[row, wsp, osp, osp] + [ANY] * len(deps),
        out_specs=[osp] * 2, out_shape=[jax.ShapeDtypeStruct((length, D_FF), BF16)] * 2,
        compiler_params=_params(("parallel", "parallel")),
    )(dr, wd, hg, hu, *deps)


def ffn_fwd(tag, x, wg, wu, wd, deps=()):
    hg, hu, act = ffn_gate_up_act(tag + "_gate_up", x, wg, wu, deps)
    if callable(wd):
        wd = wd(act)
    f = mm(tag + "_down", act, wd)
    return f, (hg, hu, act), wd


def ffn_bwd(tag, x, res, dr, wg, wu, wd, deps=(), on_dwd=None, on_dwgu=None, also=None):
    hg, hu, act = res
    dwd = mm(tag + "_dwd", act, dr, ta=True, scale=0.5, deps=deps)
    dhg, dhu = ffn_dact(tag + "_dact", dr, wd, hg, hu, deps=on_dwd(dwd) if on_dwd else ())
    dwg = mm(tag + "_dwg", dhg, x, ta=True)
    dwu = mm(tag + "_dwu", dhu, x, ta=True)
    dx = mm(tag + "_dxg", dhg, wg, deps=on_dwgu(dwg, dwu) if on_dwgu else ())
    dx = mm(tag + "_dxu", dhu, wu, add=[(1.0, dx)] + ([also] if also else []))
    return dx, dwg, dwu, dwd


def _conv_taps(ext, taps, n):
    out = taps[3] * ext
    for j in range(3):
        out = out + taps[j] * pltpu.roll(ext, 3 - j, 0)
    return out


def _l2n(x):
    r = lax.rsqrt(_rsum(x * x) + L2_EPS)
    return x * r, r


def conv_fwd(name, pre, taps, tl=ROWS_WIDE, deps=()):
    length = pre.shape[0]
    tl = min(tl, length)

    def fn(ctx, rows, consts, prevs, nexts):
        prev = jnp.where(ctx.i > 0, prevs[0], 0.0)
        ext = jnp.concatenate([prev, rows[0]], axis=0)
        s = _silu(_conv_taps(ext, consts, tl + HALO)[HALO:])
        q = _cat([_l2n(x)[0] * (HD ** -0.5) for x in _heads(s[:, :DN_WIDTH], DN_HEADS, HD)])
        k = _cat([_l2n(x)[0] for x in _heads(s[:, DN_WIDTH:2 * DN_WIDTH], DN_HEADS, HD)])
        return [q, k, s[:, 2 * DN_WIDTH:]], []

    return rowwise(name, fn, length, tl, rows=[pre], consts=list(taps), prevs=[pre],
                   out_rows=[(DN_WIDTH, F32)] * 3, deps=deps)


def conv_bwd(name, pre, dq, dk, dv, taps, tl=ROWS_WIDE):
    length = pre.shape[0]
    tl = min(tl, length)
    n = tl + 2 * HALO

    def fn(ctx, rows, consts, prevs, nexts):
        last = ctx.i == ctx.nblk - 1
        prev = jnp.where(ctx.i > 0, prevs[0], 0.0)
        ext = jnp.concatenate([prev, rows[0], nexts[0]], axis=0)
        c = _conv_taps(ext, consts, n)
        sg = _sigmoid(c)
        s = c * sg
        zero = jnp.zeros((HALO, DN_WIDTH), F32)
        dqe, dke, dve = [jnp.concatenate([zero, rows[1 + t], jnp.where(last, 0.0, nexts[1 + t])], axis=0)
                         for t in range(3)]

        def l2_bwd(x, dy):
            y, r = _l2n(x)
            return r * (dy - y * _rsum(dy * y))

        dsq = _cat([l2_bwd(x, d * (HD ** -0.5)) for x, d in zip(_heads(s[:, :DN_WIDTH], DN_HEADS, HD),
                                                                 _heads(dqe, DN_HEADS, HD))])
        dsk = _cat([l2_bwd(x, d) for x, d in zip(_heads(s[:, DN_WIDTH:2 * DN_WIDTH], DN_HEADS, HD),
                                                  _heads(dke, DN_HEADS, HD))])
        dc = _cat([dsq, dsk, dve]) * (sg * (1.0 + c * (1.0 - sg)))
        dpre = consts[3] * dc
        for j in range(3):
            dpre = dpre + consts[j] * pltpu.roll(dc, n - (3 - j), 0)
        dc_cur = dc[HALO:HALO + tl]
        dws = [_csum(dc_cur * pltpu.roll(ext, 3 - j, 0)[HALO:HALO + tl]) for j in range(3)]
        dws.append(_csum(dc_cur * ext[HALO:HALO + tl]))
        return [dpre[HALO:HALO + tl]], dws

    return rowwise(name, fn, length, tl, rows=[pre, dq, dk, dv], consts=list(taps), prevs=[pre],
                   nexts=[pre, dq, dk, dv], out_rows=[(3 * DN_WIDTH, BF16)],
                   out_accs=[((1, 3 * DN_WIDTH), F32)] * 4)


def _gate_consts():
    lane = jnp.arange(128)[:, None]
    col = jnp.arange(2 * DN_WIDTH)[None, :]
    sel = ((lane < 2 * DN_HEADS) & (col // HD == lane)).astype(F32)
    pick = ((col.T == lane.T * HD) & (lane.T < 2 * DN_HEADS)).astype(F32)
    return sel, pick


def _gate_math(ab, alog, dtb):
    z = ab + dtb
    g = -jnp.exp(alog) * _softplus(z)
    beta = _sigmoid(ab)
    return z, g, beta


def gates_fwd(name, ab, alog, dtb, sel, tl=ROWS):
    length = ab.shape[0]

    def fn(ctx, rows, consts, prevs, nexts):
        _, g, beta = _gate_math(rows[0], consts[0], consts[1])
        lane = _iota(g.shape, 1)
        small = jnp.where(lane < DN_HEADS, g, jnp.where(lane < 2 * DN_HEADS, beta, 0.0))
        big = dnn(small, consts[2], HI)
        return [big[:, :DN_WIDTH], big[:, DN_WIDTH:]], []

    return rowwise(name, fn, length, min(tl, length), rows=[ab], consts=[alog, dtb, sel],
                   out_rows=[(DN_WIDTH, F32)] * 2)


def gates_bwd(name, ab, dgb, dbb, alog, dtb, pick, tl=ROWS):
    length = ab.shape[0]

    def fn(ctx, rows, consts, prevs, nexts):
        z, g, beta = _gate_math(rows[0], consts[0], consts[1])
        dsmall = dnn(_cat([rows[1], rows[2]]), consts[2], HI)
        lane = _iota(g.shape, 1)
        is_a = lane < DN_HEADS
        da = jnp.where(is_a, dsmall * (-jnp.exp(consts[0])) * _sigmoid(z), 0.0)
        db = jnp.where((lane >= DN_HEADS) & (lane < 2 * DN_HEADS), dsmall * beta * (1.0 - beta), 0.0)
        return [da + db], [_csum(jnp.where(is_a, dsmall * g, 0.0)), _csum(da)]

    return rowwise(name, fn, length, min(tl, length), rows=[ab, dgb, dbb], consts=[alog, dtb, pick],
                   out_rows=[(128, BF16)], out_accs=[((1, 128), F32)] * 2)


CPS = 2


def _chunk_scan_rows(x, suffix=False):
    n = x.shape[0]
    rc = _iota(x.shape, 0) & (CHUNK - 1)
    sh = 1
    while sh < CHUNK:
        if suffix:
            x = x + jnp.where(rc < CHUNK - sh, pltpu.roll(x, n - sh, 0), 0.0)
        else:
            x = x + jnp.where(rc >= sh, pltpu.roll(x, sh, 0), 0.0)
        sh *= 2
    return x


def _tri_inv(a_list, eye, bd):
    def each(f, *ls):
        return [f(*xs) for xs in zip(*ls)]

    dg = [jnp.where(bd, a, 0.0) for a in a_list]
    lo = each(lambda a, d: a - d, a_list, dg)
    n1 = [-d for d in dg]
    n2 = each(lambda n: dnn(n, n, X3), n1)
    n4 = each(lambda n: dnn(n, n, X3), n2)
    td = each(lambda p, s: dnn(eye + p, eye + s, X3), n1, n2)
    n8 = each(lambda n: dnn(n, n, X3), n4)
    td = each(lambda t, n: dnn(t, eye + n, X3), td, n4)
    td = each(lambda t, n: dnn(t, eye + n, X3), td, n8)
    m = each(lambda t, l: dnn(t, l, X3), td, lo)
    m2 = each(lambda x: dnn(x, x, X3), m)
    x = each(lambda p, s: dnn(eye - p, eye + s, X3), m, m2)
    return each(lambda p, t: dnn(p, t, X3), x, td)


def _chunk_common(q, k, v, gcb, bb):
    egb = jnp.exp(gcb)
    gc64 = gcb[:, :CHUNK]
    ii, jj = _iota((CHUNK, CHUNK), 0), _iota((CHUNK, CHUNK), 1)
    incl, strict = ii >= jj, ii > jj
    decay = jnp.exp(jnp.where(incl, gc64 - gc64.T, -jnp.inf))
    kb = k * bb
    vb = v * bb
    kbe = kb * egb
    pq = dnt(jnp.concatenate([kb, q], axis=0), k, X3)
    ekb = jnp.exp(gcb[CHUNK - 1:CHUNK, :] - gcb)
    return dict(egb=egb, decay=decay, kb=kb, vb=vb, kbe=kbe, pm=pq[:CHUNK], qm=pq[CHUNK:], ekb=ekb,
                incl=incl, strict=strict, ii=ii, jj=jj)


def _chunk_head(vals, ci, h):
    return [v[ci * CHUNK:(ci + 1) * CHUNK, h * HD:(h + 1) * HD] for v in vals]


def _assemble(per_chunk):
    return jnp.concatenate([_cat(hs) for hs in per_chunk], axis=0)


def _assemble3(per_chunk):
    return jnp.stack([jnp.concatenate([per_chunk[ci][h] for ci in range(CPS)], axis=0) for h in range(DN_HEADS)])


def delta_prep_fwd(name, q, k, v, gb, bb):
    length = q.shape[0]

    def fn(ctx, rows, consts, prevs, nexts):
        gcb_all = _chunk_scan_rows(rows[3])
        vals = [rows[0], rows[1], rows[2], gcb_all, rows[4]]
        units = [(ci, h) for ci in range(CPS) for h in range(DN_HEADS)]
        ins = [_chunk_head(vals, ci, h) for ci, h in units]
        cs = [_chunk_common(*i) for i in ins]
        eye = (cs[0]["ii"] == cs[0]["jj"]).astype(F32)
        ts = _tri_inv([jnp.where(c["strict"], c["pm"] * c["decay"], 0.0) for c in cs], eye,
                      (cs[0]["ii"] >> 4) == (cs[0]["jj"] >> 4))
        uws = [dnn(t, _cat([c["vb"], c["kbe"]]), X3) for t, c in zip(ts, cs)]

        def grid2(xs):
            return [xs[ci * DN_HEADS:(ci + 1) * DN_HEADS] for ci in range(CPS)]

        return [_assemble(grid2([uw[:, :HD] for uw in uws])), _assemble(grid2([uw[:, HD:] for uw in uws])),
                _assemble(grid2([i[0] * c["egb"] for i, c in zip(ins, cs)])),
                _assemble(grid2([i[1] * c["ekb"] for i, c in zip(ins, cs)])), gcb_all,
                _assemble3(grid2([c["qm"] * c["decay"] for c in cs])), _assemble3(grid2(ts))], []

    return rowwise(name, fn, length, CHUNK * CPS, rows=[q, k, v, gb, bb],
                   out_rows=[(DN_WIDTH, F32)] * 5 + [(DN_HEADS, CHUNK, F32)] * 2)


def delta_prep_bwd(name, q, k, v, gb, bb, t3, du, dw, dqd, dkd, dattn3, dgl):
    length = q.shape[0]

    def fn(ctx, rows, consts, prevs, nexts):
        gcb_all = _chunk_scan_rows(rows[3])
        vals = [rows[0], rows[1], rows[2], gcb_all] + list(rows[4:9])
        t3v, da3v, dglv = rows[9], rows[10], rows[11]
        units = [(ci, h) for ci in range(CPS) for h in range(DN_HEADS)]
        ins = [_chunk_head(vals, ci, h) for ci, h in units]
        cs = [_chunk_common(*i[:5]) for i in ins]
        ts = [t3v[h][ci * CHUNK:(ci + 1) * CHUNK] for ci, h in units]
        dattns = [jnp.where(c["incl"], da3v[h][ci * CHUNK:(ci + 1) * CHUNK], 0.0) for (ci, h), c in zip(units, cs)]
        duws = [_cat([i[5], i[6]]) for i in ins]
        dvks = [dtn(t, d, X3) for t, d in zip(ts, duws)]
        dts = [dnt(d, _cat([c["vb"], c["kbe"]]), X3) for d, c in zip(duws, cs)]
        dts = [dnt(d, t, X3) for d, t in zip(dts, ts)]
        das = [jnp.where(c["strict"], -dtn(t, d, X3), 0.0) for c, t, d in zip(cs, ts, dts)]
        dpqs = [jnp.concatenate([da * c["decay"], dat * c["decay"]], axis=0) for da, dat, c in zip(das, dattns, cs)]
        dpqks = [dnn(d, i[1], X3) for d, i in zip(dpqs, ins)]
        dkps = [dtn(d, jnp.concatenate([c["kb"], i[0]], axis=0), X3) for d, c, i in zip(dpqs, cs, ins)]
        dqs, dks, dvs, dgcs, dbs = [], [], [], [], []
        for (ci, h), i, c, dvk, da, dattn, dpqk, dkp in zip(units, ins, cs, dvks, das, dattns, dpqks, dkps):
            qh, kh, vh, _, bh, _, _, dqdh, dkdh = i
            dvb, dkbe = dvk[:, :HD], dvk[:, HD:]
            dkb = dpqk[:CHUNK] + dkbe * c["egb"]
            c1 = _rsum(dkbe * c["kb"] + dqdh * qh) * c["egb"]
            c2 = _rsum(dkdh * kh) * c["ekb"]
            e = (da * c["pm"] + dattn * c["qm"]) * c["decay"]
            dgc = c1 - c2 + _rsum(e) - _rsum(e.T)
            dgl_tot = jnp.max(dglv[ci * 8:(ci + 1) * 8, h * HD:(h + 1) * HD], axis=0, keepdims=True) + _csum(c2)
            dgcs.append(dgc + jnp.where(_iota((CHUNK, HD), 0) == CHUNK - 1, dgl_tot, 0.0))
            dqs.append(dpqk[CHUNK:] + dqdh * c["egb"])
            dks.append(dkp + dkdh * c["ekb"] + dkb * bh)
            dvs.append(dvb * bh)
            dbs.append(jnp.broadcast_to(_rsum(dkb * kh + dvb * vh), (CHUNK, HD)))

        def grid2(xs):
            return [xs[ci * DN_HEADS:(ci + 1) * DN_HEADS] for ci in range(CPS)]

        return [_assemble(grid2(dqs)), _assemble(grid2(dks)), _assemble(grid2(dvs)),
                _chunk_scan_rows(_assemble(grid2(dgcs)), suffix=True), _assemble(grid2(dbs))], []

    return rowwise(name, fn, length, CHUNK * CPS,
                   rows=[q, k, v, gb, bb, du, dw, dqd, dkd, t3, dattn3, (dgl, DN_WIDTH, 0, 8 * CPS)],
                   out_rows=[(DN_WIDTH, F32)] * 5)


SCAN_CHUNKS = 4


def _scan_chunks(n):
    return SCAN_CHUNKS if n % SCAN_CHUNKS == 0 else 1


def delta_scan_fwd(name, qd, kd, u, w, attn3, gcb):
    length = qd.shape[0]
    n = length // CHUNK
    sc = _scan_chunks(n)
    row = pl.BlockSpec((sc * CHUNK, DN_WIDTH), lambda c: (c, 0))
    sq = pl.BlockSpec((DN_HEADS, sc * CHUNK, CHUNK), lambda c: (0, c, 0))

    def body(qd_ref, kd_ref, u_ref, w_ref, attn_ref, gc_ref, o_ref, vn_ref, st_ref, s_ref):
        c = pl.program_id(0)

        @pl.when(c == 0)
        def _():
            s_ref[...] = jnp.zeros_like(s_ref)

        heads = range(DN_HEADS)
        sls = [pl.ds(h * HD, HD) for h in heads]
        ss = [s_ref[h] for h in heads]
        for ci in range(sc):
            rs = pl.ds(ci * CHUNK, CHUNK)
            ws = [dnn(w_ref[rs, sl], s) for sl, s in zip(sls, ss)]
            qs = [dnn(qd_ref[rs, sl], s) for sl, s in zip(sls, ss)]
            vns = [u_ref[rs, sl] - x for sl, x in zip(sls, ws)]
            avs = [dnn(attn_ref[h, rs, :], vn) for h, vn in zip(heads, vns)]
            kvs = [dtn(kd_ref[rs, sl], vn) for sl, vn in zip(sls, vns)]
            for h, sl in zip(heads, sls):
                st_ref[ci, h] = ss[h]
                o_ref[rs, sl] = qs[h] + avs[h]
                vn_ref[rs, sl] = vns[h]
            ss = [s * jnp.exp(gc_ref[pl.ds(ci * CHUNK + CHUNK - 1, 1), sl]) + kv for s, sl, kv in zip(ss, sls, kvs)]
        for h in heads:
            s_ref[h] = ss[h]

    return pl.pallas_call(
        body, name=name, grid=(n // sc,), in_specs=[row, row, row, row, sq, row],
        out_specs=[row, row, pl.BlockSpec((sc, DN_HEADS, HD, HD), lambda c: (c, 0, 0, 0))],
        out_shape=[jax.ShapeDtypeStruct((length, DN_WIDTH), F32), jax.ShapeDtypeStruct((length, DN_WIDTH), F32),
                   jax.ShapeDtypeStruct((n, DN_HEADS, HD, HD), F32)],
        scratch_shapes=[pltpu.VMEM((DN_HEADS, HD, HD), F32)],
        compiler_params=_params(("arbitrary",)),
    )(qd, kd, u, w, attn3, gcb)


def delta_scan_bwd(name, do, qd, kd, w, attn3, vn, st, gcb):
    length = qd.shape[0]
    n = length // CHUNK
    sc = _scan_chunks(n)
    nb = n // sc
    row = pl.BlockSpec((sc * CHUNK, DN_WIDTH), lambda c: (nb - 1 - c, 0))
    sq = pl.BlockSpec((DN_HEADS, sc * CHUNK, CHUNK), lambda c: (0, nb - 1 - c, 0))
    stb = pl.BlockSpec((sc, DN_HEADS, HD, HD), lambda c: (nb - 1 - c, 0, 0, 0))
    glb = pl.BlockSpec((sc * 8, DN_WIDTH), lambda c: (nb - 1 - c, 0))

    def body(do_ref, qd_ref, kd_ref, w_ref, attn_ref, vn_ref, st_ref, gc_ref,
             dqd_ref, dkd_ref, du_ref, dw_ref, dattn_ref, dgl_ref, ds_ref):
        c = pl.program_id(0)

        @pl.when(c == 0)
        def _():
            ds_ref[...] = jnp.zeros_like(ds_ref)

        heads = range(DN_HEADS)
        sls = [pl.ds(h * HD, HD) for h in heads]
        dsns = [ds_ref[h] for h in heads]
        for ci in reversed(range(sc)):
            rs = pl.ds(ci * CHUNK, CHUNK)
            ss = [st_ref[ci, h] for h in heads]
            dos = [do_ref[rs, sl] for sl in sls]
            vns = [vn_ref[rs, sl] for sl in sls]
            dvns = [dtn(attn_ref[h, rs, :], d) for h, d in zip(heads, dos)]
            dvns = [x + dnn(kd_ref[rs, sl], dsn) for x, sl, dsn in zip(dvns, sls, dsns)]
            qdos = [dtn(qd_ref[rs, sl], d) for sl, d in zip(sls, dos)]
            for h, sl in zip(heads, sls):
                dattn_ref[h, rs, :] = dnt(dos[h], vns[h])
                dqd_ref[rs, sl] = dnt(dos[h], ss[h])
                dkd_ref[rs, sl] = dnt(vns[h], dsns[h])
                du_ref[rs, sl] = dvns[h]
            dws = [dnt(dvn, s) for dvn, s in zip(dvns, ss)]
            wdvs = [dtn(w_ref[rs, sl], dvn) for sl, dvn in zip(sls, dvns)]
            nxt = []
            for h, sl in zip(heads, sls):
                egl = jnp.exp(gc_ref[pl.ds(ci * CHUNK + CHUNK - 1, 1), sl])
                dw_ref[rs, sl] = -dws[h]
                dgl_ref[pl.ds(ci * 8, 8), sl] = jnp.broadcast_to(_csum(_rsum(dsns[h] * ss[h])) * egl, (8, HD))
                nxt.append(dsns[h] * egl + qdos[h] - wdvs[h])
            dsns = nxt
        for h in heads:
            ds_ref[h] = dsns[h]

    return pl.pallas_call(
        body, name=name, grid=(nb,), in_specs=[row, row, row, row, sq, row, stb, row],
        out_specs=[row, row, row, row, sq, glb],
        out_shape=[jax.ShapeDtypeStruct((length, DN_WIDTH), F32)] * 4
        + [jax.ShapeDtypeStruct((DN_HEADS, length, CHUNK), F32), jax.ShapeDtypeStruct((n * 8, DN_WIDTH), F32)],
        scratch_shapes=[pltpu.VMEM((DN_HEADS, HD, HD), F32)],
        compiler_params=_params(("arbitrary",)),
    )(do, qd, kd, w, attn3, vn, st, gcb)


def onorm_fwd(name, o, z, nw, tl=ROWS):
    length = o.shape[0]

    def fn(ctx, rows, consts, prevs, nexts):
        outs = []
        for oh, zh in zip(_heads(rows[0], DN_HEADS, HD), _heads(rows[1], DN_HEADS, HD)):
            r = lax.rsqrt(jnp.mean(oh * oh, axis=1, keepdims=True) + RMS_EPS)
            outs.append(oh * r * consts[0] * _silu(zh))
        return [_cat(outs)], []

    return rowwise(name, fn, length, min(tl, length), rows=[o, z], consts=[nw], out_rows=[(DN_WIDTH, BF16)])[0]


def onorm_bwd(name, o, z, d_on, nw, tl=ROWS):
    length = o.shape[0]

    def fn(ctx, rows, consts, prevs, nexts):
        dos, dzs = [], []
        dnw = jnp.zeros((1, HD), F32)
        for oh, zh, dh in zip(*[_heads(r, DN_HEADS, HD) for r in rows]):
            r = lax.rsqrt(jnp.mean(oh * oh, axis=1, keepdims=True) + RMS_EPS)
            y = oh * r
            sz = _silu(zh)
            t = dh * sz * consts[0]
            dos.append(r * (t - y * jnp.mean(t * y, axis=1, keepdims=True)))
            dzs.append(dh * y * consts[0] * _dsilu(zh))
            dnw = dnw + _csum(dh * y * sz)
        return [_cat(dos), _cat(dzs)], [dnw]

    return rowwise(name, fn, length, min(tl, length), rows=[o, z, d_on], consts=[nw],
                   out_rows=[(DN_WIDTH, F32), (DN_WIDTH, BF16)], out_accs=[((1, HD), F32)])


def merge_fwd(name, gates, ydn, ypool, tl=ROWS_WIDE):
    length = ydn.shape[0]

    def fn(ctx, rows, consts, prevs, nexts):
        gt = rows[0]
        return [_sigmoid(gt[:, :D_MODEL]) * rows[1] + _sigmoid(gt[:, D_MODEL:]) * rows[2]], []

    return rowwise(name, fn, length, min(tl, length), rows=[gates, ydn, ypool], out_rows=[(D_MODEL, BF16)])[0]


def merge_bwd(name, gates, ydn, ypool, dm, tl=ROWS_WIDE):
    length = ydn.shape[0]

    def fn(ctx, rows, consts, prevs, nexts):
        gt, yd, yp, d = rows
        sd, sp = _sigmoid(gt[:, :D_MODEL]), _sigmoid(gt[:, D_MODEL:])
        dgates = _cat([d * yd * sd * (1.0 - sd), d * yp * sp * (1.0 - sp)])
        return [d * sd, d * sp, dgates], []

    return rowwise(name, fn, length, min(tl, length), rows=[gates, ydn, ypool, dm],
                   out_rows=[(D_MODEL, BF16), (D_MODEL, BF16), (2 * D_MODEL, BF16)])


def _trailing_sums(ext, upto):
    s, sh = ext, 1
    while sh < upto:
        s = s + pltpu.roll(s, sh, 0)
        sh *= 2
    return s


def _leading_sums(ext, upto, n):
    s, sh = ext, 1
    while sh < upto:
        s = s + pltpu.roll(s, n - sh, 0)
        sh *= 2
    return s


def _pool_mixed(ctx, p, prev, tl):
    prevm = jnp.where(ctx.i > 0, prev, 0.0)
    t1 = (_row_index(ctx, tl) + 1).astype(F32)
    outs = []
    for gi, win in enumerate(POOL_WINDOWS):
        sl = slice(gi * HD, (gi + 1) * HD)
        ext = jnp.concatenate([prevm[:, sl], p[:, sl]], axis=0)
        mean = _trailing_sums(ext, win)[HALO:] / jnp.minimum(t1, float(win))
        outs.append(mean - p[:, sl])
    return outs


def pool_fwd(name, p, pool_w, scale, tl=ROWS):
    length = p.shape[0]
    tl = min(tl, length)

    def fn(ctx, rows, consts, prevs, nexts):
        mixed = _pool_mixed(ctx, rows[0], prevs[0], tl)
        y = _cat([dnn(m, consts[0][gi]) for gi, m in enumerate(mixed)])
        return [y * consts[1]], []

    return rowwise(name, fn, length, tl, rows=[p], consts=[pool_w, scale], prevs=[p],
                   out_rows=[(POOL_WIDTH, BF16)])[0]


def pool_bwd(name, p, dpo, pool_w, scale, tl=ROWS):
    length = p.shape[0]
    tl = min(tl, length)
    n = tl + HALO

    def fn(ctx, rows, consts, prevs, nexts):
        last = ctx.i == ctx.nblk - 1
        mixed = _pool_mixed(ctx, rows[0], prevs[0], tl)
        dext = jnp.concatenate([rows[1], jnp.where(last, 0.0, nexts[0])], axis=0)
        t1 = (_row_index(ctx, n) + 1).astype(F32)
        dps, dws, dscs = [], [], []
        for gi, win in enumerate(POOL_WINDOWS):
            sl = slice(gi * HD, (gi + 1) * HD)
            wg = consts[0][gi]
            dyraw = dext[:, sl] * consts[1][:, sl]
            dmix = dnt(dyraw, wg)
            dws.append(dtn(mixed[gi], dyraw[:tl]))
            dscs.append(_csum(rows[1][:, sl] * dnn(mixed[gi], wg)))
            lead = _leading_sums(dmix / jnp.minimum(t1, float(win)), win, n)
            dps.append(lead[:tl] - dmix[:tl])
        return [_cat(dps)], [jnp.stack(dws), _cat(dscs)]

    return rowwise(name, fn, length, tl, rows=[p, dpo], consts=[pool_w, scale], prevs=[p], nexts=[dpo],
                   out_rows=[(POOL_WIDTH, BF16)],
                   out_accs=[((len(POOL_WINDOWS), HD, HD), F32), ((1, POOL_WIDTH), F32)])


def _xa_probs(qh, kh):
    s = dnt(qh, kh) * (XA_HD ** -0.5)
    e = jnp.exp(s - jnp.max(s, axis=1, keepdims=True))
    return e / _rsum(e)


def xattn_fwd(name, qx, kx, vx, tl=ROWS):
    length = qx.shape[0]

    def fn(ctx, rows, consts, prevs, nexts):
        outs = [dnn(_xa_probs(qh, kh), vh) for qh, kh, vh in
                zip(_heads(rows[0], XA_HEADS, XA_HD), _heads(consts[0], XA_HEADS, XA_HD),
                    _heads(consts[1], XA_HEADS, XA_HD))]
        return [_cat(outs)], []

    return rowwise(name, fn, length, min(tl, length), rows=[qx], consts=[kx, vx], out_rows=[(D_MODEL, BF16)])[0]


def xattn_bwd(name, qx, dox, kx, vx, tl=ROWS):
    length = qx.shape[0]

    def fn(ctx, rows, consts, prevs, nexts):
        dqs, dks, dvs = [], [], []
        for qh, dh, kh, vh in zip(_heads(rows[0], XA_HEADS, XA_HD), _heads(rows[1], XA_HEADS, XA_HD),
                                  _heads(consts[0], XA_HEADS, XA_HD), _heads(consts[1], XA_HEADS, XA_HD)):
            pr = _xa_probs(qh, kh)
            dpr = dnt(dh, vh)
            ds = pr * (dpr - _rsum(dpr * pr)) * (XA_HD ** -0.5)
            dqs.append(dnn(ds, kh))
            dks.append(dtn(ds, qh))
            dvs.append(dtn(pr, dh))
        return [_cat(dqs)], [_cat(dks), _cat(dvs)]

    return rowwise(name, fn, length, min(tl, length), rows=[qx, dox], consts=[kx, vx],
                   out_rows=[(D_MODEL, BF16)], out_accs=[((N_MEM, D_MODEL), F32)] * 2)


def local_step(x, mem, target, w, io):
    sel, pick = _gate_consts()
    alog = jnp.pad(w["a_log"], ((0, 0), (0, 128 - DN_HEADS)))
    dtb = jnp.pad(w["dt_bias"], ((0, 0), (0, 128 - DN_HEADS)))

    f1, res1, w_down1 = ffn_fwd("ffn1", x, w["ffn1_w_gate"], w["ffn1_w_up"], io.ffn1_down, deps=io.rest_started())
    x1, r1 = ln_fwd("ln1", [(ALPHA, x), (0.5, f1)], w["ln1_g"], w["ln1_b"], deps=io.halfway("mixer", f1))
    w = dict(w, ffn1_w_down=w_down1, **io.weights("mixer", x1))
    taps = [w["conv_w"][j:j + 1] for j in range(4)]

    pre = mm("in_qkv", x1, w["in_qkv"], tb=True)
    z = mm("in_z", x1, w["in_z"], tb=True)
    gates = mm("in_gates", x1, w["in_gates"], tb=True)
    p = mm("in_p", x1, w["in_p"], tb=True)
    ab = mm("in_ab", x1, w["in_ab"], tb=True)
    q, k, v = conv_fwd("conv", pre, taps, deps=io.halfway("xa", pre))
    gb, bb = gates_fwd("gates", ab, alog, dtb, sel)
    u, wd_, qd, kd, gcb, attn3, t3 = delta_prep_fwd("dprep", q, k, v, gb, bb)
    o, vn, st = delta_scan_fwd("dscan", qd, kd, u, wd_, attn3, gcb)
    on = onorm_fwd("onorm", o, z, w["dn_norm_w"])
    ydn = mm("dn_branch", on, w["w_dn_branch"], tb=True)
    po = pool_fwd("pool", p, w["pool_w"], w["pool_scale"])
    ypool = mm("pool_branch", po, w["w_pool_branch"], tb=True)
    merged = merge_fwd("merge", gates, ydn, ypool)
    mix = mm("mix_out", merged, w["w_mix_out"])
    x2, r2 = ln_fwd("ln2", [(ALPHA, x1), (1.0, mix)], w["ln2_g"], w["ln2_b"])

    w = dict(w, **io.weights("xa", x2))
    m, _ = ln_fwd("ln_mem", [(1.0, mem)], w["mem_ln_g"], w["mem_ln_b"])
    qx = mm("xa_q", x2, w["xa_wq"], deps=io.halfway("ffn2", x2))
    kx = mm("xa_k", m, w["xa_wk"])
    vx = mm("xa_v", m, w["xa_wv"])
    ox = xattn_fwd("xattn", qx, kx, vx)
    xa = mm("xa_o", ox, w["xa_wo"])
    x3, r3 = ln_fwd("ln3", [(ALPHA, x2), (1.0, xa)], w["ln3_g"], w["ln3_b"])
    w = dict(w, **io.weights("ffn2", x3))

    f2, res2, _ = ffn_fwd("ffn2", x3, w["ffn2_w_gate"], w["ffn2_w_up"], w["ffn2_w_down"])
    dy4, r4, loss = ln_loss("ln4_loss", [(ALPHA, x3), (0.5, f2)], w["ln4_g"], w["ln4_b"], target)

    g = {}
    dr4, g["ln4_g"], g["ln4_b"] = ln_bwd("ln4_b", r4, [(1.0, dy4)], w["ln4_g"])
    dx3, g["ffn2_w_gate"], g["ffn2_w_up"], g["ffn2_w_down"] = ffn_bwd(
        "ffn2b", x3, res2, dr4, w["ffn2_w_gate"], w["ffn2_w_up"], w["ffn2_w_down"])
    dep = io.grads_out("ffn2", g)
    dr3, g["ln3_g"], g["ln3_b"] = ln_bwd("ln3_b", r3, [(ALPHA, dr4), (1.0, dx3)], w["ln3_g"], deps=dep)

    dox = mm("xa_do", dr3, w["xa_wo"], tb=True)
    g["xa_wo"] = mm("xa_dwo", ox, dr3, ta=True)
    dqx, dkx, dvx = xattn_bwd("xattn_b", qx, dox, kx, vx)
    g["xa_wq"] = mm("xa_dwq", x2, dqx, ta=True)
    dx2 = mm("xa_dx", dqx, w["xa_wq"], tb=True)
    g["xa_wk"] = mm("xa_dwk", m, dkx, ta=True)
    g["xa_wv"] = mm("xa_dwv", m, dvx, ta=True)
    dmm = mm("xa_dmk", dkx, w["xa_wk"], tb=True, deps=io.grads_out("xa", g))
    dmm = mm("xa_dmv", dvx, w["xa_wv"], tb=True, add=dmm)
    _, g["mem_ln_g"], g["mem_ln_b"] = ln_bwd("ln_mem_b", mem, [(1.0, dmm)], w["mem_ln_g"])
    dr2, g["ln2_g"], g["ln2_b"] = ln_bwd("ln2_b", r2, [(ALPHA, dr3), (1.0, dx2)], w["ln2_g"])
    io.grads_in("ffn2", dr2)

    dmerged = mm("mix_dm", dr2, w["w_mix_out"], tb=True)
    g["w_mix_out"] = mm("mix_dw", merged, dr2, ta=True)
    d_ydn, d_ypool, d_gates = merge_bwd("merge_b", gates, ydn, ypool, dmerged)
    g["w_dn_branch"] = mm("dn_dw", d_ydn, on, ta=True)
    d_on = mm("dn_dx", d_ydn, w["w_dn_branch"])
    g["w_pool_branch"] = mm("pool_dw", d_ypool, po, ta=True)
    d_po = mm("pool_dx", d_ypool, w["w_pool_branch"])
    dp, g["pool_w"], g["pool_scale"] = pool_bwd("pool_b", p, d_po, w["pool_w"], w["pool_scale"])
    d_o, dz, g["dn_norm_w"] = onorm_bwd("onorm_b", o, z, d_on, w["dn_norm_w"])
    dqd, dkd, du, dw_, dattn3, dgl = delta_scan_bwd("dscan_b", d_o, qd, kd, wd_, attn3, vn, st, gcb)
    dq, dk, dv, dgb, dbb = delta_prep_bwd("dprep_b", q, k, v, gb, bb, t3, du, dw_, dqd, dkd, dattn3, dgl)
    dpre, dc0, dc1, dc2, dc3 = conv_bwd("conv_b", pre, dq, dk, dv, taps)
    g["conv_w"] = jnp.concatenate([dc0, dc1, dc2, dc3], axis=0)
    d_ab, dalog, ddtb = gates_bwd("gates_b", ab, dgb, dbb, alog, dtb, pick)
    g["a_log"] = dalog[:, :DN_HEADS]
    g["dt_bias"] = ddtb[:, :DN_HEADS]
    g["in_qkv"] = mm("in_dwqkv", dpre, x1, ta=True)
    g["in_z"] = mm("in_dwz", dz, x1, ta=True)
    g["in_gates"] = mm("in_dwgates", d_gates, x1, ta=True)
    g["in_p"] = mm("in_dwp", dp, x1, ta=True)
    g["in_ab"] = mm("in_dwab", d_ab, x1, ta=True)
    io.grads_in("xa", g["in_ab"])
    dx1 = mm("in_dxqkv", dpre, w["in_qkv"], deps=io.grads_out("mixer", g))
    dx1 = mm("in_dxz", dz, w["in_z"], add=dx1)
    dx1 = mm("in_dxgates", d_gates, w["in_gates"], add=dx1)
    dx1 = mm("in_dxp", dp, w["in_p"], add=dx1)
    dx1 = mm("in_dxab", d_ab, w["in_ab"], add=dx1)
    dr1, g["ln1_g"], g["ln1_b"] = ln_bwd("ln1_b", r1, [(ALPHA, dr2), (1.0, dx1)], w["ln1_g"])

    def on_dwd(dwd):
        return io.small_out(dict(g, loss=loss[0, :1])) + io.grads_out("ffn1_d", dict(ffn1_w_down=dwd))

    def on_dwgu(dwg, dwu):
        return io.grads_out("ffn1_gu", dict(ffn1_w_gate=dwg, ffn1_w_up=dwu))

    grad_x, g["ffn1_w_gate"], g["ffn1_w_up"], g["ffn1_w_down"] = ffn_bwd(
        "ffn1b", x, res1, dr1, w["ffn1_w_gate"], w["ffn1_w_up"], w["ffn1_w_down"], on_dwd=on_dwd, on_dwgu=on_dwgu,
        also=(ALPHA, dr1))
    return loss, grad_x, g


WEIGHT_NAMES = ['ffn1_w_gate', 'ffn1_w_up', 'ffn1_w_down', 'ln1_g', 'ln1_b', 'w_in', 'conv_w', 'a_log', 'dt_bias',
                'dn_norm_w', 'w_dn_branch', 'pool_w', 'pool_scale', 'w_pool_branch', 'w_mix_out', 'ln2_g', 'ln2_b',
                'mem_ln_g', 'mem_ln_b', 'xa_wq', 'xa_wk', 'xa_wv', 'xa_wo', 'ln3_g', 'ln3_b', 'ffn2_w_gate',
                'ffn2_w_up', 'ffn2_w_down', 'ln4_g', 'ln4_b']
SHARDED = [
    ("ffn1_w_gate", "cols", (1024, 352)), ("ffn1_w_up", "cols", (1024, 352)), ("ffn1_w_down", "rows", (352, 1024)),
    ("w_in", "cols", (1024, 577)), ("conv_w", "flat", (4, 192)), ("w_dn_branch", "cols", (512, 128)),
    ("w_pool_branch", "cols", (512, 128)), ("w_mix_out", "rows", (128, 1024)), ("xa_wq", "rows", (128, 1024)),
    ("xa_wk", "rows", (128, 1024)), ("xa_wv", "rows", (128, 1024)), ("xa_wo", "rows", (128, 1024)),
    ("ffn2_w_gate", "cols", (1024, 352)), ("ffn2_w_up", "cols", (1024, 352)), ("ffn2_w_down", "rows", (352, 1024)),
]
REPLICATED = [n for n in WEIGHT_NAMES if n not in {s[0] for s in SHARDED}]
ROW_ALIGN = 16
ROW_BLOCKS = (512, 384, 352, 256, 192, 176, 128)
GROUPS = {"ffn1_gu": ("ffn1_w_gate", "ffn1_w_up"), "ffn1_d": ("ffn1_w_down",),
          "mixer": ("w_in", "conv_w", "w_dn_branch", "w_pool_branch", "w_mix_out"),
          "xa": ("xa_wq", "xa_wk", "xa_wv", "xa_wo"),
          "ffn2": ("ffn2_w_gate", "ffn2_w_up", "ffn2_w_down")}
W_IN_COLS = 577
W_IN_PIECES = (("in_qkv", 0, 1536), ("in_z", 1536, 2048), ("in_ab", 2048, 2056), ("in_p", 2056, 2568),
               ("in_gates", 2568, 4616))


def _round_up(n, m):
    return -(-n // m) * m


def _layout():
    off, table = 0, {}
    for name, form, shape in SHARDED:
        valid = {"rows": shape[0], "cols": shape[1], "flat": 2}[form]
        width = {"rows": shape[1], "cols": shape[0], "flat": shape[0] * shape[1]}[form]
        rows = _round_up(valid, ROW_ALIGN)
        table[name] = (off, rows, valid, width, form, shape)
        off += rows
    return table


LAYOUT = _layout()


def _group_span(names):
    base = LAYOUT[names[0]][0]
    rows = LAYOUT[names[-1]][0] + LAYOUT[names[-1]][1] - base
    while not any(rows % b == 0 for b in ROW_BLOCKS):
        rows += ROW_ALIGN
    return base, rows


def _row_block(rows):
    return _pick(rows, ROW_BLOCKS)


def _pad_block(blk, rows):
    return jnp.pad(blk, ((0, rows - blk.shape[0]), (0, LANES - blk.shape[1])))


def pack_weight_shards(shards, names):
    parts, used = [], 0
    for name in names:
        off, rows, valid, width, form, _ = LAYOUT[name]
        s = shards[name]
        if form == "flat":
            flat = s.reshape(1, -1)
            hi = flat.astype(BF16)
            blk = jnp.concatenate([hi, (flat - hi.astype(F32)).astype(BF16)], axis=0)
        else:
            blk = (s.T if form == "cols" else s).astype(BF16)
        parts.append(_pad_block(blk, rows))
        used += rows
    if _group_span(names)[1] > used:
        parts.append(jnp.zeros((_group_span(names)[1] - used, LANES), BF16))
    return jnp.concatenate(parts, axis=0)


IN_AB_ROWS = 128


def _w_in_segments(rows, first, last):
    segs = []
    for k in range(N_DEV):
        lo, hi = max(first, k * W_IN_COLS), min(last, (k + 1) * W_IN_COLS)
        if lo < hi:
            segs.append((k * rows + lo - k * W_IN_COLS, lo - first, hi - lo))
    return segs


def w_in_pieces(name, padded, rows):
    sizes = [IN_AB_ROWS if piece == "in_ab" else last - first for piece, first, last in W_IN_PIECES]

    def body(src_ref, *outs):
        for o_ref, (piece, first, last) in zip(outs, W_IN_PIECES):
            if piece == "in_ab":
                o_ref[...] = jnp.zeros_like(o_ref)
            for src, dst, count in _w_in_segments(rows, first, last):
                o_ref[pl.ds(dst, count), :] = src_ref[pl.ds(src, count), :]

    outs = pl.pallas_call(
        body, name=name, out_shape=[jax.ShapeDtypeStruct((n, LANES), padded.dtype) for n in sizes],
        compiler_params=pltpu.CompilerParams(vmem_limit_bytes=VMEM_LIMIT_BYTES),
    )(padded)
    return {piece: o for (piece, _, _), o in zip(W_IN_PIECES, outs)}


def unpack_full_weights(gathered, names):
    out, base = {}, _group_span(names)[0]
    for name in names:
        off, rows, valid, width, form, shape = LAYOUT[name]
        seg = gathered[:, off - base:off - base + rows]
        if form == "flat":
            flat = seg[:, 0, :width].astype(F32) + seg[:, 1, :width].astype(F32)
            out[name] = flat.reshape((N_DEV,) + shape).transpose(1, 0, 2).reshape(shape[0], N_DEV * shape[1])
        elif name == "w_in":
            out.update(w_in_pieces("w_in_pieces", seg.reshape(N_DEV * rows, LANES), rows))
        else:
            out[name] = seg[:, :valid, :width].reshape(N_DEV * valid, width)
    return out


def pack_full_grads(grads, names, me):
    wire, own, used = [], [], 0
    for name in names:
        off, rows, valid, width, form, shape = LAYOUT[name]
        if form == "flat":
            full = grads[name].reshape(shape[0], N_DEV, shape[1]).transpose(1, 0, 2).reshape(N_DEV, 1, width)
        elif name == "w_in":
            full = jnp.concatenate([grads[piece][:last - first] for piece, first, last in W_IN_PIECES], axis=0)
            full = full.reshape(N_DEV, valid, width)
        else:
            full = grads[name].reshape(N_DEV, valid, width)
        pad = ((0, rows - full.shape[1]), (0, LANES - width))
        wire.append(jnp.pad(full.astype(WIRE), ((0, 0),) + pad))
        own.append(jnp.pad(lax.dynamic_index_in_dim(full, me, 0, keepdims=False), pad))
        used += rows
    if _group_span(names)[1] > used:
        wire.append(jnp.zeros((N_DEV, _group_span(names)[1] - used, LANES), WIRE))
        own.append(jnp.zeros((_group_span(names)[1] - used, LANES), F32))
    return jnp.concatenate(wire, axis=1), jnp.concatenate(own, axis=0)


TRANSPOSED = ("ffn1_w_gate", "ffn1_w_up", "ffn2_w_gate", "ffn2_w_up", "w_in")


def unpack_grad_shards(packed, names):
    out, base = {}, _group_span(names)[0]
    for name in names:
        off, rows, valid, width, form, shape = LAYOUT[name]
        off -= base
        if form == "flat":
            out[name] = packed[off, :width].reshape(shape)
        elif name in TRANSPOSED:
            out[name] = packed[off:off + valid, :width]
        elif form == "cols":
            out[name] = packed[off:off + valid, :width].T
        else:
            out[name] = packed[off:off + valid, :width]
    return out


SMALL_SHAPES = {n: (1024,) for n in REPLICATED}
SMALL_SHAPES.update(pool_w=(4, 128, 128), pool_scale=(512,), dn_norm_w=(128,), a_log=(4,), dt_bias=(4,))


SMALL_SHAPES["loss"] = (1,)
SMALL_NAMES = REPLICATED + ["loss"]


def _small_layout():
    off, table = 0, {}
    for name in SMALL_NAMES:
        numel = 1
        for d in SMALL_SHAPES[name]:
            numel *= d
        rows = _round_up(-(-numel // LANES), 8)
        table[name] = (off, rows, numel)
        off += rows
    return table, off


SMALL_LAYOUT, SMALL_ROWS = _small_layout()


def _to_rows(flat, rows):
    return jnp.pad(flat, (0, rows * LANES - flat.shape[0])).reshape(rows, LANES)


def pack_small(values):
    return jnp.concatenate([_to_rows(values[name].reshape(-1), SMALL_LAYOUT[name][1]) for name in SMALL_NAMES], axis=0)


def unpack_small(packed):
    out = {}
    for name in SMALL_NAMES:
        off, rows, numel = SMALL_LAYOUT[name]
        out[name] = packed[off:off + rows].reshape(-1)[:numel].reshape(SMALL_SHAPES[name])
    return out


MESH = pl.DeviceIdType.MESH


def _position():
    return lax.axis_index("x"), lax.axis_index("y"), lax.axis_index("c")


def _other_chips(x, y):
    return [(1 - x, y), (x, 1 - y), (1 - x, 1 - y)]


def all_gather(name, block):
    rows, n = block.shape

    def body(x_ref, out_ref, send_sems, recv_sems, local_sem):
        x, y, c = _position()
        me, sibling = (x, y, c), (x, y, 1 - c)
        chips = _other_chips(x, y)

        def slot(px, py, pc):
            return out_ref.at[4 * px + 2 * py + pc]

        def copy(k, blk, to, src=None):
            return pltpu.make_async_remote_copy(
                src_ref=slot(*blk) if src is None else src, dst_ref=slot(*blk),
                send_sem=send_sems.at[k], recv_sem=recv_sems.at[k], device_id=to, device_id_type=MESH)

        mine = pltpu.make_async_copy(x_ref, slot(*me), local_sem)
        mine.start()
        first = [copy(0, me, sibling, src=x_ref)]
        first += [copy(1 + j, me, (*chip, c), src=x_ref) for j, chip in enumerate(chips)]
        for cp in first:
            cp.start()
        passed = [copy(4 + j, (*chip, c), sibling) for j, chip in enumerate(chips)]
        for j, chip in enumerate(chips):
            copy(1 + j, (*chip, c), me).wait_recv()
            passed[j].start()
        copy(0, sibling, me).wait_recv()
        for j, chip in enumerate(chips):
            copy(4 + j, (*chip, 1 - c), me).wait_recv()
        for cp in first + passed:
            cp.wait_send()
        mine.wait()

    return pl.pallas_call(
        body, name=name, out_shape=jax.ShapeDtypeStruct((N_DEV, rows, n), block.dtype),
        in_specs=[ANY], out_specs=ANY,
        scratch_shapes=[pltpu.SemaphoreType.DMA((7,)), pltpu.SemaphoreType.DMA((7,)), pltpu.SemaphoreType.DMA(())],
    )(block)


HBM = pl.BlockSpec(memory_space=pltpu.HBM)
SEM = pl.BlockSpec(memory_space=pltpu.SEMAPHORE)
EFFECT = pltpu.SideEffectType.DATAFLOW_SIDE_EFFECTING


def _remote(src, dst, send_sem, recv_sem, to):
    return pltpu.make_async_remote_copy(src_ref=src, dst_ref=dst, send_sem=send_sem, recv_sem=recv_sem,
                                        device_id=to, device_id_type=MESH)


def split_start(name, bufs, n, make_copies):
    nb = len(bufs)

    def body(*refs):
        for out_cp, _ in make_copies(refs[:nb], refs[nb:nb + n], refs[nb + n:nb + 2 * n]):
            out_cp.start()
        refs[-1][...] = jnp.zeros_like(refs[-1])

    outs = pl.pallas_call(
        body, name=name,
        out_shape=tuple([pltpu.SemaphoreType.DMA(())] * (2 * n)) + tuple(pltpu.HBM(b.shape, b.dtype) for b in bufs)
        + (jax.ShapeDtypeStruct((8, 128), F32),),
        in_specs=[HBM] * nb,
        out_specs=tuple([SEM] * (2 * n) + [HBM] * nb + [pl.BlockSpec(memory_space=pltpu.VMEM)]),
        input_output_aliases={i: 2 * n + i for i in range(nb)},
        compiler_params=pltpu.CompilerParams(has_side_effects=EFFECT),
    )(*[pltpu.with_memory_space_constraint(b, pltpu.HBM) for b in bufs])
    return list(outs[:2 * n]), list(outs[2 * n:2 * n + nb]), outs[-1]


def split_wait(name, bufs, sems, n, make_copies, after):
    nb = len(bufs)

    def body(*refs):
        for out_cp, in_cp in make_copies(refs[:nb], refs[nb:nb + n], refs[nb + n:nb + 2 * n]):
            out_cp.wait_send()
            in_cp.wait_recv()

    outs = pl.pallas_call(
        body, name=name, out_shape=tuple(pltpu.HBM(b.shape, b.dtype) for b in bufs),
        in_specs=[HBM] * nb + [SEM] * (2 * n) + [ANY], out_specs=tuple([HBM] * nb),
        input_output_aliases={i: i for i in range(nb)},
        compiler_params=pltpu.CompilerParams(has_side_effects=EFFECT),
    )(*bufs, *sems, after)
    return list(outs)


def _gather_stage1(refs, send, recv):
    src, land = refs
    x, y, c = _position()
    peers = [(x, y, 1 - c)] + [(*chip, c) for chip in _other_chips(x, y)]
    return [(_remote(src, land.at[4 * x + 2 * y + c], send[k], recv[k], p),
             _remote(src, land.at[4 * p[0] + 2 * p[1] + p[2]], send[k], recv[k], p)) for k, p in enumerate(peers)]


def _gather_stage2(refs, send, recv):
    (land,) = refs
    x, y, c = _position()
    out = []
    for j, (px, py) in enumerate(_other_chips(x, y)):
        mine, theirs = land.at[4 * px + 2 * py + c], land.at[4 * px + 2 * py + 1 - c]
        out.append((_remote(mine, mine, send[j], recv[j], (x, y, 1 - c)),
                    _remote(theirs, theirs, send[j], recv[j], (x, y, 1 - c))))
    return out


def _flips():
    return [(a, b, d) for a in (0, 1) for b in (0, 1) for d in (0, 1) if a | b | d]


def _gather_direct(refs, send, recv):
    src, land = refs
    x, y, c = _position()
    out = []
    for k, (fx, fy, fc) in enumerate(_flips()):
        p = (1 - x if fx else x, 1 - y if fy else y, 1 - c if fc else c)
        out.append((_remote(src, land.at[4 * x + 2 * y + c], send[k], recv[k], p),
                    _remote(src, land.at[4 * p[0] + 2 * p[1] + p[2]], send[k], recv[k], p)))
    return out


def _scatter_direct(refs, send, recv):
    sendbuf, land = refs
    x, y, c = _position()
    me = 4 * x + 2 * y + c
    out = []
    for k, (fx, fy, fc) in enumerate(_flips()):
        p = (1 - x if fx else x, 1 - y if fy else y, 1 - c if fc else c)
        peer = 4 * p[0] + 2 * p[1] + p[2]
        out.append((_remote(sendbuf.at[peer], land.at[me], send[k], recv[k], p),
                    _remote(sendbuf.at[peer], land.at[peer], send[k], recv[k], p)))
    return out


def _own_plus_slots(name, own, landed):
    n, rows, _ = landed.shape
    tr = _row_block(rows)

    def body(g_ref, l_ref, o_ref):
        acc = g_ref[...]
        for j in range(n):
            acc = acc + l_ref[j].astype(F32)
        o_ref[...] = acc

    return pl.pallas_call(
        body, name=name, grid=(rows // tr,),
        in_specs=[pl.BlockSpec((tr, LANES), lambda i: (i, 0)), pl.BlockSpec((n, tr, LANES), lambda i: (0, i, 0))],
        out_specs=pl.BlockSpec((tr, LANES), lambda i: (i, 0)),
        out_shape=jax.ShapeDtypeStruct((rows, LANES), F32), compiler_params=_params(("parallel",)),
    )(own, landed)


def _sum_slots(name, stack):
    n, rows, _ = stack.shape

    def body(s_ref, o_ref):
        acc = s_ref[0]
        for j in range(1, n):
            acc = acc + s_ref[j]
        o_ref[...] = acc

    return pl.pallas_call(
        body, name=name, in_specs=[pl.BlockSpec(stack.shape, lambda: (0, 0, 0))],
        out_specs=pl.BlockSpec((rows, LANES), lambda: (0, 0)), out_shape=jax.ShapeDtypeStruct((rows, LANES), F32),
    )(stack)


def adamw(name, w, g, m, v):
    shape = w.shape
    last = shape[-1]
    w2, g2, m2, v2 = [a.reshape(-1, last) for a in (w, g, m, v)]
    rows = w2.shape[0]
    tr = _pick(rows, (256, 176, 128))

    def body(w_ref, g_ref, m_ref, v_ref, d_ref, nm_ref, nv_ref):
        gg = g_ref[...]
        nm = ADAM_B1 * m_ref[...] + (1.0 - ADAM_B1) * gg
        nv = ADAM_B2 * v_ref[...] + (1.0 - ADAM_B2) * (gg * gg)
        m_hat = nm / (1.0 - ADAM_B1 ** ADAM_STEP)
        v_hat = nv / (1.0 - ADAM_B2 ** ADAM_STEP)
        d_ref[...] = -ADAM_LR * (m_hat / (jnp.sqrt(v_hat) + ADAM_EPS) + ADAM_WD * w_ref[...])
        nm_ref[...] = nm
        nv_ref[...] = nv

    spec = pl.BlockSpec((tr, last), lambda i: (i, 0))
    outs = pl.pallas_call(
        body, name=name, grid=(rows // tr,), in_specs=[spec] * 4, out_specs=[spec] * 3,
        out_shape=[jax.ShapeDtypeStruct((rows, last), F32)] * 3, compiler_params=_params(("parallel",)),
    )(w2, g2, m2, v2)
    return [o.reshape(shape) for o in outs]


def _landing(block_shape, dtype, own):
    x, y, c = _position()
    return lax.dynamic_update_slice(lax.empty((N_DEV,) + block_shape, dtype), own[None], (4 * x + 2 * y + c, 0, 0))


class _Exchanges:
    def __init__(self, shards):
        self.shards = shards
        self.pending = {}
        self.reduced = {}

    def first_weights(self):
        names = GROUPS["ffn1_gu"]
        return unpack_full_weights(all_gather("ag_ffn1_gu", pack_weight_shards(self.shards, names)), names)

    def rest_started(self):
        tokens = []
        block = pack_weight_shards(self.shards, GROUPS["ffn1_d"])
        sems, bufs, token = split_start("ag_ffn1_d_s", [block, _landing(block.shape, block.dtype, block)], N_DEV - 1,
                                        _gather_direct)
        self.pending["ffn1_d"] = (sems, bufs)
        tokens.append(token)
        for key in ("mixer", "xa", "ffn2"):
            block = pack_weight_shards(self.shards, GROUPS[key])
            sems, bufs, token = split_start(f"ag_{key}_s1", [block, _landing(block.shape, block.dtype, block)], 4,
                                            _gather_stage1)
            self.pending[key] = (sems, bufs)
            tokens.append(token)
        return tuple(tokens)

    def ffn1_down(self, after):
        sems, bufs = self.pending.pop("ffn1_d")
        _, gathered = split_wait("ag_ffn1_d_w", bufs, sems, N_DEV - 1, _gather_direct, after)
        return unpack_full_weights(gathered, GROUPS["ffn1_d"])["ffn1_w_down"]

    def halfway(self, key, after):
        sems, bufs = self.pending.pop(key)
        _, land = split_wait(f"ag_{key}_w1", bufs, sems, 4, _gather_stage1, after)
        sems, bufs, token = split_start(f"ag_{key}_s2", [land], 3, _gather_stage2)
        self.pending[key] = (sems, bufs)
        return (token,)

    def weights(self, key, after):
        sems, bufs = self.pending.pop(key)
        (gathered,) = split_wait(f"ag_{key}_w2", bufs, sems, 3, _gather_stage2, after)
        return unpack_full_weights(gathered, GROUPS[key])

    def grads_out(self, key, grads):
        x, y, c = _position()
        wire, own = pack_full_grads(grads, GROUPS[key], 4 * x + 2 * y + c)
        land = _landing(wire.shape[1:], WIRE, jnp.zeros(wire.shape[1:], WIRE))
        sems, bufs, token = split_start(f"rs_{key}_start", [wire, land], N_DEV - 1, _scatter_direct)
        self.pending[key] = (sems, bufs, own)
        return (token,)

    def grads_in(self, key, after):
        sems, bufs, own = self.pending.pop(key)
        _, landed = split_wait(f"rs_{key}_wait", bufs, sems, N_DEV - 1, _scatter_direct, after)
        self.reduced.update(unpack_grad_shards(_own_plus_slots(f"rs_{key}_sum", own, landed), GROUPS[key]))

    def small_out(self, values):
        block = pack_small(values)
        sems, bufs, token = split_start("ag_small_s", [block, _landing(block.shape, block.dtype, block)], N_DEV - 1,
                                        _gather_direct)
        self.pending["small"] = (sems, bufs)
        return (token,)

    def small_in(self, after):
        sems, bufs = self.pending.pop("small")
        _, gathered = split_wait("ag_small_w", bufs, sems, N_DEV - 1, _gather_direct, after)
        return unpack_small(_sum_slots("small_sum", gathered))


def kernel(x, mem, ffn1_w_gate, ffn1_w_up, ffn1_w_down, ln1_g, ln1_b, w_in, conv_w, a_log, dt_bias, dn_norm_w, w_dn_branch, pool_w, pool_scale, w_pool_branch, w_mix_out, ln2_g, ln2_b, mem_ln_g, mem_ln_b, xa_wq, xa_wk, xa_wv, xa_wo, ln3_g, ln3_b, ffn2_w_gate, ffn2_w_up, ffn2_w_down, ln4_g, ln4_b, loss_target, m_ffn1_w_gate, m_ffn1_w_up, m_ffn1_w_down, m_ln1_g, m_ln1_b, m_w_in, m_conv_w, m_a_log, m_dt_bias, m_dn_norm_w, m_w_dn_branch, m_pool_w, m_pool_scale, m_w_pool_branch, m_w_mix_out, m_ln2_g, m_ln2_b, m_mem_ln_g, m_mem_ln_b, m_xa_wq, m_xa_wk, m_xa_wv, m_xa_wo, m_ln3_g, m_ln3_b, m_ffn2_w_gate, m_ffn2_w_up, m_ffn2_w_down, m_ln4_g, m_ln4_b, v_ffn1_w_gate, v_ffn1_w_up, v_ffn1_w_down, v_ln1_g, v_ln1_b, v_w_in, v_conv_w, v_a_log, v_dt_bias, v_dn_norm_w, v_w_dn_branch, v_pool_w, v_pool_scale, v_w_pool_branch, v_w_mix_out, v_ln2_g, v_ln2_b, v_mem_ln_g, v_mem_ln_b, v_xa_wq, v_xa_wk, v_xa_wv, v_xa_wo, v_ln3_g, v_ln3_b, v_ffn2_w_gate, v_ffn2_w_up, v_ffn2_w_down, v_ln4_g, v_ln4_b):
    given = dict(locals())
    shards = {n: given[n] for n in WEIGHT_NAMES}
    io = _Exchanges({n: shards[n][0] for n, _, _ in SHARDED})
    w = io.first_weights()
    for n in REPLICATED:
        w[n] = shards[n][0] if n == "pool_w" else shards[n]
    loss_part, grad_x, g = local_step(x[0], mem[0], loss_target[0], w, io)

    grad, updates = {}, {}

    def update(names, reduced):
        for n in names:
            if n in TRANSPOSED:
                outs = adamw("adamw_" + n, shards[n][0].T, reduced[n], given["m_" + n][0].T, given["v_" + n][0].T)
                grad[n], updates[n] = reduced[n].T[None], [o.T[None] for o in outs]
            else:
                grad[n] = reduced[n].reshape(shards[n].shape)
                updates[n] = adamw("adamw_" + n, shards[n], grad[n], given["m_" + n], given["v_" + n])
        return updates[names[-1]][0]

    update(GROUPS["ffn2"] + GROUPS["xa"], io.reduced)
    io.grads_in("mixer", grad_x)
    done = update(GROUPS["mixer"], io.reduced)
    small = io.small_in(done)
    loss = small.pop("loss")[0]
    done = update(REPLICATED, small)
    io.grads_in("ffn1_d", done)
    done = update(GROUPS["ffn1_d"], io.reduced)
    io.grads_in("ffn1_gu", done)
    update(GROUPS["ffn1_gu"], io.reduced)
    return (loss, grad_x[None], *[grad[n] for n in WEIGHT_NAMES], *[updates[n][0] for n in WEIGHT_NAMES],
            *[updates[n][1] for n in WEIGHT_NAMES], *[updates[n][2] for n in WEIGHT_NAMES])
```

```python
import functools

import jax
import jax.numpy as jnp
from jax import lax
from jax.experimental import pallas as pl
from jax.experimental.pallas import tpu as pltpu

F32 = jnp.float32
BF16 = jnp.bfloat16
MMD = BF16
WIRE = BF16
HI = lax.Precision.HIGHEST
X3 = lax.Precision.HIGH
VMEM_LIMIT_BYTES = 48 * 1024 * 1024

D_MODEL = 1024
D_FF = 2816
CHUNK = 64
N_MEM = 256
DN_HEADS = 4
HD = 128
DN_WIDTH = 512
POOL_WINDOWS = (2, 4, 8, 16)
POOL_WIDTH = 512
XA_HEADS = 4
XA_HD = 256
LN_EPS = 1e-5
RMS_EPS = 1e-6
L2_EPS = 1e-6
ALPHA = 2.0 ** 0.25
HALO = 16
ROWS = 512
ROWS_WIDE = 256

ADAM_LR = 0.001
ADAM_B1 = 0.9
ADAM_B2 = 0.999
ADAM_EPS = 1e-08
ADAM_WD = 0.01
ADAM_STEP = 10

N_DEV = 8
LANES = 1024
ANY = pl.BlockSpec(memory_space=pl.ANY)


def _dot(a, b, ca, cb, prec):
    dn = (((ca,), (cb,)), ((), ()))
    if prec is not None:
        return lax.dot_general(a.astype(F32), b.astype(F32), dn, precision=prec, preferred_element_type=F32)
    return lax.dot_general(a.astype(MMD), b.astype(MMD), dn, preferred_element_type=F32)


def dnn(a, b, prec=None):
    return _dot(a, b, 1, 0, prec)


def dnt(a, b, prec=None):
    return _dot(a, b, 1, 1, prec)


def dtn(a, b, prec=None):
    return _dot(a, b, 0, 0, prec)


def _sigmoid(x):
    return jax.nn.sigmoid(x)


def _silu(x):
    return x * _sigmoid(x)


def _dsilu(x):
    s = _sigmoid(x)
    return s * (1.0 + x * (1.0 - s))


def _softplus(x):
    return jnp.maximum(x, 0.0) + jnp.log1p(jnp.exp(-jnp.abs(x)))


def _iota(shape, dim):
    return lax.broadcasted_iota(jnp.int32, shape, dim)


def _rsum(x):
    return jnp.sum(x, axis=1, keepdims=True)


def _csum(x):
    return jnp.sum(x, axis=0, keepdims=True)


def _pick(n, cands):
    for c in cands:
        if n % c == 0:
            return c
    return n


def _params(sem):
    return pltpu.CompilerParams(dimension_semantics=sem, vmem_limit_bytes=VMEM_LIMIT_BYTES)


MM_TILE_SIZES = (4096, 2816, 2048, 1536, 1408, 1024, 768, 512, 384, 256, 128)
MM_VMEM_BUDGET = 36 * 1024 * 1024
HBM_BYTES_PER_US = 3.0e6
GRID_STEP_US = 0.35


def _mm_tiles(m, n, kc, a_bytes, b_bytes, o_bytes):
    def sizes(d):
        return [d] if d <= 512 else [t for t in MM_TILE_SIZES if d % t == 0]

    best = None
    for tm in sizes(m):
        for tn in sizes(n):
            for tk in sizes(kc):
                vmem = 2 * (tm * tk * a_bytes + tk * tn * b_bytes + tm * tn * o_bytes) + tm * tn * 4
                if vmem > MM_VMEM_BUDGET:
                    continue
                steps = (m // tm) * (n // tn) * (kc // tk)
                traffic = m * kc * a_bytes * (n // tn) + kc * n * b_bytes * (m // tm) + m * n * o_bytes
                edge = tm * tk * a_bytes + tk * tn * b_bytes + tm * tn * o_bytes
                cost = (traffic + edge) / HBM_BYTES_PER_US + steps * GRID_STEP_US
                if best is None or cost < best[0]:
                    best = (cost, tm, tn, tk)
    return best[1:]


def mm(name, a, b, *, ta=False, tb=False, out_dtype=F32, add=None, scale=None, deps=()):
    adds = [] if add is None else (list(add) if isinstance(add, (list, tuple)) else [(1.0, add)])
    if ta:
        kc, m = a.shape
    else:
        m, kc = a.shape
    if tb:
        n, kb = b.shape
    else:
        kb, n = b.shape
    assert kc == kb, (name, a.shape, b.shape)
    tm, tn, tk = _mm_tiles(m, n, kc, a.dtype.itemsize, b.dtype.itemsize,
                           jnp.dtype(out_dtype).itemsize * (1 + len(adds)))
    nk = kc // tk
    grid = (m // tm, n // tn, nk)
    a_spec = pl.BlockSpec((tk, tm), lambda i, j, k: (k, i)) if ta else pl.BlockSpec((tm, tk), lambda i, j, k: (i, k))
    b_spec = pl.BlockSpec((tn, tk), lambda i, j, k: (j, k)) if tb else pl.BlockSpec((tk, tn), lambda i, j, k: (k, j))
    o_spec = pl.BlockSpec((tm, tn), lambda i, j, k: (i, j))
    ca, cb = (0 if ta else 1), (1 if tb else 0)

    def body(*refs):
        a_ref, b_ref = refs[0], refs[1]
        o_ref = refs[-1] if nk == 1 else refs[-2]
        k = pl.program_id(2)
        part = _dot(a_ref[...], b_ref[...], ca, cb, None)

        def finish(r):
            if scale is not None:
                r = r * scale
            for (coef, _), add_ref in zip(adds, refs[2:2 + len(adds)]):
                r = r + (add_ref[...] if coef == 1.0 else coef * add_ref[...])
            o_ref[...] = r.astype(o_ref.dtype)

        if nk == 1:
            finish(part)
            return
        acc_ref = refs[-1]

        @pl.when(k == 0)
        def _():
            acc_ref[...] = part

        if nk > 2:
            @pl.when((k > 0) & (k < nk - 1))
            def _():
                acc_ref[...] += part

        @pl.when(k == nk - 1)
        def _():
            finish(acc_ref[...] + part)

    ins = [a, b] + [t for _, t in adds] + list(deps)
    specs = [a_spec, b_spec] + [o_spec] * len(adds) + [ANY] * len(deps)
    return pl.pallas_call(
        body, name=name, grid=grid, in_specs=specs, out_specs=o_spec,
        out_shape=jax.ShapeDtypeStruct((m, n), out_dtype),
        scratch_shapes=[pltpu.VMEM((tm, tn), F32)] if nk > 1 else [],
        compiler_params=_params(("parallel", "parallel", "arbitrary")),
    )(*ins)


def mm_sum(name, pairs, deps=()):
    m, n = pairs[0][0].shape[0], pairs[0][1].shape[1]
    tm = min(512, m)
    np_ = len(pairs)

    def body(*refs):
        acc = dnn(refs[0][...], refs[1][...])
        for p in range(1, np_):
            acc = acc + dnn(refs[2 * p][...], refs[2 * p + 1][...])
        refs[-1][...] = acc

    specs, ins = [], []
    for a, b in pairs:
        specs += [pl.BlockSpec((tm, a.shape[1]), lambda i: (i, 0)), pl.BlockSpec(b.shape, lambda i: (0, 0))]
        ins += [a, b]
    return pl.pallas_call(
        body, name=name, grid=(m // tm,), in_specs=specs + [ANY] * len(deps),
        out_specs=pl.BlockSpec((tm, n), lambda i: (i, 0)), out_shape=jax.ShapeDtypeStruct((m, n), F32),
        compiler_params=_params(("parallel",)),
    )(*ins, *deps)


class _Ctx:
    def __init__(self, i, nblk, tl):
        self.i, self.nblk, self.tl = i, nblk, tl


def _norm_item(it):
    if isinstance(it, tuple):
        a, w, j = it[:3]
        rows = it[3] if len(it) > 3 else None
        return a, w, j, rows
    return it, it.shape[-1], 0, None


def rowwise(name, fn, length, tl, *, rows=(), consts=(), prevs=(), nexts=(), out_rows=(), out_accs=(), deps=()):
    nblk = length // tl
    hb = tl // HALO
    nhalo = length // HALO
    arrays, specs = [], []
    for it in rows:
        a, w, j, r = _norm_item(it)
        if a.ndim == 3:
            specs.append(pl.BlockSpec((a.shape[0], tl, w), lambda i, j=j: (0, i, j)))
        else:
            specs.append(pl.BlockSpec((r or tl, w), lambda i, j=j: (i, j)))
        arrays.append(a)
    for a in consts:
        specs.append(pl.BlockSpec(a.shape, lambda i, nd=a.ndim: (0,) * nd))
        arrays.append(a)
    for it in prevs:
        a, w, j, _ = _norm_item(it)
        specs.append(pl.BlockSpec((HALO, w), lambda i, j=j: (jnp.maximum(i * hb - 1, 0), j)))
        arrays.append(a)
    for it in nexts:
        a, w, j, _ = _norm_item(it)
        specs.append(pl.BlockSpec((HALO, w), lambda i, j=j: (jnp.minimum((i + 1) * hb, nhalo - 1), j)))
        arrays.append(a)
    out_shape, out_specs = [], []
    for spec in out_rows:
        if len(spec) == 3:
            h, w, dt = spec
            out_shape.append(jax.ShapeDtypeStruct((h, length, w), dt))
            out_specs.append(pl.BlockSpec((h, tl, w), lambda i: (0, i, 0)))
        else:
            w, dt = spec
            out_shape.append(jax.ShapeDtypeStruct((length, w), dt))
            out_specs.append(pl.BlockSpec((tl, w), lambda i: (i, 0)))
    for shape, dt in out_accs:
        out_shape.append(jax.ShapeDtypeStruct(shape, dt))
        out_specs.append(pl.BlockSpec(shape, lambda i, nd=len(shape): (0,) * nd))
    n_r, n_c, n_p, n_n = len(rows), len(consts), len(prevs), len(nexts)
    n_in = n_r + n_c + n_p + n_n
    n_or = len(out_rows)
    arrays, specs = arrays + list(deps), specs + [ANY] * len(deps)

    def body(*refs):
        i = pl.program_id(0)
        vals = [r[...] for r in refs[:n_in]]
        outs = refs[n_in + len(deps):]
        ctx = _Ctx(i, nblk, tl)
        ro, ao = fn(ctx, vals[:n_r], vals[n_r:n_r + n_c], vals[n_r + n_c:n_r + n_c + n_p], vals[n_r + n_c + n_p:])
        for r, v in zip(outs[:n_or], ro, strict=True):
            r[...] = v.astype(r.dtype)
        for r, v in zip(outs[n_or:], ao, strict=True):
            @pl.when(i == 0)
            def _(r=r, v=v):
                r[...] = v.astype(r.dtype)

            @pl.when(i > 0)
            def _(r=r, v=v):
                r[...] += v.astype(r.dtype)

    res = pl.pallas_call(
        body, name=name, grid=(nblk,), in_specs=specs, out_specs=out_specs, out_shape=out_shape,
        compiler_params=_params(("arbitrary",) if out_accs else ("parallel",)),
    )(*arrays)
    return res


def _heads(x, n, w):
    return [x[:, h * w:(h + 1) * w] for h in range(n)]


def _cat(xs):
    return jnp.concatenate(xs, axis=1)


def _row_index(ctx, nrows, offset=0):
    return ctx.i * ctx.tl + offset + _iota((nrows, 1), 0)


def _ln_stats(r):
    mu = jnp.mean(r, axis=1, keepdims=True)
    d = r - mu
    var = jnp.mean(d * d, axis=1, keepdims=True)
    rstd = lax.rsqrt(var + LN_EPS)
    return d * rstd, rstd


def ln_fwd(name, terms, g, b, tl=ROWS, deps=()):
    coefs = [c for c, _ in terms]
    length = terms[0][1].shape[0]

    def fn(ctx, rows, consts, prevs, nexts):
        r = sum(c * t for c, t in zip(coefs, rows))
        xh, _ = _ln_stats(r)
        return [xh * consts[0] + consts[1], r], []

    return rowwise(name, fn, length, min(tl, length), rows=[t for _, t in terms], consts=[g, b],
                   out_rows=[(D_MODEL, F32), (D_MODEL, F32)], deps=deps)


def ln_bwd(name, r, terms, g, tl=ROWS, deps=()):
    coefs = [c for c, _ in terms]
    length = r.shape[0]

    def fn(ctx, rows, consts, prevs, nexts):
        xh, rstd = _ln_stats(rows[0])
        dy = sum(c * t for c, t in zip(coefs, rows[1:]))
        dxh = dy * consts[0]
        dr = rstd * (dxh - jnp.mean(dxh, axis=1, keepdims=True) - xh * jnp.mean(dxh * xh, axis=1, keepdims=True))
        return [dr], [_csum(dy * xh), _csum(dy)]

    return rowwise(name, fn, length, min(tl, length), rows=[r] + [t for _, t in terms], consts=[g],
                   out_rows=[(D_MODEL, F32)], out_accs=[((1, D_MODEL), F32), ((1, D_MODEL), F32)], deps=deps)


def ln_loss(name, terms, g, b, target, tl=ROWS):
    coefs = [c for c, _ in terms]
    length = target.shape[0]
    nt = len(terms)

    def fn(ctx, rows, consts, prevs, nexts):
        r = sum(c * t for c, t in zip(coefs, rows[:nt]))
        xh, _ = _ln_stats(r)
        err = xh * consts[0] + consts[1] - rows[nt]
        tot = _csum(_rsum(err * err)) * (0.5 / D_MODEL)
        return [err * (1.0 / D_MODEL), r], [jnp.broadcast_to(tot, (1, 128))]

    return rowwise(name, fn, length, min(tl, length), rows=[t for _, t in terms] + [target], consts=[g, b],
                   out_rows=[(D_MODEL, F32), (D_MODEL, F32)], out_accs=[((1, 128), F32)])


def _ffn_blocks(length):
    return min(512, length), D_FF // 2


def ffn_gate_up_act(name, x, wg, wu, deps=()):
    length = x.shape[0]
    tm, tn = _ffn_blocks(length)

    def body(x_ref, wg_ref, wu_ref, *rest):
        hg_ref, hu_ref, act_ref = rest[-3:]
        xb = x_ref[...].astype(MMD)
        hg = dnt(xb, wg_ref[...])
        hu = dnt(xb, wu_ref[...])
        hg_ref[...] = hg
        hu_ref[...] = hu
        act_ref[...] = (_silu(hg) * hu).astype(act_ref.dtype)

    row = pl.BlockSpec((tm, D_MODEL), lambda i, j: (i, 0))
    wsp = pl.BlockSpec((tn, D_MODEL), lambda i, j: (j, 0))
    osp = pl.BlockSpec((tm, tn), lambda i, j: (i, j))
    return pl.pallas_call(
        body, name=name, grid=(length // tm, D_FF // tn), in_specs=[row, wsp, wsp] + [ANY] * len(deps),
        out_specs=[osp] * 3,
        out_shape=[jax.ShapeDtypeStruct((length, D_FF), F32)] * 2 + [jax.ShapeDtypeStruct((length, D_FF), BF16)],
        compiler_params=_params(("parallel", "parallel")),
    )(x, wg, wu, *deps)


def ffn_dact(name, dr, wd, hg, hu, deps=()):
    length = dr.shape[0]
    tm, tn = _ffn_blocks(length)

    def body(dr_ref, wd_ref, hg_ref, hu_ref, *rest):
        dhg_ref, dhu_ref = rest[-2:]
        da = 0.5 * dnt(dr_ref[...], wd_ref[...])
        g = hg_ref[...]
        s = _sigmoid(g)
        dhg_ref[...] = (da * hu_ref[...] * (s * (1.0 + g * (1.0 - s)))).astype(dhg_ref.dtype)
        dhu_ref[...] = (da * (g * s)).astype(dhu_ref.dtype)

    row = pl.BlockSpec((tm, D_MODEL), lambda i, j: (i, 0))
    wsp = pl.BlockSpec((tn, D_MODEL), lambda i, j: (j, 0))
    osp = pl.BlockSpec((tm, tn), lambda i, j: (i, j))
    return pl.pallas_call(
        body, name=name, grid=(length // tm, D_FF // tn), in_specs=[row, wsp, osp, osp] + [ANY] * len(deps),
        out_specs=[osp] * 2, out_shape=[jax.ShapeDtypeStruct((length, D_FF), BF16)] * 2,
        compiler_params=_params(("parallel", "parallel")),
    )(dr, wd, hg, hu, *deps)


def ffn_fwd(tag, x, wg, wu, wd, deps=()):
    hg, hu, act = ffn_gate_up_act(tag + "_gate_up", x, wg, wu, deps)
    if callable(wd):
        wd = wd(act)
    f = mm(tag + "_down", act, wd)
    return f, (hg, hu, act), wd


def ffn_bwd(tag, x, res, dr, wg, wu, wd, deps=(), on_dwd=None, on_dwgu=None, also=None):
    hg, hu, act = res
    dwd = mm(tag + "_dwd", act, dr, ta=True, scale=0.5, deps=deps)
    dhg, dhu = ffn_dact(tag + "_dact", dr, wd, hg, hu, deps=on_dwd(dwd) if on_dwd else ())
    dwg = mm(tag + "_dwg", dhg, x, ta=True)
    dwu = mm(tag + "_dwu", dhu, x, ta=True)
    dx = mm(tag + "_dxg", dhg, wg, deps=on_dwgu(dwg, dwu) if on_dwgu else ())
    dx = mm(tag + "_dxu", dhu, wu, add=[(1.0, dx)] + ([also] if also else []))
    return dx, dwg, dwu, dwd


def _conv_taps(ext, taps, n):
    out = taps[3] * ext
    for j in range(3):
        out = out + taps[j] * pltpu.roll(ext, 3 - j, 0)
    return out


def _l2n(x):
    r = lax.rsqrt(_rsum(x * x) + L2_EPS)
    return x * r, r


def conv_fwd(name, pre, taps, tl=ROWS_WIDE, deps=()):
    length = pre.shape[0]
    tl = min(tl, length)

    def fn(ctx, rows, consts, prevs, nexts):
        prev = jnp.where(ctx.i > 0, prevs[0], 0.0)
        ext = jnp.concatenate([prev, rows[0]], axis=0)
        s = _silu(_conv_taps(ext, consts, tl + HALO)[HALO:])
        q = _cat([_l2n(x)[0] * (HD ** -0.5) for x in _heads(s[:, :DN_WIDTH], DN_HEADS, HD)])
        k = _cat([_l2n(x)[0] for x in _heads(s[:, DN_WIDTH:2 * DN_WIDTH], DN_HEADS, HD)])
        return [q, k, s[:, 2 * DN_WIDTH:]], []

    return rowwise(name, fn, length, tl, rows=[pre], consts=list(taps), prevs=[pre],
                   out_rows=[(DN_WIDTH, F32)] * 3, deps=deps)


def conv_bwd(name, pre, dq, dk, dv, taps, tl=ROWS_WIDE):
    length = pre.shape[0]
    tl = min(tl, length)
    n = tl + 2 * HALO

    def fn(ctx, rows, consts, prevs, nexts):
        last = ctx.i == ctx.nblk - 1
        prev = jnp.where(ctx.i > 0, prevs[0], 0.0)
        ext = jnp.concatenate([prev, rows[0], nexts[0]], axis=0)
        c = _conv_taps(ext, consts, n)
        sg = _sigmoid(c)
        s = c * sg
        zero = jnp.zeros((HALO, DN_WIDTH), F32)
        dqe, dke, dve = [jnp.concatenate([zero, rows[1 + t], jnp.where(last, 0.0, nexts[1 + t])], axis=0)
                         for t in range(3)]

        def l2_bwd(x, dy):
            y, r = _l2n(x)
            return r * (dy - y * _rsum(dy * y))

        dsq = _cat([l2_bwd(x, d * (HD ** -0.5)) for x, d in zip(_heads(s[:, :DN_WIDTH], DN_HEADS, HD),
                                                                 _heads(dqe, DN_HEADS, HD))])
        dsk = _cat([l2_bwd(x, d) for x, d in zip(_heads(s[:, DN_WIDTH:2 * DN_WIDTH], DN_HEADS, HD),
                                                  _heads(dke, DN_HEADS, HD))])
        dc = _cat([dsq, dsk, dve]) * (sg * (1.0 + c * (1.0 - sg)))
        dpre = consts[3] * dc
        for j in range(3):
            dpre = dpre + consts[j] * pltpu.roll(dc, n - (3 - j), 0)
        dc_cur = dc[HALO:HALO + tl]
        dws = [_csum(dc_cur * pltpu.roll(ext, 3 - j, 0)[HALO:HALO + tl]) for j in range(3)]
        dws.append(_csum(dc_cur * ext[HALO:HALO + tl]))
        return [dpre[HALO:HALO + tl]], dws

    return rowwise(name, fn, length, tl, rows=[pre, dq, dk, dv], consts=list(taps), prevs=[pre],
                   nexts=[pre, dq, dk, dv], out_rows=[(3 * DN_WIDTH, BF16)],
                   out_accs=[((1, 3 * DN_WIDTH), F32)] * 4)


def _gate_consts():
    lane = jnp.arange(128)[:, None]
    col = jnp.arange(2 * DN_WIDTH)[None, :]
    sel = ((lane < 2 * DN_HEADS) & (col // HD == lane)).astype(F32)
    pick = ((col.T == lane.T * HD) & (lane.T < 2 * DN_HEADS)).astype(F32)
    return sel, pick


def _gate_math(ab, alog, dtb):
    z = ab + dtb
    g = -jnp.exp(alog) * _softplus(z)
    beta = _sigmoid(ab)
    return z, g, beta


def gates_fwd(name, ab, alog, dtb, sel, tl=ROWS):
    length = ab.shape[0]

    def fn(ctx, rows, consts, prevs, nexts):
        _, g, beta = _gate_math(rows[0], consts[0], consts[1])
        lane = _iota(g.shape, 1)
        small = jnp.where(lane < DN_HEADS, g, jnp.where(lane < 2 * DN_HEADS, beta, 0.0))
        big = dnn(small, consts[2], HI)
        return [big[:, :DN_WIDTH], big[:, DN_WIDTH:]], []

    return rowwise(name, fn, length, min(tl, length), rows=[ab], consts=[alog, dtb, sel],
                   out_rows=[(DN_WIDTH, F32)] * 2)


def gates_bwd(name, ab, dgb, dbb, alog, dtb, pick, tl=ROWS):
    length = ab.shape[0]

    def fn(ctx, rows, consts, prevs, nexts):
        z, g, beta = _gate_math(rows[0], consts[0], consts[1])
        dsmall = dnn(_cat([rows[1], rows[2]]), consts[2], HI)
        lane = _iota(g.shape, 1)
        is_a = lane < DN_HEADS
        da = jnp.where(is_a, dsmall * (-jnp.exp(consts[0])) * _sigmoid(z), 0.0)
        db = jnp.where((lane >= DN_HEADS) & (lane < 2 * DN_HEADS), dsmall * beta * (1.0 - beta), 0.0)
        return [da + db], [_csum(jnp.where(is_a, dsmall * g, 0.0)), _csum(da)]

    return rowwise(name, fn, length, min(tl, length), rows=[ab, dgb, dbb], consts=[alog, dtb, pick],
                   out_rows=[(128, BF16)], out_accs=[((1, 128), F32)] * 2)


CPS = 2


def _chunk_scan_rows(x, suffix=False):
    n = x.shape[0]
    rc = _iota(x.shape, 0) & (CHUNK - 1)
    sh = 1
    while sh < CHUNK:
        if suffix:
            x = x + jnp.where(rc < CHUNK - sh, pltpu.roll(x, n - sh, 0), 0.0)
        else:
            x = x + jnp.where(rc >= sh, pltpu.roll(x, sh, 0), 0.0)
        sh *= 2
    return x


def _tri_inv(a_list, eye, bd):
    def each(f, *ls):
        return [f(*xs) for xs in zip(*ls)]

    dg = [jnp.where(bd, a, 0.0) for a in a_list]
    lo = each(lambda a, d: a - d, a_list, dg)
    n1 = [-d for d in dg]
    n2 = each(lambda n: dnn(n, n, X3), n1)
    n4 = each(lambda n: dnn(n, n, X3), n2)
    td = each(lambda p, s: dnn(eye + p, eye + s, X3), n1, n2)
    n8 = each(lambda n: dnn(n, n, X3), n4)
    td = each(lambda t, n: dnn(t, eye + n, X3), td, n4)
    td = each(lambda t, n: dnn(t, eye + n, X3), td, n8)
    m = each(lambda t, l: dnn(t, l, X3), td, lo)
    m2 = each(lambda x: dnn(x, x, X3), m)
    x = each(lambda p, s: dnn(eye - p, eye + s, X3), m, m2)
    return each(lambda p, t: dnn(p, t, X3), x, td)


def _chunk_common(q, k, v, gcb, bb):
    egb = jnp.exp(gcb)
    gc64 = gcb[:, :CHUNK]
    ii, jj = _iota((CHUNK, CHUNK), 0), _iota((CHUNK, CHUNK), 1)
    incl, strict = ii >= jj, ii > jj
    decay = jnp.exp(jnp.where(incl, gc64 - gc64.T, -jnp.inf))
    kb = k * bb
    vb = v * bb
    kbe = kb * egb
    pq = dnt(jnp.concatenate([kb, q], axis=0), k, X3)
    ekb = jnp.exp(gcb[CHUNK - 1:CHUNK, :] - gcb)
    return dict(egb=egb, decay=decay, kb=kb, vb=vb, kbe=kbe, pm=pq[:CHUNK], qm=pq[CHUNK:], ekb=ekb,
                incl=incl, strict=strict, ii=ii, jj=jj)


def _chunk_head(vals, ci, h):
    return [v[ci * CHUNK:(ci + 1) * CHUNK, h * HD:(h + 1) * HD] for v in vals]


def _assemble(per_chunk):
    return jnp.concatenate([_cat(hs) for hs in per_chunk], axis=0)


def _assemble3(per_chunk):
    return jnp.stack([jnp.concatenate([per_chunk[ci][h] for ci in range(CPS)], axis=0) for h in range(DN_HEADS)])


def delta_prep_fwd(name, q, k, v, gb, bb):
    length = q.shape[0]

    def fn(ctx, rows, consts, prevs, nexts):
        gcb_all = _chunk_scan_rows(rows[3])
        vals = [rows[0], rows[1], rows[2], gcb_all, rows[4]]
        units = [(ci, h) for ci in range(CPS) for h in range(DN_HEADS)]
        ins = [_chunk_head(vals, ci, h) for ci, h in units]
        cs = [_chunk_common(*i) for i in ins]
        eye = (cs[0]["ii"] == cs[0]["jj"]).astype(F32)
        ts = _tri_inv([jnp.where(c["strict"], c["pm"] * c["decay"], 0.0) for c in cs], eye,
                      (cs[0]["ii"] >> 4) == (cs[0]["jj"] >> 4))
        uws = [dnn(t, _cat([c["vb"], c["kbe"]]), X3) for t, c in zip(ts, cs)]

        def grid2(xs):
            return [xs[ci * DN_HEADS:(ci + 1) * DN_HEADS] for ci in range(CPS)]

        return [_assemble(grid2([uw[:, :HD] for uw in uws])), _assemble(grid2([uw[:, HD:] for uw in uws])),
                _assemble(grid2([i[0] * c["egb"] for i, c in zip(ins, cs)])),
                _assemble(grid2([i[1] * c["ekb"] for i, c in zip(ins, cs)])), gcb_all,
                _assemble3(grid2([c["qm"] * c["decay"] for c in cs])), _assemble3(grid2(ts))], []

    return rowwise(name, fn, length, CHUNK * CPS, rows=[q, k, v, gb, bb],
                   out_rows=[(DN_WIDTH, F32)] * 5 + [(DN_HEADS, CHUNK, F32)] * 2)


def delta_prep_bwd(name, q, k, v, gb, bb, t3, du, dw, dqd, dkd, dattn3, dgl):
    length = q.shape[0]

    def fn(ctx, rows, consts, prevs, nexts):
        gcb_all = _chunk_scan_rows(rows[3])
        vals = [rows[0], rows[1], rows[2], gcb_all] + list(rows[4:9])
        t3v, da3v, dglv = rows[9], rows[10], rows[11]
        units = [(ci, h) for ci in range(CPS) for h in range(DN_HEADS)]
        ins = [_chunk_head(vals, ci, h) for ci, h in units]
        cs = [_chunk_common(*i[:5]) for i in ins]
        ts = [t3v[h][ci * CHUNK:(ci + 1) * CHUNK] for ci, h in units]
        dattns = [jnp.where(c["incl"], da3v[h][ci * CHUNK:(ci + 1) * CHUNK], 0.0) for (ci, h), c in zip(units, cs)]
        duws = [_cat([i[5], i[6]]) for i in ins]
        dvks = [dtn(t, d, X3) for t, d in zip(ts, duws)]
        dts = [dnt(d, _cat([c["vb"], c["kbe"]]), X3) for d, c in zip(duws, cs)]
        dts = [dnt(d, t, X3) for d, t in zip(dts, ts)]
        das = [jnp.where(c["strict"], -dtn(t, d, X3), 0.0) for c, t, d in zip(cs, ts, dts)]
        dpqs = [jnp.concatenate([da * c["decay"], dat * c["decay"]], axis=0) for da, dat, c in zip(das, dattns, cs)]
        dpqks = [dnn(d, i[1], X3) for d, i in zip(dpqs, ins)]
        dkps = [dtn(d, jnp.concatenate([c["kb"], i[0]], axis=0), X3) for d, c, i in zip(dpqs, cs, ins)]
        dqs, dks, dvs, dgcs, dbs = [], [], [], [], []
        for (ci, h), i, c, dvk, da, dattn, dpqk, dkp in zip(units, ins, cs, dvks, das, dattns, dpqks, dkps):
            qh, kh, vh, _, bh, _, _, dqdh, dkdh = i
            dvb, dkbe = dvk[:, :HD], dvk[:, HD:]
            dkb = dpqk[:CHUNK] + dkbe * c["egb"]
            c1 = _rsum(dkbe * c["kb"] + dqdh * qh) * c["egb"]
            c2 = _rsum(dkdh * kh) * c["ekb"]
            e = (da * c["pm"] + dattn * c["qm"]) * c["decay"]
            dgc = c1 - c2 + _rsum(e) - _rsum(e.T)
            dgl_tot = jnp.max(dglv[ci * 8:(ci + 1) * 8, h * HD:(h + 1) * HD], axis=0, keepdims=True) + _csum(c2)
            dgcs.append(dgc + jnp.where(_iota((CHUNK, HD), 0) == CHUNK - 1, dgl_tot, 0.0))
            dqs.append(dpqk[CHUNK:] + dqdh * c["egb"])
            dks.append(dkp + dkdh * c["ekb"] + dkb * bh)
            dvs.append(dvb * bh)
            dbs.append(jnp.broadcast_to(_rsum(dkb * kh + dvb * vh), (CHUNK, HD)))

        def grid2(xs):
            return [xs[ci * DN_HEADS:(ci + 1) * DN_HEADS] for ci in range(CPS)]

        return [_assemble(grid2(dqs)), _assemble(grid2(dks)), _assemble(grid2(dvs)),
                _chunk_scan_rows(_assemble(grid2(dgcs)), suffix=True), _assemble(grid2(dbs))], []

    return rowwise(name, fn, length, CHUNK * CPS,
                   rows=[q, k, v, gb, bb, du, dw, dqd, dkd, t3, dattn3, (dgl, DN_WIDTH, 0, 8 * CPS)],
                   out_rows=[(DN_WIDTH, F32)] * 5)


SCAN_CHUNKS = 4


def _scan_chunks(n):
    return SCAN_CHUNKS if n % SCAN_CHUNKS == 0 else 1


def delta_scan_fwd(name, qd, kd, u, w, attn3, gcb):
    length = qd.shape[0]
    n = length // CHUNK
    sc = _scan_chunks(n)
    row = pl.BlockSpec((sc * CHUNK, DN_WIDTH), lambda c: (c, 0))
    sq = pl.BlockSpec((DN_HEADS, sc * CHUNK, CHUNK), lambda c: (0, c, 0))

    def body(qd_ref, kd_ref, u_ref, w_ref, attn_ref, gc_ref, o_ref, vn_ref, st_ref, s_ref):
        c = pl.program_id(0)

        @pl.when(c == 0)
        def _():
            s_ref[...] = jnp.zeros_like(s_ref)

        heads = range(DN_HEADS)
        sls = [pl.ds(h * HD, HD) for h in heads]
        ss = [s_ref[h] for h in heads]
        for ci in range(sc):
            rs = pl.ds(ci * CHUNK, CHUNK)
            ws = [dnn(w_ref[rs, sl], s) for sl, s in zip(sls, ss)]
            qs = [dnn(qd_ref[rs, sl], s) for sl, s in zip(sls, ss)]
            vns = [u_ref[rs, sl] - x for sl, x in zip(sls, ws)]
            avs = [dnn(attn_ref[h, rs, :], vn) for h, vn in zip(heads, vns)]
            kvs = [dtn(kd_ref[rs, sl], vn) for sl, vn in zip(sls, vns)]
            for h, sl in zip(heads, sls):
                st_ref[ci, h] = ss[h]
                o_ref[rs, sl] = qs[h] + avs[h]
                vn_ref[rs, sl] = vns[h]
            ss = [s * jnp.exp(gc_ref[pl.ds(ci * CHUNK + CHUNK - 1, 1), sl]) + kv for s, sl, kv in zip(ss, sls, kvs)]
        for h in heads:
            s_ref[h] = ss[h]

    return pl.pallas_call(
        body, name=name, grid=(n // sc,), in_specs=[row, row, row, row, sq, row],
        out_specs=[row, row, pl.BlockSpec((sc, DN_HEADS, HD, HD), lambda c: (c, 0, 0, 0))],
        out_shape=[jax.ShapeDtypeStruct((length, DN_WIDTH), F32), jax.ShapeDtypeStruct((length, DN_WIDTH), F32),
                   jax.ShapeDtypeStruct((n, DN_HEADS, HD, HD), F32)],
        scratch_shapes=[pltpu.VMEM((DN_HEADS, HD, HD), F32)],
        compiler_params=_params(("arbitrary",)),
    )(qd, kd, u, w, attn3, gcb)


def delta_scan_bwd(name, do, qd, kd, w, attn3, vn, st, gcb):
    length = qd.shape[0]
    n = length // CHUNK
    sc = _scan_chunks(n)
    nb = n // sc
    row = pl.BlockSpec((sc * CHUNK, DN_WIDTH), lambda c: (nb - 1 - c, 0))
    sq = pl.BlockSpec((DN_HEADS, sc * CHUNK, CHUNK), lambda c: (0, nb - 1 - c, 0))
    stb = pl.BlockSpec((sc, DN_HEADS, HD, HD), lambda c: (nb - 1 - c, 0, 0, 0))
    glb = pl.BlockSpec((sc * 8, DN_WIDTH), lambda c: (nb - 1 - c, 0))

    def body(do_ref, qd_ref, kd_ref, w_ref, attn_ref, vn_ref, st_ref, gc_ref,
             dqd_ref, dkd_ref, du_ref, dw_ref, dattn_ref, dgl_ref, ds_ref):
        c = pl.program_id(0)

        @pl.when(c == 0)
        def _():
            ds_ref[...] = jnp.zeros_like(ds_ref)

        heads = range(DN_HEADS)
        sls = [pl.ds(h * HD, HD) for h in heads]
        dsns = [ds_ref[h] for h in heads]
        for ci in reversed(range(sc)):
            rs = pl.ds(ci * CHUNK, CHUNK)
            ss = [st_ref[ci, h] for h in heads]
            dos = [do_ref[rs, sl] for sl in sls]
            vns = [vn_ref[rs, sl] for sl in sls]
            dvns = [dtn(attn_ref[h, rs, :], d) for h, d in zip(heads, dos)]
            dvns = [x + dnn(kd_ref[rs, sl], dsn) for x, sl, dsn in zip(dvns, sls, dsns)]
            qdos = [dtn(qd_ref[rs, sl], d) for sl, d in zip(sls, dos)]
            for h, sl in zip(heads, sls):
                dattn_ref[h, rs, :] = dnt(dos[h], vns[h])
                dqd_ref[rs, sl] = dnt(dos[h], ss[h])
                dkd_ref[rs, sl] = dnt(vns[h], dsns[h])
                du_ref[rs, sl] = dvns[h]
            dws = [dnt(dvn, s) for dvn, s in zip(dvns, ss)]
            wdvs = [dtn(w_ref[rs, sl], dvn) for sl, dvn in zip(sls, dvns)]
            nxt = []
            for h, sl in zip(heads, sls):
                egl = jnp.exp(gc_ref[pl.ds(ci * CHUNK + CHUNK - 1, 1), sl])
                dw_ref[rs, sl] = -dws[h]
                dgl_ref[pl.ds(ci * 8, 8), sl] = jnp.broadcast_to(_csum(_rsum(dsns[h] * ss[h])) * egl, (8, HD))
                nxt.append(dsns[h] * egl + qdos[h] - wdvs[h])
            dsns = nxt
        for h in heads:
            ds_ref[h] = dsns[h]

    return pl.pallas_call(
        body, name=name, grid=(nb,), in_specs=[row, row, row, row, sq, row, stb, row],
        out_specs=[row, row, row, row, sq, glb],
        out_shape=[jax.ShapeDtypeStruct((length, DN_WIDTH), F32)] * 4
        + [jax.ShapeDtypeStruct((DN_HEADS, length, CHUNK), F32), jax.ShapeDtypeStruct((n * 8, DN_WIDTH), F32)],
        scratch_shapes=[pltpu.VMEM((DN_HEADS, HD, HD), F32)],
        compiler_params=_params(("arbitrary",)),
    )(do, qd, kd, w, attn3, vn, st, gcb)


def onorm_fwd(name, o, z, nw, tl=ROWS):
    length = o.shape[0]

    def fn(ctx, rows, consts, prevs, nexts):
        outs = []
        for oh, zh in zip(_heads(rows[0], DN_HEADS, HD), _heads(rows[1], DN_HEADS, HD)):
            r = lax.rsqrt(jnp.mean(oh * oh, axis=1, keepdims=True) + RMS_EPS)
            outs.append(oh * r * consts[0] * _silu(zh))
        return [_cat(outs)], []

    return rowwise(name, fn, length, min(tl, length), rows=[o, z], consts=[nw], out_rows=[(DN_WIDTH, BF16)])[0]


def onorm_bwd(name, o, z, d_on, nw, tl=ROWS):
    length = o.shape[0]

    def fn(ctx, rows, consts, prevs, nexts):
        dos, dzs = [], []
        dnw = jnp.zeros((1, HD), F32)
        for oh, zh, dh in zip(*[_heads(r, DN_HEADS, HD) for r in rows]):
            r = lax.rsqrt(jnp.mean(oh * oh, axis=1, keepdims=True) + RMS_EPS)
            y = oh * r
            sz = _silu(zh)
            t = dh * sz * consts[0]
            dos.append(r * (t - y * jnp.mean(t * y, axis=1, keepdims=True)))
            dzs.append(dh * y * consts[0] * _dsilu(zh))
            dnw = dnw + _csum(dh * y * sz)
        return [_cat(dos), _cat(dzs)], [dnw]

    return rowwise(name, fn, length, min(tl, length), rows=[o, z, d_on], consts=[nw],
                   out_rows=[(DN_WIDTH, F32), (DN_WIDTH, BF16)], out_accs=[((1, HD), F32)])


def merge_fwd(name, gates, ydn, ypool, tl=ROWS_WIDE):
    length = ydn.shape[0]

    def fn(ctx, rows, consts, prevs, nexts):
        gt = rows[0]
        return [_sigmoid(gt[:, :D_MODEL]) * rows[1] + _sigmoid(gt[:, D_MODEL:]) * rows[2]], []

    return rowwise(name, fn, length, min(tl, length), rows=[gates, ydn, ypool], out_rows=[(D_MODEL, BF16)])[0]


def merge_bwd(name, gates, ydn, ypool, dm, tl=ROWS_WIDE):
    length = ydn.shape[0]

    def fn(ctx, rows, consts, prevs, nexts):
        gt, yd, yp, d = rows
        sd, sp = _sigmoid(gt[:, :D_MODEL]), _sigmoid(gt[:, D_MODEL:])
        dgates = _cat([d * yd * sd * (1.0 - sd), d * yp * sp * (1.0 - sp)])
        return [d * sd, d * sp, dgates], []

    return rowwise(name, fn, length, min(tl, length), rows=[gates, ydn, ypool, dm],
                   out_rows=[(D_MODEL, BF16), (D_MODEL, BF16), (2 * D_MODEL, BF16)])


def _trailing_sums(ext, upto):
    s, sh = ext, 1
    while sh < upto:
        s = s + pltpu.roll(s, sh, 0)
        sh *= 2
    return s


def _leading_sums(ext, upto, n):
    s, sh = ext, 1
    while sh < upto:
        s = s + pltpu.roll(s, n - sh, 0)
        sh *= 2
    return s


def _pool_mixed(ctx, p, prev, tl):
    prevm = jnp.where(ctx.i > 0, prev, 0.0)
    t1 = (_row_index(ctx, tl) + 1).astype(F32)
    outs = []
    for gi, win in enumerate(POOL_WINDOWS):
        sl = slice(gi * HD, (gi + 1) * HD)
        ext = jnp.concatenate([prevm[:, sl], p[:, sl]], axis=0)
        mean = _trailing_sums(ext, win)[HALO:] / jnp.minimum(t1, float(win))
        outs.append(mean - p[:, sl])
    return outs


def pool_fwd(name, p, pool_w, scale, tl=ROWS):
    length = p.shape[0]
    tl = min(tl, length)

    def fn(ctx, rows, consts, prevs, nexts):
        mixed = _pool_mixed(ctx, rows[0], prevs[0], tl)
        y = _cat([dnn(m, consts[0][gi]) for gi, m in enumerate(mixed)])
        return [y * consts[1]], []

    return rowwise(name, fn, length, tl, rows=[p], consts=[pool_w, scale], prevs=[p],
                   out_rows=[(POOL_WIDTH, BF16)])[0]


def pool_bwd(name, p, dpo, pool_w, scale, tl=ROWS):
    length = p.shape[0]
    tl = min(tl, length)
    n = tl + HALO

    def fn(ctx, rows, consts, prevs, nexts):
        last = ctx.i == ctx.nblk - 1
        mixed = _pool_mixed(ctx, rows[0], prevs[0], tl)
        dext = jnp.concatenate([rows[1], jnp.where(last, 0.0, nexts[0])], axis=0)
        t1 = (_row_index(ctx, n) + 1).astype(F32)
        dps, dws, dscs = [], [], []
        for gi, win in enumerate(POOL_WINDOWS):
            sl = slice(gi * HD, (gi + 1) * HD)
            wg = consts[0][gi]
            dyraw = dext[:, sl] * consts[1][:, sl]
            dmix = dnt(dyraw, wg)
            dws.append(dtn(mixed[gi], dyraw[:tl]))
            dscs.append(_csum(rows[1][:, sl] * dnn(mixed[gi], wg)))
            lead = _leading_sums(dmix / jnp.minimum(t1, float(win)), win, n)
            dps.append(lead[:tl] - dmix[:tl])
        return [_cat(dps)], [jnp.stack(dws), _cat(dscs)]

    return rowwise(name, fn, length, tl, rows=[p, dpo], consts=[pool_w, scale], prevs=[p], nexts=[dpo],
                   out_rows=[(POOL_WIDTH, BF16)],
                   out_accs=[((len(POOL_WINDOWS), HD, HD), F32), ((1, POOL_WIDTH), F32)])


def _xa_probs(qh, kh):
    s = dnt(qh, kh) * (XA_HD ** -0.5)
    e = jnp.exp(s - jnp.max(s, axis=1, keepdims=True))
    return e / _rsum(e)


def xattn_fwd(name, qx, kx, vx, tl=ROWS):
    length = qx.shape[0]

    def fn(ctx, rows, consts, prevs, nexts):
        outs = [dnn(_xa_probs(qh, kh), vh) for qh, kh, vh in
                zip(_heads(rows[0], XA_HEADS, XA_HD), _heads(consts[0], XA_HEADS, XA_HD),
                    _heads(consts[1], XA_HEADS, XA_HD))]
        return [_cat(outs)], []

    return rowwise(name, fn, length, min(tl, length), rows=[qx], consts=[kx, vx], out_rows=[(D_MODEL, BF16)])[0]


def xattn_bwd(name, qx, dox, kx, vx, tl=ROWS):
    length = qx.shape[0]

    def fn(ctx, rows, consts, prevs, nexts):
        dqs, dks, dvs = [], [], []
        for qh, dh, kh, vh in zip(_heads(rows[0], XA_HEADS, XA_HD), _heads(rows[1], XA_HEADS, XA_HD),
                                  _heads(consts[0], XA_HEADS, XA_HD), _heads(consts[1], XA_HEADS, XA_HD)):
            pr = _xa_probs(qh, kh)
            dpr = dnt(dh, vh)
            ds = pr * (dpr - _rsum(dpr * pr)) * (XA_HD ** -0.5)
            dqs.append(dnn(ds, kh))
            dks.append(dtn(ds, qh))
            dvs.append(dtn(pr, dh))
        return [_cat(dqs)], [_cat(dks), _cat(dvs)]

    return rowwise(name, fn, length, min(tl, length), rows=[qx, dox], consts=[kx, vx],
                   out_rows=[(D_MODEL, BF16)], out_accs=[((N_MEM, D_MODEL), F32)] * 2)


def local_step(x, mem, target, w, io):
    sel, pick = _gate_consts()
    alog = jnp.pad(w["a_log"], ((0, 0), (0, 128 - DN_HEADS)))
    dtb = jnp.pad(w["dt_bias"], ((0, 0), (0, 128 - DN_HEADS)))

    f1, res1, w_down1 = ffn_fwd("ffn1", x, w["ffn1_w_gate"], w["ffn1_w_up"], io.ffn1_down, deps=io.rest_started())
    x1, r1 = ln_fwd("ln1", [(ALPHA, x), (0.5, f1)], w["ln1_g"], w["ln1_b"], deps=io.halfway("mixer", f1))
    w = dict(w, ffn1_w_down=w_down1, **io.weights("mixer", x1))
    taps = [w["conv_w"][j:j + 1] for j in range(4)]

    pre = mm("in_qkv", x1, w["in_qkv"], tb=True)
    z = mm("in_z", x1, w["in_z"], tb=True)
    gates = mm("in_gates", x1, w["in_gates"], tb=True)
    p = mm("in_p", x1, w["in_p"], tb=True)
    ab = mm("in_ab", x1, w["in_ab"], tb=True)
    q, k, v = conv_fwd("conv", pre, taps, deps=io.halfway("xa", pre))
    gb, bb = gates_fwd("gates", ab, alog, dtb, sel)
    u, wd_, qd, kd, gcb, attn3, t3 = delta_prep_fwd("dprep", q, k, v, gb, bb)
    o, vn, st = delta_scan_fwd("dscan", qd, kd, u, wd_, attn3, gcb)
    on = onorm_fwd("onorm", o, z, w["dn_norm_w"])
    ydn = mm("dn_branch", on, w["w_dn_branch"], tb=True)
    po = pool_fwd("pool", p, w["pool_w"], w["pool_scale"])
    ypool = mm("pool_branch", po, w["w_pool_branch"], tb=True)
    merged = merge_fwd("merge", gates, ydn, ypool)
    mix = mm("mix_out", merged, w["w_mix_out"])
    x2, r2 = ln_fwd("ln2", [(ALPHA, x1), (1.0, mix)], w["ln2_g"], w["ln2_b"])

    w = dict(w, **io.weights("xa", x2))
    m, _ = ln_fwd("ln_mem", [(1.0, mem)], w["mem_ln_g"], w["mem_ln_b"])
    qx = mm("xa_q", x2, w["xa_wq"], deps=io.halfway("ffn2", x2))
    kx = mm("xa_k", m, w["xa_wk"])
    vx = mm("xa_v", m, w["xa_wv"])
    ox = xattn_fwd("xattn", qx, kx, vx)
    xa = mm("xa_o", ox, w["xa_wo"])
    x3, r3 = ln_fwd("ln3", [(ALPHA, x2), (1.0, xa)], w["ln3_g"], w["ln3_b"])
    w = dict(w, **io.weights("ffn2", x3))

    f2, res2, _ = ffn_fwd("ffn2", x3, w["ffn2_w_gate"], w["ffn2_w_up"], w["ffn2_w_down"])
    dy4, r4, loss = ln_loss("ln4_loss", [(ALPHA, x3), (0.5, f2)], w["ln4_g"], w["ln4_b"], target)

    g = {}
    dr4, g["ln4_g"], g["ln4_b"] = ln_bwd("ln4_b", r4, [(1.0, dy4)], w["ln4_g"])
    dx3, g["ffn2_w_gate"], g["ffn2_w_up"], g["ffn2_w_down"] = ffn_bwd(
        "ffn2b", x3, res2, dr4, w["ffn2_w_gate"], w["ffn2_w_up"], w["ffn2_w_down"])
    dep = io.grads_out("ffn2", g)
    dr3, g["ln3_g"], g["ln3_b"] = ln_bwd("ln3_b", r3, [(ALPHA, dr4), (1.0, dx3)], w["ln3_g"], deps=dep)

    dox = mm("xa_do", dr3, w["xa_wo"], tb=True)
    g["xa_wo"] = mm("xa_dwo", ox, dr3, ta=True)
    dqx, dkx, dvx = xattn_bwd("xattn_b", qx, dox, kx, vx)
    g["xa_wq"] = mm("xa_dwq", x2, dqx, ta=True)
    dx2 = mm("xa_dx", dqx, w["xa_wq"], tb=True)
    g["xa_wk"] = mm("xa_dwk", m, dkx, ta=True)
    g["xa_wv"] = mm("xa_dwv", m, dvx, ta=True)
    dmm = mm("xa_dmk", dkx, w["xa_wk"], tb=True, deps=io.grads_out("xa", g))
    dmm = mm("xa_dmv", dvx, w["xa_wv"], tb=True, add=dmm)
    _, g["mem_ln_g"], g["mem_ln_b"] = ln_bwd("ln_mem_b", mem, [(1.0, dmm)], w["mem_ln_g"])
    dr2, g["ln2_g"], g["ln2_b"] = ln_bwd("ln2_b", r2, [(ALPHA, dr3), (1.0, dx2)], w["ln2_g"])
    io.grads_in("ffn2", dr2)

    dmerged = mm("mix_dm", dr2, w["w_mix_out"], tb=True)
    g["w_mix_out"] = mm("mix_dw", merged, dr2, ta=True)
    d_ydn, d_ypool, d_gates = merge_bwd("merge_b", gates, ydn, ypool, dmerged)
    g["w_dn_branch"] = mm("dn_dw", d_ydn, on, ta=True)
    d_on = mm("dn_dx", d_ydn, w["w_dn_branch"])
    g["w_pool_branch"] = mm("pool_dw", d_ypool, po, ta=True)
    d_po = mm("pool_dx", d_ypool, w["w_pool_branch"])
    dp, g["pool_w"], g["pool_scale"] = pool_bwd("pool_b", p, d_po, w["pool_w"], w["pool_scale"])
    d_o, dz, g["dn_norm_w"] = onorm_bwd("onorm_b", o, z, d_on, w["dn_norm_w"])
    dqd, dkd, du, dw_, dattn3, dgl = delta_scan_bwd("dscan_b", d_o, qd, kd, wd_, attn3, vn, st, gcb)
    dq, dk, dv, dgb, dbb = delta_prep_bwd("dprep_b", q, k, v, gb, bb, t3, du, dw_, dqd, dkd, dattn3, dgl)
    dpre, dc0, dc1, dc2, dc3 = conv_bwd("conv_b", pre, dq, dk, dv, taps)
    g["conv_w"] = jnp.concatenate([dc0, dc1, dc2, dc3], axis=0)
    d_ab, dalog, ddtb = gates_bwd("gates_b", ab, dgb, dbb, alog, dtb, pick)
    g["a_log"] = dalog[:, :DN_HEADS]
    g["dt_bias"] = ddtb[:, :DN_HEADS]
    g["in_qkv"] = mm("in_dwqkv", dpre, x1, ta=True)
    g["in_z"] = mm("in_dwz", dz, x1, ta=True)
    g["in_gates"] = mm("in_dwgates", d_gates, x1, ta=True)
    g["in_p"] = mm("in_dwp", dp, x1, ta=True)
    g["in_ab"] = mm("in_dwab", d_ab, x1, ta=True)
    io.grads_in("xa", g["in_ab"])
    dx1 = mm_sum("in_dx", [(dpre, w["in_qkv"]), (dz, w["in_z"]), (d_gates, w["in_gates"]), (dp, w["in_p"]),
                           (d_ab, w["in_ab"])], deps=io.grads_out("mixer", g))
    dr1, g["ln1_g"], g["ln1_b"] = ln_bwd("ln1_b", r1, [(ALPHA, dr2), (1.0, dx1)], w["ln1_g"])

    def on_dwd(dwd):
        return io.small_out(dict(g, loss=loss[0, :1])) + io.grads_out("ffn1_d", dict(ffn1_w_down=dwd))

    def on_dwgu(dwg, dwu):
        return io.grads_out("ffn1_gu", dict(ffn1_w_gate=dwg, ffn1_w_up=dwu))

    grad_x, g["ffn1_w_gate"], g["ffn1_w_up"], g["ffn1_w_down"] = ffn_bwd(
        "ffn1b", x, res1, dr1, w["ffn1_w_gate"], w["ffn1_w_up"], w["ffn1_w_down"], on_dwd=on_dwd, on_dwgu=on_dwgu,
        also=(ALPHA, dr1))
    return loss, grad_x, g


WEIGHT_NAMES = ['ffn1_w_gate', 'ffn1_w_up', 'ffn1_w_down', 'ln1_g', 'ln1_b', 'w_in', 'conv_w', 'a_log', 'dt_bias',
                'dn_norm_w', 'w_dn_branch', 'pool_w', 'pool_scale', 'w_pool_branch', 'w_mix_out', 'ln2_g', 'ln2_b',
                'mem_ln_g', 'mem_ln_b', 'xa_wq', 'xa_wk', 'xa_wv', 'xa_wo', 'ln3_g', 'ln3_b', 'ffn2_w_gate',
                'ffn2_w_up', 'ffn2_w_down', 'ln4_g', 'ln4_b']
SHARDED = [
    ("ffn1_w_gate", "cols", (1024, 352)), ("ffn1_w_up", "cols", (1024, 352)), ("ffn1_w_down", "rows", (352, 1024)),
    ("w_in", "cols", (1024, 577)), ("conv_w", "flat", (4, 192)), ("w_dn_branch", "cols", (512, 128)),
    ("w_pool_branch", "cols", (512, 128)), ("w_mix_out", "rows", (128, 1024)), ("xa_wq", "rows", (128, 1024)),
    ("xa_wk", "rows", (128, 1024)), ("xa_wv", "rows", (128, 1024)), ("xa_wo", "rows", (128, 1024)),
    ("ffn2_w_gate", "cols", (1024, 352)), ("ffn2_w_up", "cols", (1024, 352)), ("ffn2_w_down", "rows", (352, 1024)),
]
REPLICATED = [n for n in WEIGHT_NAMES if n not in {s[0] for s in SHARDED}]
ROW_ALIGN = 16
ROW_BLOCKS = (512, 384, 352, 256, 192, 176, 128)
GROUPS = {"ffn1_gu": ("ffn1_w_gate", "ffn1_w_up"), "ffn1_d": ("ffn1_w_down",),
          "mixer": ("w_in", "conv_w", "w_dn_branch", "w_pool_branch", "w_mix_out"),
          "xa": ("xa_wq", "xa_wk", "xa_wv", "xa_wo"),
          "ffn2": ("ffn2_w_gate", "ffn2_w_up", "ffn2_w_down")}
W_IN_COLS = 577
W_IN_PIECES = (("in_qkv", 0, 1536), ("in_z", 1536, 2048), ("in_ab", 2048, 2056), ("in_p", 2056, 2568),
               ("in_gates", 2568, 4616))


def _round_up(n, m):
    return -(-n // m) * m


def _layout():
    off, table = 0, {}
    for name, form, shape in SHARDED:
        valid = {"rows": shape[0], "cols": shape[1], "flat": 2}[form]
        width = {"rows": shape[1], "cols": shape[0], "flat": shape[0] * shape[1]}[form]
        rows = _round_up(valid, ROW_ALIGN)
        table[name] = (off, rows, valid, width, form, shape)
        off += rows
    return table


LAYOUT = _layout()


def _group_span(names):
    base = LAYOUT[names[0]][0]
    rows = LAYOUT[names[-1]][0] + LAYOUT[names[-1]][1] - base
    while not any(rows % b == 0 for b in ROW_BLOCKS):
        rows += ROW_ALIGN
    return base, rows


def _row_block(rows):
    return _pick(rows, ROW_BLOCKS)


def _pad_block(blk, rows):
    return jnp.pad(blk, ((0, rows - blk.shape[0]), (0, LANES - blk.shape[1])))


def pack_weight_shards(shards, names):
    parts, used = [], 0
    for name in names:
        off, rows, valid, width, form, _ = LAYOUT[name]
        s = shards[name]
        if form == "flat":
            flat = s.reshape(1, -1)
            hi = flat.astype(BF16)
            blk = jnp.concatenate([hi, (flat - hi.astype(F32)).astype(BF16)], axis=0)
        else:
            blk = (s.T if form == "cols" else s).astype(BF16)
        parts.append(_pad_block(blk, rows))
        used += rows
    if _group_span(names)[1] > used:
        parts.append(jnp.zeros((_group_span(names)[1] - used, LANES), BF16))
    return jnp.concatenate(parts, axis=0)


IN_AB_ROWS = 128


def _w_in_segments(rows, first, last):
    segs = []
    for k in range(N_DEV):
        lo, hi = max(first, k * W_IN_COLS), min(last, (k + 1) * W_IN_COLS)
        if lo < hi:
            segs.append((k * rows + lo - k * W_IN_COLS, lo - first, hi - lo))
    return segs


def w_in_pieces(name, padded, rows):
    sizes = [IN_AB_ROWS if piece == "in_ab" else last - first for piece, first, last in W_IN_PIECES]

    def body(src_ref, *outs):
        for o_ref, (piece, first, last) in zip(outs, W_IN_PIECES):
            if piece == "in_ab":
                o_ref[...] = jnp.zeros_like(o_ref)
            for src, dst, count in _w_in_segments(rows, first, last):
                o_ref[pl.ds(dst, count), :] = src_ref[pl.ds(src, count), :]

    outs = pl.pallas_call(
        body, name=name, out_shape=[jax.ShapeDtypeStruct((n, LANES), padded.dtype) for n in sizes],
        compiler_params=pltpu.CompilerParams(vmem_limit_bytes=VMEM_LIMIT_BYTES),
    )(padded)
    return {piece: o for (piece, _, _), o in zip(W_IN_PIECES, outs)}


def unpack_full_weights(gathered, names):
    out, base = {}, _group_span(names)[0]
    for name in names:
        off, rows, valid, width, form, shape = LAYOUT[name]
        seg = gathered[:, off - base:off - base + rows]
        if form == "flat":
            flat = seg[:, 0, :width].astype(F32) + seg[:, 1, :width].astype(F32)
            out[name] = flat.reshape((N_DEV,) + shape).transpose(1, 0, 2).reshape(shape[0], N_DEV * shape[1])
        elif name == "w_in":
            out.update(w_in_pieces("w_in_pieces", seg.reshape(N_DEV * rows, LANES), rows))
        else:
            out[name] = seg[:, :valid, :width].reshape(N_DEV * valid, width)
    return out


def pack_full_grads(grads, names, me):
    wire, own, used = [], [], 0
    for name in names:
        off, rows, valid, width, form, shape = LAYOUT[name]
        if form == "flat":
            full = grads[name].reshape(shape[0], N_DEV, shape[1]).transpose(1, 0, 2).reshape(N_DEV, 1, width)
        elif name == "w_in":
            full = jnp.concatenate([grads[piece][:last - first] for piece, first, last in W_IN_PIECES], axis=0)
            full = full.reshape(N_DEV, valid, width)
        else:
            full = grads[name].reshape(N_DEV, valid, width)
        pad = ((0, rows - full.shape[1]), (0, LANES - width))
        wire.append(jnp.pad(full.astype(WIRE), ((0, 0),) + pad))
        own.append(jnp.pad(lax.dynamic_index_in_dim(full, me, 0, keepdims=False), pad))
        used += rows
    if _group_span(names)[1] > used:
        wire.append(jnp.zeros((N_DEV, _group_span(names)[1] - used, LANES), WIRE))
        own.append(jnp.zeros((_group_span(names)[1] - used, LANES), F32))
    return jnp.concatenate(wire, axis=1), jnp.concatenate(own, axis=0)


TRANSPOSED = ("ffn1_w_gate", "ffn1_w_up", "ffn2_w_gate", "ffn2_w_up", "w_in")


def unpack_grad_shards(packed, names):
    out, base = {}, _group_span(names)[0]
    for name in names:
        off, rows, valid, width, form, shape = LAYOUT[name]
        off -= base
        if form == "flat":
            out[name] = packed[off, :width].reshape(shape)
        elif name in TRANSPOSED:
            out[name] = packed[off:off + valid, :width]
        elif form == "cols":
            out[name] = packed[off:off + valid, :width].T
        else:
            out[name] = packed[off:off + valid, :width]
    return out


SMALL_SHAPES = {n: (1024,) for n in REPLICATED}
SMALL_SHAPES.update(pool_w=(4, 128, 128), pool_scale=(512,), dn_norm_w=(128,), a_log=(4,), dt_bias=(4,))


SMALL_SHAPES["loss"] = (1,)
SMALL_NAMES = REPLICATED + ["loss"]


def _small_layout():
    off, table = 0, {}
    for name in SMALL_NAMES:
        numel = 1
        for d in SMALL_SHAPES[name]:
            numel *= d
        rows = _round_up(-(-numel // LANES), 8)
        table[name] = (off, rows, numel)
        off += rows
    return table, off


SMALL_LAYOUT, SMALL_ROWS = _small_layout()


def _to_rows(flat, rows):
    return jnp.pad(flat, (0, rows * LANES - flat.shape[0])).reshape(rows, LANES)


def pack_small(values):
    return jnp.concatenate([_to_rows(values[name].reshape(-1), SMALL_LAYOUT[name][1]) for name in SMALL_NAMES], axis=0)


def unpack_small(packed):
    out = {}
    for name in SMALL_NAMES:
        off, rows, numel = SMALL_LAYOUT[name]
        out[name] = packed[off:off + rows].reshape(-1)[:numel].reshape(SMALL_SHAPES[name])
    return out


MESH = pl.DeviceIdType.MESH


def _position():
    return lax.axis_index("x"), lax.axis_index("y"), lax.axis_index("c")


def _other_chips(x, y):
    return [(1 - x, y), (x, 1 - y), (1 - x, 1 - y)]


def all_gather(name, block):
    rows, n = block.shape

    def body(x_ref, out_ref, send_sems, recv_sems, local_sem):
        x, y, c = _position()
        me, sibling = (x, y, c), (x, y, 1 - c)
        chips = _other_chips(x, y)

        def slot(px, py, pc):
            return out_ref.at[4 * px + 2 * py + pc]

        def copy(k, blk, to, src=None):
            return pltpu.make_async_remote_copy(
                src_ref=slot(*blk) if src is None else src, dst_ref=slot(*blk),
                send_sem=send_sems.at[k], recv_sem=recv_sems.at[k], device_id=to, device_id_type=MESH)

        mine = pltpu.make_async_copy(x_ref, slot(*me), local_sem)
        mine.start()
        first = [copy(0, me, sibling, src=x_ref)]
        first += [copy(1 + j, me, (*chip, c), src=x_ref) for j, chip in enumerate(chips)]
        for cp in first:
            cp.start()
        passed = [copy(4 + j, (*chip, c), sibling) for j, chip in enumerate(chips)]
        for j, chip in enumerate(chips):
            copy(1 + j, (*chip, c), me).wait_recv()
            passed[j].start()
        copy(0, sibling, me).wait_recv()
        for j, chip in enumerate(chips):
            copy(4 + j, (*chip, 1 - c), me).wait_recv()
        for cp in first + passed:
            cp.wait_send()
        mine.wait()

    return pl.pallas_call(
        body, name=name, out_shape=jax.ShapeDtypeStruct((N_DEV, rows, n), block.dtype),
        in_specs=[ANY], out_specs=ANY,
        scratch_shapes=[pltpu.SemaphoreType.DMA((7,)), pltpu.SemaphoreType.DMA((7,)), pltpu.SemaphoreType.DMA(())],
    )(block)


HBM = pl.BlockSpec(memory_space=pltpu.HBM)
SEM = pl.BlockSpec(memory_space=pltpu.SEMAPHORE)
EFFECT = pltpu.SideEffectType.DATAFLOW_SIDE_EFFECTING


def _remote(src, dst, send_sem, recv_sem, to):
    return pltpu.make_async_remote_copy(src_ref=src, dst_ref=dst, send_sem=send_sem, recv_sem=recv_sem,
                                        device_id=to, device_id_type=MESH)


def split_start(name, bufs, n, make_copies):
    nb = len(bufs)

    def body(*refs):
        for out_cp, _ in make_copies(refs[:nb], refs[nb:nb + n], refs[nb + n:nb + 2 * n]):
            out_cp.start()
        refs[-1][...] = jnp.zeros_like(refs[-1])

    outs = pl.pallas_call(
        body, name=name,
        out_shape=tuple([pltpu.SemaphoreType.DMA(())] * (2 * n)) + tuple(pltpu.HBM(b.shape, b.dtype) for b in bufs)
        + (jax.ShapeDtypeStruct((8, 128), F32),),
        in_specs=[HBM] * nb,
        out_specs=tuple([SEM] * (2 * n) + [HBM] * nb + [pl.BlockSpec(memory_space=pltpu.VMEM)]),
        input_output_aliases={i: 2 * n + i for i in range(nb)},
        compiler_params=pltpu.CompilerParams(has_side_effects=EFFECT),
    )(*[pltpu.with_memory_space_constraint(b, pltpu.HBM) for b in bufs])
    return list(outs[:2 * n]), list(outs[2 * n:2 * n + nb]), outs[-1]


def split_wait(name, bufs, sems, n, make_copies, after):
    nb = len(bufs)

    def body(*refs):
        for out_cp, in_cp in make_copies(refs[:nb], refs[nb:nb + n], refs[nb + n:nb + 2 * n]):
            out_cp.wait_send()
            in_cp.wait_recv()

    outs = pl.pallas_call(
        body, name=name, out_shape=tuple(pltpu.HBM(b.shape, b.dtype) for b in bufs),
        in_specs=[HBM] * nb + [SEM] * (2 * n) + [ANY], out_specs=tuple([HBM] * nb),
        input_output_aliases={i: i for i in range(nb)},
        compiler_params=pltpu.CompilerParams(has_side_effects=EFFECT),
    )(*bufs, *sems, after)
    return list(outs)


def _gather_stage1(refs, send, recv):
    src, land = refs
    x, y, c = _position()
    peers = [(x, y, 1 - c)] + [(*chip, c) for chip in _other_chips(x, y)]
    return [(_remote(src, land.at[4 * x + 2 * y + c], send[k], recv[k], p),
             _remote(src, land.at[4 * p[0] + 2 * p[1] + p[2]], send[k], recv[k], p)) for k, p in enumerate(peers)]


def _gather_stage2(refs, send, recv):
    (land,) = refs
    x, y, c = _position()
    out = []
    for j, (px, py) in enumerate(_other_chips(x, y)):
        mine, theirs = land.at[4 * px + 2 * py + c], land.at[4 * px + 2 * py + 1 - c]
        out.append((_remote(mine, mine, send[j], recv[j], (x, y, 1 - c)),
                    _remote(theirs, theirs, send[j], recv[j], (x, y, 1 - c))))
    return out


def _flips():
    return [(a, b, d) for a in (0, 1) for b in (0, 1) for d in (0, 1) if a | b | d]


def _gather_direct(refs, send, recv):
    src, land = refs
    x, y, c = _position()
    out = []
    for k, (fx, fy, fc) in enumerate(_flips()):
        p = (1 - x if fx else x, 1 - y if fy else y, 1 - c if fc else c)
        out.append((_remote(src, land.at[4 * x + 2 * y + c], send[k], recv[k], p),
                    _remote(src, land.at[4 * p[0] + 2 * p[1] + p[2]], send[k], recv[k], p)))
    return out


def _scatter_direct(refs, send, recv):
    sendbuf, land = refs
    x, y, c = _position()
    out = []
    for k, (fx, fy, fc) in enumerate(_flips()):
        p = (1 - x if fx else x, 1 - y if fy else y, 1 - c if fc else c)
        cp = _remote(sendbuf.at[4 * p[0] + 2 * p[1] + p[2]], land.at[k], send[k], recv[k], p)
        out.append((cp, cp))
    return out


def _own_plus_slots(name, own, landed):
    n, rows, _ = landed.shape
    tr = _row_block(rows)

    def body(g_ref, l_ref, o_ref):
        acc = g_ref[...]
        for j in range(n):
            acc = acc + l_ref[j].astype(F32)
        o_ref[...] = acc

    return pl.pallas_call(
        body, name=name, grid=(rows // tr,),
        in_specs=[pl.BlockSpec((tr, LANES), lambda i: (i, 0)), pl.BlockSpec((n, tr, LANES), lambda i: (0, i, 0))],
        out_specs=pl.BlockSpec((tr, LANES), lambda i: (i, 0)),
        out_shape=jax.ShapeDtypeStruct((rows, LANES), F32), compiler_params=_params(("parallel",)),
    )(own, landed)


def _sum_slots(name, stack):
    n, rows, _ = stack.shape

    def body(s_ref, o_ref):
        acc = s_ref[0]
        for j in range(1, n):
            acc = acc + s_ref[j]
        o_ref[...] = acc

    return pl.pallas_call(
        body, name=name, in_specs=[pl.BlockSpec(stack.shape, lambda: (0, 0, 0))],
        out_specs=pl.BlockSpec((rows, LANES), lambda: (0, 0)), out_shape=jax.ShapeDtypeStruct((rows, LANES), F32),
    )(stack)


def adamw(name, w, g, m, v):
    shape = w.shape
    last = shape[-1]
    w2, g2, m2, v2 = [a.reshape(-1, last) for a in (w, g, m, v)]
    rows = w2.shape[0]
    tr = _pick(rows, (256, 176, 128))

    def body(w_ref, g_ref, m_ref, v_ref, d_ref, nm_ref, nv_ref):
        gg = g_ref[...]
        nm = ADAM_B1 * m_ref[...] + (1.0 - ADAM_B1) * gg
        nv = ADAM_B2 * v_ref[...] + (1.0 - ADAM_B2) * (gg * gg)
        m_hat = nm / (1.0 - ADAM_B1 ** ADAM_STEP)
        v_hat = nv / (1.0 - ADAM_B2 ** ADAM_STEP)
        d_ref[...] = -ADAM_LR * (m_hat / (jnp.sqrt(v_hat) + ADAM_EPS) + ADAM_WD * w_ref[...])
        nm_ref[...] = nm
        nv_ref[...] = nv

    spec = pl.BlockSpec((tr, last), lambda i: (i, 0))
    outs = pl.pallas_call(
        body, name=name, grid=(rows // tr,), in_specs=[spec] * 4, out_specs=[spec] * 3,
        out_shape=[jax.ShapeDtypeStruct((rows, last), F32)] * 3, compiler_params=_params(("parallel",)),
    )(w2, g2, m2, v2)
    return [o.reshape(shape) for o in outs]


def _landing(block_shape, dtype, own):
    x, y, c = _position()
    return lax.dynamic_update_slice(lax.empty((N_DEV,) + block_shape, dtype), own[None], (4 * x + 2 * y + c, 0, 0))


class _Exchanges:
    def __init__(self, shards):
        self.shards = shards
        self.pending = {}
        self.reduced = {}

    def first_weights(self):
        names = GROUPS["ffn1_gu"]
        return unpack_full_weights(all_gather("ag_ffn1_gu", pack_weight_shards(self.shards, names)), names)

    def rest_started(self):
        tokens = []
        block = pack_weight_shards(self.shards, GROUPS["ffn1_d"])
        sems, bufs, token = split_start("ag_ffn1_d_s", [block, _landing(block.shape, block.dtype, block)], N_DEV - 1,
                                        _gather_direct)
        self.pending["ffn1_d"] = (sems, bufs)
        tokens.append(token)
        for key in ("mixer", "xa", "ffn2"):
            block = pack_weight_shards(self.shards, GROUPS[key])
            sems, bufs, token = split_start(f"ag_{key}_s1", [block, _landing(block.shape, block.dtype, block)], 4,
                                            _gather_stage1)
            self.pending[key] = (sems, bufs)
            tokens.append(token)
        return tuple(tokens)

    def ffn1_down(self, after):
        sems, bufs = self.pending.pop("ffn1_d")
        _, gathered = split_wait("ag_ffn1_d_w", bufs, sems, N_DEV - 1, _gather_direct, after)
        return unpack_full_weights(gathered, GROUPS["ffn1_d"])["ffn1_w_down"]

    def halfway(self, key, after):
        sems, bufs = self.pending.pop(key)
        _, land = split_wait(f"ag_{key}_w1", bufs, sems, 4, _gather_stage1, after)
        sems, bufs, token = split_start(f"ag_{key}_s2", [land], 3, _gather_stage2)
        self.pending[key] = (sems, bufs)
        return (token,)

    def weights(self, key, after):
        sems, bufs = self.pending.pop(key)
        (gathered,) = split_wait(f"ag_{key}_w2", bufs, sems, 3, _gather_stage2, after)
        return unpack_full_weights(gathered, GROUPS[key])

    def grads_out(self, key, grads):
        x, y, c = _position()
        wire, own = pack_full_grads(grads, GROUPS[key], 4 * x + 2 * y + c)
        land = lax.empty((N_DEV - 1,) + wire.shape[1:], WIRE)
        sems, bufs, token = split_start(f"rs_{key}_start", [wire, land], N_DEV - 1, _scatter_direct)
        self.pending[key] = (sems, bufs, own)
        return (token,)

    def grads_in(self, key, after):
        sems, bufs, own = self.pending.pop(key)
        _, landed = split_wait(f"rs_{key}_wait", bufs, sems, N_DEV - 1, _scatter_direct, after)
        self.reduced.update(unpack_grad_shards(_own_plus_slots(f"rs_{key}_sum", own, landed), GROUPS[key]))

    def small_out(self, values):
        block = pack_small(values)
        sems, bufs, token = split_start("ag_small_s", [block, _landing(block.shape, block.dtype, block)], N_DEV - 1,
                                        _gather_direct)
        self.pending["small"] = (sems, bufs)
        return (token,)

    def small_in(self, after):
        sems, bufs = self.pending.pop("small")
        _, gathered = split_wait("ag_small_w", bufs, sems, N_DEV - 1, _gather_direct, after)
        return unpack_small(_sum_slots("small_sum", gathered))


def kernel(x, mem, ffn1_w_gate, ffn1_w_up, ffn1_w_down, ln1_g, ln1_b, w_in, conv_w, a_log, dt_bias, dn_norm_w, w_dn_branch, pool_w, pool_scale, w_pool_branch, w_mix_out, ln2_g, ln2_b, mem_ln_g, mem_ln_b, xa_wq, xa_wk, xa_wv, xa_wo, ln3_g, ln3_b, ffn2_w_gate, ffn2_w_up, ffn2_w_down, ln4_g, ln4_b, loss_target, m_ffn1_w_gate, m_ffn1_w_up, m_ffn1_w_down, m_ln1_g, m_ln1_b, m_w_in, m_conv_w, m_a_log, m_dt_bias, m_dn_norm_w, m_w_dn_branch, m_pool_w, m_pool_scale, m_w_pool_branch, m_w_mix_out, m_ln2_g, m_ln2_b, m_mem_ln_g, m_mem_ln_b, m_xa_wq, m_xa_wk, m_xa_wv, m_xa_wo, m_ln3_g, m_ln3_b, m_ffn2_w_gate, m_ffn2_w_up, m_ffn2_w_down, m_ln4_g, m_ln4_b, v_ffn1_w_gate, v_ffn1_w_up, v_ffn1_w_down, v_ln1_g, v_ln1_b, v_w_in, v_conv_w, v_a_log, v_dt_bias, v_dn_norm_w, v_w_dn_branch, v_pool_w, v_pool_scale, v_w_pool_branch, v_w_mix_out, v_ln2_g, v_ln2_b, v_mem_ln_g, v_mem_ln_b, v_xa_wq, v_xa_wk, v_xa_wv, v_xa_wo, v_ln3_g, v_ln3_b, v_ffn2_w_gate, v_ffn2_w_up, v_ffn2_w_down, v_ln4_g, v_ln4_b):
    given = dict(locals())
    shards = {n: given[n] for n in WEIGHT_NAMES}
    io = _Exchanges({n: shards[n][0] for n, _, _ in SHARDED})
    w = io.first_weights()
    for n in REPLICATED:
        w[n] = shards[n][0] if n == "pool_w" else shards[n]
    loss_part, grad_x, g = local_step(x[0], mem[0], loss_target[0], w, io)

    grad, updates = {}, {}

    def update(names, reduced):
        for n in names:
            if n in TRANSPOSED:
                outs = adamw("adamw_" + n, shards[n][0].T, reduced[n], given["m_" + n][0].T, given["v_" + n][0].T)
                grad[n], updates[n] = reduced[n].T[None], [o.T[None] for o in outs]
            else:
                grad[n] = reduced[n].reshape(shards[n].shape)
                updates[n] = adamw("adamw_" + n, shards[n], grad[n], given["m_" + n], given["v_" + n])
        return updates[names[-1]][0]

    update(GROUPS["ffn2"] + GROUPS["xa"], io.reduced)
    io.grads_in("mixer", grad_x)
    done = update(GROUPS["mixer"], io.reduced)
    small = io.small_in(done)
    loss = small.pop("loss")[0]
    done = update(REPLICATED, small)
    io.grads_in("ffn1_d", done)
    done = update(GROUPS["ffn1_d"], io.reduced)
    io.grads_in("ffn1_gu", done)
    update(GROUPS["ffn1_gu"], io.reduced)
    return (loss, grad_x[None], *[grad[n] for n in WEIGHT_NAMES], *[updates[n][0] for n in WEIGHT_NAMES],
            *[updates[n][1] for n in WEIGHT_NAMES], *[updates[n][2] for n in WEIGHT_NAMES])
```

```python
import functools

import jax
import jax.numpy as jnp
from jax import lax
from jax.experimental import pallas as pl
from jax.experimental.pallas import tpu as pltpu

F32 = jnp.float32
BF16 = jnp.bfloat16
MMD = BF16
WIRE = BF16
X3 =lax.Precision.HIGH
VMEM_LIMIT_BYTES = 48 * 1024 * 1024

D_MODEL = 1024
D_FF = 2816
CHUNK = 64
N_MEM = 256
DN_HEADS = 4
HD = 128
DN_WIDTH = 512
POOL_WINDOWS = (2, 4, 8, 16)
POOL_WIDTH = 512
XA_HEADS = 4
XA_HD = 256
LN_EPS = 1e-5
RMS_EPS = 1e-6
L2_EPS = 1e-6
ALPHA = 2.0 ** 0.25
HALO = 16
ROWS = 512
ROWS_WIDE = 256

ADAM_LR = 0.001
ADAM_B1 = 0.9
ADAM_B2 = 0.999
ADAM_EPS = 1e-08
ADAM_WD = 0.01
ADAM_STEP = 10

N_DEV = 8
LANES = 1024
ANY = pl.BlockSpec(memory_space=pl.ANY)


def _dot(a, b, ca, cb, prec):
    dn = (((ca,), (cb,)), ((), ()))
    if prec is not None:
        return lax.dot_general(a.astype(F32), b.astype(F32), dn, precision=prec, preferred_element_type=F32)
    return lax.dot_general(a.astype(MMD), b.astype(MMD), dn, preferred_element_type=F32)


def dnn(a, b, prec=None):
    return _dot(a, b, 1, 0, prec)


def dnt(a, b, prec=None):
    return _dot(a, b, 1, 1, prec)


def dtn(a, b, prec=None):
    return _dot(a, b, 0, 0, prec)


def _sigmoid(x):
    return jax.nn.sigmoid(x)


def _silu(x):
    return x * _sigmoid(x)


def _dsilu(x):
    s = _sigmoid(x)
    return s * (1.0 + x * (1.0 - s))


def _softplus(x):
    return jnp.maximum(x, 0.0) + jnp.log1p(jnp.exp(-jnp.abs(x)))


def _iota(shape, dim):
    return lax.broadcasted_iota(jnp.int32, shape, dim)


def _rsum(x):
    return jnp.sum(x, axis=1, keepdims=True)


def _csum(x):
    return jnp.sum(x, axis=0, keepdims=True)


def _pick(n, cands):
    for c in cands:
        if n % c == 0:
            return c
    return n


def _params(sem):
    return pltpu.CompilerParams(dimension_semantics=sem, vmem_limit_bytes=VMEM_LIMIT_BYTES)


MM_TILE_SIZES = (4096, 2816, 2048, 1536, 1408, 1024, 768, 512, 384, 256, 128)
MM_VMEM_BUDGET = 36 * 1024 * 1024
HBM_BYTES_PER_US = 3.0e6
GRID_STEP_US = 0.35


def _mm_tiles(m, n, kc, a_bytes, b_bytes, o_bytes):
    def sizes(d):
        return [d] if d <= 512 else [t for t in MM_TILE_SIZES if d % t == 0]

    best = None
    for tm in sizes(m):
        for tn in sizes(n):
            for tk in sizes(kc):
                vmem = 2 * (tm * tk * a_bytes + tk * tn * b_bytes + tm * tn * o_bytes) + tm * tn * 4
                if vmem > MM_VMEM_BUDGET:
                    continue
                steps = (m // tm) * (n // tn) * (kc // tk)
                traffic = m * kc * a_bytes * (n // tn) + kc * n * b_bytes * (m // tm) + m * n * o_bytes
                edge = tm * tk * a_bytes + tk * tn * b_bytes + tm * tn * o_bytes
                cost = (traffic + edge) / HBM_BYTES_PER_US + steps * GRID_STEP_US
                if best is None or cost < best[0]:
                    best = (cost, tm, tn, tk)
    return best[1:]


def mm(name, a, b, *, ta=False, tb=False, out_dtype=F32, add=None, scale=None, deps=()):
    adds = [] if add is None else (list(add) if isinstance(add, (list, tuple)) else [(1.0, add)])
    if ta:
        kc, m = a.shape
    else:
        m, kc = a.shape
    if tb:
        n, kb = b.shape
    else:
        kb, n = b.shape
    assert kc == kb, (name, a.shape, b.shape)
    tm, tn, tk = _mm_tiles(m, n, kc, a.dtype.itemsize, b.dtype.itemsize,
                           jnp.dtype(out_dtype).itemsize * (1 + len(adds)))
    nk = kc // tk
    grid = (m // tm, n // tn, nk)
    a_spec = pl.BlockSpec((tk, tm), lambda i, j, k: (k, i)) if ta else pl.BlockSpec((tm, tk), lambda i, j, k: (i, k))
    b_spec = pl.BlockSpec((tn, tk), lambda i, j, k: (j, k)) if tb else pl.BlockSpec((tk, tn), lambda i, j, k: (k, j))
    o_spec = pl.BlockSpec((tm, tn), lambda i, j, k: (i, j))
    ca, cb = (0 if ta else 1), (1 if tb else 0)

    def body(*refs):
        a_ref, b_ref = refs[0], refs[1]
        o_ref = refs[-1] if nk == 1 else refs[-2]
        k = pl.program_id(2)
        part = _dot(a_ref[...], b_ref[...], ca, cb, None)

        def finish(r):
            if scale is not None:
                r = r * scale
            for (coef, _), add_ref in zip(adds, refs[2:2 + len(adds)]):
                r = r + (add_ref[...] if coef == 1.0 else coef * add_ref[...])
            o_ref[...] = r.astype(o_ref.dtype)

        if nk == 1:
            finish(part)
            return
        acc_ref = refs[-1]

        @pl.when(k == 0)
        def _():
            acc_ref[...] = part

        if nk > 2:
            @pl.when((k > 0) & (k < nk - 1))
            def _():
                acc_ref[...] += part

        @pl.when(k == nk - 1)
        def _():
            finish(acc_ref[...] + part)

    ins = [a, b] + [t for _, t in adds] + list(deps)
    specs = [a_spec, b_spec] + [o_spec] * len(adds) + [ANY] * len(deps)
    return pl.pallas_call(
        body, name=name, grid=grid, in_specs=specs, out_specs=o_spec,
        out_shape=jax.ShapeDtypeStruct((m, n), out_dtype),
        scratch_shapes=[pltpu.VMEM((tm, tn), F32)] if nk > 1 else [],
        compiler_params=_params(("parallel", "parallel", "arbitrary")),
    )(*ins)


def mm_sum(name, pairs, deps=()):
    m, n = pairs[0][0].shape[0], pairs[0][1].shape[1]
    tm = min(512, m)
    np_ = len(pairs)

    def body(*refs):
        acc = dnn(refs[0][...], refs[1][...])
        for p in range(1, np_):
            acc = acc + dnn(refs[2 * p][...], refs[2 * p + 1][...])
        refs[-1][...] = acc

    specs, ins = [], []
    for a, b in pairs:
        specs += [pl.BlockSpec((tm, a.shape[1]), lambda i: (i, 0)), pl.BlockSpec(b.shape, lambda i: (0, 0))]
        ins += [a, b]
    return pl.pallas_call(
        body, name=name, grid=(m // tm,), in_specs=specs + [ANY] * len(deps),
        out_specs=pl.BlockSpec((tm, n), lambda i: (i, 0)), out_shape=jax.ShapeDtypeStruct((m, n), F32),
        compiler_params=_params(("parallel",)),
    )(*ins, *deps)


class _Ctx:
    def __init__(self, i, nblk, tl):
        self.i, self.nblk, self.tl = i, nblk, tl


def _norm_item(it):
    if isinstance(it, tuple):
        a, w, j = it[:3]
        rows = it[3] if len(it) > 3 else None
        return a, w, j, rows
    return it, it.shape[-1], 0, None


def rowwise(name, fn, length, tl, *, rows=(), consts=(), prevs=(), nexts=(), out_rows=(), out_accs=(), deps=()):
    nblk = length // tl
    hb = tl // HALO
    nhalo = length // HALO
    arrays, specs = [], []
    for it in rows:
        a, w, j, r = _norm_item(it)
        if a.ndim == 3:
            specs.append(pl.BlockSpec((a.shape[0], tl, w), lambda i, j=j: (0, i, j)))
        else:
            specs.append(pl.BlockSpec((r or tl, w), lambda i, j=j: (i, j)))
        arrays.append(a)
    for a in consts:
        specs.append(pl.BlockSpec(a.shape, lambda i, nd=a.ndim: (0,) * nd))
        arrays.append(a)
    for it in prevs:
        a, w, j, _ = _norm_item(it)
        specs.append(pl.BlockSpec((HALO, w), lambda i, j=j: (jnp.maximum(i * hb - 1, 0), j)))
        arrays.append(a)
    for it in nexts:
        a, w, j, _ = _norm_item(it)
        specs.append(pl.BlockSpec((HALO, w), lambda i, j=j: (jnp.minimum((i + 1) * hb, nhalo - 1), j)))
        arrays.append(a)
    out_shape, out_specs = [], []
    for spec in out_rows:
        if len(spec) == 3:
            h, w, dt = spec
            out_shape.append(jax.ShapeDtypeStruct((h, length, w), dt))
            out_specs.append(pl.BlockSpec((h, tl, w), lambda i: (0, i, 0)))
        else:
            w, dt = spec
            out_shape.append(jax.ShapeDtypeStruct((length, w), dt))
            out_specs.append(pl.BlockSpec((tl, w), lambda i: (i, 0)))
    for shape, dt in out_accs:
        out_shape.append(jax.ShapeDtypeStruct(shape, dt))
        out_specs.append(pl.BlockSpec(shape, lambda i, nd=len(shape): (0,) * nd))
    n_r, n_c, n_p, n_n = len(rows), len(consts), len(prevs), len(nexts)
    n_in = n_r + n_c + n_p + n_n
    n_or = len(out_rows)
    arrays, specs = arrays + list(deps), specs + [ANY] * len(deps)

    def body(*refs):
        i = pl.program_id(0)
        vals = [r[...] for r in refs[:n_in]]
        outs = refs[n_in + len(deps):]
        ctx = _Ctx(i, nblk, tl)
        ro, ao = fn(ctx, vals[:n_r], vals[n_r:n_r + n_c], vals[n_r + n_c:n_r + n_c + n_p], vals[n_r + n_c + n_p:])
        for r, v in zip(outs[:n_or], ro, strict=True):
            r[...] = v.astype(r.dtype)
        for r, v in zip(outs[n_or:], ao, strict=True):
            @pl.when(i == 0)
            def _(r=r, v=v):
                r[...] = v.astype(r.dtype)

            @pl.when(i > 0)
            def _(r=r, v=v):
                r[...] += v.astype(r.dtype)

    res = pl.pallas_call(
        body, name=name, grid=(nblk,), in_specs=specs, out_specs=out_specs, out_shape=out_shape,
        compiler_params=_params(("arbitrary",) if out_accs else ("parallel",)),
    )(*arrays)
    return res


def _heads(x, n, w):
    return [x[:, h * w:(h + 1) * w] for h in range(n)]


def _cat(xs):
    return jnp.concatenate(xs, axis=1)


def _row_index(ctx, nrows, offset=0):
    return ctx.i * ctx.tl + offset + _iota((nrows, 1), 0)


def _ln_stats(r):
    mu = jnp.mean(r, axis=1, keepdims=True)
    d = r - mu
    var = jnp.mean(d * d, axis=1, keepdims=True)
    rstd = lax.rsqrt(var + LN_EPS)
    return d * rstd, rstd


def ln_fwd(name, terms, g, b, tl=ROWS, deps=()):
    coefs = [c for c, _ in terms]
    length = terms[0][1].shape[0]

    def fn(ctx, rows, consts, prevs, nexts):
        r = sum(c * t for c, t in zip(coefs, rows))
        xh, _ = _ln_stats(r)
        return [xh * consts[0] + consts[1], r], []

    return rowwise(name, fn, length, min(tl, length), rows=[t for _, t in terms], consts=[g, b],
                   out_rows=[(D_MODEL, F32), (D_MODEL, F32)], deps=deps)


def ln_bwd(name, r, terms, g, tl=ROWS, deps=()):
    coefs = [c for c, _ in terms]
    length = r.shape[0]

    def fn(ctx, rows, consts, prevs, nexts):
        xh, rstd = _ln_stats(rows[0])
        dy = sum(c * t for c, t in zip(coefs, rows[1:]))
        dxh = dy * consts[0]
        dr = rstd * (dxh - jnp.mean(dxh, axis=1, keepdims=True) - xh * jnp.mean(dxh * xh, axis=1, keepdims=True))
        return [dr], [_csum(dy * xh), _csum(dy)]

    return rowwise(name, fn, length, min(tl, length), rows=[r] + [t for _, t in terms], consts=[g],
                   out_rows=[(D_MODEL, F32)], out_accs=[((1, D_MODEL), F32), ((1, D_MODEL), F32)], deps=deps)


def ln_loss(name, terms, g, b, target, tl=ROWS):
    coefs = [c for c, _ in terms]
    length = target.shape[0]
    nt = len(terms)

    def fn(ctx, rows, consts, prevs, nexts):
        r = sum(c * t for c, t in zip(coefs, rows[:nt]))
        xh, _ = _ln_stats(r)
        err = xh * consts[0] + consts[1] - rows[nt]
        tot = _csum(_rsum(err * err)) * (0.5 / D_MODEL)
        return [err * (1.0 / D_MODEL), r], [jnp.broadcast_to(tot, (1, 128))]

    return rowwise(name, fn, length, min(tl, length), rows=[t for _, t in terms] + [target], consts=[g, b],
                   out_rows=[(D_MODEL, F32), (D_MODEL, F32)], out_accs=[((1, 128), F32)])


def _ffn_blocks(length):
    return min(512, length), D_FF // 2


def ffn_gate_up_act(name, x, wg, wu, deps=()):
    length = x.shape[0]
    tm, tn = _ffn_blocks(length)

    def body(x_ref, wg_ref, wu_ref, *rest):
        hg_ref, hu_ref, act_ref = rest[-3:]
        xb = x_ref[...].astype(MMD)
        hg = dnt(xb, wg_ref[...])
        hu = dnt(xb, wu_ref[...])
        hg_ref[...] = hg
        hu_ref[...] = hu
        act_ref[...] = (_silu(hg) * hu).astype(act_ref.dtype)

    row = pl.BlockSpec((tm, D_MODEL), lambda i, j: (i, 0))
    wsp = pl.BlockSpec((tn, D_MODEL), lambda i, j: (j, 0))
    osp = pl.BlockSpec((tm, tn), lambda i, j: (i, j))
    return pl.pallas_call(
        body, name=name, grid=(length // tm, D_FF // tn), in_specs=[row, wsp, wsp] + [ANY] * len(deps),
        out_specs=[osp] * 3,
        out_shape=[jax.ShapeDtypeStruct((length, D_FF), F32)] * 2 + [jax.ShapeDtypeStruct((length, D_FF), BF16)],
        compiler_params=_params(("parallel", "parallel")),
    )(x, wg, wu, *deps)


def ffn_dact(name, dr, wd, hg, hu, deps=()):
    length = dr.shape[0]
    tm, tn = _ffn_blocks(length)

    def body(dr_ref, wd_ref, hg_ref, hu_ref, *rest):
        dhg_ref, dhu_ref = rest[-2:]
        da = 0.5 * dnt(dr_ref[...], wd_ref[...])
        g = hg_ref[...]
        s = _sigmoid(g)
        dhg_ref[...] = (da * hu_ref[...] * (s * (1.0 + g * (1.0 - s)))).astype(dhg_ref.dtype)
        dhu_ref[...] = (da * (g * s)).astype(dhu_ref.dtype)

    row = pl.BlockSpec((tm, D_MODEL), lambda i, j: (i, 0))
    wsp = pl.BlockSpec((tn, D_MODEL), lambda i, j: (j, 0))
    osp = pl.BlockSpec((tm, tn), lambda i, j: (i, j))
    return pl.pallas_call(
        body, name=name, grid=(length // tm, D_FF // tn), in_specs=[row, wsp, osp, osp] + [ANY] * len(deps),
        out_specs=[osp] * 2, out_shape=[jax.ShapeDtypeStruct((length, D_FF), BF16)] * 2,
        compiler_params=_params(("parallel", "parallel")),
    )(dr, wd, hg, hu, *deps)


def ffn_fwd(tag, x, wg, wu, wd, deps=()):
    hg, hu, act = ffn_gate_up_act(tag + "_gate_up", x, wg, wu, deps)
    if callable(wd):
        wd = wd(act)
    f = mm(tag + "_down", act, wd)
    return f, (hg, hu, act), wd


def ffn_bwd(tag, x, res, dr, wg, wu, wd, deps=(), on_dwd=None, on_dwgu=None, also=None):
    hg, hu, act = res
    dwd = mm(tag + "_dwd", act, dr, ta=True, scale=0.5, deps=deps)
    dhg, dhu = ffn_dact(tag + "_dact", dr, wd, hg, hu, deps=on_dwd(dwd) if on_dwd else ())
    dwg = mm(tag + "_dwg", dhg, x, ta=True)
    dwu = mm(tag + "_dwu", dhu, x, ta=True)
    dx = mm(tag + "_dxg", dhg, wg, deps=on_dwgu(dwg, dwu) if on_dwgu else ())
    dx = mm(tag + "_dxu", dhu, wu, add=[(1.0, dx)] + ([also] if also else []))
    return dx, dwg, dwu, dwd


def _conv_taps(ext, taps, n):
    out = taps[3] * ext
    for j in range(3):
        out = out + taps[j] * pltpu.roll(ext, 3 - j, 0)
    return out


def _l2n(x):
    r = lax.rsqrt(_rsum(x * x) + L2_EPS)
    return x * r, r


def conv_fwd(name, pre, taps, tl=ROWS_WIDE, deps=()):
    length = pre.shape[0]
    tl = min(tl, length)

    def fn(ctx, rows, consts, prevs, nexts):
        prev = jnp.where(ctx.i > 0, prevs[0], 0.0)
        ext = jnp.concatenate([prev, rows[0]], axis=0)
        s = _silu(_conv_taps(ext, consts, tl + HALO)[HALO:])
        q = _cat([_l2n(x)[0] * (HD ** -0.5) for x in _heads(s[:, :DN_WIDTH], DN_HEADS, HD)])
        k = _cat([_l2n(x)[0] for x in _heads(s[:, DN_WIDTH:2 * DN_WIDTH], DN_HEADS, HD)])
        return [q, k, s[:, 2 * DN_WIDTH:]], []

    return rowwise(name, fn, length, tl, rows=[pre], consts=list(taps), prevs=[pre],
                   out_rows=[(DN_WIDTH, F32)] * 3, deps=deps)


def conv_bwd(name, pre, dq, dk, dv, taps, tl=ROWS_WIDE):
    length = pre.shape[0]
    tl = min(tl, length)
    n = tl + 2 * HALO

    def fn(ctx, rows, consts, prevs, nexts):
        last = ctx.i == ctx.nblk - 1
        prev = jnp.where(ctx.i > 0, prevs[0], 0.0)
        ext = jnp.concatenate([prev, rows[0], nexts[0]], axis=0)
        c = _conv_taps(ext, consts, n)
        sg = _sigmoid(c)
        s = c * sg
        zero = jnp.zeros((HALO, DN_WIDTH), F32)
        dqe, dke, dve = [jnp.concatenate([zero, rows[1 + t], jnp.where(last, 0.0, nexts[1 + t])], axis=0)
                         for t in range(3)]

        def l2_bwd(x, dy):
            y, r = _l2n(x)
            return r * (dy - y * _rsum(dy * y))

        dsq = _cat([l2_bwd(x, d * (HD ** -0.5)) for x, d in zip(_heads(s[:, :DN_WIDTH], DN_HEADS, HD),
                                                                 _heads(dqe, DN_HEADS, HD))])
        dsk = _cat([l2_bwd(x, d) for x, d in zip(_heads(s[:, DN_WIDTH:2 * DN_WIDTH], DN_HEADS, HD),
                                                  _heads(dke, DN_HEADS, HD))])
        dc = _cat([dsq, dsk, dve]) * (sg * (1.0 + c * (1.0 - sg)))
        dpre = consts[3] * dc
        for j in range(3):
            dpre = dpre + consts[j] * pltpu.roll(dc, n - (3 - j), 0)
        dc_cur = dc[HALO:HALO + tl]
        dws = [_csum(dc_cur * pltpu.roll(ext, 3 - j, 0)[HALO:HALO + tl]) for j in range(3)]
        dws.append(_csum(dc_cur * ext[HALO:HALO + tl]))
        return [dpre[HALO:HALO + tl]], dws

    return rowwise(name, fn, length, tl, rows=[pre, dq, dk, dv], consts=list(taps), prevs=[pre],
                   nexts=[pre, dq, dk, dv], out_rows=[(3 * DN_WIDTH, BF16)],
                   out_accs=[((1, 3 * DN_WIDTH), F32)] * 4)


def _gate_math(ab, alog, dtb):
    z = ab + dtb
    g = -jnp.exp(alog) * _softplus(z)
    beta = _sigmoid(ab)
    return z, g, beta


def gates_fwd(name, ab, alog, dtb, tl=ROWS):
    length = ab.shape[0]

    def fn(ctx, rows, consts, prevs, nexts):
        _, g, beta = _gate_math(rows[0], consts[0], consts[1])
        spread = [jnp.broadcast_to(v[:, h:h + 1], (v.shape[0], HD))
                  for v, first in ((g, 0), (beta, DN_HEADS)) for h in range(first, first + DN_HEADS)]
        return [_cat(spread[:DN_HEADS]), _cat(spread[DN_HEADS:])], []

    return rowwise(name, fn, length, min(tl, length), rows=[ab], consts=[alog, dtb],
                   out_rows=[(DN_WIDTH, F32)] * 2)


def gates_bwd(name, ab, dgb, dbb, alog, dtb, tl=ROWS):
    length = ab.shape[0]

    def fn(ctx, rows, consts, prevs, nexts):
        z, g, beta = _gate_math(rows[0], consts[0], consts[1])
        lane = _iota(g.shape, 1)
        dsmall = jnp.zeros_like(g)
        for h in range(DN_HEADS):
            dsmall = jnp.where(lane == h, rows[1][:, h * HD:h * HD + 1], dsmall)
            dsmall = jnp.where(lane == DN_HEADS + h, rows[2][:, h * HD:h * HD + 1], dsmall)
        is_a = lane < DN_HEADS
        da = jnp.where(is_a, dsmall * (-jnp.exp(consts[0])) * _sigmoid(z), 0.0)
        db = jnp.where((lane >= DN_HEADS) & (lane < 2 * DN_HEADS), dsmall * beta * (1.0 - beta), 0.0)
        return [da + db], [_csum(jnp.where(is_a, dsmall * g, 0.0)), _csum(da)]

    return rowwise(name, fn, length, min(tl, length), rows=[ab, dgb, dbb], consts=[alog, dtb],
                   out_rows=[(128, BF16)], out_accs=[((1, 128), F32)] * 2)


CPS = 2


def _chunk_scan_rows(x, suffix=False):
    n = x.shape[0]
    rc = _iota(x.shape, 0) & (CHUNK - 1)
    sh = 1
    while sh < CHUNK:
        if suffix:
            x = x + jnp.where(rc < CHUNK - sh, pltpu.roll(x, n - sh, 0), 0.0)
        else:
            x = x + jnp.where(rc >= sh, pltpu.roll(x, sh, 0), 0.0)
        sh *= 2
    return x


def _tri_inv(a_list, eye, bd):
    def each(f, *ls):
        return [f(*xs) for xs in zip(*ls)]

    dg = [jnp.where(bd, a, 0.0) for a in a_list]
    lo = each(lambda a, d: a - d, a_list, dg)
    n1 = [-d for d in dg]
    n2 = each(lambda n: dnn(n, n, X3), n1)
    n4 = each(lambda n: dnn(n, n, X3), n2)
    td = each(lambda p, s: dnn(eye + p, eye + s, X3), n1, n2)
    n8 = each(lambda n: dnn(n, n, X3), n4)
    td = each(lambda t, n: dnn(t, eye + n, X3), td, n4)
    td = each(lambda t, n: dnn(t, eye + n, X3), td, n8)
    m = each(lambda t, l: dnn(t, l, X3), td, lo)
    m2 = each(lambda x: dnn(x, x, X3), m)
    x = each(lambda p, s: dnn(eye - p, eye + s, X3), m, m2)
    return each(lambda p, t: dnn(p, t, X3), x, td)


def _chunk_common(q, k, v, gcb, bb):
    egb = jnp.exp(gcb)
    gc64 = gcb[:, :CHUNK]
    ii, jj = _iota((CHUNK, CHUNK), 0), _iota((CHUNK, CHUNK), 1)
    incl, strict = ii >= jj, ii > jj
    decay = jnp.exp(jnp.where(incl, gc64 - gc64.T, -jnp.inf))
    kb = k * bb
    vb = v * bb
    kbe = kb * egb
    pq = dnt(jnp.concatenate([kb, q], axis=0), k, X3)
    ekb = jnp.exp(gcb[CHUNK - 1:CHUNK, :] - gcb)
    return dict(egb=egb, decay=decay, kb=kb, vb=vb, kbe=kbe, pm=pq[:CHUNK], qm=pq[CHUNK:], ekb=ekb,
                incl=incl, strict=strict, ii=ii, jj=jj)


def _chunk_head(vals, ci, h):
    return [v[ci * CHUNK:(ci + 1) * CHUNK, h * HD:(h + 1) * HD] for v in vals]


def _assemble(per_chunk):
    return jnp.concatenate([_cat(hs) for hs in per_chunk], axis=0)


def _assemble3(per_chunk):
    return jnp.stack([jnp.concatenate([per_chunk[ci][h] for ci in range(CPS)], axis=0) for h in range(DN_HEADS)])


def delta_prep_fwd(name, q, k, v, gb, bb):
    length = q.shape[0]

    def fn(ctx, rows, consts, prevs, nexts):
        gcb_all = _chunk_scan_rows(rows[3])
        vals = [rows[0], rows[1], rows[2], gcb_all, rows[4]]
        units = [(ci, h) for ci in range(CPS) for h in range(DN_HEADS)]
        ins = [_chunk_head(vals, ci, h) for ci, h in units]
        cs = [_chunk_common(*i) for i in ins]
        eye = (cs[0]["ii"] == cs[0]["jj"]).astype(F32)
        ts = _tri_inv([jnp.where(c["strict"], c["pm"] * c["decay"], 0.0) for c in cs], eye,
                      (cs[0]["ii"] >> 4) == (cs[0]["jj"] >> 4))
        uws = [dnn(t, _cat([c["vb"], c["kbe"]]), X3) for t, c in zip(ts, cs)]

        def grid2(xs):
            return [xs[ci * DN_HEADS:(ci + 1) * DN_HEADS] for ci in range(CPS)]

        return [_assemble(grid2([uw[:, :HD] for uw in uws])), _assemble(grid2([uw[:, HD:] for uw in uws])),
                _assemble(grid2([i[0] * c["egb"] for i, c in zip(ins, cs)])),
                _assemble(grid2([i[1] * c["ekb"] for i, c in zip(ins, cs)])), gcb_all,
                _assemble3(grid2([c["qm"] * c["decay"] for c in cs])), _assemble3(grid2(ts))], []

    return rowwise(name, fn, length, CHUNK * CPS, rows=[q, k, v, gb, bb],
                   out_rows=[(DN_WIDTH, F32)] * 5 + [(DN_HEADS, CHUNK, F32)] * 2)


def delta_prep_bwd(name, q, k, v, gb, bb, t3, du, dw, dqd, dkd, dattn3, dgl):
    length = q.shape[0]

    def fn(ctx, rows, consts, prevs, nexts):
        gcb_all = _chunk_scan_rows(rows[3])
        vals = [rows[0], rows[1], rows[2], gcb_all] + list(rows[4:9])
        t3v, da3v, dglv = rows[9], rows[10], rows[11]
        units = [(ci, h) for ci in range(CPS) for h in range(DN_HEADS)]
        ins = [_chunk_head(vals, ci, h) for ci, h in units]
        cs = [_chunk_common(*i[:5]) for i in ins]
        ts = [t3v[h][ci * CHUNK:(ci + 1) * CHUNK] for ci, h in units]
        dattns = [jnp.where(c["incl"], da3v[h][ci * CHUNK:(ci + 1) * CHUNK], 0.0) for (ci, h), c in zip(units, cs)]
        duws = [_cat([i[5], i[6]]) for i in ins]
        dvks = [dtn(t, d, X3) for t, d in zip(ts, duws)]
        dts = [dnt(d, _cat([c["vb"], c["kbe"]]), X3) for d, c in zip(duws, cs)]
        dts = [dnt(d, t, X3) for d, t in zip(dts, ts)]
        das = [jnp.where(c["strict"], -dtn(t, d, X3), 0.0) for c, t, d in zip(cs, ts, dts)]
        dpqs = [jnp.concatenate([da * c["decay"], dat * c["decay"]], axis=0) for da, dat, c in zip(das, dattns, cs)]
        dpqks = [dnn(d, i[1], X3) for d, i in zip(dpqs, ins)]
        dkps = [dtn(d, jnp.concatenate([c["kb"], i[0]], axis=0), X3) for d, c, i in zip(dpqs, cs, ins)]
        dqs, dks, dvs, dgcs, dbs = [], [], [], [], []
        for (ci, h), i, c, dvk, da, dattn, dpqk, dkp in zip(units, ins, cs, dvks, das, dattns, dpqks, dkps):
            qh, kh, vh, _, bh, _, _, dqdh, dkdh = i
            dvb, dkbe = dvk[:, :HD], dvk[:, HD:]
            dkb = dpqk[:CHUNK] + dkbe * c["egb"]
            c1 = _rsum(dkbe * c["kb"] + dqdh * qh) * c["egb"]
            c2 = _rsum(dkdh * kh) * c["ekb"]
            e = (da * c["pm"] + dattn * c["qm"]) * c["decay"]
            dgc = c1 - c2 + _rsum(e) - _rsum(e.T)
            dgl_tot = jnp.max(dglv[ci * 8:(ci + 1) * 8, h * HD:(h + 1) * HD], axis=0, keepdims=True) + _csum(c2)
            dgcs.append(dgc + jnp.where(_iota((CHUNK, HD), 0) == CHUNK - 1, dgl_tot, 0.0))
            dqs.append(dpqk[CHUNK:] + dqdh * c["egb"])
            dks.append(dkp + dkdh * c["ekb"] + dkb * bh)
            dvs.append(dvb * bh)
            dbs.append(jnp.broadcast_to(_rsum(dkb * kh + dvb * vh), (CHUNK, HD)))

        def grid2(xs):
            return [xs[ci * DN_HEADS:(ci + 1) * DN_HEADS] for ci in range(CPS)]

        return [_assemble(grid2(dqs)), _assemble(grid2(dks)), _assemble(grid2(dvs)),
                _chunk_scan_rows(_assemble(grid2(dgcs)), suffix=True), _assemble(grid2(dbs))], []

    return rowwise(name, fn, length, CHUNK * CPS,
                   rows=[q, k, v, gb, bb, du, dw, dqd, dkd, t3, dattn3, (dgl, DN_WIDTH, 0, 8 * CPS)],
                   out_rows=[(DN_WIDTH, F32)] * 5)


SCAN_CHUNKS = 4


def _scan_chunks(n):
    return SCAN_CHUNKS if n % SCAN_CHUNKS == 0 else 1


def delta_scan_fwd(name, qd, kd, u, w, attn3, gcb):
    length = qd.shape[0]
    n = length // CHUNK
    sc = _scan_chunks(n)
    row = pl.BlockSpec((sc * CHUNK, DN_WIDTH), lambda c: (c, 0))
    sq = pl.BlockSpec((DN_HEADS, sc * CHUNK, CHUNK), lambda c: (0, c, 0))

    def body(qd_ref, kd_ref, u_ref, w_ref, attn_ref, gc_ref, o_ref, vn_ref, st_ref, s_ref):
        c = pl.program_id(0)

        @pl.when(c == 0)
        def _():
            s_ref[...] = jnp.zeros_like(s_ref)

        heads = range(DN_HEADS)
        sls = [pl.ds(h * HD, HD) for h in heads]
        ss = [s_ref[h] for h in heads]
        for ci in range(sc):
            rs = pl.ds(ci * CHUNK, CHUNK)
            ws = [dnn(w_ref[rs, sl], s) for sl, s in zip(sls, ss)]
            qs = [dnn(qd_ref[rs, sl], s) for sl, s in zip(sls, ss)]
            vns = [u_ref[rs, sl] - x for sl, x in zip(sls, ws)]
            avs = [dnn(attn_ref[h, rs, :], vn) for h, vn in zip(heads, vns)]
            kvs = [dtn(kd_ref[rs, sl], vn) for sl, vn in zip(sls, vns)]
            for h, sl in zip(heads, sls):
                st_ref[ci, h] = ss[h]
                o_ref[rs, sl] = qs[h] + avs[h]
                vn_ref[rs, sl] = vns[h]
            ss = [s * jnp.exp(gc_ref[pl.ds(ci * CHUNK + CHUNK - 1, 1), sl]) + kv for s, sl, kv in zip(ss, sls, kvs)]
        for h in heads:
            s_ref[h] = ss[h]

    return pl.pallas_call(
        body, name=name, grid=(n // sc,), in_specs=[row, row, row, row, sq, row],
        out_specs=[row, row, pl.BlockSpec((sc, DN_HEADS, HD, HD), lambda c: (c, 0, 0, 0))],
        out_shape=[jax.ShapeDtypeStruct((length, DN_WIDTH), F32), jax.ShapeDtypeStruct((length, DN_WIDTH), F32),
                   jax.ShapeDtypeStruct((n, DN_HEADS, HD, HD), F32)],
        scratch_shapes=[pltpu.VMEM((DN_HEADS, HD, HD), F32)],
        compiler_params=_params(("arbitrary",)),
    )(qd, kd, u, w, attn3, gcb)


def delta_scan_bwd(name, do, qd, kd, w, attn3, vn, st, gcb):
    length = qd.shape[0]
    n = length // CHUNK
    sc = _scan_chunks(n)
    nb = n // sc
    row = pl.BlockSpec((sc * CHUNK, DN_WIDTH), lambda c: (nb - 1 - c, 0))
    sq = pl.BlockSpec((DN_HEADS, sc * CHUNK, CHUNK), lambda c: (0, nb - 1 - c, 0))
    stb = pl.BlockSpec((sc, DN_HEADS, HD, HD), lambda c: (nb - 1 - c, 0, 0, 0))
    glb = pl.BlockSpec((sc * 8, DN_WIDTH), lambda c: (nb - 1 - c, 0))

    def body(do_ref, qd_ref, kd_ref, w_ref, attn_ref, vn_ref, st_ref, gc_ref,
             dqd_ref, dkd_ref, du_ref, dw_ref, dattn_ref, dgl_ref, ds_ref):
        c = pl.program_id(0)

        @pl.when(c == 0)
        def _():
            ds_ref[...] = jnp.zeros_like(ds_ref)

        heads = range(DN_HEADS)
        sls = [pl.ds(h * HD, HD) for h in heads]
        dsns = [ds_ref[h] for h in heads]
        for ci in reversed(range(sc)):
            rs = pl.ds(ci * CHUNK, CHUNK)
            ss = [st_ref[ci, h] for h in heads]
            dos = [do_ref[rs, sl] for sl in sls]
            vns = [vn_ref[rs, sl] for sl in sls]
            dvns = [dtn(attn_ref[h, rs, :], d) for h, d in zip(heads, dos)]
            dvns = [x + dnn(kd_ref[rs, sl], dsn) for x, sl, dsn in zip(dvns, sls, dsns)]
            qdos = [dtn(qd_ref[rs, sl], d) for sl, d in zip(sls, dos)]
            for h, sl in zip(heads, sls):
                dattn_ref[h, rs, :] = dnt(dos[h], vns[h])
                dqd_ref[rs, sl] = dnt(dos[h], ss[h])
                dkd_ref[rs, sl] = dnt(vns[h], dsns[h])
                du_ref[rs, sl] = dvns[h]
            dws = [dnt(dvn, s) for dvn, s in zip(dvns, ss)]
            wdvs = [dtn(w_ref[rs, sl], dvn) for sl, dvn in zip(sls, dvns)]
            nxt = []
            for h, sl in zip(heads, sls):
                egl = jnp.exp(gc_ref[pl.ds(ci * CHUNK + CHUNK - 1, 1), sl])
                dw_ref[rs, sl] = -dws[h]
                dgl_ref[pl.ds(ci * 8, 8), sl] = jnp.broadcast_to(_csum(_rsum(dsns[h] * ss[h])) * egl, (8, HD))
                nxt.append(dsns[h] * egl + qdos[h] - wdvs[h])
            dsns = nxt
        for h in heads:
            ds_ref[h] = dsns[h]

    return pl.pallas_call(
        body, name=name, grid=(nb,), in_specs=[row, row, row, row, sq, row, stb, row],
        out_specs=[row, row, row, row, sq, glb],
        out_shape=[jax.ShapeDtypeStruct((length, DN_WIDTH), F32)] * 4
        + [jax.ShapeDtypeStruct((DN_HEADS, length, CHUNK), F32), jax.ShapeDtypeStruct((n * 8, DN_WIDTH), F32)],
        scratch_shapes=[pltpu.VMEM((DN_HEADS, HD, HD), F32)],
        compiler_params=_params(("arbitrary",)),
    )(do, qd, kd, w, attn3, vn, st, gcb)


def onorm_fwd(name, o, z, nw, tl=ROWS):
    length = o.shape[0]

    def fn(ctx, rows, consts, prevs, nexts):
        outs = []
        for oh, zh in zip(_heads(rows[0], DN_HEADS, HD), _heads(rows[1], DN_HEADS, HD)):
            r = lax.rsqrt(jnp.mean(oh * oh, axis=1, keepdims=True) + RMS_EPS)
            outs.append(oh * r * consts[0] * _silu(zh))
        return [_cat(outs)], []

    return rowwise(name, fn, length, min(tl, length), rows=[o, z], consts=[nw], out_rows=[(DN_WIDTH, BF16)])[0]


def onorm_bwd(name, o, z, d_on, nw, tl=ROWS):
    length = o.shape[0]

    def fn(ctx, rows, consts, prevs, nexts):
        dos, dzs = [], []
        dnw = jnp.zeros((1, HD), F32)
        for oh, zh, dh in zip(*[_heads(r, DN_HEADS, HD) for r in rows]):
            r = lax.rsqrt(jnp.mean(oh * oh, axis=1, keepdims=True) + RMS_EPS)
            y = oh * r
            sz = _silu(zh)
            t = dh * sz * consts[0]
            dos.append(r * (t - y * jnp.mean(t * y, axis=1, keepdims=True)))
            dzs.append(dh * y * consts[0] * _dsilu(zh))
            dnw = dnw + _csum(dh * y * sz)
        return [_cat(dos), _cat(dzs)], [dnw]

    return rowwise(name, fn, length, min(tl, length), rows=[o, z, d_on], consts=[nw],
                   out_rows=[(DN_WIDTH, F32), (DN_WIDTH, BF16)], out_accs=[((1, HD), F32)])


def merge_fwd(name, gates, ydn, ypool, tl=ROWS_WIDE):
    length = ydn.shape[0]

    def fn(ctx, rows, consts, prevs, nexts):
        gt = rows[0]
        return [_sigmoid(gt[:, :D_MODEL]) * rows[1] + _sigmoid(gt[:, D_MODEL:]) * rows[2]], []

    return rowwise(name, fn, length, min(tl, length), rows=[gates, ydn, ypool], out_rows=[(D_MODEL, BF16)])[0]


def merge_bwd(name, gates, ydn, ypool, dm, tl=ROWS_WIDE):
    length = ydn.shape[0]

    def fn(ctx, rows, consts, prevs, nexts):
        gt, yd, yp, d = rows
        sd, sp = _sigmoid(gt[:, :D_MODEL]), _sigmoid(gt[:, D_MODEL:])
        dgates = _cat([d * yd * sd * (1.0 - sd), d * yp * sp * (1.0 - sp)])
        return [d * sd, d * sp, dgates], []

    return rowwise(name, fn, length, min(tl, length), rows=[gates, ydn, ypool, dm],
                   out_rows=[(D_MODEL, BF16), (D_MODEL, BF16), (2 * D_MODEL, BF16)])


def _trailing_sums(ext, upto):
    s, sh = ext, 1
    while sh < upto:
        s = s + pltpu.roll(s, sh, 0)
        sh *= 2
    return s


def _leading_sums(ext, upto, n):
    s, sh = ext, 1
    while sh < upto:
        s = s + pltpu.roll(s, n - sh, 0)
        sh *= 2
    return s


def _pool_mixed(ctx, p, prev, tl):
    prevm = jnp.where(ctx.i > 0, prev, 0.0)
    t1 = (_row_index(ctx, tl) + 1).astype(F32)
    outs = []
    for gi, win in enumerate(POOL_WINDOWS):
        sl = slice(gi * HD, (gi + 1) * HD)
        ext = jnp.concatenate([prevm[:, sl], p[:, sl]], axis=0)
        mean = _trailing_sums(ext, win)[HALO:] / jnp.minimum(t1, float(win))
        outs.append(mean - p[:, sl])
    return outs


def pool_fwd(name, p, pool_w, scale, tl=ROWS):
    length = p.shape[0]
    tl = min(tl, length)

    def fn(ctx, rows, consts, prevs, nexts):
        mixed = _pool_mixed(ctx, rows[0], prevs[0], tl)
        y = _cat([dnn(m, consts[0][gi]) for gi, m in enumerate(mixed)])
        return [y * consts[1]], []

    return rowwise(name, fn, length, tl, rows=[p], consts=[pool_w, scale], prevs=[p],
                   out_rows=[(POOL_WIDTH, BF16)])[0]


def pool_bwd(name, p, dpo, pool_w, scale, tl=ROWS):
    length = p.shape[0]
    tl = min(tl, length)
    n = tl + HALO

    def fn(ctx, rows, consts, prevs, nexts):
        last = ctx.i == ctx.nblk - 1
        mixed = _pool_mixed(ctx, rows[0], prevs[0], tl)
        dext = jnp.concatenate([rows[1], jnp.where(last, 0.0, nexts[0])], axis=0)
        t1 = (_row_index(ctx, n) + 1).astype(F32)
        dps, dws, dscs = [], [], []
        for gi, win in enumerate(POOL_WINDOWS):
            sl = slice(gi * HD, (gi + 1) * HD)
            wg = consts[0][gi]
            dyraw = dext[:, sl] * consts[1][:, sl]
            dmix = dnt(dyraw, wg)
            dws.append(dtn(mixed[gi], dyraw[:tl]))
            dscs.append(_csum(rows[1][:, sl] * dnn(mixed[gi], wg)))
            lead = _leading_sums(dmix / jnp.minimum(t1, float(win)), win, n)
            dps.append(lead[:tl] - dmix[:tl])
        return [_cat(dps)], [jnp.stack(dws), _cat(dscs)]

    return rowwise(name, fn, length, tl, rows=[p, dpo], consts=[pool_w, scale], prevs=[p], nexts=[dpo],
                   out_rows=[(POOL_WIDTH, BF16)],
                   out_accs=[((len(POOL_WINDOWS), HD, HD), F32), ((1, POOL_WIDTH), F32)])


def _xa_probs(qh, kh):
    s = dnt(qh, kh) * (XA_HD ** -0.5)
    e = jnp.exp(s - jnp.max(s, axis=1, keepdims=True))
    return e / _rsum(e)


def xattn_fwd(name, qx, kx, vx, tl=ROWS):
    length = qx.shape[0]

    def fn(ctx, rows, consts, prevs, nexts):
        outs = [dnn(_xa_probs(qh, kh), vh) for qh, kh, vh in
                zip(_heads(rows[0], XA_HEADS, XA_HD), _heads(consts[0], XA_HEADS, XA_HD),
                    _heads(consts[1], XA_HEADS, XA_HD))]
        return [_cat(outs)], []

    return rowwise(name, fn, length, min(tl, length), rows=[qx], consts=[kx, vx], out_rows=[(D_MODEL, BF16)])[0]


def xattn_bwd(name, qx, dox, kx, vx, tl=ROWS):
    length = qx.shape[0]

    def fn(ctx, rows, consts, prevs, nexts):
        dqs, dks, dvs = [], [], []
        for qh, dh, kh, vh in zip(_heads(rows[0], XA_HEADS, XA_HD), _heads(rows[1], XA_HEADS, XA_HD),
                                  _heads(consts[0], XA_HEADS, XA_HD), _heads(consts[1], XA_HEADS, XA_HD)):
            pr = _xa_probs(qh, kh)
            dpr = dnt(dh, vh)
            ds = pr * (dpr - _rsum(dpr * pr)) * (XA_HD ** -0.5)
            dqs.append(dnn(ds, kh))
            dks.append(dtn(ds, qh))
            dvs.append(dtn(pr, dh))
        return [_cat(dqs)], [_cat(dks), _cat(dvs)]

    return rowwise(name, fn, length, min(tl, length), rows=[qx, dox], consts=[kx, vx],
                   out_rows=[(D_MODEL, BF16)], out_accs=[((N_MEM, D_MODEL), F32)] * 2)


def local_step(x, mem, target, w, io):
    alog = jnp.pad(w["a_log"], ((0, 0), (0, 128 - DN_HEADS)))
    dtb = jnp.pad(w["dt_bias"], ((0, 0), (0, 128 - DN_HEADS)))

    f1, res1, w_down1 = ffn_fwd("ffn1", x, w["ffn1_w_gate"], w["ffn1_w_up"], io.ffn1_down, deps=io.rest_started())
    x1, r1 = ln_fwd("ln1", [(ALPHA, x), (0.5, f1)], w["ln1_g"], w["ln1_b"], deps=io.halfway("mixer", f1))
    w = dict(w, ffn1_w_down=w_down1, **io.weights("mixer", x1))
    taps = [w["conv_w"][j:j + 1] for j in range(4)]

    pre = mm("in_qkv", x1, w["in_qkv"], tb=True)
    z = mm("in_z", x1, w["in_z"], tb=True)
    gates = mm("in_gates", x1, w["in_gates"], tb=True)
    p = mm("in_p", x1, w["in_p"], tb=True)
    ab = mm("in_ab", x1, w["in_ab"], tb=True)
    q, k, v = conv_fwd("conv", pre, taps, deps=io.halfway("xa", pre))
    gb, bb = gates_fwd("gates", ab, alog, dtb)
    u, wd_, qd, kd, gcb, attn3, t3 = delta_prep_fwd("dprep", q, k, v, gb, bb)
    o, vn, st = delta_scan_fwd("dscan", qd, kd, u, wd_, attn3, gcb)
    on = onorm_fwd("onorm", o, z, w["dn_norm_w"])
    ydn = mm("dn_branch", on, w["w_dn_branch"], tb=True)
    po = pool_fwd("pool", p, w["pool_w"], w["pool_scale"])
    ypool = mm("pool_branch", po, w["w_pool_branch"], tb=True)
    merged = merge_fwd("merge", gates, ydn, ypool)
    mix = mm("mix_out", merged, w["w_mix_out"])
    x2, r2 = ln_fwd("ln2", [(ALPHA, x1), (1.0, mix)], w["ln2_g"], w["ln2_b"])

    w = dict(w, **io.weights("xa", x2))
    m, _ = ln_fwd("ln_mem", [(1.0, mem)], w["mem_ln_g"], w["mem_ln_b"])
    qx = mm("xa_q", x2, w["xa_wq"], deps=io.halfway("ffn2", x2))
    kx = mm("xa_k", m, w["xa_wk"])
    vx = mm("xa_v", m, w["xa_wv"])
    ox = xattn_fwd("xattn", qx, kx, vx)
    xa = mm("xa_o", ox, w["xa_wo"])
    x3, r3 = ln_fwd("ln3", [(ALPHA, x2), (1.0, xa)], w["ln3_g"], w["ln3_b"])
    w = dict(w, **io.weights("ffn2", x3))

    f2, res2, _ = ffn_fwd("ffn2", x3, w["ffn2_w_gate"], w["ffn2_w_up"], w["ffn2_w_down"])
    dy4, r4, loss = ln_loss("ln4_loss", [(ALPHA, x3), (0.5, f2)], w["ln4_g"], w["ln4_b"], target)

    g = {}
    dr4, g["ln4_g"], g["ln4_b"] = ln_bwd("ln4_b", r4, [(1.0, dy4)], w["ln4_g"])
    dx3, g["ffn2_w_gate"], g["ffn2_w_up"], g["ffn2_w_down"] = ffn_bwd(
        "ffn2b", x3, res2, dr4, w["ffn2_w_gate"], w["ffn2_w_up"], w["ffn2_w_down"])
    dep = io.grads_out("ffn2", g)
    dr3, g["ln3_g"], g["ln3_b"] = ln_bwd("ln3_b", r3, [(ALPHA, dr4), (1.0, dx3)], w["ln3_g"], deps=dep)

    dox = mm("xa_do", dr3, w["xa_wo"], tb=True)
    g["xa_wo"] = mm("xa_dwo", ox, dr3, ta=True)
    dqx, dkx, dvx = xattn_bwd("xattn_b", qx, dox, kx, vx)
    g["xa_wq"] = mm("xa_dwq", x2, dqx, ta=True)
    dx2 = mm("xa_dx", dqx, w["xa_wq"], tb=True)
    g["xa_wk"] = mm("xa_dwk", m, dkx, ta=True)
    g["xa_wv"] = mm("xa_dwv", m, dvx, ta=True)
    dmm = mm("xa_dmk", dkx, w["xa_wk"], tb=True, deps=io.grads_out("xa", g))
    dmm = mm("xa_dmv", dvx, w["xa_wv"], tb=True, add=dmm)
    _, g["mem_ln_g"], g["mem_ln_b"] = ln_bwd("ln_mem_b", mem, [(1.0, dmm)], w["mem_ln_g"])
    dr2, g["ln2_g"], g["ln2_b"] = ln_bwd("ln2_b", r2, [(ALPHA, dr3), (1.0, dx2)], w["ln2_g"])
    io.grads_in("ffn2", dr2)

    dmerged = mm("mix_dm", dr2, w["w_mix_out"], tb=True)
    g["w_mix_out"] = mm("mix_dw", merged, dr2, ta=True)
    d_ydn, d_ypool, d_gates = merge_bwd("merge_b", gates, ydn, ypool, dmerged)
    g["w_dn_branch"] = mm("dn_dw", d_ydn, on, ta=True)
    d_on = mm("dn_dx", d_ydn, w["w_dn_branch"])
    g["w_pool_branch"] = mm("pool_dw", d_ypool, po, ta=True)
    d_po = mm("pool_dx", d_ypool, w["w_pool_branch"])
    dp, g["pool_w"], g["pool_scale"] = pool_bwd("pool_b", p, d_po, w["pool_w"], w["pool_scale"])
    d_o, dz, g["dn_norm_w"] = onorm_bwd("onorm_b", o, z, d_on, w["dn_norm_w"])
    dqd, dkd, du, dw_, dattn3, dgl = delta_scan_bwd("dscan_b", d_o, qd, kd, wd_, attn3, vn, st, gcb)
    dq, dk, dv, dgb, dbb = delta_prep_bwd("dprep_b", q, k, v, gb, bb, t3, du, dw_, dqd, dkd, dattn3, dgl)
    dpre, dc0, dc1, dc2, dc3 = conv_bwd("conv_b", pre, dq, dk, dv, taps)
    g["conv_w"] = jnp.concatenate([dc0, dc1, dc2, dc3], axis=0)
    d_ab, dalog, ddtb = gates_bwd("gates_b", ab, dgb, dbb, alog, dtb)
    g["a_log"] = dalog[:, :DN_HEADS]
    g["dt_bias"] = ddtb[:, :DN_HEADS]
    g["in_qkv"] = mm("in_dwqkv", dpre, x1, ta=True)
    g["in_z"] = mm("in_dwz", dz, x1, ta=True)
    g["in_gates"] = mm("in_dwgates", d_gates, x1, ta=True)
    g["in_p"] = mm("in_dwp", dp, x1, ta=True)
    g["in_ab"] = mm("in_dwab", d_ab, x1, ta=True)
    io.grads_in("xa", g["in_ab"])
    dx1 = mm_sum("in_dx", [(dpre, w["in_qkv"]), (dz, w["in_z"]), (d_gates, w["in_gates"]), (dp, w["in_p"]),
                           (d_ab, w["in_ab"])], deps=io.grads_out("mixer", g))
    dr1, g["ln1_g"], g["ln1_b"] = ln_bwd("ln1_b", r1, [(ALPHA, dr2), (1.0, dx1)], w["ln1_g"])

    def on_dwd(dwd):
        return io.small_out(dict(g, loss=loss[0, :1])) + io.grads_out("ffn1_d", dict(ffn1_w_down=dwd))

    def on_dwgu(dwg, dwu):
        return io.grads_out("ffn1_gu", dict(ffn1_w_gate=dwg, ffn1_w_up=dwu))

    grad_x, g["ffn1_w_gate"], g["ffn1_w_up"], g["ffn1_w_down"] = ffn_bwd(
        "ffn1b", x, res1, dr1, w["ffn1_w_gate"], w["ffn1_w_up"], w["ffn1_w_down"], on_dwd=on_dwd, on_dwgu=on_dwgu,
        also=(ALPHA, dr1))
    return loss, grad_x, g


WEIGHT_NAMES = ['ffn1_w_gate', 'ffn1_w_up', 'ffn1_w_down', 'ln1_g', 'ln1_b', 'w_in', 'conv_w', 'a_log', 'dt_bias',
                'dn_norm_w', 'w_dn_branch', 'pool_w', 'pool_scale', 'w_pool_branch', 'w_mix_out', 'ln2_g', 'ln2_b',
                'mem_ln_g', 'mem_ln_b', 'xa_wq', 'xa_wk', 'xa_wv', 'xa_wo', 'ln3_g', 'ln3_b', 'ffn2_w_gate',
                'ffn2_w_up', 'ffn2_w_down', 'ln4_g', 'ln4_b']
SHARDED = [
    ("ffn1_w_gate", "cols", (1024, 352)), ("ffn1_w_up", "cols", (1024, 352)), ("ffn1_w_down", "rows", (352, 1024)),
    ("w_in", "cols", (1024, 577)), ("conv_w", "flat", (4, 192)), ("w_dn_branch", "cols", (512, 128)),
    ("w_pool_branch", "cols", (512, 128)), ("w_mix_out", "rows", (128, 1024)), ("xa_wq", "rows", (128, 1024)),
    ("xa_wk", "rows", (128, 1024)), ("xa_wv", "rows", (128, 1024)), ("xa_wo", "rows", (128, 1024)),
    ("ffn2_w_gate", "cols", (1024, 352)), ("ffn2_w_up", "cols", (1024, 352)), ("ffn2_w_down", "rows", (352, 1024)),
]
REPLICATED = [n for n in WEIGHT_NAMES if n not in {s[0] for s in SHARDED}]
ROW_ALIGN = 16
ROW_BLOCKS = (512, 384, 352, 256, 192, 176, 128)
GROUPS = {"ffn1_gu": ("ffn1_w_gate", "ffn1_w_up"), "ffn1_d": ("ffn1_w_down",),
          "mixer": ("w_in", "conv_w", "w_dn_branch", "w_pool_branch", "w_mix_out"),
          "xa": ("xa_wq", "xa_wk", "xa_wv", "xa_wo"),
          "ffn2": ("ffn2_w_gate", "ffn2_w_up", "ffn2_w_down")}
W_IN_COLS = 577
W_IN_PIECES = (("in_qkv", 0, 1536), ("in_z", 1536, 2048), ("in_ab", 2048, 2056), ("in_p", 2056, 2568),
               ("in_gates", 2568, 4616))


def _round_up(n, m):
    return -(-n // m) * m


def _layout():
    off, table = 0, {}
    for name, form, shape in SHARDED:
        valid = {"rows": shape[0], "cols": shape[1], "flat": 2}[form]
        width = {"rows": shape[1], "cols": shape[0], "flat": shape[0] * shape[1]}[form]
        rows = _round_up(valid, ROW_ALIGN)
        table[name] = (off, rows, valid, width, form, shape)
        off += rows
    return table


LAYOUT = _layout()


def _group_span(names):
    base = LAYOUT[names[0]][0]
    rows = LAYOUT[names[-1]][0] + LAYOUT[names[-1]][1] - base
    while not any(rows % b == 0 for b in ROW_BLOCKS):
        rows += ROW_ALIGN
    return base, rows


def _row_block(rows):
    return _pick(rows, ROW_BLOCKS)


def _pad_block(blk, rows):
    return jnp.pad(blk, ((0, rows - blk.shape[0]), (0, LANES - blk.shape[1])))


def pack_weight_shards(shards, names):
    parts, used = [], 0
    for name in names:
        off, rows, valid, width, form, _ = LAYOUT[name]
        s = shards[name]
        if form == "flat":
            flat = s.reshape(1, -1)
            hi = flat.astype(BF16)
            blk = jnp.concatenate([hi, (flat - hi.astype(F32)).astype(BF16)], axis=0)
        else:
            blk = (s.T if form == "cols" else s).astype(BF16)
        parts.append(_pad_block(blk, rows))
        used += rows
    if _group_span(names)[1] > used:
        parts.append(jnp.zeros((_group_span(names)[1] - used, LANES), BF16))
    return jnp.concatenate(parts, axis=0)


IN_AB_ROWS = 128


def _w_in_segments(first, last):
    segs = []
    for k in range(N_DEV):
        lo, hi = max(first, k * W_IN_COLS), min(last, (k + 1) * W_IN_COLS)
        if lo < hi:
            segs.append((k, lo - k * W_IN_COLS, lo - first, hi - lo))
    return segs


def w_in_pieces(name, gathered, off, rows):
    assert off % rows == 0
    sizes = [IN_AB_ROWS if piece == "in_ab" else last - first for piece, first, last in W_IN_PIECES]

    def body(src_ref, *outs):
        for o_ref, (piece, first, last) in zip(outs, W_IN_PIECES):
            if piece == "in_ab":
                o_ref[...] = jnp.zeros_like(o_ref)
            for k, src, dst, count in _w_in_segments(first, last):
                o_ref[pl.ds(dst, count), :] = src_ref[k, pl.ds(src, count), :]

    outs = pl.pallas_call(
        body, name=name, grid=(1,), in_specs=[pl.BlockSpec((N_DEV, rows, LANES), lambda i: (0, off // rows, 0))],
        out_specs=[pl.BlockSpec((n, LANES), lambda i: (0, 0)) for n in sizes],
        out_shape=[jax.ShapeDtypeStruct((n, LANES), gathered.dtype) for n in sizes],
        compiler_params=_params(("arbitrary",)),
    )(gathered)
    return {piece: o for (piece, _, _), o in zip(W_IN_PIECES, outs)}


def unpack_full_weights(gathered, names):
    out, base = {}, _group_span(names)[0]
    for name in names:
        off, rows, valid, width, form, shape = LAYOUT[name]
        seg = gathered[:, off - base:off - base + rows]
        if form == "flat":
            flat = seg[:, 0, :width].astype(F32) + seg[:, 1, :width].astype(F32)
            out[name] = flat.reshape((N_DEV,) + shape).transpose(1, 0, 2).reshape(shape[0], N_DEV * shape[1])
        elif name == "w_in":
            out.update(w_in_pieces("w_in_pieces", gathered, off - base, rows))
        else:
            out[name] = seg[:, :valid, :width].reshape(N_DEV * valid, width)
    return out


def pack_full_grads(grads, names, me):
    wire, own, used = [], [], 0
    for name in names:
        off, rows, valid, width, form, shape = LAYOUT[name]
        if form == "flat":
            full = grads[name].reshape(shape[0], N_DEV, shape[1]).transpose(1, 0, 2).reshape(N_DEV, 1, width)
        elif name == "w_in":
            full = jnp.concatenate([grads[piece][:last - first] for piece, first, last in W_IN_PIECES], axis=0)
            full = full.reshape(N_DEV, valid, width)
        else:
            full = grads[name].reshape(N_DEV, valid, width)
        pad = ((0, rows - full.shape[1]), (0, LANES - width))
        wire.append(jnp.pad(full.astype(WIRE), ((0, 0),) + pad))
        own.append(jnp.pad(lax.dynamic_index_in_dim(full, me, 0, keepdims=False), pad))
        used += rows
    if _group_span(names)[1] > used:
        wire.append(jnp.zeros((N_DEV, _group_span(names)[1] - used, LANES), WIRE))
        own.append(jnp.zeros((_group_span(names)[1] - used, LANES), F32))
    return jnp.concatenate(wire, axis=1), jnp.concatenate(own, axis=0)


TRANSPOSED = ("ffn1_w_gate", "ffn1_w_up", "ffn2_w_gate", "ffn2_w_up", "w_in")


def unpack_grad_shards(packed, names):
    out, base = {}, _group_span(names)[0]
    for name in names:
        off, rows, valid, width, form, shape = LAYOUT[name]
        off -= base
        if form == "flat":
            out[name] = packed[off, :width].reshape(shape)
        elif name in TRANSPOSED:
            out[name] = packed[off:off + valid, :width]
        elif form == "cols":
            out[name] = packed[off:off + valid, :width].T
        else:
            out[name] = packed[off:off + valid, :width]
    return out


SMALL_SHAPES = {n: (1024,) for n in REPLICATED}
SMALL_SHAPES.update(pool_w=(4, 128, 128), pool_scale=(512,), dn_norm_w=(128,), a_log=(4,), dt_bias=(4,))


SMALL_SHAPES["loss"] = (1,)
SMALL_NAMES = REPLICATED + ["loss"]


def _small_layout():
    off, table = 0, {}
    for name in SMALL_NAMES:
        numel = 1
        for d in SMALL_SHAPES[name]:
            numel *= d
        rows = _round_up(-(-numel // LANES), 8)
        table[name] = (off, rows, numel)
        off += rows
    return table, off


SMALL_LAYOUT, SMALL_ROWS = _small_layout()


def _to_rows(flat, rows):
    return jnp.pad(flat, (0, rows * LANES - flat.shape[0])).reshape(rows, LANES)


def pack_small(values):
    return jnp.concatenate([_to_rows(values[name].reshape(-1), SMALL_LAYOUT[name][1]) for name in SMALL_NAMES], axis=0)


def unpack_small(packed):
    out = {}
    for name in SMALL_NAMES:
        off, rows, numel = SMALL_LAYOUT[name]
        out[name] = packed[off:off + rows].reshape(-1)[:numel].reshape(SMALL_SHAPES[name])
    return out


MESH = pl.DeviceIdType.MESH


def _position():
    return lax.axis_index("x"), lax.axis_index("y"), lax.axis_index("c")


def _other_chips(x, y):
    return [(1 - x, y), (x, 1 - y), (1 - x, 1 - y)]


def all_gather(name, block):
    rows, n = block.shape

    def body(x_ref, out_ref, send_sems, recv_sems, local_sem):
        x, y, c = _position()
        me, sibling = (x, y, c), (x, y, 1 - c)
        chips = _other_chips(x, y)

        def slot(px, py, pc):
            return out_ref.at[4 * px + 2 * py + pc]

        def copy(k, blk, to, src=None):
            return pltpu.make_async_remote_copy(
                src_ref=slot(*blk) if src is None else src, dst_ref=slot(*blk),
                send_sem=send_sems.at[k], recv_sem=recv_sems.at[k], device_id=to, device_id_type=MESH)

        mine = pltpu.make_async_copy(x_ref, slot(*me), local_sem)
        mine.start()
        first = [copy(0, me, sibling, src=x_ref)]
        first += [copy(1 + j, me, (*chip, c), src=x_ref) for j, chip in enumerate(chips)]
        for cp in first:
            cp.start()
        passed = [copy(4 + j, (*chip, c), sibling) for j, chip in enumerate(chips)]
        for j, chip in enumerate(chips):
            copy(1 + j, (*chip, c), me).wait_recv()
            passed[j].start()
        copy(0, sibling, me).wait_recv()
        for j, chip in enumerate(chips):
            copy(4 + j, (*chip, 1 - c), me).wait_recv()
        for cp in first + passed:
            cp.wait_send()
        mine.wait()

    return pl.pallas_call(
        body, name=name, out_shape=jax.ShapeDtypeStruct((N_DEV, rows, n), block.dtype),
        in_specs=[ANY], out_specs=ANY,
        scratch_shapes=[pltpu.SemaphoreType.DMA((7,)), pltpu.SemaphoreType.DMA((7,)), pltpu.SemaphoreType.DMA(())],
    )(block)


HBM = pl.BlockSpec(memory_space=pltpu.HBM)
SEM = pl.BlockSpec(memory_space=pltpu.SEMAPHORE)
EFFECT = pltpu.SideEffectType.DATAFLOW_SIDE_EFFECTING


def _remote(src, dst, send_sem, recv_sem, to):
    return pltpu.make_async_remote_copy(src_ref=src, dst_ref=dst, send_sem=send_sem, recv_sem=recv_sem,
                                        device_id=to, device_id_type=MESH)


def split_start(name, bufs, n, make_copies):
    nb = len(bufs)

    def body(*refs):
        for out_cp, _ in make_copies(refs[:nb], refs[nb:nb + n], refs[nb + n:nb + 2 * n]):
            out_cp.start()
        refs[-1][...] = jnp.zeros_like(refs[-1])

    outs = pl.pallas_call(
        body, name=name,
        out_shape=tuple([pltpu.SemaphoreType.DMA(())] * (2 * n)) + tuple(pltpu.HBM(b.shape, b.dtype) for b in bufs)
        + (jax.ShapeDtypeStruct((8, 128), F32),),
        in_specs=[HBM] * nb,
        out_specs=tuple([SEM] * (2 * n) + [HBM] * nb + [pl.BlockSpec(memory_space=pltpu.VMEM)]),
        input_output_aliases={i: 2 * n + i for i in range(nb)},
        compiler_params=pltpu.CompilerParams(has_side_effects=EFFECT),
    )(*[pltpu.with_memory_space_constraint(b, pltpu.HBM) for b in bufs])
    return list(outs[:2 * n]), list(outs[2 * n:2 * n + nb]), outs[-1]


def split_wait(name, bufs, sems, n, make_copies, after):
    nb = len(bufs)

    def body(*refs):
        for out_cp, in_cp in make_copies(refs[:nb], refs[nb:nb + n], refs[nb + n:nb + 2 * n]):
            out_cp.wait_send()
            in_cp.wait_recv()

    outs = pl.pallas_call(
        body, name=name, out_shape=tuple(pltpu.HBM(b.shape, b.dtype) for b in bufs),
        in_specs=[HBM] * nb + [SEM] * (2 * n) + [ANY], out_specs=tuple([HBM] * nb),
        input_output_aliases={i: i for i in range(nb)},
        compiler_params=pltpu.CompilerParams(has_side_effects=EFFECT),
    )(*bufs, *sems, after)
    return list(outs)


def _gather_stage1(refs, send, recv):
    src, land = refs
    x, y, c = _position()
    peers = [(x, y, 1 - c)] + [(*chip, c) for chip in _other_chips(x, y)]
    return [(_remote(src, land.at[4 * x + 2 * y + c], send[k], recv[k], p),
             _remote(src, land.at[4 * p[0] + 2 * p[1] + p[2]], send[k], recv[k], p)) for k, p in enumerate(peers)]


def _gather_stage2(refs, send, recv):
    (land,) = refs
    x, y, c = _position()
    out = []
    for j, (px, py) in enumerate(_other_chips(x, y)):
        mine, theirs = land.at[4 * px + 2 * py + c], land.at[4 * px + 2 * py + 1 - c]
        out.append((_remote(mine, mine, send[j], recv[j], (x, y, 1 - c)),
                    _remote(theirs, theirs, send[j], recv[j], (x, y, 1 - c))))
    return out


def _flips():
    return [(a, b, d) for a in (0, 1) for b in (0, 1) for d in (0, 1) if a | b | d]


def _gather_direct(refs, send, recv):
    src, land = refs
    x, y, c = _position()
    out = []
    for k, (fx, fy, fc) in enumerate(_flips()):
        p = (1 - x if fx else x, 1 - y if fy else y, 1 - c if fc else c)
        out.append((_remote(src, land.at[4 * x + 2 * y + c], send[k], recv[k], p),
                    _remote(src, land.at[4 * p[0] + 2 * p[1] + p[2]], send[k], recv[k], p)))
    return out


def _scatter_direct(refs, send, recv):
    sendbuf, land = refs
    x, y, c = _position()
    out = []
    for k, (fx, fy, fc) in enumerate(_flips()):
        p = (1 - x if fx else x, 1 - y if fy else y, 1 - c if fc else c)
        cp = _remote(sendbuf.at[4 * p[0] + 2 * p[1] + p[2]], land.at[k], send[k], recv[k], p)
        out.append((cp, cp))
    return out


def _own_plus_slots(name, own, landed):
    n, rows, _ = landed.shape
    tr = _row_block(rows)

    def body(g_ref, l_ref, o_ref):
        acc = g_ref[...]
        for j in range(n):
            acc = acc + l_ref[j].astype(F32)
        o_ref[...] = acc

    return pl.pallas_call(
        body, name=name, grid=(rows // tr,),
        in_specs=[pl.BlockSpec((tr, LANES), lambda i: (i, 0)), pl.BlockSpec((n, tr, LANES), lambda i: (0, i, 0))],
        out_specs=pl.BlockSpec((tr, LANES), lambda i: (i, 0)),
        out_shape=jax.ShapeDtypeStruct((rows, LANES), F32), compiler_params=_params(("parallel",)),
    )(own, landed)


def _sum_slots(name, stack):
    n, rows, _ = stack.shape

    def body(s_ref, o_ref):
        acc = s_ref[0]
        for j in range(1, n):
            acc = acc + s_ref[j]
        o_ref[...] = acc

    return pl.pallas_call(
        body, name=name, in_specs=[pl.BlockSpec(stack.shape, lambda: (0, 0, 0))],
        out_specs=pl.BlockSpec((rows, LANES), lambda: (0, 0)), out_shape=jax.ShapeDtypeStruct((rows, LANES), F32),
    )(stack)


def adamw(name, w, g, m, v):
    shape = w.shape
    last = shape[-1]
    w2, g2, m2, v2 = [a.reshape(-1, last) for a in (w, g, m, v)]
    rows = w2.shape[0]
    tr = _pick(rows, (256, 176, 128))

    def body(w_ref, g_ref, m_ref, v_ref, d_ref, nm_ref, nv_ref):
        gg = g_ref[...]
        nm = ADAM_B1 * m_ref[...] + (1.0 - ADAM_B1) * gg
        nv = ADAM_B2 * v_ref[...] + (1.0 - ADAM_B2) * (gg * gg)
        m_hat = nm / (1.0 - ADAM_B1 ** ADAM_STEP)
        v_hat = nv / (1.0 - ADAM_B2 ** ADAM_STEP)
        d_ref[...] = -ADAM_LR * (m_hat / (jnp.sqrt(v_hat) + ADAM_EPS) + ADAM_WD * w_ref[...])
        nm_ref[...] = nm
        nv_ref[...] = nv

    spec = pl.BlockSpec((tr, last), lambda i: (i, 0))
    outs = pl.pallas_call(
        body, name=name, grid=(rows // tr,), in_specs=[spec] * 4, out_specs=[spec] * 3,
        out_shape=[jax.ShapeDtypeStruct((rows, last), F32)] * 3, compiler_params=_params(("parallel",)),
    )(w2, g2, m2, v2)
    return [o.reshape(shape) for o in outs]


def _landing(block_shape, dtype, own):
    x, y, c = _position()
    return lax.dynamic_update_slice(lax.empty((N_DEV,) + block_shape, dtype), own[None], (4 * x + 2 * y + c, 0, 0))


class _Exchanges:
    def __init__(self, shards):
        self.shards = shards
        self.pending = {}
        self.reduced = {}

    def first_weights(self):
        names = GROUPS["ffn1_gu"]
        return unpack_full_weights(all_gather("ag_ffn1_gu", pack_weight_shards(self.shards, names)), names)

    def rest_started(self):
        tokens = []
        block = pack_weight_shards(self.shards, GROUPS["ffn1_d"])
        sems, bufs, token = split_start("ag_ffn1_d_s", [block, _landing(block.shape, block.dtype, block)], N_DEV - 1,
                                        _gather_direct)
        self.pending["ffn1_d"] = (sems, bufs)
        tokens.append(token)
        for key in ("mixer", "xa", "ffn2"):
            block = pack_weight_shards(self.shards, GROUPS[key])
            sems, bufs, token = split_start(f"ag_{key}_s1", [block, _landing(block.shape, block.dtype, block)], 4,
                                            _gather_stage1)
            self.pending[key] = (sems, bufs)
            tokens.append(token)
        return tuple(tokens)

    def ffn1_down(self, after):
        sems, bufs = self.pending.pop("ffn1_d")
        _, gathered = split_wait("ag_ffn1_d_w", bufs, sems, N_DEV - 1, _gather_direct, after)
        return unpack_full_weights(gathered, GROUPS["ffn1_d"])["ffn1_w_down"]

    def halfway(self, key, after):
        sems, bufs = self.pending.pop(key)
        _, land = split_wait(f"ag_{key}_w1", bufs, sems, 4, _gather_stage1, after)
        sems, bufs, token = split_start(f"ag_{key}_s2", [land], 3, _gather_stage2)
        self.pending[key] = (sems, bufs)
        return (token,)

    def weights(self, key, after):
        sems, bufs = self.pending.pop(key)
        (gathered,) = split_wait(f"ag_{key}_w2", bufs, sems, 3, _gather_stage2, after)
        return unpack_full_weights(gathered, GROUPS[key])

    def grads_out(self, key, grads):
        x, y, c = _position()
        wire, own = pack_full_grads(grads, GROUPS[key], 4 * x + 2 * y + c)
        land = lax.empty((N_DEV - 1,) + wire.shape[1:], WIRE)
        sems, bufs, token = split_start(f"rs_{key}_start", [wire, land], N_DEV - 1, _scatter_direct)
        self.pending[key] = (sems, bufs, own)
        return (token,)

    def grads_in(self, key, after):
        sems, bufs, own = self.pending.pop(key)
        _, landed = split_wait(f"rs_{key}_wait", bufs, sems, N_DEV - 1, _scatter_direct, after)
        self.reduced.update(unpack_grad_shards(_own_plus_slots(f"rs_{key}_sum", own, landed), GROUPS[key]))

    def small_out(self, values):
        block = pack_small(values)
        sems, bufs, token = split_start("ag_small_s", [block, _landing(block.shape, block.dtype, block)], N_DEV - 1,
                                        _gather_direct)
        self.pending["small"] = (sems, bufs)
        return (token,)

    def small_in(self, after):
        sems, bufs = self.pending.pop("small")
        _, gathered = split_wait("ag_small_w", bufs, sems, N_DEV - 1, _gather_direct, after)
        return unpack_small(_sum_slots("small_sum", gathered))


def kernel(x, mem, ffn1_w_gate, ffn1_w_up, ffn1_w_down, ln1_g, ln1_b, w_in, conv_w, a_log, dt_bias, dn_norm_w, w_dn_branch, pool_w, pool_scale, w_pool_branch, w_mix_out, ln2_g, ln2_b, mem_ln_g, mem_ln_b, xa_wq, xa_wk, xa_wv, xa_wo, ln3_g, ln3_b, ffn2_w_gate, ffn2_w_up, ffn2_w_down, ln4_g, ln4_b, loss_target, m_ffn1_w_gate, m_ffn1_w_up, m_ffn1_w_down, m_ln1_g, m_ln1_b, m_w_in, m_conv_w, m_a_log, m_dt_bias, m_dn_norm_w, m_w_dn_branch, m_pool_w, m_pool_scale, m_w_pool_branch, m_w_mix_out, m_ln2_g, m_ln2_b, m_mem_ln_g, m_mem_ln_b, m_xa_wq, m_xa_wk, m_xa_wv, m_xa_wo, m_ln3_g, m_ln3_b, m_ffn2_w_gate, m_ffn2_w_up, m_ffn2_w_down, m_ln4_g, m_ln4_b, v_ffn1_w_gate, v_ffn1_w_up, v_ffn1_w_down, v_ln1_g, v_ln1_b, v_w_in, v_conv_w, v_a_log, v_dt_bias, v_dn_norm_w, v_w_dn_branch, v_pool_w, v_pool_scale, v_w_pool_branch, v_w_mix_out, v_ln2_g, v_ln2_b, v_mem_ln_g, v_mem_ln_b, v_xa_wq, v_xa_wk, v_xa_wv, v_xa_wo, v_ln3_g, v_ln3_b, v_ffn2_w_gate, v_ffn2_w_up, v_ffn2_w_down, v_ln4_g, v_ln4_b):
    given = dict(locals())
    shards = {n: given[n] for n in WEIGHT_NAMES}
    io = _Exchanges({n: shards[n][0] for n, _, _ in SHARDED})
    w = io.first_weights()
    for n in REPLICATED:
        w[n] = shards[n][0] if n == "pool_w" else shards[n]
    loss_part, grad_x, g = local_step(x[0], mem[0], loss_target[0], w, io)

    grad, updates = {}, {}

    def update(names, reduced):
        for n in names:
            if n in TRANSPOSED:
                outs = adamw("adamw_" + n, shards[n][0].T, reduced[n], given["m_" + n][0].T, given["v_" + n][0].T)
                grad[n], updates[n] = reduced[n].T[None], [o.T[None] for o in outs]
            else:
                grad[n] = reduced[n].reshape(shards[n].shape)
                updates[n] = adamw("adamw_" + n, shards[n], grad[n], given["m_" + n], given["v_" + n])
        return updates[names[-1]][0]

    update(GROUPS["ffn2"] + GROUPS["xa"], io.reduced)
    io.grads_in("mixer", grad_x)
    done = update(GROUPS["mixer"], io.reduced)
    small = io.small_in(done)
    loss = small.pop("loss")[0]
    done = update(REPLICATED, small)
    io.grads_in("ffn1_d", done)
    done = update(GROUPS["ffn1_d"], io.reduced)
    io.grads_in("ffn1_gu", done)
    update(GROUPS["ffn1_gu"], io.reduced)
    return (loss, grad_x[None], *[grad[n] for n in WEIGHT_NAMES], *[updates[n][0] for n in WEIGHT_NAMES],
            *[updates[n][1] for n in WEIGHT_NAMES], *[updates[n][2] for n in WEIGHT_NAMES])
```

```python
import functools

import jax
import jax.numpy as jnp
from jax import lax
from jax.experimental import pallas as pl
from jax.experimental.pallas import tpu as pltpu

F32 = jnp.float32
BF16 = jnp.bfloat16
MMD = BF16
WIRE = BF16
X3 =lax.Precision.HIGH
VMEM_LIMIT_BYTES = 48 * 1024 * 1024

D_MODEL = 1024
D_FF = 2816
CHUNK = 64
N_MEM = 256
DN_HEADS = 4
HD = 128
DN_WIDTH = 512
POOL_WINDOWS = (2, 4, 8, 16)
POOL_WIDTH = 512
XA_HEADS = 4
XA_HD = 256
LN_EPS = 1e-5
RMS_EPS = 1e-6
L2_EPS = 1e-6
ALPHA = 2.0 ** 0.25
HALO = 16
ROWS = 512
ROWS_WIDE = 256

ADAM_LR = 0.001
ADAM_B1 = 0.9
ADAM_B2 = 0.999
ADAM_EPS = 1e-08
ADAM_WD = 0.01
ADAM_STEP = 10

N_DEV = 8
LANES = 1024
ANY = pl.BlockSpec(memory_space=pl.ANY)


def _dot(a, b, ca, cb, prec):
    dn = (((ca,), (cb,)), ((), ()))
    if prec is not None:
        return lax.dot_general(a.astype(F32), b.astype(F32), dn, precision=prec, preferred_element_type=F32)
    return lax.dot_general(a.astype(MMD), b.astype(MMD), dn, preferred_element_type=F32)


def dnn(a, b, prec=None):
    return _dot(a, b, 1, 0, prec)


def dnt(a, b, prec=None):
    return _dot(a, b, 1, 1, prec)


def dtn(a, b, prec=None):
    return _dot(a, b, 0, 0, prec)


def _sigmoid(x):
    return jax.nn.sigmoid(x)


def _silu(x):
    return x * _sigmoid(x)


def _dsilu(x):
    s = _sigmoid(x)
    return s * (1.0 + x * (1.0 - s))


def _softplus(x):
    return jnp.maximum(x, 0.0) + jnp.log1p(jnp.exp(-jnp.abs(x)))


def _iota(shape, dim):
    return lax.broadcasted_iota(jnp.int32, shape, dim)


def _rsum(x):
    return jnp.sum(x, axis=1, keepdims=True)


def _csum(x):
    return jnp.sum(x, axis=0, keepdims=True)


def _pick(n, cands):
    for c in cands:
        if n % c == 0:
            return c
    return n


def _params(sem):
    return pltpu.CompilerParams(dimension_semantics=sem, vmem_limit_bytes=VMEM_LIMIT_BYTES)


MM_TILE_SIZES = (4096, 2816, 2048, 1536, 1408, 1024, 768, 512, 384, 256, 128)
MM_VMEM_BUDGET = 36 * 1024 * 1024
HBM_BYTES_PER_US = 3.0e6
GRID_STEP_US = 0.35


def _mm_tiles(m, n, kc, a_bytes, b_bytes, o_bytes):
    def sizes(d):
        return [d] if d <= 512 else [t for t in MM_TILE_SIZES if d % t == 0]

    best = None
    for tm in sizes(m):
        for tn in sizes(n):
            for tk in sizes(kc):
                vmem = 2 * (tm * tk * a_bytes + tk * tn * b_bytes + tm * tn * o_bytes) + tm * tn * 4
                if vmem > MM_VMEM_BUDGET:
                    continue
                steps = (m // tm) * (n // tn) * (kc // tk)
                traffic = m * kc * a_bytes * (n // tn) + kc * n * b_bytes * (m // tm) + m * n * o_bytes
                edge = tm * tk * a_bytes + tk * tn * b_bytes + tm * tn * o_bytes
                cost = (traffic + edge) / HBM_BYTES_PER_US + steps * GRID_STEP_US
                if best is None or cost < best[0]:
                    best = (cost, tm, tn, tk)
    return best[1:]


def mm(name, a, b, *, ta=False, tb=False, out_dtype=F32, add=None, scale=None, deps=()):
    adds = [] if add is None else (list(add) if isinstance(add, (list, tuple)) else [(1.0, add)])
    if ta:
        kc, m = a.shape
    else:
        m, kc = a.shape
    if tb:
        n, kb = b.shape
    else:
        kb, n = b.shape
    assert kc == kb, (name, a.shape, b.shape)
    tm, tn, tk = _mm_tiles(m, n, kc, a.dtype.itemsize, b.dtype.itemsize,
                           jnp.dtype(out_dtype).itemsize * (1 + len(adds)))
    nk = kc // tk
    grid = (m // tm, n // tn, nk)
    a_spec = pl.BlockSpec((tk, tm), lambda i, j, k: (k, i)) if ta else pl.BlockSpec((tm, tk), lambda i, j, k: (i, k))
    b_spec = pl.BlockSpec((tn, tk), lambda i, j, k: (j, k)) if tb else pl.BlockSpec((tk, tn), lambda i, j, k: (k, j))
    o_spec = pl.BlockSpec((tm, tn), lambda i, j, k: (i, j))
    ca, cb = (0 if ta else 1), (1 if tb else 0)

    def body(*refs):
        a_ref, b_ref = refs[0], refs[1]
        o_ref = refs[-1] if nk == 1 else refs[-2]
        k = pl.program_id(2)
        part = _dot(a_ref[...], b_ref[...], ca, cb, None)

        def finish(r):
            if scale is not None:
                r = r * scale
            for (coef, _), add_ref in zip(adds, refs[2:2 + len(adds)]):
                r = r + (add_ref[...] if coef == 1.0 else coef * add_ref[...])
            o_ref[...] = r.astype(o_ref.dtype)

        if nk == 1:
            finish(part)
            return
        acc_ref = refs[-1]

        @pl.when(k == 0)
        def _():
            acc_ref[...] = part

        if nk > 2:
            @pl.when((k > 0) & (k < nk - 1))
            def _():
                acc_ref[...] += part

        @pl.when(k == nk - 1)
        def _():
            finish(acc_ref[...] + part)

    ins = [a, b] + [t for _, t in adds] + list(deps)
    specs = [a_spec, b_spec] + [o_spec] * len(adds) + [ANY] * len(deps)
    return pl.pallas_call(
        body, name=name, grid=grid, in_specs=specs, out_specs=o_spec,
        out_shape=jax.ShapeDtypeStruct((m, n), out_dtype),
        scratch_shapes=[pltpu.VMEM((tm, tn), F32)] if nk > 1 else [],
        compiler_params=_params(("parallel", "parallel", "arbitrary")),
    )(*ins)


def mm_sum(name, pairs, deps=()):
    m, n = pairs[0][0].shape[0], pairs[0][1].shape[1]
    tm = min(512, m)
    np_ = len(pairs)

    def body(*refs):
        acc = dnn(refs[0][...], refs[1][...])
        for p in range(1, np_):
            acc = acc + dnn(refs[2 * p][...], refs[2 * p + 1][...])
        refs[-1][...] = acc

    specs, ins = [], []
    for a, b in pairs:
        specs += [pl.BlockSpec((tm, a.shape[1]), lambda i: (i, 0)), pl.BlockSpec(b.shape, lambda i: (0, 0))]
        ins += [a, b]
    return pl.pallas_call(
        body, name=name, grid=(m // tm,), in_specs=specs + [ANY] * len(deps),
        out_specs=pl.BlockSpec((tm, n), lambda i: (i, 0)), out_shape=jax.ShapeDtypeStruct((m, n), F32),
        compiler_params=_params(("parallel",)),
    )(*ins, *deps)


class _Ctx:
    def __init__(self, i, nblk, tl):
        self.i, self.nblk, self.tl = i, nblk, tl


def _norm_item(it):
    if isinstance(it, tuple):
        a, w, j = it[:3]
        rows = it[3] if len(it) > 3 else None
        return a, w, j, rows
    return it, it.shape[-1], 0, None


def rowwise(name, fn, length, tl, *, rows=(), consts=(), prevs=(), nexts=(), out_rows=(), out_accs=(), deps=()):
    nblk = length // tl
    hb = tl // HALO
    nhalo = length // HALO
    arrays, specs = [], []
    for it in rows:
        a, w, j, r = _norm_item(it)
        if a.ndim == 3:
            specs.append(pl.BlockSpec((a.shape[0], tl, w), lambda i, j=j: (0, i, j)))
        else:
            specs.append(pl.BlockSpec((r or tl, w), lambda i, j=j: (i, j)))
        arrays.append(a)
    for a in consts:
        specs.append(pl.BlockSpec(a.shape, lambda i, nd=a.ndim: (0,) * nd))
        arrays.append(a)
    for it in prevs:
        a, w, j, _ = _norm_item(it)
        specs.append(pl.BlockSpec((HALO, w), lambda i, j=j: (jnp.maximum(i * hb - 1, 0), j)))
        arrays.append(a)
    for it in nexts:
        a, w, j, _ = _norm_item(it)
        specs.append(pl.BlockSpec((HALO, w), lambda i, j=j: (jnp.minimum((i + 1) * hb, nhalo - 1), j)))
        arrays.append(a)
    out_shape, out_specs = [], []
    for spec in out_rows:
        if len(spec) == 3:
            h, w, dt = spec
            out_shape.append(jax.ShapeDtypeStruct((h, length, w), dt))
            out_specs.append(pl.BlockSpec((h, tl, w), lambda i: (0, i, 0)))
        else:
            w, dt = spec
            out_shape.append(jax.ShapeDtypeStruct((length, w), dt))
            out_specs.append(pl.BlockSpec((tl, w), lambda i: (i, 0)))
    for shape, dt in out_accs:
        out_shape.append(jax.ShapeDtypeStruct(shape, dt))
        out_specs.append(pl.BlockSpec(shape, lambda i, nd=len(shape): (0,) * nd))
    n_r, n_c, n_p, n_n = len(rows), len(consts), len(prevs), len(nexts)
    n_in = n_r + n_c + n_p + n_n
    n_or = len(out_rows)
    arrays, specs = arrays + list(deps), specs + [ANY] * len(deps)

    def body(*refs):
        i = pl.program_id(0)
        vals = [r[...] for r in refs[:n_in]]
        outs = refs[n_in + len(deps):]
        ctx = _Ctx(i, nblk, tl)
        ro, ao = fn(ctx, vals[:n_r], vals[n_r:n_r + n_c], vals[n_r + n_c:n_r + n_c + n_p], vals[n_r + n_c + n_p:])
        for r, v in zip(outs[:n_or], ro, strict=True):
            r[...] = v.astype(r.dtype)
        for r, v in zip(outs[n_or:], ao, strict=True):
            @pl.when(i == 0)
            def _(r=r, v=v):
                r[...] = v.astype(r.dtype)

            @pl.when(i > 0)
            def _(r=r, v=v):
                r[...] += v.astype(r.dtype)

    res = pl.pallas_call(
        body, name=name, grid=(nblk,), in_specs=specs, out_specs=out_specs, out_shape=out_shape,
        compiler_params=_params(("arbitrary",) if out_accs else ("parallel",)),
    )(*arrays)
    return res


def _heads(x, n, w):
    return [x[:, h * w:(h + 1) * w] for h in range(n)]


def _cat(xs):
    return jnp.concatenate(xs, axis=1)


def _row_index(ctx, nrows, offset=0):
    return ctx.i * ctx.tl + offset + _iota((nrows, 1), 0)


def _ln_stats(r):
    mu = jnp.mean(r, axis=1, keepdims=True)
    d = r - mu
    var = jnp.mean(d * d, axis=1, keepdims=True)
    rstd = lax.rsqrt(var + LN_EPS)
    return d * rstd, rstd


def ln_fwd(name, terms, g, b, tl=ROWS, deps=()):
    coefs = [c for c, _ in terms]
    length = terms[0][1].shape[0]

    def fn(ctx, rows, consts, prevs, nexts):
        r = sum(c * t for c, t in zip(coefs, rows))
        xh, _ = _ln_stats(r)
        return [xh * consts[0] + consts[1], r], []

    return rowwise(name, fn, length, min(tl, length), rows=[t for _, t in terms], consts=[g, b],
                   out_rows=[(D_MODEL, F32), (D_MODEL, F32)], deps=deps)


def ln_bwd(name, r, terms, g, tl=ROWS, deps=()):
    coefs = [c for c, _ in terms]
    length = r.shape[0]

    def fn(ctx, rows, consts, prevs, nexts):
        xh, rstd = _ln_stats(rows[0])
        dy = sum(c * t for c, t in zip(coefs, rows[1:]))
        dxh = dy * consts[0]
        dr = rstd * (dxh - jnp.mean(dxh, axis=1, keepdims=True) - xh * jnp.mean(dxh * xh, axis=1, keepdims=True))
        return [dr], [_csum(dy * xh), _csum(dy)]

    return rowwise(name, fn, length, min(tl, length), rows=[r] + [t for _, t in terms], consts=[g],
                   out_rows=[(D_MODEL, F32)], out_accs=[((1, D_MODEL), F32), ((1, D_MODEL), F32)], deps=deps)


def ln_loss(name, terms, g, b, target, tl=ROWS):
    coefs = [c for c, _ in terms]
    length = target.shape[0]
    nt = len(terms)

    def fn(ctx, rows, consts, prevs, nexts):
        r = sum(c * t for c, t in zip(coefs, rows[:nt]))
        xh, _ = _ln_stats(r)
        err = xh * consts[0] + consts[1] - rows[nt]
        tot = _csum(_rsum(err * err)) * (0.5 / D_MODEL)
        return [err * (1.0 / D_MODEL), r], [jnp.broadcast_to(tot, (1, 128))]

    return rowwise(name, fn, length, min(tl, length), rows=[t for _, t in terms] + [target], consts=[g, b],
                   out_rows=[(D_MODEL, F32), (D_MODEL, F32)], out_accs=[((1, 128), F32)])


def _ffn_blocks(length):
    return min(512, length), D_FF // 2


def ffn_gate_up_act(name, x, wg, wu, deps=()):
    length = x.shape[0]
    tm, tn = _ffn_blocks(length)

    def body(x_ref, wg_ref, wu_ref, *rest):
        hg_ref, hu_ref, act_ref = rest[-3:]
        xb = x_ref[...].astype(MMD)
        hg = dnt(xb, wg_ref[...])
        hu = dnt(xb, wu_ref[...])
        hg_ref[...] = hg
        hu_ref[...] = hu
        act_ref[...] = (_silu(hg) * hu).astype(act_ref.dtype)

    row = pl.BlockSpec((tm, D_MODEL), lambda i, j: (i, 0))
    wsp = pl.BlockSpec((tn, D_MODEL), lambda i, j: (j, 0))
    osp = pl.BlockSpec((tm, tn), lambda i, j: (i, j))
    return pl.pallas_call(
        body, name=name, grid=(length // tm, D_FF // tn), in_specs=[row, wsp, wsp] + [ANY] * len(deps),
        out_specs=[osp] * 3,
        out_shape=[jax.ShapeDtypeStruct((length, D_FF), F32)] * 2 + [jax.ShapeDtypeStruct((length, D_FF), BF16)],
        compiler_params=_params(("parallel", "parallel")),
    )(x, wg, wu, *deps)


def ffn_dact(name, dr, wd, hg, hu, deps=()):
    length = dr.shape[0]
    tm, tn = _ffn_blocks(length)

    def body(dr_ref, wd_ref, hg_ref, hu_ref, *rest):
        dhg_ref, dhu_ref = rest[-2:]
        da = 0.5 * dnt(dr_ref[...], wd_ref[...])
        g = hg_ref[...]
        s = _sigmoid(g)
        dhg_ref[...] = (da * hu_ref[...] * (s * (1.0 + g * (1.0 - s)))).astype(dhg_ref.dtype)
        dhu_ref[...] = (da * (g * s)).astype(dhu_ref.dtype)

    row = pl.BlockSpec((tm, D_MODEL), lambda i, j: (i, 0))
    wsp = pl.BlockSpec((tn, D_MODEL), lambda i, j: (j, 0))
    osp = pl.BlockSpec((tm, tn), lambda i, j: (i, j))
    return pl.pallas_call(
        body, name=name, grid=(length // tm, D_FF // tn), in_specs=[row, wsp, osp, osp] + [ANY] * len(deps),
        out_specs=[osp] * 2, out_shape=[jax.ShapeDtypeStruct((length, D_FF), BF16)] * 2,
        compiler_params=_params(("parallel", "parallel")),
    )(dr, wd, hg, hu, *deps)


def ffn_fwd(tag, x, wg, wu, wd, deps=()):
    hg, hu, act = ffn_gate_up_act(tag + "_gate_up", x, wg, wu, deps)
    if callable(wd):
        wd = wd(act)
    f = mm(tag + "_down", act, wd)
    return f, (hg, hu, act), wd


def ffn_bwd(tag, x, res, dr, wg, wu, wd, deps=(), on_dw=None, also=None):
    on_dw = on_dw or (lambda which, dw: ())
    hg, hu, act = res
    dwd = mm(tag + "_dwd", act, dr, ta=True, scale=0.5, deps=deps)
    dhg, dhu = ffn_dact(tag + "_dact", dr, wd, hg, hu, deps=on_dw("down", dwd))
    dwg = mm(tag + "_dwg", dhg, x, ta=True)
    dwu = mm(tag + "_dwu", dhu, x, ta=True, deps=on_dw("gate", dwg))
    dx = mm(tag + "_dxg", dhg, wg, deps=on_dw("up", dwu))
    dx = mm(tag + "_dxu", dhu, wu, add=[(1.0, dx)] + ([also] if also else []))
    return dx, dwg, dwu, dwd


def _conv_taps(ext, taps, n):
    out = taps[3] * ext
    for j in range(3):
        out = out + taps[j] * pltpu.roll(ext, 3 - j, 0)
    return out


def _l2n(x):
    r = lax.rsqrt(_rsum(x * x) + L2_EPS)
    return x * r, r


def conv_fwd(name, pre, taps, tl=ROWS_WIDE, deps=()):
    length = pre.shape[0]
    tl = min(tl, length)

    def fn(ctx, rows, consts, prevs, nexts):
        prev = jnp.where(ctx.i > 0, prevs[0], 0.0)
        ext = jnp.concatenate([prev, rows[0]], axis=0)
        s = _silu(_conv_taps(ext, consts, tl + HALO)[HALO:])
        q = _cat([_l2n(x)[0] * (HD ** -0.5) for x in _heads(s[:, :DN_WIDTH], DN_HEADS, HD)])
        k = _cat([_l2n(x)[0] for x in _heads(s[:, DN_WIDTH:2 * DN_WIDTH], DN_HEADS, HD)])
        return [q, k, s[:, 2 * DN_WIDTH:]], []

    return rowwise(name, fn, length, tl, rows=[pre], consts=list(taps), prevs=[pre],
                   out_rows=[(DN_WIDTH, F32)] * 3, deps=deps)


def conv_bwd(name, pre, dq, dk, dv, taps, tl=ROWS_WIDE):
    length = pre.shape[0]
    tl = min(tl, length)
    n = tl + 2 * HALO

    def fn(ctx, rows, consts, prevs, nexts):
        last = ctx.i == ctx.nblk - 1
        prev = jnp.where(ctx.i > 0, prevs[0], 0.0)
        ext = jnp.concatenate([prev, rows[0], nexts[0]], axis=0)
        c = _conv_taps(ext, consts, n)
        sg = _sigmoid(c)
        s = c * sg
        zero = jnp.zeros((HALO, DN_WIDTH), F32)
        dqe, dke, dve = [jnp.concatenate([zero, rows[1 + t], jnp.where(last, 0.0, nexts[1 + t])], axis=0)
                         for t in range(3)]

        def l2_bwd(x, dy):
            y, r = _l2n(x)
            return r * (dy - y * _rsum(dy * y))

        dsq = _cat([l2_bwd(x, d * (HD ** -0.5)) for x, d in zip(_heads(s[:, :DN_WIDTH], DN_HEADS, HD),
                                                                 _heads(dqe, DN_HEADS, HD))])
        dsk = _cat([l2_bwd(x, d) for x, d in zip(_heads(s[:, DN_WIDTH:2 * DN_WIDTH], DN_HEADS, HD),
                                                  _heads(dke, DN_HEADS, HD))])
        dc = _cat([dsq, dsk, dve]) * (sg * (1.0 + c * (1.0 - sg)))
        dpre = consts[3] * dc
        for j in range(3):
            dpre = dpre + consts[j] * pltpu.roll(dc, n - (3 - j), 0)
        dc_cur = dc[HALO:HALO + tl]
        dws = [_csum(dc_cur * pltpu.roll(ext, 3 - j, 0)[HALO:HALO + tl]) for j in range(3)]
        dws.append(_csum(dc_cur * ext[HALO:HALO + tl]))
        return [dpre[HALO:HALO + tl]], dws

    return rowwise(name, fn, length, tl, rows=[pre, dq, dk, dv], consts=list(taps), prevs=[pre],
                   nexts=[pre, dq, dk, dv], out_rows=[(3 * DN_WIDTH, BF16)],
                   out_accs=[((1, 3 * DN_WIDTH), F32)] * 4)


def _gate_math(ab, alog, dtb):
    z = ab + dtb
    g = -jnp.exp(alog) * _softplus(z)
    beta = _sigmoid(ab)
    return z, g, beta


def gates_fwd(name, ab, alog, dtb, tl=ROWS):
    length = ab.shape[0]

    def fn(ctx, rows, consts, prevs, nexts):
        _, g, beta = _gate_math(rows[0], consts[0], consts[1])
        spread = [jnp.broadcast_to(v[:, h:h + 1], (v.shape[0], HD))
                  for v, first in ((g, 0), (beta, DN_HEADS)) for h in range(first, first + DN_HEADS)]
        return [_cat(spread[:DN_HEADS]), _cat(spread[DN_HEADS:])], []

    return rowwise(name, fn, length, min(tl, length), rows=[ab], consts=[alog, dtb],
                   out_rows=[(DN_WIDTH, F32)] * 2)


def gates_bwd(name, ab, dgb, dbb, alog, dtb, tl=ROWS):
    length = ab.shape[0]

    def fn(ctx, rows, consts, prevs, nexts):
        z, g, beta = _gate_math(rows[0], consts[0], consts[1])
        lane = _iota(g.shape, 1)
        dsmall = jnp.zeros_like(g)
        for h in range(DN_HEADS):
            dsmall = jnp.where(lane == h, rows[1][:, h * HD:h * HD + 1], dsmall)
            dsmall = jnp.where(lane == DN_HEADS + h, rows[2][:, h * HD:h * HD + 1], dsmall)
        is_a = lane < DN_HEADS
        da = jnp.where(is_a, dsmall * (-jnp.exp(consts[0])) * _sigmoid(z), 0.0)
        db = jnp.where((lane >= DN_HEADS) & (lane < 2 * DN_HEADS), dsmall * beta * (1.0 - beta), 0.0)
        return [da + db], [_csum(jnp.where(is_a, dsmall * g, 0.0)), _csum(da)]

    return rowwise(name, fn, length, min(tl, length), rows=[ab, dgb, dbb], consts=[alog, dtb],
                   out_rows=[(128, BF16)], out_accs=[((1, 128), F32)] * 2)


CPS = 2


def _chunk_scan_rows(x, suffix=False):
    n = x.shape[0]
    rc = _iota(x.shape, 0) & (CHUNK - 1)
    sh = 1
    while sh < CHUNK:
        if suffix:
            x = x + jnp.where(rc < CHUNK - sh, pltpu.roll(x, n - sh, 0), 0.0)
        else:
            x = x + jnp.where(rc >= sh, pltpu.roll(x, sh, 0), 0.0)
        sh *= 2
    return x


def _tri_inv(a_list, eye, bd):
    def each(f, *ls):
        return [f(*xs) for xs in zip(*ls)]

    dg = [jnp.where(bd, a, 0.0) for a in a_list]
    lo = each(lambda a, d: a - d, a_list, dg)
    n1 = [-d for d in dg]
    n2 = each(lambda n: dnn(n, n, X3), n1)
    n4 = each(lambda n: dnn(n, n, X3), n2)
    td = each(lambda p, s: dnn(eye + p, eye + s, X3), n1, n2)
    n8 = each(lambda n: dnn(n, n, X3), n4)
    td = each(lambda t, n: dnn(t, eye + n, X3), td, n4)
    td = each(lambda t, n: dnn(t, eye + n, X3), td, n8)
    m = each(lambda t, l: dnn(t, l, X3), td, lo)
    m2 = each(lambda x: dnn(x, x, X3), m)
    x = each(lambda p, s: dnn(eye - p, eye + s, X3), m, m2)
    return each(lambda p, t: dnn(p, t, X3), x, td)


def _chunk_common(q, k, v, gcb, bb):
    egb = jnp.exp(gcb)
    gc64 = gcb[:, :CHUNK]
    ii, jj = _iota((CHUNK, CHUNK), 0), _iota((CHUNK, CHUNK), 1)
    incl, strict = ii >= jj, ii > jj
    decay = jnp.exp(jnp.where(incl, gc64 - gc64.T, -jnp.inf))
    kb = k * bb
    vb = v * bb
    kbe = kb * egb
    pq = dnt(jnp.concatenate([kb, q], axis=0), k, X3)
    ekb = jnp.exp(gcb[CHUNK - 1:CHUNK, :] - gcb)
    return dict(egb=egb, decay=decay, kb=kb, vb=vb, kbe=kbe, pm=pq[:CHUNK], qm=pq[CHUNK:], ekb=ekb,
                incl=incl, strict=strict, ii=ii, jj=jj)


def _chunk_head(vals, ci, h):
    return [v[ci * CHUNK:(ci + 1) * CHUNK, h * HD:(h + 1) * HD] for v in vals]


def _assemble(per_chunk):
    return jnp.concatenate([_cat(hs) for hs in per_chunk], axis=0)


def _assemble3(per_chunk):
    return jnp.stack([jnp.concatenate([per_chunk[ci][h] for ci in range(CPS)], axis=0) for h in range(DN_HEADS)])


def delta_prep_fwd(name, q, k, v, gb, bb):
    length = q.shape[0]

    def fn(ctx, rows, consts, prevs, nexts):
        gcb_all = _chunk_scan_rows(rows[3])
        vals = [rows[0], rows[1], rows[2], gcb_all, rows[4]]
        units = [(ci, h) for ci in range(CPS) for h in range(DN_HEADS)]
        ins = [_chunk_head(vals, ci, h) for ci, h in units]
        cs = [_chunk_common(*i) for i in ins]
        eye = (cs[0]["ii"] == cs[0]["jj"]).astype(F32)
        ts = _tri_inv([jnp.where(c["strict"], c["pm"] * c["decay"], 0.0) for c in cs], eye,
                      (cs[0]["ii"] >> 4) == (cs[0]["jj"] >> 4))
        uws = [dnn(t, _cat([c["vb"], c["kbe"]]), X3) for t, c in zip(ts, cs)]

        def grid2(xs):
            return [xs[ci * DN_HEADS:(ci + 1) * DN_HEADS] for ci in range(CPS)]

        return [_assemble(grid2([uw[:, :HD] for uw in uws])), _assemble(grid2([uw[:, HD:] for uw in uws])),
                _assemble(grid2([i[0] * c["egb"] for i, c in zip(ins, cs)])),
                _assemble(grid2([i[1] * c["ekb"] for i, c in zip(ins, cs)])), gcb_all,
                _assemble3(grid2([c["qm"] * c["decay"] for c in cs])), _assemble3(grid2(ts))], []

    return rowwise(name, fn, length, CHUNK * CPS, rows=[q, k, v, gb, bb],
                   out_rows=[(DN_WIDTH, F32)] * 5 + [(DN_HEADS, CHUNK, F32)] * 2)


def delta_prep_bwd(name, q, k, v, gb, bb, t3, du, dw, dqd, dkd, dattn3, dgl):
    length = q.shape[0]

    def fn(ctx, rows, consts, prevs, nexts):
        gcb_all = _chunk_scan_rows(rows[3])
        vals = [rows[0], rows[1], rows[2], gcb_all] + list(rows[4:9])
        t3v, da3v, dglv = rows[9], rows[10], rows[11]
        units = [(ci, h) for ci in range(CPS) for h in range(DN_HEADS)]
        ins = [_chunk_head(vals, ci, h) for ci, h in units]
        cs = [_chunk_common(*i[:5]) for i in ins]
        ts = [t3v[h][ci * CHUNK:(ci + 1) * CHUNK] for ci, h in units]
        dattns = [jnp.where(c["incl"], da3v[h][ci * CHUNK:(ci + 1) * CHUNK], 0.0) for (ci, h), c in zip(units, cs)]
        duws = [_cat([i[5], i[6]]) for i in ins]
        dvks = [dtn(t, d, X3) for t, d in zip(ts, duws)]
        dts = [dnt(d, _cat([c["vb"], c["kbe"]]), X3) for d, c in zip(duws, cs)]
        dts = [dnt(d, t, X3) for d, t in zip(dts, ts)]
        das = [jnp.where(c["strict"], -dtn(t, d, X3), 0.0) for c, t, d in zip(cs, ts, dts)]
        dpqs = [jnp.concatenate([da * c["decay"], dat * c["decay"]], axis=0) for da, dat, c in zip(das, dattns, cs)]
        dpqks = [dnn(d, i[1], X3) for d, i in zip(dpqs, ins)]
        dkps = [dtn(d, jnp.concatenate([c["kb"], i[0]], axis=0), X3) for d, c, i in zip(dpqs, cs, ins)]
        dqs, dks, dvs, dgcs, dbs = [], [], [], [], []
        for (ci, h), i, c, dvk, da, dattn, dpqk, dkp in zip(units, ins, cs, dvks, das, dattns, dpqks, dkps):
            qh, kh, vh, _, bh, _, _, dqdh, dkdh = i
            dvb, dkbe = dvk[:, :HD], dvk[:, HD:]
            dkb = dpqk[:CHUNK] + dkbe * c["egb"]
            c1 = _rsum(dkbe * c["kb"] + dqdh * qh) * c["egb"]
            c2 = _rsum(dkdh * kh) * c["ekb"]
            e = (da * c["pm"] + dattn * c["qm"]) * c["decay"]
            dgc = c1 - c2 + _rsum(e) - _rsum(e.T)
            dgl_tot = jnp.max(dglv[ci * 8:(ci + 1) * 8, h * HD:(h + 1) * HD], axis=0, keepdims=True) + _csum(c2)
            dgcs.append(dgc + jnp.where(_iota((CHUNK, HD), 0) == CHUNK - 1, dgl_tot, 0.0))
            dqs.append(dpqk[CHUNK:] + dqdh * c["egb"])
            dks.append(dkp + dkdh * c["ekb"] + dkb * bh)
            dvs.append(dvb * bh)
            dbs.append(jnp.broadcast_to(_rsum(dkb * kh + dvb * vh), (CHUNK, HD)))

        def grid2(xs):
            return [xs[ci * DN_HEADS:(ci + 1) * DN_HEADS] for ci in range(CPS)]

        return [_assemble(grid2(dqs)), _assemble(grid2(dks)), _assemble(grid2(dvs)),
                _chunk_scan_rows(_assemble(grid2(dgcs)), suffix=True), _assemble(grid2(dbs))], []

    return rowwise(name, fn, length, CHUNK * CPS,
                   rows=[q, k, v, gb, bb, du, dw, dqd, dkd, t3, dattn3, (dgl, DN_WIDTH, 0, 8 * CPS)],
                   out_rows=[(DN_WIDTH, F32)] * 5)


SCAN_CHUNKS = 8


def _scan_chunks(n):
    return SCAN_CHUNKS if n % SCAN_CHUNKS == 0 else 1


def delta_scan_fwd(name, qd, kd, u, w, attn3, gcb):
    length = qd.shape[0]
    n = length // CHUNK
    sc = _scan_chunks(n)
    row = pl.BlockSpec((sc * CHUNK, DN_WIDTH), lambda c: (c, 0))
    sq = pl.BlockSpec((DN_HEADS, sc * CHUNK, CHUNK), lambda c: (0, c, 0))

    def body(qd_ref, kd_ref, u_ref, w_ref, attn_ref, gc_ref, o_ref, vn_ref, st_ref, s_ref):
        c = pl.program_id(0)

        @pl.when(c == 0)
        def _():
            s_ref[...] = jnp.zeros_like(s_ref)

        heads = range(DN_HEADS)
        sls = [pl.ds(h * HD, HD) for h in heads]
        ss = [s_ref[h] for h in heads]
        for ci in range(sc):
            rs = pl.ds(ci * CHUNK, CHUNK)
            ws = [dnn(w_ref[rs, sl], s) for sl, s in zip(sls, ss)]
            qs = [dnn(qd_ref[rs, sl], s) for sl, s in zip(sls, ss)]
            vns = [u_ref[rs, sl] - x for sl, x in zip(sls, ws)]
            avs = [dnn(attn_ref[h, rs, :], vn) for h, vn in zip(heads, vns)]
            kvs = [dtn(kd_ref[rs, sl], vn) for sl, vn in zip(sls, vns)]
            for h, sl in zip(heads, sls):
                st_ref[ci, h] = ss[h]
                o_ref[rs, sl] = qs[h] + avs[h]
                vn_ref[rs, sl] = vns[h]
            ss = [s * jnp.exp(gc_ref[pl.ds(ci * CHUNK + CHUNK - 1, 1), sl]) + kv for s, sl, kv in zip(ss, sls, kvs)]
        for h in heads:
            s_ref[h] = ss[h]

    return pl.pallas_call(
        body, name=name, grid=(n // sc,), in_specs=[row, row, row, row, sq, row],
        out_specs=[row, row, pl.BlockSpec((sc, DN_HEADS, HD, HD), lambda c: (c, 0, 0, 0))],
        out_shape=[jax.ShapeDtypeStruct((length, DN_WIDTH), F32), jax.ShapeDtypeStruct((length, DN_WIDTH), F32),
                   jax.ShapeDtypeStruct((n, DN_HEADS, HD, HD), F32)],
        scratch_shapes=[pltpu.VMEM((DN_HEADS, HD, HD), F32)],
        compiler_params=_params(("arbitrary",)),
    )(qd, kd, u, w, attn3, gcb)


def delta_scan_bwd(name, do, qd, kd, w, attn3, vn, st, gcb):
    length = qd.shape[0]
    n = length // CHUNK
    sc = _scan_chunks(n)
    nb = n // sc
    row = pl.BlockSpec((sc * CHUNK, DN_WIDTH), lambda c: (nb - 1 - c, 0))
    sq = pl.BlockSpec((DN_HEADS, sc * CHUNK, CHUNK), lambda c: (0, nb - 1 - c, 0))
    stb = pl.BlockSpec((sc, DN_HEADS, HD, HD), lambda c: (nb - 1 - c, 0, 0, 0))
    glb = pl.BlockSpec((sc * 8, DN_WIDTH), lambda c: (nb - 1 - c, 0))

    def body(do_ref, qd_ref, kd_ref, w_ref, attn_ref, vn_ref, st_ref, gc_ref,
             dqd_ref, dkd_ref, du_ref, dw_ref, dattn_ref, dgl_ref, ds_ref):
        c = pl.program_id(0)

        @pl.when(c == 0)
        def _():
            ds_ref[...] = jnp.zeros_like(ds_ref)

        heads = range(DN_HEADS)
        sls = [pl.ds(h * HD, HD) for h in heads]
        dsns = [ds_ref[h] for h in heads]
        for ci in reversed(range(sc)):
            rs = pl.ds(ci * CHUNK, CHUNK)
            ss = [st_ref[ci, h] for h in heads]
            dos = [do_ref[rs, sl] for sl in sls]
            vns = [vn_ref[rs, sl] for sl in sls]
            dvns = [dtn(attn_ref[h, rs, :], d) for h, d in zip(heads, dos)]
            dvns = [x + dnn(kd_ref[rs, sl], dsn) for x, sl, dsn in zip(dvns, sls, dsns)]
            qdos = [dtn(qd_ref[rs, sl], d) for sl, d in zip(sls, dos)]
            for h, sl in zip(heads, sls):
                dattn_ref[h, rs, :] = dnt(dos[h], vns[h])
                dqd_ref[rs, sl] = dnt(dos[h], ss[h])
                dkd_ref[rs, sl] = dnt(vns[h], dsns[h])
                du_ref[rs, sl] = dvns[h]
            dws = [dnt(dvn, s) for dvn, s in zip(dvns, ss)]
            wdvs = [dtn(w_ref[rs, sl], dvn) for sl, dvn in zip(sls, dvns)]
            nxt = []
            for h, sl in zip(heads, sls):
                egl = jnp.exp(gc_ref[pl.ds(ci * CHUNK + CHUNK - 1, 1), sl])
                dw_ref[rs, sl] = -dws[h]
                dgl_ref[pl.ds(ci * 8, 8), sl] = jnp.broadcast_to(_csum(_rsum(dsns[h] * ss[h])) * egl, (8, HD))
                nxt.append(dsns[h] * egl + qdos[h] - wdvs[h])
            dsns = nxt
        for h in heads:
            ds_ref[h] = dsns[h]

    return pl.pallas_call(
        body, name=name, grid=(nb,), in_specs=[row, row, row, row, sq, row, stb, row],
        out_specs=[row, row, row, row, sq, glb],
        out_shape=[jax.ShapeDtypeStruct((length, DN_WIDTH), F32)] * 4
        + [jax.ShapeDtypeStruct((DN_HEADS, length, CHUNK), F32), jax.ShapeDtypeStruct((n * 8, DN_WIDTH), F32)],
        scratch_shapes=[pltpu.VMEM((DN_HEADS, HD, HD), F32)],
        compiler_params=_params(("arbitrary",)),
    )(do, qd, kd, w, attn3, vn, st, gcb)


def onorm_fwd(name, o, z, nw, tl=ROWS):
    length = o.shape[0]

    def fn(ctx, rows, consts, prevs, nexts):
        outs = []
        for oh, zh in zip(_heads(rows[0], DN_HEADS, HD), _heads(rows[1], DN_HEADS, HD)):
            r = lax.rsqrt(jnp.mean(oh * oh, axis=1, keepdims=True) + RMS_EPS)
            outs.append(oh * r * consts[0] * _silu(zh))
        return [_cat(outs)], []

    return rowwise(name, fn, length, min(tl, length), rows=[o, z], consts=[nw], out_rows=[(DN_WIDTH, BF16)])[0]


def onorm_bwd(name, o, z, d_on, nw, tl=ROWS):
    length = o.shape[0]

    def fn(ctx, rows, consts, prevs, nexts):
        dos, dzs = [], []
        dnw = jnp.zeros((1, HD), F32)
        for oh, zh, dh in zip(*[_heads(r, DN_HEADS, HD) for r in rows]):
            r = lax.rsqrt(jnp.mean(oh * oh, axis=1, keepdims=True) + RMS_EPS)
            y = oh * r
            sz = _silu(zh)
            t = dh * sz * consts[0]
            dos.append(r * (t - y * jnp.mean(t * y, axis=1, keepdims=True)))
            dzs.append(dh * y * consts[0] * _dsilu(zh))
            dnw = dnw + _csum(dh * y * sz)
        return [_cat(dos), _cat(dzs)], [dnw]

    return rowwise(name, fn, length, min(tl, length), rows=[o, z, d_on], consts=[nw],
                   out_rows=[(DN_WIDTH, F32), (DN_WIDTH, BF16)], out_accs=[((1, HD), F32)])


def merge_fwd(name, gates, ydn, ypool, tl=ROWS_WIDE):
    length = ydn.shape[0]

    def fn(ctx, rows, consts, prevs, nexts):
        gt = rows[0]
        return [_sigmoid(gt[:, :D_MODEL]) * rows[1] + _sigmoid(gt[:, D_MODEL:]) * rows[2]], []

    return rowwise(name, fn, length, min(tl, length), rows=[gates, ydn, ypool], out_rows=[(D_MODEL, BF16)])[0]


def merge_bwd(name, gates, ydn, ypool, dm, tl=ROWS_WIDE):
    length = ydn.shape[0]

    def fn(ctx, rows, consts, prevs, nexts):
        gt, yd, yp, d = rows
        sd, sp = _sigmoid(gt[:, :D_MODEL]), _sigmoid(gt[:, D_MODEL:])
        dgates = _cat([d * yd * sd * (1.0 - sd), d * yp * sp * (1.0 - sp)])
        return [d * sd, d * sp, dgates], []

    return rowwise(name, fn, length, min(tl, length), rows=[gates, ydn, ypool, dm],
                   out_rows=[(D_MODEL, BF16), (D_MODEL, BF16), (2 * D_MODEL, BF16)])


def _trailing_sums(ext, upto):
    s, sh = ext, 1
    while sh < upto:
        s = s + pltpu.roll(s, sh, 0)
        sh *= 2
    return s


def _leading_sums(ext, upto, n):
    s, sh = ext, 1
    while sh < upto:
        s = s + pltpu.roll(s, n - sh, 0)
        sh *= 2
    return s


def _pool_mixed(ctx, p, prev, tl):
    prevm = jnp.where(ctx.i > 0, prev, 0.0)
    t1 = (_row_index(ctx, tl) + 1).astype(F32)
    outs = []
    for gi, win in enumerate(POOL_WINDOWS):
        sl = slice(gi * HD, (gi + 1) * HD)
        ext = jnp.concatenate([prevm[:, sl], p[:, sl]], axis=0)
        mean = _trailing_sums(ext, win)[HALO:] / jnp.minimum(t1, float(win))
        outs.append(mean - p[:, sl])
    return outs


def pool_fwd(name, p, pool_w, scale, tl=ROWS):
    length = p.shape[0]
    tl = min(tl, length)

    def fn(ctx, rows, consts, prevs, nexts):
        mixed = _pool_mixed(ctx, rows[0], prevs[0], tl)
        y = _cat([dnn(m, consts[0][gi]) for gi, m in enumerate(mixed)])
        return [y * consts[1]], []

    return rowwise(name, fn, length, tl, rows=[p], consts=[pool_w, scale], prevs=[p],
                   out_rows=[(POOL_WIDTH, BF16)])[0]


def pool_bwd(name, p, dpo, pool_w, scale, tl=ROWS):
    length = p.shape[0]
    tl = min(tl, length)
    n = tl + HALO

    def fn(ctx, rows, consts, prevs, nexts):
        last = ctx.i == ctx.nblk - 1
        mixed = _pool_mixed(ctx, rows[0], prevs[0], tl)
        dext = jnp.concatenate([rows[1], jnp.where(last, 0.0, nexts[0])], axis=0)
        t1 = (_row_index(ctx, n) + 1).astype(F32)
        dps, dws, dscs = [], [], []
        for gi, win in enumerate(POOL_WINDOWS):
            sl = slice(gi * HD, (gi + 1) * HD)
            wg = consts[0][gi]
            dyraw = dext[:, sl] * consts[1][:, sl]
            dmix = dnt(dyraw, wg)
            dws.append(dtn(mixed[gi], dyraw[:tl]))
            dscs.append(_csum(rows[1][:, sl] * dnn(mixed[gi], wg)))
            lead = _leading_sums(dmix / jnp.minimum(t1, float(win)), win, n)
            dps.append(lead[:tl] - dmix[:tl])
        return [_cat(dps)], [jnp.stack(dws), _cat(dscs)]

    return rowwise(name, fn, length, tl, rows=[p, dpo], consts=[pool_w, scale], prevs=[p], nexts=[dpo],
                   out_rows=[(POOL_WIDTH, BF16)],
                   out_accs=[((len(POOL_WINDOWS), HD, HD), F32), ((1, POOL_WIDTH), F32)])


def _xa_probs(qh, kh):
    s = dnt(qh, kh) * (XA_HD ** -0.5)
    e = jnp.exp(s - jnp.max(s, axis=1, keepdims=True))
    return e / _rsum(e)


def xattn_fwd(name, qx, kx, vx, tl=ROWS):
    length = qx.shape[0]

    def fn(ctx, rows, consts, prevs, nexts):
        outs = [dnn(_xa_probs(qh, kh), vh) for qh, kh, vh in
                zip(_heads(rows[0], XA_HEADS, XA_HD), _heads(consts[0], XA_HEADS, XA_HD),
                    _heads(consts[1], XA_HEADS, XA_HD))]
        return [_cat(outs)], []

    return rowwise(name, fn, length, min(tl, length), rows=[qx], consts=[kx, vx], out_rows=[(D_MODEL, BF16)])[0]


def xattn_bwd(name, qx, dox, kx, vx, tl=ROWS):
    length = qx.shape[0]

    def fn(ctx, rows, consts, prevs, nexts):
        dqs, dks, dvs = [], [], []
        for qh, dh, kh, vh in zip(_heads(rows[0], XA_HEADS, XA_HD), _heads(rows[1], XA_HEADS, XA_HD),
                                  _heads(consts[0], XA_HEADS, XA_HD), _heads(consts[1], XA_HEADS, XA_HD)):
            pr = _xa_probs(qh, kh)
            dpr = dnt(dh, vh)
            ds = pr * (dpr - _rsum(dpr * pr)) * (XA_HD ** -0.5)
            dqs.append(dnn(ds, kh))
            dks.append(dtn(ds, qh))
            dvs.append(dtn(pr, dh))
        return [_cat(dqs)], [_cat(dks), _cat(dvs)]

    return rowwise(name, fn, length, min(tl, length), rows=[qx, dox], consts=[kx, vx],
                   out_rows=[(D_MODEL, BF16)], out_accs=[((N_MEM, D_MODEL), F32)] * 2)


def local_step(x, mem, target, w, io):
    alog = jnp.pad(w["a_log"], ((0, 0), (0, 128 - DN_HEADS)))
    dtb = jnp.pad(w["dt_bias"], ((0, 0), (0, 128 - DN_HEADS)))

    f1, res1, w_down1 = ffn_fwd("ffn1", x, w["ffn1_w_gate"], w["ffn1_w_up"], io.ffn1_down, deps=io.rest_started())
    x1, r1 = ln_fwd("ln1", [(ALPHA, x), (0.5, f1)], w["ln1_g"], w["ln1_b"], deps=io.halfway("mixer", f1))
    w = dict(w, ffn1_w_down=w_down1, **io.weights("mixer", x1))
    taps = [w["conv_w"][j:j + 1] for j in range(4)]

    pre = mm("in_qkv", x1, w["in_qkv"], tb=True)
    z = mm("in_z", x1, w["in_z"], tb=True)
    gates = mm("in_gates", x1, w["in_gates"], tb=True)
    p = mm("in_p", x1, w["in_p"], tb=True)
    ab = mm("in_ab", x1, w["in_ab"], tb=True)
    q, k, v = conv_fwd("conv", pre, taps, deps=io.halfway("xa", pre))
    gb, bb = gates_fwd("gates", ab, alog, dtb)
    u, wd_, qd, kd, gcb, attn3, t3 = delta_prep_fwd("dprep", q, k, v, gb, bb)
    o, vn, st = delta_scan_fwd("dscan", qd, kd, u, wd_, attn3, gcb)
    on = onorm_fwd("onorm", o, z, w["dn_norm_w"])
    ydn = mm("dn_branch", on, w["w_dn_branch"], tb=True)
    po = pool_fwd("pool", p, w["pool_w"], w["pool_scale"])
    ypool = mm("pool_branch", po, w["w_pool_branch"], tb=True)
    merged = merge_fwd("merge", gates, ydn, ypool)
    mix = mm("mix_out", merged, w["w_mix_out"])
    x2, r2 = ln_fwd("ln2", [(ALPHA, x1), (1.0, mix)], w["ln2_g"], w["ln2_b"])

    w = dict(w, **io.weights("xa", x2))
    m, _ = ln_fwd("ln_mem", [(1.0, mem)], w["mem_ln_g"], w["mem_ln_b"])
    qx = mm("xa_q", x2, w["xa_wq"], deps=io.halfway("ffn2", x2))
    kx = mm("xa_k", m, w["xa_wk"])
    vx = mm("xa_v", m, w["xa_wv"])
    ox = xattn_fwd("xattn", qx, kx, vx)
    xa = mm("xa_o", ox, w["xa_wo"])
    x3, r3 = ln_fwd("ln3", [(ALPHA, x2), (1.0, xa)], w["ln3_g"], w["ln3_b"])
    w = dict(w, **io.weights("ffn2", x3))

    f2, res2, _ = ffn_fwd("ffn2", x3, w["ffn2_w_gate"], w["ffn2_w_up"], w["ffn2_w_down"])
    dy4, r4, loss = ln_loss("ln4_loss", [(ALPHA, x3), (0.5, f2)], w["ln4_g"], w["ln4_b"], target)

    g = {}
    dr4, g["ln4_g"], g["ln4_b"] = ln_bwd("ln4_b", r4, [(1.0, dy4)], w["ln4_g"])
    dx3, g["ffn2_w_gate"], g["ffn2_w_up"], g["ffn2_w_down"] = ffn_bwd(
        "ffn2b", x3, res2, dr4, w["ffn2_w_gate"], w["ffn2_w_up"], w["ffn2_w_down"])
    dep = io.grads_out("ffn2", g)
    dr3, g["ln3_g"], g["ln3_b"] = ln_bwd("ln3_b", r3, [(ALPHA, dr4), (1.0, dx3)], w["ln3_g"], deps=dep)

    dox = mm("xa_do", dr3, w["xa_wo"], tb=True)
    g["xa_wo"] = mm("xa_dwo", ox, dr3, ta=True)
    dqx, dkx, dvx = xattn_bwd("xattn_b", qx, dox, kx, vx)
    g["xa_wq"] = mm("xa_dwq", x2, dqx, ta=True)
    dx2 = mm("xa_dx", dqx, w["xa_wq"], tb=True)
    g["xa_wk"] = mm("xa_dwk", m, dkx, ta=True)
    g["xa_wv"] = mm("xa_dwv", m, dvx, ta=True)
    dmm = mm("xa_dmk", dkx, w["xa_wk"], tb=True, deps=io.grads_out("xa", g))
    dmm = mm("xa_dmv", dvx, w["xa_wv"], tb=True, add=dmm)
    _, g["mem_ln_g"], g["mem_ln_b"] = ln_bwd("ln_mem_b", mem, [(1.0, dmm)], w["mem_ln_g"])
    dr2, g["ln2_g"], g["ln2_b"] = ln_bwd("ln2_b", r2, [(ALPHA, dr3), (1.0, dx2)], w["ln2_g"])
    io.grads_in("ffn2", dr2)

    dmerged = mm("mix_dm", dr2, w["w_mix_out"], tb=True)
    g["w_mix_out"] = mm("mix_dw", merged, dr2, ta=True)
    d_ydn, d_ypool, d_gates = merge_bwd("merge_b", gates, ydn, ypool, dmerged)
    g["w_dn_branch"] = mm("dn_dw", d_ydn, on, ta=True)
    d_on = mm("dn_dx", d_ydn, w["w_dn_branch"])
    g["w_pool_branch"] = mm("pool_dw", d_ypool, po, ta=True)
    d_po = mm("pool_dx", d_ypool, w["w_pool_branch"])
    dp, g["pool_w"], g["pool_scale"] = pool_bwd("pool_b", p, d_po, w["pool_w"], w["pool_scale"])
    d_o, dz, g["dn_norm_w"] = onorm_bwd("onorm_b", o, z, d_on, w["dn_norm_w"])
    dqd, dkd, du, dw_, dattn3, dgl = delta_scan_bwd("dscan_b", d_o, qd, kd, wd_, attn3, vn, st, gcb)
    dq, dk, dv, dgb, dbb = delta_prep_bwd("dprep_b", q, k, v, gb, bb, t3, du, dw_, dqd, dkd, dattn3, dgl)
    dpre, dc0, dc1, dc2, dc3 = conv_bwd("conv_b", pre, dq, dk, dv, taps)
    g["conv_w"] = jnp.concatenate([dc0, dc1, dc2, dc3], axis=0)
    d_ab, dalog, ddtb = gates_bwd("gates_b", ab, dgb, dbb, alog, dtb)
    g["a_log"] = dalog[:, :DN_HEADS]
    g["dt_bias"] = ddtb[:, :DN_HEADS]
    g["in_qkv"] = mm("in_dwqkv", dpre, x1, ta=True)
    g["in_z"] = mm("in_dwz", dz, x1, ta=True)
    g["in_gates"] = mm("in_dwgates", d_gates, x1, ta=True)
    g["in_p"] = mm("in_dwp", dp, x1, ta=True)
    g["in_ab"] = mm("in_dwab", d_ab, x1, ta=True)
    io.grads_in("xa", g["in_ab"])
    dx1 = mm_sum("in_dx", [(dpre, w["in_qkv"]), (dz, w["in_z"]), (d_gates, w["in_gates"]), (dp, w["in_p"]),
                           (d_ab, w["in_ab"])], deps=io.grads_out("mixer", g))
    dr1, g["ln1_g"], g["ln1_b"] = ln_bwd("ln1_b", r1, [(ALPHA, dr2), (1.0, dx1)], w["ln1_g"])

    def on_dw(which, dw):
        name = "ffn1_w_" + which
        small = io.small_out(dict(g, loss=loss[0, :1])) if which == "down" else ()
        return small + io.grads_out(name, {name: dw})

    grad_x, g["ffn1_w_gate"], g["ffn1_w_up"], g["ffn1_w_down"] = ffn_bwd(
        "ffn1b", x, res1, dr1, w["ffn1_w_gate"], w["ffn1_w_up"], w["ffn1_w_down"], on_dw=on_dw, also=(ALPHA, dr1))
    return loss, grad_x, g


WEIGHT_NAMES = ['ffn1_w_gate', 'ffn1_w_up', 'ffn1_w_down', 'ln1_g', 'ln1_b', 'w_in', 'conv_w', 'a_log', 'dt_bias',
                'dn_norm_w', 'w_dn_branch', 'pool_w', 'pool_scale', 'w_pool_branch', 'w_mix_out', 'ln2_g', 'ln2_b',
                'mem_ln_g', 'mem_ln_b', 'xa_wq', 'xa_wk', 'xa_wv', 'xa_wo', 'ln3_g', 'ln3_b', 'ffn2_w_gate',
                'ffn2_w_up', 'ffn2_w_down', 'ln4_g', 'ln4_b']
SHARDED = [
    ("ffn1_w_gate", "cols", (1024, 352)), ("ffn1_w_up", "cols", (1024, 352)), ("ffn1_w_down", "rows", (352, 1024)),
    ("w_in", "cols", (1024, 577)), ("conv_w", "flat", (4, 192)), ("w_dn_branch", "cols", (512, 128)),
    ("w_pool_branch", "cols", (512, 128)), ("w_mix_out", "rows", (128, 1024)), ("xa_wq", "rows", (128, 1024)),
    ("xa_wk", "rows", (128, 1024)), ("xa_wv", "rows", (128, 1024)), ("xa_wo", "rows", (128, 1024)),
    ("ffn2_w_gate", "cols", (1024, 352)), ("ffn2_w_up", "cols", (1024, 352)), ("ffn2_w_down", "rows", (352, 1024)),
]
REPLICATED = [n for n in WEIGHT_NAMES if n not in {s[0] for s in SHARDED}]
ROW_ALIGN = 16
ROW_BLOCKS = (512, 384, 352, 256, 192, 176, 128)
GROUPS = {"ffn1_gu": ("ffn1_w_gate", "ffn1_w_up"), "ffn1_d": ("ffn1_w_down",),
          "ffn1_w_gate": ("ffn1_w_gate",), "ffn1_w_up": ("ffn1_w_up",), "ffn1_w_down": ("ffn1_w_down",),
          "mixer": ("w_in", "conv_w", "w_dn_branch", "w_pool_branch", "w_mix_out"),
          "xa": ("xa_wq", "xa_wk", "xa_wv", "xa_wo"),
          "ffn2": ("ffn2_w_gate", "ffn2_w_up", "ffn2_w_down")}
W_IN_COLS = 577
W_IN_PIECES = (("in_qkv", 0, 1536), ("in_z", 1536, 2048), ("in_ab", 2048, 2056), ("in_p", 2056, 2568),
               ("in_gates", 2568, 4616))


def _round_up(n, m):
    return -(-n // m) * m


def _layout():
    off, table = 0, {}
    for name, form, shape in SHARDED:
        valid = {"rows": shape[0], "cols": shape[1], "flat": 2}[form]
        width = {"rows": shape[1], "cols": shape[0], "flat": shape[0] * shape[1]}[form]
        rows = _round_up(valid, ROW_ALIGN)
        table[name] = (off, rows, valid, width, form, shape)
        off += rows
    return table


LAYOUT = _layout()


def _group_span(names):
    base = LAYOUT[names[0]][0]
    rows = LAYOUT[names[-1]][0] + LAYOUT[names[-1]][1] - base
    while not any(rows % b == 0 for b in ROW_BLOCKS):
        rows += ROW_ALIGN
    return base, rows


def _row_block(rows):
    return _pick(rows, ROW_BLOCKS)


def _pad_block(blk, rows):
    return jnp.pad(blk, ((0, rows - blk.shape[0]), (0, LANES - blk.shape[1])))


def pack_weight_shards(shards, names):
    parts, used = [], 0
    for name in names:
        off, rows, valid, width, form, _ = LAYOUT[name]
        s = shards[name]
        if form == "flat":
            flat = s.reshape(1, -1)
            hi = flat.astype(BF16)
            blk = jnp.concatenate([hi, (flat - hi.astype(F32)).astype(BF16)], axis=0)
        else:
            blk = (s.T if form == "cols" else s).astype(BF16)
        parts.append(_pad_block(blk, rows))
        used += rows
    if _group_span(names)[1] > used:
        parts.append(jnp.zeros((_group_span(names)[1] - used, LANES), BF16))
    return jnp.concatenate(parts, axis=0)


IN_AB_ROWS = 128


def _w_in_segments(first, last):
    segs = []
    for k in range(N_DEV):
        lo, hi = max(first, k * W_IN_COLS), min(last, (k + 1) * W_IN_COLS)
        if lo < hi:
            segs.append((k, lo - k * W_IN_COLS, lo - first, hi - lo))
    return segs


def w_in_pieces(name, gathered, off, rows):
    assert off % rows == 0
    sizes = [IN_AB_ROWS if piece == "in_ab" else last - first for piece, first, last in W_IN_PIECES]

    def body(src_ref, *outs):
        for o_ref, (piece, first, last) in zip(outs, W_IN_PIECES):
            if piece == "in_ab":
                o_ref[...] = jnp.zeros_like(o_ref)
            for k, src, dst, count in _w_in_segments(first, last):
                o_ref[pl.ds(dst, count), :] = src_ref[k, pl.ds(src, count), :]

    outs = pl.pallas_call(
        body, name=name, grid=(1,), in_specs=[pl.BlockSpec((N_DEV, rows, LANES), lambda i: (0, off // rows, 0))],
        out_specs=[pl.BlockSpec((n, LANES), lambda i: (0, 0)) for n in sizes],
        out_shape=[jax.ShapeDtypeStruct((n, LANES), gathered.dtype) for n in sizes],
        compiler_params=_params(("arbitrary",)),
    )(gathered)
    return {piece: o for (piece, _, _), o in zip(W_IN_PIECES, outs)}


def unpack_full_weights(gathered, names):
    out, base = {}, _group_span(names)[0]
    for name in names:
        off, rows, valid, width, form, shape = LAYOUT[name]
        seg = gathered[:, off - base:off - base + rows]
        if form == "flat":
            flat = seg[:, 0, :width].astype(F32) + seg[:, 1, :width].astype(F32)
            out[name] = flat.reshape((N_DEV,) + shape).transpose(1, 0, 2).reshape(shape[0], N_DEV * shape[1])
        elif name == "w_in":
            out.update(w_in_pieces("w_in_pieces", gathered, off - base, rows))
        else:
            out[name] = seg[:, :valid, :width].reshape(N_DEV * valid, width)
    return out


def pack_full_grads(grads, names, me):
    wire, own, used = [], [], 0
    for name in names:
        off, rows, valid, width, form, shape = LAYOUT[name]
        if form == "flat":
            full = grads[name].reshape(shape[0], N_DEV, shape[1]).transpose(1, 0, 2).reshape(N_DEV, 1, width)
        elif name == "w_in":
            full = jnp.concatenate([grads[piece][:last - first] for piece, first, last in W_IN_PIECES], axis=0)
            full = full.reshape(N_DEV, valid, width)
        else:
            full = grads[name].reshape(N_DEV, valid, width)
        pad = ((0, rows - full.shape[1]), (0, LANES - width))
        wire.append(jnp.pad(full.astype(WIRE), ((0, 0),) + pad))
        own.append(jnp.pad(lax.dynamic_index_in_dim(full, me, 0, keepdims=False), pad))
        used += rows
    if _group_span(names)[1] > used:
        wire.append(jnp.zeros((N_DEV, _group_span(names)[1] - used, LANES), WIRE))
        own.append(jnp.zeros((_group_span(names)[1] - used, LANES), F32))
    return jnp.concatenate(wire, axis=1), jnp.concatenate(own, axis=0)


TRANSPOSED = ("ffn1_w_gate", "ffn1_w_up", "ffn2_w_gate", "ffn2_w_up", "w_in")


def unpack_grad_shards(packed, names):
    out, base = {}, _group_span(names)[0]
    for name in names:
        off, rows, valid, width, form, shape = LAYOUT[name]
        off -= base
        if form == "flat":
            out[name] = packed[off, :width].reshape(shape)
        elif name in TRANSPOSED:
            out[name] = packed[off:off + valid, :width]
        elif form == "cols":
            out[name] = packed[off:off + valid, :width].T
        else:
            out[name] = packed[off:off + valid, :width]
    return out


SMALL_SHAPES = {n: (1024,) for n in REPLICATED}
SMALL_SHAPES.update(pool_w=(4, 128, 128), pool_scale=(512,), dn_norm_w=(128,), a_log=(4,), dt_bias=(4,))


SMALL_SHAPES["loss"] = (1,)
SMALL_NAMES = REPLICATED + ["loss"]


def _small_layout():
    off, table = 0, {}
    for name in SMALL_NAMES:
        numel = 1
        for d in SMALL_SHAPES[name]:
            numel *= d
        rows = _round_up(-(-numel // LANES), 8)
        table[name] = (off, rows, numel)
        off += rows
    return table, off


SMALL_LAYOUT, SMALL_ROWS = _small_layout()


def _to_rows(flat, rows):
    return jnp.pad(flat, (0, rows * LANES - flat.shape[0])).reshape(rows, LANES)


def pack_small(values):
    return jnp.concatenate([_to_rows(values[name].reshape(-1), SMALL_LAYOUT[name][1]) for name in SMALL_NAMES], axis=0)


def unpack_small(packed):
    out = {}
    for name in SMALL_NAMES:
        off, rows, numel = SMALL_LAYOUT[name]
        out[name] = packed[off:off + rows].reshape(-1)[:numel].reshape(SMALL_SHAPES[name])
    return out


MESH = pl.DeviceIdType.MESH


def _position():
    return lax.axis_index("x"), lax.axis_index("y"), lax.axis_index("c")


def _other_chips(x, y):
    return [(1 - x, y), (x, 1 - y), (1 - x, 1 - y)]


def all_gather(name, block):
    rows, n = block.shape

    def body(x_ref, out_ref, send_sems, recv_sems, local_sem):
        x, y, c = _position()
        me, sibling = (x, y, c), (x, y, 1 - c)
        chips = _other_chips(x, y)

        def slot(px, py, pc):
            return out_ref.at[4 * px + 2 * py + pc]

        def copy(k, blk, to, src=None):
            return pltpu.make_async_remote_copy(
                src_ref=slot(*blk) if src is None else src, dst_ref=slot(*blk),
                send_sem=send_sems.at[k], recv_sem=recv_sems.at[k], device_id=to, device_id_type=MESH)

        mine = pltpu.make_async_copy(x_ref, slot(*me), local_sem)
        mine.start()
        first = [copy(0, me, sibling, src=x_ref)]
        first += [copy(1 + j, me, (*chip, c), src=x_ref) for j, chip in enumerate(chips)]
        for cp in first:
            cp.start()
        passed = [copy(4 + j, (*chip, c), sibling) for j, chip in enumerate(chips)]
        for j, chip in enumerate(chips):
            copy(1 + j, (*chip, c), me).wait_recv()
            passed[j].start()
        copy(0, sibling, me).wait_recv()
        for j, chip in enumerate(chips):
            copy(4 + j, (*chip, 1 - c), me).wait_recv()
        for cp in first + passed:
            cp.wait_send()
        mine.wait()

    return pl.pallas_call(
        body, name=name, out_shape=jax.ShapeDtypeStruct((N_DEV, rows, n), block.dtype),
        in_specs=[ANY], out_specs=ANY,
        scratch_shapes=[pltpu.SemaphoreType.DMA((7,)), pltpu.SemaphoreType.DMA((7,)), pltpu.SemaphoreType.DMA(())],
    )(block)


HBM = pl.BlockSpec(memory_space=pltpu.HBM)
SEM = pl.BlockSpec(memory_space=pltpu.SEMAPHORE)
EFFECT = pltpu.SideEffectType.DATAFLOW_SIDE_EFFECTING


def _remote(src, dst, send_sem, recv_sem, to):
    return pltpu.make_async_remote_copy(src_ref=src, dst_ref=dst, send_sem=send_sem, recv_sem=recv_sem,
                                        device_id=to, device_id_type=MESH)


def split_start(name, bufs, n, make_copies):
    nb = len(bufs)

    def body(*refs):
        for out_cp, _ in make_copies(refs[:nb], refs[nb:nb + n], refs[nb + n:nb + 2 * n]):
            out_cp.start()
        refs[-1][...] = jnp.zeros_like(refs[-1])

    outs = pl.pallas_call(
        body, name=name,
        out_shape=tuple([pltpu.SemaphoreType.DMA(())] * (2 * n)) + tuple(pltpu.HBM(b.shape, b.dtype) for b in bufs)
        + (jax.ShapeDtypeStruct((8, 128), F32),),
        in_specs=[HBM] * nb,
        out_specs=tuple([SEM] * (2 * n) + [HBM] * nb + [pl.BlockSpec(memory_space=pltpu.VMEM)]),
        input_output_aliases={i: 2 * n + i for i in range(nb)},
        compiler_params=pltpu.CompilerParams(has_side_effects=EFFECT),
    )(*[pltpu.with_memory_space_constraint(b, pltpu.HBM) for b in bufs])
    return list(outs[:2 * n]), list(outs[2 * n:2 * n + nb]), outs[-1]


def split_wait(name, bufs, sems, n, make_copies, after):
    nb = len(bufs)

    def body(*refs):
        for out_cp, in_cp in make_copies(refs[:nb], refs[nb:nb + n], refs[nb + n:nb + 2 * n]):
            out_cp.wait_send()
            in_cp.wait_recv()

    outs = pl.pallas_call(
        body, name=name, out_shape=tuple(pltpu.HBM(b.shape, b.dtype) for b in bufs),
        in_specs=[HBM] * nb + [SEM] * (2 * n) + [ANY], out_specs=tuple([HBM] * nb),
        input_output_aliases={i: i for i in range(nb)},
        compiler_params=pltpu.CompilerParams(has_side_effects=EFFECT),
    )(*bufs, *sems, after)
    return list(outs)


def _gather_stage1(refs, send, recv):
    src, land = refs
    x, y, c = _position()
    peers = [(x, y, 1 - c)] + [(*chip, c) for chip in _other_chips(x, y)]
    return [(_remote(src, land.at[4 * x + 2 * y + c], send[k], recv[k], p),
             _remote(src, land.at[4 * p[0] + 2 * p[1] + p[2]], send[k], recv[k], p)) for k, p in enumerate(peers)]


def _gather_stage2(refs, send, recv):
    (land,) = refs
    x, y, c = _position()
    out = []
    for j, (px, py) in enumerate(_other_chips(x, y)):
        mine, theirs = land.at[4 * px + 2 * py + c], land.at[4 * px + 2 * py + 1 - c]
        out.append((_remote(mine, mine, send[j], recv[j], (x, y, 1 - c)),
                    _remote(theirs, theirs, send[j], recv[j], (x, y, 1 - c))))
    return out


def _flips():
    return [(a, b, d) for a in (0, 1) for b in (0, 1) for d in (0, 1) if a | b | d]


def _gather_direct(refs, send, recv):
    src, land = refs
    x, y, c = _position()
    out = []
    for k, (fx, fy, fc) in enumerate(_flips()):
        p = (1 - x if fx else x, 1 - y if fy else y, 1 - c if fc else c)
        out.append((_remote(src, land.at[4 * x + 2 * y + c], send[k], recv[k], p),
                    _remote(src, land.at[4 * p[0] + 2 * p[1] + p[2]], send[k], recv[k], p)))
    return out


def _scatter_direct(refs, send, recv):
    sendbuf, land = refs
    x, y, c = _position()
    out = []
    for k, (fx, fy, fc) in enumerate(_flips()):
        p = (1 - x if fx else x, 1 - y if fy else y, 1 - c if fc else c)
        cp = _remote(sendbuf.at[4 * p[0] + 2 * p[1] + p[2]], land.at[k], send[k], recv[k], p)
        out.append((cp, cp))
    return out


def _own_plus_slots(name, own, landed):
    n, rows, _ = landed.shape
    tr = _row_block(rows)

    def body(g_ref, l_ref, o_ref):
        acc = g_ref[...]
        for j in range(n):
            acc = acc + l_ref[j].astype(F32)
        o_ref[...] = acc

    return pl.pallas_call(
        body, name=name, grid=(rows // tr,),
        in_specs=[pl.BlockSpec((tr, LANES), lambda i: (i, 0)), pl.BlockSpec((n, tr, LANES), lambda i: (0, i, 0))],
        out_specs=pl.BlockSpec((tr, LANES), lambda i: (i, 0)),
        out_shape=jax.ShapeDtypeStruct((rows, LANES), F32), compiler_params=_params(("parallel",)),
    )(own, landed)


def _sum_slots(name, stack):
    n, rows, _ = stack.shape

    def body(s_ref, o_ref):
        acc = s_ref[0]
        for j in range(1, n):
            acc = acc + s_ref[j]
        o_ref[...] = acc

    return pl.pallas_call(
        body, name=name, in_specs=[pl.BlockSpec(stack.shape, lambda: (0, 0, 0))],
        out_specs=pl.BlockSpec((rows, LANES), lambda: (0, 0)), out_shape=jax.ShapeDtypeStruct((rows, LANES), F32),
    )(stack)


def adamw(name, w, g, m, v):
    shape = w.shape
    last = shape[-1]
    w2, g2, m2, v2 = [a.reshape(-1, last) for a in (w, g, m, v)]
    rows = w2.shape[0]
    tr = _pick(rows, (256, 176, 128))

    def body(w_ref, g_ref, m_ref, v_ref, d_ref, nm_ref, nv_ref):
        gg = g_ref[...]
        nm = ADAM_B1 * m_ref[...] + (1.0 - ADAM_B1) * gg
        nv = ADAM_B2 * v_ref[...] + (1.0 - ADAM_B2) * (gg * gg)
        m_hat = nm / (1.0 - ADAM_B1 ** ADAM_STEP)
        v_hat = nv / (1.0 - ADAM_B2 ** ADAM_STEP)
        d_ref[...] = -ADAM_LR * (m_hat / (jnp.sqrt(v_hat) + ADAM_EPS) + ADAM_WD * w_ref[...])
        nm_ref[...] = nm
        nv_ref[...] = nv

    spec = pl.BlockSpec((tr, last), lambda i: (i, 0))
    outs = pl.pallas_call(
        body, name=name, grid=(rows // tr,), in_specs=[spec] * 4, out_specs=[spec] * 3,
        out_shape=[jax.ShapeDtypeStruct((rows, last), F32)] * 3, compiler_params=_params(("parallel",)),
    )(w2, g2, m2, v2)
    return [o.reshape(shape) for o in outs]


def _landing(block_shape, dtype, own):
    x, y, c = _position()
    return lax.dynamic_update_slice(lax.empty((N_DEV,) + block_shape, dtype), own[None], (4 * x + 2 * y + c, 0, 0))


class _Exchanges:
    def __init__(self, shards):
        self.shards = shards
        self.pending = {}
        self.reduced = {}

    def first_weights(self):
        names = GROUPS["ffn1_gu"]
        return unpack_full_weights(all_gather("ag_ffn1_gu", pack_weight_shards(self.shards, names)), names)

    def rest_started(self):
        tokens = []
        block = pack_weight_shards(self.shards, GROUPS["ffn1_d"])
        sems, bufs, token = split_start("ag_ffn1_d_s", [block, _landing(block.shape, block.dtype, block)], N_DEV - 1,
                                        _gather_direct)
        self.pending["ffn1_d"] = (sems, bufs)
        tokens.append(token)
        for key in ("mixer", "xa", "ffn2"):
            block = pack_weight_shards(self.shards, GROUPS[key])
            sems, bufs, token = split_start(f"ag_{key}_s1", [block, _landing(block.shape, block.dtype, block)], 4,
                                            _gather_stage1)
            self.pending[key] = (sems, bufs)
            tokens.append(token)
        return tuple(tokens)

    def ffn1_down(self, after):
        sems, bufs = self.pending.pop("ffn1_d")
        _, gathered = split_wait("ag_ffn1_d_w", bufs, sems, N_DEV - 1, _gather_direct, after)
        return unpack_full_weights(gathered, GROUPS["ffn1_d"])["ffn1_w_down"]

    def halfway(self, key, after):
        sems, bufs = self.pending.pop(key)
        _, land = split_wait(f"ag_{key}_w1", bufs, sems, 4, _gather_stage1, after)
        sems, bufs, token = split_start(f"ag_{key}_s2", [land], 3, _gather_stage2)
        self.pending[key] = (sems, bufs)
        return (token,)

    def weights(self, key, after):
        sems, bufs = self.pending.pop(key)
        (gathered,) = split_wait(f"ag_{key}_w2", bufs, sems, 3, _gather_stage2, after)
        return unpack_full_weights(gathered, GROUPS[key])

    def grads_out(self, key, grads):
        x, y, c = _position()
        wire, own = pack_full_grads(grads, GROUPS[key], 4 * x + 2 * y + c)
        land = lax.empty((N_DEV - 1,) + wire.shape[1:], WIRE)
        sems, bufs, token = split_start(f"rs_{key}_start", [wire, land], N_DEV - 1, _scatter_direct)
        self.pending[key] = (sems, bufs, own)
        return (token,)

    def grads_in(self, key, after):
        sems, bufs, own = self.pending.pop(key)
        _, landed = split_wait(f"rs_{key}_wait", bufs, sems, N_DEV - 1, _scatter_direct, after)
        self.reduced.update(unpack_grad_shards(_own_plus_slots(f"rs_{key}_sum", own, landed), GROUPS[key]))

    def small_out(self, values):
        block = pack_small(values)
        sems, bufs, token = split_start("ag_small_s", [block, _landing(block.shape, block.dtype, block)], N_DEV - 1,
                                        _gather_direct)
        self.pending["small"] = (sems, bufs)
        return (token,)

    def small_in(self, after):
        sems, bufs = self.pending.pop("small")
        _, gathered = split_wait("ag_small_w", bufs, sems, N_DEV - 1, _gather_direct, after)
        return unpack_small(_sum_slots("small_sum", gathered))


def kernel(x, mem, ffn1_w_gate, ffn1_w_up, ffn1_w_down, ln1_g, ln1_b, w_in, conv_w, a_log, dt_bias, dn_norm_w, w_dn_branch, pool_w, pool_scale, w_pool_branch, w_mix_out, ln2_g, ln2_b, mem_ln_g, mem_ln_b, xa_wq, xa_wk, xa_wv, xa_wo, ln3_g, ln3_b, ffn2_w_gate, ffn2_w_up, ffn2_w_down, ln4_g, ln4_b, loss_target, m_ffn1_w_gate, m_ffn1_w_up, m_ffn1_w_down, m_ln1_g, m_ln1_b, m_w_in, m_conv_w, m_a_log, m_dt_bias, m_dn_norm_w, m_w_dn_branch, m_pool_w, m_pool_scale, m_w_pool_branch, m_w_mix_out, m_ln2_g, m_ln2_b, m_mem_ln_g, m_mem_ln_b, m_xa_wq, m_xa_wk, m_xa_wv, m_xa_wo, m_ln3_g, m_ln3_b, m_ffn2_w_gate, m_ffn2_w_up, m_ffn2_w_down, m_ln4_g, m_ln4_b, v_ffn1_w_gate, v_ffn1_w_up, v_ffn1_w_down, v_ln1_g, v_ln1_b, v_w_in, v_conv_w, v_a_log, v_dt_bias, v_dn_norm_w, v_w_dn_branch, v_pool_w, v_pool_scale, v_w_pool_branch, v_w_mix_out, v_ln2_g, v_ln2_b, v_mem_ln_g, v_mem_ln_b, v_xa_wq, v_xa_wk, v_xa_wv, v_xa_wo, v_ln3_g, v_ln3_b, v_ffn2_w_gate, v_ffn2_w_up, v_ffn2_w_down, v_ln4_g, v_ln4_b):
    given = dict(locals())
    shards = {n: given[n] for n in WEIGHT_NAMES}
    io = _Exchanges({n: shards[n][0] for n, _, _ in SHARDED})
    w = io.first_weights()
    for n in REPLICATED:
        w[n] = shards[n][0] if n == "pool_w" else shards[n]
    loss_part, grad_x, g = local_step(x[0], mem[0], loss_target[0], w, io)

    grad, updates = {}, {}

    def update(names, reduced):
        for n in names:
            if n in TRANSPOSED:
                outs = adamw("adamw_" + n, shards[n][0].T, reduced[n], given["m_" + n][0].T, given["v_" + n][0].T)
                grad[n], updates[n] = reduced[n].T[None], [o.T[None] for o in outs]
            else:
                grad[n] = reduced[n].reshape(shards[n].shape)
                updates[n] = adamw("adamw_" + n, shards[n], grad[n], given["m_" + n], given["v_" + n])
        return updates[names[-1]][0]

    update(GROUPS["ffn2"] + GROUPS["xa"], io.reduced)
    io.grads_in("mixer", grad_x)
    done = update(GROUPS["mixer"], io.reduced)
    small = io.small_in(done)
    loss = small.pop("loss")[0]
    done = update(REPLICATED, small)
    for n in ("ffn1_w_down", "ffn1_w_gate", "ffn1_w_up"):
        io.grads_in(n, done)
        done = update(GROUPS[n], io.reduced)
    return (loss, grad_x[None], *[grad[n] for n in WEIGHT_NAMES], *[updates[n][0] for n in WEIGHT_NAMES],
            *[updates[n][1] for n in WEIGHT_NAMES], *[updates[n][2] for n in WEIGHT_NAMES])
```

```python
import jax
import jax.numpy as jnp
from jax import lax
from jax.experimental import pallas as pl
from jax.experimental.pallas import tpu as pltpu

F32 = jnp.float32
BF16 = jnp.bfloat16
MMD = BF16
WIRE = BF16
X3 =lax.Precision.HIGH
VMEM_LIMIT_BYTES = 48 * 1024 * 1024

D_MODEL = 1024
D_FF = 2816
CHUNK = 64
N_MEM = 256
DN_HEADS = 4
HD = 128
DN_WIDTH = 512
POOL_WINDOWS = (2, 4, 8, 16)
POOL_WIDTH = 512
XA_HEADS = 4
XA_HD = 256
LN_EPS = 1e-5
RMS_EPS = 1e-6
L2_EPS = 1e-6
ALPHA = 2.0 ** 0.25
HALO = 16
ROWS = 512
ROWS_WIDE = 256

ADAM_LR = 0.001
ADAM_B1 = 0.9
ADAM_B2 = 0.999
ADAM_EPS = 1e-08
ADAM_WD = 0.01
ADAM_STEP = 10

N_DEV = 8
LANES = 1024
ANY = pl.BlockSpec(memory_space=pl.ANY)


def _dot(a, b, ca, cb, prec):
    dn = (((ca,), (cb,)), ((), ()))
    if prec is not None:
        return lax.dot_general(a.astype(F32), b.astype(F32), dn, precision=prec, preferred_element_type=F32)
    return lax.dot_general(a.astype(MMD), b.astype(MMD), dn, preferred_element_type=F32)


def dnn(a, b, prec=None):
    return _dot(a, b, 1, 0, prec)


def dnt(a, b, prec=None):
    return _dot(a, b, 1, 1, prec)


def dtn(a, b, prec=None):
    return _dot(a, b, 0, 0, prec)


def _sigmoid(x):
    return jax.nn.sigmoid(x)


def _silu(x):
    return x * _sigmoid(x)


def _dsilu(x):
    s = _sigmoid(x)
    return s * (1.0 + x * (1.0 - s))


def _softplus(x):
    return jnp.maximum(x, 0.0) + jnp.log1p(jnp.exp(-jnp.abs(x)))


def _iota(shape, dim):
    return lax.broadcasted_iota(jnp.int32, shape, dim)


def _rsum(x):
    return jnp.sum(x, axis=1, keepdims=True)


def _csum(x):
    return jnp.sum(x, axis=0, keepdims=True)


def _pick(n, cands):
    for c in cands:
        if n % c == 0:
            return c
    return n


def _params(sem):
    return pltpu.CompilerParams(dimension_semantics=sem, vmem_limit_bytes=VMEM_LIMIT_BYTES)


MM_TILE_SIZES = (4096, 2816, 2048, 1536, 1408, 1024, 768, 512, 384, 256, 128)
MM_VMEM_BUDGET = 36 * 1024 * 1024
HBM_BYTES_PER_US = 3.0e6
GRID_STEP_US = 0.35


def _mm_tiles(m, n, kc, a_bytes, b_bytes, o_bytes):
    def sizes(d):
        return [d] if d <= 512 else [t for t in MM_TILE_SIZES if d % t == 0]

    best = None
    for tm in sizes(m):
        for tn in sizes(n):
            for tk in sizes(kc):
                vmem = 2 * (tm * tk * a_bytes + tk * tn * b_bytes + tm * tn * o_bytes) + tm * tn * 4
                if vmem > MM_VMEM_BUDGET:
                    continue
                steps = (m // tm) * (n // tn) * (kc // tk)
                traffic = m * kc * a_bytes * (n // tn) + kc * n * b_bytes * (m // tm) + m * n * o_bytes
                edge = tm * tk * a_bytes + tk * tn * b_bytes + tm * tn * o_bytes
                cost = (traffic + edge) / HBM_BYTES_PER_US + steps * GRID_STEP_US
                if best is None or cost < best[0]:
                    best = (cost, tm, tn, tk)
    return best[1:]


def mm(name, a, b, *, ta=False, tb=False, out_dtype=F32, add=None, scale=None, deps=()):
    adds = [] if add is None else (list(add) if isinstance(add, (list, tuple)) else [(1.0, add)])
    if ta:
        kc, m = a.shape
    else:
        m, kc = a.shape
    if tb:
        n, kb = b.shape
    else:
        kb, n = b.shape
    assert kc == kb, (name, a.shape, b.shape)
    tm, tn, tk = _mm_tiles(m, n, kc, a.dtype.itemsize, b.dtype.itemsize,
                           jnp.dtype(out_dtype).itemsize * (1 + len(adds)))
    nk = kc // tk
    grid = (m // tm, n // tn, nk)
    a_spec = pl.BlockSpec((tk, tm), lambda i, j, k: (k, i)) if ta else pl.BlockSpec((tm, tk), lambda i, j, k: (i, k))
    b_spec = pl.BlockSpec((tn, tk), lambda i, j, k: (j, k)) if tb else pl.BlockSpec((tk, tn), lambda i, j, k: (k, j))
    o_spec = pl.BlockSpec((tm, tn), lambda i, j, k: (i, j))
    ca, cb = (0 if ta else 1), (1 if tb else 0)

    def body(*refs):
        a_ref, b_ref = refs[0], refs[1]
        o_ref = refs[-1] if nk == 1 else refs[-2]
        k = pl.program_id(2)
        part = _dot(a_ref[...], b_ref[...], ca, cb, None)

        def finish(r):
            if scale is not None:
                r = r * scale
            for (coef, _), add_ref in zip(adds, refs[2:2 + len(adds)]):
                r = r + (add_ref[...] if coef == 1.0 else coef * add_ref[...])
            o_ref[...] = r.astype(o_ref.dtype)

        if nk == 1:
            finish(part)
            return
        acc_ref = refs[-1]

        @pl.when(k == 0)
        def _():
            acc_ref[...] = part

        if nk > 2:
            @pl.when((k > 0) & (k < nk - 1))
            def _():
                acc_ref[...] += part

        @pl.when(k == nk - 1)
        def _():
            finish(acc_ref[...] + part)

    ins = [a, b] + [t for _, t in adds] + list(deps)
    specs = [a_spec, b_spec] + [o_spec] * len(adds) + [ANY] * len(deps)
    return pl.pallas_call(
        body, name=name, grid=grid, in_specs=specs, out_specs=o_spec,
        out_shape=jax.ShapeDtypeStruct((m, n), out_dtype),
        scratch_shapes=[pltpu.VMEM((tm, tn), F32)] if nk > 1 else [],
        compiler_params=_params(("parallel", "parallel", "arbitrary")),
    )(*ins)


def mm_sum(name, pairs, deps=()):
    m, n = pairs[0][0].shape[0], pairs[0][1].shape[1]
    tm = min(512, m)
    np_ = len(pairs)

    def body(*refs):
        acc = dnn(refs[0][...], refs[1][...])
        for p in range(1, np_):
            acc = acc + dnn(refs[2 * p][...], refs[2 * p + 1][...])
        refs[-1][...] = acc

    specs, ins = [], []
    for a, b in pairs:
        specs += [pl.BlockSpec((tm, a.shape[1]), lambda i: (i, 0)), pl.BlockSpec(b.shape, lambda i: (0, 0))]
        ins += [a, b]
    return pl.pallas_call(
        body, name=name, grid=(m // tm,), in_specs=specs + [ANY] * len(deps),
        out_specs=pl.BlockSpec((tm, n), lambda i: (i, 0)), out_shape=jax.ShapeDtypeStruct((m, n), F32),
        compiler_params=_params(("parallel",)),
    )(*ins, *deps)


class _Ctx:
    def __init__(self, i, nblk, tl):
        self.i, self.nblk, self.tl = i, nblk, tl


def _norm_item(it):
    if isinstance(it, tuple):
        a, w, j = it[:3]
        rows = it[3] if len(it) > 3 else None
        return a, w, j, rows
    return it, it.shape[-1], 0, None


def rowwise(name, fn, length, tl, *, rows=(), consts=(), prevs=(), nexts=(), out_rows=(), out_accs=(), deps=()):
    nblk = length // tl
    hb = tl // HALO
    nhalo = length // HALO
    arrays, specs = [], []
    for it in rows:
        a, w, j, r = _norm_item(it)
        if a.ndim == 3:
            specs.append(pl.BlockSpec((a.shape[0], tl, w), lambda i, j=j: (0, i, j)))
        else:
            specs.append(pl.BlockSpec((r or tl, w), lambda i, j=j: (i, j)))
        arrays.append(a)
    for a in consts:
        specs.append(pl.BlockSpec(a.shape, lambda i, nd=a.ndim: (0,) * nd))
        arrays.append(a)
    for it in prevs:
        a, w, j, _ = _norm_item(it)
        specs.append(pl.BlockSpec((HALO, w), lambda i, j=j: (jnp.maximum(i * hb - 1, 0), j)))
        arrays.append(a)
    for it in nexts:
        a, w, j, _ = _norm_item(it)
        specs.append(pl.BlockSpec((HALO, w), lambda i, j=j: (jnp.minimum((i + 1) * hb, nhalo - 1), j)))
        arrays.append(a)
    out_shape, out_specs = [], []
    for spec in out_rows:
        if len(spec) == 3:
            h, w, dt = spec
            out_shape.append(jax.ShapeDtypeStruct((h, length, w), dt))
            out_specs.append(pl.BlockSpec((h, tl, w), lambda i: (0, i, 0)))
        else:
            w, dt = spec
            out_shape.append(jax.ShapeDtypeStruct((length, w), dt))
            out_specs.append(pl.BlockSpec((tl, w), lambda i: (i, 0)))
    for shape, dt in out_accs:
        out_shape.append(jax.ShapeDtypeStruct(shape, dt))
        out_specs.append(pl.BlockSpec(shape, lambda i, nd=len(shape): (0,) * nd))
    n_r, n_c, n_p, n_n = len(rows), len(consts), len(prevs), len(nexts)
    n_in = n_r + n_c + n_p + n_n
    n_or = len(out_rows)
    arrays, specs = arrays + list(deps), specs + [ANY] * len(deps)

    def body(*refs):
        i = pl.program_id(0)
        vals = [r[...] for r in refs[:n_in]]
        outs = refs[n_in + len(deps):]
        ctx = _Ctx(i, nblk, tl)
        ro, ao = fn(ctx, vals[:n_r], vals[n_r:n_r + n_c], vals[n_r + n_c:n_r + n_c + n_p], vals[n_r + n_c + n_p:])
        for r, v in zip(outs[:n_or], ro, strict=True):
            r[...] = v.astype(r.dtype)
        for r, v in zip(outs[n_or:], ao, strict=True):
            @pl.when(i == 0)
            def _(r=r, v=v):
                r[...] = v.astype(r.dtype)

            @pl.when(i > 0)
            def _(r=r, v=v):
                r[...] += v.astype(r.dtype)

    res = pl.pallas_call(
        body, name=name, grid=(nblk,), in_specs=specs, out_specs=out_specs, out_shape=out_shape,
        compiler_params=_params(("arbitrary",) if out_accs else ("parallel",)),
    )(*arrays)
    return res


def _heads(x, n, w):
    return [x[:, h * w:(h + 1) * w] for h in range(n)]


def _cat(xs):
    return jnp.concatenate(xs, axis=1)


def _row_index(ctx, nrows, offset=0):
    return ctx.i * ctx.tl + offset + _iota((nrows, 1), 0)


def _ln_stats(r):
    mu = jnp.mean(r, axis=1, keepdims=True)
    d = r - mu
    var = jnp.mean(d * d, axis=1, keepdims=True)
    rstd = lax.rsqrt(var + LN_EPS)
    return d * rstd, rstd


def ln_fwd(name, terms, g, b, tl=ROWS, deps=()):
    coefs = [c for c, _ in terms]
    length = terms[0][1].shape[0]

    def fn(ctx, rows, consts, prevs, nexts):
        r = sum(c * t for c, t in zip(coefs, rows))
        xh, _ = _ln_stats(r)
        return [xh * consts[0] + consts[1], r], []

    return rowwise(name, fn, length, min(tl, length), rows=[t for _, t in terms], consts=[g, b],
                   out_rows=[(D_MODEL, F32), (D_MODEL, F32)], deps=deps)


def ln_bwd(name, r, terms, g, tl=ROWS, deps=()):
    coefs = [c for c, _ in terms]
    length = r.shape[0]

    def fn(ctx, rows, consts, prevs, nexts):
        xh, rstd = _ln_stats(rows[0])
        dy = sum(c * t for c, t in zip(coefs, rows[1:]))
        dxh = dy * consts[0]
        dr = rstd * (dxh - jnp.mean(dxh, axis=1, keepdims=True) - xh * jnp.mean(dxh * xh, axis=1, keepdims=True))
        return [dr], [_csum(dy * xh), _csum(dy)]

    return rowwise(name, fn, length, min(tl, length), rows=[r] + [t for _, t in terms], consts=[g],
                   out_rows=[(D_MODEL, F32)], out_accs=[((1, D_MODEL), F32), ((1, D_MODEL), F32)], deps=deps)


def ln_loss(name, terms, g, b, target, tl=ROWS):
    coefs = [c for c, _ in terms]
    length = target.shape[0]
    nt = len(terms)

    def fn(ctx, rows, consts, prevs, nexts):
        r = sum(c * t for c, t in zip(coefs, rows[:nt]))
        xh, _ = _ln_stats(r)
        err = xh * consts[0] + consts[1] - rows[nt]
        tot = _csum(_rsum(err * err)) * (0.5 / D_MODEL)
        return [err * (1.0 / D_MODEL), r], [jnp.broadcast_to(tot, (1, 128))]

    return rowwise(name, fn, length, min(tl, length), rows=[t for _, t in terms] + [target], consts=[g, b],
                   out_rows=[(D_MODEL, F32), (D_MODEL, F32)], out_accs=[((1, 128), F32)])


def _ffn_blocks(length):
    return min(512, length), D_FF // 2


def ffn_gate_up_act(name, x, wg, wu, deps=()):
    length = x.shape[0]
    tm, tn = _ffn_blocks(length)

    def body(x_ref, wg_ref, wu_ref, *rest):
        hg_ref, hu_ref, act_ref = rest[-3:]
        xb = x_ref[...].astype(MMD)
        hg = dnt(xb, wg_ref[...])
        hu = dnt(xb, wu_ref[...])
        hg_ref[...] = hg
        hu_ref[...] = hu
        act_ref[...] = (_silu(hg) * hu).astype(act_ref.dtype)

    row = pl.BlockSpec((tm, D_MODEL), lambda i, j: (i, 0))
    wsp = pl.BlockSpec((tn, D_MODEL), lambda i, j: (j, 0))
    osp = pl.BlockSpec((tm, tn), lambda i, j: (i, j))
    return pl.pallas_call(
        body, name=name, grid=(length // tm, D_FF // tn), in_specs=[row, wsp, wsp] + [ANY] * len(deps),
        out_specs=[osp] * 3,
        out_shape=[jax.ShapeDtypeStruct((length, D_FF), F32)] * 2 + [jax.ShapeDtypeStruct((length, D_FF), BF16)],
        compiler_params=_params(("parallel", "parallel")),
    )(x, wg, wu, *deps)


def ffn_dact(name, dr, wd, hg, hu, deps=()):
    length = dr.shape[0]
    tm, tn = _ffn_blocks(length)

    def body(dr_ref, wd_ref, hg_ref, hu_ref, *rest):
        dhg_ref, dhu_ref = rest[-2:]
        da = 0.5 * dnt(dr_ref[...], wd_ref[...])
        g = hg_ref[...]
        s = _sigmoid(g)
        dhg_ref[...] = (da * hu_ref[...] * (s * (1.0 + g * (1.0 - s)))).astype(dhg_ref.dtype)
        dhu_ref[...] = (da * (g * s)).astype(dhu_ref.dtype)

    row = pl.BlockSpec((tm, D_MODEL), lambda i, j: (i, 0))
    wsp = pl.BlockSpec((tn, D_MODEL), lambda i, j: (j, 0))
    osp = pl.BlockSpec((tm, tn), lambda i, j: (i, j))
    return pl.pallas_call(
        body, name=name, grid=(length // tm, D_FF // tn), in_specs=[row, wsp, osp, osp] + [ANY] * len(deps),
        out_specs=[osp] * 2, out_shape=[jax.ShapeDtypeStruct((length, D_FF), BF16)] * 2,
        compiler_params=_params(("parallel", "parallel")),
    )(dr, wd, hg, hu, *deps)


def ffn_fwd(tag, x, wg, wu, wd, deps=()):
    hg, hu, act = ffn_gate_up_act(tag + "_gate_up", x, wg, wu, deps)
    if callable(wd):
        wd = wd(act)
    f = mm(tag + "_down", act, wd)
    return f, (hg, hu, act), wd


def ffn_bwd(tag, x, res, dr, wg, wu, wd, deps=(), on_dw=None, also=None):
    on_dw = on_dw or (lambda which, dw: ())
    hg, hu, act = res
    dwd = mm(tag + "_dwd", act, dr, ta=True, scale=0.5, deps=deps)
    dhg, dhu = ffn_dact(tag + "_dact", dr, wd, hg, hu, deps=on_dw("down", dwd))
    dwg = mm(tag + "_dwg", dhg, x, ta=True)
    dwu = mm(tag + "_dwu", dhu, x, ta=True, deps=on_dw("gate", dwg))
    dx = mm(tag + "_dxg", dhg, wg, deps=on_dw("up", dwu))
    dx = mm(tag + "_dxu", dhu, wu, add=[(1.0, dx)] + ([also] if also else []))
    return dx, dwg, dwu, dwd


def _conv_taps(ext, taps, n):
    out = taps[3] * ext
    for j in range(3):
        out = out + taps[j] * pltpu.roll(ext, 3 - j, 0)
    return out


def _l2n(x):
    r = lax.rsqrt(_rsum(x * x) + L2_EPS)
    return x * r, r


def conv_fwd(name, pre, taps, tl=ROWS_WIDE, deps=()):
    length = pre.shape[0]
    tl = min(tl, length)

    def fn(ctx, rows, consts, prevs, nexts):
        prev = jnp.where(ctx.i > 0, prevs[0], 0.0)
        ext = jnp.concatenate([prev, rows[0]], axis=0)
        s = _silu(_conv_taps(ext, consts, tl + HALO)[HALO:])
        q = _cat([_l2n(x)[0] * (HD ** -0.5) for x in _heads(s[:, :DN_WIDTH], DN_HEADS, HD)])
        k = _cat([_l2n(x)[0] for x in _heads(s[:, DN_WIDTH:2 * DN_WIDTH], DN_HEADS, HD)])
        return [q, k, s[:, 2 * DN_WIDTH:]], []

    return rowwise(name, fn, length, tl, rows=[pre], consts=list(taps), prevs=[pre],
                   out_rows=[(DN_WIDTH, F32)] * 3, deps=deps)


def conv_bwd(name, pre, dq, dk, dv, taps, tl=ROWS_WIDE):
    length = pre.shape[0]
    tl = min(tl, length)
    n = tl + 2 * HALO

    def fn(ctx, rows, consts, prevs, nexts):
        last = ctx.i == ctx.nblk - 1
        prev = jnp.where(ctx.i > 0, prevs[0], 0.0)
        ext = jnp.concatenate([prev, rows[0], nexts[0]], axis=0)
        c = _conv_taps(ext, consts, n)
        sg = _sigmoid(c)
        s = c * sg
        zero = jnp.zeros((HALO, DN_WIDTH), F32)
        dqe, dke, dve = [jnp.concatenate([zero, rows[1 + t], jnp.where(last, 0.0, nexts[1 + t])], axis=0)
                         for t in range(3)]

        def l2_bwd(x, dy):
            y, r = _l2n(x)
            return r * (dy - y * _rsum(dy * y))

        dsq = _cat([l2_bwd(x, d * (HD ** -0.5)) for x, d in zip(_heads(s[:, :DN_WIDTH], DN_HEADS, HD),
                                                                 _heads(dqe, DN_HEADS, HD))])
        dsk = _cat([l2_bwd(x, d) for x, d in zip(_heads(s[:, DN_WIDTH:2 * DN_WIDTH], DN_HEADS, HD),
                                                  _heads(dke, DN_HEADS, HD))])
        dc = _cat([dsq, dsk, dve]) * (sg * (1.0 + c * (1.0 - sg)))
        dpre = consts[3] * dc
        for j in range(3):
            dpre = dpre + consts[j] * pltpu.roll(dc, n - (3 - j), 0)
        dc_cur = dc[HALO:HALO + tl]
        dws = [_csum(dc_cur * pltpu.roll(ext, 3 - j, 0)[HALO:HALO + tl]) for j in range(3)]
        dws.append(_csum(dc_cur * ext[HALO:HALO + tl]))
        return [dpre[HALO:HALO + tl]], dws

    return rowwise(name, fn, length, tl, rows=[pre, dq, dk, dv], consts=list(taps), prevs=[pre],
                   nexts=[pre, dq, dk, dv], out_rows=[(3 * DN_WIDTH, BF16)],
                   out_accs=[((1, 3 * DN_WIDTH), F32)] * 4)


def _gate_math(ab, alog, dtb):
    z = ab + dtb
    g = -jnp.exp(alog) * _softplus(z)
    beta = _sigmoid(ab)
    return z, g, beta


def gates_fwd(name, ab, alog, dtb, tl=ROWS):
    length = ab.shape[0]

    def fn(ctx, rows, consts, prevs, nexts):
        _, g, beta = _gate_math(rows[0], consts[0], consts[1])
        spread = [jnp.broadcast_to(v[:, h:h + 1], (v.shape[0], HD))
                  for v, first in ((g, 0), (beta, DN_HEADS)) for h in range(first, first + DN_HEADS)]
        return [_cat(spread[:DN_HEADS]), _cat(spread[DN_HEADS:])], []

    return rowwise(name, fn, length, min(tl, length), rows=[ab], consts=[alog, dtb],
                   out_rows=[(DN_WIDTH, F32)] * 2)


def gates_bwd(name, ab, dgb, dbb, alog, dtb, tl=ROWS):
    length = ab.shape[0]

    def fn(ctx, rows, consts, prevs, nexts):
        z, g, beta = _gate_math(rows[0], consts[0], consts[1])
        lane = _iota(g.shape, 1)
        dsmall = jnp.zeros_like(g)
        for h in range(DN_HEADS):
            dsmall = jnp.where(lane == h, rows[1][:, h * HD:h * HD + 1], dsmall)
            dsmall = jnp.where(lane == DN_HEADS + h, rows[2][:, h * HD:h * HD + 1], dsmall)
        is_a = lane < DN_HEADS
        da = jnp.where(is_a, dsmall * (-jnp.exp(consts[0])) * _sigmoid(z), 0.0)
        db = jnp.where((lane >= DN_HEADS) & (lane < 2 * DN_HEADS), dsmall * beta * (1.0 - beta), 0.0)
        return [da + db], [_csum(jnp.where(is_a, dsmall * g, 0.0)), _csum(da)]

    return rowwise(name, fn, length, min(tl, length), rows=[ab, dgb, dbb], consts=[alog, dtb],
                   out_rows=[(128, BF16)], out_accs=[((1, 128), F32)] * 2)


CPS = 4


def _chunk_scan_rows(x, suffix=False):
    n = x.shape[0]
    rc = _iota(x.shape, 0) & (CHUNK - 1)
    sh = 1
    while sh < CHUNK:
        if suffix:
            x = x + jnp.where(rc < CHUNK - sh, pltpu.roll(x, n - sh, 0), 0.0)
        else:
            x = x + jnp.where(rc >= sh, pltpu.roll(x, sh, 0), 0.0)
        sh *= 2
    return x


def _tri_inv(a_list, eye, bd):
    def each(f, *ls):
        return [f(*xs) for xs in zip(*ls)]

    dg = [jnp.where(bd, a, 0.0) for a in a_list]
    lo = each(lambda a, d: a - d, a_list, dg)
    n1 = [-d for d in dg]
    n2 = each(lambda n: dnn(n, n, X3), n1)
    n4 = each(lambda n: dnn(n, n, X3), n2)
    td = each(lambda p, s: dnn(eye + p, eye + s, X3), n1, n2)
    n8 = each(lambda n: dnn(n, n, X3), n4)
    td = each(lambda t, n: dnn(t, eye + n, X3), td, n4)
    td = each(lambda t, n: dnn(t, eye + n, X3), td, n8)
    m = each(lambda t, l: dnn(t, l, X3), td, lo)
    m2 = each(lambda x: dnn(x, x, X3), m)
    x = each(lambda p, s: dnn(eye - p, eye + s, X3), m, m2)
    return each(lambda p, t: dnn(p, t, X3), x, td)


def _chunk_common(q, k, v, gcb, bb):
    egb = jnp.exp(gcb)
    gc64 = gcb[:, :CHUNK]
    ii, jj = _iota((CHUNK, CHUNK), 0), _iota((CHUNK, CHUNK), 1)
    incl, strict = ii >= jj, ii > jj
    decay = jnp.exp(jnp.where(incl, gc64 - gc64.T, -jnp.inf))
    kb = k * bb
    vb = v * bb
    kbe = kb * egb
    pq = dnt(jnp.concatenate([kb, q], axis=0), k, X3)
    ekb = jnp.exp(gcb[CHUNK - 1:CHUNK, :] - gcb)
    return dict(egb=egb, decay=decay, kb=kb, vb=vb, kbe=kbe, pm=pq[:CHUNK], qm=pq[CHUNK:], ekb=ekb,
                incl=incl, strict=strict, ii=ii, jj=jj)


def _chunk_head(vals, ci, h):
    return [v[ci * CHUNK:(ci + 1) * CHUNK, h * HD:(h + 1) * HD] for v in vals]


def _assemble(per_chunk):
    return jnp.concatenate([_cat(hs) for hs in per_chunk], axis=0)


def _assemble3(per_chunk):
    return jnp.stack([jnp.concatenate([per_chunk[ci][h] for ci in range(CPS)], axis=0) for h in range(DN_HEADS)])


def delta_prep_fwd(name, q, k, v, gb, bb):
    length = q.shape[0]

    def fn(ctx, rows, consts, prevs, nexts):
        gcb_all = _chunk_scan_rows(rows[3])
        vals = [rows[0], rows[1], rows[2], gcb_all, rows[4]]
        units = [(ci, h) for ci in range(CPS) for h in range(DN_HEADS)]
        ins = [_chunk_head(vals, ci, h) for ci, h in units]
        cs = [_chunk_common(*i) for i in ins]
        eye = (cs[0]["ii"] == cs[0]["jj"]).astype(F32)
        ts = _tri_inv([jnp.where(c["strict"], c["pm"] * c["decay"], 0.0) for c in cs], eye,
                      (cs[0]["ii"] >> 4) == (cs[0]["jj"] >> 4))
        uws = [dnn(t, _cat([c["vb"], c["kbe"]]), X3) for t, c in zip(ts, cs)]

        def grid2(xs):
            return [xs[ci * DN_HEADS:(ci + 1) * DN_HEADS] for ci in range(CPS)]

        return [_assemble(grid2([uw[:, :HD] for uw in uws])), _assemble(grid2([uw[:, HD:] for uw in uws])),
                _assemble(grid2([i[0] * c["egb"] for i, c in zip(ins, cs)])),
                _assemble(grid2([i[1] * c["ekb"] for i, c in zip(ins, cs)])), gcb_all,
                _assemble3(grid2([c["qm"] * c["decay"] for c in cs])), _assemble3(grid2(ts))], []

    return rowwise(name, fn, length, CHUNK * CPS, rows=[q, k, v, gb, bb],
                   out_rows=[(DN_WIDTH, F32)] * 5 + [(DN_HEADS, CHUNK, F32)] * 2)


def delta_prep_bwd(name, q, k, v, gb, bb, t3, du, dw, dqd, dkd, dattn3, dgl):
    length = q.shape[0]

    def fn(ctx, rows, consts, prevs, nexts):
        gcb_all = _chunk_scan_rows(rows[3])
        vals = [rows[0], rows[1], rows[2], gcb_all] + list(rows[4:9])
        t3v, da3v, dglv = rows[9], rows[10], rows[11]
        units = [(ci, h) for ci in range(CPS) for h in range(DN_HEADS)]
        ins = [_chunk_head(vals, ci, h) for ci, h in units]
        cs = [_chunk_common(*i[:5]) for i in ins]
        ts = [t3v[h][ci * CHUNK:(ci + 1) * CHUNK] for ci, h in units]
        dattns = [jnp.where(c["incl"], da3v[h][ci * CHUNK:(ci + 1) * CHUNK], 0.0) for (ci, h), c in zip(units, cs)]
        duws = [_cat([i[5], i[6]]) for i in ins]
        dvks = [dtn(t, d, X3) for t, d in zip(ts, duws)]
        dts = [dnt(d, _cat([c["vb"], c["kbe"]]), X3) for d, c in zip(duws, cs)]
        dts = [dnt(d, t, X3) for d, t in zip(dts, ts)]
        das = [jnp.where(c["strict"], -dtn(t, d, X3), 0.0) for c, t, d in zip(cs, ts, dts)]
        dpqs = [jnp.concatenate([da * c["decay"], dat * c["decay"]], axis=0) for da, dat, c in zip(das, dattns, cs)]
        dpqks = [dnn(d, i[1], X3) for d, i in zip(dpqs, ins)]
        dkps = [dtn(d, jnp.concatenate([c["kb"], i[0]], axis=0), X3) for d, c, i in zip(dpqs, cs, ins)]
        dqs, dks, dvs, dgcs, dbs = [], [], [], [], []
        for (ci, h), i, c, dvk, da, dattn, dpqk, dkp in zip(units, ins, cs, dvks, das, dattns, dpqks, dkps):
            qh, kh, vh, _, bh, _, _, dqdh, dkdh = i
            dvb, dkbe = dvk[:, :HD], dvk[:, HD:]
            dkb = dpqk[:CHUNK] + dkbe * c["egb"]
            c1 = _rsum(dkbe * c["kb"] + dqdh * qh) * c["egb"]
            c2 = _rsum(dkdh * kh) * c["ekb"]
            e = (da * c["pm"] + dattn * c["qm"]) * c["decay"]
            dgc = c1 - c2 + _rsum(e) - _rsum(e.T)
            dgl_tot = jnp.max(dglv[ci * 8:(ci + 1) * 8, h * HD:(h + 1) * HD], axis=0, keepdims=True) + _csum(c2)
            dgcs.append(dgc + jnp.where(_iota((CHUNK, HD), 0) == CHUNK - 1, dgl_tot, 0.0))
            dqs.append(dpqk[CHUNK:] + dqdh * c["egb"])
            dks.append(dkp + dkdh * c["ekb"] + dkb * bh)
            dvs.append(dvb * bh)
            dbs.append(jnp.broadcast_to(_rsum(dkb * kh + dvb * vh), (CHUNK, HD)))

        def grid2(xs):
            return [xs[ci * DN_HEADS:(ci + 1) * DN_HEADS] for ci in range(CPS)]

        return [_assemble(grid2(dqs)), _assemble(grid2(dks)), _assemble(grid2(dvs)),
                _chunk_scan_rows(_assemble(grid2(dgcs)), suffix=True), _assemble(grid2(dbs))], []

    return rowwise(name, fn, length, CHUNK * CPS,
                   rows=[q, k, v, gb, bb, du, dw, dqd, dkd, t3, dattn3, (dgl, DN_WIDTH, 0, 8 * CPS)],
                   out_rows=[(DN_WIDTH, F32)] * 5)


SCAN_CHUNKS = 8


def _scan_chunks(n):
    return SCAN_CHUNKS if n % SCAN_CHUNKS == 0 else 1


def delta_scan_fwd(name, qd, kd, u, w, attn3, gcb):
    length = qd.shape[0]
    n = length // CHUNK
    sc = _scan_chunks(n)
    row = pl.BlockSpec((sc * CHUNK, DN_WIDTH), lambda c: (c, 0))
    sq = pl.BlockSpec((DN_HEADS, sc * CHUNK, CHUNK), lambda c: (0, c, 0))

    def body(qd_ref, kd_ref, u_ref, w_ref, attn_ref, gc_ref, o_ref, vn_ref, st_ref, s_ref):
        c = pl.program_id(0)

        @pl.when(c == 0)
        def _():
            s_ref[...] = jnp.zeros_like(s_ref)

        heads = range(DN_HEADS)
        sls = [pl.ds(h * HD, HD) for h in heads]
        ss = [s_ref[h] for h in heads]
        for ci in range(sc):
            rs = pl.ds(ci * CHUNK, CHUNK)
            ws = [dnn(w_ref[rs, sl], s) for sl, s in zip(sls, ss)]
            qs = [dnn(qd_ref[rs, sl], s) for sl, s in zip(sls, ss)]
            vns = [u_ref[rs, sl] - x for sl, x in zip(sls, ws)]
            avs = [dnn(attn_ref[h, rs, :], vn) for h, vn in zip(heads, vns)]
            kvs = [dtn(kd_ref[rs, sl], vn) for sl, vn in zip(sls, vns)]
            for h, sl in zip(heads, sls):
                st_ref[ci, h] = ss[h]
                o_ref[rs, sl] = qs[h] + avs[h]
                vn_ref[rs, sl] = vns[h]
            ss = [s * jnp.exp(gc_ref[pl.ds(ci * CHUNK + CHUNK - 1, 1), sl]) + kv for s, sl, kv in zip(ss, sls, kvs)]
        for h in heads:
            s_ref[h] = ss[h]

    return pl.pallas_call(
        body, name=name, grid=(n // sc,), in_specs=[row, row, row, row, sq, row],
        out_specs=[row, row, pl.BlockSpec((sc, DN_HEADS, HD, HD), lambda c: (c, 0, 0, 0))],
        out_shape=[jax.ShapeDtypeStruct((length, DN_WIDTH), F32), jax.ShapeDtypeStruct((length, DN_WIDTH), F32),
                   jax.ShapeDtypeStruct((n, DN_HEADS, HD, HD), F32)],
        scratch_shapes=[pltpu.VMEM((DN_HEADS, HD, HD), F32)],
        compiler_params=_params(("arbitrary",)),
    )(qd, kd, u, w, attn3, gcb)


def delta_scan_bwd(name, do, qd, kd, w, attn3, vn, st, gcb):
    length = qd.shape[0]
    n = length // CHUNK
    sc = _scan_chunks(n)
    nb = n // sc
    row = pl.BlockSpec((sc * CHUNK, DN_WIDTH), lambda c: (nb - 1 - c, 0))
    sq = pl.BlockSpec((DN_HEADS, sc * CHUNK, CHUNK), lambda c: (0, nb - 1 - c, 0))
    stb = pl.BlockSpec((sc, DN_HEADS, HD, HD), lambda c: (nb - 1 - c, 0, 0, 0))
    glb = pl.BlockSpec((sc * 8, DN_WIDTH), lambda c: (nb - 1 - c, 0))

    def body(do_ref, qd_ref, kd_ref, w_ref, attn_ref, vn_ref, st_ref, gc_ref,
             dqd_ref, dkd_ref, du_ref, dw_ref, dattn_ref, dgl_ref, ds_ref):
        c = pl.program_id(0)

        @pl.when(c == 0)
        def _():
            ds_ref[...] = jnp.zeros_like(ds_ref)

        heads = range(DN_HEADS)
        sls = [pl.ds(h * HD, HD) for h in heads]
        dsns = [ds_ref[h] for h in heads]
        for ci in reversed(range(sc)):
            rs = pl.ds(ci * CHUNK, CHUNK)
            ss = [st_ref[ci, h] for h in heads]
            dos = [do_ref[rs, sl] for sl in sls]
            vns = [vn_ref[rs, sl] for sl in sls]
            dvns = [dtn(attn_ref[h, rs, :], d) for h, d in zip(heads, dos)]
            dvns = [x + dnn(kd_ref[rs, sl], dsn) for x, sl, dsn in zip(dvns, sls, dsns)]
            qdos = [dtn(qd_ref[rs, sl], d) for sl, d in zip(sls, dos)]
            for h, sl in zip(heads, sls):
                dattn_ref[h, rs, :] = dnt(dos[h], vns[h])
                dqd_ref[rs, sl] = dnt(dos[h], ss[h])
                dkd_ref[rs, sl] = dnt(vns[h], dsns[h])
                du_ref[rs, sl] = dvns[h]
            dws = [dnt(dvn, s) for dvn, s in zip(dvns, ss)]
            wdvs = [dtn(w_ref[rs, sl], dvn) for sl, dvn in zip(sls, dvns)]
            nxt = []
            for h, sl in zip(heads, sls):
                egl = jnp.exp(gc_ref[pl.ds(ci * CHUNK + CHUNK - 1, 1), sl])
                dw_ref[rs, sl] = -dws[h]
                dgl_ref[pl.ds(ci * 8, 8), sl] = jnp.broadcast_to(_csum(_rsum(dsns[h] * ss[h])) * egl, (8, HD))
                nxt.append(dsns[h] * egl + qdos[h] - wdvs[h])
            dsns = nxt
        for h in heads:
            ds_ref[h] = dsns[h]

    return pl.pallas_call(
        body, name=name, grid=(nb,), in_specs=[row, row, row, row, sq, row, stb, row],
        out_specs=[row, row, row, row, sq, glb],
        out_shape=[jax.ShapeDtypeStruct((length, DN_WIDTH), F32)] * 4
        + [jax.ShapeDtypeStruct((DN_HEADS, length, CHUNK), F32), jax.ShapeDtypeStruct((n * 8, DN_WIDTH), F32)],
        scratch_shapes=[pltpu.VMEM((DN_HEADS, HD, HD), F32)],
        compiler_params=_params(("arbitrary",)),
    )(do, qd, kd, w, attn3, vn, st, gcb)


def onorm_fwd(name, o, z, nw, tl=ROWS):
    length = o.shape[0]

    def fn(ctx, rows, consts, prevs, nexts):
        outs = []
        for oh, zh in zip(_heads(rows[0], DN_HEADS, HD), _heads(rows[1], DN_HEADS, HD)):
            r = lax.rsqrt(jnp.mean(oh * oh, axis=1, keepdims=True) + RMS_EPS)
            outs.append(oh * r * consts[0] * _silu(zh))
        return [_cat(outs)], []

    return rowwise(name, fn, length, min(tl, length), rows=[o, z], consts=[nw], out_rows=[(DN_WIDTH, BF16)])[0]


def onorm_bwd(name, o, z, d_on, nw, tl=ROWS):
    length = o.shape[0]

    def fn(ctx, rows, consts, prevs, nexts):
        dos, dzs = [], []
        dnw = jnp.zeros((1, HD), F32)
        for oh, zh, dh in zip(*[_heads(r, DN_HEADS, HD) for r in rows]):
            r = lax.rsqrt(jnp.mean(oh * oh, axis=1, keepdims=True) + RMS_EPS)
            y = oh * r
            sz = _silu(zh)
            t = dh * sz * consts[0]
            dos.append(r * (t - y * jnp.mean(t * y, axis=1, keepdims=True)))
            dzs.append(dh * y * consts[0] * _dsilu(zh))
            dnw = dnw + _csum(dh * y * sz)
        return [_cat(dos), _cat(dzs)], [dnw]

    return rowwise(name, fn, length, min(tl, length), rows=[o, z, d_on], consts=[nw],
                   out_rows=[(DN_WIDTH, F32), (DN_WIDTH, BF16)], out_accs=[((1, HD), F32)])


def merge_fwd(name, gates, ydn, ypool, tl=ROWS_WIDE):
    length = ydn.shape[0]

    def fn(ctx, rows, consts, prevs, nexts):
        gt = rows[0]
        return [_sigmoid(gt[:, :D_MODEL]) * rows[1] + _sigmoid(gt[:, D_MODEL:]) * rows[2]], []

    return rowwise(name, fn, length, min(tl, length), rows=[gates, ydn, ypool], out_rows=[(D_MODEL, BF16)])[0]


def merge_bwd(name, gates, ydn, ypool, dm, tl=ROWS_WIDE):
    length = ydn.shape[0]

    def fn(ctx, rows, consts, prevs, nexts):
        gt, yd, yp, d = rows
        sd, sp = _sigmoid(gt[:, :D_MODEL]), _sigmoid(gt[:, D_MODEL:])
        dgates = _cat([d * yd * sd * (1.0 - sd), d * yp * sp * (1.0 - sp)])
        return [d * sd, d * sp, dgates], []

    return rowwise(name, fn, length, min(tl, length), rows=[gates, ydn, ypool, dm],
                   out_rows=[(D_MODEL, BF16), (D_MODEL, BF16), (2 * D_MODEL, BF16)])


def _trailing_sums(ext, upto):
    s, sh = ext, 1
    while sh < upto:
        s = s + pltpu.roll(s, sh, 0)
        sh *= 2
    return s


def _leading_sums(ext, upto, n):
    s, sh = ext, 1
    while sh < upto:
        s = s + pltpu.roll(s, n - sh, 0)
        sh *= 2
    return s


def _pool_mixed(ctx, p, prev, tl):
    prevm = jnp.where(ctx.i > 0, prev, 0.0)
    t1 = (_row_index(ctx, tl) + 1).astype(F32)
    outs = []
    for gi, win in enumerate(POOL_WINDOWS):
        sl = slice(gi * HD, (gi + 1) * HD)
        ext = jnp.concatenate([prevm[:, sl], p[:, sl]], axis=0)
        mean = _trailing_sums(ext, win)[HALO:] / jnp.minimum(t1, float(win))
        outs.append(mean - p[:, sl])
    return outs


def pool_fwd(name, p, pool_w, scale, tl=ROWS):
    length = p.shape[0]
    tl = min(tl, length)

    def fn(ctx, rows, consts, prevs, nexts):
        mixed = _pool_mixed(ctx, rows[0], prevs[0], tl)
        y = _cat([dnn(m, consts[0][gi]) for gi, m in enumerate(mixed)])
        return [y * consts[1]], []

    return rowwise(name, fn, length, tl, rows=[p], consts=[pool_w, scale], prevs=[p],
                   out_rows=[(POOL_WIDTH, BF16)])[0]


def pool_bwd(name, p, dpo, pool_w, scale, tl=ROWS):
    length = p.shape[0]
    tl = min(tl, length)
    n = tl + HALO

    def fn(ctx, rows, consts, prevs, nexts):
        last = ctx.i == ctx.nblk - 1
        mixed = _pool_mixed(ctx, rows[0], prevs[0], tl)
        dext = jnp.concatenate([rows[1], jnp.where(last, 0.0, nexts[0])], axis=0)
        t1 = (_row_index(ctx, n) + 1).astype(F32)
        dps, dws, dscs = [], [], []
        for gi, win in enumerate(POOL_WINDOWS):
            sl = slice(gi * HD, (gi + 1) * HD)
            wg = consts[0][gi]
            dyraw = dext[:, sl] * consts[1][:, sl]
            dmix = dnt(dyraw, wg)
            dws.append(dtn(mixed[gi], dyraw[:tl]))
            dscs.append(_csum(rows[1][:, sl] * dnn(mixed[gi], wg)))
            lead = _leading_sums(dmix / jnp.minimum(t1, float(win)), win, n)
            dps.append(lead[:tl] - dmix[:tl])
        return [_cat(dps)], [jnp.stack(dws), _cat(dscs)]

    return rowwise(name, fn, length, tl, rows=[p, dpo], consts=[pool_w, scale], prevs=[p], nexts=[dpo],
                   out_rows=[(POOL_WIDTH, BF16)],
                   out_accs=[((len(POOL_WINDOWS), HD, HD), F32), ((1, POOL_WIDTH), F32)])


def _xa_probs(qh, kh):
    s = dnt(qh, kh) * (XA_HD ** -0.5)
    e = jnp.exp(s - jnp.max(s, axis=1, keepdims=True))
    return e / _rsum(e)


def xattn_fwd(name, qx, kx, vx, tl=ROWS):
    length = qx.shape[0]

    def fn(ctx, rows, consts, prevs, nexts):
        outs = [dnn(_xa_probs(qh, kh), vh) for qh, kh, vh in
                zip(_heads(rows[0], XA_HEADS, XA_HD), _heads(consts[0], XA_HEADS, XA_HD),
                    _heads(consts[1], XA_HEADS, XA_HD))]
        return [_cat(outs)], []

    return rowwise(name, fn, length, min(tl, length), rows=[qx], consts=[kx, vx], out_rows=[(D_MODEL, BF16)])[0]


def xattn_bwd(name, qx, dox, kx, vx, tl=ROWS):
    length = qx.shape[0]

    def fn(ctx, rows, consts, prevs, nexts):
        dqs, dks, dvs = [], [], []
        for qh, dh, kh, vh in zip(_heads(rows[0], XA_HEADS, XA_HD), _heads(rows[1], XA_HEADS, XA_HD),
                                  _heads(consts[0], XA_HEADS, XA_HD), _heads(consts[1], XA_HEADS, XA_HD)):
            pr = _xa_probs(qh, kh)
            dpr = dnt(dh, vh)
            ds = pr * (dpr - _rsum(dpr * pr)) * (XA_HD ** -0.5)
            dqs.append(dnn(ds, kh))
            dks.append(dtn(ds, qh))
            dvs.append(dtn(pr, dh))
        return [_cat(dqs)], [_cat(dks), _cat(dvs)]

    return rowwise(name, fn, length, min(tl, length), rows=[qx, dox], consts=[kx, vx],
                   out_rows=[(D_MODEL, BF16)], out_accs=[((N_MEM, D_MODEL), F32)] * 2)


def local_step(x, mem, target, w, io):
    alog = jnp.pad(w["a_log"], ((0, 0), (0, 128 - DN_HEADS)))
    dtb = jnp.pad(w["dt_bias"], ((0, 0), (0, 128 - DN_HEADS)))

    f1, res1, w_down1 = ffn_fwd("ffn1", x, w["ffn1_w_gate"], w["ffn1_w_up"], io.ffn1_down, deps=io.rest_started())
    x1, r1 = ln_fwd("ln1", [(ALPHA, x), (0.5, f1)], w["ln1_g"], w["ln1_b"], deps=io.halfway("mixer", f1))
    w = dict(w, ffn1_w_down=w_down1, **io.weights("mixer", x1))
    taps = [w["conv_w"][j:j + 1] for j in range(4)]

    pre = mm("in_qkv", x1, w["in_qkv"], tb=True)
    z = mm("in_z", x1, w["in_z"], tb=True)
    gates = mm("in_gates", x1, w["in_gates"], tb=True)
    p = mm("in_p", x1, w["in_p"], tb=True)
    ab = mm("in_ab", x1, w["in_ab"], tb=True)
    q, k, v = conv_fwd("conv", pre, taps, deps=io.halfway("xa", pre))
    gb, bb = gates_fwd("gates", ab, alog, dtb)
    u, wd_, qd, kd, gcb, attn3, t3 = delta_prep_fwd("dprep", q, k, v, gb, bb)
    o, vn, st = delta_scan_fwd("dscan", qd, kd, u, wd_, attn3, gcb)
    on = onorm_fwd("onorm", o, z, w["dn_norm_w"])
    ydn = mm("dn_branch", on, w["w_dn_branch"], tb=True)
    po = pool_fwd("pool", p, w["pool_w"], w["pool_scale"])
    ypool = mm("pool_branch", po, w["w_pool_branch"], tb=True)
    merged = merge_fwd("merge", gates, ydn, ypool)
    mix = mm("mix_out", merged, w["w_mix_out"])
    x2, r2 = ln_fwd("ln2", [(ALPHA, x1), (1.0, mix)], w["ln2_g"], w["ln2_b"])

    w = dict(w, **io.weights("xa", x2))
    m, _ = ln_fwd("ln_mem", [(1.0, mem)], w["mem_ln_g"], w["mem_ln_b"])
    qx = mm("xa_q", x2, w["xa_wq"], deps=io.halfway("ffn2", x2))
    kx = mm("xa_k", m, w["xa_wk"])
    vx = mm("xa_v", m, w["xa_wv"])
    ox = xattn_fwd("xattn", qx, kx, vx)
    xa = mm("xa_o", ox, w["xa_wo"])
    x3, r3 = ln_fwd("ln3", [(ALPHA, x2), (1.0, xa)], w["ln3_g"], w["ln3_b"])
    w = dict(w, **io.weights("ffn2", x3))

    f2, res2, _ = ffn_fwd("ffn2", x3, w["ffn2_w_gate"], w["ffn2_w_up"], w["ffn2_w_down"])
    dy4, r4, loss = ln_loss("ln4_loss", [(ALPHA, x3), (0.5, f2)], w["ln4_g"], w["ln4_b"], target)

    g = {}
    dr4, g["ln4_g"], g["ln4_b"] = ln_bwd("ln4_b", r4, [(1.0, dy4)], w["ln4_g"])
    dx3, g["ffn2_w_gate"], g["ffn2_w_up"], g["ffn2_w_down"] = ffn_bwd(
        "ffn2b", x3, res2, dr4, w["ffn2_w_gate"], w["ffn2_w_up"], w["ffn2_w_down"])
    dep = io.grads_out("ffn2", g)
    dr3, g["ln3_g"], g["ln3_b"] = ln_bwd("ln3_b", r3, [(ALPHA, dr4), (1.0, dx3)], w["ln3_g"], deps=dep)

    dox = mm("xa_do", dr3, w["xa_wo"], tb=True)
    g["xa_wo"] = mm("xa_dwo", ox, dr3, ta=True)
    dqx, dkx, dvx = xattn_bwd("xattn_b", qx, dox, kx, vx)
    g["xa_wq"] = mm("xa_dwq", x2, dqx, ta=True)
    dx2 = mm("xa_dx", dqx, w["xa_wq"], tb=True)
    g["xa_wk"] = mm("xa_dwk", m, dkx, ta=True)
    g["xa_wv"] = mm("xa_dwv", m, dvx, ta=True)
    dmm = mm("xa_dmk", dkx, w["xa_wk"], tb=True, deps=io.grads_out("xa", g))
    dmm = mm("xa_dmv", dvx, w["xa_wv"], tb=True, add=dmm)
    _, g["mem_ln_g"], g["mem_ln_b"] = ln_bwd("ln_mem_b", mem, [(1.0, dmm)], w["mem_ln_g"])
    dr2, g["ln2_g"], g["ln2_b"] = ln_bwd("ln2_b", r2, [(ALPHA, dr3), (1.0, dx2)], w["ln2_g"])
    io.grads_in("ffn2", dr2)

    dmerged = mm("mix_dm", dr2, w["w_mix_out"], tb=True)
    g["w_mix_out"] = mm("mix_dw", merged, dr2, ta=True)
    d_ydn, d_ypool, d_gates = merge_bwd("merge_b", gates, ydn, ypool, dmerged)
    g["w_dn_branch"] = mm("dn_dw", d_ydn, on, ta=True)
    d_on = mm("dn_dx", d_ydn, w["w_dn_branch"])
    g["w_pool_branch"] = mm("pool_dw", d_ypool, po, ta=True)
    d_po = mm("pool_dx", d_ypool, w["w_pool_branch"])
    dp, g["pool_w"], g["pool_scale"] = pool_bwd("pool_b", p, d_po, w["pool_w"], w["pool_scale"])
    d_o, dz, g["dn_norm_w"] = onorm_bwd("onorm_b", o, z, d_on, w["dn_norm_w"])
    dqd, dkd, du, dw_, dattn3, dgl = delta_scan_bwd("dscan_b", d_o, qd, kd, wd_, attn3, vn, st, gcb)
    dq, dk, dv, dgb, dbb = delta_prep_bwd("dprep_b", q, k, v, gb, bb, t3, du, dw_, dqd, dkd, dattn3, dgl)
    dpre, dc0, dc1, dc2, dc3 = conv_bwd("conv_b", pre, dq, dk, dv, taps)
    g["conv_w"] = jnp.concatenate([dc0, dc1, dc2, dc3], axis=0)
    d_ab, dalog, ddtb = gates_bwd("gates_b", ab, dgb, dbb, alog, dtb)
    g["a_log"] = dalog[:, :DN_HEADS]
    g["dt_bias"] = ddtb[:, :DN_HEADS]
    g["in_qkv"] = mm("in_dwqkv", dpre, x1, ta=True)
    g["in_z"] = mm("in_dwz", dz, x1, ta=True)
    g["in_gates"] = mm("in_dwgates", d_gates, x1, ta=True)
    g["in_p"] = mm("in_dwp", dp, x1, ta=True)
    g["in_ab"] = mm("in_dwab", d_ab, x1, ta=True)
    io.grads_in("xa", g["in_ab"])
    dx1 = mm_sum("in_dx", [(dpre, w["in_qkv"]), (dz, w["in_z"]), (d_gates, w["in_gates"]), (dp, w["in_p"]),
                           (d_ab, w["in_ab"])], deps=io.grads_out("mixer", g))
    dr1, g["ln1_g"], g["ln1_b"] = ln_bwd("ln1_b", r1, [(ALPHA, dr2), (1.0, dx1)], w["ln1_g"])

    def on_dw(which, dw):
        name = "ffn1_w_" + which
        small = io.small_out(dict(g, loss=loss[0, :1])) if which == "down" else ()
        return small + io.grads_out(name, {name: dw})

    grad_x, g["ffn1_w_gate"], g["ffn1_w_up"], g["ffn1_w_down"] = ffn_bwd(
        "ffn1b", x, res1, dr1, w["ffn1_w_gate"], w["ffn1_w_up"], w["ffn1_w_down"], on_dw=on_dw, also=(ALPHA, dr1))
    return loss, grad_x, g


WEIGHT_NAMES = ['ffn1_w_gate', 'ffn1_w_up', 'ffn1_w_down', 'ln1_g', 'ln1_b', 'w_in', 'conv_w', 'a_log', 'dt_bias',
                'dn_norm_w', 'w_dn_branch', 'pool_w', 'pool_scale', 'w_pool_branch', 'w_mix_out', 'ln2_g', 'ln2_b',
                'mem_ln_g', 'mem_ln_b', 'xa_wq', 'xa_wk', 'xa_wv', 'xa_wo', 'ln3_g', 'ln3_b', 'ffn2_w_gate',
                'ffn2_w_up', 'ffn2_w_down', 'ln4_g', 'ln4_b']
SHARDED = [
    ("ffn1_w_gate", "cols", (1024, 352)), ("ffn1_w_up", "cols", (1024, 352)), ("ffn1_w_down", "rows", (352, 1024)),
    ("w_in", "cols", (1024, 577)), ("conv_w", "flat", (4, 192)), ("w_dn_branch", "cols", (512, 128)),
    ("w_pool_branch", "cols", (512, 128)), ("w_mix_out", "rows", (128, 1024)), ("xa_wq", "rows", (128, 1024)),
    ("xa_wk", "rows", (128, 1024)), ("xa_wv", "rows", (128, 1024)), ("xa_wo", "rows", (128, 1024)),
    ("ffn2_w_gate", "cols", (1024, 352)), ("ffn2_w_up", "cols", (1024, 352)), ("ffn2_w_down", "rows", (352, 1024)),
]
REPLICATED = [n for n in WEIGHT_NAMES if n not in {s[0] for s in SHARDED}]
ROW_ALIGN = 16
ROW_BLOCKS = (512, 384, 352, 256, 192, 176, 128)
GROUPS = {"ffn1_gu": ("ffn1_w_gate", "ffn1_w_up"), "ffn1_d": ("ffn1_w_down",),
          "ffn1_w_gate": ("ffn1_w_gate",), "ffn1_w_up": ("ffn1_w_up",), "ffn1_w_down": ("ffn1_w_down",),
          "mixer": ("w_in", "conv_w", "w_dn_branch", "w_pool_branch", "w_mix_out"),
          "xa": ("xa_wq", "xa_wk", "xa_wv", "xa_wo"),
          "ffn2": ("ffn2_w_gate", "ffn2_w_up", "ffn2_w_down")}
W_IN_COLS = 577
W_IN_PIECES = (("in_qkv", 0, 1536), ("in_z", 1536, 2048), ("in_ab", 2048, 2056), ("in_p", 2056, 2568),
               ("in_gates", 2568, 4616))


def _round_up(n, m):
    return -(-n // m) * m


def _layout():
    off, table = 0, {}
    for name, form, shape in SHARDED:
        valid = {"rows": shape[0], "cols": shape[1], "flat": 2}[form]
        width = {"rows": shape[1], "cols": shape[0], "flat": shape[0] * shape[1]}[form]
        rows = _round_up(valid, ROW_ALIGN)
        table[name] = (off, rows, valid, width, form, shape)
        off += rows
    return table


LAYOUT = _layout()


def _group_span(names):
    base = LAYOUT[names[0]][0]
    rows = LAYOUT[names[-1]][0] + LAYOUT[names[-1]][1] - base
    while not any(rows % b == 0 for b in ROW_BLOCKS):
        rows += ROW_ALIGN
    return base, rows


def _row_block(rows):
    return _pick(rows, ROW_BLOCKS)


def _pad_block(blk, rows):
    return jnp.pad(blk, ((0, rows - blk.shape[0]), (0, LANES - blk.shape[1])))


def pack_weight_shards(shards, names):
    parts, used = [], 0
    for name in names:
        off, rows, valid, width, form, _ = LAYOUT[name]
        s = shards[name]
        if form == "flat":
            flat = s.reshape(1, -1)
            hi = flat.astype(BF16)
            blk = jnp.concatenate([hi, (flat - hi.astype(F32)).astype(BF16)], axis=0)
        else:
            blk = (s.T if form == "cols" else s).astype(BF16)
        parts.append(_pad_block(blk, rows))
        used += rows
    if _group_span(names)[1] > used:
        parts.append(jnp.zeros((_group_span(names)[1] - used, LANES), BF16))
    return jnp.concatenate(parts, axis=0)


IN_AB_ROWS = 128


def _w_in_segments(first, last):
    segs = []
    for k in range(N_DEV):
        lo, hi = max(first, k * W_IN_COLS), min(last, (k + 1) * W_IN_COLS)
        if lo < hi:
            segs.append((k, lo - k * W_IN_COLS, lo - first, hi - lo))
    return segs


def w_in_pieces(name, gathered, off, rows):
    assert off % rows == 0
    sizes = [IN_AB_ROWS if piece == "in_ab" else last - first for piece, first, last in W_IN_PIECES]

    def body(src_ref, *outs):
        for o_ref, (piece, first, last) in zip(outs, W_IN_PIECES):
            if piece == "in_ab":
                o_ref[...] = jnp.zeros_like(o_ref)
            for k, src, dst, count in _w_in_segments(first, last):
                o_ref[pl.ds(dst, count), :] = src_ref[k, pl.ds(src, count), :]

    outs = pl.pallas_call(
        body, name=name, grid=(1,), in_specs=[pl.BlockSpec((N_DEV, rows, LANES), lambda i: (0, off // rows, 0))],
        out_specs=[pl.BlockSpec((n, LANES), lambda i: (0, 0)) for n in sizes],
        out_shape=[jax.ShapeDtypeStruct((n, LANES), gathered.dtype) for n in sizes],
        compiler_params=_params(("arbitrary",)),
    )(gathered)
    return {piece: o for (piece, _, _), o in zip(W_IN_PIECES, outs)}


def unpack_full_weights(gathered, names):
    out, base = {}, _group_span(names)[0]
    for name in names:
        off, rows, valid, width, form, shape = LAYOUT[name]
        seg = gathered[:, off - base:off - base + rows]
        if form == "flat":
            flat = seg[:, 0, :width].astype(F32) + seg[:, 1, :width].astype(F32)
            out[name] = flat.reshape((N_DEV,) + shape).transpose(1, 0, 2).reshape(shape[0], N_DEV * shape[1])
        elif name == "w_in":
            out.update(w_in_pieces("w_in_pieces", gathered, off - base, rows))
        else:
            out[name] = seg[:, :valid, :width].reshape(N_DEV * valid, width)
    return out


def pack_full_grads(grads, names, me):
    wire, own, used = [], [], 0
    for name in names:
        off, rows, valid, width, form, shape = LAYOUT[name]
        if form == "flat":
            full = grads[name].reshape(shape[0], N_DEV, shape[1]).transpose(1, 0, 2).reshape(N_DEV, 1, width)
        elif name == "w_in":
            full = jnp.concatenate([grads[piece][:last - first] for piece, first, last in W_IN_PIECES], axis=0)
            full = full.reshape(N_DEV, valid, width)
        else:
            full = grads[name].reshape(N_DEV, valid, width)
        pad = ((0, rows - full.shape[1]), (0, LANES - width))
        wire.append(jnp.pad(full.astype(WIRE), ((0, 0),) + pad))
        own.append(jnp.pad(lax.dynamic_index_in_dim(full, me, 0, keepdims=False), pad))
        used += rows
    if _group_span(names)[1] > used:
        wire.append(jnp.zeros((N_DEV, _group_span(names)[1] - used, LANES), WIRE))
        own.append(jnp.zeros((_group_span(names)[1] - used, LANES), F32))
    return jnp.concatenate(wire, axis=1), jnp.concatenate(own, axis=0)


TRANSPOSED = ("ffn1_w_gate", "ffn1_w_up", "ffn2_w_gate", "ffn2_w_up", "w_in")


def unpack_grad_shards(packed, names):
    out, base = {}, _group_span(names)[0]
    for name in names:
        off, rows, valid, width, form, shape = LAYOUT[name]
        off -= base
        if form == "flat":
            out[name] = packed[off, :width].reshape(shape)
        elif name in TRANSPOSED:
            out[name] = packed[off:off + valid, :width]
        elif form == "cols":
            out[name] = packed[off:off + valid, :width].T
        else:
            out[name] = packed[off:off + valid, :width]
    return out


SMALL_SHAPES = {n: (1024,) for n in REPLICATED}
SMALL_SHAPES.update(pool_w=(4, 128, 128), pool_scale=(512,), dn_norm_w=(128,), a_log=(4,), dt_bias=(4,))


SMALL_SHAPES["loss"] = (1,)
SMALL_NAMES = REPLICATED + ["loss"]


def _small_layout():
    off, table = 0, {}
    for name in SMALL_NAMES:
        numel = 1
        for d in SMALL_SHAPES[name]:
            numel *= d
        rows = _round_up(-(-numel // LANES), 8)
        table[name] = (off, rows, numel)
        off += rows
    return table, off


SMALL_LAYOUT, SMALL_ROWS = _small_layout()


def _to_rows(flat, rows):
    return jnp.pad(flat, (0, rows * LANES - flat.shape[0])).reshape(rows, LANES)


def pack_small(values):
    return jnp.concatenate([_to_rows(values[name].reshape(-1), SMALL_LAYOUT[name][1]) for name in SMALL_NAMES], axis=0)


def unpack_small(packed):
    out = {}
    for name in SMALL_NAMES:
        off, rows, numel = SMALL_LAYOUT[name]
        out[name] = packed[off:off + rows].reshape(-1)[:numel].reshape(SMALL_SHAPES[name])
    return out


MESH = pl.DeviceIdType.MESH


def _position():
    return lax.axis_index("x"), lax.axis_index("y"), lax.axis_index("c")


def _other_chips(x, y):
    return [(1 - x, y), (x, 1 - y), (1 - x, 1 - y)]


def all_gather(name, block):
    rows, n = block.shape

    def body(x_ref, out_ref, send_sems, recv_sems, local_sem):
        x, y, c = _position()
        me, sibling = (x, y, c), (x, y, 1 - c)
        chips = _other_chips(x, y)

        def slot(px, py, pc):
            return out_ref.at[4 * px + 2 * py + pc]

        def copy(k, blk, to, src=None):
            return pltpu.make_async_remote_copy(
                src_ref=slot(*blk) if src is None else src, dst_ref=slot(*blk),
                send_sem=send_sems.at[k], recv_sem=recv_sems.at[k], device_id=to, device_id_type=MESH)

        mine = pltpu.make_async_copy(x_ref, slot(*me), local_sem)
        mine.start()
        first = [copy(0, me, sibling, src=x_ref)]
        first += [copy(1 + j, me, (*chip, c), src=x_ref) for j, chip in enumerate(chips)]
        for cp in first:
            cp.start()
        passed = [copy(4 + j, (*chip, c), sibling) for j, chip in enumerate(chips)]
        for j, chip in enumerate(chips):
            copy(1 + j, (*chip, c), me).wait_recv()
            passed[j].start()
        copy(0, sibling, me).wait_recv()
        for j, chip in enumerate(chips):
            copy(4 + j, (*chip, 1 - c), me).wait_recv()
        for cp in first + passed:
            cp.wait_send()
        mine.wait()

    return pl.pallas_call(
        body, name=name, out_shape=jax.ShapeDtypeStruct((N_DEV, rows, n), block.dtype),
        in_specs=[ANY], out_specs=ANY,
        scratch_shapes=[pltpu.SemaphoreType.DMA((7,)), pltpu.SemaphoreType.DMA((7,)), pltpu.SemaphoreType.DMA(())],
    )(block)


HBM = pl.BlockSpec(memory_space=pltpu.HBM)
SEM = pl.BlockSpec(memory_space=pltpu.SEMAPHORE)
EFFECT = pltpu.SideEffectType.DATAFLOW_SIDE_EFFECTING


def _remote(src, dst, send_sem, recv_sem, to):
    return pltpu.make_async_remote_copy(src_ref=src, dst_ref=dst, send_sem=send_sem, recv_sem=recv_sem,
                                        device_id=to, device_id_type=MESH)


def split_start(name, bufs, n, make_copies):
    nb = len(bufs)

    def body(*refs):
        for out_cp, _ in make_copies(refs[:nb], refs[nb:nb + n], refs[nb + n:nb + 2 * n]):
            out_cp.start()
        refs[-1][...] = jnp.zeros_like(refs[-1])

    outs = pl.pallas_call(
        body, name=name,
        out_shape=tuple([pltpu.SemaphoreType.DMA(())] * (2 * n)) + tuple(pltpu.HBM(b.shape, b.dtype) for b in bufs)
        + (jax.ShapeDtypeStruct((8, 128), F32),),
        in_specs=[HBM] * nb,
        out_specs=tuple([SEM] * (2 * n) + [HBM] * nb + [pl.BlockSpec(memory_space=pltpu.VMEM)]),
        input_output_aliases={i: 2 * n + i for i in range(nb)},
        compiler_params=pltpu.CompilerParams(has_side_effects=EFFECT),
    )(*[pltpu.with_memory_space_constraint(b, pltpu.HBM) for b in bufs])
    return list(outs[:2 * n]), list(outs[2 * n:2 * n + nb]), outs[-1]


def split_wait(name, bufs, sems, n, make_copies, after):
    nb = len(bufs)

    def body(*refs):
        for out_cp, in_cp in make_copies(refs[:nb], refs[nb:nb + n], refs[nb + n:nb + 2 * n]):
            out_cp.wait_send()
            in_cp.wait_recv()

    outs = pl.pallas_call(
        body, name=name, out_shape=tuple(pltpu.HBM(b.shape, b.dtype) for b in bufs),
        in_specs=[HBM] * nb + [SEM] * (2 * n) + [ANY], out_specs=tuple([HBM] * nb),
        input_output_aliases={i: i for i in range(nb)},
        compiler_params=pltpu.CompilerParams(has_side_effects=EFFECT),
    )(*bufs, *sems, after)
    return list(outs)


def _gather_stage1(refs, send, recv):
    src, land = refs
    x, y, c = _position()
    peers = [(x, y, 1 - c)] + [(*chip, c) for chip in _other_chips(x, y)]
    return [(_remote(src, land.at[4 * x + 2 * y + c], send[k], recv[k], p),
             _remote(src, land.at[4 * p[0] + 2 * p[1] + p[2]], send[k], recv[k], p)) for k, p in enumerate(peers)]


def _gather_stage2(refs, send, recv):
    (land,) = refs
    x, y, c = _position()
    out = []
    for j, (px, py) in enumerate(_other_chips(x, y)):
        mine, theirs = land.at[4 * px + 2 * py + c], land.at[4 * px + 2 * py + 1 - c]
        out.append((_remote(mine, mine, send[j], recv[j], (x, y, 1 - c)),
                    _remote(theirs, theirs, send[j], recv[j], (x, y, 1 - c))))
    return out


def _flips():
    return [(a, b, d) for a in (0, 1) for b in (0, 1) for d in (0, 1) if a | b | d]


def _gather_direct(refs, send, recv):
    src, land = refs
    x, y, c = _position()
    out = []
    for k, (fx, fy, fc) in enumerate(_flips()):
        p = (1 - x if fx else x, 1 - y if fy else y, 1 - c if fc else c)
        out.append((_remote(src, land.at[4 * x + 2 * y + c], send[k], recv[k], p),
                    _remote(src, land.at[4 * p[0] + 2 * p[1] + p[2]], send[k], recv[k], p)))
    return out


def _scatter_direct(refs, send, recv):
    sendbuf, land = refs
    x, y, c = _position()
    out = []
    for k, (fx, fy, fc) in enumerate(_flips()):
        p = (1 - x if fx else x, 1 - y if fy else y, 1 - c if fc else c)
        cp = _remote(sendbuf.at[4 * p[0] + 2 * p[1] + p[2]], land.at[k], send[k], recv[k], p)
        out.append((cp, cp))
    return out


def _own_plus_slots(name, own, landed):
    n, rows, _ = landed.shape
    tr = _row_block(rows)

    def body(g_ref, l_ref, o_ref):
        acc = g_ref[...]
        for j in range(n):
            acc = acc + l_ref[j].astype(F32)
        o_ref[...] = acc

    return pl.pallas_call(
        body, name=name, grid=(rows // tr,),
        in_specs=[pl.BlockSpec((tr, LANES), lambda i: (i, 0)), pl.BlockSpec((n, tr, LANES), lambda i: (0, i, 0))],
        out_specs=pl.BlockSpec((tr, LANES), lambda i: (i, 0)),
        out_shape=jax.ShapeDtypeStruct((rows, LANES), F32), compiler_params=_params(("parallel",)),
    )(own, landed)


def _sum_slots(name, stack):
    n, rows, _ = stack.shape

    def body(s_ref, o_ref):
        acc = s_ref[0]
        for j in range(1, n):
            acc = acc + s_ref[j]
        o_ref[...] = acc

    return pl.pallas_call(
        body, name=name, in_specs=[pl.BlockSpec(stack.shape, lambda: (0, 0, 0))],
        out_specs=pl.BlockSpec((rows, LANES), lambda: (0, 0)), out_shape=jax.ShapeDtypeStruct((rows, LANES), F32),
    )(stack)


def adamw(name, w, g, m, v):
    shape = w.shape
    last = shape[-1]
    w2, g2, m2, v2 = [a.reshape(-1, last) for a in (w, g, m, v)]
    rows = w2.shape[0]
    tr = _pick(rows, (256, 176, 128))

    def body(w_ref, g_ref, m_ref, v_ref, d_ref, nm_ref, nv_ref):
        gg = g_ref[...]
        nm = ADAM_B1 * m_ref[...] + (1.0 - ADAM_B1) * gg
        nv = ADAM_B2 * v_ref[...] + (1.0 - ADAM_B2) * (gg * gg)
        m_hat = nm / (1.0 - ADAM_B1 ** ADAM_STEP)
        v_hat = nv / (1.0 - ADAM_B2 ** ADAM_STEP)
        d_ref[...] = -ADAM_LR * (m_hat / (jnp.sqrt(v_hat) + ADAM_EPS) + ADAM_WD * w_ref[...])
        nm_ref[...] = nm
        nv_ref[...] = nv

    spec = pl.BlockSpec((tr, last), lambda i: (i, 0))
    outs = pl.pallas_call(
        body, name=name, grid=(rows // tr,), in_specs=[spec] * 4, out_specs=[spec] * 3,
        out_shape=[jax.ShapeDtypeStruct((rows, last), F32)] * 3, compiler_params=_params(("parallel",)),
    )(w2, g2, m2, v2)
    return [o.reshape(shape) for o in outs]


def _landing(block_shape, dtype, own):
    x, y, c = _position()
    return lax.dynamic_update_slice(lax.empty((N_DEV,) + block_shape, dtype), own[None], (4 * x + 2 * y + c, 0, 0))


class _Exchanges:
    def __init__(self, shards):
        self.shards = shards
        self.pending = {}
        self.reduced = {}

    def first_weights(self):
        names = GROUPS["ffn1_gu"]
        return unpack_full_weights(all_gather("ag_ffn1_gu", pack_weight_shards(self.shards, names)), names)

    def rest_started(self):
        tokens = []
        block = pack_weight_shards(self.shards, GROUPS["ffn1_d"])
        sems, bufs, token = split_start("ag_ffn1_d_s", [block, _landing(block.shape, block.dtype, block)], N_DEV - 1,
                                        _gather_direct)
        self.pending["ffn1_d"] = (sems, bufs)
        tokens.append(token)
        for key in ("mixer", "xa", "ffn2"):
            block = pack_weight_shards(self.shards, GROUPS[key])
            sems, bufs, token = split_start(f"ag_{key}_s1", [block, _landing(block.shape, block.dtype, block)], 4,
                                            _gather_stage1)
            self.pending[key] = (sems, bufs)
            tokens.append(token)
        return tuple(tokens)

    def ffn1_down(self, after):
        sems, bufs = self.pending.pop("ffn1_d")
        _, gathered = split_wait("ag_ffn1_d_w", bufs, sems, N_DEV - 1, _gather_direct, after)
        return unpack_full_weights(gathered, GROUPS["ffn1_d"])["ffn1_w_down"]

    def halfway(self, key, after):
        sems, bufs = self.pending.pop(key)
        _, land = split_wait(f"ag_{key}_w1", bufs, sems, 4, _gather_stage1, after)
        sems, bufs, token = split_start(f"ag_{key}_s2", [land], 3, _gather_stage2)
        self.pending[key] = (sems, bufs)
        return (token,)

    def weights(self, key, after):
        sems, bufs = self.pending.pop(key)
        (gathered,) = split_wait(f"ag_{key}_w2", bufs, sems, 3, _gather_stage2, after)
        return unpack_full_weights(gathered, GROUPS[key])

    def grads_out(self, key, grads):
        x, y, c = _position()
        wire, own = pack_full_grads(grads, GROUPS[key], 4 * x + 2 * y + c)
        land = lax.empty((N_DEV - 1,) + wire.shape[1:], WIRE)
        sems, bufs, token = split_start(f"rs_{key}_start", [wire, land], N_DEV - 1, _scatter_direct)
        self.pending[key] = (sems, bufs, own)
        return (token,)

    def grads_in(self, key, after):
        sems, bufs, own = self.pending.pop(key)
        _, landed = split_wait(f"rs_{key}_wait", bufs, sems, N_DEV - 1, _scatter_direct, after)
        self.reduced.update(unpack_grad_shards(_own_plus_slots(f"rs_{key}_sum", own, landed), GROUPS[key]))

    def small_out(self, values):
        block = pack_small(values)
        sems, bufs, token = split_start("ag_small_s", [block, _landing(block.shape, block.dtype, block)], N_DEV - 1,
                                        _gather_direct)
        self.pending["small"] = (sems, bufs)
        return (token,)

    def small_in(self, after):
        sems, bufs = self.pending.pop("small")
        _, gathered = split_wait("ag_small_w", bufs, sems, N_DEV - 1, _gather_direct, after)
        return unpack_small(_sum_slots("small_sum", gathered))


def kernel(x, mem, ffn1_w_gate, ffn1_w_up, ffn1_w_down, ln1_g, ln1_b, w_in, conv_w, a_log, dt_bias, dn_norm_w, w_dn_branch, pool_w, pool_scale, w_pool_branch, w_mix_out, ln2_g, ln2_b, mem_ln_g, mem_ln_b, xa_wq, xa_wk, xa_wv, xa_wo, ln3_g, ln3_b, ffn2_w_gate, ffn2_w_up, ffn2_w_down, ln4_g, ln4_b, loss_target, m_ffn1_w_gate, m_ffn1_w_up, m_ffn1_w_down, m_ln1_g, m_ln1_b, m_w_in, m_conv_w, m_a_log, m_dt_bias, m_dn_norm_w, m_w_dn_branch, m_pool_w, m_pool_scale, m_w_pool_branch, m_w_mix_out, m_ln2_g, m_ln2_b, m_mem_ln_g, m_mem_ln_b, m_xa_wq, m_xa_wk, m_xa_wv, m_xa_wo, m_ln3_g, m_ln3_b, m_ffn2_w_gate, m_ffn2_w_up, m_ffn2_w_down, m_ln4_g, m_ln4_b, v_ffn1_w_gate, v_ffn1_w_up, v_ffn1_w_down, v_ln1_g, v_ln1_b, v_w_in, v_conv_w, v_a_log, v_dt_bias, v_dn_norm_w, v_w_dn_branch, v_pool_w, v_pool_scale, v_w_pool_branch, v_w_mix_out, v_ln2_g, v_ln2_b, v_mem_ln_g, v_mem_ln_b, v_xa_wq, v_xa_wk, v_xa_wv, v_xa_wo, v_ln3_g, v_ln3_b, v_ffn2_w_gate, v_ffn2_w_up, v_ffn2_w_down, v_ln4_g, v_ln4_b):
    given = dict(locals())
    shards = {n: given[n] for n in WEIGHT_NAMES}
    io = _Exchanges({n: shards[n][0] for n, _, _ in SHARDED})
    w = io.first_weights()
    for n in REPLICATED:
        w[n] = shards[n][0] if n == "pool_w" else shards[n]
    loss_part, grad_x, g = local_step(x[0], mem[0], loss_target[0], w, io)

    grad, updates = {}, {}

    def update(names, reduced):
        for n in names:
            if n in TRANSPOSED:
                outs = adamw("adamw_" + n, shards[n][0].T, reduced[n], given["m_" + n][0].T, given["v_" + n][0].T)
                grad[n], updates[n] = reduced[n].T[None], [o.T[None] for o in outs]
            else:
                grad[n] = reduced[n].reshape(shards[n].shape)
                updates[n] = adamw("adamw_" + n, shards[n], grad[n], given["m_" + n], given["v_" + n])
        return updates[names[-1]][0]

    update(GROUPS["ffn2"] + GROUPS["xa"], io.reduced)
    io.grads_in("mixer", grad_x)
    done = update(GROUPS["mixer"], io.reduced)
    small = io.small_in(done)
    loss = small.pop("loss")[0]
    done = update(REPLICATED, small)
    for n in ("ffn1_w_down", "ffn1_w_gate", "ffn1_w_up"):
        io.grads_in(n, done)
        done = update(GROUPS[n], io.reduced)
    return (loss, grad_x[None], *[grad[n] for n in WEIGHT_NAMES], *[updates[n][0] for n in WEIGHT_NAMES],
            *[updates[n][1] for n in WEIGHT_NAMES], *[updates[n][2] for n in WEIGHT_NAMES])
```

```python
import jax
import jax.numpy as jnp
from jax import lax
from jax.experimental import pallas as pl
from jax.experimental.pallas import tpu as pltpu

F32 = jnp.float32
BF16 = jnp.bfloat16
MMD = BF16
WIRE = BF16
X3 =lax.Precision.HIGH
VMEM_LIMIT_BYTES = 48 * 1024 * 1024

D_MODEL = 1024
D_FF = 2816
CHUNK = 64
N_MEM = 256
DN_HEADS = 4
HD = 128
DN_WIDTH = 512
POOL_WINDOWS = (2, 4, 8, 16)
POOL_WIDTH = 512
XA_HEADS = 4
XA_HD = 256
LN_EPS = 1e-5
RMS_EPS = 1e-6
L2_EPS = 1e-6
ALPHA = 2.0 ** 0.25
HALO = 16
ROWS = 512
ROWS_WIDE = 256

ADAM_LR = 0.001
ADAM_B1 = 0.9
ADAM_B2 = 0.999
ADAM_EPS = 1e-08
ADAM_WD = 0.01
ADAM_STEP = 10

N_DEV = 8
LANES = 1024
ANY = pl.BlockSpec(memory_space=pl.ANY)


def _dot(a, b, ca, cb, prec):
    dn = (((ca,), (cb,)), ((), ()))
    if prec is not None:
        return lax.dot_general(a.astype(F32), b.astype(F32), dn, precision=prec, preferred_element_type=F32)
    return lax.dot_general(a.astype(MMD), b.astype(MMD), dn, preferred_element_type=F32)


def dnn(a, b, prec=None):
    return _dot(a, b, 1, 0, prec)


def dnt(a, b, prec=None):
    return _dot(a, b, 1, 1, prec)


def dtn(a, b, prec=None):
    return _dot(a, b, 0, 0, prec)


def _sigmoid(x):
    return jax.nn.sigmoid(x)


def _silu(x):
    return x * _sigmoid(x)


def _dsilu(x):
    s = _sigmoid(x)
    return s * (1.0 + x * (1.0 - s))


def _softplus(x):
    return jnp.maximum(x, 0.0) + jnp.log1p(jnp.exp(-jnp.abs(x)))


def _iota(shape, dim):
    return lax.broadcasted_iota(jnp.int32, shape, dim)


def _rsum(x):
    return jnp.sum(x, axis=1, keepdims=True)


def _csum(x):
    return jnp.sum(x, axis=0, keepdims=True)


def _pick(n, cands):
    for c in cands:
        if n % c == 0:
            return c
    return n


def _params(sem):
    return pltpu.CompilerParams(dimension_semantics=sem, vmem_limit_bytes=VMEM_LIMIT_BYTES)


MM_TILE_SIZES = (4096, 2816, 2048, 1536, 1408, 1024, 768, 512, 384, 256, 128)
MM_VMEM_BUDGET = 36 * 1024 * 1024
HBM_BYTES_PER_US = 3.0e6
GRID_STEP_US = 0.35


def _mm_tiles(m, n, kc, a_bytes, b_bytes, o_bytes):
    def sizes(d):
        return [d] if d <= 512 else [t for t in MM_TILE_SIZES if d % t == 0]

    best = None
    for tm in sizes(m):
        for tn in sizes(n):
            for tk in sizes(kc):
                vmem = 2 * (tm * tk * a_bytes + tk * tn * b_bytes + tm * tn * o_bytes) + tm * tn * 4
                if vmem > MM_VMEM_BUDGET:
                    continue
                steps = (m // tm) * (n // tn) * (kc // tk)
                traffic = m * kc * a_bytes * (n // tn) + kc * n * b_bytes * (m // tm) + m * n * o_bytes
                edge = tm * tk * a_bytes + tk * tn * b_bytes + tm * tn * o_bytes
                cost = (traffic + edge) / HBM_BYTES_PER_US + steps * GRID_STEP_US
                if best is None or cost < best[0]:
                    best = (cost, tm, tn, tk)
    return best[1:]


def mm(name, a, b, *, ta=False, tb=False, out_dtype=F32, add=None, scale=None, deps=()):
    adds = [] if add is None else (list(add) if isinstance(add, (list, tuple)) else [(1.0, add)])
    if ta:
        kc, m = a.shape
    else:
        m, kc = a.shape
    if tb:
        n, kb = b.shape
    else:
        kb, n = b.shape
    assert kc == kb, (name, a.shape, b.shape)
    tm, tn, tk = _mm_tiles(m, n, kc, a.dtype.itemsize, b.dtype.itemsize,
                           jnp.dtype(out_dtype).itemsize * (1 + len(adds)))
    nk = kc // tk
    grid = (m // tm, n // tn, nk)
    a_spec = pl.BlockSpec((tk, tm), lambda i, j, k: (k, i)) if ta else pl.BlockSpec((tm, tk), lambda i, j, k: (i, k))
    b_spec = pl.BlockSpec((tn, tk), lambda i, j, k: (j, k)) if tb else pl.BlockSpec((tk, tn), lambda i, j, k: (k, j))
    o_spec = pl.BlockSpec((tm, tn), lambda i, j, k: (i, j))
    ca, cb = (0 if ta else 1), (1 if tb else 0)

    def body(*refs):
        a_ref, b_ref = refs[0], refs[1]
        o_ref = refs[-1] if nk == 1 else refs[-2]
        k = pl.program_id(2)
        part = _dot(a_ref[...], b_ref[...], ca, cb, None)

        def finish(r):
            if scale is not None:
                r = r * scale
            for (coef, _), add_ref in zip(adds, refs[2:2 + len(adds)]):
                r = r + (add_ref[...] if coef == 1.0 else coef * add_ref[...])
            o_ref[...] = r.astype(o_ref.dtype)

        if nk == 1:
            finish(part)
            return
        acc_ref = refs[-1]

        @pl.when(k == 0)
        def _():
            acc_ref[...] = part

        if nk > 2:
            @pl.when((k > 0) & (k < nk - 1))
            def _():
                acc_ref[...] += part

        @pl.when(k == nk - 1)
        def _():
            finish(acc_ref[...] + part)

    ins = [a, b] + [t for _, t in adds] + list(deps)
    specs = [a_spec, b_spec] + [o_spec] * len(adds) + [ANY] * len(deps)
    return pl.pallas_call(
        body, name=name, grid=grid, in_specs=specs, out_specs=o_spec,
        out_shape=jax.ShapeDtypeStruct((m, n), out_dtype),
        scratch_shapes=[pltpu.VMEM((tm, tn), F32)] if nk > 1 else [],
        compiler_params=_params(("parallel", "parallel", "arbitrary")),
    )(*ins)


def mm_fan_t(name, lefts, b):
    kc, n = b.shape
    tk = min(512, kc)
    nk = kc // tk

    def body(*refs):
        k = pl.program_id(0)
        bb = refs[len(lefts)][...].astype(MMD)
        for a_ref, o_ref in zip(refs[:len(lefts)], refs[len(lefts) + 1:]):
            part = dtn(a_ref[...], bb)

            @pl.when(k == 0)
            def _(o_ref=o_ref, part=part):
                o_ref[...] = part

            @pl.when(k > 0)
            def _(o_ref=o_ref, part=part):
                o_ref[...] += part

    return pl.pallas_call(
        body, name=name, grid=(nk,),
        in_specs=[pl.BlockSpec((tk, a.shape[1]), lambda k: (k, 0)) for a in lefts] + [pl.BlockSpec((tk, n), lambda k: (k, 0))],
        out_specs=[pl.BlockSpec((a.shape[1], n), lambda k: (0, 0)) for a in lefts],
        out_shape=[jax.ShapeDtypeStruct((a.shape[1], n), F32) for a in lefts],
        compiler_params=_params(("arbitrary",)),
    )(*lefts, b)


def mm_sum(name, pairs, deps=()):
    m, n = pairs[0][0].shape[0], pairs[0][1].shape[1]
    tm = min(512, m)
    np_ = len(pairs)

    def body(*refs):
        acc = dnn(refs[0][...], refs[1][...])
        for p in range(1, np_):
            acc = acc + dnn(refs[2 * p][...], refs[2 * p + 1][...])
        refs[-1][...] = acc

    specs, ins = [], []
    for a, b in pairs:
        specs += [pl.BlockSpec((tm, a.shape[1]), lambda i: (i, 0)), pl.BlockSpec(b.shape, lambda i: (0, 0))]
        ins += [a, b]
    return pl.pallas_call(
        body, name=name, grid=(m // tm,), in_specs=specs + [ANY] * len(deps),
        out_specs=pl.BlockSpec((tm, n), lambda i: (i, 0)), out_shape=jax.ShapeDtypeStruct((m, n), F32),
        compiler_params=_params(("parallel",)),
    )(*ins, *deps)


class _Ctx:
    def __init__(self, i, nblk, tl):
        self.i, self.nblk, self.tl = i, nblk, tl


def _norm_item(it):
    if isinstance(it, tuple):
        a, w, j = it[:3]
        rows = it[3] if len(it) > 3 else None
        return a, w, j, rows
    return it, it.shape[-1], 0, None


def rowwise(name, fn, length, tl, *, rows=(), consts=(), prevs=(), nexts=(), out_rows=(), out_accs=(), deps=()):
    nblk = length // tl
    hb = tl // HALO
    nhalo = length // HALO
    arrays, specs = [], []
    for it in rows:
        a, w, j, r = _norm_item(it)
        if a.ndim == 3:
            specs.append(pl.BlockSpec((a.shape[0], tl, w), lambda i, j=j: (0, i, j)))
        else:
            specs.append(pl.BlockSpec((r or tl, w), lambda i, j=j: (i, j)))
        arrays.append(a)
    for a in consts:
        specs.append(pl.BlockSpec(a.shape, lambda i, nd=a.ndim: (0,) * nd))
        arrays.append(a)
    for it in prevs:
        a, w, j, _ = _norm_item(it)
        specs.append(pl.BlockSpec((HALO, w), lambda i, j=j: (jnp.maximum(i * hb - 1, 0), j)))
        arrays.append(a)
    for it in nexts:
        a, w, j, _ = _norm_item(it)
        specs.append(pl.BlockSpec((HALO, w), lambda i, j=j: (jnp.minimum((i + 1) * hb, nhalo - 1), j)))
        arrays.append(a)
    out_shape, out_specs = [], []
    for spec in out_rows:
        if len(spec) == 3:
            h, w, dt = spec
            out_shape.append(jax.ShapeDtypeStruct((h, length, w), dt))
            out_specs.append(pl.BlockSpec((h, tl, w), lambda i: (0, i, 0)))
        else:
            w, dt = spec
            out_shape.append(jax.ShapeDtypeStruct((length, w), dt))
            out_specs.append(pl.BlockSpec((tl, w), lambda i: (i, 0)))
    for shape, dt in out_accs:
        out_shape.append(jax.ShapeDtypeStruct(shape, dt))
        out_specs.append(pl.BlockSpec(shape, lambda i, nd=len(shape): (0,) * nd))
    n_r, n_c, n_p, n_n = len(rows), len(consts), len(prevs), len(nexts)
    n_in = n_r + n_c + n_p + n_n
    n_or = len(out_rows)
    arrays, specs = arrays + list(deps), specs + [ANY] * len(deps)

    def body(*refs):
        i = pl.program_id(0)
        vals = [r[...] for r in refs[:n_in]]
        outs = refs[n_in + len(deps):]
        ctx = _Ctx(i, nblk, tl)
        ro, ao = fn(ctx, vals[:n_r], vals[n_r:n_r + n_c], vals[n_r + n_c:n_r + n_c + n_p], vals[n_r + n_c + n_p:])
        for r, v in zip(outs[:n_or], ro, strict=True):
            r[...] = v.astype(r.dtype)
        for r, v in zip(outs[n_or:], ao, strict=True):
            @pl.when(i == 0)
            def _(r=r, v=v):
                r[...] = v.astype(r.dtype)

            @pl.when(i > 0)
            def _(r=r, v=v):
                r[...] += v.astype(r.dtype)

    res = pl.pallas_call(
        body, name=name, grid=(nblk,), in_specs=specs, out_specs=out_specs, out_shape=out_shape,
        compiler_params=_params(("arbitrary",) if out_accs else ("parallel",)),
    )(*arrays)
    return res


def _heads(x, n, w):
    return [x[:, h * w:(h + 1) * w] for h in range(n)]


def _cat(xs):
    return jnp.concatenate(xs, axis=1)


def _row_index(ctx, nrows, offset=0):
    return ctx.i * ctx.tl + offset + _iota((nrows, 1), 0)


def _ln_stats(r):
    mu = jnp.mean(r, axis=1, keepdims=True)
    d = r - mu
    var = jnp.mean(d * d, axis=1, keepdims=True)
    rstd = lax.rsqrt(var + LN_EPS)
    return d * rstd, rstd


def ln_fwd(name, terms, g, b, tl=ROWS, deps=()):
    coefs = [c for c, _ in terms]
    length = terms[0][1].shape[0]

    def fn(ctx, rows, consts, prevs, nexts):
        r = sum(c * t for c, t in zip(coefs, rows))
        xh, _ = _ln_stats(r)
        return [xh * consts[0] + consts[1], r], []

    return rowwise(name, fn, length, min(tl, length), rows=[t for _, t in terms], consts=[g, b],
                   out_rows=[(D_MODEL, F32), (D_MODEL, F32)], deps=deps)


def ln_bwd(name, r, terms, g, tl=ROWS, deps=()):
    coefs = [c for c, _ in terms]
    length = r.shape[0]

    def fn(ctx, rows, consts, prevs, nexts):
        xh, rstd = _ln_stats(rows[0])
        dy = sum(c * t for c, t in zip(coefs, rows[1:]))
        dxh = dy * consts[0]
        dr = rstd * (dxh - jnp.mean(dxh, axis=1, keepdims=True) - xh * jnp.mean(dxh * xh, axis=1, keepdims=True))
        return [dr], [_csum(dy * xh), _csum(dy)]

    return rowwise(name, fn, length, min(tl, length), rows=[r] + [t for _, t in terms], consts=[g],
                   out_rows=[(D_MODEL, F32)], out_accs=[((1, D_MODEL), F32), ((1, D_MODEL), F32)], deps=deps)


def ln_loss(name, terms, g, b, target, tl=ROWS):
    coefs = [c for c, _ in terms]
    length = target.shape[0]
    nt = len(terms)

    def fn(ctx, rows, consts, prevs, nexts):
        r = sum(c * t for c, t in zip(coefs, rows[:nt]))
        xh, _ = _ln_stats(r)
        err = xh * consts[0] + consts[1] - rows[nt]
        tot = _csum(_rsum(err * err)) * (0.5 / D_MODEL)
        return [err * (1.0 / D_MODEL), r], [jnp.broadcast_to(tot, (1, 128))]

    return rowwise(name, fn, length, min(tl, length), rows=[t for _, t in terms] + [target], consts=[g, b],
                   out_rows=[(D_MODEL, F32), (D_MODEL, F32)], out_accs=[((1, 128), F32)])


def _ffn_blocks(length):
    return min(512, length), D_FF // 2


def ffn_gate_up_act(name, x, wg, wu, deps=()):
    length = x.shape[0]
    tm, tn = _ffn_blocks(length)

    def body(x_ref, wg_ref, wu_ref, *rest):
        hg_ref, hu_ref, act_ref = rest[-3:]
        xb = x_ref[...].astype(MMD)
        hg = dnt(xb, wg_ref[...])
        hu = dnt(xb, wu_ref[...])
        hg_ref[...] = hg
        hu_ref[...] = hu
        act_ref[...] = (_silu(hg) * hu).astype(act_ref.dtype)

    row = pl.BlockSpec((tm, D_MODEL), lambda i, j: (i, 0))
    wsp = pl.BlockSpec((tn, D_MODEL), lambda i, j: (j, 0))
    osp = pl.BlockSpec((tm, tn), lambda i, j: (i, j))
    return pl.pallas_call(
        body, name=name, grid=(length // tm, D_FF // tn), in_specs=[row, wsp, wsp] + [ANY] * len(deps),
        out_specs=[osp] * 3,
        out_shape=[jax.ShapeDtypeStruct((length, D_FF), F32)] * 2 + [jax.ShapeDtypeStruct((length, D_FF), BF16)],
        compiler_params=_params(("parallel", "parallel")),
    )(x, wg, wu, *deps)


def ffn_dact(name, dr, wd, hg, hu, deps=()):
    length = dr.shape[0]
    tm, tn = _ffn_blocks(length)

    def body(dr_ref, wd_ref, hg_ref, hu_ref, *rest):
        dhg_ref, dhu_ref = rest[-2:]
        da = 0.5 * dnt(dr_ref[...], wd_ref[...])
        g = hg_ref[...]
        s = _sigmoid(g)
        dhg_ref[...] = (da * hu_ref[...] * (s * (1.0 + g * (1.0 - s)))).astype(dhg_ref.dtype)
        dhu_ref[...] = (da * (g * s)).astype(dhu_ref.dtype)

    row = pl.BlockSpec((tm, D_MODEL), lambda i, j: (i, 0))
    wsp = pl.BlockSpec((tn, D_MODEL), lambda i, j: (j, 0))
    osp = pl.BlockSpec((tm, tn), lambda i, j: (i, j))
    return pl.pallas_call(
        body, name=name, grid=(length // tm, D_FF // tn), in_specs=[row, wsp, osp, osp] + [ANY] * len(deps),
        out_specs=[osp] * 2, out_shape=[jax.ShapeDtypeStruct((length, D_FF), BF16)] * 2,
        compiler_params=_params(("parallel", "parallel")),
    )(dr, wd, hg, hu, *deps)


def ffn_fwd(tag, x, wg, wu, wd, deps=()):
    hg, hu, act = ffn_gate_up_act(tag + "_gate_up", x, wg, wu, deps)
    if callable(wd):
        wd = wd(act)
    f = mm(tag + "_down", act, wd)
    return f, (hg, hu, act), wd


def ffn_bwd(tag, x, res, dr, wg, wu, wd, deps=(), on_dw=None, also=None):
    on_dw = on_dw or (lambda which, dw: ())
    hg, hu, act = res
    dwd = mm(tag + "_dwd", act, dr, ta=True, scale=0.5, deps=deps)
    dhg, dhu = ffn_dact(tag + "_dact", dr, wd, hg, hu, deps=on_dw("down", dwd))
    dwg = mm(tag + "_dwg", dhg, x, ta=True)
    dwu = mm(tag + "_dwu", dhu, x, ta=True, deps=on_dw("gate", dwg))
    dx = mm(tag + "_dxg", dhg, wg, deps=on_dw("up", dwu))
    dx = mm(tag + "_dxu", dhu, wu, add=[(1.0, dx)] + ([also] if also else []))
    return dx, dwg, dwu, dwd


def _conv_taps(ext, taps, n):
    out = taps[3] * ext
    for j in range(3):
        out = out + taps[j] * pltpu.roll(ext, 3 - j, 0)
    return out


def _l2n(x):
    r = lax.rsqrt(_rsum(x * x) + L2_EPS)
    return x * r, r


def conv_fwd(name, pre, taps, tl=ROWS_WIDE, deps=()):
    length = pre.shape[0]
    tl = min(tl, length)

    def fn(ctx, rows, consts, prevs, nexts):
        prev = jnp.where(ctx.i > 0, prevs[0], 0.0)
        ext = jnp.concatenate([prev, rows[0]], axis=0)
        s = _silu(_conv_taps(ext, consts, tl + HALO)[HALO:])
        q = _cat([_l2n(x)[0] * (HD ** -0.5) for x in _heads(s[:, :DN_WIDTH], DN_HEADS, HD)])
        k = _cat([_l2n(x)[0] for x in _heads(s[:, DN_WIDTH:2 * DN_WIDTH], DN_HEADS, HD)])
        return [q, k, s[:, 2 * DN_WIDTH:]], []

    return rowwise(name, fn, length, tl, rows=[pre], consts=list(taps), prevs=[pre],
                   out_rows=[(DN_WIDTH, F32)] * 3, deps=deps)


def conv_bwd(name, pre, dq, dk, dv, taps, tl=ROWS_WIDE):
    length = pre.shape[0]
    tl = min(tl, length)
    n = tl + 2 * HALO

    def fn(ctx, rows, consts, prevs, nexts):
        last = ctx.i == ctx.nblk - 1
        prev = jnp.where(ctx.i > 0, prevs[0], 0.0)
        ext = jnp.concatenate([prev, rows[0], nexts[0]], axis=0)
        c = _conv_taps(ext, consts, n)
        sg = _sigmoid(c)
        s = c * sg
        zero = jnp.zeros((HALO, DN_WIDTH), F32)
        dqe, dke, dve = [jnp.concatenate([zero, rows[1 + t], jnp.where(last, 0.0, nexts[1 + t])], axis=0)
                         for t in range(3)]

        def l2_bwd(x, dy):
            y, r = _l2n(x)
            return r * (dy - y * _rsum(dy * y))

        dsq = _cat([l2_bwd(x, d * (HD ** -0.5)) for x, d in zip(_heads(s[:, :DN_WIDTH], DN_HEADS, HD),
                                                                 _heads(dqe, DN_HEADS, HD))])
        dsk = _cat([l2_bwd(x, d) for x, d in zip(_heads(s[:, DN_WIDTH:2 * DN_WIDTH], DN_HEADS, HD),
                                                  _heads(dke, DN_HEADS, HD))])
        dc = _cat([dsq, dsk, dve]) * (sg * (1.0 + c * (1.0 - sg)))
        dpre = consts[3] * dc
        for j in range(3):
            dpre = dpre + consts[j] * pltpu.roll(dc, n - (3 - j), 0)
        dc_cur = dc[HALO:HALO + tl]
        dws = [_csum(dc_cur * pltpu.roll(ext, 3 - j, 0)[HALO:HALO + tl]) for j in range(3)]
        dws.append(_csum(dc_cur * ext[HALO:HALO + tl]))
        return [dpre[HALO:HALO + tl]], dws

    return rowwise(name, fn, length, tl, rows=[pre, dq, dk, dv], consts=list(taps), prevs=[pre],
                   nexts=[pre, dq, dk, dv], out_rows=[(3 * DN_WIDTH, BF16)],
                   out_accs=[((1, 3 * DN_WIDTH), F32)] * 4)


def _gate_math(ab, alog, dtb):
    z = ab + dtb
    g = -jnp.exp(alog) * _softplus(z)
    beta = _sigmoid(ab)
    return z, g, beta


def gates_fwd(name, ab, alog, dtb, tl=ROWS):
    length = ab.shape[0]

    def fn(ctx, rows, consts, prevs, nexts):
        _, g, beta = _gate_math(rows[0], consts[0], consts[1])
        spread = [jnp.broadcast_to(v[:, h:h + 1], (v.shape[0], HD))
                  for v, first in ((g, 0), (beta, DN_HEADS)) for h in range(first, first + DN_HEADS)]
        return [_cat(spread[:DN_HEADS]), _cat(spread[DN_HEADS:])], []

    return rowwise(name, fn, length, min(tl, length), rows=[ab], consts=[alog, dtb],
                   out_rows=[(DN_WIDTH, F32)] * 2)


def gates_bwd(name, ab, dgb, dbb, alog, dtb, tl=ROWS):
    length = ab.shape[0]

    def fn(ctx, rows, consts, prevs, nexts):
        z, g, beta = _gate_math(rows[0], consts[0], consts[1])
        lane = _iota(g.shape, 1)
        dsmall = jnp.zeros_like(g)
        for h in range(DN_HEADS):
            dsmall = jnp.where(lane == h, rows[1][:, h * HD:h * HD + 1], dsmall)
            dsmall = jnp.where(lane == DN_HEADS + h, rows[2][:, h * HD:h * HD + 1], dsmall)
        is_a = lane < DN_HEADS
        da = jnp.where(is_a, dsmall * (-jnp.exp(consts[0])) * _sigmoid(z), 0.0)
        db = jnp.where((lane >= DN_HEADS) & (lane < 2 * DN_HEADS), dsmall * beta * (1.0 - beta), 0.0)
        return [da + db], [_csum(jnp.where(is_a, dsmall * g, 0.0)), _csum(da)]

    return rowwise(name, fn, length, min(tl, length), rows=[ab, dgb, dbb], consts=[alog, dtb],
                   out_rows=[(128, BF16)], out_accs=[((1, 128), F32)] * 2)


CPS = 4


def _chunk_scan_rows(x, suffix=False):
    n = x.shape[0]
    rc = _iota(x.shape, 0) & (CHUNK - 1)
    sh = 1
    while sh < CHUNK:
        if suffix:
            x = x + jnp.where(rc < CHUNK - sh, pltpu.roll(x, n - sh, 0), 0.0)
        else:
            x = x + jnp.where(rc >= sh, pltpu.roll(x, sh, 0), 0.0)
        sh *= 2
    return x


def _tri_inv(a_list, eye, bd):
    def each(f, *ls):
        return [f(*xs) for xs in zip(*ls)]

    dg = [jnp.where(bd, a, 0.0) for a in a_list]
    lo = each(lambda a, d: a - d, a_list, dg)
    n1 = [-d for d in dg]
    n2 = each(lambda n: dnn(n, n, X3), n1)
    n4 = each(lambda n: dnn(n, n, X3), n2)
    td = each(lambda p, s: dnn(eye + p, eye + s, X3), n1, n2)
    n8 = each(lambda n: dnn(n, n, X3), n4)
    td = each(lambda t, n: dnn(t, eye + n, X3), td, n4)
    td = each(lambda t, n: dnn(t, eye + n, X3), td, n8)
    m = each(lambda t, l: dnn(t, l, X3), td, lo)
    m2 = each(lambda x: dnn(x, x, X3), m)
    x = each(lambda p, s: dnn(eye - p, eye + s, X3), m, m2)
    return each(lambda p, t: dnn(p, t, X3), x, td)


def _chunk_common(q, k, v, gcb, bb):
    egb = jnp.exp(gcb)
    gc64 = gcb[:, :CHUNK]
    ii, jj = _iota((CHUNK, CHUNK), 0), _iota((CHUNK, CHUNK), 1)
    incl, strict = ii >= jj, ii > jj
    decay = jnp.exp(jnp.where(incl, gc64 - gc64.T, -jnp.inf))
    kb = k * bb
    vb = v * bb
    kbe = kb * egb
    pq = dnt(jnp.concatenate([kb, q], axis=0), k, X3)
    ekb = jnp.exp(gcb[CHUNK - 1:CHUNK, :] - gcb)
    return dict(egb=egb, decay=decay, kb=kb, vb=vb, kbe=kbe, pm=pq[:CHUNK], qm=pq[CHUNK:], ekb=ekb,
                incl=incl, strict=strict, ii=ii, jj=jj)


def _chunk_head(vals, ci, h):
    return [v[ci * CHUNK:(ci + 1) * CHUNK, h * HD:(h + 1) * HD] for v in vals]


def _assemble(per_chunk):
    return jnp.concatenate([_cat(hs) for hs in per_chunk], axis=0)


def _assemble3(per_chunk):
    return jnp.stack([jnp.concatenate([per_chunk[ci][h] for ci in range(CPS)], axis=0) for h in range(DN_HEADS)])


def delta_prep_fwd(name, q, k, v, gb, bb):
    length = q.shape[0]

    def fn(ctx, rows, consts, prevs, nexts):
        gcb_all = _chunk_scan_rows(rows[3])
        vals = [rows[0], rows[1], rows[2], gcb_all, rows[4]]
        units = [(ci, h) for ci in range(CPS) for h in range(DN_HEADS)]
        ins = [_chunk_head(vals, ci, h) for ci, h in units]
        cs = [_chunk_common(*i) for i in ins]
        eye = (cs[0]["ii"] == cs[0]["jj"]).astype(F32)
        ts = _tri_inv([jnp.where(c["strict"], c["pm"] * c["decay"], 0.0) for c in cs], eye,
                      (cs[0]["ii"] >> 4) == (cs[0]["jj"] >> 4))
        uws = [dnn(t, _cat([c["vb"], c["kbe"]]), X3) for t, c in zip(ts, cs)]

        def grid2(xs):
            return [xs[ci * DN_HEADS:(ci + 1) * DN_HEADS] for ci in range(CPS)]

        return [_assemble(grid2([uw[:, :HD] for uw in uws])), _assemble(grid2([uw[:, HD:] for uw in uws])),
                _assemble(grid2([i[0] * c["egb"] for i, c in zip(ins, cs)])),
                _assemble(grid2([i[1] * c["ekb"] for i, c in zip(ins, cs)])), gcb_all,
                _assemble3(grid2([c["qm"] * c["decay"] for c in cs])), _assemble3(grid2(ts))], []

    return rowwise(name, fn, length, CHUNK * CPS, rows=[q, k, v, gb, bb],
                   out_rows=[(DN_WIDTH, F32)] * 5 + [(DN_HEADS, CHUNK, F32)] * 2)


def delta_prep_bwd(name, q, k, v, gb, bb, t3, du, dw, dqd, dkd, dattn3, dgl):
    length = q.shape[0]

    def fn(ctx, rows, consts, prevs, nexts):
        gcb_all = _chunk_scan_rows(rows[3])
        vals = [rows[0], rows[1], rows[2], gcb_all] + list(rows[4:9])
        t3v, da3v, dglv = rows[9], rows[10], rows[11]
        units = [(ci, h) for ci in range(CPS) for h in range(DN_HEADS)]
        ins = [_chunk_head(vals, ci, h) for ci, h in units]
        cs = [_chunk_common(*i[:5]) for i in ins]
        ts = [t3v[h][ci * CHUNK:(ci + 1) * CHUNK] for ci, h in units]
        dattns = [jnp.where(c["incl"], da3v[h][ci * CHUNK:(ci + 1) * CHUNK], 0.0) for (ci, h), c in zip(units, cs)]
        duws = [_cat([i[5], i[6]]) for i in ins]
        dvks = [dtn(t, d, X3) for t, d in zip(ts, duws)]
        dts = [dnt(d, _cat([c["vb"], c["kbe"]]), X3) for d, c in zip(duws, cs)]
        dts = [dnt(d, t, X3) for d, t in zip(dts, ts)]
        das = [jnp.where(c["strict"], -dtn(t, d, X3), 0.0) for c, t, d in zip(cs, ts, dts)]
        dpqs = [jnp.concatenate([da * c["decay"], dat * c["decay"]], axis=0) for da, dat, c in zip(das, dattns, cs)]
        dpqks = [dnn(d, i[1], X3) for d, i in zip(dpqs, ins)]
        dkps = [dtn(d, jnp.concatenate([c["kb"], i[0]], axis=0), X3) for d, c, i in zip(dpqs, cs, ins)]
        dqs, dks, dvs, dgcs, dbs = [], [], [], [], []
        for (ci, h), i, c, dvk, da, dattn, dpqk, dkp in zip(units, ins, cs, dvks, das, dattns, dpqks, dkps):
            qh, kh, vh, _, bh, _, _, dqdh, dkdh = i
            dvb, dkbe = dvk[:, :HD], dvk[:, HD:]
            dkb = dpqk[:CHUNK] + dkbe * c["egb"]
            c1 = _rsum(dkbe * c["kb"] + dqdh * qh) * c["egb"]
            c2 = _rsum(dkdh * kh) * c["ekb"]
            e = (da * c["pm"] + dattn * c["qm"]) * c["decay"]
            dgc = c1 - c2 + _rsum(e) - _rsum(e.T)
            dgl_tot = jnp.max(dglv[ci * 8:(ci + 1) * 8, h * HD:(h + 1) * HD], axis=0, keepdims=True) + _csum(c2)
            dgcs.append(dgc + jnp.where(_iota((CHUNK, HD), 0) == CHUNK - 1, dgl_tot, 0.0))
            dqs.append(dpqk[CHUNK:] + dqdh * c["egb"])
            dks.append(dkp + dkdh * c["ekb"] + dkb * bh)
            dvs.append(dvb * bh)
            dbs.append(jnp.broadcast_to(_rsum(dkb * kh + dvb * vh), (CHUNK, HD)))

        def grid2(xs):
            return [xs[ci * DN_HEADS:(ci + 1) * DN_HEADS] for ci in range(CPS)]

        return [_assemble(grid2(dqs)), _assemble(grid2(dks)), _assemble(grid2(dvs)),
                _chunk_scan_rows(_assemble(grid2(dgcs)), suffix=True), _assemble(grid2(dbs))], []

    return rowwise(name, fn, length, CHUNK * CPS,
                   rows=[q, k, v, gb, bb, du, dw, dqd, dkd, t3, dattn3, (dgl, DN_WIDTH, 0, 8 * CPS)],
                   out_rows=[(DN_WIDTH, F32)] * 5)


SCAN_CHUNKS = 8


def _scan_chunks(n):
    return SCAN_CHUNKS if n % SCAN_CHUNKS == 0 else 1


def delta_scan_fwd(name, qd, kd, u, w, attn3, gcb):
    length = qd.shape[0]
    n = length // CHUNK
    sc = _scan_chunks(n)
    row = pl.BlockSpec((sc * CHUNK, DN_WIDTH), lambda c: (c, 0))
    sq = pl.BlockSpec((DN_HEADS, sc * CHUNK, CHUNK), lambda c: (0, c, 0))

    def body(qd_ref, kd_ref, u_ref, w_ref, attn_ref, gc_ref, o_ref, vn_ref, st_ref, s_ref):
        c = pl.program_id(0)

        @pl.when(c == 0)
        def _():
            s_ref[...] = jnp.zeros_like(s_ref)

        heads = range(DN_HEADS)
        sls = [pl.ds(h * HD, HD) for h in heads]
        ss = [s_ref[h] for h in heads]
        for ci in range(sc):
            rs = pl.ds(ci * CHUNK, CHUNK)
            ws = [dnn(w_ref[rs, sl], s) for sl, s in zip(sls, ss)]
            qs = [dnn(qd_ref[rs, sl], s) for sl, s in zip(sls, ss)]
            vns = [u_ref[rs, sl] - x for sl, x in zip(sls, ws)]
            avs = [dnn(attn_ref[h, rs, :], vn) for h, vn in zip(heads, vns)]
            kvs = [dtn(kd_ref[rs, sl], vn) for sl, vn in zip(sls, vns)]
            for h, sl in zip(heads, sls):
                st_ref[ci, h] = ss[h]
                o_ref[rs, sl] = qs[h] + avs[h]
                vn_ref[rs, sl] = vns[h]
            ss = [s * jnp.exp(gc_ref[pl.ds(ci * CHUNK + CHUNK - 1, 1), sl]) + kv for s, sl, kv in zip(ss, sls, kvs)]
        for h in heads:
            s_ref[h] = ss[h]

    return pl.pallas_call(
        body, name=name, grid=(n // sc,), in_specs=[row, row, row, row, sq, row],
        out_specs=[row, row, pl.BlockSpec((sc, DN_HEADS, HD, HD), lambda c: (c, 0, 0, 0))],
        out_shape=[jax.ShapeDtypeStruct((length, DN_WIDTH), F32), jax.ShapeDtypeStruct((length, DN_WIDTH), F32),
                   jax.ShapeDtypeStruct((n, DN_HEADS, HD, HD), F32)],
        scratch_shapes=[pltpu.VMEM((DN_HEADS, HD, HD), F32)],
        compiler_params=_params(("arbitrary",)),
    )(qd, kd, u, w, attn3, gcb)


def delta_scan_bwd(name, do, qd, kd, w, attn3, vn, st, gcb):
    length = qd.shape[0]
    n = length // CHUNK
    sc = _scan_chunks(n)
    nb = n // sc
    row = pl.BlockSpec((sc * CHUNK, DN_WIDTH), lambda c: (nb - 1 - c, 0))
    sq = pl.BlockSpec((DN_HEADS, sc * CHUNK, CHUNK), lambda c: (0, nb - 1 - c, 0))
    stb = pl.BlockSpec((sc, DN_HEADS, HD, HD), lambda c: (nb - 1 - c, 0, 0, 0))
    glb = pl.BlockSpec((sc * 8, DN_WIDTH), lambda c: (nb - 1 - c, 0))

    def body(do_ref, qd_ref, kd_ref, w_ref, attn_ref, vn_ref, st_ref, gc_ref,
             dqd_ref, dkd_ref, du_ref, dw_ref, dattn_ref, dgl_ref, ds_ref):
        c = pl.program_id(0)

        @pl.when(c == 0)
        def _():
            ds_ref[...] = jnp.zeros_like(ds_ref)

        heads = range(DN_HEADS)
        sls = [pl.ds(h * HD, HD) for h in heads]
        dsns = [ds_ref[h] for h in heads]
        for ci in reversed(range(sc)):
            rs = pl.ds(ci * CHUNK, CHUNK)
            ss = [st_ref[ci, h] for h in heads]
            dos = [do_ref[rs, sl] for sl in sls]
            vns = [vn_ref[rs, sl] for sl in sls]
            dvns = [dtn(attn_ref[h, rs, :], d) for h, d in zip(heads, dos)]
            dvns = [x + dnn(kd_ref[rs, sl], dsn) for x, sl, dsn in zip(dvns, sls, dsns)]
            qdos = [dtn(qd_ref[rs, sl], d) for sl, d in zip(sls, dos)]
            for h, sl in zip(heads, sls):
                dattn_ref[h, rs, :] = dnt(dos[h], vns[h])
                dqd_ref[rs, sl] = dnt(dos[h], ss[h])
                dkd_ref[rs, sl] = dnt(vns[h], dsns[h])
                du_ref[rs, sl] = dvns[h]
            dws = [dnt(dvn, s) for dvn, s in zip(dvns, ss)]
            wdvs = [dtn(w_ref[rs, sl], dvn) for sl, dvn in zip(sls, dvns)]
            nxt = []
            for h, sl in zip(heads, sls):
                egl = jnp.exp(gc_ref[pl.ds(ci * CHUNK + CHUNK - 1, 1), sl])
                dw_ref[rs, sl] = -dws[h]
                dgl_ref[pl.ds(ci * 8, 8), sl] = jnp.broadcast_to(_csum(_rsum(dsns[h] * ss[h])) * egl, (8, HD))
                nxt.append(dsns[h] * egl + qdos[h] - wdvs[h])
            dsns = nxt
        for h in heads:
            ds_ref[h] = dsns[h]

    return pl.pallas_call(
        body, name=name, grid=(nb,), in_specs=[row, row, row, row, sq, row, stb, row],
        out_specs=[row, row, row, row, sq, glb],
        out_shape=[jax.ShapeDtypeStruct((length, DN_WIDTH), F32)] * 4
        + [jax.ShapeDtypeStruct((DN_HEADS, length, CHUNK), F32), jax.ShapeDtypeStruct((n * 8, DN_WIDTH), F32)],
        scratch_shapes=[pltpu.VMEM((DN_HEADS, HD, HD), F32)],
        compiler_params=_params(("arbitrary",)),
    )(do, qd, kd, w, attn3, vn, st, gcb)


def onorm_fwd(name, o, z, nw, tl=ROWS):
    length = o.shape[0]

    def fn(ctx, rows, consts, prevs, nexts):
        outs = []
        for oh, zh in zip(_heads(rows[0], DN_HEADS, HD), _heads(rows[1], DN_HEADS, HD)):
            r = lax.rsqrt(jnp.mean(oh * oh, axis=1, keepdims=True) + RMS_EPS)
            outs.append(oh * r * consts[0] * _silu(zh))
        return [_cat(outs)], []

    return rowwise(name, fn, length, min(tl, length), rows=[o, z], consts=[nw], out_rows=[(DN_WIDTH, BF16)])[0]


def onorm_bwd(name, o, z, d_on, nw, tl=ROWS):
    length = o.shape[0]

    def fn(ctx, rows, consts, prevs, nexts):
        dos, dzs = [], []
        dnw = jnp.zeros((1, HD), F32)
        for oh, zh, dh in zip(*[_heads(r, DN_HEADS, HD) for r in rows]):
            r = lax.rsqrt(jnp.mean(oh * oh, axis=1, keepdims=True) + RMS_EPS)
            y = oh * r
            sz = _silu(zh)
            t = dh * sz * consts[0]
            dos.append(r * (t - y * jnp.mean(t * y, axis=1, keepdims=True)))
            dzs.append(dh * y * consts[0] * _dsilu(zh))
            dnw = dnw + _csum(dh * y * sz)
        return [_cat(dos), _cat(dzs)], [dnw]

    return rowwise(name, fn, length, min(tl, length), rows=[o, z, d_on], consts=[nw],
                   out_rows=[(DN_WIDTH, F32), (DN_WIDTH, BF16)], out_accs=[((1, HD), F32)])


def merge_fwd(name, gates, ydn, ypool, tl=ROWS_WIDE):
    length = ydn.shape[0]

    def fn(ctx, rows, consts, prevs, nexts):
        gt = rows[0]
        return [_sigmoid(gt[:, :D_MODEL]) * rows[1] + _sigmoid(gt[:, D_MODEL:]) * rows[2]], []

    return rowwise(name, fn, length, min(tl, length), rows=[gates, ydn, ypool], out_rows=[(D_MODEL, BF16)])[0]


def merge_bwd(name, gates, ydn, ypool, dm, tl=ROWS_WIDE):
    length = ydn.shape[0]

    def fn(ctx, rows, consts, prevs, nexts):
        gt, yd, yp, d = rows
        sd, sp = _sigmoid(gt[:, :D_MODEL]), _sigmoid(gt[:, D_MODEL:])
        dgates = _cat([d * yd * sd * (1.0 - sd), d * yp * sp * (1.0 - sp)])
        return [d * sd, d * sp, dgates], []

    return rowwise(name, fn, length, min(tl, length), rows=[gates, ydn, ypool, dm],
                   out_rows=[(D_MODEL, BF16), (D_MODEL, BF16), (2 * D_MODEL, BF16)])


def _trailing_sums(ext, upto):
    s, sh = ext, 1
    while sh < upto:
        s = s + pltpu.roll(s, sh, 0)
        sh *= 2
    return s


def _leading_sums(ext, upto, n):
    s, sh = ext, 1
    while sh < upto:
        s = s + pltpu.roll(s, n - sh, 0)
        sh *= 2
    return s


def _pool_mixed(ctx, p, prev, tl):
    prevm = jnp.where(ctx.i > 0, prev, 0.0)
    t1 = (_row_index(ctx, tl) + 1).astype(F32)
    outs = []
    for gi, win in enumerate(POOL_WINDOWS):
        sl = slice(gi * HD, (gi + 1) * HD)
        ext = jnp.concatenate([prevm[:, sl], p[:, sl]], axis=0)
        mean = _trailing_sums(ext, win)[HALO:] / jnp.minimum(t1, float(win))
        outs.append(mean - p[:, sl])
    return outs


def pool_fwd(name, p, pool_w, scale, tl=ROWS):
    length = p.shape[0]
    tl = min(tl, length)

    def fn(ctx, rows, consts, prevs, nexts):
        mixed = _pool_mixed(ctx, rows[0], prevs[0], tl)
        y = _cat([dnn(m, consts[0][gi]) for gi, m in enumerate(mixed)])
        return [y * consts[1]], []

    return rowwise(name, fn, length, tl, rows=[p], consts=[pool_w, scale], prevs=[p],
                   out_rows=[(POOL_WIDTH, BF16)])[0]


def pool_bwd(name, p, dpo, pool_w, scale, tl=ROWS):
    length = p.shape[0]
    tl = min(tl, length)
    n = tl + HALO

    def fn(ctx, rows, consts, prevs, nexts):
        last = ctx.i == ctx.nblk - 1
        mixed = _pool_mixed(ctx, rows[0], prevs[0], tl)
        dext = jnp.concatenate([rows[1], jnp.where(last, 0.0, nexts[0])], axis=0)
        t1 = (_row_index(ctx, n) + 1).astype(F32)
        dps, dws, dscs = [], [], []
        for gi, win in enumerate(POOL_WINDOWS):
            sl = slice(gi * HD, (gi + 1) * HD)
            wg = consts[0][gi]
            dyraw = dext[:, sl] * consts[1][:, sl]
            dmix = dnt(dyraw, wg)
            dws.append(dtn(mixed[gi], dyraw[:tl]))
            dscs.append(_csum(rows[1][:, sl] * dnn(mixed[gi], wg)))
            lead = _leading_sums(dmix / jnp.minimum(t1, float(win)), win, n)
            dps.append(lead[:tl] - dmix[:tl])
        return [_cat(dps)], [jnp.stack(dws), _cat(dscs)]

    return rowwise(name, fn, length, tl, rows=[p, dpo], consts=[pool_w, scale], prevs=[p], nexts=[dpo],
                   out_rows=[(POOL_WIDTH, BF16)],
                   out_accs=[((len(POOL_WINDOWS), HD, HD), F32), ((1, POOL_WIDTH), F32)])


def _xa_probs(qh, kh):
    s = dnt(qh, kh) * (XA_HD ** -0.5)
    e = jnp.exp(s - jnp.max(s, axis=1, keepdims=True))
    return e / _rsum(e)


def xattn_fwd(name, qx, kx, vx, tl=ROWS):
    length = qx.shape[0]

    def fn(ctx, rows, consts, prevs, nexts):
        outs = [dnn(_xa_probs(qh, kh), vh) for qh, kh, vh in
                zip(_heads(rows[0], XA_HEADS, XA_HD), _heads(consts[0], XA_HEADS, XA_HD),
                    _heads(consts[1], XA_HEADS, XA_HD))]
        return [_cat(outs)], []

    return rowwise(name, fn, length, min(tl, length), rows=[qx], consts=[kx, vx], out_rows=[(D_MODEL, BF16)])[0]


def xattn_bwd(name, qx, dox, kx, vx, tl=ROWS):
    length = qx.shape[0]

    def fn(ctx, rows, consts, prevs, nexts):
        dqs, dks, dvs = [], [], []
        for qh, dh, kh, vh in zip(_heads(rows[0], XA_HEADS, XA_HD), _heads(rows[1], XA_HEADS, XA_HD),
                                  _heads(consts[0], XA_HEADS, XA_HD), _heads(consts[1], XA_HEADS, XA_HD)):
            pr = _xa_probs(qh, kh)
            dpr = dnt(dh, vh)
            ds = pr * (dpr - _rsum(dpr * pr)) * (XA_HD ** -0.5)
            dqs.append(dnn(ds, kh))
            dks.append(dtn(ds, qh))
            dvs.append(dtn(pr, dh))
        return [_cat(dqs)], [_cat(dks), _cat(dvs)]

    return rowwise(name, fn, length, min(tl, length), rows=[qx, dox], consts=[kx, vx],
                   out_rows=[(D_MODEL, BF16)], out_accs=[((N_MEM, D_MODEL), F32)] * 2)


def local_step(x, mem, target, w, io):
    alog = jnp.pad(w["a_log"], ((0, 0), (0, 128 - DN_HEADS)))
    dtb = jnp.pad(w["dt_bias"], ((0, 0), (0, 128 - DN_HEADS)))

    f1, res1, w_down1 = ffn_fwd("ffn1", x, w["ffn1_w_gate"], w["ffn1_w_up"], io.ffn1_down, deps=io.rest_started())
    x1, r1 = ln_fwd("ln1", [(ALPHA, x), (0.5, f1)], w["ln1_g"], w["ln1_b"], deps=io.halfway("mixer", f1))
    w = dict(w, ffn1_w_down=w_down1, **io.weights("mixer", x1))
    taps = [w["conv_w"][j:j + 1] for j in range(4)]

    pre = mm("in_qkv", x1, w["in_qkv"], tb=True)
    z = mm("in_z", x1, w["in_z"], tb=True)
    gates = mm("in_gates", x1, w["in_gates"], tb=True)
    p = mm("in_p", x1, w["in_p"], tb=True)
    ab = mm("in_ab", x1, w["in_ab"], tb=True)
    q, k, v = conv_fwd("conv", pre, taps, deps=io.halfway("xa", pre))
    gb, bb = gates_fwd("gates", ab, alog, dtb)
    u, wd_, qd, kd, gcb, attn3, t3 = delta_prep_fwd("dprep", q, k, v, gb, bb)
    o, vn, st = delta_scan_fwd("dscan", qd, kd, u, wd_, attn3, gcb)
    on = onorm_fwd("onorm", o, z, w["dn_norm_w"])
    ydn = mm("dn_branch", on, w["w_dn_branch"], tb=True)
    po = pool_fwd("pool", p, w["pool_w"], w["pool_scale"])
    ypool = mm("pool_branch", po, w["w_pool_branch"], tb=True)
    merged = merge_fwd("merge", gates, ydn, ypool)
    mix = mm("mix_out", merged, w["w_mix_out"])
    x2, r2 = ln_fwd("ln2", [(ALPHA, x1), (1.0, mix)], w["ln2_g"], w["ln2_b"])

    w = dict(w, **io.weights("xa", x2))
    m, _ = ln_fwd("ln_mem", [(1.0, mem)], w["mem_ln_g"], w["mem_ln_b"])
    qx = mm("xa_q", x2, w["xa_wq"], deps=io.halfway("ffn2", x2))
    kx = mm("xa_k", m, w["xa_wk"])
    vx = mm("xa_v", m, w["xa_wv"])
    ox = xattn_fwd("xattn", qx, kx, vx)
    xa = mm("xa_o", ox, w["xa_wo"])
    x3, r3 = ln_fwd("ln3", [(ALPHA, x2), (1.0, xa)], w["ln3_g"], w["ln3_b"])
    w = dict(w, **io.weights("ffn2", x3))

    f2, res2, _ = ffn_fwd("ffn2", x3, w["ffn2_w_gate"], w["ffn2_w_up"], w["ffn2_w_down"])
    dy4, r4, loss = ln_loss("ln4_loss", [(ALPHA, x3), (0.5, f2)], w["ln4_g"], w["ln4_b"], target)

    g = {}
    dr4, g["ln4_g"], g["ln4_b"] = ln_bwd("ln4_b", r4, [(1.0, dy4)], w["ln4_g"])
    dx3, g["ffn2_w_gate"], g["ffn2_w_up"], g["ffn2_w_down"] = ffn_bwd(
        "ffn2b", x3, res2, dr4, w["ffn2_w_gate"], w["ffn2_w_up"], w["ffn2_w_down"])
    dep = io.grads_out("ffn2", g)
    dr3, g["ln3_g"], g["ln3_b"] = ln_bwd("ln3_b", r3, [(ALPHA, dr4), (1.0, dx3)], w["ln3_g"], deps=dep)

    dox = mm("xa_do", dr3, w["xa_wo"], tb=True)
    g["xa_wo"] = mm("xa_dwo", ox, dr3, ta=True)
    dqx, dkx, dvx = xattn_bwd("xattn_b", qx, dox, kx, vx)
    g["xa_wq"] = mm("xa_dwq", x2, dqx, ta=True)
    dx2 = mm("xa_dx", dqx, w["xa_wq"], tb=True)
    g["xa_wk"] = mm("xa_dwk", m, dkx, ta=True)
    g["xa_wv"] = mm("xa_dwv", m, dvx, ta=True)
    dmm = mm("xa_dmk", dkx, w["xa_wk"], tb=True, deps=io.grads_out("xa", g))
    dmm = mm("xa_dmv", dvx, w["xa_wv"], tb=True, add=dmm)
    _, g["mem_ln_g"], g["mem_ln_b"] = ln_bwd("ln_mem_b", mem, [(1.0, dmm)], w["mem_ln_g"])
    dr2, g["ln2_g"], g["ln2_b"] = ln_bwd("ln2_b", r2, [(ALPHA, dr3), (1.0, dx2)], w["ln2_g"])
    io.grads_in("ffn2", dr2)

    dmerged = mm("mix_dm", dr2, w["w_mix_out"], tb=True)
    g["w_mix_out"] = mm("mix_dw", merged, dr2, ta=True)
    d_ydn, d_ypool, d_gates = merge_bwd("merge_b", gates, ydn, ypool, dmerged)
    g["w_dn_branch"] = mm("dn_dw", d_ydn, on, ta=True)
    d_on = mm("dn_dx", d_ydn, w["w_dn_branch"])
    g["w_pool_branch"] = mm("pool_dw", d_ypool, po, ta=True)
    d_po = mm("pool_dx", d_ypool, w["w_pool_branch"])
    dp, g["pool_w"], g["pool_scale"] = pool_bwd("pool_b", p, d_po, w["pool_w"], w["pool_scale"])
    d_o, dz, g["dn_norm_w"] = onorm_bwd("onorm_b", o, z, d_on, w["dn_norm_w"])
    dqd, dkd, du, dw_, dattn3, dgl = delta_scan_bwd("dscan_b", d_o, qd, kd, wd_, attn3, vn, st, gcb)
    dq, dk, dv, dgb, dbb = delta_prep_bwd("dprep_b", q, k, v, gb, bb, t3, du, dw_, dqd, dkd, dattn3, dgl)
    dpre, dc0, dc1, dc2, dc3 = conv_bwd("conv_b", pre, dq, dk, dv, taps)
    g["conv_w"] = jnp.concatenate([dc0, dc1, dc2, dc3], axis=0)
    d_ab, dalog, ddtb = gates_bwd("gates_b", ab, dgb, dbb, alog, dtb)
    g["a_log"] = dalog[:, :DN_HEADS]
    g["dt_bias"] = ddtb[:, :DN_HEADS]
    g["in_qkv"], g["in_z"], g["in_ab"] = mm_fan_t("in_dw_a", [dpre, dz, d_ab], x1)
    g["in_gates"], g["in_p"] = mm_fan_t("in_dw_b", [d_gates, dp], x1)
    io.grads_in("xa", g["in_ab"])
    dx1 = mm_sum("in_dx", [(dpre, w["in_qkv"]), (dz, w["in_z"]), (d_gates, w["in_gates"]), (dp, w["in_p"]),
                           (d_ab, w["in_ab"])], deps=io.grads_out("mixer", g))
    dr1, g["ln1_g"], g["ln1_b"] = ln_bwd("ln1_b", r1, [(ALPHA, dr2), (1.0, dx1)], w["ln1_g"])

    def on_dw(which, dw):
        name = "ffn1_w_" + which
        small = io.small_out(dict(g, loss=loss[0, :1])) if which == "down" else ()
        return small + io.grads_out(name, {name: dw})

    grad_x, g["ffn1_w_gate"], g["ffn1_w_up"], g["ffn1_w_down"] = ffn_bwd(
        "ffn1b", x, res1, dr1, w["ffn1_w_gate"], w["ffn1_w_up"], w["ffn1_w_down"], on_dw=on_dw, also=(ALPHA, dr1))
    return loss, grad_x, g


WEIGHT_NAMES = ['ffn1_w_gate', 'ffn1_w_up', 'ffn1_w_down', 'ln1_g', 'ln1_b', 'w_in', 'conv_w', 'a_log', 'dt_bias',
                'dn_norm_w', 'w_dn_branch', 'pool_w', 'pool_scale', 'w_pool_branch', 'w_mix_out', 'ln2_g', 'ln2_b',
                'mem_ln_g', 'mem_ln_b', 'xa_wq', 'xa_wk', 'xa_wv', 'xa_wo', 'ln3_g', 'ln3_b', 'ffn2_w_gate',
                'ffn2_w_up', 'ffn2_w_down', 'ln4_g', 'ln4_b']
SHARDED = [
    ("ffn1_w_gate", "cols", (1024, 352)), ("ffn1_w_up", "cols", (1024, 352)), ("ffn1_w_down", "rows", (352, 1024)),
    ("w_in", "cols", (1024, 577)), ("conv_w", "flat", (4, 192)), ("w_dn_branch", "cols", (512, 128)),
    ("w_pool_branch", "cols", (512, 128)), ("w_mix_out", "rows", (128, 1024)), ("xa_wq", "rows", (128, 1024)),
    ("xa_wk", "rows", (128, 1024)), ("xa_wv", "rows", (128, 1024)), ("xa_wo", "rows", (128, 1024)),
    ("ffn2_w_gate", "cols", (1024, 352)), ("ffn2_w_up", "cols", (1024, 352)), ("ffn2_w_down", "rows", (352, 1024)),
]
REPLICATED = [n for n in WEIGHT_NAMES if n not in {s[0] for s in SHARDED}]
ROW_ALIGN = 16
ROW_BLOCKS = (512, 384, 352, 256, 192, 176, 128)
GROUPS = {"ffn1_gu": ("ffn1_w_gate", "ffn1_w_up"), "ffn1_d": ("ffn1_w_down",),
          "ffn1_w_gate": ("ffn1_w_gate",), "ffn1_w_up": ("ffn1_w_up",), "ffn1_w_down": ("ffn1_w_down",),
          "mixer": ("w_in", "conv_w", "w_dn_branch", "w_pool_branch", "w_mix_out"),
          "xa": ("xa_wq", "xa_wk", "xa_wv", "xa_wo"),
          "ffn2": ("ffn2_w_gate", "ffn2_w_up", "ffn2_w_down")}
W_IN_COLS = 577
W_IN_PIECES = (("in_qkv", 0, 1536), ("in_z", 1536, 2048), ("in_ab", 2048, 2056), ("in_p", 2056, 2568),
               ("in_gates", 2568, 4616))


def _round_up(n, m):
    return -(-n // m) * m


def _layout():
    off, table = 0, {}
    for name, form, shape in SHARDED:
        valid = {"rows": shape[0], "cols": shape[1], "flat": 2}[form]
        width = {"rows": shape[1], "cols": shape[0], "flat": shape[0] * shape[1]}[form]
        rows = _round_up(valid, ROW_ALIGN)
        table[name] = (off, rows, valid, width, form, shape)
        off += rows
    return table


LAYOUT = _layout()


def _group_span(names):
    base = LAYOUT[names[0]][0]
    rows = LAYOUT[names[-1]][0] + LAYOUT[names[-1]][1] - base
    while not any(rows % b == 0 for b in ROW_BLOCKS):
        rows += ROW_ALIGN
    return base, rows


def _row_block(rows):
    return _pick(rows, ROW_BLOCKS)


def _pad_block(blk, rows):
    return jnp.pad(blk, ((0, rows - blk.shape[0]), (0, LANES - blk.shape[1])))


def pack_weight_shards(shards, names):
    parts, used = [], 0
    for name in names:
        off, rows, valid, width, form, _ = LAYOUT[name]
        s = shards[name]
        if form == "flat":
            flat = s.reshape(1, -1)
            hi = flat.astype(BF16)
            blk = jnp.concatenate([hi, (flat - hi.astype(F32)).astype(BF16)], axis=0)
        else:
            blk = (s.T if form == "cols" else s).astype(BF16)
        parts.append(_pad_block(blk, rows))
        used += rows
    if _group_span(names)[1] > used:
        parts.append(jnp.zeros((_group_span(names)[1] - used, LANES), BF16))
    return jnp.concatenate(parts, axis=0)


IN_AB_ROWS = 128


def _w_in_segments(first, last):
    segs = []
    for k in range(N_DEV):
        lo, hi = max(first, k * W_IN_COLS), min(last, (k + 1) * W_IN_COLS)
        if lo < hi:
            segs.append((k, lo - k * W_IN_COLS, lo - first, hi - lo))
    return segs


def w_in_pieces(name, gathered, off, rows):
    assert off % rows == 0
    sizes = [IN_AB_ROWS if piece == "in_ab" else last - first for piece, first, last in W_IN_PIECES]

    def body(src_ref, *outs):
        for o_ref, (piece, first, last) in zip(outs, W_IN_PIECES):
            if piece == "in_ab":
                o_ref[...] = jnp.zeros_like(o_ref)
            for k, src, dst, count in _w_in_segments(first, last):
                o_ref[pl.ds(dst, count), :] = src_ref[k, pl.ds(src, count), :]

    outs = pl.pallas_call(
        body, name=name, grid=(1,), in_specs=[pl.BlockSpec((N_DEV, rows, LANES), lambda i: (0, off // rows, 0))],
        out_specs=[pl.BlockSpec((n, LANES), lambda i: (0, 0)) for n in sizes],
        out_shape=[jax.ShapeDtypeStruct((n, LANES), gathered.dtype) for n in sizes],
        compiler_params=_params(("arbitrary",)),
    )(gathered)
    return {piece: o for (piece, _, _), o in zip(W_IN_PIECES, outs)}


def unpack_full_weights(gathered, names):
    out, base = {}, _group_span(names)[0]
    for name in names:
        off, rows, valid, width, form, shape = LAYOUT[name]
        seg = gathered[:, off - base:off - base + rows]
        if form == "flat":
            flat = seg[:, 0, :width].astype(F32) + seg[:, 1, :width].astype(F32)
            out[name] = flat.reshape((N_DEV,) + shape).transpose(1, 0, 2).reshape(shape[0], N_DEV * shape[1])
        elif name == "w_in":
            out.update(w_in_pieces("w_in_pieces", gathered, off - base, rows))
        else:
            out[name] = seg[:, :valid, :width].reshape(N_DEV * valid, width)
    return out


def pack_full_grads(grads, names, me):
    wire, own, used = [], [], 0
    for name in names:
        off, rows, valid, width, form, shape = LAYOUT[name]
        if form == "flat":
            full = grads[name].reshape(shape[0], N_DEV, shape[1]).transpose(1, 0, 2).reshape(N_DEV, 1, width)
        elif name == "w_in":
            full = jnp.concatenate([grads[piece][:last - first] for piece, first, last in W_IN_PIECES], axis=0)
            full = full.reshape(N_DEV, valid, width)
        else:
            full = grads[name].reshape(N_DEV, valid, width)
        pad = ((0, rows - full.shape[1]), (0, LANES - width))
        wire.append(jnp.pad(full.astype(WIRE), ((0, 0),) + pad))
        own.append(jnp.pad(lax.dynamic_index_in_dim(full, me, 0, keepdims=False), pad))
        used += rows
    if _group_span(names)[1] > used:
        wire.append(jnp.zeros((N_DEV, _group_span(names)[1] - used, LANES), WIRE))
        own.append(jnp.zeros((_group_span(names)[1] - used, LANES), F32))
    return jnp.concatenate(wire, axis=1), jnp.concatenate(own, axis=0)


TRANSPOSED = ("ffn1_w_gate", "ffn1_w_up", "ffn2_w_gate", "ffn2_w_up", "w_in")


def unpack_grad_shards(packed, names):
    out, base = {}, _group_span(names)[0]
    for name in names:
        off, rows, valid, width, form, shape = LAYOUT[name]
        off -= base
        if form == "flat":
            out[name] = packed[off, :width].reshape(shape)
        elif name in TRANSPOSED:
            out[name] = packed[off:off + valid, :width]
        elif form == "cols":
            out[name] = packed[off:off + valid, :width].T
        else:
            out[name] = packed[off:off + valid, :width]
    return out


SMALL_SHAPES = {n: (1024,) for n in REPLICATED}
SMALL_SHAPES.update(pool_w=(4, 128, 128), pool_scale=(512,), dn_norm_w=(128,), a_log=(4,), dt_bias=(4,))


SMALL_SHAPES["loss"] = (1,)
SMALL_NAMES = REPLICATED + ["loss"]


def _small_layout():
    off, table = 0, {}
    for name in SMALL_NAMES:
        numel = 1
        for d in SMALL_SHAPES[name]:
            numel *= d
        rows = _round_up(-(-numel // LANES), 8)
        table[name] = (off, rows, numel)
        off += rows
    return table, off


SMALL_LAYOUT, SMALL_ROWS = _small_layout()


def _to_rows(flat, rows):
    return jnp.pad(flat, (0, rows * LANES - flat.shape[0])).reshape(rows, LANES)


def pack_small(values):
    return jnp.concatenate([_to_rows(values[name].reshape(-1), SMALL_LAYOUT[name][1]) for name in SMALL_NAMES], axis=0)


def unpack_small(packed):
    out = {}
    for name in SMALL_NAMES:
        off, rows, numel = SMALL_LAYOUT[name]
        out[name] = packed[off:off + rows].reshape(-1)[:numel].reshape(SMALL_SHAPES[name])
    return out


MESH = pl.DeviceIdType.MESH


def _position():
    return lax.axis_index("x"), lax.axis_index("y"), lax.axis_index("c")


def _other_chips(x, y):
    return [(1 - x, y), (x, 1 - y), (1 - x, 1 - y)]


def all_gather(name, block):
    rows, n = block.shape

    def body(x_ref, out_ref, send_sems, recv_sems, local_sem):
        x, y, c = _position()
        me, sibling = (x, y, c), (x, y, 1 - c)
        chips = _other_chips(x, y)

        def slot(px, py, pc):
            return out_ref.at[4 * px + 2 * py + pc]

        def copy(k, blk, to, src=None):
            return pltpu.make_async_remote_copy(
                src_ref=slot(*blk) if src is None else src, dst_ref=slot(*blk),
                send_sem=send_sems.at[k], recv_sem=recv_sems.at[k], device_id=to, device_id_type=MESH)

        mine = pltpu.make_async_copy(x_ref, slot(*me), local_sem)
        mine.start()
        first = [copy(0, me, sibling, src=x_ref)]
        first += [copy(1 + j, me, (*chip, c), src=x_ref) for j, chip in enumerate(chips)]
        for cp in first:
            cp.start()
        passed = [copy(4 + j, (*chip, c), sibling) for j, chip in enumerate(chips)]
        for j, chip in enumerate(chips):
            copy(1 + j, (*chip, c), me).wait_recv()
            passed[j].start()
        copy(0, sibling, me).wait_recv()
        for j, chip in enumerate(chips):
            copy(4 + j, (*chip, 1 - c), me).wait_recv()
        for cp in first + passed:
            cp.wait_send()
        mine.wait()

    return pl.pallas_call(
        body, name=name, out_shape=jax.ShapeDtypeStruct((N_DEV, rows, n), block.dtype),
        in_specs=[ANY], out_specs=ANY,
        scratch_shapes=[pltpu.SemaphoreType.DMA((7,)), pltpu.SemaphoreType.DMA((7,)), pltpu.SemaphoreType.DMA(())],
    )(block)


HBM = pl.BlockSpec(memory_space=pltpu.HBM)
SEM = pl.BlockSpec(memory_space=pltpu.SEMAPHORE)
EFFECT = pltpu.SideEffectType.DATAFLOW_SIDE_EFFECTING


def _remote(src, dst, send_sem, recv_sem, to):
    return pltpu.make_async_remote_copy(src_ref=src, dst_ref=dst, send_sem=send_sem, recv_sem=recv_sem,
                                        device_id=to, device_id_type=MESH)


def split_start(name, bufs, n, make_copies):
    nb = len(bufs)

    def body(*refs):
        for out_cp, _ in make_copies(refs[:nb], refs[nb:nb + n], refs[nb + n:nb + 2 * n]):
            out_cp.start()
        refs[-1][...] = jnp.zeros_like(refs[-1])

    outs = pl.pallas_call(
        body, name=name,
        out_shape=tuple([pltpu.SemaphoreType.DMA(())] * (2 * n)) + tuple(pltpu.HBM(b.shape, b.dtype) for b in bufs)
        + (jax.ShapeDtypeStruct((8, 128), F32),),
        in_specs=[HBM] * nb,
        out_specs=tuple([SEM] * (2 * n) + [HBM] * nb + [pl.BlockSpec(memory_space=pltpu.VMEM)]),
        input_output_aliases={i: 2 * n + i for i in range(nb)},
        compiler_params=pltpu.CompilerParams(has_side_effects=EFFECT),
    )(*[pltpu.with_memory_space_constraint(b, pltpu.HBM) for b in bufs])
    return list(outs[:2 * n]), list(outs[2 * n:2 * n + nb]), outs[-1]


def split_wait(name, bufs, sems, n, make_copies, after):
    nb = len(bufs)

    def body(*refs):
        for out_cp, in_cp in make_copies(refs[:nb], refs[nb:nb + n], refs[nb + n:nb + 2 * n]):
            out_cp.wait_send()
            in_cp.wait_recv()

    outs = pl.pallas_call(
        body, name=name, out_shape=tuple(pltpu.HBM(b.shape, b.dtype) for b in bufs),
        in_specs=[HBM] * nb + [SEM] * (2 * n) + [ANY], out_specs=tuple([HBM] * nb),
        input_output_aliases={i: i for i in range(nb)},
        compiler_params=pltpu.CompilerParams(has_side_effects=EFFECT),
    )(*bufs, *sems, after)
    return list(outs)


def _gather_stage1(refs, send, recv):
    src, land = refs
    x, y, c = _position()
    peers = [(x, y, 1 - c)] + [(*chip, c) for chip in _other_chips(x, y)]
    return [(_remote(src, land.at[4 * x + 2 * y + c], send[k], recv[k], p),
             _remote(src, land.at[4 * p[0] + 2 * p[1] + p[2]], send[k], recv[k], p)) for k, p in enumerate(peers)]


def _gather_stage2(refs, send, recv):
    (land,) = refs
    x, y, c = _position()
    out = []
    for j, (px, py) in enumerate(_other_chips(x, y)):
        mine, theirs = land.at[4 * px + 2 * py + c], land.at[4 * px + 2 * py + 1 - c]
        out.append((_remote(mine, mine, send[j], recv[j], (x, y, 1 - c)),
                    _remote(theirs, theirs, send[j], recv[j], (x, y, 1 - c))))
    return out


def _flips():
    return [(a, b, d) for a in (0, 1) for b in (0, 1) for d in (0, 1) if a | b | d]


def _gather_direct(refs, send, recv):
    src, land = refs
    x, y, c = _position()
    out = []
    for k, (fx, fy, fc) in enumerate(_flips()):
        p = (1 - x if fx else x, 1 - y if fy else y, 1 - c if fc else c)
        out.append((_remote(src, land.at[4 * x + 2 * y + c], send[k], recv[k], p),
                    _remote(src, land.at[4 * p[0] + 2 * p[1] + p[2]], send[k], recv[k], p)))
    return out


def _scatter_direct(refs, send, recv):
    sendbuf, land = refs
    x, y, c = _position()
    out = []
    for k, (fx, fy, fc) in enumerate(_flips()):
        p = (1 - x if fx else x, 1 - y if fy else y, 1 - c if fc else c)
        cp = _remote(sendbuf.at[4 * p[0] + 2 * p[1] + p[2]], land.at[k], send[k], recv[k], p)
        out.append((cp, cp))
    return out


def _own_plus_slots(name, own, landed):
    n, rows, _ = landed.shape
    tr = _row_block(rows)

    def body(g_ref, l_ref, o_ref):
        acc = g_ref[...]
        for j in range(n):
            acc = acc + l_ref[j].astype(F32)
        o_ref[...] = acc

    return pl.pallas_call(
        body, name=name, grid=(rows // tr,),
        in_specs=[pl.BlockSpec((tr, LANES), lambda i: (i, 0)), pl.BlockSpec((n, tr, LANES), lambda i: (0, i, 0))],
        out_specs=pl.BlockSpec((tr, LANES), lambda i: (i, 0)),
        out_shape=jax.ShapeDtypeStruct((rows, LANES), F32), compiler_params=_params(("parallel",)),
    )(own, landed)


def _sum_slots(name, stack):
    n, rows, _ = stack.shape

    def body(s_ref, o_ref):
        acc = s_ref[0]
        for j in range(1, n):
            acc = acc + s_ref[j]
        o_ref[...] = acc

    return pl.pallas_call(
        body, name=name, in_specs=[pl.BlockSpec(stack.shape, lambda: (0, 0, 0))],
        out_specs=pl.BlockSpec((rows, LANES), lambda: (0, 0)), out_shape=jax.ShapeDtypeStruct((rows, LANES), F32),
    )(stack)


def adamw(name, w, g, m, v):
    shape = w.shape
    last = shape[-1]
    w2, g2, m2, v2 = [a.reshape(-1, last) for a in (w, g, m, v)]
    rows = w2.shape[0]
    tr = _pick(rows, (256, 176, 128))

    def body(w_ref, g_ref, m_ref, v_ref, d_ref, nm_ref, nv_ref):
        gg = g_ref[...]
        nm = ADAM_B1 * m_ref[...] + (1.0 - ADAM_B1) * gg
        nv = ADAM_B2 * v_ref[...] + (1.0 - ADAM_B2) * (gg * gg)
        m_hat = nm / (1.0 - ADAM_B1 ** ADAM_STEP)
        v_hat = nv / (1.0 - ADAM_B2 ** ADAM_STEP)
        d_ref[...] = -ADAM_LR * (m_hat / (jnp.sqrt(v_hat) + ADAM_EPS) + ADAM_WD * w_ref[...])
        nm_ref[...] = nm
        nv_ref[...] = nv

    spec = pl.BlockSpec((tr, last), lambda i: (i, 0))
    outs = pl.pallas_call(
        body, name=name, grid=(rows // tr,), in_specs=[spec] * 4, out_specs=[spec] * 3,
        out_shape=[jax.ShapeDtypeStruct((rows, last), F32)] * 3, compiler_params=_params(("parallel",)),
    )(w2, g2, m2, v2)
    return [o.reshape(shape) for o in outs]


def _landing(block_shape, dtype, own):
    x, y, c = _position()
    return lax.dynamic_update_slice(lax.empty((N_DEV,) + block_shape, dtype), own[None], (4 * x + 2 * y + c, 0, 0))


class _Exchanges:
    def __init__(self, shards):
        self.shards = shards
        self.pending = {}
        self.reduced = {}

    def first_weights(self):
        names = GROUPS["ffn1_gu"]
        return unpack_full_weights(all_gather("ag_ffn1_gu", pack_weight_shards(self.shards, names)), names)

    def rest_started(self):
        tokens = []
        block = pack_weight_shards(self.shards, GROUPS["ffn1_d"])
        sems, bufs, token = split_start("ag_ffn1_d_s", [block, _landing(block.shape, block.dtype, block)], N_DEV - 1,
                                        _gather_direct)
        self.pending["ffn1_d"] = (sems, bufs)
        tokens.append(token)
        for key in ("mixer", "xa", "ffn2"):
            block = pack_weight_shards(self.shards, GROUPS[key])
            sems, bufs, token = split_start(f"ag_{key}_s1", [block, _landing(block.shape, block.dtype, block)], 4,
                                            _gather_stage1)
            self.pending[key] = (sems, bufs)
            tokens.append(token)
        return tuple(tokens)

    def ffn1_down(self, after):
        sems, bufs = self.pending.pop("ffn1_d")
        _, gathered = split_wait("ag_ffn1_d_w", bufs, sems, N_DEV - 1, _gather_direct, after)
        return unpack_full_weights(gathered, GROUPS["ffn1_d"])["ffn1_w_down"]

    def halfway(self, key, after):
        sems, bufs = self.pending.pop(key)
        _, land = split_wait(f"ag_{key}_w1", bufs, sems, 4, _gather_stage1, after)
        sems, bufs, token = split_start(f"ag_{key}_s2", [land], 3, _gather_stage2)
        self.pending[key] = (sems, bufs)
        return (token,)

    def weights(self, key, after):
        sems, bufs = self.pending.pop(key)
        (gathered,) = split_wait(f"ag_{key}_w2", bufs, sems, 3, _gather_stage2, after)
        return unpack_full_weights(gathered, GROUPS[key])

    def grads_out(self, key, grads):
        x, y, c = _position()
        wire, own = pack_full_grads(grads, GROUPS[key], 4 * x + 2 * y + c)
        land = lax.empty((N_DEV - 1,) + wire.shape[1:], WIRE)
        sems, bufs, token = split_start(f"rs_{key}_start", [wire, land], N_DEV - 1, _scatter_direct)
        self.pending[key] = (sems, bufs, own)
        return (token,)

    def grads_in(self, key, after):
        sems, bufs, own = self.pending.pop(key)
        _, landed = split_wait(f"rs_{key}_wait", bufs, sems, N_DEV - 1, _scatter_direct, after)
        self.reduced.update(unpack_grad_shards(_own_plus_slots(f"rs_{key}_sum", own, landed), GROUPS[key]))

    def small_out(self, values):
        block = pack_small(values)
        sems, bufs, token = split_start("ag_small_s", [block, _landing(block.shape, block.dtype, block)], N_DEV - 1,
                                        _gather_direct)
        self.pending["small"] = (sems, bufs)
        return (token,)

    def small_in(self, after):
        sems, bufs = self.pending.pop("small")
        _, gathered = split_wait("ag_small_w", bufs, sems, N_DEV - 1, _gather_direct, after)
        return unpack_small(_sum_slots("small_sum", gathered))


def kernel(x, mem, ffn1_w_gate, ffn1_w_up, ffn1_w_down, ln1_g, ln1_b, w_in, conv_w, a_log, dt_bias, dn_norm_w, w_dn_branch, pool_w, pool_scale, w_pool_branch, w_mix_out, ln2_g, ln2_b, mem_ln_g, mem_ln_b, xa_wq, xa_wk, xa_wv, xa_wo, ln3_g, ln3_b, ffn2_w_gate, ffn2_w_up, ffn2_w_down, ln4_g, ln4_b, loss_target, m_ffn1_w_gate, m_ffn1_w_up, m_ffn1_w_down, m_ln1_g, m_ln1_b, m_w_in, m_conv_w, m_a_log, m_dt_bias, m_dn_norm_w, m_w_dn_branch, m_pool_w, m_pool_scale, m_w_pool_branch, m_w_mix_out, m_ln2_g, m_ln2_b, m_mem_ln_g, m_mem_ln_b, m_xa_wq, m_xa_wk, m_xa_wv, m_xa_wo, m_ln3_g, m_ln3_b, m_ffn2_w_gate, m_ffn2_w_up, m_ffn2_w_down, m_ln4_g, m_ln4_b, v_ffn1_w_gate, v_ffn1_w_up, v_ffn1_w_down, v_ln1_g, v_ln1_b, v_w_in, v_conv_w, v_a_log, v_dt_bias, v_dn_norm_w, v_w_dn_branch, v_pool_w, v_pool_scale, v_w_pool_branch, v_w_mix_out, v_ln2_g, v_ln2_b, v_mem_ln_g, v_mem_ln_b, v_xa_wq, v_xa_wk, v_xa_wv, v_xa_wo, v_ln3_g, v_ln3_b, v_ffn2_w_gate, v_ffn2_w_up, v_ffn2_w_down, v_ln4_g, v_ln4_b):
    given = dict(locals())
    shards = {n: given[n] for n in WEIGHT_NAMES}
    io = _Exchanges({n: shards[n][0] for n, _, _ in SHARDED})
    w = io.first_weights()
    for n in REPLICATED:
        w[n] = shards[n][0] if n == "pool_w" else shards[n]
    loss_part, grad_x, g = local_step(x[0], mem[0], loss_target[0], w, io)

    grad, updates = {}, {}

    def update(names, reduced):
        for n in names:
            if n in TRANSPOSED:
                outs = adamw("adamw_" + n, shards[n][0].T, reduced[n], given["m_" + n][0].T, given["v_" + n][0].T)
                grad[n], updates[n] = reduced[n].T[None], [o.T[None] for o in outs]
            else:
                grad[n] = reduced[n].reshape(shards[n].shape)
                updates[n] = adamw("adamw_" + n, shards[n], grad[n], given["m_" + n], given["v_" + n])
        return updates[names[-1]][0]

    update(GROUPS["ffn2"] + GROUPS["xa"], io.reduced)
    io.grads_in("mixer", grad_x)
    done = update(GROUPS["mixer"], io.reduced)
    small = io.small_in(done)
    loss = small.pop("loss")[0]
    done = update(REPLICATED, small)
    for n in ("ffn1_w_down", "ffn1_w_gate", "ffn1_w_up"):
        io.grads_in(n, done)
        done = update(GROUPS[n], io.reduced)
    return (loss, grad_x[None], *[grad[n] for n in WEIGHT_NAMES], *[updates[n][0] for n in WEIGHT_NAMES],
            *[updates[n][1] for n in WEIGHT_NAMES], *[updates[n][2] for n in WEIGHT_NAMES])
```

```python
import jax
import jax.numpy as jnp
from jax import lax
from jax.experimental import pallas as pl
from jax.experimental.pallas import tpu as pltpu

F32 = jnp.float32
BF16 = jnp.bfloat16
MMD = BF16
WIRE = BF16
X3 =lax.Precision.HIGH
VMEM_LIMIT_BYTES = 48 * 1024 * 1024

D_MODEL = 1024
D_FF = 2816
CHUNK = 64
N_MEM = 256
DN_HEADS = 4
HD = 128
DN_WIDTH = 512
POOL_WINDOWS = (2, 4, 8, 16)
POOL_WIDTH = 512
XA_HEADS = 4
XA_HD = 256
LN_EPS = 1e-5
RMS_EPS = 1e-6
L2_EPS = 1e-6
ALPHA = 2.0 ** 0.25
HALO = 16
ROWS = 512
ROWS_LN = 1024
ROWS_WIDE = 256

ADAM_LR = 0.001
ADAM_B1 = 0.9
ADAM_B2 = 0.999
ADAM_EPS = 1e-08
ADAM_WD = 0.01
ADAM_STEP = 10

N_DEV = 8
LANES = 1024
ANY = pl.BlockSpec(memory_space=pl.ANY)


def _dot(a, b, ca, cb, prec):
    dn = (((ca,), (cb,)), ((), ()))
    if prec is not None:
        return lax.dot_general(a.astype(F32), b.astype(F32), dn, precision=prec, preferred_element_type=F32)
    return lax.dot_general(a.astype(MMD), b.astype(MMD), dn, preferred_element_type=F32)


def dnn(a, b, prec=None):
    return _dot(a, b, 1, 0, prec)


def dnt(a, b, prec=None):
    return _dot(a, b, 1, 1, prec)


def dtn(a, b, prec=None):
    return _dot(a, b, 0, 0, prec)


def _sigmoid(x):
    return jax.nn.sigmoid(x)


def _silu(x):
    return x * _sigmoid(x)


def _dsilu(x):
    s = _sigmoid(x)
    return s * (1.0 + x * (1.0 - s))


def _softplus(x):
    return jnp.maximum(x, 0.0) + jnp.log1p(jnp.exp(-jnp.abs(x)))


def _iota(shape, dim):
    return lax.broadcasted_iota(jnp.int32, shape, dim)


def _rsum(x):
    return jnp.sum(x, axis=1, keepdims=True)


def _csum(x):
    return jnp.sum(x, axis=0, keepdims=True)


def _pick(n, cands):
    for c in cands:
        if n % c == 0:
            return c
    return n


def _params(sem):
    return pltpu.CompilerParams(dimension_semantics=sem, vmem_limit_bytes=VMEM_LIMIT_BYTES)


MM_TILE_SIZES = (4096, 2816, 2048, 1536, 1408, 1024, 768, 512, 384, 256, 128)
MM_VMEM_BUDGET = 36 * 1024 * 1024
HBM_BYTES_PER_US = 3.0e6
GRID_STEP_US = 0.35


def _mm_tiles(m, n, kc, a_bytes, b_bytes, o_bytes):
    def sizes(d):
        return [d] if d <= 512 else [t for t in MM_TILE_SIZES if d % t == 0]

    best = None
    for tm in sizes(m):
        for tn in sizes(n):
            for tk in sizes(kc):
                vmem = 2 * (tm * tk * a_bytes + tk * tn * b_bytes + tm * tn * o_bytes) + tm * tn * 4
                if vmem > MM_VMEM_BUDGET:
                    continue
                steps = (m // tm) * (n // tn) * (kc // tk)
                traffic = m * kc * a_bytes * (n // tn) + kc * n * b_bytes * (m // tm) + m * n * o_bytes
                edge = tm * tk * a_bytes + tk * tn * b_bytes + tm * tn * o_bytes
                cost = (traffic + edge) / HBM_BYTES_PER_US + steps * GRID_STEP_US
                if best is None or cost < best[0]:
                    best = (cost, tm, tn, tk)
    return best[1:]


def mm(name, a, b, *, ta=False, tb=False, out_dtype=F32, add=None, scale=None, deps=()):
    adds = [] if add is None else (list(add) if isinstance(add, (list, tuple)) else [(1.0, add)])
    if ta:
        kc, m = a.shape
    else:
        m, kc = a.shape
    if tb:
        n, kb = b.shape
    else:
        kb, n = b.shape
    assert kc == kb, (name, a.shape, b.shape)
    tm, tn, tk = _mm_tiles(m, n, kc, a.dtype.itemsize, b.dtype.itemsize,
                           jnp.dtype(out_dtype).itemsize * (1 + len(adds)))
    nk = kc // tk
    grid = (m // tm, n // tn, nk)
    a_spec = pl.BlockSpec((tk, tm), lambda i, j, k: (k, i)) if ta else pl.BlockSpec((tm, tk), lambda i, j, k: (i, k))
    b_spec = pl.BlockSpec((tn, tk), lambda i, j, k: (j, k)) if tb else pl.BlockSpec((tk, tn), lambda i, j, k: (k, j))
    o_spec = pl.BlockSpec((tm, tn), lambda i, j, k: (i, j))
    ca, cb = (0 if ta else 1), (1 if tb else 0)

    def body(*refs):
        a_ref, b_ref = refs[0], refs[1]
        o_ref = refs[-1] if nk == 1 else refs[-2]
        k = pl.program_id(2)
        part = _dot(a_ref[...], b_ref[...], ca, cb, None)

        def finish(r):
            if scale is not None:
                r = r * scale
            for (coef, _), add_ref in zip(adds, refs[2:2 + len(adds)]):
                r = r + (add_ref[...] if coef == 1.0 else coef * add_ref[...])
            o_ref[...] = r.astype(o_ref.dtype)

        if nk == 1:
            finish(part)
            return
        acc_ref = refs[-1]

        @pl.when(k == 0)
        def _():
            acc_ref[...] = part

        if nk > 2:
            @pl.when((k > 0) & (k < nk - 1))
            def _():
                acc_ref[...] += part

        @pl.when(k == nk - 1)
        def _():
            finish(acc_ref[...] + part)

    ins = [a, b] + [t for _, t in adds] + list(deps)
    specs = [a_spec, b_spec] + [o_spec] * len(adds) + [ANY] * len(deps)
    return pl.pallas_call(
        body, name=name, grid=grid, in_specs=specs, out_specs=o_spec,
        out_shape=jax.ShapeDtypeStruct((m, n), out_dtype),
        scratch_shapes=[pltpu.VMEM((tm, tn), F32)] if nk > 1 else [],
        compiler_params=_params(("parallel", "parallel", "arbitrary")),
    )(*ins)


def mm_fan_t(name, lefts, b):
    kc, n = b.shape
    tk = min(512, kc)
    nk = kc // tk

    def body(*refs):
        k = pl.program_id(0)
        bb = refs[len(lefts)][...].astype(MMD)
        for a_ref, o_ref in zip(refs[:len(lefts)], refs[len(lefts) + 1:]):
            part = dtn(a_ref[...], bb)

            @pl.when(k == 0)
            def _(o_ref=o_ref, part=part):
                o_ref[...] = part

            @pl.when(k > 0)
            def _(o_ref=o_ref, part=part):
                o_ref[...] += part

    return pl.pallas_call(
        body, name=name, grid=(nk,),
        in_specs=[pl.BlockSpec((tk, a.shape[1]), lambda k: (k, 0)) for a in lefts] + [pl.BlockSpec((tk, n), lambda k: (k, 0))],
        out_specs=[pl.BlockSpec((a.shape[1], n), lambda k: (0, 0)) for a in lefts],
        out_shape=[jax.ShapeDtypeStruct((a.shape[1], n), F32) for a in lefts],
        compiler_params=_params(("arbitrary",)),
    )(*lefts, b)


def mm_sum(name, pairs, deps=()):
    m, n = pairs[0][0].shape[0], pairs[0][1].shape[1]
    tm = min(512, m)
    np_ = len(pairs)

    def body(*refs):
        acc = dnn(refs[0][...], refs[1][...])
        for p in range(1, np_):
            acc = acc + dnn(refs[2 * p][...], refs[2 * p + 1][...])
        refs[-1][...] = acc

    specs, ins = [], []
    for a, b in pairs:
        specs += [pl.BlockSpec((tm, a.shape[1]), lambda i: (i, 0)), pl.BlockSpec(b.shape, lambda i: (0, 0))]
        ins += [a, b]
    return pl.pallas_call(
        body, name=name, grid=(m // tm,), in_specs=specs + [ANY] * len(deps),
        out_specs=pl.BlockSpec((tm, n), lambda i: (i, 0)), out_shape=jax.ShapeDtypeStruct((m, n), F32),
        compiler_params=_params(("parallel",)),
    )(*ins, *deps)


class _Ctx:
    def __init__(self, i, nblk, tl):
        self.i, self.nblk, self.tl = i, nblk, tl


def _norm_item(it):
    if isinstance(it, tuple):
        a, w, j = it[:3]
        rows = it[3] if len(it) > 3 else None
        return a, w, j, rows
    return it, it.shape[-1], 0, None


def rowwise(name, fn, length, tl, *, rows=(), consts=(), prevs=(), nexts=(), out_rows=(), out_accs=(), deps=()):
    nblk = length // tl
    hb = tl // HALO
    nhalo = length // HALO
    arrays, specs = [], []
    for it in rows:
        a, w, j, r = _norm_item(it)
        if a.ndim == 3:
            specs.append(pl.BlockSpec((a.shape[0], tl, w), lambda i, j=j: (0, i, j)))
        else:
            specs.append(pl.BlockSpec((r or tl, w), lambda i, j=j: (i, j)))
        arrays.append(a)
    for a in consts:
        specs.append(pl.BlockSpec(a.shape, lambda i, nd=a.ndim: (0,) * nd))
        arrays.append(a)
    for it in prevs:
        a, w, j, _ = _norm_item(it)
        specs.append(pl.BlockSpec((HALO, w), lambda i, j=j: (jnp.maximum(i * hb - 1, 0), j)))
        arrays.append(a)
    for it in nexts:
        a, w, j, _ = _norm_item(it)
        specs.append(pl.BlockSpec((HALO, w), lambda i, j=j: (jnp.minimum((i + 1) * hb, nhalo - 1), j)))
        arrays.append(a)
    out_shape, out_specs = [], []
    for spec in out_rows:
        if len(spec) == 3:
            h, w, dt = spec
            out_shape.append(jax.ShapeDtypeStruct((h, length, w), dt))
            out_specs.append(pl.BlockSpec((h, tl, w), lambda i: (0, i, 0)))
        else:
            w, dt = spec
            out_shape.append(jax.ShapeDtypeStruct((length, w), dt))
            out_specs.append(pl.BlockSpec((tl, w), lambda i: (i, 0)))
    for shape, dt in out_accs:
        out_shape.append(jax.ShapeDtypeStruct(shape, dt))
        out_specs.append(pl.BlockSpec(shape, lambda i, nd=len(shape): (0,) * nd))
    n_r, n_c, n_p, n_n = len(rows), len(consts), len(prevs), len(nexts)
    n_in = n_r + n_c + n_p + n_n
    n_or = len(out_rows)
    arrays, specs = arrays + list(deps), specs + [ANY] * len(deps)

    def body(*refs):
        i = pl.program_id(0)
        vals = [r[...] for r in refs[:n_in]]
        outs = refs[n_in + len(deps):]
        ctx = _Ctx(i, nblk, tl)
        ro, ao = fn(ctx, vals[:n_r], vals[n_r:n_r + n_c], vals[n_r + n_c:n_r + n_c + n_p], vals[n_r + n_c + n_p:])
        for r, v in zip(outs[:n_or], ro, strict=True):
            r[...] = v.astype(r.dtype)
        for r, v in zip(outs[n_or:], ao, strict=True):
            @pl.when(i == 0)
            def _(r=r, v=v):
                r[...] = v.astype(r.dtype)

            @pl.when(i > 0)
            def _(r=r, v=v):
                r[...] += v.astype(r.dtype)

    res = pl.pallas_call(
        body, name=name, grid=(nblk,), in_specs=specs, out_specs=out_specs, out_shape=out_shape,
        compiler_params=_params(("arbitrary",) if out_accs else ("parallel",)),
    )(*arrays)
    return res


def _heads(x, n, w):
    return [x[:, h * w:(h + 1) * w] for h in range(n)]


def _cat(xs):
    return jnp.concatenate(xs, axis=1)


def _row_index(ctx, nrows, offset=0):
    return ctx.i * ctx.tl + offset + _iota((nrows, 1), 0)


def _ln_stats(r):
    mu = jnp.mean(r, axis=1, keepdims=True)
    d = r - mu
    var = jnp.mean(d * d, axis=1, keepdims=True)
    rstd = lax.rsqrt(var + LN_EPS)
    return d * rstd, rstd


def ln_fwd(name, terms, g, b, tl=ROWS_LN, deps=()):
    coefs = [c for c, _ in terms]
    length = terms[0][1].shape[0]

    def fn(ctx, rows, consts, prevs, nexts):
        r = sum(c * t for c, t in zip(coefs, rows))
        xh, _ = _ln_stats(r)
        return [xh * consts[0] + consts[1], r], []

    return rowwise(name, fn, length, min(tl, length), rows=[t for _, t in terms], consts=[g, b],
                   out_rows=[(D_MODEL, F32), (D_MODEL, F32)], deps=deps)


def ln_bwd(name, r, terms, g, tl=ROWS_LN, deps=()):
    coefs = [c for c, _ in terms]
    length = r.shape[0]

    def fn(ctx, rows, consts, prevs, nexts):
        xh, rstd = _ln_stats(rows[0])
        dy = sum(c * t for c, t in zip(coefs, rows[1:]))
        dxh = dy * consts[0]
        dr = rstd * (dxh - jnp.mean(dxh, axis=1, keepdims=True) - xh * jnp.mean(dxh * xh, axis=1, keepdims=True))
        return [dr], [_csum(dy * xh), _csum(dy)]

    return rowwise(name, fn, length, min(tl, length), rows=[r] + [t for _, t in terms], consts=[g],
                   out_rows=[(D_MODEL, F32)], out_accs=[((1, D_MODEL), F32), ((1, D_MODEL), F32)], deps=deps)


def ln_loss(name, terms, g, b, target, tl=ROWS_LN):
    coefs = [c for c, _ in terms]
    length = target.shape[0]
    nt = len(terms)

    def fn(ctx, rows, consts, prevs, nexts):
        r = sum(c * t for c, t in zip(coefs, rows[:nt]))
        xh, _ = _ln_stats(r)
        err = xh * consts[0] + consts[1] - rows[nt]
        tot = _csum(_rsum(err * err)) * (0.5 / D_MODEL)
        return [err * (1.0 / D_MODEL), r], [jnp.broadcast_to(tot, (1, 128))]

    return rowwise(name, fn, length, min(tl, length), rows=[t for _, t in terms] + [target], consts=[g, b],
                   out_rows=[(D_MODEL, F32), (D_MODEL, F32)], out_accs=[((1, 128), F32)])


def _ffn_blocks(length):
    return min(512, length), D_FF // 2


def ffn_gate_up_act(name, x, wg, wu, deps=()):
    length = x.shape[0]
    tm, tn = _ffn_blocks(length)

    def body(x_ref, wg_ref, wu_ref, *rest):
        hg_ref, hu_ref, act_ref = rest[-3:]
        xb = x_ref[...].astype(MMD)
        hg = dnt(xb, wg_ref[...])
        hu = dnt(xb, wu_ref[...])
        hg_ref[...] = hg
        hu_ref[...] = hu
        act_ref[...] = (_silu(hg) * hu).astype(act_ref.dtype)

    row = pl.BlockSpec((tm, D_MODEL), lambda i, j: (i, 0))
    wsp = pl.BlockSpec((tn, D_MODEL), lambda i, j: (j, 0))
    osp = pl.BlockSpec((tm, tn), lambda i, j: (i, j))
    return pl.pallas_call(
        body, name=name, grid=(length // tm, D_FF // tn), in_specs=[row, wsp, wsp] + [ANY] * len(deps),
        out_specs=[osp] * 3,
        out_shape=[jax.ShapeDtypeStruct((length, D_FF), F32)] * 2 + [jax.ShapeDtypeStruct((length, D_FF), BF16)],
        compiler_params=_params(("parallel", "parallel")),
    )(x, wg, wu, *deps)


def ffn_dact(name, dr, wd, hg, hu, deps=()):
    length = dr.shape[0]
    tm, tn = _ffn_blocks(length)

    def body(dr_ref, wd_ref, hg_ref, hu_ref, *rest):
        dhg_ref, dhu_ref = rest[-2:]
        da = 0.5 * dnt(dr_ref[...], wd_ref[...])
        g = hg_ref[...]
        s = _sigmoid(g)
        dhg_ref[...] = (da * hu_ref[...] * (s * (1.0 + g * (1.0 - s)))).astype(dhg_ref.dtype)
        dhu_ref[...] = (da * (g * s)).astype(dhu_ref.dtype)

    row = pl.BlockSpec((tm, D_MODEL), lambda i, j: (i, 0))
    wsp = pl.BlockSpec((tn, D_MODEL), lambda i, j: (j, 0))
    osp = pl.BlockSpec((tm, tn), lambda i, j: (i, j))
    return pl.pallas_call(
        body, name=name, grid=(length // tm, D_FF // tn), in_specs=[row, wsp, osp, osp] + [ANY] * len(deps),
        out_specs=[osp] * 2, out_shape=[jax.ShapeDtypeStruct((length, D_FF), BF16)] * 2,
        compiler_params=_params(("parallel", "parallel")),
    )(dr, wd, hg, hu, *deps)


def ffn_fwd(tag, x, wg, wu, wd, deps=()):
    hg, hu, act = ffn_gate_up_act(tag + "_gate_up", x, wg, wu, deps)
    if callable(wd):
        wd = wd(act)
    f = mm(tag + "_down", act, wd)
    return f, (hg, hu, act), wd


def ffn_bwd(tag, x, res, dr, wg, wu, wd, deps=(), on_dw=None, also=None):
    on_dw = on_dw or (lambda which, dw: ())
    hg, hu, act = res
    dwd = mm(tag + "_dwd", act, dr, ta=True, scale=0.5, deps=deps)
    dhg, dhu = ffn_dact(tag + "_dact", dr, wd, hg, hu, deps=on_dw("down", dwd))
    dwg = mm(tag + "_dwg", dhg, x, ta=True)
    dwu = mm(tag + "_dwu", dhu, x, ta=True, deps=on_dw("gate", dwg))
    dx = mm(tag + "_dxg", dhg, wg, deps=on_dw("up", dwu))
    dx = mm(tag + "_dxu", dhu, wu, add=[(1.0, dx)] + ([also] if also else []))
    return dx, dwg, dwu, dwd


def _conv_taps(ext, taps, n):
    out = taps[3] * ext
    for j in range(3):
        out = out + taps[j] * pltpu.roll(ext, 3 - j, 0)
    return out


def _l2n(x):
    r = lax.rsqrt(_rsum(x * x) + L2_EPS)
    return x * r, r


def conv_fwd(name, pre, taps, tl=ROWS_WIDE, deps=()):
    length = pre.shape[0]
    tl = min(tl, length)

    def fn(ctx, rows, consts, prevs, nexts):
        prev = jnp.where(ctx.i > 0, prevs[0], 0.0)
        ext = jnp.concatenate([prev, rows[0]], axis=0)
        s = _silu(_conv_taps(ext, consts, tl + HALO)[HALO:])
        q = _cat([_l2n(x)[0] * (HD ** -0.5) for x in _heads(s[:, :DN_WIDTH], DN_HEADS, HD)])
        k = _cat([_l2n(x)[0] for x in _heads(s[:, DN_WIDTH:2 * DN_WIDTH], DN_HEADS, HD)])
        return [q, k, s[:, 2 * DN_WIDTH:]], []

    return rowwise(name, fn, length, tl, rows=[pre], consts=list(taps), prevs=[pre],
                   out_rows=[(DN_WIDTH, F32)] * 3, deps=deps)


def conv_bwd(name, pre, dq, dk, dv, taps, tl=ROWS_WIDE):
    length = pre.shape[0]
    tl = min(tl, length)
    n = tl + 2 * HALO

    def fn(ctx, rows, consts, prevs, nexts):
        last = ctx.i == ctx.nblk - 1
        prev = jnp.where(ctx.i > 0, prevs[0], 0.0)
        ext = jnp.concatenate([prev, rows[0], nexts[0]], axis=0)
        c = _conv_taps(ext, consts, n)
        sg = _sigmoid(c)
        s = c * sg
        zero = jnp.zeros((HALO, DN_WIDTH), F32)
        dqe, dke, dve = [jnp.concatenate([zero, rows[1 + t], jnp.where(last, 0.0, nexts[1 + t])], axis=0)
                         for t in range(3)]

        def l2_bwd(x, dy):
            y, r = _l2n(x)
            return r * (dy - y * _rsum(dy * y))

        dsq = _cat([l2_bwd(x, d * (HD ** -0.5)) for x, d in zip(_heads(s[:, :DN_WIDTH], DN_HEADS, HD),
                                                                 _heads(dqe, DN_HEADS, HD))])
        dsk = _cat([l2_bwd(x, d) for x, d in zip(_heads(s[:, DN_WIDTH:2 * DN_WIDTH], DN_HEADS, HD),
                                                  _heads(dke, DN_HEADS, HD))])
        dc = _cat([dsq, dsk, dve]) * (sg * (1.0 + c * (1.0 - sg)))
        dpre = consts[3] * dc
        for j in range(3):
            dpre = dpre + consts[j] * pltpu.roll(dc, n - (3 - j), 0)
        dc_cur = dc[HALO:HALO + tl]
        dws = [_csum(dc_cur * pltpu.roll(ext, 3 - j, 0)[HALO:HALO + tl]) for j in range(3)]
        dws.append(_csum(dc_cur * ext[HALO:HALO + tl]))
        return [dpre[HALO:HALO + tl]], dws

    return rowwise(name, fn, length, tl, rows=[pre, dq, dk, dv], consts=list(taps), prevs=[pre],
                   nexts=[pre, dq, dk, dv], out_rows=[(3 * DN_WIDTH, BF16)],
                   out_accs=[((1, 3 * DN_WIDTH), F32)] * 4)


def _gate_math(ab, alog, dtb):
    z = ab + dtb
    g = -jnp.exp(alog) * _softplus(z)
    beta = _sigmoid(ab)
    return z, g, beta


def gates_fwd(name, ab, alog, dtb, tl=ROWS):
    length = ab.shape[0]

    def fn(ctx, rows, consts, prevs, nexts):
        _, g, beta = _gate_math(rows[0], consts[0], consts[1])
        spread = [jnp.broadcast_to(v[:, h:h + 1], (v.shape[0], HD))
                  for v, first in ((g, 0), (beta, DN_HEADS)) for h in range(first, first + DN_HEADS)]
        return [_cat(spread[:DN_HEADS]), _cat(spread[DN_HEADS:])], []

    return rowwise(name, fn, length, min(tl, length), rows=[ab], consts=[alog, dtb],
                   out_rows=[(DN_WIDTH, F32)] * 2)


def gates_bwd(name, ab, dgb, dbb, alog, dtb, tl=ROWS):
    length = ab.shape[0]

    def fn(ctx, rows, consts, prevs, nexts):
        z, g, beta = _gate_math(rows[0], consts[0], consts[1])
        lane = _iota(g.shape, 1)
        dsmall = jnp.zeros_like(g)
        for h in range(DN_HEADS):
            dsmall = jnp.where(lane == h, rows[1][:, h * HD:h * HD + 1], dsmall)
            dsmall = jnp.where(lane == DN_HEADS + h, rows[2][:, h * HD:h * HD + 1], dsmall)
        is_a = lane < DN_HEADS
        da = jnp.where(is_a, dsmall * (-jnp.exp(consts[0])) * _sigmoid(z), 0.0)
        db = jnp.where((lane >= DN_HEADS) & (lane < 2 * DN_HEADS), dsmall * beta * (1.0 - beta), 0.0)
        return [da + db], [_csum(jnp.where(is_a, dsmall * g, 0.0)), _csum(da)]

    return rowwise(name, fn, length, min(tl, length), rows=[ab, dgb, dbb], consts=[alog, dtb],
                   out_rows=[(128, BF16)], out_accs=[((1, 128), F32)] * 2)


CPS = 4


def _chunk_scan_rows(x, suffix=False):
    n = x.shape[0]
    rc = _iota(x.shape, 0) & (CHUNK - 1)
    sh = 1
    while sh < CHUNK:
        if suffix:
            x = x + jnp.where(rc < CHUNK - sh, pltpu.roll(x, n - sh, 0), 0.0)
        else:
            x = x + jnp.where(rc >= sh, pltpu.roll(x, sh, 0), 0.0)
        sh *= 2
    return x


def _tri_inv(a_list, eye, bd):
    def each(f, *ls):
        return [f(*xs) for xs in zip(*ls)]

    dg = [jnp.where(bd, a, 0.0) for a in a_list]
    lo = each(lambda a, d: a - d, a_list, dg)
    n1 = [-d for d in dg]
    n2 = each(lambda n: dnn(n, n, X3), n1)
    n4 = each(lambda n: dnn(n, n, X3), n2)
    td = each(lambda p, s: dnn(eye + p, eye + s, X3), n1, n2)
    n8 = each(lambda n: dnn(n, n, X3), n4)
    td = each(lambda t, n: dnn(t, eye + n, X3), td, n4)
    td = each(lambda t, n: dnn(t, eye + n, X3), td, n8)
    m = each(lambda t, l: dnn(t, l, X3), td, lo)
    m2 = each(lambda x: dnn(x, x, X3), m)
    x = each(lambda p, s: dnn(eye - p, eye + s, X3), m, m2)
    return each(lambda p, t: dnn(p, t, X3), x, td)


def _chunk_common(q, k, v, gcb, bb):
    egb = jnp.exp(gcb)
    gc64 = gcb[:, :CHUNK]
    ii, jj = _iota((CHUNK, CHUNK), 0), _iota((CHUNK, CHUNK), 1)
    incl, strict = ii >= jj, ii > jj
    decay = jnp.exp(jnp.where(incl, gc64 - gc64.T, -jnp.inf))
    kb = k * bb
    vb = v * bb
    kbe = kb * egb
    pq = dnt(jnp.concatenate([kb, q], axis=0), k, X3)
    ekb = jnp.exp(gcb[CHUNK - 1:CHUNK, :] - gcb)
    return dict(egb=egb, decay=decay, kb=kb, vb=vb, kbe=kbe, pm=pq[:CHUNK], qm=pq[CHUNK:], ekb=ekb,
                incl=incl, strict=strict, ii=ii, jj=jj)


def _chunk_head(vals, ci, h):
    return [v[ci * CHUNK:(ci + 1) * CHUNK, h * HD:(h + 1) * HD] for v in vals]


def _assemble(per_chunk):
    return jnp.concatenate([_cat(hs) for hs in per_chunk], axis=0)


def _assemble3(per_chunk):
    return jnp.stack([jnp.concatenate([per_chunk[ci][h] for ci in range(CPS)], axis=0) for h in range(DN_HEADS)])


def delta_prep_fwd(name, q, k, v, gb, bb):
    length = q.shape[0]

    def fn(ctx, rows, consts, prevs, nexts):
        gcb_all = _chunk_scan_rows(rows[3])
        vals = [rows[0], rows[1], rows[2], gcb_all, rows[4]]
        units = [(ci, h) for ci in range(CPS) for h in range(DN_HEADS)]
        ins = [_chunk_head(vals, ci, h) for ci, h in units]
        cs = [_chunk_common(*i) for i in ins]
        eye = (cs[0]["ii"] == cs[0]["jj"]).astype(F32)
        ts = _tri_inv([jnp.where(c["strict"], c["pm"] * c["decay"], 0.0) for c in cs], eye,
                      (cs[0]["ii"] >> 4) == (cs[0]["jj"] >> 4))
        uws = [dnn(t, _cat([c["vb"], c["kbe"]]), X3) for t, c in zip(ts, cs)]

        def grid2(xs):
            return [xs[ci * DN_HEADS:(ci + 1) * DN_HEADS] for ci in range(CPS)]

        return [_assemble(grid2([uw[:, :HD] for uw in uws])), _assemble(grid2([uw[:, HD:] for uw in uws])),
                _assemble(grid2([i[0] * c["egb"] for i, c in zip(ins, cs)])),
                _assemble(grid2([i[1] * c["ekb"] for i, c in zip(ins, cs)])), gcb_all,
                _assemble3(grid2([c["qm"] * c["decay"] for c in cs])), _assemble3(grid2(ts))], []

    return rowwise(name, fn, length, CHUNK * CPS, rows=[q, k, v, gb, bb],
                   out_rows=[(DN_WIDTH, F32)] * 5 + [(DN_HEADS, CHUNK, F32)] * 2)


def delta_prep_bwd(name, q, k, v, gb, bb, t3, du, dw, dqd, dkd, dattn3, dgl):
    length = q.shape[0]

    def fn(ctx, rows, consts, prevs, nexts):
        gcb_all = _chunk_scan_rows(rows[3])
        vals = [rows[0], rows[1], rows[2], gcb_all] + list(rows[4:9])
        t3v, da3v, dglv = rows[9], rows[10], rows[11]
        units = [(ci, h) for ci in range(CPS) for h in range(DN_HEADS)]
        ins = [_chunk_head(vals, ci, h) for ci, h in units]
        cs = [_chunk_common(*i[:5]) for i in ins]
        ts = [t3v[h][ci * CHUNK:(ci + 1) * CHUNK] for ci, h in units]
        dattns = [jnp.where(c["incl"], da3v[h][ci * CHUNK:(ci + 1) * CHUNK], 0.0) for (ci, h), c in zip(units, cs)]
        duws = [_cat([i[5], i[6]]) for i in ins]
        dvks = [dtn(t, d, X3) for t, d in zip(ts, duws)]
        dts = [dnt(d, _cat([c["vb"], c["kbe"]]), X3) for d, c in zip(duws, cs)]
        dts = [dnt(d, t, X3) for d, t in zip(dts, ts)]
        das = [jnp.where(c["strict"], -dtn(t, d, X3), 0.0) for c, t, d in zip(cs, ts, dts)]
        dpqs = [jnp.concatenate([da * c["decay"], dat * c["decay"]], axis=0) for da, dat, c in zip(das, dattns, cs)]
        dpqks = [dnn(d, i[1], X3) for d, i in zip(dpqs, ins)]
        dkps = [dtn(d, jnp.concatenate([c["kb"], i[0]], axis=0), X3) for d, c, i in zip(dpqs, cs, ins)]
        dqs, dks, dvs, dgcs, dbs = [], [], [], [], []
        for (ci, h), i, c, dvk, da, dattn, dpqk, dkp in zip(units, ins, cs, dvks, das, dattns, dpqks, dkps):
            qh, kh, vh, _, bh, _, _, dqdh, dkdh = i
            dvb, dkbe = dvk[:, :HD], dvk[:, HD:]
            dkb = dpqk[:CHUNK] + dkbe * c["egb"]
            c1 = _rsum(dkbe * c["kb"] + dqdh * qh) * c["egb"]
            c2 = _rsum(dkdh * kh) * c["ekb"]
            e = (da * c["pm"] + dattn * c["qm"]) * c["decay"]
            dgc = c1 - c2 + _rsum(e) - _rsum(e.T)
            dgl_tot = jnp.max(dglv[ci * 8:(ci + 1) * 8, h * HD:(h + 1) * HD], axis=0, keepdims=True) + _csum(c2)
            dgcs.append(dgc + jnp.where(_iota((CHUNK, HD), 0) == CHUNK - 1, dgl_tot, 0.0))
            dqs.append(dpqk[CHUNK:] + dqdh * c["egb"])
            dks.append(dkp + dkdh * c["ekb"] + dkb * bh)
            dvs.append(dvb * bh)
            dbs.append(jnp.broadcast_to(_rsum(dkb * kh + dvb * vh), (CHUNK, HD)))

        def grid2(xs):
            return [xs[ci * DN_HEADS:(ci + 1) * DN_HEADS] for ci in range(CPS)]

        return [_assemble(grid2(dqs)), _assemble(grid2(dks)), _assemble(grid2(dvs)),
                _chunk_scan_rows(_assemble(grid2(dgcs)), suffix=True), _assemble(grid2(dbs))], []

    return rowwise(name, fn, length, CHUNK * CPS,
                   rows=[q, k, v, gb, bb, du, dw, dqd, dkd, t3, dattn3, (dgl, DN_WIDTH, 0, 8 * CPS)],
                   out_rows=[(DN_WIDTH, F32)] * 5)


SCAN_CHUNKS = 8


def _scan_chunks(n):
    return SCAN_CHUNKS if n % SCAN_CHUNKS == 0 else 1


def delta_scan_fwd(name, qd, kd, u, w, attn3, gcb):
    length = qd.shape[0]
    n = length // CHUNK
    sc = _scan_chunks(n)
    row = pl.BlockSpec((sc * CHUNK, DN_WIDTH), lambda c: (c, 0))
    sq = pl.BlockSpec((DN_HEADS, sc * CHUNK, CHUNK), lambda c: (0, c, 0))

    def body(qd_ref, kd_ref, u_ref, w_ref, attn_ref, gc_ref, o_ref, vn_ref, st_ref, s_ref):
        c = pl.program_id(0)

        @pl.when(c == 0)
        def _():
            s_ref[...] = jnp.zeros_like(s_ref)

        heads = range(DN_HEADS)
        sls = [pl.ds(h * HD, HD) for h in heads]
        ss = [s_ref[h] for h in heads]
        for ci in range(sc):
            rs = pl.ds(ci * CHUNK, CHUNK)
            ws = [dnn(w_ref[rs, sl], s) for sl, s in zip(sls, ss)]
            qs = [dnn(qd_ref[rs, sl], s) for sl, s in zip(sls, ss)]
            vns = [u_ref[rs, sl] - x for sl, x in zip(sls, ws)]
            avs = [dnn(attn_ref[h, rs, :], vn) for h, vn in zip(heads, vns)]
            kvs = [dtn(kd_ref[rs, sl], vn) for sl, vn in zip(sls, vns)]
            for h, sl in zip(heads, sls):
                st_ref[ci, h] = ss[h]
                o_ref[rs, sl] = qs[h] + avs[h]
                vn_ref[rs, sl] = vns[h]
            ss = [s * jnp.exp(gc_ref[pl.ds(ci * CHUNK + CHUNK - 1, 1), sl]) + kv for s, sl, kv in zip(ss, sls, kvs)]
        for h in heads:
            s_ref[h] = ss[h]

    return pl.pallas_call(
        body, name=name, grid=(n // sc,), in_specs=[row, row, row, row, sq, row],
        out_specs=[row, row, pl.BlockSpec((sc, DN_HEADS, HD, HD), lambda c: (c, 0, 0, 0))],
        out_shape=[jax.ShapeDtypeStruct((length, DN_WIDTH), F32), jax.ShapeDtypeStruct((length, DN_WIDTH), F32),
                   jax.ShapeDtypeStruct((n, DN_HEADS, HD, HD), F32)],
        scratch_shapes=[pltpu.VMEM((DN_HEADS, HD, HD), F32)],
        compiler_params=_params(("arbitrary",)),
    )(qd, kd, u, w, attn3, gcb)


def delta_scan_bwd(name, do, qd, kd, w, attn3, vn, st, gcb):
    length = qd.shape[0]
    n = length // CHUNK
    sc = _scan_chunks(n)
    nb = n // sc
    row = pl.BlockSpec((sc * CHUNK, DN_WIDTH), lambda c: (nb - 1 - c, 0))
    sq = pl.BlockSpec((DN_HEADS, sc * CHUNK, CHUNK), lambda c: (0, nb - 1 - c, 0))
    stb = pl.BlockSpec((sc, DN_HEADS, HD, HD), lambda c: (nb - 1 - c, 0, 0, 0))
    glb = pl.BlockSpec((sc * 8, DN_WIDTH), lambda c: (nb - 1 - c, 0))

    def body(do_ref, qd_ref, kd_ref, w_ref, attn_ref, vn_ref, st_ref, gc_ref,
             dqd_ref, dkd_ref, du_ref, dw_ref, dattn_ref, dgl_ref, ds_ref):
        c = pl.program_id(0)

        @pl.when(c == 0)
        def _():
            ds_ref[...] = jnp.zeros_like(ds_ref)

        heads = range(DN_HEADS)
        sls = [pl.ds(h * HD, HD) for h in heads]
        dsns = [ds_ref[h] for h in heads]
        for ci in reversed(range(sc)):
            rs = pl.ds(ci * CHUNK, CHUNK)
            ss = [st_ref[ci, h] for h in heads]
            dos = [do_ref[rs, sl] for sl in sls]
            vns = [vn_ref[rs, sl] for sl in sls]
            dvns = [dtn(attn_ref[h, rs, :], d) for h, d in zip(heads, dos)]
            dvns = [x + dnn(kd_ref[rs, sl], dsn) for x, sl, dsn in zip(dvns, sls, dsns)]
            qdos = [dtn(qd_ref[rs, sl], d) for sl, d in zip(sls, dos)]
            for h, sl in zip(heads, sls):
                dattn_ref[h, rs, :] = dnt(dos[h], vns[h])
                dqd_ref[rs, sl] = dnt(dos[h], ss[h])
                dkd_ref[rs, sl] = dnt(vns[h], dsns[h])
                du_ref[rs, sl] = dvns[h]
            dws = [dnt(dvn, s) for dvn, s in zip(dvns, ss)]
            wdvs = [dtn(w_ref[rs, sl], dvn) for sl, dvn in zip(sls, dvns)]
            nxt = []
            for h, sl in zip(heads, sls):
                egl = jnp.exp(gc_ref[pl.ds(ci * CHUNK + CHUNK - 1, 1), sl])
                dw_ref[rs, sl] = -dws[h]
                dgl_ref[pl.ds(ci * 8, 8), sl] = jnp.broadcast_to(_csum(_rsum(dsns[h] * ss[h])) * egl, (8, HD))
                nxt.append(dsns[h] * egl + qdos[h] - wdvs[h])
            dsns = nxt
        for h in heads:
            ds_ref[h] = dsns[h]

    return pl.pallas_call(
        body, name=name, grid=(nb,), in_specs=[row, row, row, row, sq, row, stb, row],
        out_specs=[row, row, row, row, sq, glb],
        out_shape=[jax.ShapeDtypeStruct((length, DN_WIDTH), F32)] * 4
        + [jax.ShapeDtypeStruct((DN_HEADS, length, CHUNK), F32), jax.ShapeDtypeStruct((n * 8, DN_WIDTH), F32)],
        scratch_shapes=[pltpu.VMEM((DN_HEADS, HD, HD), F32)],
        compiler_params=_params(("arbitrary",)),
    )(do, qd, kd, w, attn3, vn, st, gcb)


def onorm_fwd(name, o, z, nw, tl=ROWS):
    length = o.shape[0]

    def fn(ctx, rows, consts, prevs, nexts):
        outs = []
        for oh, zh in zip(_heads(rows[0], DN_HEADS, HD), _heads(rows[1], DN_HEADS, HD)):
            r = lax.rsqrt(jnp.mean(oh * oh, axis=1, keepdims=True) + RMS_EPS)
            outs.append(oh * r * consts[0] * _silu(zh))
        return [_cat(outs)], []

    return rowwise(name, fn, length, min(tl, length), rows=[o, z], consts=[nw], out_rows=[(DN_WIDTH, BF16)])[0]


def onorm_bwd(name, o, z, d_on, nw, tl=ROWS):
    length = o.shape[0]

    def fn(ctx, rows, consts, prevs, nexts):
        dos, dzs = [], []
        dnw = jnp.zeros((1, HD), F32)
        for oh, zh, dh in zip(*[_heads(r, DN_HEADS, HD) for r in rows]):
            r = lax.rsqrt(jnp.mean(oh * oh, axis=1, keepdims=True) + RMS_EPS)
            y = oh * r
            sz = _silu(zh)
            t = dh * sz * consts[0]
            dos.append(r * (t - y * jnp.mean(t * y, axis=1, keepdims=True)))
            dzs.append(dh * y * consts[0] * _dsilu(zh))
            dnw = dnw + _csum(dh * y * sz)
        return [_cat(dos), _cat(dzs)], [dnw]

    return rowwise(name, fn, length, min(tl, length), rows=[o, z, d_on], consts=[nw],
                   out_rows=[(DN_WIDTH, F32), (DN_WIDTH, BF16)], out_accs=[((1, HD), F32)])


def merge_fwd(name, gates, ydn, ypool, tl=ROWS):
    length = ydn.shape[0]

    def fn(ctx, rows, consts, prevs, nexts):
        gt = rows[0]
        return [_sigmoid(gt[:, :D_MODEL]) * rows[1] + _sigmoid(gt[:, D_MODEL:]) * rows[2]], []

    return rowwise(name, fn, length, min(tl, length), rows=[gates, ydn, ypool], out_rows=[(D_MODEL, BF16)])[0]


def merge_bwd(name, gates, ydn, ypool, dm, tl=ROWS_WIDE):
    length = ydn.shape[0]

    def fn(ctx, rows, consts, prevs, nexts):
        gt, yd, yp, d = rows
        sd, sp = _sigmoid(gt[:, :D_MODEL]), _sigmoid(gt[:, D_MODEL:])
        dgates = _cat([d * yd * sd * (1.0 - sd), d * yp * sp * (1.0 - sp)])
        return [d * sd, d * sp, dgates], []

    return rowwise(name, fn, length, min(tl, length), rows=[gates, ydn, ypool, dm],
                   out_rows=[(D_MODEL, BF16), (D_MODEL, BF16), (2 * D_MODEL, BF16)])


def _trailing_sums(ext, upto):
    s, sh = ext, 1
    while sh < upto:
        s = s + pltpu.roll(s, sh, 0)
        sh *= 2
    return s


def _leading_sums(ext, upto, n):
    s, sh = ext, 1
    while sh < upto:
        s = s + pltpu.roll(s, n - sh, 0)
        sh *= 2
    return s


def _pool_mixed(ctx, p, prev, tl):
    prevm = jnp.where(ctx.i > 0, prev, 0.0)
    t1 = (_row_index(ctx, tl) + 1).astype(F32)
    outs = []
    for gi, win in enumerate(POOL_WINDOWS):
        sl = slice(gi * HD, (gi + 1) * HD)
        ext = jnp.concatenate([prevm[:, sl], p[:, sl]], axis=0)
        mean = _trailing_sums(ext, win)[HALO:] / jnp.minimum(t1, float(win))
        outs.append(mean - p[:, sl])
    return outs


def pool_fwd(name, p, pool_w, scale, tl=ROWS):
    length = p.shape[0]
    tl = min(tl, length)

    def fn(ctx, rows, consts, prevs, nexts):
        mixed = _pool_mixed(ctx, rows[0], prevs[0], tl)
        y = _cat([dnn(m, consts[0][gi]) for gi, m in enumerate(mixed)])
        return [y * consts[1]], []

    return rowwise(name, fn, length, tl, rows=[p], consts=[pool_w, scale], prevs=[p],
                   out_rows=[(POOL_WIDTH, BF16)])[0]


def pool_bwd(name, p, dpo, pool_w, scale, tl=ROWS):
    length = p.shape[0]
    tl = min(tl, length)
    n = tl + HALO

    def fn(ctx, rows, consts, prevs, nexts):
        last = ctx.i == ctx.nblk - 1
        mixed = _pool_mixed(ctx, rows[0], prevs[0], tl)
        dext = jnp.concatenate([rows[1], jnp.where(last, 0.0, nexts[0])], axis=0)
        t1 = (_row_index(ctx, n) + 1).astype(F32)
        dps, dws, dscs = [], [], []
        for gi, win in enumerate(POOL_WINDOWS):
            sl = slice(gi * HD, (gi + 1) * HD)
            wg = consts[0][gi]
            dyraw = dext[:, sl] * consts[1][:, sl]
            dmix = dnt(dyraw, wg)
            dws.append(dtn(mixed[gi], dyraw[:tl]))
            dscs.append(_csum(rows[1][:, sl] * dnn(mixed[gi], wg)))
            lead = _leading_sums(dmix / jnp.minimum(t1, float(win)), win, n)
            dps.append(lead[:tl] - dmix[:tl])
        return [_cat(dps)], [jnp.stack(dws), _cat(dscs)]

    return rowwise(name, fn, length, tl, rows=[p, dpo], consts=[pool_w, scale], prevs=[p], nexts=[dpo],
                   out_rows=[(POOL_WIDTH, BF16)],
                   out_accs=[((len(POOL_WINDOWS), HD, HD), F32), ((1, POOL_WIDTH), F32)])


def _xa_probs(qh, kh):
    s = dnt(qh, kh) * (XA_HD ** -0.5)
    e = jnp.exp(s - jnp.max(s, axis=1, keepdims=True))
    return e / _rsum(e)


def xattn_fwd(name, qx, kx, vx, tl=ROWS):
    length = qx.shape[0]

    def fn(ctx, rows, consts, prevs, nexts):
        outs = [dnn(_xa_probs(qh, kh), vh) for qh, kh, vh in
                zip(_heads(rows[0], XA_HEADS, XA_HD), _heads(consts[0], XA_HEADS, XA_HD),
                    _heads(consts[1], XA_HEADS, XA_HD))]
        return [_cat(outs)], []

    return rowwise(name, fn, length, min(tl, length), rows=[qx], consts=[kx, vx], out_rows=[(D_MODEL, BF16)])[0]


def xattn_bwd(name, qx, dox, kx, vx, tl=ROWS):
    length = qx.shape[0]

    def fn(ctx, rows, consts, prevs, nexts):
        dqs, dks, dvs = [], [], []
        for qh, dh, kh, vh in zip(_heads(rows[0], XA_HEADS, XA_HD), _heads(rows[1], XA_HEADS, XA_HD),
                                  _heads(consts[0], XA_HEADS, XA_HD), _heads(consts[1], XA_HEADS, XA_HD)):
            pr = _xa_probs(qh, kh)
            dpr = dnt(dh, vh)
            ds = pr * (dpr - _rsum(dpr * pr)) * (XA_HD ** -0.5)
            dqs.append(dnn(ds, kh))
            dks.append(dtn(ds, qh))
            dvs.append(dtn(pr, dh))
        return [_cat(dqs)], [_cat(dks), _cat(dvs)]

    return rowwise(name, fn, length, min(tl, length), rows=[qx, dox], consts=[kx, vx],
                   out_rows=[(D_MODEL, BF16)], out_accs=[((N_MEM, D_MODEL), F32)] * 2)


def local_step(x, mem, target, w, io):
    alog = jnp.pad(w["a_log"], ((0, 0), (0, 128 - DN_HEADS)))
    dtb = jnp.pad(w["dt_bias"], ((0, 0), (0, 128 - DN_HEADS)))

    f1, res1, w_down1 = ffn_fwd("ffn1", x, w["ffn1_w_gate"], w["ffn1_w_up"], io.ffn1_down, deps=io.rest_started())
    x1, r1 = ln_fwd("ln1", [(ALPHA, x), (0.5, f1)], w["ln1_g"], w["ln1_b"], deps=io.halfway("mixer", f1))
    w = dict(w, ffn1_w_down=w_down1, **io.weights("mixer", x1))
    taps = [w["conv_w"][j:j + 1] for j in range(4)]

    pre = mm("in_qkv", x1, w["in_qkv"], tb=True)
    z = mm("in_z", x1, w["in_z"], tb=True)
    gates = mm("in_gates", x1, w["in_gates"], tb=True)
    p = mm("in_p", x1, w["in_p"], tb=True)
    ab = mm("in_ab", x1, w["in_ab"], tb=True)
    q, k, v = conv_fwd("conv", pre, taps, deps=io.halfway("xa", pre))
    gb, bb = gates_fwd("gates", ab, alog, dtb)
    u, wd_, qd, kd, gcb, attn3, t3 = delta_prep_fwd("dprep", q, k, v, gb, bb)
    o, vn, st = delta_scan_fwd("dscan", qd, kd, u, wd_, attn3, gcb)
    on = onorm_fwd("onorm", o, z, w["dn_norm_w"])
    ydn = mm("dn_branch", on, w["w_dn_branch"], tb=True)
    po = pool_fwd("pool", p, w["pool_w"], w["pool_scale"])
    ypool = mm("pool_branch", po, w["w_pool_branch"], tb=True)
    merged = merge_fwd("merge", gates, ydn, ypool)
    mix = mm("mix_out", merged, w["w_mix_out"])
    x2, r2 = ln_fwd("ln2", [(ALPHA, x1), (1.0, mix)], w["ln2_g"], w["ln2_b"])

    w = dict(w, **io.weights("xa", x2))
    m, _ = ln_fwd("ln_mem", [(1.0, mem)], w["mem_ln_g"], w["mem_ln_b"])
    qx = mm("xa_q", x2, w["xa_wq"], deps=io.halfway("ffn2", x2))
    kx = mm("xa_k", m, w["xa_wk"])
    vx = mm("xa_v", m, w["xa_wv"])
    ox = xattn_fwd("xattn", qx, kx, vx)
    xa = mm("xa_o", ox, w["xa_wo"])
    x3, r3 = ln_fwd("ln3", [(ALPHA, x2), (1.0, xa)], w["ln3_g"], w["ln3_b"])
    w = dict(w, **io.weights("ffn2", x3))

    f2, res2, _ = ffn_fwd("ffn2", x3, w["ffn2_w_gate"], w["ffn2_w_up"], w["ffn2_w_down"])
    dy4, r4, loss = ln_loss("ln4_loss", [(ALPHA, x3), (0.5, f2)], w["ln4_g"], w["ln4_b"], target)

    g = {}
    dr4, g["ln4_g"], g["ln4_b"] = ln_bwd("ln4_b", r4, [(1.0, dy4)], w["ln4_g"])
    dx3, g["ffn2_w_gate"], g["ffn2_w_up"], g["ffn2_w_down"] = ffn_bwd(
        "ffn2b", x3, res2, dr4, w["ffn2_w_gate"], w["ffn2_w_up"], w["ffn2_w_down"])
    dep = io.grads_out("ffn2", g)
    dr3, g["ln3_g"], g["ln3_b"] = ln_bwd("ln3_b", r3, [(ALPHA, dr4), (1.0, dx3)], w["ln3_g"], deps=dep)

    dox = mm("xa_do", dr3, w["xa_wo"], tb=True)
    g["xa_wo"] = mm("xa_dwo", ox, dr3, ta=True)
    dqx, dkx, dvx = xattn_bwd("xattn_b", qx, dox, kx, vx)
    g["xa_wq"] = mm("xa_dwq", x2, dqx, ta=True)
    dx2 = mm("xa_dx", dqx, w["xa_wq"], tb=True)
    g["xa_wk"] = mm("xa_dwk", m, dkx, ta=True)
    g["xa_wv"] = mm("xa_dwv", m, dvx, ta=True)
    dmm = mm("xa_dmk", dkx, w["xa_wk"], tb=True, deps=io.grads_out("xa", g))
    dmm = mm("xa_dmv", dvx, w["xa_wv"], tb=True, add=dmm)
    _, g["mem_ln_g"], g["mem_ln_b"] = ln_bwd("ln_mem_b", mem, [(1.0, dmm)], w["mem_ln_g"])
    dr2, g["ln2_g"], g["ln2_b"] = ln_bwd("ln2_b", r2, [(ALPHA, dr3), (1.0, dx2)], w["ln2_g"])
    io.grads_in("ffn2", dr2)

    dmerged = mm("mix_dm", dr2, w["w_mix_out"], tb=True)
    g["w_mix_out"] = mm("mix_dw", merged, dr2, ta=True)
    d_ydn, d_ypool, d_gates = merge_bwd("merge_b", gates, ydn, ypool, dmerged)
    g["w_dn_branch"] = mm("dn_dw", d_ydn, on, ta=True)
    d_on = mm("dn_dx", d_ydn, w["w_dn_branch"])
    g["w_pool_branch"] = mm("pool_dw", d_ypool, po, ta=True)
    d_po = mm("pool_dx", d_ypool, w["w_pool_branch"])
    dp, g["pool_w"], g["pool_scale"] = pool_bwd("pool_b", p, d_po, w["pool_w"], w["pool_scale"])
    d_o, dz, g["dn_norm_w"] = onorm_bwd("onorm_b", o, z, d_on, w["dn_norm_w"])
    dqd, dkd, du, dw_, dattn3, dgl = delta_scan_bwd("dscan_b", d_o, qd, kd, wd_, attn3, vn, st, gcb)
    dq, dk, dv, dgb, dbb = delta_prep_bwd("dprep_b", q, k, v, gb, bb, t3, du, dw_, dqd, dkd, dattn3, dgl)
    dpre, dc0, dc1, dc2, dc3 = conv_bwd("conv_b", pre, dq, dk, dv, taps)
    g["conv_w"] = jnp.concatenate([dc0, dc1, dc2, dc3], axis=0)
    d_ab, dalog, ddtb = gates_bwd("gates_b", ab, dgb, dbb, alog, dtb)
    g["a_log"] = dalog[:, :DN_HEADS]
    g["dt_bias"] = ddtb[:, :DN_HEADS]
    g["in_qkv"], g["in_z"], g["in_ab"] = mm_fan_t("in_dw_a", [dpre, dz, d_ab], x1)
    g["in_gates"], g["in_p"] = mm_fan_t("in_dw_b", [d_gates, dp], x1)
    io.grads_in("xa", g["in_ab"])
    dx1 = mm_sum("in_dx", [(dpre, w["in_qkv"]), (dz, w["in_z"]), (d_gates, w["in_gates"]), (dp, w["in_p"]),
                           (d_ab, w["in_ab"])], deps=io.grads_out("mixer", g))
    dr1, g["ln1_g"], g["ln1_b"] = ln_bwd("ln1_b", r1, [(ALPHA, dr2), (1.0, dx1)], w["ln1_g"])

    def on_dw(which, dw):
        name = "ffn1_w_" + which
        small = io.small_out(dict(g, loss=loss[0, :1])) if which == "down" else ()
        return small + io.grads_out(name, {name: dw})

    grad_x, g["ffn1_w_gate"], g["ffn1_w_up"], g["ffn1_w_down"] = ffn_bwd(
        "ffn1b", x, res1, dr1, w["ffn1_w_gate"], w["ffn1_w_up"], w["ffn1_w_down"], on_dw=on_dw, also=(ALPHA, dr1))
    return loss, grad_x, g


WEIGHT_NAMES = ['ffn1_w_gate', 'ffn1_w_up', 'ffn1_w_down', 'ln1_g', 'ln1_b', 'w_in', 'conv_w', 'a_log', 'dt_bias',
                'dn_norm_w', 'w_dn_branch', 'pool_w', 'pool_scale', 'w_pool_branch', 'w_mix_out', 'ln2_g', 'ln2_b',
                'mem_ln_g', 'mem_ln_b', 'xa_wq', 'xa_wk', 'xa_wv', 'xa_wo', 'ln3_g', 'ln3_b', 'ffn2_w_gate',
                'ffn2_w_up', 'ffn2_w_down', 'ln4_g', 'ln4_b']
SHARDED = [
    ("ffn1_w_gate", "cols", (1024, 352)), ("ffn1_w_up", "cols", (1024, 352)), ("ffn1_w_down", "rows", (352, 1024)),
    ("w_in", "cols", (1024, 577)), ("conv_w", "flat", (4, 192)), ("w_dn_branch", "cols", (512, 128)),
    ("w_pool_branch", "cols", (512, 128)), ("w_mix_out", "rows", (128, 1024)), ("xa_wq", "rows", (128, 1024)),
    ("xa_wk", "rows", (128, 1024)), ("xa_wv", "rows", (128, 1024)), ("xa_wo", "rows", (128, 1024)),
    ("ffn2_w_gate", "cols", (1024, 352)), ("ffn2_w_up", "cols", (1024, 352)), ("ffn2_w_down", "rows", (352, 1024)),
]
REPLICATED = [n for n in WEIGHT_NAMES if n not in {s[0] for s in SHARDED}]
ROW_ALIGN = 16
ROW_BLOCKS = (512, 384, 352, 256, 192, 176, 128)
GROUPS = {"ffn1_gu": ("ffn1_w_gate", "ffn1_w_up"), "ffn1_d": ("ffn1_w_down",),
          "ffn1_w_gate": ("ffn1_w_gate",), "ffn1_w_up": ("ffn1_w_up",), "ffn1_w_down": ("ffn1_w_down",),
          "mixer": ("w_in", "conv_w", "w_dn_branch", "w_pool_branch", "w_mix_out"),
          "xa": ("xa_wq", "xa_wk", "xa_wv", "xa_wo"),
          "ffn2": ("ffn2_w_gate", "ffn2_w_up", "ffn2_w_down")}
W_IN_COLS = 577
W_IN_PIECES = (("in_qkv", 0, 1536), ("in_z", 1536, 2048), ("in_ab", 2048, 2056), ("in_p", 2056, 2568),
               ("in_gates", 2568, 4616))


def _round_up(n, m):
    return -(-n // m) * m


def _layout():
    off, table = 0, {}
    for name, form, shape in SHARDED:
        valid = {"rows": shape[0], "cols": shape[1], "flat": 2}[form]
        width = {"rows": shape[1], "cols": shape[0], "flat": shape[0] * shape[1]}[form]
        rows = _round_up(valid, ROW_ALIGN)
        table[name] = (off, rows, valid, width, form, shape)
        off += rows
    return table


LAYOUT = _layout()


def _group_span(names):
    base = LAYOUT[names[0]][0]
    rows = LAYOUT[names[-1]][0] + LAYOUT[names[-1]][1] - base
    while not any(rows % b == 0 for b in ROW_BLOCKS):
        rows += ROW_ALIGN
    return base, rows


def _row_block(rows):
    return _pick(rows, ROW_BLOCKS)


def _pad_block(blk, rows):
    return jnp.pad(blk, ((0, rows - blk.shape[0]), (0, LANES - blk.shape[1])))


def pack_weight_shards(shards, names):
    parts, used = [], 0
    for name in names:
        off, rows, valid, width, form, _ = LAYOUT[name]
        s = shards[name]
        if form == "flat":
            flat = s.reshape(1, -1)
            hi = flat.astype(BF16)
            blk = jnp.concatenate([hi, (flat - hi.astype(F32)).astype(BF16)], axis=0)
        else:
            blk = (s.T if form == "cols" else s).astype(BF16)
        parts.append(_pad_block(blk, rows))
        used += rows
    if _group_span(names)[1] > used:
        parts.append(jnp.zeros((_group_span(names)[1] - used, LANES), BF16))
    return jnp.concatenate(parts, axis=0)


IN_AB_ROWS = 128


def _w_in_segments(first, last):
    segs = []
    for k in range(N_DEV):
        lo, hi = max(first, k * W_IN_COLS), min(last, (k + 1) * W_IN_COLS)
        if lo < hi:
            segs.append((k, lo - k * W_IN_COLS, lo - first, hi - lo))
    return segs


def w_in_pieces(name, gathered, off, rows):
    assert off % rows == 0
    sizes = [IN_AB_ROWS if piece == "in_ab" else last - first for piece, first, last in W_IN_PIECES]

    def body(src_ref, *outs):
        for o_ref, (piece, first, last) in zip(outs, W_IN_PIECES):
            if piece == "in_ab":
                o_ref[...] = jnp.zeros_like(o_ref)
            for k, src, dst, count in _w_in_segments(first, last):
                o_ref[pl.ds(dst, count), :] = src_ref[k, pl.ds(src, count), :]

    outs = pl.pallas_call(
        body, name=name, grid=(1,), in_specs=[pl.BlockSpec((N_DEV, rows, LANES), lambda i: (0, off // rows, 0))],
        out_specs=[pl.BlockSpec((n, LANES), lambda i: (0, 0)) for n in sizes],
        out_shape=[jax.ShapeDtypeStruct((n, LANES), gathered.dtype) for n in sizes],
        compiler_params=_params(("arbitrary",)),
    )(gathered)
    return {piece: o for (piece, _, _), o in zip(W_IN_PIECES, outs)}


def unpack_full_weights(gathered, names):
    out, base = {}, _group_span(names)[0]
    for name in names:
        off, rows, valid, width, form, shape = LAYOUT[name]
        seg = gathered[:, off - base:off - base + rows]
        if form == "flat":
            flat = seg[:, 0, :width].astype(F32) + seg[:, 1, :width].astype(F32)
            out[name] = flat.reshape((N_DEV,) + shape).transpose(1, 0, 2).reshape(shape[0], N_DEV * shape[1])
        elif name == "w_in":
            out.update(w_in_pieces("w_in_pieces", gathered, off - base, rows))
        else:
            out[name] = seg[:, :valid, :width].reshape(N_DEV * valid, width)
    return out


def pack_full_grads(grads, names, me):
    wire, own, used = [], [], 0
    for name in names:
        off, rows, valid, width, form, shape = LAYOUT[name]
        if form == "flat":
            full = grads[name].reshape(shape[0], N_DEV, shape[1]).transpose(1, 0, 2).reshape(N_DEV, 1, width)
        elif name == "w_in":
            full = jnp.concatenate([grads[piece][:last - first] for piece, first, last in W_IN_PIECES], axis=0)
            full = full.reshape(N_DEV, valid, width)
        else:
            full = grads[name].reshape(N_DEV, valid, width)
        pad = ((0, rows - full.shape[1]), (0, LANES - width))
        wire.append(jnp.pad(full.astype(WIRE), ((0, 0),) + pad))
        own.append(jnp.pad(lax.dynamic_index_in_dim(full, me, 0, keepdims=False), pad))
        used += rows
    if _group_span(names)[1] > used:
        wire.append(jnp.zeros((N_DEV, _group_span(names)[1] - used, LANES), WIRE))
        own.append(jnp.zeros((_group_span(names)[1] - used, LANES), F32))
    return jnp.concatenate(wire, axis=1), jnp.concatenate(own, axis=0)


TRANSPOSED = ("ffn1_w_gate", "ffn1_w_up", "ffn2_w_gate", "ffn2_w_up", "w_in")


def unpack_grad_shards(packed, names):
    out, base = {}, _group_span(names)[0]
    for name in names:
        off, rows, valid, width, form, shape = LAYOUT[name]
        off -= base
        if form == "flat":
            out[name] = packed[off, :width].reshape(shape)
        elif name in TRANSPOSED:
            out[name] = packed[off:off + valid, :width]
        elif form == "cols":
            out[name] = packed[off:off + valid, :width].T
        else:
            out[name] = packed[off:off + valid, :width]
    return out


SMALL_SHAPES = {n: (1024,) for n in REPLICATED}
SMALL_SHAPES.update(pool_w=(4, 128, 128), pool_scale=(512,), dn_norm_w=(128,), a_log=(4,), dt_bias=(4,))


SMALL_SHAPES["loss"] = (1,)
SMALL_NAMES = REPLICATED + ["loss"]


def _small_layout():
    off, table = 0, {}
    for name in SMALL_NAMES:
        numel = 1
        for d in SMALL_SHAPES[name]:
            numel *= d
        rows = _round_up(-(-numel // LANES), 8)
        table[name] = (off, rows, numel)
        off += rows
    return table, off


SMALL_LAYOUT, SMALL_ROWS = _small_layout()


def _to_rows(flat, rows):
    return jnp.pad(flat, (0, rows * LANES - flat.shape[0])).reshape(rows, LANES)


def pack_small(values):
    return jnp.concatenate([_to_rows(values[name].reshape(-1), SMALL_LAYOUT[name][1]) for name in SMALL_NAMES], axis=0)


def unpack_small(packed):
    out = {}
    for name in SMALL_NAMES:
        off, rows, numel = SMALL_LAYOUT[name]
        out[name] = packed[off:off + rows].reshape(-1)[:numel].reshape(SMALL_SHAPES[name])
    return out


MESH = pl.DeviceIdType.MESH


def _position():
    return lax.axis_index("x"), lax.axis_index("y"), lax.axis_index("c")


def _other_chips(x, y):
    return [(1 - x, y), (x, 1 - y), (1 - x, 1 - y)]


def all_gather(name, block):
    rows, n = block.shape

    def body(x_ref, out_ref, send_sems, recv_sems, local_sem):
        x, y, c = _position()
        me, sibling = (x, y, c), (x, y, 1 - c)
        chips = _other_chips(x, y)

        def slot(px, py, pc):
            return out_ref.at[4 * px + 2 * py + pc]

        def copy(k, blk, to, src=None):
            return pltpu.make_async_remote_copy(
                src_ref=slot(*blk) if src is None else src, dst_ref=slot(*blk),
                send_sem=send_sems.at[k], recv_sem=recv_sems.at[k], device_id=to, device_id_type=MESH)

        mine = pltpu.make_async_copy(x_ref, slot(*me), local_sem)
        mine.start()
        first = [copy(0, me, sibling, src=x_ref)]
        first += [copy(1 + j, me, (*chip, c), src=x_ref) for j, chip in enumerate(chips)]
        for cp in first:
            cp.start()
        passed = [copy(4 + j, (*chip, c), sibling) for j, chip in enumerate(chips)]
        for j, chip in enumerate(chips):
            copy(1 + j, (*chip, c), me).wait_recv()
            passed[j].start()
        copy(0, sibling, me).wait_recv()
        for j, chip in enumerate(chips):
            copy(4 + j, (*chip, 1 - c), me).wait_recv()
        for cp in first + passed:
            cp.wait_send()
        mine.wait()

    return pl.pallas_call(
        body, name=name, out_shape=jax.ShapeDtypeStruct((N_DEV, rows, n), block.dtype),
        in_specs=[ANY], out_specs=ANY,
        scratch_shapes=[pltpu.SemaphoreType.DMA((7,)), pltpu.SemaphoreType.DMA((7,)), pltpu.SemaphoreType.DMA(())],
    )(block)


HBM = pl.BlockSpec(memory_space=pltpu.HBM)
SEM = pl.BlockSpec(memory_space=pltpu.SEMAPHORE)
EFFECT = pltpu.SideEffectType.DATAFLOW_SIDE_EFFECTING


def _remote(src, dst, send_sem, recv_sem, to):
    return pltpu.make_async_remote_copy(src_ref=src, dst_ref=dst, send_sem=send_sem, recv_sem=recv_sem,
                                        device_id=to, device_id_type=MESH)


def split_start(name, bufs, n, make_copies):
    nb = len(bufs)

    def body(*refs):
        for out_cp, _ in make_copies(refs[:nb], refs[nb:nb + n], refs[nb + n:nb + 2 * n]):
            out_cp.start()
        refs[-1][...] = jnp.zeros_like(refs[-1])

    outs = pl.pallas_call(
        body, name=name,
        out_shape=tuple([pltpu.SemaphoreType.DMA(())] * (2 * n)) + tuple(pltpu.HBM(b.shape, b.dtype) for b in bufs)
        + (jax.ShapeDtypeStruct((8, 128), F32),),
        in_specs=[HBM] * nb,
        out_specs=tuple([SEM] * (2 * n) + [HBM] * nb + [pl.BlockSpec(memory_space=pltpu.VMEM)]),
        input_output_aliases={i: 2 * n + i for i in range(nb)},
        compiler_params=pltpu.CompilerParams(has_side_effects=EFFECT),
    )(*[pltpu.with_memory_space_constraint(b, pltpu.HBM) for b in bufs])
    return list(outs[:2 * n]), list(outs[2 * n:2 * n + nb]), outs[-1]


def split_wait(name, bufs, sems, n, make_copies, after):
    nb = len(bufs)

    def body(*refs):
        for out_cp, in_cp in make_copies(refs[:nb], refs[nb:nb + n], refs[nb + n:nb + 2 * n]):
            out_cp.wait_send()
            in_cp.wait_recv()

    outs = pl.pallas_call(
        body, name=name, out_shape=tuple(pltpu.HBM(b.shape, b.dtype) for b in bufs),
        in_specs=[HBM] * nb + [SEM] * (2 * n) + [ANY], out_specs=tuple([HBM] * nb),
        input_output_aliases={i: i for i in range(nb)},
        compiler_params=pltpu.CompilerParams(has_side_effects=EFFECT),
    )(*bufs, *sems, after)
    return list(outs)


def _gather_stage1(refs, send, recv):
    src, land = refs
    x, y, c = _position()
    peers = [(x, y, 1 - c)] + [(*chip, c) for chip in _other_chips(x, y)]
    return [(_remote(src, land.at[4 * x + 2 * y + c], send[k], recv[k], p),
             _remote(src, land.at[4 * p[0] + 2 * p[1] + p[2]], send[k], recv[k], p)) for k, p in enumerate(peers)]


def _gather_stage2(refs, send, recv):
    (land,) = refs
    x, y, c = _position()
    out = []
    for j, (px, py) in enumerate(_other_chips(x, y)):
        mine, theirs = land.at[4 * px + 2 * py + c], land.at[4 * px + 2 * py + 1 - c]
        out.append((_remote(mine, mine, send[j], recv[j], (x, y, 1 - c)),
                    _remote(theirs, theirs, send[j], recv[j], (x, y, 1 - c))))
    return out


def _flips():
    return [(a, b, d) for a in (0, 1) for b in (0, 1) for d in (0, 1) if a | b | d]


def _gather_direct(refs, send, recv):
    src, land = refs
    x, y, c = _position()
    out = []
    for k, (fx, fy, fc) in enumerate(_flips()):
        p = (1 - x if fx else x, 1 - y if fy else y, 1 - c if fc else c)
        out.append((_remote(src, land.at[4 * x + 2 * y + c], send[k], recv[k], p),
                    _remote(src, land.at[4 * p[0] + 2 * p[1] + p[2]], send[k], recv[k], p)))
    return out


def _scatter_direct(refs, send, recv):
    sendbuf, land = refs
    x, y, c = _position()
    out = []
    for k, (fx, fy, fc) in enumerate(_flips()):
        p = (1 - x if fx else x, 1 - y if fy else y, 1 - c if fc else c)
        cp = _remote(sendbuf.at[4 * p[0] + 2 * p[1] + p[2]], land.at[k], send[k], recv[k], p)
        out.append((cp, cp))
    return out


def _own_plus_slots(name, own, landed):
    n, rows, _ = landed.shape
    tr = _row_block(rows)

    def body(g_ref, l_ref, o_ref):
        acc = g_ref[...]
        for j in range(n):
            acc = acc + l_ref[j].astype(F32)
        o_ref[...] = acc

    return pl.pallas_call(
        body, name=name, grid=(rows // tr,),
        in_specs=[pl.BlockSpec((tr, LANES), lambda i: (i, 0)), pl.BlockSpec((n, tr, LANES), lambda i: (0, i, 0))],
        out_specs=pl.BlockSpec((tr, LANES), lambda i: (i, 0)),
        out_shape=jax.ShapeDtypeStruct((rows, LANES), F32), compiler_params=_params(("parallel",)),
    )(own, landed)


def _sum_slots(name, stack):
    n, rows, _ = stack.shape

    def body(s_ref, o_ref):
        acc = s_ref[0]
        for j in range(1, n):
            acc = acc + s_ref[j]
        o_ref[...] = acc

    return pl.pallas_call(
        body, name=name, in_specs=[pl.BlockSpec(stack.shape, lambda: (0, 0, 0))],
        out_specs=pl.BlockSpec((rows, LANES), lambda: (0, 0)), out_shape=jax.ShapeDtypeStruct((rows, LANES), F32),
    )(stack)


def adamw(name, w, g, m, v):
    shape = w.shape
    last = shape[-1]
    w2, g2, m2, v2 = [a.reshape(-1, last) for a in (w, g, m, v)]
    rows = w2.shape[0]
    tr = _pick(rows, (256, 176, 128))

    def body(w_ref, g_ref, m_ref, v_ref, d_ref, nm_ref, nv_ref):
        gg = g_ref[...]
        nm = ADAM_B1 * m_ref[...] + (1.0 - ADAM_B1) * gg
        nv = ADAM_B2 * v_ref[...] + (1.0 - ADAM_B2) * (gg * gg)
        m_hat = nm / (1.0 - ADAM_B1 ** ADAM_STEP)
        v_hat = nv / (1.0 - ADAM_B2 ** ADAM_STEP)
        d_ref[...] = -ADAM_LR * (m_hat / (jnp.sqrt(v_hat) + ADAM_EPS) + ADAM_WD * w_ref[...])
        nm_ref[...] = nm
        nv_ref[...] = nv

    spec = pl.BlockSpec((tr, last), lambda i: (i, 0))
    outs = pl.pallas_call(
        body, name=name, grid=(rows // tr,), in_specs=[spec] * 4, out_specs=[spec] * 3,
        out_shape=[jax.ShapeDtypeStruct((rows, last), F32)] * 3, compiler_params=_params(("parallel",)),
    )(w2, g2, m2, v2)
    return [o.reshape(shape) for o in outs]


def _landing(block_shape, dtype, own):
    x, y, c = _position()
    return lax.dynamic_update_slice(lax.empty((N_DEV,) + block_shape, dtype), own[None], (4 * x + 2 * y + c, 0, 0))


class _Exchanges:
    def __init__(self, shards):
        self.shards = shards
        self.pending = {}
        self.reduced = {}

    def first_weights(self):
        names = GROUPS["ffn1_gu"]
        return unpack_full_weights(all_gather("ag_ffn1_gu", pack_weight_shards(self.shards, names)), names)

    def rest_started(self):
        tokens = []
        block = pack_weight_shards(self.shards, GROUPS["ffn1_d"])
        sems, bufs, token = split_start("ag_ffn1_d_s", [block, _landing(block.shape, block.dtype, block)], N_DEV - 1,
                                        _gather_direct)
        self.pending["ffn1_d"] = (sems, bufs)
        tokens.append(token)
        for key in ("mixer", "xa", "ffn2"):
            block = pack_weight_shards(self.shards, GROUPS[key])
            sems, bufs, token = split_start(f"ag_{key}_s1", [block, _landing(block.shape, block.dtype, block)], 4,
                                            _gather_stage1)
            self.pending[key] = (sems, bufs)
            tokens.append(token)
        return tuple(tokens)

    def ffn1_down(self, after):
        sems, bufs = self.pending.pop("ffn1_d")
        _, gathered = split_wait("ag_ffn1_d_w", bufs, sems, N_DEV - 1, _gather_direct, after)
        return unpack_full_weights(gathered, GROUPS["ffn1_d"])["ffn1_w_down"]

    def halfway(self, key, after):
        sems, bufs = self.pending.pop(key)
        _, land = split_wait(f"ag_{key}_w1", bufs, sems, 4, _gather_stage1, after)
        sems, bufs, token = split_start(f"ag_{key}_s2", [land], 3, _gather_stage2)
        self.pending[key] = (sems, bufs)
        return (token,)

    def weights(self, key, after):
        sems, bufs = self.pending.pop(key)
        (gathered,) = split_wait(f"ag_{key}_w2", bufs, sems, 3, _gather_stage2, after)
        return unpack_full_weights(gathered, GROUPS[key])

    def grads_out(self, key, grads):
        x, y, c = _position()
        wire, own = pack_full_grads(grads, GROUPS[key], 4 * x + 2 * y + c)
        land = lax.empty((N_DEV - 1,) + wire.shape[1:], WIRE)
        sems, bufs, token = split_start(f"rs_{key}_start", [wire, land], N_DEV - 1, _scatter_direct)
        self.pending[key] = (sems, bufs, own)
        return (token,)

    def grads_in(self, key, after):
        sems, bufs, own = self.pending.pop(key)
        _, landed = split_wait(f"rs_{key}_wait", bufs, sems, N_DEV - 1, _scatter_direct, after)
        self.reduced.update(unpack_grad_shards(_own_plus_slots(f"rs_{key}_sum", own, landed), GROUPS[key]))

    def small_out(self, values):
        block = pack_small(values)
        sems, bufs, token = split_start("ag_small_s", [block, _landing(block.shape, block.dtype, block)], N_DEV - 1,
                                        _gather_direct)
        self.pending["small"] = (sems, bufs)
        return (token,)

    def small_in(self, after):
        sems, bufs = self.pending.pop("small")
        _, gathered = split_wait("ag_small_w", bufs, sems, N_DEV - 1, _gather_direct, after)
        return unpack_small(_sum_slots("small_sum", gathered))


def kernel(x, mem, ffn1_w_gate, ffn1_w_up, ffn1_w_down, ln1_g, ln1_b, w_in, conv_w, a_log, dt_bias, dn_norm_w, w_dn_branch, pool_w, pool_scale, w_pool_branch, w_mix_out, ln2_g, ln2_b, mem_ln_g, mem_ln_b, xa_wq, xa_wk, xa_wv, xa_wo, ln3_g, ln3_b, ffn2_w_gate, ffn2_w_up, ffn2_w_down, ln4_g, ln4_b, loss_target, m_ffn1_w_gate, m_ffn1_w_up, m_ffn1_w_down, m_ln1_g, m_ln1_b, m_w_in, m_conv_w, m_a_log, m_dt_bias, m_dn_norm_w, m_w_dn_branch, m_pool_w, m_pool_scale, m_w_pool_branch, m_w_mix_out, m_ln2_g, m_ln2_b, m_mem_ln_g, m_mem_ln_b, m_xa_wq, m_xa_wk, m_xa_wv, m_xa_wo, m_ln3_g, m_ln3_b, m_ffn2_w_gate, m_ffn2_w_up, m_ffn2_w_down, m_ln4_g, m_ln4_b, v_ffn1_w_gate, v_ffn1_w_up, v_ffn1_w_down, v_ln1_g, v_ln1_b, v_w_in, v_conv_w, v_a_log, v_dt_bias, v_dn_norm_w, v_w_dn_branch, v_pool_w, v_pool_scale, v_w_pool_branch, v_w_mix_out, v_ln2_g, v_ln2_b, v_mem_ln_g, v_mem_ln_b, v_xa_wq, v_xa_wk, v_xa_wv, v_xa_wo, v_ln3_g, v_ln3_b, v_ffn2_w_gate, v_ffn2_w_up, v_ffn2_w_down, v_ln4_g, v_ln4_b):
    given = dict(locals())
    shards = {n: given[n] for n in WEIGHT_NAMES}
    io = _Exchanges({n: shards[n][0] for n, _, _ in SHARDED})
    w = io.first_weights()
    for n in REPLICATED:
        w[n] = shards[n][0] if n == "pool_w" else shards[n]
    loss_part, grad_x, g = local_step(x[0], mem[0], loss_target[0], w, io)

    grad, updates = {}, {}

    def update(names, reduced):
        for n in names:
            if n in TRANSPOSED:
                outs = adamw("adamw_" + n, shards[n][0].T, reduced[n], given["m_" + n][0].T, given["v_" + n][0].T)
                grad[n], updates[n] = reduced[n].T[None], [o.T[None] for o in outs]
            else:
                grad[n] = reduced[n].reshape(shards[n].shape)
                updates[n] = adamw("adamw_" + n, shards[n], grad[n], given["m_" + n], given["v_" + n])
        return updates[names[-1]][0]

    update(GROUPS["ffn2"] + GROUPS["xa"], io.reduced)
    io.grads_in("mixer", grad_x)
    done = update(GROUPS["mixer"], io.reduced)
    small = io.small_in(done)
    loss = small.pop("loss")[0]
    done = update(REPLICATED, small)
    for n in ("ffn1_w_down", "ffn1_w_gate", "ffn1_w_up"):
        io.grads_in(n, done)
        done = update(GROUPS[n], io.reduced)
    return (loss, grad_x[None], *[grad[n] for n in WEIGHT_NAMES], *[updates[n][0] for n in WEIGHT_NAMES],
            *[updates[n][1] for n in WEIGHT_NAMES], *[updates[n][2] for n in WEIGHT_NAMES])
```

```python
import jax
import jax.numpy as jnp
from jax import lax
from jax.experimental import pallas as pl
from jax.experimental.pallas import tpu as pltpu

F32 = jnp.float32
BF16 = jnp.bfloat16
MMD = BF16
WIRE = BF16
X3 =lax.Precision.HIGH
VMEM_LIMIT_BYTES = 48 * 1024 * 1024

D_MODEL = 1024
D_FF = 2816
CHUNK = 64
N_MEM = 256
DN_HEADS = 4
HD = 128
DN_WIDTH = 512
POOL_WINDOWS = (2, 4, 8, 16)
POOL_WIDTH = 512
XA_HEADS = 4
XA_HD = 256
LN_EPS = 1e-5
RMS_EPS = 1e-6
L2_EPS = 1e-6
ALPHA = 2.0 ** 0.25
HALO = 16
ROWS = 512
ROWS_LIGHT = 1024
ROWS_WIDE = 256

ADAM_LR = 0.001
ADAM_B1 = 0.9
ADAM_B2 = 0.999
ADAM_EPS = 1e-08
ADAM_WD = 0.01
ADAM_STEP = 10

N_DEV = 8
LANES = 1024
ANY = pl.BlockSpec(memory_space=pl.ANY)


def _dot(a, b, ca, cb, prec):
    dn = (((ca,), (cb,)), ((), ()))
    if prec is not None:
        return lax.dot_general(a.astype(F32), b.astype(F32), dn, precision=prec, preferred_element_type=F32)
    return lax.dot_general(a.astype(MMD), b.astype(MMD), dn, preferred_element_type=F32)


def dnn(a, b, prec=None):
    return _dot(a, b, 1, 0, prec)


def dnt(a, b, prec=None):
    return _dot(a, b, 1, 1, prec)


def dtn(a, b, prec=None):
    return _dot(a, b, 0, 0, prec)


def _sigmoid(x):
    return jax.nn.sigmoid(x)


def _silu(x):
    return x * _sigmoid(x)


def _dsilu(x):
    s = _sigmoid(x)
    return s * (1.0 + x * (1.0 - s))


def _softplus(x):
    return jnp.maximum(x, 0.0) + jnp.log1p(jnp.exp(-jnp.abs(x)))


def _iota(shape, dim):
    return lax.broadcasted_iota(jnp.int32, shape, dim)


def _rsum(x):
    return jnp.sum(x, axis=1, keepdims=True)


def _csum(x):
    return jnp.sum(x, axis=0, keepdims=True)


def _pick(n, cands):
    for c in cands:
        if n % c == 0:
            return c
    return n


def _params(sem):
    return pltpu.CompilerParams(dimension_semantics=sem, vmem_limit_bytes=VMEM_LIMIT_BYTES)


MM_TILE_SIZES = (4096, 2816, 2048, 1536, 1408, 1024, 768, 512, 384, 256, 128)
MM_VMEM_BUDGET = 36 * 1024 * 1024
HBM_BYTES_PER_US = 3.0e6
GRID_STEP_US = 0.35


def _mm_tiles(m, n, kc, a_bytes, b_bytes, o_bytes):
    def sizes(d):
        return [d] if d <= 512 else [t for t in MM_TILE_SIZES if d % t == 0]

    best = None
    for tm in sizes(m):
        for tn in sizes(n):
            for tk in sizes(kc):
                vmem = 2 * (tm * tk * a_bytes + tk * tn * b_bytes + tm * tn * o_bytes) + tm * tn * 4
                if vmem > MM_VMEM_BUDGET:
                    continue
                steps = (m // tm) * (n // tn) * (kc // tk)
                traffic = m * kc * a_bytes * (n // tn) + kc * n * b_bytes * (m // tm) + m * n * o_bytes
                edge = tm * tk * a_bytes + tk * tn * b_bytes + tm * tn * o_bytes
                cost = (traffic + edge) / HBM_BYTES_PER_US + steps * GRID_STEP_US
                if best is None or cost < best[0]:
                    best = (cost, tm, tn, tk)
    return best[1:]


def mm(name, a, b, *, ta=False, tb=False, out_dtype=F32, add=None, scale=None, deps=()):
    adds = [] if add is None else (list(add) if isinstance(add, (list, tuple)) else [(1.0, add)])
    if ta:
        kc, m = a.shape
    else:
        m, kc = a.shape
    if tb:
        n, kb = b.shape
    else:
        kb, n = b.shape
    assert kc == kb, (name, a.shape, b.shape)
    tm, tn, tk = _mm_tiles(m, n, kc, a.dtype.itemsize, b.dtype.itemsize,
                           jnp.dtype(out_dtype).itemsize * (1 + len(adds)))
    nk = kc // tk
    grid = (m // tm, n // tn, nk)
    a_spec = pl.BlockSpec((tk, tm), lambda i, j, k: (k, i)) if ta else pl.BlockSpec((tm, tk), lambda i, j, k: (i, k))
    b_spec = pl.BlockSpec((tn, tk), lambda i, j, k: (j, k)) if tb else pl.BlockSpec((tk, tn), lambda i, j, k: (k, j))
    o_spec = pl.BlockSpec((tm, tn), lambda i, j, k: (i, j))
    ca, cb = (0 if ta else 1), (1 if tb else 0)

    def body(*refs):
        a_ref, b_ref = refs[0], refs[1]
        o_ref = refs[-1] if nk == 1 else refs[-2]
        k = pl.program_id(2)
        part = _dot(a_ref[...], b_ref[...], ca, cb, None)

        def finish(r):
            if scale is not None:
                r = r * scale
            for (coef, _), add_ref in zip(adds, refs[2:2 + len(adds)]):
                r = r + (add_ref[...] if coef == 1.0 else coef * add_ref[...])
            o_ref[...] = r.astype(o_ref.dtype)

        if nk == 1:
            finish(part)
            return
        acc_ref = refs[-1]

        @pl.when(k == 0)
        def _():
            acc_ref[...] = part

        if nk > 2:
            @pl.when((k > 0) & (k < nk - 1))
            def _():
                acc_ref[...] += part

        @pl.when(k == nk - 1)
        def _():
            finish(acc_ref[...] + part)

    ins = [a, b] + [t for _, t in adds] + list(deps)
    specs = [a_spec, b_spec] + [o_spec] * len(adds) + [ANY] * len(deps)
    return pl.pallas_call(
        body, name=name, grid=grid, in_specs=specs, out_specs=o_spec,
        out_shape=jax.ShapeDtypeStruct((m, n), out_dtype),
        scratch_shapes=[pltpu.VMEM((tm, tn), F32)] if nk > 1 else [],
        compiler_params=_params(("parallel", "parallel", "arbitrary")),
    )(*ins)


def mm_fan_t(name, lefts, b):
    kc, n = b.shape
    tk = min(512, kc)
    nk = kc // tk

    def body(*refs):
        k = pl.program_id(0)
        bb = refs[len(lefts)][...].astype(MMD)
        for a_ref, o_ref in zip(refs[:len(lefts)], refs[len(lefts) + 1:]):
            part = dtn(a_ref[...], bb)

            @pl.when(k == 0)
            def _(o_ref=o_ref, part=part):
                o_ref[...] = part

            @pl.when(k > 0)
            def _(o_ref=o_ref, part=part):
                o_ref[...] += part

    return pl.pallas_call(
        body, name=name, grid=(nk,),
        in_specs=[pl.BlockSpec((tk, a.shape[1]), lambda k: (k, 0)) for a in lefts] + [pl.BlockSpec((tk, n), lambda k: (k, 0))],
        out_specs=[pl.BlockSpec((a.shape[1], n), lambda k: (0, 0)) for a in lefts],
        out_shape=[jax.ShapeDtypeStruct((a.shape[1], n), F32) for a in lefts],
        compiler_params=_params(("arbitrary",)),
    )(*lefts, b)


def mm_sum(name, pairs, deps=()):
    m, n = pairs[0][0].shape[0], pairs[0][1].shape[1]
    tm = min(512, m)
    np_ = len(pairs)

    def body(*refs):
        acc = dnn(refs[0][...], refs[1][...])
        for p in range(1, np_):
            acc = acc + dnn(refs[2 * p][...], refs[2 * p + 1][...])
        refs[-1][...] = acc

    specs, ins = [], []
    for a, b in pairs:
        specs += [pl.BlockSpec((tm, a.shape[1]), lambda i: (i, 0)), pl.BlockSpec(b.shape, lambda i: (0, 0))]
        ins += [a, b]
    return pl.pallas_call(
        body, name=name, grid=(m // tm,), in_specs=specs + [ANY] * len(deps),
        out_specs=pl.BlockSpec((tm, n), lambda i: (i, 0)), out_shape=jax.ShapeDtypeStruct((m, n), F32),
        compiler_params=_params(("parallel",)),
    )(*ins, *deps)


class _Ctx:
    def __init__(self, i, nblk, tl):
        self.i, self.nblk, self.tl = i, nblk, tl


def _norm_item(it):
    if isinstance(it, tuple):
        a, w, j = it[:3]
        rows = it[3] if len(it) > 3 else None
        return a, w, j, rows
    return it, it.shape[-1], 0, None


def rowwise(name, fn, length, tl, *, rows=(), consts=(), prevs=(), nexts=(), out_rows=(), out_accs=(), deps=()):
    nblk = length // tl
    hb = tl // HALO
    nhalo = length // HALO
    arrays, specs = [], []
    for it in rows:
        a, w, j, r = _norm_item(it)
        if a.ndim == 3:
            specs.append(pl.BlockSpec((a.shape[0], tl, w), lambda i, j=j: (0, i, j)))
        else:
            specs.append(pl.BlockSpec((r or tl, w), lambda i, j=j: (i, j)))
        arrays.append(a)
    for a in consts:
        specs.append(pl.BlockSpec(a.shape, lambda i, nd=a.ndim: (0,) * nd))
        arrays.append(a)
    for it in prevs:
        a, w, j, _ = _norm_item(it)
        specs.append(pl.BlockSpec((HALO, w), lambda i, j=j: (jnp.maximum(i * hb - 1, 0), j)))
        arrays.append(a)
    for it in nexts:
        a, w, j, _ = _norm_item(it)
        specs.append(pl.BlockSpec((HALO, w), lambda i, j=j: (jnp.minimum((i + 1) * hb, nhalo - 1), j)))
        arrays.append(a)
    out_shape, out_specs = [], []
    for spec in out_rows:
        if len(spec) == 3:
            h, w, dt = spec
            out_shape.append(jax.ShapeDtypeStruct((h, length, w), dt))
            out_specs.append(pl.BlockSpec((h, tl, w), lambda i: (0, i, 0)))
        else:
            w, dt = spec
            out_shape.append(jax.ShapeDtypeStruct((length, w), dt))
            out_specs.append(pl.BlockSpec((tl, w), lambda i: (i, 0)))
    for shape, dt in out_accs:
        out_shape.append(jax.ShapeDtypeStruct(shape, dt))
        out_specs.append(pl.BlockSpec(shape, lambda i, nd=len(shape): (0,) * nd))
    n_r, n_c, n_p, n_n = len(rows), len(consts), len(prevs), len(nexts)
    n_in = n_r + n_c + n_p + n_n
    n_or = len(out_rows)
    arrays, specs = arrays + list(deps), specs + [ANY] * len(deps)

    def body(*refs):
        i = pl.program_id(0)
        vals = [r[...] for r in refs[:n_in]]
        outs = refs[n_in + len(deps):]
        ctx = _Ctx(i, nblk, tl)
        ro, ao = fn(ctx, vals[:n_r], vals[n_r:n_r + n_c], vals[n_r + n_c:n_r + n_c + n_p], vals[n_r + n_c + n_p:])
        for r, v in zip(outs[:n_or], ro, strict=True):
            r[...] = v.astype(r.dtype)
        for r, v in zip(outs[n_or:], ao, strict=True):
            @pl.when(i == 0)
            def _(r=r, v=v):
                r[...] = v.astype(r.dtype)

            @pl.when(i > 0)
            def _(r=r, v=v):
                r[...] += v.astype(r.dtype)

    res = pl.pallas_call(
        body, name=name, grid=(nblk,), in_specs=specs, out_specs=out_specs, out_shape=out_shape,
        compiler_params=_params(("arbitrary",) if out_accs else ("parallel",)),
    )(*arrays)
    return res


def _heads(x, n, w):
    return [x[:, h * w:(h + 1) * w] for h in range(n)]


def _cat(xs):
    return jnp.concatenate(xs, axis=1)


def _row_index(ctx, nrows, offset=0):
    return ctx.i * ctx.tl + offset + _iota((nrows, 1), 0)


def _ln_stats(r):
    mu = jnp.mean(r, axis=1, keepdims=True)
    d = r - mu
    var = jnp.mean(d * d, axis=1, keepdims=True)
    rstd = lax.rsqrt(var + LN_EPS)
    return d * rstd, rstd


def ln_fwd(name, terms, g, b, tl=ROWS_LIGHT, deps=()):
    coefs = [c for c, _ in terms]
    length = terms[0][1].shape[0]

    def fn(ctx, rows, consts, prevs, nexts):
        r = sum(c * t for c, t in zip(coefs, rows))
        xh, _ = _ln_stats(r)
        return [xh * consts[0] + consts[1], r], []

    return rowwise(name, fn, length, min(tl, length), rows=[t for _, t in terms], consts=[g, b],
                   out_rows=[(D_MODEL, F32), (D_MODEL, F32)], deps=deps)


def ln_bwd(name, r, terms, g, tl=ROWS_LIGHT, deps=()):
    coefs = [c for c, _ in terms]
    length = r.shape[0]

    def fn(ctx, rows, consts, prevs, nexts):
        xh, rstd = _ln_stats(rows[0])
        dy = sum(c * t for c, t in zip(coefs, rows[1:]))
        dxh = dy * consts[0]
        dr = rstd * (dxh - jnp.mean(dxh, axis=1, keepdims=True) - xh * jnp.mean(dxh * xh, axis=1, keepdims=True))
        return [dr], [_csum(dy * xh), _csum(dy)]

    return rowwise(name, fn, length, min(tl, length), rows=[r] + [t for _, t in terms], consts=[g],
                   out_rows=[(D_MODEL, F32)], out_accs=[((1, D_MODEL), F32), ((1, D_MODEL), F32)], deps=deps)


def ln_loss(name, terms, g, b, target, tl=ROWS_LIGHT):
    coefs = [c for c, _ in terms]
    length = target.shape[0]
    nt = len(terms)

    def fn(ctx, rows, consts, prevs, nexts):
        r = sum(c * t for c, t in zip(coefs, rows[:nt]))
        xh, _ = _ln_stats(r)
        err = xh * consts[0] + consts[1] - rows[nt]
        tot = _csum(_rsum(err * err)) * (0.5 / D_MODEL)
        return [err * (1.0 / D_MODEL), r], [jnp.broadcast_to(tot, (1, 128))]

    return rowwise(name, fn, length, min(tl, length), rows=[t for _, t in terms] + [target], consts=[g, b],
                   out_rows=[(D_MODEL, F32), (D_MODEL, F32)], out_accs=[((1, 128), F32)])


def _ffn_blocks(length):
    return min(512, length), D_FF // 2


def ffn_gate_up_act(name, x, wg, wu, deps=()):
    length = x.shape[0]
    tm, tn = _ffn_blocks(length)

    def body(x_ref, wg_ref, wu_ref, *rest):
        hg_ref, hu_ref, act_ref = rest[-3:]
        xb = x_ref[...].astype(MMD)
        hg = dnt(xb, wg_ref[...])
        hu = dnt(xb, wu_ref[...])
        hg_ref[...] = hg
        hu_ref[...] = hu
        act_ref[...] = (_silu(hg) * hu).astype(act_ref.dtype)

    row = pl.BlockSpec((tm, D_MODEL), lambda i, j: (i, 0))
    wsp = pl.BlockSpec((tn, D_MODEL), lambda i, j: (j, 0))
    osp = pl.BlockSpec((tm, tn), lambda i, j: (i, j))
    return pl.pallas_call(
        body, name=name, grid=(length // tm, D_FF // tn), in_specs=[row, wsp, wsp] + [ANY] * len(deps),
        out_specs=[osp] * 3,
        out_shape=[jax.ShapeDtypeStruct((length, D_FF), F32)] * 2 + [jax.ShapeDtypeStruct((length, D_FF), BF16)],
        compiler_params=_params(("parallel", "parallel")),
    )(x, wg, wu, *deps)


def ffn_dact(name, dr, wd, hg, hu, deps=()):
    length = dr.shape[0]
    tm, tn = _ffn_blocks(length)

    def body(dr_ref, wd_ref, hg_ref, hu_ref, *rest):
        dhg_ref, dhu_ref = rest[-2:]
        da = 0.5 * dnt(dr_ref[...], wd_ref[...])
        g = hg_ref[...]
        s = _sigmoid(g)
        dhg_ref[...] = (da * hu_ref[...] * (s * (1.0 + g * (1.0 - s)))).astype(dhg_ref.dtype)
        dhu_ref[...] = (da * (g * s)).astype(dhu_ref.dtype)

    row = pl.BlockSpec((tm, D_MODEL), lambda i, j: (i, 0))
    wsp = pl.BlockSpec((tn, D_MODEL), lambda i, j: (j, 0))
    osp = pl.BlockSpec((tm, tn), lambda i, j: (i, j))
    return pl.pallas_call(
        body, name=name, grid=(length // tm, D_FF // tn), in_specs=[row, wsp, osp, osp] + [ANY] * len(deps),
        out_specs=[osp] * 2, out_shape=[jax.ShapeDtypeStruct((length, D_FF), BF16)] * 2,
        compiler_params=_params(("parallel", "parallel")),
    )(dr, wd, hg, hu, *deps)


def ffn_fwd(tag, x, wg, wu, wd, deps=()):
    hg, hu, act = ffn_gate_up_act(tag + "_gate_up", x, wg, wu, deps)
    if callable(wd):
        wd = wd(act)
    f = mm(tag + "_down", act, wd)
    return f, (hg, hu, act), wd


def ffn_bwd(tag, x, res, dr, wg, wu, wd, deps=(), on_dw=None, also=None):
    on_dw = on_dw or (lambda which, dw: ())
    hg, hu, act = res
    dwd = mm(tag + "_dwd", act, dr, ta=True, scale=0.5, deps=deps)
    dhg, dhu = ffn_dact(tag + "_dact", dr, wd, hg, hu, deps=on_dw("down", dwd))
    dwg = mm(tag + "_dwg", dhg, x, ta=True)
    dwu = mm(tag + "_dwu", dhu, x, ta=True, deps=on_dw("gate", dwg))
    dx = mm(tag + "_dxg", dhg, wg, deps=on_dw("up", dwu))
    dx = mm(tag + "_dxu", dhu, wu, add=[(1.0, dx)] + ([also] if also else []))
    return dx, dwg, dwu, dwd


def _conv_taps(ext, taps, n):
    out = taps[3] * ext
    for j in range(3):
        out = out + taps[j] * pltpu.roll(ext, 3 - j, 0)
    return out


def _l2n(x):
    r = lax.rsqrt(_rsum(x * x) + L2_EPS)
    return x * r, r


def conv_fwd(name, pre, taps, tl=ROWS_WIDE, deps=()):
    length = pre.shape[0]
    tl = min(tl, length)

    def fn(ctx, rows, consts, prevs, nexts):
        prev = jnp.where(ctx.i > 0, prevs[0], 0.0)
        ext = jnp.concatenate([prev, rows[0]], axis=0)
        s = _silu(_conv_taps(ext, consts, tl + HALO)[HALO:])
        q = _cat([_l2n(x)[0] * (HD ** -0.5) for x in _heads(s[:, :DN_WIDTH], DN_HEADS, HD)])
        k = _cat([_l2n(x)[0] for x in _heads(s[:, DN_WIDTH:2 * DN_WIDTH], DN_HEADS, HD)])
        return [q, k, s[:, 2 * DN_WIDTH:]], []

    return rowwise(name, fn, length, tl, rows=[pre], consts=list(taps), prevs=[pre],
                   out_rows=[(DN_WIDTH, F32)] * 3, deps=deps)


def conv_bwd(name, pre, dq, dk, dv, taps, tl=ROWS_WIDE):
    length = pre.shape[0]
    tl = min(tl, length)
    n = tl + 2 * HALO

    def fn(ctx, rows, consts, prevs, nexts):
        last = ctx.i == ctx.nblk - 1
        prev = jnp.where(ctx.i > 0, prevs[0], 0.0)
        ext = jnp.concatenate([prev, rows[0], nexts[0]], axis=0)
        c = _conv_taps(ext, consts, n)
        sg = _sigmoid(c)
        s = c * sg
        zero = jnp.zeros((HALO, DN_WIDTH), F32)
        dqe, dke, dve = [jnp.concatenate([zero, rows[1 + t], jnp.where(last, 0.0, nexts[1 + t])], axis=0)
                         for t in range(3)]

        def l2_bwd(x, dy):
            y, r = _l2n(x)
            return r * (dy - y * _rsum(dy * y))

        dsq = _cat([l2_bwd(x, d * (HD ** -0.5)) for x, d in zip(_heads(s[:, :DN_WIDTH], DN_HEADS, HD),
                                                                 _heads(dqe, DN_HEADS, HD))])
        dsk = _cat([l2_bwd(x, d) for x, d in zip(_heads(s[:, DN_WIDTH:2 * DN_WIDTH], DN_HEADS, HD),
                                                  _heads(dke, DN_HEADS, HD))])
        dc = _cat([dsq, dsk, dve]) * (sg * (1.0 + c * (1.0 - sg)))
        dpre = consts[3] * dc
        for j in range(3):
            dpre = dpre + consts[j] * pltpu.roll(dc, n - (3 - j), 0)
        dc_cur = dc[HALO:HALO + tl]
        dws = [_csum(dc_cur * pltpu.roll(ext, 3 - j, 0)[HALO:HALO + tl]) for j in range(3)]
        dws.append(_csum(dc_cur * ext[HALO:HALO + tl]))
        return [dpre[HALO:HALO + tl]], dws

    return rowwise(name, fn, length, tl, rows=[pre, dq, dk, dv], consts=list(taps), prevs=[pre],
                   nexts=[pre, dq, dk, dv], out_rows=[(3 * DN_WIDTH, BF16)],
                   out_accs=[((1, 3 * DN_WIDTH), F32)] * 4)


def _gate_math(ab, alog, dtb):
    z = ab + dtb
    g = -jnp.exp(alog) * _softplus(z)
    beta = _sigmoid(ab)
    return z, g, beta


def gates_fwd(name, ab, alog, dtb, tl=ROWS_LIGHT):
    length = ab.shape[0]

    def fn(ctx, rows, consts, prevs, nexts):
        _, g, beta = _gate_math(rows[0], consts[0], consts[1])
        spread = [jnp.broadcast_to(v[:, h:h + 1], (v.shape[0], HD))
                  for v, first in ((g, 0), (beta, DN_HEADS)) for h in range(first, first + DN_HEADS)]
        return [_cat(spread[:DN_HEADS]), _cat(spread[DN_HEADS:])], []

    return rowwise(name, fn, length, min(tl, length), rows=[ab], consts=[alog, dtb],
                   out_rows=[(DN_WIDTH, F32)] * 2)


def gates_bwd(name, ab, dgb, dbb, alog, dtb, tl=ROWS_LIGHT):
    length = ab.shape[0]

    def fn(ctx, rows, consts, prevs, nexts):
        z, g, beta = _gate_math(rows[0], consts[0], consts[1])
        lane = _iota(g.shape, 1)
        dsmall = jnp.zeros_like(g)
        for h in range(DN_HEADS):
            dsmall = jnp.where(lane == h, rows[1][:, h * HD:h * HD + 1], dsmall)
            dsmall = jnp.where(lane == DN_HEADS + h, rows[2][:, h * HD:h * HD + 1], dsmall)
        is_a = lane < DN_HEADS
        da = jnp.where(is_a, dsmall * (-jnp.exp(consts[0])) * _sigmoid(z), 0.0)
        db = jnp.where((lane >= DN_HEADS) & (lane < 2 * DN_HEADS), dsmall * beta * (1.0 - beta), 0.0)
        return [da + db], [_csum(jnp.where(is_a, dsmall * g, 0.0)), _csum(da)]

    return rowwise(name, fn, length, min(tl, length), rows=[ab, dgb, dbb], consts=[alog, dtb],
                   out_rows=[(128, BF16)], out_accs=[((1, 128), F32)] * 2)


CPS = 4


def _chunk_scan_rows(x, suffix=False):
    n = x.shape[0]
    rc = _iota(x.shape, 0) & (CHUNK - 1)
    sh = 1
    while sh < CHUNK:
        if suffix:
            x = x + jnp.where(rc < CHUNK - sh, pltpu.roll(x, n - sh, 0), 0.0)
        else:
            x = x + jnp.where(rc >= sh, pltpu.roll(x, sh, 0), 0.0)
        sh *= 2
    return x


def _tri_inv(a_list, eye, bd):
    def each(f, *ls):
        return [f(*xs) for xs in zip(*ls)]

    dg = [jnp.where(bd, a, 0.0) for a in a_list]
    lo = each(lambda a, d: a - d, a_list, dg)
    n1 = [-d for d in dg]
    n2 = each(lambda n: dnn(n, n, X3), n1)
    n4 = each(lambda n: dnn(n, n, X3), n2)
    td = each(lambda p, s: dnn(eye + p, eye + s, X3), n1, n2)
    n8 = each(lambda n: dnn(n, n, X3), n4)
    td = each(lambda t, n: dnn(t, eye + n, X3), td, n4)
    td = each(lambda t, n: dnn(t, eye + n, X3), td, n8)
    m = each(lambda t, l: dnn(t, l, X3), td, lo)
    m2 = each(lambda x: dnn(x, x, X3), m)
    x = each(lambda p, s: dnn(eye - p, eye + s, X3), m, m2)
    return each(lambda p, t: dnn(p, t, X3), x, td)


def _chunk_common(q, k, v, gcb, bb):
    egb = jnp.exp(gcb)
    gc64 = gcb[:, :CHUNK]
    ii, jj = _iota((CHUNK, CHUNK), 0), _iota((CHUNK, CHUNK), 1)
    incl, strict = ii >= jj, ii > jj
    decay = jnp.exp(jnp.where(incl, gc64 - gc64.T, -jnp.inf))
    kb = k * bb
    vb = v * bb
    kbe = kb * egb
    pq = dnt(jnp.concatenate([kb, q], axis=0), k, X3)
    ekb = jnp.exp(gcb[CHUNK - 1:CHUNK, :] - gcb)
    return dict(egb=egb, decay=decay, kb=kb, vb=vb, kbe=kbe, pm=pq[:CHUNK], qm=pq[CHUNK:], ekb=ekb,
                incl=incl, strict=strict, ii=ii, jj=jj)


def _chunk_head(vals, ci, h):
    return [v[ci * CHUNK:(ci + 1) * CHUNK, h * HD:(h + 1) * HD] for v in vals]


def _assemble(per_chunk):
    return jnp.concatenate([_cat(hs) for hs in per_chunk], axis=0)


def _assemble3(per_chunk):
    return jnp.stack([jnp.concatenate([per_chunk[ci][h] for ci in range(CPS)], axis=0) for h in range(DN_HEADS)])


def delta_prep_fwd(name, q, k, v, gb, bb):
    length = q.shape[0]

    def fn(ctx, rows, consts, prevs, nexts):
        gcb_all = _chunk_scan_rows(rows[3])
        vals = [rows[0], rows[1], rows[2], gcb_all, rows[4]]
        units = [(ci, h) for ci in range(CPS) for h in range(DN_HEADS)]
        ins = [_chunk_head(vals, ci, h) for ci, h in units]
        cs = [_chunk_common(*i) for i in ins]
        eye = (cs[0]["ii"] == cs[0]["jj"]).astype(F32)
        ts = _tri_inv([jnp.where(c["strict"], c["pm"] * c["decay"], 0.0) for c in cs], eye,
                      (cs[0]["ii"] >> 4) == (cs[0]["jj"] >> 4))
        uws = [dnn(t, _cat([c["vb"], c["kbe"]]), X3) for t, c in zip(ts, cs)]

        def grid2(xs):
            return [xs[ci * DN_HEADS:(ci + 1) * DN_HEADS] for ci in range(CPS)]

        return [_assemble(grid2([uw[:, :HD] for uw in uws])), _assemble(grid2([uw[:, HD:] for uw in uws])),
                _assemble(grid2([i[0] * c["egb"] for i, c in zip(ins, cs)])),
                _assemble(grid2([i[1] * c["ekb"] for i, c in zip(ins, cs)])), gcb_all,
                _assemble3(grid2([c["qm"] * c["decay"] for c in cs])), _assemble3(grid2(ts))], []

    return rowwise(name, fn, length, CHUNK * CPS, rows=[q, k, v, gb, bb],
                   out_rows=[(DN_WIDTH, F32)] * 5 + [(DN_HEADS, CHUNK, F32)] * 2)


def delta_prep_bwd(name, q, k, v, gb, bb, t3, du, dw, dqd, dkd, dattn3, dgl):
    length = q.shape[0]

    def fn(ctx, rows, consts, prevs, nexts):
        gcb_all = _chunk_scan_rows(rows[3])
        vals = [rows[0], rows[1], rows[2], gcb_all] + list(rows[4:9])
        t3v, da3v, dglv = rows[9], rows[10], rows[11]
        units = [(ci, h) for ci in range(CPS) for h in range(DN_HEADS)]
        ins = [_chunk_head(vals, ci, h) for ci, h in units]
        cs = [_chunk_common(*i[:5]) for i in ins]
        ts = [t3v[h][ci * CHUNK:(ci + 1) * CHUNK] for ci, h in units]
        dattns = [jnp.where(c["incl"], da3v[h][ci * CHUNK:(ci + 1) * CHUNK], 0.0) for (ci, h), c in zip(units, cs)]
        duws = [_cat([i[5], i[6]]) for i in ins]
        dvks = [dtn(t, d, X3) for t, d in zip(ts, duws)]
        dts = [dnt(d, _cat([c["vb"], c["kbe"]]), X3) for d, c in zip(duws, cs)]
        dts = [dnt(d, t, X3) for d, t in zip(dts, ts)]
        das = [jnp.where(c["strict"], -dtn(t, d, X3), 0.0) for c, t, d in zip(cs, ts, dts)]
        dpqs = [jnp.concatenate([da * c["decay"], dat * c["decay"]], axis=0) for da, dat, c in zip(das, dattns, cs)]
        dpqks = [dnn(d, i[1], X3) for d, i in zip(dpqs, ins)]
        dkps = [dtn(d, jnp.concatenate([c["kb"], i[0]], axis=0), X3) for d, c, i in zip(dpqs, cs, ins)]
        dqs, dks, dvs, dgcs, dbs = [], [], [], [], []
        for (ci, h), i, c, dvk, da, dattn, dpqk, dkp in zip(units, ins, cs, dvks, das, dattns, dpqks, dkps):
            qh, kh, vh, _, bh, _, _, dqdh, dkdh = i
            dvb, dkbe = dvk[:, :HD], dvk[:, HD:]
            dkb = dpqk[:CHUNK] + dkbe * c["egb"]
            c1 = _rsum(dkbe * c["kb"] + dqdh * qh) * c["egb"]
            c2 = _rsum(dkdh * kh) * c["ekb"]
            e = (da * c["pm"] + dattn * c["qm"]) * c["decay"]
            dgc = c1 - c2 + _rsum(e) - _rsum(e.T)
            dgl_tot = jnp.max(dglv[ci * 8:(ci + 1) * 8, h * HD:(h + 1) * HD], axis=0, keepdims=True) + _csum(c2)
            dgcs.append(dgc + jnp.where(_iota((CHUNK, HD), 0) == CHUNK - 1, dgl_tot, 0.0))
            dqs.append(dpqk[CHUNK:] + dqdh * c["egb"])
            dks.append(dkp + dkdh * c["ekb"] + dkb * bh)
            dvs.append(dvb * bh)
            dbs.append(jnp.broadcast_to(_rsum(dkb * kh + dvb * vh), (CHUNK, HD)))

        def grid2(xs):
            return [xs[ci * DN_HEADS:(ci + 1) * DN_HEADS] for ci in range(CPS)]

        return [_assemble(grid2(dqs)), _assemble(grid2(dks)), _assemble(grid2(dvs)),
                _chunk_scan_rows(_assemble(grid2(dgcs)), suffix=True), _assemble(grid2(dbs))], []

    return rowwise(name, fn, length, CHUNK * CPS,
                   rows=[q, k, v, gb, bb, du, dw, dqd, dkd, t3, dattn3, (dgl, DN_WIDTH, 0, 8 * CPS)],
                   out_rows=[(DN_WIDTH, F32)] * 5)


SCAN_CHUNKS = 8


def _scan_chunks(n):
    return SCAN_CHUNKS if n % SCAN_CHUNKS == 0 else 1


def delta_scan_fwd(name, qd, kd, u, w, attn3, gcb):
    length = qd.shape[0]
    n = length // CHUNK
    sc = _scan_chunks(n)
    row = pl.BlockSpec((sc * CHUNK, DN_WIDTH), lambda c: (c, 0))
    sq = pl.BlockSpec((DN_HEADS, sc * CHUNK, CHUNK), lambda c: (0, c, 0))

    def body(qd_ref, kd_ref, u_ref, w_ref, attn_ref, gc_ref, o_ref, vn_ref, st_ref, s_ref):
        c = pl.program_id(0)

        @pl.when(c == 0)
        def _():
            s_ref[...] = jnp.zeros_like(s_ref)

        heads = range(DN_HEADS)
        sls = [pl.ds(h * HD, HD) for h in heads]
        ss = [s_ref[h] for h in heads]
        for ci in range(sc):
            rs = pl.ds(ci * CHUNK, CHUNK)
            ws = [dnn(w_ref[rs, sl], s) for sl, s in zip(sls, ss)]
            qs = [dnn(qd_ref[rs, sl], s) for sl, s in zip(sls, ss)]
            vns = [u_ref[rs, sl] - x for sl, x in zip(sls, ws)]
            avs = [dnn(attn_ref[h, rs, :], vn) for h, vn in zip(heads, vns)]
            kvs = [dtn(kd_ref[rs, sl], vn) for sl, vn in zip(sls, vns)]
            for h, sl in zip(heads, sls):
                st_ref[ci, h] = ss[h]
                o_ref[rs, sl] = qs[h] + avs[h]
                vn_ref[rs, sl] = vns[h]
            ss = [s * jnp.exp(gc_ref[pl.ds(ci * CHUNK + CHUNK - 1, 1), sl]) + kv for s, sl, kv in zip(ss, sls, kvs)]
        for h in heads:
            s_ref[h] = ss[h]

    return pl.pallas_call(
        body, name=name, grid=(n // sc,), in_specs=[row, row, row, row, sq, row],
        out_specs=[row, row, pl.BlockSpec((sc, DN_HEADS, HD, HD), lambda c: (c, 0, 0, 0))],
        out_shape=[jax.ShapeDtypeStruct((length, DN_WIDTH), F32), jax.ShapeDtypeStruct((length, DN_WIDTH), F32),
                   jax.ShapeDtypeStruct((n, DN_HEADS, HD, HD), F32)],
        scratch_shapes=[pltpu.VMEM((DN_HEADS, HD, HD), F32)],
        compiler_params=_params(("arbitrary",)),
    )(qd, kd, u, w, attn3, gcb)


def delta_scan_bwd(name, do, qd, kd, w, attn3, vn, st, gcb):
    length = qd.shape[0]
    n = length // CHUNK
    sc = _scan_chunks(n)
    nb = n // sc
    row = pl.BlockSpec((sc * CHUNK, DN_WIDTH), lambda c: (nb - 1 - c, 0))
    sq = pl.BlockSpec((DN_HEADS, sc * CHUNK, CHUNK), lambda c: (0, nb - 1 - c, 0))
    stb = pl.BlockSpec((sc, DN_HEADS, HD, HD), lambda c: (nb - 1 - c, 0, 0, 0))
    glb = pl.BlockSpec((sc * 8, DN_WIDTH), lambda c: (nb - 1 - c, 0))

    def body(do_ref, qd_ref, kd_ref, w_ref, attn_ref, vn_ref, st_ref, gc_ref,
             dqd_ref, dkd_ref, du_ref, dw_ref, dattn_ref, dgl_ref, ds_ref):
        c = pl.program_id(0)

        @pl.when(c == 0)
        def _():
            ds_ref[...] = jnp.zeros_like(ds_ref)

        heads = range(DN_HEADS)
        sls = [pl.ds(h * HD, HD) for h in heads]
        dsns = [ds_ref[h] for h in heads]
        for ci in reversed(range(sc)):
            rs = pl.ds(ci * CHUNK, CHUNK)
            ss = [st_ref[ci, h] for h in heads]
            dos = [do_ref[rs, sl] for sl in sls]
            vns = [vn_ref[rs, sl] for sl in sls]
            dvns = [dtn(attn_ref[h, rs, :], d) for h, d in zip(heads, dos)]
            dvns = [x + dnn(kd_ref[rs, sl], dsn) for x, sl, dsn in zip(dvns, sls, dsns)]
            qdos = [dtn(qd_ref[rs, sl], d) for sl, d in zip(sls, dos)]
            for h, sl in zip(heads, sls):
                dattn_ref[h, rs, :] = dnt(dos[h], vns[h])
                dqd_ref[rs, sl] = dnt(dos[h], ss[h])
                dkd_ref[rs, sl] = dnt(vns[h], dsns[h])
                du_ref[rs, sl] = dvns[h]
            dws = [dnt(dvn, s) for dvn, s in zip(dvns, ss)]
            wdvs = [dtn(w_ref[rs, sl], dvn) for sl, dvn in zip(sls, dvns)]
            nxt = []
            for h, sl in zip(heads, sls):
                egl = jnp.exp(gc_ref[pl.ds(ci * CHUNK + CHUNK - 1, 1), sl])
                dw_ref[rs, sl] = -dws[h]
                dgl_ref[pl.ds(ci * 8, 8), sl] = jnp.broadcast_to(_csum(_rsum(dsns[h] * ss[h])) * egl, (8, HD))
                nxt.append(dsns[h] * egl + qdos[h] - wdvs[h])
            dsns = nxt
        for h in heads:
            ds_ref[h] = dsns[h]

    return pl.pallas_call(
        body, name=name, grid=(nb,), in_specs=[row, row, row, row, sq, row, stb, row],
        out_specs=[row, row, row, row, sq, glb],
        out_shape=[jax.ShapeDtypeStruct((length, DN_WIDTH), F32)] * 4
        + [jax.ShapeDtypeStruct((DN_HEADS, length, CHUNK), F32), jax.ShapeDtypeStruct((n * 8, DN_WIDTH), F32)],
        scratch_shapes=[pltpu.VMEM((DN_HEADS, HD, HD), F32)],
        compiler_params=_params(("arbitrary",)),
    )(do, qd, kd, w, attn3, vn, st, gcb)


def onorm_fwd(name, o, z, nw, tl=ROWS_LIGHT):
    length = o.shape[0]

    def fn(ctx, rows, consts, prevs, nexts):
        outs = []
        for oh, zh in zip(_heads(rows[0], DN_HEADS, HD), _heads(rows[1], DN_HEADS, HD)):
            r = lax.rsqrt(jnp.mean(oh * oh, axis=1, keepdims=True) + RMS_EPS)
            outs.append(oh * r * consts[0] * _silu(zh))
        return [_cat(outs)], []

    return rowwise(name, fn, length, min(tl, length), rows=[o, z], consts=[nw], out_rows=[(DN_WIDTH, BF16)])[0]


def onorm_bwd(name, o, z, d_on, nw, tl=ROWS_LIGHT):
    length = o.shape[0]

    def fn(ctx, rows, consts, prevs, nexts):
        dos, dzs = [], []
        dnw = jnp.zeros((1, HD), F32)
        for oh, zh, dh in zip(*[_heads(r, DN_HEADS, HD) for r in rows]):
            r = lax.rsqrt(jnp.mean(oh * oh, axis=1, keepdims=True) + RMS_EPS)
            y = oh * r
            sz = _silu(zh)
            t = dh * sz * consts[0]
            dos.append(r * (t - y * jnp.mean(t * y, axis=1, keepdims=True)))
            dzs.append(dh * y * consts[0] * _dsilu(zh))
            dnw = dnw + _csum(dh * y * sz)
        return [_cat(dos), _cat(dzs)], [dnw]

    return rowwise(name, fn, length, min(tl, length), rows=[o, z, d_on], consts=[nw],
                   out_rows=[(DN_WIDTH, F32), (DN_WIDTH, BF16)], out_accs=[((1, HD), F32)])


def merge_fwd(name, gates, ydn, ypool, tl=ROWS):
    length = ydn.shape[0]

    def fn(ctx, rows, consts, prevs, nexts):
        gt = rows[0]
        return [_sigmoid(gt[:, :D_MODEL]) * rows[1] + _sigmoid(gt[:, D_MODEL:]) * rows[2]], []

    return rowwise(name, fn, length, min(tl, length), rows=[gates, ydn, ypool], out_rows=[(D_MODEL, BF16)])[0]


def merge_bwd(name, gates, ydn, ypool, dm, tl=ROWS_WIDE):
    length = ydn.shape[0]

    def fn(ctx, rows, consts, prevs, nexts):
        gt, yd, yp, d = rows
        sd, sp = _sigmoid(gt[:, :D_MODEL]), _sigmoid(gt[:, D_MODEL:])
        dgates = _cat([d * yd * sd * (1.0 - sd), d * yp * sp * (1.0 - sp)])
        return [d * sd, d * sp, dgates], []

    return rowwise(name, fn, length, min(tl, length), rows=[gates, ydn, ypool, dm],
                   out_rows=[(D_MODEL, BF16), (D_MODEL, BF16), (2 * D_MODEL, BF16)])


def _trailing_sums(ext, upto):
    s, sh = ext, 1
    while sh < upto:
        s = s + pltpu.roll(s, sh, 0)
        sh *= 2
    return s


def _leading_sums(ext, upto, n):
    s, sh = ext, 1
    while sh < upto:
        s = s + pltpu.roll(s, n - sh, 0)
        sh *= 2
    return s


def _pool_mixed(ctx, p, prev, tl):
    prevm = jnp.where(ctx.i > 0, prev, 0.0)
    t1 = (_row_index(ctx, tl) + 1).astype(F32)
    outs = []
    for gi, win in enumerate(POOL_WINDOWS):
        sl = slice(gi * HD, (gi + 1) * HD)
        ext = jnp.concatenate([prevm[:, sl], p[:, sl]], axis=0)
        mean = _trailing_sums(ext, win)[HALO:] / jnp.minimum(t1, float(win))
        outs.append(mean - p[:, sl])
    return outs


def pool_fwd(name, p, pool_w, scale, tl=ROWS_LIGHT):
    length = p.shape[0]
    tl = min(tl, length)

    def fn(ctx, rows, consts, prevs, nexts):
        mixed = _pool_mixed(ctx, rows[0], prevs[0], tl)
        y = _cat([dnn(m, consts[0][gi]) for gi, m in enumerate(mixed)])
        return [y * consts[1]], []

    return rowwise(name, fn, length, tl, rows=[p], consts=[pool_w, scale], prevs=[p],
                   out_rows=[(POOL_WIDTH, BF16)])[0]


def pool_bwd(name, p, dpo, pool_w, scale, tl=ROWS_LIGHT):
    length = p.shape[0]
    tl = min(tl, length)
    n = tl + HALO

    def fn(ctx, rows, consts, prevs, nexts):
        last = ctx.i == ctx.nblk - 1
        mixed = _pool_mixed(ctx, rows[0], prevs[0], tl)
        dext = jnp.concatenate([rows[1], jnp.where(last, 0.0, nexts[0])], axis=0)
        t1 = (_row_index(ctx, n) + 1).astype(F32)
        dps, dws, dscs = [], [], []
        for gi, win in enumerate(POOL_WINDOWS):
            sl = slice(gi * HD, (gi + 1) * HD)
            wg = consts[0][gi]
            dyraw = dext[:, sl] * consts[1][:, sl]
            dmix = dnt(dyraw, wg)
            dws.append(dtn(mixed[gi], dyraw[:tl]))
            dscs.append(_csum(rows[1][:, sl] * dnn(mixed[gi], wg)))
            lead = _leading_sums(dmix / jnp.minimum(t1, float(win)), win, n)
            dps.append(lead[:tl] - dmix[:tl])
        return [_cat(dps)], [jnp.stack(dws), _cat(dscs)]

    return rowwise(name, fn, length, tl, rows=[p, dpo], consts=[pool_w, scale], prevs=[p], nexts=[dpo],
                   out_rows=[(POOL_WIDTH, BF16)],
                   out_accs=[((len(POOL_WINDOWS), HD, HD), F32), ((1, POOL_WIDTH), F32)])


def _xa_probs(qh, kh):
    s = dnt(qh, kh) * (XA_HD ** -0.5)
    e = jnp.exp(s - jnp.max(s, axis=1, keepdims=True))
    return e / _rsum(e)


def xattn_fwd(name, qx, kx, vx, tl=ROWS_LIGHT):
    length = qx.shape[0]

    def fn(ctx, rows, consts, prevs, nexts):
        outs = [dnn(_xa_probs(qh, kh), vh) for qh, kh, vh in
                zip(_heads(rows[0], XA_HEADS, XA_HD), _heads(consts[0], XA_HEADS, XA_HD),
                    _heads(consts[1], XA_HEADS, XA_HD))]
        return [_cat(outs)], []

    return rowwise(name, fn, length, min(tl, length), rows=[qx], consts=[kx, vx], out_rows=[(D_MODEL, BF16)])[0]


def xattn_bwd(name, qx, dox, kx, vx, tl=ROWS):
    length = qx.shape[0]

    def fn(ctx, rows, consts, prevs, nexts):
        dqs, dks, dvs = [], [], []
        for qh, dh, kh, vh in zip(_heads(rows[0], XA_HEADS, XA_HD), _heads(rows[1], XA_HEADS, XA_HD),
                                  _heads(consts[0], XA_HEADS, XA_HD), _heads(consts[1], XA_HEADS, XA_HD)):
            pr = _xa_probs(qh, kh)
            dpr = dnt(dh, vh)
            ds = pr * (dpr - _rsum(dpr * pr)) * (XA_HD ** -0.5)
            dqs.append(dnn(ds, kh))
            dks.append(dtn(ds, qh))
            dvs.append(dtn(pr, dh))
        return [_cat(dqs)], [_cat(dks), _cat(dvs)]

    return rowwise(name, fn, length, min(tl, length), rows=[qx, dox], consts=[kx, vx],
                   out_rows=[(D_MODEL, BF16)], out_accs=[((N_MEM, D_MODEL), F32)] * 2)


def local_step(x, mem, target, w, io):
    alog = jnp.pad(w["a_log"], ((0, 0), (0, 128 - DN_HEADS)))
    dtb = jnp.pad(w["dt_bias"], ((0, 0), (0, 128 - DN_HEADS)))

    f1, res1, w_down1 = ffn_fwd("ffn1", x, w["ffn1_w_gate"], w["ffn1_w_up"], io.ffn1_down, deps=io.rest_started())
    x1, r1 = ln_fwd("ln1", [(ALPHA, x), (0.5, f1)], w["ln1_g"], w["ln1_b"], deps=io.halfway("mixer", f1))
    w = dict(w, ffn1_w_down=w_down1, **io.weights("mixer", x1))
    taps = [w["conv_w"][j:j + 1] for j in range(4)]

    pre = mm("in_qkv", x1, w["in_qkv"], tb=True)
    z = mm("in_z", x1, w["in_z"], tb=True)
    gates = mm("in_gates", x1, w["in_gates"], tb=True)
    p = mm("in_p", x1, w["in_p"], tb=True)
    ab = mm("in_ab", x1, w["in_ab"], tb=True)
    q, k, v = conv_fwd("conv", pre, taps, deps=io.halfway("xa", pre))
    gb, bb = gates_fwd("gates", ab, alog, dtb)
    u, wd_, qd, kd, gcb, attn3, t3 = delta_prep_fwd("dprep", q, k, v, gb, bb)
    o, vn, st = delta_scan_fwd("dscan", qd, kd, u, wd_, attn3, gcb)
    on = onorm_fwd("onorm", o, z, w["dn_norm_w"])
    ydn = mm("dn_branch", on, w["w_dn_branch"], tb=True)
    po = pool_fwd("pool", p, w["pool_w"], w["pool_scale"])
    ypool = mm("pool_branch", po, w["w_pool_branch"], tb=True)
    merged = merge_fwd("merge", gates, ydn, ypool)
    mix = mm("mix_out", merged, w["w_mix_out"])
    x2, r2 = ln_fwd("ln2", [(ALPHA, x1), (1.0, mix)], w["ln2_g"], w["ln2_b"])

    w = dict(w, **io.weights("xa", x2))
    m, _ = ln_fwd("ln_mem", [(1.0, mem)], w["mem_ln_g"], w["mem_ln_b"])
    qx = mm("xa_q", x2, w["xa_wq"], deps=io.halfway("ffn2", x2))
    kx = mm("xa_k", m, w["xa_wk"])
    vx = mm("xa_v", m, w["xa_wv"])
    ox = xattn_fwd("xattn", qx, kx, vx)
    xa = mm("xa_o", ox, w["xa_wo"])
    x3, r3 = ln_fwd("ln3", [(ALPHA, x2), (1.0, xa)], w["ln3_g"], w["ln3_b"])
    w = dict(w, **io.weights("ffn2", x3))

    f2, res2, _ = ffn_fwd("ffn2", x3, w["ffn2_w_gate"], w["ffn2_w_up"], w["ffn2_w_down"])
    dy4, r4, loss = ln_loss("ln4_loss", [(ALPHA, x3), (0.5, f2)], w["ln4_g"], w["ln4_b"], target)

    g = {}
    dr4, g["ln4_g"], g["ln4_b"] = ln_bwd("ln4_b", r4, [(1.0, dy4)], w["ln4_g"])
    dx3, g["ffn2_w_gate"], g["ffn2_w_up"], g["ffn2_w_down"] = ffn_bwd(
        "ffn2b", x3, res2, dr4, w["ffn2_w_gate"], w["ffn2_w_up"], w["ffn2_w_down"])
    dep = io.grads_out("ffn2", g)
    dr3, g["ln3_g"], g["ln3_b"] = ln_bwd("ln3_b", r3, [(ALPHA, dr4), (1.0, dx3)], w["ln3_g"], deps=dep)

    dox = mm("xa_do", dr3, w["xa_wo"], tb=True)
    g["xa_wo"] = mm("xa_dwo", ox, dr3, ta=True)
    dqx, dkx, dvx = xattn_bwd("xattn_b", qx, dox, kx, vx)
    g["xa_wq"] = mm("xa_dwq", x2, dqx, ta=True)
    dx2 = mm("xa_dx", dqx, w["xa_wq"], tb=True)
    g["xa_wk"] = mm("xa_dwk", m, dkx, ta=True)
    g["xa_wv"] = mm("xa_dwv", m, dvx, ta=True)
    dmm = mm("xa_dmk", dkx, w["xa_wk"], tb=True, deps=io.grads_out("xa", g))
    dmm = mm("xa_dmv", dvx, w["xa_wv"], tb=True, add=dmm)
    _, g["mem_ln_g"], g["mem_ln_b"] = ln_bwd("ln_mem_b", mem, [(1.0, dmm)], w["mem_ln_g"])
    dr2, g["ln2_g"], g["ln2_b"] = ln_bwd("ln2_b", r2, [(ALPHA, dr3), (1.0, dx2)], w["ln2_g"])
    io.grads_in("ffn2", dr2)

    dmerged = mm("mix_dm", dr2, w["w_mix_out"], tb=True)
    g["w_mix_out"] = mm("mix_dw", merged, dr2, ta=True)
    d_ydn, d_ypool, d_gates = merge_bwd("merge_b", gates, ydn, ypool, dmerged)
    g["w_dn_branch"] = mm("dn_dw", d_ydn, on, ta=True)
    d_on = mm("dn_dx", d_ydn, w["w_dn_branch"])
    g["w_pool_branch"] = mm("pool_dw", d_ypool, po, ta=True)
    d_po = mm("pool_dx", d_ypool, w["w_pool_branch"])
    dp, g["pool_w"], g["pool_scale"] = pool_bwd("pool_b", p, d_po, w["pool_w"], w["pool_scale"])
    d_o, dz, g["dn_norm_w"] = onorm_bwd("onorm_b", o, z, d_on, w["dn_norm_w"])
    dqd, dkd, du, dw_, dattn3, dgl = delta_scan_bwd("dscan_b", d_o, qd, kd, wd_, attn3, vn, st, gcb)
    dq, dk, dv, dgb, dbb = delta_prep_bwd("dprep_b", q, k, v, gb, bb, t3, du, dw_, dqd, dkd, dattn3, dgl)
    dpre, dc0, dc1, dc2, dc3 = conv_bwd("conv_b", pre, dq, dk, dv, taps)
    g["conv_w"] = jnp.concatenate([dc0, dc1, dc2, dc3], axis=0)
    d_ab, dalog, ddtb = gates_bwd("gates_b", ab, dgb, dbb, alog, dtb)
    g["a_log"] = dalog[:, :DN_HEADS]
    g["dt_bias"] = ddtb[:, :DN_HEADS]
    g["in_qkv"], g["in_z"], g["in_ab"] = mm_fan_t("in_dw_a", [dpre, dz, d_ab], x1)
    g["in_gates"], g["in_p"] = mm_fan_t("in_dw_b", [d_gates, dp], x1)
    io.grads_in("xa", g["in_ab"])
    dx1 = mm_sum("in_dx", [(dpre, w["in_qkv"]), (dz, w["in_z"]), (d_gates, w["in_gates"]), (dp, w["in_p"]),
                           (d_ab, w["in_ab"])], deps=io.grads_out("mixer", g))
    dr1, g["ln1_g"], g["ln1_b"] = ln_bwd("ln1_b", r1, [(ALPHA, dr2), (1.0, dx1)], w["ln1_g"])

    def on_dw(which, dw):
        name = "ffn1_w_" + which
        small = io.small_out(dict(g, loss=loss[0, :1])) if which == "down" else ()
        return small + io.grads_out(name, {name: dw})

    grad_x, g["ffn1_w_gate"], g["ffn1_w_up"], g["ffn1_w_down"] = ffn_bwd(
        "ffn1b", x, res1, dr1, w["ffn1_w_gate"], w["ffn1_w_up"], w["ffn1_w_down"], on_dw=on_dw, also=(ALPHA, dr1))
    return loss, grad_x, g


WEIGHT_NAMES = ['ffn1_w_gate', 'ffn1_w_up', 'ffn1_w_down', 'ln1_g', 'ln1_b', 'w_in', 'conv_w', 'a_log', 'dt_bias',
                'dn_norm_w', 'w_dn_branch', 'pool_w', 'pool_scale', 'w_pool_branch', 'w_mix_out', 'ln2_g', 'ln2_b',
                'mem_ln_g', 'mem_ln_b', 'xa_wq', 'xa_wk', 'xa_wv', 'xa_wo', 'ln3_g', 'ln3_b', 'ffn2_w_gate',
                'ffn2_w_up', 'ffn2_w_down', 'ln4_g', 'ln4_b']
SHARDED = [
    ("ffn1_w_gate", "cols", (1024, 352)), ("ffn1_w_up", "cols", (1024, 352)), ("ffn1_w_down", "rows", (352, 1024)),
    ("w_in", "cols", (1024, 577)), ("conv_w", "flat", (4, 192)), ("w_dn_branch", "cols", (512, 128)),
    ("w_pool_branch", "cols", (512, 128)), ("w_mix_out", "rows", (128, 1024)), ("xa_wq", "rows", (128, 1024)),
    ("xa_wk", "rows", (128, 1024)), ("xa_wv", "rows", (128, 1024)), ("xa_wo", "rows", (128, 1024)),
    ("ffn2_w_gate", "cols", (1024, 352)), ("ffn2_w_up", "cols", (1024, 352)), ("ffn2_w_down", "rows", (352, 1024)),
]
REPLICATED = [n for n in WEIGHT_NAMES if n not in {s[0] for s in SHARDED}]
ROW_ALIGN = 16
ROW_BLOCKS = (512, 384, 352, 256, 192, 176, 128)
GROUPS = {"ffn1_gu": ("ffn1_w_gate", "ffn1_w_up"), "ffn1_d": ("ffn1_w_down",),
          "ffn1_w_gate": ("ffn1_w_gate",), "ffn1_w_up": ("ffn1_w_up",), "ffn1_w_down": ("ffn1_w_down",),
          "mixer": ("w_in", "conv_w", "w_dn_branch", "w_pool_branch", "w_mix_out"),
          "xa": ("xa_wq", "xa_wk", "xa_wv", "xa_wo"),
          "ffn2": ("ffn2_w_gate", "ffn2_w_up", "ffn2_w_down")}
W_IN_COLS = 577
W_IN_PIECES = (("in_qkv", 0, 1536), ("in_z", 1536, 2048), ("in_ab", 2048, 2056), ("in_p", 2056, 2568),
               ("in_gates", 2568, 4616))


def _round_up(n, m):
    return -(-n // m) * m


def _layout():
    off, table = 0, {}
    for name, form, shape in SHARDED:
        valid = {"rows": shape[0], "cols": shape[1], "flat": 2}[form]
        width = {"rows": shape[1], "cols": shape[0], "flat": shape[0] * shape[1]}[form]
        rows = _round_up(valid, ROW_ALIGN)
        table[name] = (off, rows, valid, width, form, shape)
        off += rows
    return table


LAYOUT = _layout()


def _group_span(names):
    base = LAYOUT[names[0]][0]
    rows = LAYOUT[names[-1]][0] + LAYOUT[names[-1]][1] - base
    while not any(rows % b == 0 for b in ROW_BLOCKS):
        rows += ROW_ALIGN
    return base, rows


def _row_block(rows):
    return _pick(rows, ROW_BLOCKS)


def _pad_block(blk, rows):
    return jnp.pad(blk, ((0, rows - blk.shape[0]), (0, LANES - blk.shape[1])))


def pack_weight_shards(shards, names):
    parts, used = [], 0
    for name in names:
        off, rows, valid, width, form, _ = LAYOUT[name]
        s = shards[name]
        if form == "flat":
            flat = s.reshape(1, -1)
            hi = flat.astype(BF16)
            blk = jnp.concatenate([hi, (flat - hi.astype(F32)).astype(BF16)], axis=0)
        else:
            blk = (s.T if form == "cols" else s).astype(BF16)
        parts.append(_pad_block(blk, rows))
        used += rows
    if _group_span(names)[1] > used:
        parts.append(jnp.zeros((_group_span(names)[1] - used, LANES), BF16))
    return jnp.concatenate(parts, axis=0)


IN_AB_ROWS = 128


def _w_in_segments(first, last):
    segs = []
    for k in range(N_DEV):
        lo, hi = max(first, k * W_IN_COLS), min(last, (k + 1) * W_IN_COLS)
        if lo < hi:
            segs.append((k, lo - k * W_IN_COLS, lo - first, hi - lo))
    return segs


def w_in_pieces(name, gathered, off, rows):
    assert off % rows == 0
    sizes = [IN_AB_ROWS if piece == "in_ab" else last - first for piece, first, last in W_IN_PIECES]

    def body(src_ref, *outs):
        for o_ref, (piece, first, last) in zip(outs, W_IN_PIECES):
            if piece == "in_ab":
                o_ref[...] = jnp.zeros_like(o_ref)
            for k, src, dst, count in _w_in_segments(first, last):
                o_ref[pl.ds(dst, count), :] = src_ref[k, pl.ds(src, count), :]

    outs = pl.pallas_call(
        body, name=name, grid=(1,), in_specs=[pl.BlockSpec((N_DEV, rows, LANES), lambda i: (0, off // rows, 0))],
        out_specs=[pl.BlockSpec((n, LANES), lambda i: (0, 0)) for n in sizes],
        out_shape=[jax.ShapeDtypeStruct((n, LANES), gathered.dtype) for n in sizes],
        compiler_params=_params(("arbitrary",)),
    )(gathered)
    return {piece: o for (piece, _, _), o in zip(W_IN_PIECES, outs)}


def unpack_full_weights(gathered, names):
    out, base = {}, _group_span(names)[0]
    for name in names:
        off, rows, valid, width, form, shape = LAYOUT[name]
        seg = gathered[:, off - base:off - base + rows]
        if form == "flat":
            flat = seg[:, 0, :width].astype(F32) + seg[:, 1, :width].astype(F32)
            out[name] = flat.reshape((N_DEV,) + shape).transpose(1, 0, 2).reshape(shape[0], N_DEV * shape[1])
        elif name == "w_in":
            out.update(w_in_pieces("w_in_pieces", gathered, off - base, rows))
        else:
            out[name] = seg[:, :valid, :width].reshape(N_DEV * valid, width)
    return out


def pack_full_grads(grads, names, me):
    wire, own, used = [], [], 0
    for name in names:
        off, rows, valid, width, form, shape = LAYOUT[name]
        if form == "flat":
            full = grads[name].reshape(shape[0], N_DEV, shape[1]).transpose(1, 0, 2).reshape(N_DEV, 1, width)
        elif name == "w_in":
            full = jnp.concatenate([grads[piece][:last - first] for piece, first, last in W_IN_PIECES], axis=0)
            full = full.reshape(N_DEV, valid, width)
        else:
            full = grads[name].reshape(N_DEV, valid, width)
        pad = ((0, rows - full.shape[1]), (0, LANES - width))
        wire.append(jnp.pad(full.astype(WIRE), ((0, 0),) + pad))
        own.append(jnp.pad(lax.dynamic_index_in_dim(full, me, 0, keepdims=False), pad))
        used += rows
    if _group_span(names)[1] > used:
        wire.append(jnp.zeros((N_DEV, _group_span(names)[1] - used, LANES), WIRE))
        own.append(jnp.zeros((_group_span(names)[1] - used, LANES), F32))
    return jnp.concatenate(wire, axis=1), jnp.concatenate(own, axis=0)


TRANSPOSED = ("ffn1_w_gate", "ffn1_w_up", "ffn2_w_gate", "ffn2_w_up", "w_in")


def unpack_grad_shards(packed, names):
    out, base = {}, _group_span(names)[0]
    for name in names:
        off, rows, valid, width, form, shape = LAYOUT[name]
        off -= base
        if form == "flat":
            out[name] = packed[off, :width].reshape(shape)
        elif name in TRANSPOSED:
            out[name] = packed[off:off + valid, :width]
        elif form == "cols":
            out[name] = packed[off:off + valid, :width].T
        else:
            out[name] = packed[off:off + valid, :width]
    return out


SMALL_SHAPES = {n: (1024,) for n in REPLICATED}
SMALL_SHAPES.update(pool_w=(4, 128, 128), pool_scale=(512,), dn_norm_w=(128,), a_log=(4,), dt_bias=(4,))


SMALL_SHAPES["loss"] = (1,)
SMALL_NAMES = REPLICATED + ["loss"]


def _small_layout():
    off, table = 0, {}
    for name in SMALL_NAMES:
        numel = 1
        for d in SMALL_SHAPES[name]:
            numel *= d
        rows = _round_up(-(-numel // LANES), 8)
        table[name] = (off, rows, numel)
        off += rows
    return table, off


SMALL_LAYOUT, SMALL_ROWS = _small_layout()


def _to_rows(flat, rows):
    return jnp.pad(flat, (0, rows * LANES - flat.shape[0])).reshape(rows, LANES)


def pack_small(values):
    return jnp.concatenate([_to_rows(values[name].reshape(-1), SMALL_LAYOUT[name][1]) for name in SMALL_NAMES], axis=0)


def unpack_small(packed):
    out = {}
    for name in SMALL_NAMES:
        off, rows, numel = SMALL_LAYOUT[name]
        out[name] = packed[off:off + rows].reshape(-1)[:numel].reshape(SMALL_SHAPES[name])
    return out


MESH = pl.DeviceIdType.MESH


def _position():
    return lax.axis_index("x"), lax.axis_index("y"), lax.axis_index("c")


def _other_chips(x, y):
    return [(1 - x, y), (x, 1 - y), (1 - x, 1 - y)]


def all_gather(name, block):
    rows, n = block.shape

    def body(x_ref, out_ref, send_sems, recv_sems, local_sem):
        x, y, c = _position()
        me, sibling = (x, y, c), (x, y, 1 - c)
        chips = _other_chips(x, y)

        def slot(px, py, pc):
            return out_ref.at[4 * px + 2 * py + pc]

        def copy(k, blk, to, src=None):
            return pltpu.make_async_remote_copy(
                src_ref=slot(*blk) if src is None else src, dst_ref=slot(*blk),
                send_sem=send_sems.at[k], recv_sem=recv_sems.at[k], device_id=to, device_id_type=MESH)

        mine = pltpu.make_async_copy(x_ref, slot(*me), local_sem)
        mine.start()
        first = [copy(0, me, sibling, src=x_ref)]
        first += [copy(1 + j, me, (*chip, c), src=x_ref) for j, chip in enumerate(chips)]
        for cp in first:
            cp.start()
        passed = [copy(4 + j, (*chip, c), sibling) for j, chip in enumerate(chips)]
        for j, chip in enumerate(chips):
            copy(1 + j, (*chip, c), me).wait_recv()
            passed[j].start()
        copy(0, sibling, me).wait_recv()
        for j, chip in enumerate(chips):
            copy(4 + j, (*chip, 1 - c), me).wait_recv()
        for cp in first + passed:
            cp.wait_send()
        mine.wait()

    return pl.pallas_call(
        body, name=name, out_shape=jax.ShapeDtypeStruct((N_DEV, rows, n), block.dtype),
        in_specs=[ANY], out_specs=ANY,
        scratch_shapes=[pltpu.SemaphoreType.DMA((7,)), pltpu.SemaphoreType.DMA((7,)), pltpu.SemaphoreType.DMA(())],
    )(block)


HBM = pl.BlockSpec(memory_space=pltpu.HBM)
SEM = pl.BlockSpec(memory_space=pltpu.SEMAPHORE)
EFFECT = pltpu.SideEffectType.DATAFLOW_SIDE_EFFECTING


def _remote(src, dst, send_sem, recv_sem, to):
    return pltpu.make_async_remote_copy(src_ref=src, dst_ref=dst, send_sem=send_sem, recv_sem=recv_sem,
                                        device_id=to, device_id_type=MESH)


def split_start(name, bufs, n, make_copies):
    nb = len(bufs)

    def body(*refs):
        for out_cp, _ in make_copies(refs[:nb], refs[nb:nb + n], refs[nb + n:nb + 2 * n]):
            out_cp.start()
        refs[-1][...] = jnp.zeros_like(refs[-1])

    outs = pl.pallas_call(
        body, name=name,
        out_shape=tuple([pltpu.SemaphoreType.DMA(())] * (2 * n)) + tuple(pltpu.HBM(b.shape, b.dtype) for b in bufs)
        + (jax.ShapeDtypeStruct((8, 128), F32),),
        in_specs=[HBM] * nb,
        out_specs=tuple([SEM] * (2 * n) + [HBM] * nb + [pl.BlockSpec(memory_space=pltpu.VMEM)]),
        input_output_aliases={i: 2 * n + i for i in range(nb)},
        compiler_params=pltpu.CompilerParams(has_side_effects=EFFECT),
    )(*[pltpu.with_memory_space_constraint(b, pltpu.HBM) for b in bufs])
    return list(outs[:2 * n]), list(outs[2 * n:2 * n + nb]), outs[-1]


def split_wait(name, bufs, sems, n, make_copies, after):
    nb = len(bufs)

    def body(*refs):
        for out_cp, in_cp in make_copies(refs[:nb], refs[nb:nb + n], refs[nb + n:nb + 2 * n]):
            out_cp.wait_send()
            in_cp.wait_recv()

    outs = pl.pallas_call(
        body, name=name, out_shape=tuple(pltpu.HBM(b.shape, b.dtype) for b in bufs),
        in_specs=[HBM] * nb + [SEM] * (2 * n) + [ANY], out_specs=tuple([HBM] * nb),
        input_output_aliases={i: i for i in range(nb)},
        compiler_params=pltpu.CompilerParams(has_side_effects=EFFECT),
    )(*bufs, *sems, after)
    return list(outs)


def _gather_stage1(refs, send, recv):
    src, land = refs
    x, y, c = _position()
    peers = [(x, y, 1 - c)] + [(*chip, c) for chip in _other_chips(x, y)]
    return [(_remote(src, land.at[4 * x + 2 * y + c], send[k], recv[k], p),
             _remote(src, land.at[4 * p[0] + 2 * p[1] + p[2]], send[k], recv[k], p)) for k, p in enumerate(peers)]


def _gather_stage2(refs, send, recv):
    (land,) = refs
    x, y, c = _position()
    out = []
    for j, (px, py) in enumerate(_other_chips(x, y)):
        mine, theirs = land.at[4 * px + 2 * py + c], land.at[4 * px + 2 * py + 1 - c]
        out.append((_remote(mine, mine, send[j], recv[j], (x, y, 1 - c)),
                    _remote(theirs, theirs, send[j], recv[j], (x, y, 1 - c))))
    return out


def _flips():
    return [(a, b, d) for a in (0, 1) for b in (0, 1) for d in (0, 1) if a | b | d]


def _gather_direct(refs, send, recv):
    src, land = refs
    x, y, c = _position()
    out = []
    for k, (fx, fy, fc) in enumerate(_flips()):
        p = (1 - x if fx else x, 1 - y if fy else y, 1 - c if fc else c)
        out.append((_remote(src, land.at[4 * x + 2 * y + c], send[k], recv[k], p),
                    _remote(src, land.at[4 * p[0] + 2 * p[1] + p[2]], send[k], recv[k], p)))
    return out


def _scatter_direct(refs, send, recv):
    sendbuf, land = refs
    x, y, c = _position()
    out = []
    for k, (fx, fy, fc) in enumerate(_flips()):
        p = (1 - x if fx else x, 1 - y if fy else y, 1 - c if fc else c)
        cp = _remote(sendbuf.at[4 * p[0] + 2 * p[1] + p[2]], land.at[k], send[k], recv[k], p)
        out.append((cp, cp))
    return out


def _own_plus_slots(name, own, landed):
    n, rows, _ = landed.shape
    tr = _row_block(rows)

    def body(g_ref, l_ref, o_ref):
        acc = g_ref[...]
        for j in range(n):
            acc = acc + l_ref[j].astype(F32)
        o_ref[...] = acc

    return pl.pallas_call(
        body, name=name, grid=(rows // tr,),
        in_specs=[pl.BlockSpec((tr, LANES), lambda i: (i, 0)), pl.BlockSpec((n, tr, LANES), lambda i: (0, i, 0))],
        out_specs=pl.BlockSpec((tr, LANES), lambda i: (i, 0)),
        out_shape=jax.ShapeDtypeStruct((rows, LANES), F32), compiler_params=_params(("parallel",)),
    )(own, landed)


def _sum_slots(name, stack):
    n, rows, _ = stack.shape

    def body(s_ref, o_ref):
        acc = s_ref[0]
        for j in range(1, n):
            acc = acc + s_ref[j]
        o_ref[...] = acc

    return pl.pallas_call(
        body, name=name, in_specs=[pl.BlockSpec(stack.shape, lambda: (0, 0, 0))],
        out_specs=pl.BlockSpec((rows, LANES), lambda: (0, 0)), out_shape=jax.ShapeDtypeStruct((rows, LANES), F32),
    )(stack)


def adamw(name, w, g, m, v):
    shape = w.shape
    last = shape[-1]
    w2, g2, m2, v2 = [a.reshape(-1, last) for a in (w, g, m, v)]
    rows = w2.shape[0]
    tr = _pick(rows, (256, 176, 128))

    def body(w_ref, g_ref, m_ref, v_ref, d_ref, nm_ref, nv_ref):
        gg = g_ref[...]
        nm = ADAM_B1 * m_ref[...] + (1.0 - ADAM_B1) * gg
        nv = ADAM_B2 * v_ref[...] + (1.0 - ADAM_B2) * (gg * gg)
        m_hat = nm / (1.0 - ADAM_B1 ** ADAM_STEP)
        v_hat = nv / (1.0 - ADAM_B2 ** ADAM_STEP)
        d_ref[...] = -ADAM_LR * (m_hat / (jnp.sqrt(v_hat) + ADAM_EPS) + ADAM_WD * w_ref[...])
        nm_ref[...] = nm
        nv_ref[...] = nv

    spec = pl.BlockSpec((tr, last), lambda i: (i, 0))
    outs = pl.pallas_call(
        body, name=name, grid=(rows // tr,), in_specs=[spec] * 4, out_specs=[spec] * 3,
        out_shape=[jax.ShapeDtypeStruct((rows, last), F32)] * 3, compiler_params=_params(("parallel",)),
    )(w2, g2, m2, v2)
    return [o.reshape(shape) for o in outs]


def _landing(block_shape, dtype, own):
    x, y, c = _position()
    return lax.dynamic_update_slice(lax.empty((N_DEV,) + block_shape, dtype), own[None], (4 * x + 2 * y + c, 0, 0))


class _Exchanges:
    def __init__(self, shards):
        self.shards = shards
        self.pending = {}
        self.reduced = {}

    def first_weights(self):
        names = GROUPS["ffn1_gu"]
        return unpack_full_weights(all_gather("ag_ffn1_gu", pack_weight_shards(self.shards, names)), names)

    def rest_started(self):
        tokens = []
        block = pack_weight_shards(self.shards, GROUPS["ffn1_d"])
        sems, bufs, token = split_start("ag_ffn1_d_s", [block, _landing(block.shape, block.dtype, block)], N_DEV - 1,
                                        _gather_direct)
        self.pending["ffn1_d"] = (sems, bufs)
        tokens.append(token)
        for key in ("mixer", "xa", "ffn2"):
            block = pack_weight_shards(self.shards, GROUPS[key])
            sems, bufs, token = split_start(f"ag_{key}_s1", [block, _landing(block.shape, block.dtype, block)], 4,
                                            _gather_stage1)
            self.pending[key] = (sems, bufs)
            tokens.append(token)
        return tuple(tokens)

    def ffn1_down(self, after):
        sems, bufs = self.pending.pop("ffn1_d")
        _, gathered = split_wait("ag_ffn1_d_w", bufs, sems, N_DEV - 1, _gather_direct, after)
        return unpack_full_weights(gathered, GROUPS["ffn1_d"])["ffn1_w_down"]

    def halfway(self, key, after):
        sems, bufs = self.pending.pop(key)
        _, land = split_wait(f"ag_{key}_w1", bufs, sems, 4, _gather_stage1, after)
        sems, bufs, token = split_start(f"ag_{key}_s2", [land], 3, _gather_stage2)
        self.pending[key] = (sems, bufs)
        return (token,)

    def weights(self, key, after):
        sems, bufs = self.pending.pop(key)
        (gathered,) = split_wait(f"ag_{key}_w2", bufs, sems, 3, _gather_stage2, after)
        return unpack_full_weights(gathered, GROUPS[key])

    def grads_out(self, key, grads):
        x, y, c = _position()
        wire, own = pack_full_grads(grads, GROUPS[key], 4 * x + 2 * y + c)
        land = lax.empty((N_DEV - 1,) + wire.shape[1:], WIRE)
        sems, bufs, token = split_start(f"rs_{key}_start", [wire, land], N_DEV - 1, _scatter_direct)
        self.pending[key] = (sems, bufs, own)
        return (token,)

    def grads_in(self, key, after):
        sems, bufs, own = self.pending.pop(key)
        _, landed = split_wait(f"rs_{key}_wait", bufs, sems, N_DEV - 1, _scatter_direct, after)
        self.reduced.update(unpack_grad_shards(_own_plus_slots(f"rs_{key}_sum", own, landed), GROUPS[key]))

    def small_out(self, values):
        block = pack_small(values)
        sems, bufs, token = split_start("ag_small_s", [block, _landing(block.shape, block.dtype, block)], N_DEV - 1,
                                        _gather_direct)
        self.pending["small"] = (sems, bufs)
        return (token,)

    def small_in(self, after):
        sems, bufs = self.pending.pop("small")
        _, gathered = split_wait("ag_small_w", bufs, sems, N_DEV - 1, _gather_direct, after)
        return unpack_small(_sum_slots("small_sum", gathered))


def kernel(x, mem, ffn1_w_gate, ffn1_w_up, ffn1_w_down, ln1_g, ln1_b, w_in, conv_w, a_log, dt_bias, dn_norm_w, w_dn_branch, pool_w, pool_scale, w_pool_branch, w_mix_out, ln2_g, ln2_b, mem_ln_g, mem_ln_b, xa_wq, xa_wk, xa_wv, xa_wo, ln3_g, ln3_b, ffn2_w_gate, ffn2_w_up, ffn2_w_down, ln4_g, ln4_b, loss_target, m_ffn1_w_gate, m_ffn1_w_up, m_ffn1_w_down, m_ln1_g, m_ln1_b, m_w_in, m_conv_w, m_a_log, m_dt_bias, m_dn_norm_w, m_w_dn_branch, m_pool_w, m_pool_scale, m_w_pool_branch, m_w_mix_out, m_ln2_g, m_ln2_b, m_mem_ln_g, m_mem_ln_b, m_xa_wq, m_xa_wk, m_xa_wv, m_xa_wo, m_ln3_g, m_ln3_b, m_ffn2_w_gate, m_ffn2_w_up, m_ffn2_w_down, m_ln4_g, m_ln4_b, v_ffn1_w_gate, v_ffn1_w_up, v_ffn1_w_down, v_ln1_g, v_ln1_b, v_w_in, v_conv_w, v_a_log, v_dt_bias, v_dn_norm_w, v_w_dn_branch, v_pool_w, v_pool_scale, v_w_pool_branch, v_w_mix_out, v_ln2_g, v_ln2_b, v_mem_ln_g, v_mem_ln_b, v_xa_wq, v_xa_wk, v_xa_wv, v_xa_wo, v_ln3_g, v_ln3_b, v_ffn2_w_gate, v_ffn2_w_up, v_ffn2_w_down, v_ln4_g, v_ln4_b):
    given = dict(locals())
    shards = {n: given[n] for n in WEIGHT_NAMES}
    io = _Exchanges({n: shards[n][0] for n, _, _ in SHARDED})
    w = io.first_weights()
    for n in REPLICATED:
        w[n] = shards[n][0] if n == "pool_w" else shards[n]
    loss_part, grad_x, g = local_step(x[0], mem[0], loss_target[0], w, io)

    grad, updates = {}, {}

    def update(names, reduced):
        for n in names:
            if n in TRANSPOSED:
                outs = adamw("adamw_" + n, shards[n][0].T, reduced[n], given["m_" + n][0].T, given["v_" + n][0].T)
                grad[n], updates[n] = reduced[n].T[None], [o.T[None] for o in outs]
            else:
                grad[n] = reduced[n].reshape(shards[n].shape)
                updates[n] = adamw("adamw_" + n, shards[n], grad[n], given["m_" + n], given["v_" + n])
        return updates[names[-1]][0]

    update(GROUPS["ffn2"] + GROUPS["xa"], io.reduced)
    io.grads_in("mixer", grad_x)
    done = update(GROUPS["mixer"], io.reduced)
    small = io.small_in(done)
    loss = small.pop("loss")[0]
    done = update(REPLICATED, small)
    for n in ("ffn1_w_down", "ffn1_w_gate", "ffn1_w_up"):
        io.grads_in(n, done)
        done = update(GROUPS[n], io.reduced)
    return (loss, grad_x[None], *[grad[n] for n in WEIGHT_NAMES], *[updates[n][0] for n in WEIGHT_NAMES],
            *[updates[n][1] for n in WEIGHT_NAMES], *[updates[n][2] for n in WEIGHT_NAMES])
```

```python
import jax
import jax.numpy as jnp
from jax import lax
from jax.experimental import pallas as pl
from jax.experimental.pallas import tpu as pltpu

F32 = jnp.float32
BF16 = jnp.bfloat16
MMD = BF16
WIRE = BF16
X3 =lax.Precision.HIGH
VMEM_LIMIT_BYTES = 48 * 1024 * 1024

D_MODEL = 1024
D_FF = 2816
CHUNK = 64
N_MEM = 256
DN_HEADS = 4
HD = 128
DN_WIDTH = 512
POOL_WINDOWS = (2, 4, 8, 16)
POOL_WIDTH = 512
XA_HEADS = 4
XA_HD = 256
LN_EPS = 1e-5
RMS_EPS = 1e-6
L2_EPS = 1e-6
ALPHA = 2.0 ** 0.25
HALO = 16
ROWS = 512
ROWS_LIGHT = 1024
ROWS_WIDE = 256

ADAM_LR = 0.001
ADAM_B1 = 0.9
ADAM_B2 = 0.999
ADAM_EPS = 1e-08
ADAM_WD = 0.01
ADAM_STEP = 10

N_DEV = 8
LANES = 1024
ANY = pl.BlockSpec(memory_space=pl.ANY)


def _dot(a, b, ca, cb, prec):
    dn = (((ca,), (cb,)), ((), ()))
    if prec is not None:
        return lax.dot_general(a.astype(F32), b.astype(F32), dn, precision=prec, preferred_element_type=F32)
    return lax.dot_general(a.astype(MMD), b.astype(MMD), dn, preferred_element_type=F32)


def dnn(a, b, prec=None):
    return _dot(a, b, 1, 0, prec)


def dnt(a, b, prec=None):
    return _dot(a, b, 1, 1, prec)


def dtn(a, b, prec=None):
    return _dot(a, b, 0, 0, prec)


def _sigmoid(x):
    return jax.nn.sigmoid(x)


def _silu(x):
    return x * _sigmoid(x)


def _dsilu(x):
    s = _sigmoid(x)
    return s * (1.0 + x * (1.0 - s))


def _softplus(x):
    return jnp.maximum(x, 0.0) + jnp.log1p(jnp.exp(-jnp.abs(x)))


def _iota(shape, dim):
    return lax.broadcasted_iota(jnp.int32, shape, dim)


def _rsum(x):
    return jnp.sum(x, axis=1, keepdims=True)


def _csum(x):
    return jnp.sum(x, axis=0, keepdims=True)


def _pick(n, cands):
    for c in cands:
        if n % c == 0:
            return c
    return n


def _params(sem):
    return pltpu.CompilerParams(dimension_semantics=sem, vmem_limit_bytes=VMEM_LIMIT_BYTES)


MM_TILE_SIZES = (4096, 2816, 2048, 1536, 1408, 1024, 768, 512, 384, 256, 128)
MM_VMEM_BUDGET = 36 * 1024 * 1024
HBM_BYTES_PER_US = 3.0e6
GRID_STEP_US = 0.35


def _mm_tiles(m, n, kc, a_bytes, b_bytes, o_bytes):
    def sizes(d):
        return [d] if d <= 512 else [t for t in MM_TILE_SIZES if d % t == 0]

    best = None
    for tm in sizes(m):
        for tn in sizes(n):
            for tk in sizes(kc):
                vmem = 2 * (tm * tk * a_bytes + tk * tn * b_bytes + tm * tn * o_bytes) + tm * tn * 4
                if vmem > MM_VMEM_BUDGET:
                    continue
                steps = (m // tm) * (n // tn) * (kc // tk)
                traffic = m * kc * a_bytes * (n // tn) + kc * n * b_bytes * (m // tm) + m * n * o_bytes
                edge = tm * tk * a_bytes + tk * tn * b_bytes + tm * tn * o_bytes
                cost = (traffic + edge) / HBM_BYTES_PER_US + steps * GRID_STEP_US
                if best is None or cost < best[0]:
                    best = (cost, tm, tn, tk)
    return best[1:]


def mm(name, a, b, *, ta=False, tb=False, out_dtype=F32, add=None, scale=None, deps=()):
    adds = [] if add is None else (list(add) if isinstance(add, (list, tuple)) else [(1.0, add)])
    if ta:
        kc, m = a.shape
    else:
        m, kc = a.shape
    if tb:
        n, kb = b.shape
    else:
        kb, n = b.shape
    assert kc == kb, (name, a.shape, b.shape)
    tm, tn, tk = _mm_tiles(m, n, kc, a.dtype.itemsize, b.dtype.itemsize,
                           jnp.dtype(out_dtype).itemsize * (1 + len(adds)))
    nk = kc // tk
    grid = (m // tm, n // tn, nk)
    a_spec = pl.BlockSpec((tk, tm), lambda i, j, k: (k, i)) if ta else pl.BlockSpec((tm, tk), lambda i, j, k: (i, k))
    b_spec = pl.BlockSpec((tn, tk), lambda i, j, k: (j, k)) if tb else pl.BlockSpec((tk, tn), lambda i, j, k: (k, j))
    o_spec = pl.BlockSpec((tm, tn), lambda i, j, k: (i, j))
    ca, cb = (0 if ta else 1), (1 if tb else 0)

    def body(*refs):
        a_ref, b_ref = refs[0], refs[1]
        o_ref = refs[-1] if nk == 1 else refs[-2]
        k = pl.program_id(2)
        part = _dot(a_ref[...], b_ref[...], ca, cb, None)

        def finish(r):
            if scale is not None:
                r = r * scale
            for (coef, _), add_ref in zip(adds, refs[2:2 + len(adds)]):
                r = r + (add_ref[...] if coef == 1.0 else coef * add_ref[...])
            o_ref[...] = r.astype(o_ref.dtype)

        if nk == 1:
            finish(part)
            return
        acc_ref = refs[-1]

        @pl.when(k == 0)
        def _():
            acc_ref[...] = part

        if nk > 2:
            @pl.when((k > 0) & (k < nk - 1))
            def _():
                acc_ref[...] += part

        @pl.when(k == nk - 1)
        def _():
            finish(acc_ref[...] + part)

    ins = [a, b] + [t for _, t in adds] + list(deps)
    specs = [a_spec, b_spec] + [o_spec] * len(adds) + [ANY] * len(deps)
    return pl.pallas_call(
        body, name=name, grid=grid, in_specs=specs, out_specs=o_spec,
        out_shape=jax.ShapeDtypeStruct((m, n), out_dtype),
        scratch_shapes=[pltpu.VMEM((tm, tn), F32)] if nk > 1 else [],
        compiler_params=_params(("parallel", "parallel", "arbitrary")),
    )(*ins)


def mm_fan_t(name, lefts, b):
    kc, n = b.shape
    tk = min(512, kc)
    nk = kc // tk

    def body(*refs):
        k = pl.program_id(0)
        bb = refs[len(lefts)][...].astype(MMD)
        for a_ref, o_ref in zip(refs[:len(lefts)], refs[len(lefts) + 1:]):
            part = dtn(a_ref[...], bb)

            @pl.when(k == 0)
            def _(o_ref=o_ref, part=part):
                o_ref[...] = part

            @pl.when(k > 0)
            def _(o_ref=o_ref, part=part):
                o_ref[...] += part

    return pl.pallas_call(
        body, name=name, grid=(nk,),
        in_specs=[pl.BlockSpec((tk, a.shape[1]), lambda k: (k, 0)) for a in lefts] + [pl.BlockSpec((tk, n), lambda k: (k, 0))],
        out_specs=[pl.BlockSpec((a.shape[1], n), lambda k: (0, 0)) for a in lefts],
        out_shape=[jax.ShapeDtypeStruct((a.shape[1], n), F32) for a in lefts],
        compiler_params=_params(("arbitrary",)),
    )(*lefts, b)


def mm_sum(name, pairs, deps=()):
    m, n = pairs[0][0].shape[0], pairs[0][1].shape[1]
    tm = min(512, m)
    np_ = len(pairs)

    def body(*refs):
        acc = dnn(refs[0][...], refs[1][...])
        for p in range(1, np_):
            acc = acc + dnn(refs[2 * p][...], refs[2 * p + 1][...])
        refs[-1][...] = acc

    specs, ins = [], []
    for a, b in pairs:
        specs += [pl.BlockSpec((tm, a.shape[1]), lambda i: (i, 0)), pl.BlockSpec(b.shape, lambda i: (0, 0))]
        ins += [a, b]
    return pl.pallas_call(
        body, name=name, grid=(m // tm,), in_specs=specs + [ANY] * len(deps),
        out_specs=pl.BlockSpec((tm, n), lambda i: (i, 0)), out_shape=jax.ShapeDtypeStruct((m, n), F32),
        compiler_params=_params(("parallel",)),
    )(*ins, *deps)


class _Ctx:
    def __init__(self, i, nblk, tl):
        self.i, self.nblk, self.tl = i, nblk, tl


def _norm_item(it):
    if isinstance(it, tuple):
        a, w, j = it[:3]
        rows = it[3] if len(it) > 3 else None
        return a, w, j, rows
    return it, it.shape[-1], 0, None


def rowwise(name, fn, length, tl, *, rows=(), consts=(), prevs=(), nexts=(), out_rows=(), out_accs=(), deps=()):
    nblk = length // tl
    hb = tl // HALO
    nhalo = length // HALO
    arrays, specs = [], []
    for it in rows:
        a, w, j, r = _norm_item(it)
        if a.ndim == 3:
            specs.append(pl.BlockSpec((a.shape[0], tl, w), lambda i, j=j: (0, i, j)))
        else:
            specs.append(pl.BlockSpec((r or tl, w), lambda i, j=j: (i, j)))
        arrays.append(a)
    for a in consts:
        specs.append(pl.BlockSpec(a.shape, lambda i, nd=a.ndim: (0,) * nd))
        arrays.append(a)
    for it in prevs:
        a, w, j, _ = _norm_item(it)
        specs.append(pl.BlockSpec((HALO, w), lambda i, j=j: (jnp.maximum(i * hb - 1, 0), j)))
        arrays.append(a)
    for it in nexts:
        a, w, j, _ = _norm_item(it)
        specs.append(pl.BlockSpec((HALO, w), lambda i, j=j: (jnp.minimum((i + 1) * hb, nhalo - 1), j)))
        arrays.append(a)
    out_shape, out_specs = [], []
    for spec in out_rows:
        if len(spec) == 3:
            h, w, dt = spec
            out_shape.append(jax.ShapeDtypeStruct((h, length, w), dt))
            out_specs.append(pl.BlockSpec((h, tl, w), lambda i: (0, i, 0)))
        else:
            w, dt = spec
            out_shape.append(jax.ShapeDtypeStruct((length, w), dt))
            out_specs.append(pl.BlockSpec((tl, w), lambda i: (i, 0)))
    for shape, dt in out_accs:
        out_shape.append(jax.ShapeDtypeStruct(shape, dt))
        out_specs.append(pl.BlockSpec(shape, lambda i, nd=len(shape): (0,) * nd))
    n_r, n_c, n_p, n_n = len(rows), len(consts), len(prevs), len(nexts)
    n_in = n_r + n_c + n_p + n_n
    n_or = len(out_rows)
    arrays, specs = arrays + list(deps), specs + [ANY] * len(deps)

    def body(*refs):
        i = pl.program_id(0)
        vals = [r[...] for r in refs[:n_in]]
        outs = refs[n_in + len(deps):]
        ctx = _Ctx(i, nblk, tl)
        ro, ao = fn(ctx, vals[:n_r], vals[n_r:n_r + n_c], vals[n_r + n_c:n_r + n_c + n_p], vals[n_r + n_c + n_p:])
        for r, v in zip(outs[:n_or], ro, strict=True):
            r[...] = v.astype(r.dtype)
        for r, v in zip(outs[n_or:], ao, strict=True):
            @pl.when(i == 0)
            def _(r=r, v=v):
                r[...] = v.astype(r.dtype)

            @pl.when(i > 0)
            def _(r=r, v=v):
                r[...] += v.astype(r.dtype)

    res = pl.pallas_call(
        body, name=name, grid=(nblk,), in_specs=specs, out_specs=out_specs, out_shape=out_shape,
        compiler_params=_params(("arbitrary",) if out_accs else ("parallel",)),
    )(*arrays)
    return res


def _heads(x, n, w):
    return [x[:, h * w:(h + 1) * w] for h in range(n)]


def _cat(xs):
    return jnp.concatenate(xs, axis=1)


def _row_index(ctx, nrows, offset=0):
    return ctx.i * ctx.tl + offset + _iota((nrows, 1), 0)


def _ln_stats(r):
    mu = jnp.mean(r, axis=1, keepdims=True)
    d = r - mu
    var = jnp.mean(d * d, axis=1, keepdims=True)
    rstd = lax.rsqrt(var + LN_EPS)
    return d * rstd, rstd


def ln_fwd(name, terms, g, b, tl=ROWS_LIGHT, deps=()):
    coefs = [c for c, _ in terms]
    length = terms[0][1].shape[0]

    def fn(ctx, rows, consts, prevs, nexts):
        r = sum(c * t for c, t in zip(coefs, rows))
        xh, _ = _ln_stats(r)
        y = xh * consts[0] + consts[1]
        return [y, r, y], []

    return rowwise(name, fn, length, min(tl, length), rows=[t for _, t in terms], consts=[g, b],
                   out_rows=[(D_MODEL, F32), (D_MODEL, F32), (D_MODEL, BF16)], deps=deps)


def ln_bwd(name, r, terms, g, tl=ROWS_LIGHT, deps=()):
    coefs = [c for c, _ in terms]
    length = r.shape[0]

    def fn(ctx, rows, consts, prevs, nexts):
        xh, rstd = _ln_stats(rows[0])
        dy = sum(c * t for c, t in zip(coefs, rows[1:]))
        dxh = dy * consts[0]
        dr = rstd * (dxh - jnp.mean(dxh, axis=1, keepdims=True) - xh * jnp.mean(dxh * xh, axis=1, keepdims=True))
        return [dr], [_csum(dy * xh), _csum(dy)]

    return rowwise(name, fn, length, min(tl, length), rows=[r] + [t for _, t in terms], consts=[g],
                   out_rows=[(D_MODEL, F32)], out_accs=[((1, D_MODEL), F32), ((1, D_MODEL), F32)], deps=deps)


def ln_loss(name, terms, g, b, target, tl=ROWS_LIGHT):
    coefs = [c for c, _ in terms]
    length = target.shape[0]
    nt = len(terms)

    def fn(ctx, rows, consts, prevs, nexts):
        r = sum(c * t for c, t in zip(coefs, rows[:nt]))
        xh, _ = _ln_stats(r)
        err = xh * consts[0] + consts[1] - rows[nt]
        tot = _csum(_rsum(err * err)) * (0.5 / D_MODEL)
        return [err * (1.0 / D_MODEL), r], [jnp.broadcast_to(tot, (1, 128))]

    return rowwise(name, fn, length, min(tl, length), rows=[t for _, t in terms] + [target], consts=[g, b],
                   out_rows=[(D_MODEL, F32), (D_MODEL, F32)], out_accs=[((1, 128), F32)])


def _ffn_blocks(length):
    return min(512, length), D_FF // 2


def ffn_gate_up_act(name, x, wg, wu, deps=()):
    length = x.shape[0]
    tm, tn = _ffn_blocks(length)

    def body(x_ref, wg_ref, wu_ref, *rest):
        hg_ref, hu_ref, act_ref = rest[-3:]
        xb = x_ref[...].astype(MMD)
        hg = dnt(xb, wg_ref[...])
        hu = dnt(xb, wu_ref[...])
        hg_ref[...] = hg
        hu_ref[...] = hu
        act_ref[...] = (_silu(hg) * hu).astype(act_ref.dtype)

    row = pl.BlockSpec((tm, D_MODEL), lambda i, j: (i, 0))
    wsp = pl.BlockSpec((tn, D_MODEL), lambda i, j: (j, 0))
    osp = pl.BlockSpec((tm, tn), lambda i, j: (i, j))
    return pl.pallas_call(
        body, name=name, grid=(length // tm, D_FF // tn), in_specs=[row, wsp, wsp] + [ANY] * len(deps),
        out_specs=[osp] * 3,
        out_shape=[jax.ShapeDtypeStruct((length, D_FF), F32)] * 2 + [jax.ShapeDtypeStruct((length, D_FF), BF16)],
        compiler_params=_params(("parallel", "parallel")),
    )(x, wg, wu, *deps)


def ffn_dact(name, dr, wd, hg, hu, deps=()):
    length = dr.shape[0]
    tm, tn = _ffn_blocks(length)

    def body(dr_ref, wd_ref, hg_ref, hu_ref, *rest):
        dhg_ref, dhu_ref = rest[-2:]
        da = 0.5 * dnt(dr_ref[...], wd_ref[...])
        g = hg_ref[...]
        s = _sigmoid(g)
        dhg_ref[...] = (da * hu_ref[...] * (s * (1.0 + g * (1.0 - s)))).astype(dhg_ref.dtype)
        dhu_ref[...] = (da * (g * s)).astype(dhu_ref.dtype)

    row = pl.BlockSpec((tm, D_MODEL), lambda i, j: (i, 0))
    wsp = pl.BlockSpec((tn, D_MODEL), lambda i, j: (j, 0))
    osp = pl.BlockSpec((tm, tn), lambda i, j: (i, j))
    return pl.pallas_call(
        body, name=name, grid=(length // tm, D_FF // tn), in_specs=[row, wsp, osp, osp] + [ANY] * len(deps),
        out_specs=[osp] * 2, out_shape=[jax.ShapeDtypeStruct((length, D_FF), BF16)] * 2,
        compiler_params=_params(("parallel", "parallel")),
    )(dr, wd, hg, hu, *deps)


def ffn_fwd(tag, x, wg, wu, wd, deps=()):
    hg, hu, act = ffn_gate_up_act(tag + "_gate_up", x, wg, wu, deps)
    if callable(wd):
        wd = wd(act)
    f = mm(tag + "_down", act, wd)
    return f, (hg, hu, act), wd


def ffn_bwd(tag, x, res, dr, wg, wu, wd, deps=(), on_dw=None, also=None):
    on_dw = on_dw or (lambda which, dw: ())
    hg, hu, act = res
    dwd = mm(tag + "_dwd", act, dr, ta=True, scale=0.5, deps=deps)
    dhg, dhu = ffn_dact(tag + "_dact", dr, wd, hg, hu, deps=on_dw("down", dwd))
    dwg = mm(tag + "_dwg", dhg, x, ta=True)
    dwu = mm(tag + "_dwu", dhu, x, ta=True, deps=on_dw("gate", dwg))
    dx = mm(tag + "_dxg", dhg, wg, deps=on_dw("up", dwu))
    dx = mm(tag + "_dxu", dhu, wu, add=[(1.0, dx)] + ([also] if also else []))
    return dx, dwg, dwu, dwd


def _conv_taps(ext, taps, n):
    out = taps[3] * ext
    for j in range(3):
        out = out + taps[j] * pltpu.roll(ext, 3 - j, 0)
    return out


def _l2n(x):
    r = lax.rsqrt(_rsum(x * x) + L2_EPS)
    return x * r, r


def conv_fwd(name, pre, taps, tl=ROWS_WIDE, deps=()):
    length = pre.shape[0]
    tl = min(tl, length)

    def fn(ctx, rows, consts, prevs, nexts):
        prev = jnp.where(ctx.i > 0, prevs[0], 0.0)
        ext = jnp.concatenate([prev, rows[0]], axis=0)
        s = _silu(_conv_taps(ext, consts, tl + HALO)[HALO:])
        q = _cat([_l2n(x)[0] * (HD ** -0.5) for x in _heads(s[:, :DN_WIDTH], DN_HEADS, HD)])
        k = _cat([_l2n(x)[0] for x in _heads(s[:, DN_WIDTH:2 * DN_WIDTH], DN_HEADS, HD)])
        return [q, k, s[:, 2 * DN_WIDTH:]], []

    return rowwise(name, fn, length, tl, rows=[pre], consts=list(taps), prevs=[pre],
                   out_rows=[(DN_WIDTH, F32)] * 3, deps=deps)


def conv_bwd(name, pre, dq, dk, dv, taps, tl=ROWS_WIDE):
    length = pre.shape[0]
    tl = min(tl, length)
    n = tl + 2 * HALO

    def fn(ctx, rows, consts, prevs, nexts):
        last = ctx.i == ctx.nblk - 1
        prev = jnp.where(ctx.i > 0, prevs[0], 0.0)
        ext = jnp.concatenate([prev, rows[0], nexts[0]], axis=0)
        c = _conv_taps(ext, consts, n)
        sg = _sigmoid(c)
        s = c * sg
        zero = jnp.zeros((HALO, DN_WIDTH), F32)
        dqe, dke, dve = [jnp.concatenate([zero, rows[1 + t], jnp.where(last, 0.0, nexts[1 + t])], axis=0)
                         for t in range(3)]

        def l2_bwd(x, dy):
            y, r = _l2n(x)
            return r * (dy - y * _rsum(dy * y))

        dsq = _cat([l2_bwd(x, d * (HD ** -0.5)) for x, d in zip(_heads(s[:, :DN_WIDTH], DN_HEADS, HD),
                                                                 _heads(dqe, DN_HEADS, HD))])
        dsk = _cat([l2_bwd(x, d) for x, d in zip(_heads(s[:, DN_WIDTH:2 * DN_WIDTH], DN_HEADS, HD),
                                                  _heads(dke, DN_HEADS, HD))])
        dc = _cat([dsq, dsk, dve]) * (sg * (1.0 + c * (1.0 - sg)))
        dpre = consts[3] * dc
        for j in range(3):
            dpre = dpre + consts[j] * pltpu.roll(dc, n - (3 - j), 0)
        dc_cur = dc[HALO:HALO + tl]
        dws = [_csum(dc_cur * pltpu.roll(ext, 3 - j, 0)[HALO:HALO + tl]) for j in range(3)]
        dws.append(_csum(dc_cur * ext[HALO:HALO + tl]))
        return [dpre[HALO:HALO + tl]], dws

    return rowwise(name, fn, length, tl, rows=[pre, dq, dk, dv], consts=list(taps), prevs=[pre],
                   nexts=[pre, dq, dk, dv], out_rows=[(3 * DN_WIDTH, BF16)],
                   out_accs=[((1, 3 * DN_WIDTH), F32)] * 4)


def _gate_math(ab, alog, dtb):
    z = ab + dtb
    g = -jnp.exp(alog) * _softplus(z)
    beta = _sigmoid(ab)
    return z, g, beta


def gates_fwd(name, ab, alog, dtb, tl=ROWS_LIGHT):
    length = ab.shape[0]

    def fn(ctx, rows, consts, prevs, nexts):
        _, g, beta = _gate_math(rows[0], consts[0], consts[1])
        spread = [jnp.broadcast_to(v[:, h:h + 1], (v.shape[0], HD))
                  for v, first in ((g, 0), (beta, DN_HEADS)) for h in range(first, first + DN_HEADS)]
        return [_cat(spread[:DN_HEADS]), _cat(spread[DN_HEADS:])], []

    return rowwise(name, fn, length, min(tl, length), rows=[ab], consts=[alog, dtb],
                   out_rows=[(DN_WIDTH, F32)] * 2)


def gates_bwd(name, ab, dgb, dbb, alog, dtb, tl=ROWS_LIGHT):
    length = ab.shape[0]

    def fn(ctx, rows, consts, prevs, nexts):
        z, g, beta = _gate_math(rows[0], consts[0], consts[1])
        lane = _iota(g.shape, 1)
        dsmall = jnp.zeros_like(g)
        for h in range(DN_HEADS):
            dsmall = jnp.where(lane == h, rows[1][:, h * HD:h * HD + 1], dsmall)
            dsmall = jnp.where(lane == DN_HEADS + h, rows[2][:, h * HD:h * HD + 1], dsmall)
        is_a = lane < DN_HEADS
        da = jnp.where(is_a, dsmall * (-jnp.exp(consts[0])) * _sigmoid(z), 0.0)
        db = jnp.where((lane >= DN_HEADS) & (lane < 2 * DN_HEADS), dsmall * beta * (1.0 - beta), 0.0)
        return [da + db], [_csum(jnp.where(is_a, dsmall * g, 0.0)), _csum(da)]

    return rowwise(name, fn, length, min(tl, length), rows=[ab, dgb, dbb], consts=[alog, dtb],
                   out_rows=[(128, BF16)], out_accs=[((1, 128), F32)] * 2)


CPS = 4


def _chunk_scan_rows(x, suffix=False):
    n = x.shape[0]
    rc = _iota(x.shape, 0) & (CHUNK - 1)
    sh = 1
    while sh < CHUNK:
        if suffix:
            x = x + jnp.where(rc < CHUNK - sh, pltpu.roll(x, n - sh, 0), 0.0)
        else:
            x = x + jnp.where(rc >= sh, pltpu.roll(x, sh, 0), 0.0)
        sh *= 2
    return x


def _tri_inv(a_list, eye, bd):
    def each(f, *ls):
        return [f(*xs) for xs in zip(*ls)]

    dg = [jnp.where(bd, a, 0.0) for a in a_list]
    lo = each(lambda a, d: a - d, a_list, dg)
    n1 = [-d for d in dg]
    n2 = each(lambda n: dnn(n, n, X3), n1)
    n4 = each(lambda n: dnn(n, n, X3), n2)
    td = each(lambda p, s: dnn(eye + p, eye + s, X3), n1, n2)
    n8 = each(lambda n: dnn(n, n, X3), n4)
    td = each(lambda t, n: dnn(t, eye + n, X3), td, n4)
    td = each(lambda t, n: dnn(t, eye + n, X3), td, n8)
    m = each(lambda t, l: dnn(t, l, X3), td, lo)
    m2 = each(lambda x: dnn(x, x, X3), m)
    x = each(lambda p, s: dnn(eye - p, eye + s, X3), m, m2)
    return each(lambda p, t: dnn(p, t, X3), x, td)


def _chunk_common(q, k, v, gcb, bb):
    egb = jnp.exp(gcb)
    gc64 = gcb[:, :CHUNK]
    ii, jj = _iota((CHUNK, CHUNK), 0), _iota((CHUNK, CHUNK), 1)
    incl, strict = ii >= jj, ii > jj
    decay = jnp.exp(jnp.where(incl, gc64 - gc64.T, -jnp.inf))
    kb = k * bb
    vb = v * bb
    kbe = kb * egb
    pq = dnt(jnp.concatenate([kb, q], axis=0), k, X3)
    ekb = jnp.exp(gcb[CHUNK - 1:CHUNK, :] - gcb)
    return dict(egb=egb, decay=decay, kb=kb, vb=vb, kbe=kbe, pm=pq[:CHUNK], qm=pq[CHUNK:], ekb=ekb,
                incl=incl, strict=strict, ii=ii, jj=jj)


def _chunk_head(vals, ci, h):
    return [v[ci * CHUNK:(ci + 1) * CHUNK, h * HD:(h + 1) * HD] for v in vals]


def _assemble(per_chunk):
    return jnp.concatenate([_cat(hs) for hs in per_chunk], axis=0)


def _assemble3(per_chunk):
    return jnp.stack([jnp.concatenate([per_chunk[ci][h] for ci in range(CPS)], axis=0) for h in range(DN_HEADS)])


def delta_prep_fwd(name, q, k, v, gb, bb):
    length = q.shape[0]

    def fn(ctx, rows, consts, prevs, nexts):
        gcb_all = _chunk_scan_rows(rows[3])
        vals = [rows[0], rows[1], rows[2], gcb_all, rows[4]]
        units = [(ci, h) for ci in range(CPS) for h in range(DN_HEADS)]
        ins = [_chunk_head(vals, ci, h) for ci, h in units]
        cs = [_chunk_common(*i) for i in ins]
        eye = (cs[0]["ii"] == cs[0]["jj"]).astype(F32)
        ts = _tri_inv([jnp.where(c["strict"], c["pm"] * c["decay"], 0.0) for c in cs], eye,
                      (cs[0]["ii"] >> 4) == (cs[0]["jj"] >> 4))
        uws = [dnn(t, _cat([c["vb"], c["kbe"]]), X3) for t, c in zip(ts, cs)]

        def grid2(xs):
            return [xs[ci * DN_HEADS:(ci + 1) * DN_HEADS] for ci in range(CPS)]

        return [_assemble(grid2([uw[:, :HD] for uw in uws])), _assemble(grid2([uw[:, HD:] for uw in uws])),
                _assemble(grid2([i[0] * c["egb"] for i, c in zip(ins, cs)])),
                _assemble(grid2([i[1] * c["ekb"] for i, c in zip(ins, cs)])), gcb_all,
                _assemble3(grid2([c["qm"] * c["decay"] for c in cs])), _assemble3(grid2(ts))], []

    return rowwise(name, fn, length, CHUNK * CPS, rows=[q, k, v, gb, bb],
                   out_rows=[(DN_WIDTH, F32)] * 5 + [(DN_HEADS, CHUNK, F32)] * 2)


def delta_prep_bwd(name, q, k, v, gb, bb, t3, du, dw, dqd, dkd, dattn3, dgl):
    length = q.shape[0]

    def fn(ctx, rows, consts, prevs, nexts):
        gcb_all = _chunk_scan_rows(rows[3])
        vals = [rows[0], rows[1], rows[2], gcb_all] + list(rows[4:9])
        t3v, da3v, dglv = rows[9], rows[10], rows[11]
        units = [(ci, h) for ci in range(CPS) for h in range(DN_HEADS)]
        ins = [_chunk_head(vals, ci, h) for ci, h in units]
        cs = [_chunk_common(*i[:5]) for i in ins]
        ts = [t3v[h][ci * CHUNK:(ci + 1) * CHUNK] for ci, h in units]
        dattns = [jnp.where(c["incl"], da3v[h][ci * CHUNK:(ci + 1) * CHUNK], 0.0) for (ci, h), c in zip(units, cs)]
        duws = [_cat([i[5], i[6]]) for i in ins]
        dvks = [dtn(t, d, X3) for t, d in zip(ts, duws)]
        dts = [dnt(d, _cat([c["vb"], c["kbe"]]), X3) for d, c in zip(duws, cs)]
        dts = [dnt(d, t, X3) for d, t in zip(dts, ts)]
        das = [jnp.where(c["strict"], -dtn(t, d, X3), 0.0) for c, t, d in zip(cs, ts, dts)]
        dpqs = [jnp.concatenate([da * c["decay"], dat * c["decay"]], axis=0) for da, dat, c in zip(das, dattns, cs)]
        dpqks = [dnn(d, i[1], X3) for d, i in zip(dpqs, ins)]
        dkps = [dtn(d, jnp.concatenate([c["kb"], i[0]], axis=0), X3) for d, c, i in zip(dpqs, cs, ins)]
        dqs, dks, dvs, dgcs, dbs = [], [], [], [], []
        for (ci, h), i, c, dvk, da, dattn, dpqk, dkp in zip(units, ins, cs, dvks, das, dattns, dpqks, dkps):
            qh, kh, vh, _, bh, _, _, dqdh, dkdh = i
            dvb, dkbe = dvk[:, :HD], dvk[:, HD:]
            dkb = dpqk[:CHUNK] + dkbe * c["egb"]
            c1 = _rsum(dkbe * c["kb"] + dqdh * qh) * c["egb"]
            c2 = _rsum(dkdh * kh) * c["ekb"]
            e = (da * c["pm"] + dattn * c["qm"]) * c["decay"]
            dgc = c1 - c2 + _rsum(e) - _rsum(e.T)
            dgl_tot = jnp.max(dglv[ci * 8:(ci + 1) * 8, h * HD:(h + 1) * HD], axis=0, keepdims=True) + _csum(c2)
            dgcs.append(dgc + jnp.where(_iota((CHUNK, HD), 0) == CHUNK - 1, dgl_tot, 0.0))
            dqs.append(dpqk[CHUNK:] + dqdh * c["egb"])
            dks.append(dkp + dkdh * c["ekb"] + dkb * bh)
            dvs.append(dvb * bh)
            dbs.append(jnp.broadcast_to(_rsum(dkb * kh + dvb * vh), (CHUNK, HD)))

        def grid2(xs):
            return [xs[ci * DN_HEADS:(ci + 1) * DN_HEADS] for ci in range(CPS)]

        return [_assemble(grid2(dqs)), _assemble(grid2(dks)), _assemble(grid2(dvs)),
                _chunk_scan_rows(_assemble(grid2(dgcs)), suffix=True), _assemble(grid2(dbs))], []

    return rowwise(name, fn, length, CHUNK * CPS,
                   rows=[q, k, v, gb, bb, du, dw, dqd, dkd, t3, dattn3, (dgl, DN_WIDTH, 0, 8 * CPS)],
                   out_rows=[(DN_WIDTH, F32)] * 5)


SCAN_CHUNKS = 8


def _scan_chunks(n):
    return SCAN_CHUNKS if n % SCAN_CHUNKS == 0 else 1


def delta_scan_fwd(name, qd, kd, u, w, attn3, gcb):
    length = qd.shape[0]
    n = length // CHUNK
    sc = _scan_chunks(n)
    row = pl.BlockSpec((sc * CHUNK, DN_WIDTH), lambda c: (c, 0))
    sq = pl.BlockSpec((DN_HEADS, sc * CHUNK, CHUNK), lambda c: (0, c, 0))

    def body(qd_ref, kd_ref, u_ref, w_ref, attn_ref, gc_ref, o_ref, vn_ref, st_ref, s_ref):
        c = pl.program_id(0)

        @pl.when(c == 0)
        def _():
            s_ref[...] = jnp.zeros_like(s_ref)

        heads = range(DN_HEADS)
        sls = [pl.ds(h * HD, HD) for h in heads]
        ss = [s_ref[h] for h in heads]
        for ci in range(sc):
            rs = pl.ds(ci * CHUNK, CHUNK)
            ws = [dnn(w_ref[rs, sl], s) for sl, s in zip(sls, ss)]
            qs = [dnn(qd_ref[rs, sl], s) for sl, s in zip(sls, ss)]
            vns = [u_ref[rs, sl] - x for sl, x in zip(sls, ws)]
            avs = [dnn(attn_ref[h, rs, :], vn) for h, vn in zip(heads, vns)]
            kvs = [dtn(kd_ref[rs, sl], vn) for sl, vn in zip(sls, vns)]
            for h, sl in zip(heads, sls):
                st_ref[ci, h] = ss[h]
                o_ref[rs, sl] = qs[h] + avs[h]
                vn_ref[rs, sl] = vns[h]
            ss = [s * jnp.exp(gc_ref[pl.ds(ci * CHUNK + CHUNK - 1, 1), sl]) + kv for s, sl, kv in zip(ss, sls, kvs)]
        for h in heads:
            s_ref[h] = ss[h]

    return pl.pallas_call(
        body, name=name, grid=(n // sc,), in_specs=[row, row, row, row, sq, row],
        out_specs=[row, row, pl.BlockSpec((sc, DN_HEADS, HD, HD), lambda c: (c, 0, 0, 0))],
        out_shape=[jax.ShapeDtypeStruct((length, DN_WIDTH), F32), jax.ShapeDtypeStruct((length, DN_WIDTH), F32),
                   jax.ShapeDtypeStruct((n, DN_HEADS, HD, HD), F32)],
        scratch_shapes=[pltpu.VMEM((DN_HEADS, HD, HD), F32)],
        compiler_params=_params(("arbitrary",)),
    )(qd, kd, u, w, attn3, gcb)


def delta_scan_bwd(name, do, qd, kd, w, attn3, vn, st, gcb):
    length = qd.shape[0]
    n = length // CHUNK
    sc = _scan_chunks(n)
    nb = n // sc
    row = pl.BlockSpec((sc * CHUNK, DN_WIDTH), lambda c: (nb - 1 - c, 0))
    sq = pl.BlockSpec((DN_HEADS, sc * CHUNK, CHUNK), lambda c: (0, nb - 1 - c, 0))
    stb = pl.BlockSpec((sc, DN_HEADS, HD, HD), lambda c: (nb - 1 - c, 0, 0, 0))
    glb = pl.BlockSpec((sc * 8, DN_WIDTH), lambda c: (nb - 1 - c, 0))

    def body(do_ref, qd_ref, kd_ref, w_ref, attn_ref, vn_ref, st_ref, gc_ref,
             dqd_ref, dkd_ref, du_ref, dw_ref, dattn_ref, dgl_ref, ds_ref):
        c = pl.program_id(0)

        @pl.when(c == 0)
        def _():
            ds_ref[...] = jnp.zeros_like(ds_ref)

        heads = range(DN_HEADS)
        sls = [pl.ds(h * HD, HD) for h in heads]
        dsns = [ds_ref[h] for h in heads]
        for ci in reversed(range(sc)):
            rs = pl.ds(ci * CHUNK, CHUNK)
            ss = [st_ref[ci, h] for h in heads]
            dos = [do_ref[rs, sl] for sl in sls]
            vns = [vn_ref[rs, sl] for sl in sls]
            dvns = [dtn(attn_ref[h, rs, :], d) for h, d in zip(heads, dos)]
            dvns = [x + dnn(kd_ref[rs, sl], dsn) for x, sl, dsn in zip(dvns, sls, dsns)]
            qdos = [dtn(qd_ref[rs, sl], d) for sl, d in zip(sls, dos)]
            for h, sl in zip(heads, sls):
                dattn_ref[h, rs, :] = dnt(dos[h], vns[h])
                dqd_ref[rs, sl] = dnt(dos[h], ss[h])
                dkd_ref[rs, sl] = dnt(vns[h], dsns[h])
                du_ref[rs, sl] = dvns[h]
            dws = [dnt(dvn, s) for dvn, s in zip(dvns, ss)]
            wdvs = [dtn(w_ref[rs, sl], dvn) for sl, dvn in zip(sls, dvns)]
            nxt = []
            for h, sl in zip(heads, sls):
                egl = jnp.exp(gc_ref[pl.ds(ci * CHUNK + CHUNK - 1, 1), sl])
                dw_ref[rs, sl] = -dws[h]
                dgl_ref[pl.ds(ci * 8, 8), sl] = jnp.broadcast_to(_csum(_rsum(dsns[h] * ss[h])) * egl, (8, HD))
                nxt.append(dsns[h] * egl + qdos[h] - wdvs[h])
            dsns = nxt
        for h in heads:
            ds_ref[h] = dsns[h]

    return pl.pallas_call(
        body, name=name, grid=(nb,), in_specs=[row, row, row, row, sq, row, stb, row],
        out_specs=[row, row, row, row, sq, glb],
        out_shape=[jax.ShapeDtypeStruct((length, DN_WIDTH), F32)] * 4
        + [jax.ShapeDtypeStruct((DN_HEADS, length, CHUNK), F32), jax.ShapeDtypeStruct((n * 8, DN_WIDTH), F32)],
        scratch_shapes=[pltpu.VMEM((DN_HEADS, HD, HD), F32)],
        compiler_params=_params(("arbitrary",)),
    )(do, qd, kd, w, attn3, vn, st, gcb)


def onorm_fwd(name, o, z, nw, tl=ROWS_LIGHT):
    length = o.shape[0]

    def fn(ctx, rows, consts, prevs, nexts):
        outs = []
        for oh, zh in zip(_heads(rows[0], DN_HEADS, HD), _heads(rows[1], DN_HEADS, HD)):
            r = lax.rsqrt(jnp.mean(oh * oh, axis=1, keepdims=True) + RMS_EPS)
            outs.append(oh * r * consts[0] * _silu(zh))
        return [_cat(outs)], []

    return rowwise(name, fn, length, min(tl, length), rows=[o, z], consts=[nw], out_rows=[(DN_WIDTH, BF16)])[0]


def onorm_bwd(name, o, z, d_on, nw, tl=ROWS_LIGHT):
    length = o.shape[0]

    def fn(ctx, rows, consts, prevs, nexts):
        dos, dzs = [], []
        dnw = jnp.zeros((1, HD), F32)
        for oh, zh, dh in zip(*[_heads(r, DN_HEADS, HD) for r in rows]):
            r = lax.rsqrt(jnp.mean(oh * oh, axis=1, keepdims=True) + RMS_EPS)
            y = oh * r
            sz = _silu(zh)
            t = dh * sz * consts[0]
            dos.append(r * (t - y * jnp.mean(t * y, axis=1, keepdims=True)))
            dzs.append(dh * y * consts[0] * _dsilu(zh))
            dnw = dnw + _csum(dh * y * sz)
        return [_cat(dos), _cat(dzs)], [dnw]

    return rowwise(name, fn, length, min(tl, length), rows=[o, z, d_on], consts=[nw],
                   out_rows=[(DN_WIDTH, F32), (DN_WIDTH, BF16)], out_accs=[((1, HD), F32)])


def merge_fwd(name, gates, ydn, ypool, tl=ROWS):
    length = ydn.shape[0]

    def fn(ctx, rows, consts, prevs, nexts):
        gt = rows[0]
        return [_sigmoid(gt[:, :D_MODEL]) * rows[1] + _sigmoid(gt[:, D_MODEL:]) * rows[2]], []

    return rowwise(name, fn, length, min(tl, length), rows=[gates, ydn, ypool], out_rows=[(D_MODEL, BF16)])[0]


def merge_bwd(name, gates, ydn, ypool, dm, tl=ROWS_WIDE):
    length = ydn.shape[0]

    def fn(ctx, rows, consts, prevs, nexts):
        gt, yd, yp, d = rows
        sd, sp = _sigmoid(gt[:, :D_MODEL]), _sigmoid(gt[:, D_MODEL:])
        dgates = _cat([d * yd * sd * (1.0 - sd), d * yp * sp * (1.0 - sp)])
        return [d * sd, d * sp, dgates], []

    return rowwise(name, fn, length, min(tl, length), rows=[gates, ydn, ypool, dm],
                   out_rows=[(D_MODEL, BF16), (D_MODEL, BF16), (2 * D_MODEL, BF16)])


def _trailing_sums(ext, upto):
    s, sh = ext, 1
    while sh < upto:
        s = s + pltpu.roll(s, sh, 0)
        sh *= 2
    return s


def _leading_sums(ext, upto, n):
    s, sh = ext, 1
    while sh < upto:
        s = s + pltpu.roll(s, n - sh, 0)
        sh *= 2
    return s


def _pool_mixed(ctx, p, prev, tl):
    prevm = jnp.where(ctx.i > 0, prev, 0.0)
    t1 = (_row_index(ctx, tl) + 1).astype(F32)
    outs = []
    for gi, win in enumerate(POOL_WINDOWS):
        sl = slice(gi * HD, (gi + 1) * HD)
        ext = jnp.concatenate([prevm[:, sl], p[:, sl]], axis=0)
        mean = _trailing_sums(ext, win)[HALO:] / jnp.minimum(t1, float(win))
        outs.append(mean - p[:, sl])
    return outs


def pool_fwd(name, p, pool_w, scale, tl=ROWS_LIGHT):
    length = p.shape[0]
    tl = min(tl, length)

    def fn(ctx, rows, consts, prevs, nexts):
        mixed = _pool_mixed(ctx, rows[0], prevs[0], tl)
        y = _cat([dnn(m, consts[0][gi]) for gi, m in enumerate(mixed)])
        return [y * consts[1]], []

    return rowwise(name, fn, length, tl, rows=[p], consts=[pool_w, scale], prevs=[p],
                   out_rows=[(POOL_WIDTH, BF16)])[0]


def pool_bwd(name, p, dpo, pool_w, scale, tl=ROWS_LIGHT):
    length = p.shape[0]
    tl = min(tl, length)
    n = tl + HALO

    def fn(ctx, rows, consts, prevs, nexts):
        last = ctx.i == ctx.nblk - 1
        mixed = _pool_mixed(ctx, rows[0], prevs[0], tl)
        dext = jnp.concatenate([rows[1], jnp.where(last, 0.0, nexts[0])], axis=0)
        t1 = (_row_index(ctx, n) + 1).astype(F32)
        dps, dws, dscs = [], [], []
        for gi, win in enumerate(POOL_WINDOWS):
            sl = slice(gi * HD, (gi + 1) * HD)
            wg = consts[0][gi]
            dyraw = dext[:, sl] * consts[1][:, sl]
            dmix = dnt(dyraw, wg)
            dws.append(dtn(mixed[gi], dyraw[:tl]))
            dscs.append(_csum(rows[1][:, sl] * dnn(mixed[gi], wg)))
            lead = _leading_sums(dmix / jnp.minimum(t1, float(win)), win, n)
            dps.append(lead[:tl] - dmix[:tl])
        return [_cat(dps)], [jnp.stack(dws), _cat(dscs)]

    return rowwise(name, fn, length, tl, rows=[p, dpo], consts=[pool_w, scale], prevs=[p], nexts=[dpo],
                   out_rows=[(POOL_WIDTH, BF16)],
                   out_accs=[((len(POOL_WINDOWS), HD, HD), F32), ((1, POOL_WIDTH), F32)])


def _xa_probs(qh, kh):
    s = dnt(qh, kh) * (XA_HD ** -0.5)
    e = jnp.exp(s - jnp.max(s, axis=1, keepdims=True))
    return e / _rsum(e)


def xattn_fwd(name, qx, kx, vx, tl=ROWS_LIGHT):
    length = qx.shape[0]

    def fn(ctx, rows, consts, prevs, nexts):
        outs = [dnn(_xa_probs(qh, kh), vh) for qh, kh, vh in
                zip(_heads(rows[0], XA_HEADS, XA_HD), _heads(consts[0], XA_HEADS, XA_HD),
                    _heads(consts[1], XA_HEADS, XA_HD))]
        return [_cat(outs)], []

    return rowwise(name, fn, length, min(tl, length), rows=[qx], consts=[kx, vx], out_rows=[(D_MODEL, BF16)])[0]


def xattn_bwd(name, qx, dox, kx, vx, tl=ROWS):
    length = qx.shape[0]

    def fn(ctx, rows, consts, prevs, nexts):
        dqs, dks, dvs = [], [], []
        for qh, dh, kh, vh in zip(_heads(rows[0], XA_HEADS, XA_HD), _heads(rows[1], XA_HEADS, XA_HD),
                                  _heads(consts[0], XA_HEADS, XA_HD), _heads(consts[1], XA_HEADS, XA_HD)):
            pr = _xa_probs(qh, kh)
            dpr = dnt(dh, vh)
            ds = pr * (dpr - _rsum(dpr * pr)) * (XA_HD ** -0.5)
            dqs.append(dnn(ds, kh))
            dks.append(dtn(ds, qh))
            dvs.append(dtn(pr, dh))
        return [_cat(dqs)], [_cat(dks), _cat(dvs)]

    return rowwise(name, fn, length, min(tl, length), rows=[qx, dox], consts=[kx, vx],
                   out_rows=[(D_MODEL, BF16)], out_accs=[((N_MEM, D_MODEL), F32)] * 2)


def local_step(x, mem, target, w, io):
    alog = jnp.pad(w["a_log"], ((0, 0), (0, 128 - DN_HEADS)))
    dtb = jnp.pad(w["dt_bias"], ((0, 0), (0, 128 - DN_HEADS)))

    f1, res1, w_down1 = ffn_fwd("ffn1", x, w["ffn1_w_gate"], w["ffn1_w_up"], io.ffn1_down, deps=io.rest_started())
    x1, r1, x1b = ln_fwd("ln1", [(ALPHA, x), (0.5, f1)], w["ln1_g"], w["ln1_b"], deps=io.halfway("mixer", f1))
    w = dict(w, ffn1_w_down=w_down1, **io.weights("mixer", x1))
    taps = [w["conv_w"][j:j + 1] for j in range(4)]

    pre = mm("in_qkv", x1b, w["in_qkv"], tb=True)
    z = mm("in_z", x1b, w["in_z"], tb=True)
    gates = mm("in_gates", x1b, w["in_gates"], tb=True)
    p = mm("in_p", x1b, w["in_p"], tb=True)
    ab = mm("in_ab", x1b, w["in_ab"], tb=True)
    q, k, v = conv_fwd("conv", pre, taps, deps=io.halfway("xa", pre))
    gb, bb = gates_fwd("gates", ab, alog, dtb)
    u, wd_, qd, kd, gcb, attn3, t3 = delta_prep_fwd("dprep", q, k, v, gb, bb)
    o, vn, st = delta_scan_fwd("dscan", qd, kd, u, wd_, attn3, gcb)
    on = onorm_fwd("onorm", o, z, w["dn_norm_w"])
    ydn = mm("dn_branch", on, w["w_dn_branch"], tb=True)
    po = pool_fwd("pool", p, w["pool_w"], w["pool_scale"])
    ypool = mm("pool_branch", po, w["w_pool_branch"], tb=True)
    merged = merge_fwd("merge", gates, ydn, ypool)
    mix = mm("mix_out", merged, w["w_mix_out"])
    x2, r2, x2b = ln_fwd("ln2", [(ALPHA, x1), (1.0, mix)], w["ln2_g"], w["ln2_b"])

    w = dict(w, **io.weights("xa", x2))
    _, _, m = ln_fwd("ln_mem", [(1.0, mem)], w["mem_ln_g"], w["mem_ln_b"])
    qx = mm("xa_q", x2b, w["xa_wq"], deps=io.halfway("ffn2", x2))
    kx = mm("xa_k", m, w["xa_wk"])
    vx = mm("xa_v", m, w["xa_wv"])
    ox = xattn_fwd("xattn", qx, kx, vx)
    xa = mm("xa_o", ox, w["xa_wo"])
    x3, r3, x3b = ln_fwd("ln3", [(ALPHA, x2), (1.0, xa)], w["ln3_g"], w["ln3_b"])
    w = dict(w, **io.weights("ffn2", x3))

    f2, res2, _ = ffn_fwd("ffn2", x3b, w["ffn2_w_gate"], w["ffn2_w_up"], w["ffn2_w_down"])
    dy4, r4, loss = ln_loss("ln4_loss", [(ALPHA, x3), (0.5, f2)], w["ln4_g"], w["ln4_b"], target)

    g = {}
    dr4, g["ln4_g"], g["ln4_b"] = ln_bwd("ln4_b", r4, [(1.0, dy4)], w["ln4_g"])
    dx3, g["ffn2_w_gate"], g["ffn2_w_up"], g["ffn2_w_down"] = ffn_bwd(
        "ffn2b", x3b, res2, dr4, w["ffn2_w_gate"], w["ffn2_w_up"], w["ffn2_w_down"])
    dep = io.grads_out("ffn2", g)
    dr3, g["ln3_g"], g["ln3_b"] = ln_bwd("ln3_b", r3, [(ALPHA, dr4), (1.0, dx3)], w["ln3_g"], deps=dep)

    dox = mm("xa_do", dr3, w["xa_wo"], tb=True)
    g["xa_wo"] = mm("xa_dwo", ox, dr3, ta=True)
    dqx, dkx, dvx = xattn_bwd("xattn_b", qx, dox, kx, vx)
    g["xa_wq"] = mm("xa_dwq", x2b, dqx, ta=True)
    dx2 = mm("xa_dx", dqx, w["xa_wq"], tb=True)
    g["xa_wk"] = mm("xa_dwk", m, dkx, ta=True)
    g["xa_wv"] = mm("xa_dwv", m, dvx, ta=True)
    dmm = mm("xa_dmk", dkx, w["xa_wk"], tb=True, deps=io.grads_out("xa", g))
    dmm = mm("xa_dmv", dvx, w["xa_wv"], tb=True, add=dmm)
    _, g["mem_ln_g"], g["mem_ln_b"] = ln_bwd("ln_mem_b", mem, [(1.0, dmm)], w["mem_ln_g"])
    dr2, g["ln2_g"], g["ln2_b"] = ln_bwd("ln2_b", r2, [(ALPHA, dr3), (1.0, dx2)], w["ln2_g"])
    io.grads_in("ffn2", dr2)

    dmerged = mm("mix_dm", dr2, w["w_mix_out"], tb=True)
    g["w_mix_out"] = mm("mix_dw", merged, dr2, ta=True)
    d_ydn, d_ypool, d_gates = merge_bwd("merge_b", gates, ydn, ypool, dmerged)
    g["w_dn_branch"] = mm("dn_dw", d_ydn, on, ta=True)
    d_on = mm("dn_dx", d_ydn, w["w_dn_branch"])
    g["w_pool_branch"] = mm("pool_dw", d_ypool, po, ta=True)
    d_po = mm("pool_dx", d_ypool, w["w_pool_branch"])
    dp, g["pool_w"], g["pool_scale"] = pool_bwd("pool_b", p, d_po, w["pool_w"], w["pool_scale"])
    d_o, dz, g["dn_norm_w"] = onorm_bwd("onorm_b", o, z, d_on, w["dn_norm_w"])
    dqd, dkd, du, dw_, dattn3, dgl = delta_scan_bwd("dscan_b", d_o, qd, kd, wd_, attn3, vn, st, gcb)
    dq, dk, dv, dgb, dbb = delta_prep_bwd("dprep_b", q, k, v, gb, bb, t3, du, dw_, dqd, dkd, dattn3, dgl)
    dpre, dc0, dc1, dc2, dc3 = conv_bwd("conv_b", pre, dq, dk, dv, taps)
    g["conv_w"] = jnp.concatenate([dc0, dc1, dc2, dc3], axis=0)
    d_ab, dalog, ddtb = gates_bwd("gates_b", ab, dgb, dbb, alog, dtb)
    g["a_log"] = dalog[:, :DN_HEADS]
    g["dt_bias"] = ddtb[:, :DN_HEADS]
    g["in_qkv"], g["in_z"], g["in_ab"] = mm_fan_t("in_dw_a", [dpre, dz, d_ab], x1b)
    g["in_gates"], g["in_p"] = mm_fan_t("in_dw_b", [d_gates, dp], x1b)
    io.grads_in("xa", g["in_ab"])
    dx1 = mm_sum("in_dx", [(dpre, w["in_qkv"]), (dz, w["in_z"]), (d_gates, w["in_gates"]), (dp, w["in_p"]),
                           (d_ab, w["in_ab"])], deps=io.grads_out("mixer", g))
    dr1, g["ln1_g"], g["ln1_b"] = ln_bwd("ln1_b", r1, [(ALPHA, dr2), (1.0, dx1)], w["ln1_g"])

    def on_dw(which, dw):
        name = "ffn1_w_" + which
        small = io.small_out(dict(g, loss=loss[0, :1])) if which == "down" else ()
        return small + io.grads_out(name, {name: dw})

    grad_x, g["ffn1_w_gate"], g["ffn1_w_up"], g["ffn1_w_down"] = ffn_bwd(
        "ffn1b", x, res1, dr1, w["ffn1_w_gate"], w["ffn1_w_up"], w["ffn1_w_down"], on_dw=on_dw, also=(ALPHA, dr1))
    return loss, grad_x, g


WEIGHT_NAMES = ['ffn1_w_gate', 'ffn1_w_up', 'ffn1_w_down', 'ln1_g', 'ln1_b', 'w_in', 'conv_w', 'a_log', 'dt_bias',
                'dn_norm_w', 'w_dn_branch', 'pool_w', 'pool_scale', 'w_pool_branch', 'w_mix_out', 'ln2_g', 'ln2_b',
                'mem_ln_g', 'mem_ln_b', 'xa_wq', 'xa_wk', 'xa_wv', 'xa_wo', 'ln3_g', 'ln3_b', 'ffn2_w_gate',
                'ffn2_w_up', 'ffn2_w_down', 'ln4_g', 'ln4_b']
SHARDED = [
    ("ffn1_w_gate", "cols", (1024, 352)), ("ffn1_w_up", "cols", (1024, 352)), ("ffn1_w_down", "rows", (352, 1024)),
    ("w_in", "cols", (1024, 577)), ("conv_w", "flat", (4, 192)), ("w_dn_branch", "cols", (512, 128)),
    ("w_pool_branch", "cols", (512, 128)), ("w_mix_out", "rows", (128, 1024)), ("xa_wq", "rows", (128, 1024)),
    ("xa_wk", "rows", (128, 1024)), ("xa_wv", "rows", (128, 1024)), ("xa_wo", "rows", (128, 1024)),
    ("ffn2_w_gate", "cols", (1024, 352)), ("ffn2_w_up", "cols", (1024, 352)), ("ffn2_w_down", "rows", (352, 1024)),
]
REPLICATED = [n for n in WEIGHT_NAMES if n not in {s[0] for s in SHARDED}]
ROW_ALIGN = 16
ROW_BLOCKS = (512, 384, 352, 256, 192, 176, 128)
GROUPS = {"ffn1_gu": ("ffn1_w_gate", "ffn1_w_up"), "ffn1_d": ("ffn1_w_down",),
          "ffn1_w_gate": ("ffn1_w_gate",), "ffn1_w_up": ("ffn1_w_up",), "ffn1_w_down": ("ffn1_w_down",),
          "mixer": ("w_in", "conv_w", "w_dn_branch", "w_pool_branch", "w_mix_out"),
          "xa": ("xa_wq", "xa_wk", "xa_wv", "xa_wo"),
          "ffn2": ("ffn2_w_gate", "ffn2_w_up", "ffn2_w_down")}
W_IN_COLS = 577
W_IN_PIECES = (("in_qkv", 0, 1536), ("in_z", 1536, 2048), ("in_ab", 2048, 2056), ("in_p", 2056, 2568),
               ("in_gates", 2568, 4616))


def _round_up(n, m):
    return -(-n // m) * m


def _layout():
    off, table = 0, {}
    for name, form, shape in SHARDED:
        valid = {"rows": shape[0], "cols": shape[1], "flat": 2}[form]
        width = {"rows": shape[1], "cols": shape[0], "flat": shape[0] * shape[1]}[form]
        rows = _round_up(valid, ROW_ALIGN)
        table[name] = (off, rows, valid, width, form, shape)
        off += rows
    return table


LAYOUT = _layout()


def _group_span(names):
    base = LAYOUT[names[0]][0]
    rows = LAYOUT[names[-1]][0] + LAYOUT[names[-1]][1] - base
    while not any(rows % b == 0 for b in ROW_BLOCKS):
        rows += ROW_ALIGN
    return base, rows


def _row_block(rows):
    return _pick(rows, ROW_BLOCKS)


def _pad_block(blk, rows):
    return jnp.pad(blk, ((0, rows - blk.shape[0]), (0, LANES - blk.shape[1])))


def pack_weight_shards(shards, names):
    parts, used = [], 0
    for name in names:
        off, rows, valid, width, form, _ = LAYOUT[name]
        s = shards[name]
        if form == "flat":
            flat = s.reshape(1, -1)
            hi = flat.astype(BF16)
            blk = jnp.concatenate([hi, (flat - hi.astype(F32)).astype(BF16)], axis=0)
        else:
            blk = (s.T if form == "cols" else s).astype(BF16)
        parts.append(_pad_block(blk, rows))
        used += rows
    if _group_span(names)[1] > used:
        parts.append(jnp.zeros((_group_span(names)[1] - used, LANES), BF16))
    return jnp.concatenate(parts, axis=0)


IN_AB_ROWS = 128


def _w_in_segments(first, last):
    segs = []
    for k in range(N_DEV):
        lo, hi = max(first, k * W_IN_COLS), min(last, (k + 1) * W_IN_COLS)
        if lo < hi:
            segs.append((k, lo - k * W_IN_COLS, lo - first, hi - lo))
    return segs


def w_in_pieces(name, gathered, off, rows):
    assert off % rows == 0
    sizes = [IN_AB_ROWS if piece == "in_ab" else last - first for piece, first, last in W_IN_PIECES]

    def body(src_ref, *outs):
        for o_ref, (piece, first, last) in zip(outs, W_IN_PIECES):
            if piece == "in_ab":
                o_ref[...] = jnp.zeros_like(o_ref)
            for k, src, dst, count in _w_in_segments(first, last):
                o_ref[pl.ds(dst, count), :] = src_ref[k, pl.ds(src, count), :]

    outs = pl.pallas_call(
        body, name=name, grid=(1,), in_specs=[pl.BlockSpec((N_DEV, rows, LANES), lambda i: (0, off // rows, 0))],
        out_specs=[pl.BlockSpec((n, LANES), lambda i: (0, 0)) for n in sizes],
        out_shape=[jax.ShapeDtypeStruct((n, LANES), gathered.dtype) for n in sizes],
        compiler_params=_params(("arbitrary",)),
    )(gathered)
    return {piece: o for (piece, _, _), o in zip(W_IN_PIECES, outs)}


def unpack_full_weights(gathered, names):
    out, base = {}, _group_span(names)[0]
    for name in names:
        off, rows, valid, width, form, shape = LAYOUT[name]
        seg = gathered[:, off - base:off - base + rows]
        if form == "flat":
            flat = seg[:, 0, :width].astype(F32) + seg[:, 1, :width].astype(F32)
            out[name] = flat.reshape((N_DEV,) + shape).transpose(1, 0, 2).reshape(shape[0], N_DEV * shape[1])
        elif name == "w_in":
            out.update(w_in_pieces("w_in_pieces", gathered, off - base, rows))
        else:
            out[name] = seg[:, :valid, :width].reshape(N_DEV * valid, width)
    return out


def pack_full_grads(grads, names, me):
    wire, own, used = [], [], 0
    for name in names:
        off, rows, valid, width, form, shape = LAYOUT[name]
        if form == "flat":
            full = grads[name].reshape(shape[0], N_DEV, shape[1]).transpose(1, 0, 2).reshape(N_DEV, 1, width)
        elif name == "w_in":
            full = jnp.concatenate([grads[piece][:last - first] for piece, first, last in W_IN_PIECES], axis=0)
            full = full.reshape(N_DEV, valid, width)
        else:
            full = grads[name].reshape(N_DEV, valid, width)
        pad = ((0, rows - full.shape[1]), (0, LANES - width))
        wire.append(jnp.pad(full.astype(WIRE), ((0, 0),) + pad))
        own.append(jnp.pad(lax.dynamic_index_in_dim(full, me, 0, keepdims=False), pad))
        used += rows
    if _group_span(names)[1] > used:
        wire.append(jnp.zeros((N_DEV, _group_span(names)[1] - used, LANES), WIRE))
        own.append(jnp.zeros((_group_span(names)[1] - used, LANES), F32))
    return jnp.concatenate(wire, axis=1), jnp.concatenate(own, axis=0)


TRANSPOSED = ("ffn1_w_gate", "ffn1_w_up", "ffn2_w_gate", "ffn2_w_up", "w_in")


def unpack_grad_shards(packed, names):
    out, base = {}, _group_span(names)[0]
    for name in names:
        off, rows, valid, width, form, shape = LAYOUT[name]
        off -= base
        if form == "flat":
            out[name] = packed[off, :width].reshape(shape)
        elif name in TRANSPOSED:
            out[name] = packed[off:off + valid, :width]
        elif form == "cols":
            out[name] = packed[off:off + valid, :width].T
        else:
            out[name] = packed[off:off + valid, :width]
    return out


SMALL_SHAPES = {n: (1024,) for n in REPLICATED}
SMALL_SHAPES.update(pool_w=(4, 128, 128), pool_scale=(512,), dn_norm_w=(128,), a_log=(4,), dt_bias=(4,))


SMALL_SHAPES["loss"] = (1,)
SMALL_NAMES = REPLICATED + ["loss"]


def _small_layout():
    off, table = 0, {}
    for name in SMALL_NAMES:
        numel = 1
        for d in SMALL_SHAPES[name]:
            numel *= d
        rows = _round_up(-(-numel // LANES), 8)
        table[name] = (off, rows, numel)
        off += rows
    return table, off


SMALL_LAYOUT, SMALL_ROWS = _small_layout()


def _to_rows(flat, rows):
    return jnp.pad(flat, (0, rows * LANES - flat.shape[0])).reshape(rows, LANES)


def pack_small(values):
    return jnp.concatenate([_to_rows(values[name].reshape(-1), SMALL_LAYOUT[name][1]) for name in SMALL_NAMES], axis=0)


def unpack_small(packed):
    out = {}
    for name in SMALL_NAMES:
        off, rows, numel = SMALL_LAYOUT[name]
        out[name] = packed[off:off + rows].reshape(-1)[:numel].reshape(SMALL_SHAPES[name])
    return out


MESH = pl.DeviceIdType.MESH


def _position():
    return lax.axis_index("x"), lax.axis_index("y"), lax.axis_index("c")


def _other_chips(x, y):
    return [(1 - x, y), (x, 1 - y), (1 - x, 1 - y)]


def all_gather(name, block):
    rows, n = block.shape

    def body(x_ref, out_ref, send_sems, recv_sems, local_sem):
        x, y, c = _position()
        me, sibling = (x, y, c), (x, y, 1 - c)
        chips = _other_chips(x, y)

        def slot(px, py, pc):
            return out_ref.at[4 * px + 2 * py + pc]

        def copy(k, blk, to, src=None):
            return pltpu.make_async_remote_copy(
                src_ref=slot(*blk) if src is None else src, dst_ref=slot(*blk),
                send_sem=send_sems.at[k], recv_sem=recv_sems.at[k], device_id=to, device_id_type=MESH)

        mine = pltpu.make_async_copy(x_ref, slot(*me), local_sem)
        mine.start()
        first = [copy(0, me, sibling, src=x_ref)]
        first += [copy(1 + j, me, (*chip, c), src=x_ref) for j, chip in enumerate(chips)]
        for cp in first:
            cp.start()
        passed = [copy(4 + j, (*chip, c), sibling) for j, chip in enumerate(chips)]
        for j, chip in enumerate(chips):
            copy(1 + j, (*chip, c), me).wait_recv()
            passed[j].start()
        copy(0, sibling, me).wait_recv()
        for j, chip in enumerate(chips):
            copy(4 + j, (*chip, 1 - c), me).wait_recv()
        for cp in first + passed:
            cp.wait_send()
        mine.wait()

    return pl.pallas_call(
        body, name=name, out_shape=jax.ShapeDtypeStruct((N_DEV, rows, n), block.dtype),
        in_specs=[ANY], out_specs=ANY,
        scratch_shapes=[pltpu.SemaphoreType.DMA((7,)), pltpu.SemaphoreType.DMA((7,)), pltpu.SemaphoreType.DMA(())],
    )(block)


HBM = pl.BlockSpec(memory_space=pltpu.HBM)
SEM = pl.BlockSpec(memory_space=pltpu.SEMAPHORE)
EFFECT = pltpu.SideEffectType.DATAFLOW_SIDE_EFFECTING


def _remote(src, dst, send_sem, recv_sem, to):
    return pltpu.make_async_remote_copy(src_ref=src, dst_ref=dst, send_sem=send_sem, recv_sem=recv_sem,
                                        device_id=to, device_id_type=MESH)


def split_start(name, bufs, n, make_copies):
    nb = len(bufs)

    def body(*refs):
        for out_cp, _ in make_copies(refs[:nb], refs[nb:nb + n], refs[nb + n:nb + 2 * n]):
            out_cp.start()
        refs[-1][...] = jnp.zeros_like(refs[-1])

    outs = pl.pallas_call(
        body, name=name,
        out_shape=tuple([pltpu.SemaphoreType.DMA(())] * (2 * n)) + tuple(pltpu.HBM(b.shape, b.dtype) for b in bufs)
        + (jax.ShapeDtypeStruct((8, 128), F32),),
        in_specs=[HBM] * nb,
        out_specs=tuple([SEM] * (2 * n) + [HBM] * nb + [pl.BlockSpec(memory_space=pltpu.VMEM)]),
        input_output_aliases={i: 2 * n + i for i in range(nb)},
        compiler_params=pltpu.CompilerParams(has_side_effects=EFFECT),
    )(*[pltpu.with_memory_space_constraint(b, pltpu.HBM) for b in bufs])
    return list(outs[:2 * n]), list(outs[2 * n:2 * n + nb]), outs[-1]


def split_wait(name, bufs, sems, n, make_copies, after):
    nb = len(bufs)

    def body(*refs):
        for out_cp, in_cp in make_copies(refs[:nb], refs[nb:nb + n], refs[nb + n:nb + 2 * n]):
            out_cp.wait_send()
            in_cp.wait_recv()

    outs = pl.pallas_call(
        body, name=name, out_shape=tuple(pltpu.HBM(b.shape, b.dtype) for b in bufs),
        in_specs=[HBM] * nb + [SEM] * (2 * n) + [ANY], out_specs=tuple([HBM] * nb),
        input_output_aliases={i: i for i in range(nb)},
        compiler_params=pltpu.CompilerParams(has_side_effects=EFFECT),
    )(*bufs, *sems, after)
    return list(outs)


def _gather_stage1(refs, send, recv):
    src, land = refs
    x, y, c = _position()
    peers = [(x, y, 1 - c)] + [(*chip, c) for chip in _other_chips(x, y)]
    return [(_remote(src, land.at[4 * x + 2 * y + c], send[k], recv[k], p),
             _remote(src, land.at[4 * p[0] + 2 * p[1] + p[2]], send[k], recv[k], p)) for k, p in enumerate(peers)]


def _gather_stage2(refs, send, recv):
    (land,) = refs
    x, y, c = _position()
    out = []
    for j, (px, py) in enumerate(_other_chips(x, y)):
        mine, theirs = land.at[4 * px + 2 * py + c], land.at[4 * px + 2 * py + 1 - c]
        out.append((_remote(mine, mine, send[j], recv[j], (x, y, 1 - c)),
                    _remote(theirs, theirs, send[j], recv[j], (x, y, 1 - c))))
    return out


def _flips():
    return [(a, b, d) for a in (0, 1) for b in (0, 1) for d in (0, 1) if a | b | d]


def _gather_direct(refs, send, recv):
    src, land = refs
    x, y, c = _position()
    out = []
    for k, (fx, fy, fc) in enumerate(_flips()):
        p = (1 - x if fx else x, 1 - y if fy else y, 1 - c if fc else c)
        out.append((_remote(src, land.at[4 * x + 2 * y + c], send[k], recv[k], p),
                    _remote(src, land.at[4 * p[0] + 2 * p[1] + p[2]], send[k], recv[k], p)))
    return out


def _scatter_direct(refs, send, recv):
    sendbuf, land = refs
    x, y, c = _position()
    out = []
    for k, (fx, fy, fc) in enumerate(_flips()):
        p = (1 - x if fx else x, 1 - y if fy else y, 1 - c if fc else c)
        cp = _remote(sendbuf.at[4 * p[0] + 2 * p[1] + p[2]], land.at[k], send[k], recv[k], p)
        out.append((cp, cp))
    return out


def _own_plus_slots(name, own, landed):
    n, rows, _ = landed.shape
    tr = _row_block(rows)

    def body(g_ref, l_ref, o_ref):
        acc = g_ref[...]
        for j in range(n):
            acc = acc + l_ref[j].astype(F32)
        o_ref[...] = acc

    return pl.pallas_call(
        body, name=name, grid=(rows // tr,),
        in_specs=[pl.BlockSpec((tr, LANES), lambda i: (i, 0)), pl.BlockSpec((n, tr, LANES), lambda i: (0, i, 0))],
        out_specs=pl.BlockSpec((tr, LANES), lambda i: (i, 0)),
        out_shape=jax.ShapeDtypeStruct((rows, LANES), F32), compiler_params=_params(("parallel",)),
    )(own, landed)


def _sum_slots(name, stack):
    n, rows, _ = stack.shape

    def body(s_ref, o_ref):
        acc = s_ref[0]
        for j in range(1, n):
            acc = acc + s_ref[j]
        o_ref[...] = acc

    return pl.pallas_call(
        body, name=name, in_specs=[pl.BlockSpec(stack.shape, lambda: (0, 0, 0))],
        out_specs=pl.BlockSpec((rows, LANES), lambda: (0, 0)), out_shape=jax.ShapeDtypeStruct((rows, LANES), F32),
    )(stack)


def adamw(name, w, g, m, v):
    shape = w.shape
    last = shape[-1]
    w2, g2, m2, v2 = [a.reshape(-1, last) for a in (w, g, m, v)]
    rows = w2.shape[0]
    tr = _pick(rows, (256, 176, 128))

    def body(w_ref, g_ref, m_ref, v_ref, d_ref, nm_ref, nv_ref):
        gg = g_ref[...]
        nm = ADAM_B1 * m_ref[...] + (1.0 - ADAM_B1) * gg
        nv = ADAM_B2 * v_ref[...] + (1.0 - ADAM_B2) * (gg * gg)
        m_hat = nm / (1.0 - ADAM_B1 ** ADAM_STEP)
        v_hat = nv / (1.0 - ADAM_B2 ** ADAM_STEP)
        d_ref[...] = -ADAM_LR * (m_hat / (jnp.sqrt(v_hat) + ADAM_EPS) + ADAM_WD * w_ref[...])
        nm_ref[...] = nm
        nv_ref[...] = nv

    spec = pl.BlockSpec((tr, last), lambda i: (i, 0))
    outs = pl.pallas_call(
        body, name=name, grid=(rows // tr,), in_specs=[spec] * 4, out_specs=[spec] * 3,
        out_shape=[jax.ShapeDtypeStruct((rows, last), F32)] * 3, compiler_params=_params(("parallel",)),
    )(w2, g2, m2, v2)
    return [o.reshape(shape) for o in outs]


def _landing(block_shape, dtype, own):
    x, y, c = _position()
    return lax.dynamic_update_slice(lax.empty((N_DEV,) + block_shape, dtype), own[None], (4 * x + 2 * y + c, 0, 0))


class _Exchanges:
    def __init__(self, shards):
        self.shards = shards
        self.pending = {}
        self.reduced = {}

    def first_weights(self):
        names = GROUPS["ffn1_gu"]
        return unpack_full_weights(all_gather("ag_ffn1_gu", pack_weight_shards(self.shards, names)), names)

    def rest_started(self):
        tokens = []
        block = pack_weight_shards(self.shards, GROUPS["ffn1_d"])
        sems, bufs, token = split_start("ag_ffn1_d_s", [block, _landing(block.shape, block.dtype, block)], N_DEV - 1,
                                        _gather_direct)
        self.pending["ffn1_d"] = (sems, bufs)
        tokens.append(token)
        for key in ("mixer", "xa", "ffn2"):
            block = pack_weight_shards(self.shards, GROUPS[key])
            sems, bufs, token = split_start(f"ag_{key}_s1", [block, _landing(block.shape, block.dtype, block)], 4,
                                            _gather_stage1)
            self.pending[key] = (sems, bufs)
            tokens.append(token)
        return tuple(tokens)

    def ffn1_down(self, after):
        sems, bufs = self.pending.pop("ffn1_d")
        _, gathered = split_wait("ag_ffn1_d_w", bufs, sems, N_DEV - 1, _gather_direct, after)
        return unpack_full_weights(gathered, GROUPS["ffn1_d"])["ffn1_w_down"]

    def halfway(self, key, after):
        sems, bufs = self.pending.pop(key)
        _, land = split_wait(f"ag_{key}_w1", bufs, sems, 4, _gather_stage1, after)
        sems, bufs, token = split_start(f"ag_{key}_s2", [land], 3, _gather_stage2)
        self.pending[key] = (sems, bufs)
        return (token,)

    def weights(self, key, after):
        sems, bufs = self.pending.pop(key)
        (gathered,) = split_wait(f"ag_{key}_w2", bufs, sems, 3, _gather_stage2, after)
        return unpack_full_weights(gathered, GROUPS[key])

    def grads_out(self, key, grads):
        x, y, c = _position()
        wire, own = pack_full_grads(grads, GROUPS[key], 4 * x + 2 * y + c)
        land = lax.empty((N_DEV - 1,) + wire.shape[1:], WIRE)
        sems, bufs, token = split_start(f"rs_{key}_start", [wire, land], N_DEV - 1, _scatter_direct)
        self.pending[key] = (sems, bufs, own)
        return (token,)

    def grads_in(self, key, after):
        sems, bufs, own = self.pending.pop(key)
        _, landed = split_wait(f"rs_{key}_wait", bufs, sems, N_DEV - 1, _scatter_direct, after)
        self.reduced.update(unpack_grad_shards(_own_plus_slots(f"rs_{key}_sum", own, landed), GROUPS[key]))

    def small_out(self, values):
        block = pack_small(values)
        sems, bufs, token = split_start("ag_small_s", [block, _landing(block.shape, block.dtype, block)], N_DEV - 1,
                                        _gather_direct)
        self.pending["small"] = (sems, bufs)
        return (token,)

    def small_in(self, after):
        sems, bufs = self.pending.pop("small")
        _, gathered = split_wait("ag_small_w", bufs, sems, N_DEV - 1, _gather_direct, after)
        return unpack_small(_sum_slots("small_sum", gathered))


def kernel(x, mem, ffn1_w_gate, ffn1_w_up, ffn1_w_down, ln1_g, ln1_b, w_in, conv_w, a_log, dt_bias, dn_norm_w, w_dn_branch, pool_w, pool_scale, w_pool_branch, w_mix_out, ln2_g, ln2_b, mem_ln_g, mem_ln_b, xa_wq, xa_wk, xa_wv, xa_wo, ln3_g, ln3_b, ffn2_w_gate, ffn2_w_up, ffn2_w_down, ln4_g, ln4_b, loss_target, m_ffn1_w_gate, m_ffn1_w_up, m_ffn1_w_down, m_ln1_g, m_ln1_b, m_w_in, m_conv_w, m_a_log, m_dt_bias, m_dn_norm_w, m_w_dn_branch, m_pool_w, m_pool_scale, m_w_pool_branch, m_w_mix_out, m_ln2_g, m_ln2_b, m_mem_ln_g, m_mem_ln_b, m_xa_wq, m_xa_wk, m_xa_wv, m_xa_wo, m_ln3_g, m_ln3_b, m_ffn2_w_gate, m_ffn2_w_up, m_ffn2_w_down, m_ln4_g, m_ln4_b, v_ffn1_w_gate, v_ffn1_w_up, v_ffn1_w_down, v_ln1_g, v_ln1_b, v_w_in, v_conv_w, v_a_log, v_dt_bias, v_dn_norm_w, v_w_dn_branch, v_pool_w, v_pool_scale, v_w_pool_branch, v_w_mix_out, v_ln2_g, v_ln2_b, v_mem_ln_g, v_mem_ln_b, v_xa_wq, v_xa_wk, v_xa_wv, v_xa_wo, v_ln3_g, v_ln3_b, v_ffn2_w_gate, v_ffn2_w_up, v_ffn2_w_down, v_ln4_g, v_ln4_b):
    given = dict(locals())
    shards = {n: given[n] for n in WEIGHT_NAMES}
    io = _Exchanges({n: shards[n][0] for n, _, _ in SHARDED})
    w = io.first_weights()
    for n in REPLICATED:
        w[n] = shards[n][0] if n == "pool_w" else shards[n]
    loss_part, grad_x, g = local_step(x[0], mem[0], loss_target[0], w, io)

    grad, updates = {}, {}

    def update(names, reduced):
        for n in names:
            if n in TRANSPOSED:
                outs = adamw("adamw_" + n, shards[n][0].T, reduced[n], given["m_" + n][0].T, given["v_" + n][0].T)
                grad[n], updates[n] = reduced[n].T[None], [o.T[None] for o in outs]
            else:
                grad[n] = reduced[n].reshape(shards[n].shape)
                updates[n] = adamw("adamw_" + n, shards[n], grad[n], given["m_" + n], given["v_" + n])
        return updates[names[-1]][0]

    update(GROUPS["ffn2"] + GROUPS["xa"], io.reduced)
    io.grads_in("mixer", grad_x)
    done = update(GROUPS["mixer"], io.reduced)
    small = io.small_in(done)
    loss = small.pop("loss")[0]
    done = update(REPLICATED, small)
    for n in ("ffn1_w_down", "ffn1_w_gate", "ffn1_w_up"):
        io.grads_in(n, done)
        done = update(GROUPS[n], io.reduced)
    return (loss, grad_x[None], *[grad[n] for n in WEIGHT_NAMES], *[updates[n][0] for n in WEIGHT_NAMES],
            *[updates[n][1] for n in WEIGHT_NAMES], *[updates[n][2] for n in WEIGHT_NAMES])
```

```python
import jax
import jax.numpy as jnp
from jax import lax
from jax.experimental import pallas as pl
from jax.experimental.pallas import tpu as pltpu

F32 = jnp.float32
BF16 = jnp.bfloat16
MMD = BF16
WIRE = BF16
X3 =lax.Precision.HIGH
VMEM_LIMIT_BYTES = 48 * 1024 * 1024

D_MODEL = 1024
D_FF = 2816
CHUNK = 64
N_MEM = 256
DN_HEADS = 4
HD = 128
DN_WIDTH = 512
POOL_WINDOWS = (2, 4, 8, 16)
POOL_WIDTH = 512
XA_HEADS = 4
XA_HD = 256
LN_EPS = 1e-5
RMS_EPS = 1e-6
L2_EPS = 1e-6
ALPHA = 2.0 ** 0.25
HALO = 16
ROWS = 512
ROWS_LIGHT = 1024
ROWS_WIDE = 256

ADAM_LR = 0.001
ADAM_B1 = 0.9
ADAM_B2 = 0.999
ADAM_EPS = 1e-08
ADAM_WD = 0.01
ADAM_STEP = 10

N_DEV = 8
LANES = 1024
ANY = pl.BlockSpec(memory_space=pl.ANY)


def _dot(a, b, ca, cb, prec):
    dn = (((ca,), (cb,)), ((), ()))
    if prec is not None:
        return lax.dot_general(a.astype(F32), b.astype(F32), dn, precision=prec, preferred_element_type=F32)
    return lax.dot_general(a.astype(MMD), b.astype(MMD), dn, preferred_element_type=F32)


def dnn(a, b, prec=None):
    return _dot(a, b, 1, 0, prec)


def dnt(a, b, prec=None):
    return _dot(a, b, 1, 1, prec)


def dtn(a, b, prec=None):
    return _dot(a, b, 0, 0, prec)


def _sigmoid(x):
    return jax.nn.sigmoid(x)


def _silu(x):
    return x * _sigmoid(x)


def _dsilu(x):
    s = _sigmoid(x)
    return s * (1.0 + x * (1.0 - s))


def _softplus(x):
    return jnp.maximum(x, 0.0) + jnp.log1p(jnp.exp(-jnp.abs(x)))


def _iota(shape, dim):
    return lax.broadcasted_iota(jnp.int32, shape, dim)


def _rsum(x):
    return jnp.sum(x, axis=1, keepdims=True)


def _csum(x):
    return jnp.sum(x, axis=0, keepdims=True)


def _pick(n, cands):
    for c in cands:
        if n % c == 0:
            return c
    return n


def _params(sem):
    return pltpu.CompilerParams(dimension_semantics=sem, vmem_limit_bytes=VMEM_LIMIT_BYTES)


MM_TILE_SIZES = (4096, 2816, 2048, 1536, 1408, 1024, 768, 512, 384, 256, 128)
MM_VMEM_BUDGET = 36 * 1024 * 1024
HBM_BYTES_PER_US = 3.0e6
GRID_STEP_US = 0.35


def _mm_tiles(m, n, kc, a_bytes, b_bytes, o_bytes):
    def sizes(d):
        return [d] if d <= 512 else [t for t in MM_TILE_SIZES if d % t == 0]

    best = None
    for tm in sizes(m):
        for tn in sizes(n):
            for tk in sizes(kc):
                vmem = 2 * (tm * tk * a_bytes + tk * tn * b_bytes + tm * tn * o_bytes) + tm * tn * 4
                if vmem > MM_VMEM_BUDGET:
                    continue
                steps = (m // tm) * (n // tn) * (kc // tk)
                traffic = m * kc * a_bytes * (n // tn) + kc * n * b_bytes * (m // tm) + m * n * o_bytes
                edge = tm * tk * a_bytes + tk * tn * b_bytes + tm * tn * o_bytes
                cost = (traffic + edge) / HBM_BYTES_PER_US + steps * GRID_STEP_US
                if best is None or cost < best[0]:
                    best = (cost, tm, tn, tk)
    return best[1:]


def mm(name, a, b, *, ta=False, tb=False, out_dtype=F32, add=None, scale=None, deps=()):
    adds = [] if add is None else (list(add) if isinstance(add, (list, tuple)) else [(1.0, add)])
    if ta:
        kc, m = a.shape
    else:
        m, kc = a.shape
    if tb:
        n, kb = b.shape
    else:
        kb, n = b.shape
    assert kc == kb, (name, a.shape, b.shape)
    tm, tn, tk = _mm_tiles(m, n, kc, a.dtype.itemsize, b.dtype.itemsize,
                           jnp.dtype(out_dtype).itemsize * (1 + len(adds)))
    nk = kc // tk
    grid = (m // tm, n // tn, nk)
    a_spec = pl.BlockSpec((tk, tm), lambda i, j, k: (k, i)) if ta else pl.BlockSpec((tm, tk), lambda i, j, k: (i, k))
    b_spec = pl.BlockSpec((tn, tk), lambda i, j, k: (j, k)) if tb else pl.BlockSpec((tk, tn), lambda i, j, k: (k, j))
    o_spec = pl.BlockSpec((tm, tn), lambda i, j, k: (i, j))
    ca, cb = (0 if ta else 1), (1 if tb else 0)

    def body(*refs):
        a_ref, b_ref = refs[0], refs[1]
        o_ref = refs[-1] if nk == 1 else refs[-2]
        k = pl.program_id(2)
        part = _dot(a_ref[...], b_ref[...], ca, cb, None)

        def finish(r):
            if scale is not None:
                r = r * scale
            for (coef, _), add_ref in zip(adds, refs[2:2 + len(adds)]):
                r = r + (add_ref[...] if coef == 1.0 else coef * add_ref[...])
            o_ref[...] = r.astype(o_ref.dtype)

        if nk == 1:
            finish(part)
            return
        acc_ref = refs[-1]

        @pl.when(k == 0)
        def _():
            acc_ref[...] = part

        if nk > 2:
            @pl.when((k > 0) & (k < nk - 1))
            def _():
                acc_ref[...] += part

        @pl.when(k == nk - 1)
        def _():
            finish(acc_ref[...] + part)

    ins = [a, b] + [t for _, t in adds] + list(deps)
    specs = [a_spec, b_spec] + [o_spec] * len(adds) + [ANY] * len(deps)
    return pl.pallas_call(
        body, name=name, grid=grid, in_specs=specs, out_specs=o_spec,
        out_shape=jax.ShapeDtypeStruct((m, n), out_dtype),
        scratch_shapes=[pltpu.VMEM((tm, tn), F32)] if nk > 1 else [],
        compiler_params=_params(("parallel", "parallel", "arbitrary")),
    )(*ins)


def mm_fan_t(name, lefts, b):
    kc, n = b.shape
    tk = min(512, kc)
    nk = kc // tk

    def body(*refs):
        k = pl.program_id(0)
        bb = refs[len(lefts)][...].astype(MMD)
        for a_ref, o_ref in zip(refs[:len(lefts)], refs[len(lefts) + 1:]):
            part = dtn(a_ref[...], bb)

            @pl.when(k == 0)
            def _(o_ref=o_ref, part=part):
                o_ref[...] = part

            @pl.when(k > 0)
            def _(o_ref=o_ref, part=part):
                o_ref[...] += part

    return pl.pallas_call(
        body, name=name, grid=(nk,),
        in_specs=[pl.BlockSpec((tk, a.shape[1]), lambda k: (k, 0)) for a in lefts] + [pl.BlockSpec((tk, n), lambda k: (k, 0))],
        out_specs=[pl.BlockSpec((a.shape[1], n), lambda k: (0, 0)) for a in lefts],
        out_shape=[jax.ShapeDtypeStruct((a.shape[1], n), F32) for a in lefts],
        compiler_params=_params(("arbitrary",)),
    )(*lefts, b)


def mm_sum(name, pairs, deps=()):
    m, n = pairs[0][0].shape[0], pairs[0][1].shape[1]
    tm = min(512, m)
    np_ = len(pairs)

    def body(*refs):
        acc = dnn(refs[0][...], refs[1][...])
        for p in range(1, np_):
            acc = acc + dnn(refs[2 * p][...], refs[2 * p + 1][...])
        refs[-1][...] = acc

    specs, ins = [], []
    for a, b in pairs:
        specs += [pl.BlockSpec((tm, a.shape[1]), lambda i: (i, 0)), pl.BlockSpec(b.shape, lambda i: (0, 0))]
        ins += [a, b]
    return pl.pallas_call(
        body, name=name, grid=(m // tm,), in_specs=specs + [ANY] * len(deps),
        out_specs=pl.BlockSpec((tm, n), lambda i: (i, 0)), out_shape=jax.ShapeDtypeStruct((m, n), F32),
        compiler_params=_params(("parallel",)),
    )(*ins, *deps)


class _Ctx:
    def __init__(self, i, nblk, tl):
        self.i, self.nblk, self.tl = i, nblk, tl


def _norm_item(it):
    if isinstance(it, tuple):
        a, w, j = it[:3]
        rows = it[3] if len(it) > 3 else None
        return a, w, j, rows
    return it, it.shape[-1], 0, None


def rowwise(name, fn, length, tl, *, rows=(), consts=(), prevs=(), nexts=(), out_rows=(), out_accs=(), deps=()):
    nblk = length // tl
    hb = tl // HALO
    nhalo = length // HALO
    arrays, specs = [], []
    for it in rows:
        a, w, j, r = _norm_item(it)
        if a.ndim == 3:
            specs.append(pl.BlockSpec((a.shape[0], tl, w), lambda i, j=j: (0, i, j)))
        else:
            specs.append(pl.BlockSpec((r or tl, w), lambda i, j=j: (i, j)))
        arrays.append(a)
    for a in consts:
        specs.append(pl.BlockSpec(a.shape, lambda i, nd=a.ndim: (0,) * nd))
        arrays.append(a)
    for it in prevs:
        a, w, j, _ = _norm_item(it)
        specs.append(pl.BlockSpec((HALO, w), lambda i, j=j: (jnp.maximum(i * hb - 1, 0), j)))
        arrays.append(a)
    for it in nexts:
        a, w, j, _ = _norm_item(it)
        specs.append(pl.BlockSpec((HALO, w), lambda i, j=j: (jnp.minimum((i + 1) * hb, nhalo - 1), j)))
        arrays.append(a)
    out_shape, out_specs = [], []
    for spec in out_rows:
        if len(spec) == 3:
            h, w, dt = spec
            out_shape.append(jax.ShapeDtypeStruct((h, length, w), dt))
            out_specs.append(pl.BlockSpec((h, tl, w), lambda i: (0, i, 0)))
        else:
            w, dt = spec
            out_shape.append(jax.ShapeDtypeStruct((length, w), dt))
            out_specs.append(pl.BlockSpec((tl, w), lambda i: (i, 0)))
    for shape, dt in out_accs:
        out_shape.append(jax.ShapeDtypeStruct(shape, dt))
        out_specs.append(pl.BlockSpec(shape, lambda i, nd=len(shape): (0,) * nd))
    n_r, n_c, n_p, n_n = len(rows), len(consts), len(prevs), len(nexts)
    n_in = n_r + n_c + n_p + n_n
    n_or = len(out_rows)
    arrays, specs = arrays + list(deps), specs + [ANY] * len(deps)

    def body(*refs):
        i = pl.program_id(0)
        vals = [r[...] for r in refs[:n_in]]
        outs = refs[n_in + len(deps):]
        ctx = _Ctx(i, nblk, tl)
        ro, ao = fn(ctx, vals[:n_r], vals[n_r:n_r + n_c], vals[n_r + n_c:n_r + n_c + n_p], vals[n_r + n_c + n_p:])
        for r, v in zip(outs[:n_or], ro, strict=True):
            r[...] = v.astype(r.dtype)
        for r, v in zip(outs[n_or:], ao, strict=True):
            @pl.when(i == 0)
            def _(r=r, v=v):
                r[...] = v.astype(r.dtype)

            @pl.when(i > 0)
            def _(r=r, v=v):
                r[...] += v.astype(r.dtype)

    res = pl.pallas_call(
        body, name=name, grid=(nblk,), in_specs=specs, out_specs=out_specs, out_shape=out_shape,
        compiler_params=_params(("arbitrary",) if out_accs else ("parallel",)),
    )(*arrays)
    return res


def _heads(x, n, w):
    return [x[:, h * w:(h + 1) * w] for h in range(n)]


def _cat(xs):
    return jnp.concatenate(xs, axis=1)


def _row_index(ctx, nrows, offset=0):
    return ctx.i * ctx.tl + offset + _iota((nrows, 1), 0)


def _ln_stats(r):
    mu = jnp.mean(r, axis=1, keepdims=True)
    d = r - mu
    var = jnp.mean(d * d, axis=1, keepdims=True)
    rstd = lax.rsqrt(var + LN_EPS)
    return d * rstd, rstd


def ln_fwd(name, terms, g, b, tl=ROWS_LIGHT, deps=()):
    coefs = [c for c, _ in terms]
    length = terms[0][1].shape[0]

    def fn(ctx, rows, consts, prevs, nexts):
        r = sum(c * t for c, t in zip(coefs, rows))
        xh, _ = _ln_stats(r)
        y = xh * consts[0] + consts[1]
        return [y, r, y], []

    return rowwise(name, fn, length, min(tl, length), rows=[t for _, t in terms], consts=[g, b],
                   out_rows=[(D_MODEL, F32), (D_MODEL, F32), (D_MODEL, BF16)], deps=deps)


def ln_bwd(name, r, terms, g, tl=ROWS, deps=()):
    coefs = [c for c, _ in terms]
    length = r.shape[0]

    def fn(ctx, rows, consts, prevs, nexts):
        xh, rstd = _ln_stats(rows[0])
        dy = sum(c * t for c, t in zip(coefs, rows[1:]))
        dxh = dy * consts[0]
        dr = rstd * (dxh - jnp.mean(dxh, axis=1, keepdims=True) - xh * jnp.mean(dxh * xh, axis=1, keepdims=True))
        return [dr, dr], [_csum(dy * xh), _csum(dy)]

    return rowwise(name, fn, length, min(tl, length), rows=[r] + [t for _, t in terms], consts=[g],
                   out_rows=[(D_MODEL, F32), (D_MODEL, BF16)], out_accs=[((1, D_MODEL), F32), ((1, D_MODEL), F32)],
                   deps=deps)


def ln_loss(name, terms, g, b, target, tl=ROWS_LIGHT):
    coefs = [c for c, _ in terms]
    length = target.shape[0]
    nt = len(terms)

    def fn(ctx, rows, consts, prevs, nexts):
        r = sum(c * t for c, t in zip(coefs, rows[:nt]))
        xh, _ = _ln_stats(r)
        err = xh * consts[0] + consts[1] - rows[nt]
        tot = _csum(_rsum(err * err)) * (0.5 / D_MODEL)
        return [err * (1.0 / D_MODEL), r], [jnp.broadcast_to(tot, (1, 128))]

    return rowwise(name, fn, length, min(tl, length), rows=[t for _, t in terms] + [target], consts=[g, b],
                   out_rows=[(D_MODEL, F32), (D_MODEL, F32)], out_accs=[((1, 128), F32)])


def _ffn_blocks(length):
    return min(512, length), D_FF // 2


def ffn_gate_up_act(name, x, wg, wu, deps=()):
    length = x.shape[0]
    tm, tn = _ffn_blocks(length)

    def body(x_ref, wg_ref, wu_ref, *rest):
        hg_ref, hu_ref, act_ref = rest[-3:]
        xb = x_ref[...].astype(MMD)
        hg = dnt(xb, wg_ref[...])
        hu = dnt(xb, wu_ref[...])
        hg_ref[...] = hg
        hu_ref[...] = hu
        act_ref[...] = (_silu(hg) * hu).astype(act_ref.dtype)

    row = pl.BlockSpec((tm, D_MODEL), lambda i, j: (i, 0))
    wsp = pl.BlockSpec((tn, D_MODEL), lambda i, j: (j, 0))
    osp = pl.BlockSpec((tm, tn), lambda i, j: (i, j))
    return pl.pallas_call(
        body, name=name, grid=(length // tm, D_FF // tn), in_specs=[row, wsp, wsp] + [ANY] * len(deps),
        out_specs=[osp] * 3,
        out_shape=[jax.ShapeDtypeStruct((length, D_FF), F32)] * 2 + [jax.ShapeDtypeStruct((length, D_FF), BF16)],
        compiler_params=_params(("parallel", "parallel")),
    )(x, wg, wu, *deps)


def ffn_dact(name, dr, wd, hg, hu, deps=()):
    length = dr.shape[0]
    tm, tn = _ffn_blocks(length)

    def body(dr_ref, wd_ref, hg_ref, hu_ref, *rest):
        dhg_ref, dhu_ref = rest[-2:]
        da = 0.5 * dnt(dr_ref[...], wd_ref[...])
        g = hg_ref[...]
        s = _sigmoid(g)
        dhg_ref[...] = (da * hu_ref[...] * (s * (1.0 + g * (1.0 - s)))).astype(dhg_ref.dtype)
        dhu_ref[...] = (da * (g * s)).astype(dhu_ref.dtype)

    row = pl.BlockSpec((tm, D_MODEL), lambda i, j: (i, 0))
    wsp = pl.BlockSpec((tn, D_MODEL), lambda i, j: (j, 0))
    osp = pl.BlockSpec((tm, tn), lambda i, j: (i, j))
    return pl.pallas_call(
        body, name=name, grid=(length // tm, D_FF // tn), in_specs=[row, wsp, osp, osp] + [ANY] * len(deps),
        out_specs=[osp] * 2, out_shape=[jax.ShapeDtypeStruct((length, D_FF), BF16)] * 2,
        compiler_params=_params(("parallel", "parallel")),
    )(dr, wd, hg, hu, *deps)


def ffn_fwd(tag, x, wg, wu, wd, deps=()):
    hg, hu, act = ffn_gate_up_act(tag + "_gate_up", x, wg, wu, deps)
    if callable(wd):
        wd = wd(act)
    f = mm(tag + "_down", act, wd)
    return f, (hg, hu, act), wd


def ffn_bwd(tag, x, res, dr, wg, wu, wd, deps=(), on_dw=None, also=None):
    on_dw = on_dw or (lambda which, dw: ())
    hg, hu, act = res
    dwd = mm(tag + "_dwd", act, dr, ta=True, scale=0.5, deps=deps)
    dhg, dhu = ffn_dact(tag + "_dact", dr, wd, hg, hu, deps=on_dw("down", dwd))
    dwg = mm(tag + "_dwg", dhg, x, ta=True)
    dwu = mm(tag + "_dwu", dhu, x, ta=True, deps=on_dw("gate", dwg))
    dx = mm(tag + "_dxg", dhg, wg, deps=on_dw("up", dwu))
    dx = mm(tag + "_dxu", dhu, wu, add=[(1.0, dx)] + ([also] if also else []))
    return dx, dwg, dwu, dwd


def _conv_taps(ext, taps, n):
    out = taps[3] * ext
    for j in range(3):
        out = out + taps[j] * pltpu.roll(ext, 3 - j, 0)
    return out


def _l2n(x):
    r = lax.rsqrt(_rsum(x * x) + L2_EPS)
    return x * r, r


def conv_fwd(name, pre, taps, tl=ROWS_WIDE, deps=()):
    length = pre.shape[0]
    tl = min(tl, length)

    def fn(ctx, rows, consts, prevs, nexts):
        prev = jnp.where(ctx.i > 0, prevs[0], 0.0)
        ext = jnp.concatenate([prev, rows[0]], axis=0)
        s = _silu(_conv_taps(ext, consts, tl + HALO)[HALO:])
        q = _cat([_l2n(x)[0] * (HD ** -0.5) for x in _heads(s[:, :DN_WIDTH], DN_HEADS, HD)])
        k = _cat([_l2n(x)[0] for x in _heads(s[:, DN_WIDTH:2 * DN_WIDTH], DN_HEADS, HD)])
        return [q, k, s[:, 2 * DN_WIDTH:]], []

    return rowwise(name, fn, length, tl, rows=[pre], consts=list(taps), prevs=[pre],
                   out_rows=[(DN_WIDTH, F32)] * 3, deps=deps)


def conv_bwd(name, pre, dq, dk, dv, taps, tl=ROWS_WIDE):
    length = pre.shape[0]
    tl = min(tl, length)
    n = tl + 2 * HALO

    def fn(ctx, rows, consts, prevs, nexts):
        last = ctx.i == ctx.nblk - 1
        prev = jnp.where(ctx.i > 0, prevs[0], 0.0)
        ext = jnp.concatenate([prev, rows[0], nexts[0]], axis=0)
        c = _conv_taps(ext, consts, n)
        sg = _sigmoid(c)
        s = c * sg
        zero = jnp.zeros((HALO, DN_WIDTH), F32)
        dqe, dke, dve = [jnp.concatenate([zero, rows[1 + t], jnp.where(last, 0.0, nexts[1 + t])], axis=0)
                         for t in range(3)]

        def l2_bwd(x, dy):
            y, r = _l2n(x)
            return r * (dy - y * _rsum(dy * y))

        dsq = _cat([l2_bwd(x, d * (HD ** -0.5)) for x, d in zip(_heads(s[:, :DN_WIDTH], DN_HEADS, HD),
                                                                 _heads(dqe, DN_HEADS, HD))])
        dsk = _cat([l2_bwd(x, d) for x, d in zip(_heads(s[:, DN_WIDTH:2 * DN_WIDTH], DN_HEADS, HD),
                                                  _heads(dke, DN_HEADS, HD))])
        dc = _cat([dsq, dsk, dve]) * (sg * (1.0 + c * (1.0 - sg)))
        dpre = consts[3] * dc
        for j in range(3):
            dpre = dpre + consts[j] * pltpu.roll(dc, n - (3 - j), 0)
        dc_cur = dc[HALO:HALO + tl]
        dws = [_csum(dc_cur * pltpu.roll(ext, 3 - j, 0)[HALO:HALO + tl]) for j in range(3)]
        dws.append(_csum(dc_cur * ext[HALO:HALO + tl]))
        return [dpre[HALO:HALO + tl]], dws

    return rowwise(name, fn, length, tl, rows=[pre, dq, dk, dv], consts=list(taps), prevs=[pre],
                   nexts=[pre, dq, dk, dv], out_rows=[(3 * DN_WIDTH, BF16)],
                   out_accs=[((1, 3 * DN_WIDTH), F32)] * 4)


def _gate_math(ab, alog, dtb):
    z = ab + dtb
    g = -jnp.exp(alog) * _softplus(z)
    beta = _sigmoid(ab)
    return z, g, beta


def gates_fwd(name, ab, alog, dtb, tl=ROWS_LIGHT):
    length = ab.shape[0]

    def fn(ctx, rows, consts, prevs, nexts):
        _, g, beta = _gate_math(rows[0], consts[0], consts[1])
        spread = [jnp.broadcast_to(v[:, h:h + 1], (v.shape[0], HD))
                  for v, first in ((g, 0), (beta, DN_HEADS)) for h in range(first, first + DN_HEADS)]
        return [_cat(spread[:DN_HEADS]), _cat(spread[DN_HEADS:])], []

    return rowwise(name, fn, length, min(tl, length), rows=[ab], consts=[alog, dtb],
                   out_rows=[(DN_WIDTH, F32)] * 2)


def gates_bwd(name, ab, dgb, dbb, alog, dtb, tl=ROWS_LIGHT):
    length = ab.shape[0]

    def fn(ctx, rows, consts, prevs, nexts):
        z, g, beta = _gate_math(rows[0], consts[0], consts[1])
        lane = _iota(g.shape, 1)
        dsmall = jnp.zeros_like(g)
        for h in range(DN_HEADS):
            dsmall = jnp.where(lane == h, rows[1][:, h * HD:h * HD + 1], dsmall)
            dsmall = jnp.where(lane == DN_HEADS + h, rows[2][:, h * HD:h * HD + 1], dsmall)
        is_a = lane < DN_HEADS
        da = jnp.where(is_a, dsmall * (-jnp.exp(consts[0])) * _sigmoid(z), 0.0)
        db = jnp.where((lane >= DN_HEADS) & (lane < 2 * DN_HEADS), dsmall * beta * (1.0 - beta), 0.0)
        return [da + db], [_csum(jnp.where(is_a, dsmall * g, 0.0)), _csum(da)]

    return rowwise(name, fn, length, min(tl, length), rows=[ab, dgb, dbb], consts=[alog, dtb],
                   out_rows=[(128, BF16)], out_accs=[((1, 128), F32)] * 2)


CPS = 4


def _chunk_scan_rows(x, suffix=False):
    n = x.shape[0]
    rc = _iota(x.shape, 0) & (CHUNK - 1)
    sh = 1
    while sh < CHUNK:
        if suffix:
            x = x + jnp.where(rc < CHUNK - sh, pltpu.roll(x, n - sh, 0), 0.0)
        else:
            x = x + jnp.where(rc >= sh, pltpu.roll(x, sh, 0), 0.0)
        sh *= 2
    return x


def _tri_inv(a_list, eye, bd):
    def each(f, *ls):
        return [f(*xs) for xs in zip(*ls)]

    dg = [jnp.where(bd, a, 0.0) for a in a_list]
    lo = each(lambda a, d: a - d, a_list, dg)
    n1 = [-d for d in dg]
    n2 = each(lambda n: dnn(n, n, X3), n1)
    n4 = each(lambda n: dnn(n, n, X3), n2)
    td = each(lambda p, s: dnn(eye + p, eye + s, X3), n1, n2)
    n8 = each(lambda n: dnn(n, n, X3), n4)
    td = each(lambda t, n: dnn(t, eye + n, X3), td, n4)
    td = each(lambda t, n: dnn(t, eye + n, X3), td, n8)
    m = each(lambda t, l: dnn(t, l, X3), td, lo)
    m2 = each(lambda x: dnn(x, x, X3), m)
    x = each(lambda p, s: dnn(eye - p, eye + s, X3), m, m2)
    return each(lambda p, t: dnn(p, t, X3), x, td)


def _chunk_common(q, k, v, gcb, bb):
    egb = jnp.exp(gcb)
    gc64 = gcb[:, :CHUNK]
    ii, jj = _iota((CHUNK, CHUNK), 0), _iota((CHUNK, CHUNK), 1)
    incl, strict = ii >= jj, ii > jj
    decay = jnp.exp(jnp.where(incl, gc64 - gc64.T, -jnp.inf))
    kb = k * bb
    vb = v * bb
    kbe = kb * egb
    pq = dnt(jnp.concatenate([kb, q], axis=0), k, X3)
    ekb = jnp.exp(gcb[CHUNK - 1:CHUNK, :] - gcb)
    return dict(egb=egb, decay=decay, kb=kb, vb=vb, kbe=kbe, pm=pq[:CHUNK], qm=pq[CHUNK:], ekb=ekb,
                incl=incl, strict=strict, ii=ii, jj=jj)


def _chunk_head(vals, ci, h):
    return [v[ci * CHUNK:(ci + 1) * CHUNK, h * HD:(h + 1) * HD] for v in vals]


def _assemble(per_chunk):
    return jnp.concatenate([_cat(hs) for hs in per_chunk], axis=0)


def _assemble3(per_chunk):
    return jnp.stack([jnp.concatenate([per_chunk[ci][h] for ci in range(CPS)], axis=0) for h in range(DN_HEADS)])


def delta_prep_fwd(name, q, k, v, gb, bb):
    length = q.shape[0]

    def fn(ctx, rows, consts, prevs, nexts):
        gcb_all = _chunk_scan_rows(rows[3])
        vals = [rows[0], rows[1], rows[2], gcb_all, rows[4]]
        units = [(ci, h) for ci in range(CPS) for h in range(DN_HEADS)]
        ins = [_chunk_head(vals, ci, h) for ci, h in units]
        cs = [_chunk_common(*i) for i in ins]
        eye = (cs[0]["ii"] == cs[0]["jj"]).astype(F32)
        ts = _tri_inv([jnp.where(c["strict"], c["pm"] * c["decay"], 0.0) for c in cs], eye,
                      (cs[0]["ii"] >> 4) == (cs[0]["jj"] >> 4))
        uws = [dnn(t, _cat([c["vb"], c["kbe"]]), X3) for t, c in zip(ts, cs)]

        def grid2(xs):
            return [xs[ci * DN_HEADS:(ci + 1) * DN_HEADS] for ci in range(CPS)]

        return [_assemble(grid2([uw[:, :HD] for uw in uws])), _assemble(grid2([uw[:, HD:] for uw in uws])),
                _assemble(grid2([i[0] * c["egb"] for i, c in zip(ins, cs)])),
                _assemble(grid2([i[1] * c["ekb"] for i, c in zip(ins, cs)])), gcb_all,
                _assemble3(grid2([c["qm"] * c["decay"] for c in cs])), _assemble3(grid2(ts))], []

    return rowwise(name, fn, length, CHUNK * CPS, rows=[q, k, v, gb, bb],
                   out_rows=[(DN_WIDTH, F32)] * 5 + [(DN_HEADS, CHUNK, F32)] * 2)


def delta_prep_bwd(name, q, k, v, gb, bb, t3, du, dw, dqd, dkd, dattn3, dgl):
    length = q.shape[0]

    def fn(ctx, rows, consts, prevs, nexts):
        gcb_all = _chunk_scan_rows(rows[3])
        vals = [rows[0], rows[1], rows[2], gcb_all] + list(rows[4:9])
        t3v, da3v, dglv = rows[9], rows[10], rows[11]
        units = [(ci, h) for ci in range(CPS) for h in range(DN_HEADS)]
        ins = [_chunk_head(vals, ci, h) for ci, h in units]
        cs = [_chunk_common(*i[:5]) for i in ins]
        ts = [t3v[h][ci * CHUNK:(ci + 1) * CHUNK] for ci, h in units]
        dattns = [jnp.where(c["incl"], da3v[h][ci * CHUNK:(ci + 1) * CHUNK], 0.0) for (ci, h), c in zip(units, cs)]
        duws = [_cat([i[5], i[6]]) for i in ins]
        dvks = [dtn(t, d, X3) for t, d in zip(ts, duws)]
        dts = [dnt(d, _cat([c["vb"], c["kbe"]]), X3) for d, c in zip(duws, cs)]
        dts = [dnt(d, t, X3) for d, t in zip(dts, ts)]
        das = [jnp.where(c["strict"], -dtn(t, d, X3), 0.0) for c, t, d in zip(cs, ts, dts)]
        dpqs = [jnp.concatenate([da * c["decay"], dat * c["decay"]], axis=0) for da, dat, c in zip(das, dattns, cs)]
        dpqks = [dnn(d, i[1], X3) for d, i in zip(dpqs, ins)]
        dkps = [dtn(d, jnp.concatenate([c["kb"], i[0]], axis=0), X3) for d, c, i in zip(dpqs, cs, ins)]
        dqs, dks, dvs, dgcs, dbs = [], [], [], [], []
        for (ci, h), i, c, dvk, da, dattn, dpqk, dkp in zip(units, ins, cs, dvks, das, dattns, dpqks, dkps):
            qh, kh, vh, _, bh, _, _, dqdh, dkdh = i
            dvb, dkbe = dvk[:, :HD], dvk[:, HD:]
            dkb = dpqk[:CHUNK] + dkbe * c["egb"]
            c1 = _rsum(dkbe * c["kb"] + dqdh * qh) * c["egb"]
            c2 = _rsum(dkdh * kh) * c["ekb"]
            e = (da * c["pm"] + dattn * c["qm"]) * c["decay"]
            dgc = c1 - c2 + _rsum(e) - _rsum(e.T)
            dgl_tot = jnp.max(dglv[ci * 8:(ci + 1) * 8, h * HD:(h + 1) * HD], axis=0, keepdims=True) + _csum(c2)
            dgcs.append(dgc + jnp.where(_iota((CHUNK, HD), 0) == CHUNK - 1, dgl_tot, 0.0))
            dqs.append(dpqk[CHUNK:] + dqdh * c["egb"])
            dks.append(dkp + dkdh * c["ekb"] + dkb * bh)
            dvs.append(dvb * bh)
            dbs.append(jnp.broadcast_to(_rsum(dkb * kh + dvb * vh), (CHUNK, HD)))

        def grid2(xs):
            return [xs[ci * DN_HEADS:(ci + 1) * DN_HEADS] for ci in range(CPS)]

        return [_assemble(grid2(dqs)), _assemble(grid2(dks)), _assemble(grid2(dvs)),
                _chunk_scan_rows(_assemble(grid2(dgcs)), suffix=True), _assemble(grid2(dbs))], []

    return rowwise(name, fn, length, CHUNK * CPS,
                   rows=[q, k, v, gb, bb, du, dw, dqd, dkd, t3, dattn3, (dgl, DN_WIDTH, 0, 8 * CPS)],
                   out_rows=[(DN_WIDTH, F32)] * 5)


SCAN_CHUNKS = 8


def _scan_chunks(n):
    return SCAN_CHUNKS if n % SCAN_CHUNKS == 0 else 1


def delta_scan_fwd(name, qd, kd, u, w, attn3, gcb):
    length = qd.shape[0]
    n = length // CHUNK
    sc = _scan_chunks(n)
    row = pl.BlockSpec((sc * CHUNK, DN_WIDTH), lambda c: (c, 0))
    sq = pl.BlockSpec((DN_HEADS, sc * CHUNK, CHUNK), lambda c: (0, c, 0))

    def body(qd_ref, kd_ref, u_ref, w_ref, attn_ref, gc_ref, o_ref, vn_ref, st_ref, s_ref):
        c = pl.program_id(0)

        @pl.when(c == 0)
        def _():
            s_ref[...] = jnp.zeros_like(s_ref)

        heads = range(DN_HEADS)
        sls = [pl.ds(h * HD, HD) for h in heads]
        ss = [s_ref[h] for h in heads]
        for ci in range(sc):
            rs = pl.ds(ci * CHUNK, CHUNK)
            ws = [dnn(w_ref[rs, sl], s) for sl, s in zip(sls, ss)]
            qs = [dnn(qd_ref[rs, sl], s) for sl, s in zip(sls, ss)]
            vns = [u_ref[rs, sl] - x for sl, x in zip(sls, ws)]
            avs = [dnn(attn_ref[h, rs, :], vn) for h, vn in zip(heads, vns)]
            kvs = [dtn(kd_ref[rs, sl], vn) for sl, vn in zip(sls, vns)]
            for h, sl in zip(heads, sls):
                st_ref[ci, h] = ss[h]
                o_ref[rs, sl] = qs[h] + avs[h]
                vn_ref[rs, sl] = vns[h]
            ss = [s * jnp.exp(gc_ref[pl.ds(ci * CHUNK + CHUNK - 1, 1), sl]) + kv for s, sl, kv in zip(ss, sls, kvs)]
        for h in heads:
            s_ref[h] = ss[h]

    return pl.pallas_call(
        body, name=name, grid=(n // sc,), in_specs=[row, row, row, row, sq, row],
        out_specs=[row, row, pl.BlockSpec((sc, DN_HEADS, HD, HD), lambda c: (c, 0, 0, 0))],
        out_shape=[jax.ShapeDtypeStruct((length, DN_WIDTH), F32), jax.ShapeDtypeStruct((length, DN_WIDTH), F32),
                   jax.ShapeDtypeStruct((n, DN_HEADS, HD, HD), F32)],
        scratch_shapes=[pltpu.VMEM((DN_HEADS, HD, HD), F32)],
        compiler_params=_params(("arbitrary",)),
    )(qd, kd, u, w, attn3, gcb)


def delta_scan_bwd(name, do, qd, kd, w, attn3, vn, st, gcb):
    length = qd.shape[0]
    n = length // CHUNK
    sc = _scan_chunks(n)
    nb = n // sc
    row = pl.BlockSpec((sc * CHUNK, DN_WIDTH), lambda c: (nb - 1 - c, 0))
    sq = pl.BlockSpec((DN_HEADS, sc * CHUNK, CHUNK), lambda c: (0, nb - 1 - c, 0))
    stb = pl.BlockSpec((sc, DN_HEADS, HD, HD), lambda c: (nb - 1 - c, 0, 0, 0))
    glb = pl.BlockSpec((sc * 8, DN_WIDTH), lambda c: (nb - 1 - c, 0))

    def body(do_ref, qd_ref, kd_ref, w_ref, attn_ref, vn_ref, st_ref, gc_ref,
             dqd_ref, dkd_ref, du_ref, dw_ref, dattn_ref, dgl_ref, ds_ref):
        c = pl.program_id(0)

        @pl.when(c == 0)
        def _():
            ds_ref[...] = jnp.zeros_like(ds_ref)

        heads = range(DN_HEADS)
        sls = [pl.ds(h * HD, HD) for h in heads]
        dsns = [ds_ref[h] for h in heads]
        for ci in reversed(range(sc)):
            rs = pl.ds(ci * CHUNK, CHUNK)
            ss = [st_ref[ci, h] for h in heads]
            dos = [do_ref[rs, sl] for sl in sls]
            vns = [vn_ref[rs, sl] for sl in sls]
            dvns = [dtn(attn_ref[h, rs, :], d) for h, d in zip(heads, dos)]
            dvns = [x + dnn(kd_ref[rs, sl], dsn) for x, sl, dsn in zip(dvns, sls, dsns)]
            qdos = [dtn(qd_ref[rs, sl], d) for sl, d in zip(sls, dos)]
            for h, sl in zip(heads, sls):
                dattn_ref[h, rs, :] = dnt(dos[h], vns[h])
                dqd_ref[rs, sl] = dnt(dos[h], ss[h])
                dkd_ref[rs, sl] = dnt(vns[h], dsns[h])
                du_ref[rs, sl] = dvns[h]
            dws = [dnt(dvn, s) for dvn, s in zip(dvns, ss)]
            wdvs = [dtn(w_ref[rs, sl], dvn) for sl, dvn in zip(sls, dvns)]
            nxt = []
            for h, sl in zip(heads, sls):
                egl = jnp.exp(gc_ref[pl.ds(ci * CHUNK + CHUNK - 1, 1), sl])
                dw_ref[rs, sl] = -dws[h]
                dgl_ref[pl.ds(ci * 8, 8), sl] = jnp.broadcast_to(_csum(_rsum(dsns[h] * ss[h])) * egl, (8, HD))
                nxt.append(dsns[h] * egl + qdos[h] - wdvs[h])
            dsns = nxt
        for h in heads:
            ds_ref[h] = dsns[h]

    return pl.pallas_call(
        body, name=name, grid=(nb,), in_specs=[row, row, row, row, sq, row, stb, row],
        out_specs=[row, row, row, row, sq, glb],
        out_shape=[jax.ShapeDtypeStruct((length, DN_WIDTH), F32)] * 4
        + [jax.ShapeDtypeStruct((DN_HEADS, length, CHUNK), F32), jax.ShapeDtypeStruct((n * 8, DN_WIDTH), F32)],
        scratch_shapes=[pltpu.VMEM((DN_HEADS, HD, HD), F32)],
        compiler_params=_params(("arbitrary",)),
    )(do, qd, kd, w, attn3, vn, st, gcb)


def onorm_fwd(name, o, z, nw, tl=ROWS_LIGHT):
    length = o.shape[0]

    def fn(ctx, rows, consts, prevs, nexts):
        outs = []
        for oh, zh in zip(_heads(rows[0], DN_HEADS, HD), _heads(rows[1], DN_HEADS, HD)):
            r = lax.rsqrt(jnp.mean(oh * oh, axis=1, keepdims=True) + RMS_EPS)
            outs.append(oh * r * consts[0] * _silu(zh))
        return [_cat(outs)], []

    return rowwise(name, fn, length, min(tl, length), rows=[o, z], consts=[nw], out_rows=[(DN_WIDTH, BF16)])[0]


def onorm_bwd(name, o, z, d_on, nw, tl=ROWS_LIGHT):
    length = o.shape[0]

    def fn(ctx, rows, consts, prevs, nexts):
        dos, dzs = [], []
        dnw = jnp.zeros((1, HD), F32)
        for oh, zh, dh in zip(*[_heads(r, DN_HEADS, HD) for r in rows]):
            r = lax.rsqrt(jnp.mean(oh * oh, axis=1, keepdims=True) + RMS_EPS)
            y = oh * r
            sz = _silu(zh)
            t = dh * sz * consts[0]
            dos.append(r * (t - y * jnp.mean(t * y, axis=1, keepdims=True)))
            dzs.append(dh * y * consts[0] * _dsilu(zh))
            dnw = dnw + _csum(dh * y * sz)
        return [_cat(dos), _cat(dzs)], [dnw]

    return rowwise(name, fn, length, min(tl, length), rows=[o, z, d_on], consts=[nw],
                   out_rows=[(DN_WIDTH, F32), (DN_WIDTH, BF16)], out_accs=[((1, HD), F32)])


def merge_fwd(name, gates, ydn, ypool, tl=ROWS):
    length = ydn.shape[0]

    def fn(ctx, rows, consts, prevs, nexts):
        gt = rows[0]
        return [_sigmoid(gt[:, :D_MODEL]) * rows[1] + _sigmoid(gt[:, D_MODEL:]) * rows[2]], []

    return rowwise(name, fn, length, min(tl, length), rows=[gates, ydn, ypool], out_rows=[(D_MODEL, BF16)])[0]


def merge_bwd(name, gates, ydn, ypool, dm, tl=ROWS_WIDE):
    length = ydn.shape[0]

    def fn(ctx, rows, consts, prevs, nexts):
        gt, yd, yp, d = rows
        sd, sp = _sigmoid(gt[:, :D_MODEL]), _sigmoid(gt[:, D_MODEL:])
        dgates = _cat([d * yd * sd * (1.0 - sd), d * yp * sp * (1.0 - sp)])
        return [d * sd, d * sp, dgates], []

    return rowwise(name, fn, length, min(tl, length), rows=[gates, ydn, ypool, dm],
                   out_rows=[(D_MODEL, BF16), (D_MODEL, BF16), (2 * D_MODEL, BF16)])


def _trailing_sums(ext, upto):
    s, sh = ext, 1
    while sh < upto:
        s = s + pltpu.roll(s, sh, 0)
        sh *= 2
    return s


def _leading_sums(ext, upto, n):
    s, sh = ext, 1
    while sh < upto:
        s = s + pltpu.roll(s, n - sh, 0)
        sh *= 2
    return s


def _pool_mixed(ctx, p, prev, tl):
    prevm = jnp.where(ctx.i > 0, prev, 0.0)
    t1 = (_row_index(ctx, tl) + 1).astype(F32)
    outs = []
    for gi, win in enumerate(POOL_WINDOWS):
        sl = slice(gi * HD, (gi + 1) * HD)
        ext = jnp.concatenate([prevm[:, sl], p[:, sl]], axis=0)
        mean = _trailing_sums(ext, win)[HALO:] / jnp.minimum(t1, float(win))
        outs.append(mean - p[:, sl])
    return outs


def pool_fwd(name, p, pool_w, scale, tl=ROWS_LIGHT):
    length = p.shape[0]
    tl = min(tl, length)

    def fn(ctx, rows, consts, prevs, nexts):
        mixed = _pool_mixed(ctx, rows[0], prevs[0], tl)
        y = _cat([dnn(m, consts[0][gi]) for gi, m in enumerate(mixed)])
        return [y * consts[1]], []

    return rowwise(name, fn, length, tl, rows=[p], consts=[pool_w, scale], prevs=[p],
                   out_rows=[(POOL_WIDTH, BF16)])[0]


def pool_bwd(name, p, dpo, pool_w, scale, tl=ROWS_LIGHT):
    length = p.shape[0]
    tl = min(tl, length)
    n = tl + HALO

    def fn(ctx, rows, consts, prevs, nexts):
        last = ctx.i == ctx.nblk - 1
        mixed = _pool_mixed(ctx, rows[0], prevs[0], tl)
        dext = jnp.concatenate([rows[1], jnp.where(last, 0.0, nexts[0])], axis=0)
        t1 = (_row_index(ctx, n) + 1).astype(F32)
        dps, dws, dscs = [], [], []
        for gi, win in enumerate(POOL_WINDOWS):
            sl = slice(gi * HD, (gi + 1) * HD)
            wg = consts[0][gi]
            dyraw = dext[:, sl] * consts[1][:, sl]
            dmix = dnt(dyraw, wg)
            dws.append(dtn(mixed[gi], dyraw[:tl]))
            dscs.append(_csum(rows[1][:, sl] * dnn(mixed[gi], wg)))
            lead = _leading_sums(dmix / jnp.minimum(t1, float(win)), win, n)
            dps.append(lead[:tl] - dmix[:tl])
        return [_cat(dps)], [jnp.stack(dws), _cat(dscs)]

    return rowwise(name, fn, length, tl, rows=[p, dpo], consts=[pool_w, scale], prevs=[p], nexts=[dpo],
                   out_rows=[(POOL_WIDTH, BF16)],
                   out_accs=[((len(POOL_WINDOWS), HD, HD), F32), ((1, POOL_WIDTH), F32)])


def _xa_probs(qh, kh):
    s = dnt(qh, kh) * (XA_HD ** -0.5)
    e = jnp.exp(s - jnp.max(s, axis=1, keepdims=True))
    return e / _rsum(e)


def xattn_fwd(name, qx, kx, vx, tl=ROWS_LIGHT):
    length = qx.shape[0]

    def fn(ctx, rows, consts, prevs, nexts):
        outs = [dnn(_xa_probs(qh, kh), vh) for qh, kh, vh in
                zip(_heads(rows[0], XA_HEADS, XA_HD), _heads(consts[0], XA_HEADS, XA_HD),
                    _heads(consts[1], XA_HEADS, XA_HD))]
        return [_cat(outs)], []

    return rowwise(name, fn, length, min(tl, length), rows=[qx], consts=[kx, vx], out_rows=[(D_MODEL, BF16)])[0]


def xattn_bwd(name, qx, dox, kx, vx, tl=ROWS):
    length = qx.shape[0]

    def fn(ctx, rows, consts, prevs, nexts):
        dqs, dks, dvs = [], [], []
        for qh, dh, kh, vh in zip(_heads(rows[0], XA_HEADS, XA_HD), _heads(rows[1], XA_HEADS, XA_HD),
                                  _heads(consts[0], XA_HEADS, XA_HD), _heads(consts[1], XA_HEADS, XA_HD)):
            pr = _xa_probs(qh, kh)
            dpr = dnt(dh, vh)
            ds = pr * (dpr - _rsum(dpr * pr)) * (XA_HD ** -0.5)
            dqs.append(dnn(ds, kh))
            dks.append(dtn(ds, qh))
            dvs.append(dtn(pr, dh))
        return [_cat(dqs)], [_cat(dks), _cat(dvs)]

    return rowwise(name, fn, length, min(tl, length), rows=[qx, dox], consts=[kx, vx],
                   out_rows=[(D_MODEL, BF16)], out_accs=[((N_MEM, D_MODEL), F32)] * 2)


def local_step(x, mem, target, w, io):
    alog = jnp.pad(w["a_log"], ((0, 0), (0, 128 - DN_HEADS)))
    dtb = jnp.pad(w["dt_bias"], ((0, 0), (0, 128 - DN_HEADS)))

    f1, res1, w_down1 = ffn_fwd("ffn1", x, w["ffn1_w_gate"], w["ffn1_w_up"], io.ffn1_down, deps=io.rest_started())
    x1, r1, x1b = ln_fwd("ln1", [(ALPHA, x), (0.5, f1)], w["ln1_g"], w["ln1_b"], deps=io.halfway("mixer", f1))
    w = dict(w, ffn1_w_down=w_down1, **io.weights("mixer", x1))
    taps = [w["conv_w"][j:j + 1] for j in range(4)]

    pre = mm("in_qkv", x1b, w["in_qkv"], tb=True)
    z = mm("in_z", x1b, w["in_z"], tb=True)
    gates = mm("in_gates", x1b, w["in_gates"], tb=True)
    p = mm("in_p", x1b, w["in_p"], tb=True)
    ab = mm("in_ab", x1b, w["in_ab"], tb=True)
    q, k, v = conv_fwd("conv", pre, taps, deps=io.halfway("xa", pre))
    gb, bb = gates_fwd("gates", ab, alog, dtb)
    u, wd_, qd, kd, gcb, attn3, t3 = delta_prep_fwd("dprep", q, k, v, gb, bb)
    o, vn, st = delta_scan_fwd("dscan", qd, kd, u, wd_, attn3, gcb)
    on = onorm_fwd("onorm", o, z, w["dn_norm_w"])
    ydn = mm("dn_branch", on, w["w_dn_branch"], tb=True)
    po = pool_fwd("pool", p, w["pool_w"], w["pool_scale"])
    ypool = mm("pool_branch", po, w["w_pool_branch"], tb=True)
    merged = merge_fwd("merge", gates, ydn, ypool)
    mix = mm("mix_out", merged, w["w_mix_out"])
    x2, r2, x2b = ln_fwd("ln2", [(ALPHA, x1), (1.0, mix)], w["ln2_g"], w["ln2_b"])

    w = dict(w, **io.weights("xa", x2))
    _, _, m = ln_fwd("ln_mem", [(1.0, mem)], w["mem_ln_g"], w["mem_ln_b"])
    qx = mm("xa_q", x2b, w["xa_wq"], deps=io.halfway("ffn2", x2))
    kx = mm("xa_k", m, w["xa_wk"])
    vx = mm("xa_v", m, w["xa_wv"])
    ox = xattn_fwd("xattn", qx, kx, vx)
    xa = mm("xa_o", ox, w["xa_wo"])
    x3, r3, x3b = ln_fwd("ln3", [(ALPHA, x2), (1.0, xa)], w["ln3_g"], w["ln3_b"])
    w = dict(w, **io.weights("ffn2", x3))

    f2, res2, _ = ffn_fwd("ffn2", x3b, w["ffn2_w_gate"], w["ffn2_w_up"], w["ffn2_w_down"])
    dy4, r4, loss = ln_loss("ln4_loss", [(ALPHA, x3), (0.5, f2)], w["ln4_g"], w["ln4_b"], target)

    g = {}
    dr4, dr4b, g["ln4_g"], g["ln4_b"] = ln_bwd("ln4_b", r4, [(1.0, dy4)], w["ln4_g"])
    dx3, g["ffn2_w_gate"], g["ffn2_w_up"], g["ffn2_w_down"] = ffn_bwd(
        "ffn2b", x3b, res2, dr4b, w["ffn2_w_gate"], w["ffn2_w_up"], w["ffn2_w_down"])
    dep = io.grads_out("ffn2", g)
    dr3, dr3b, g["ln3_g"], g["ln3_b"] = ln_bwd("ln3_b", r3, [(ALPHA, dr4), (1.0, dx3)], w["ln3_g"], deps=dep)

    dox = mm("xa_do", dr3b, w["xa_wo"], tb=True)
    g["xa_wo"] = mm("xa_dwo", ox, dr3b, ta=True)
    dqx, dkx, dvx = xattn_bwd("xattn_b", qx, dox, kx, vx)
    g["xa_wq"] = mm("xa_dwq", x2b, dqx, ta=True)
    dx2 = mm("xa_dx", dqx, w["xa_wq"], tb=True)
    g["xa_wk"] = mm("xa_dwk", m, dkx, ta=True)
    g["xa_wv"] = mm("xa_dwv", m, dvx, ta=True)
    dmm = mm("xa_dmk", dkx, w["xa_wk"], tb=True, deps=io.grads_out("xa", g))
    dmm = mm("xa_dmv", dvx, w["xa_wv"], tb=True, add=dmm)
    _, _, g["mem_ln_g"], g["mem_ln_b"] = ln_bwd("ln_mem_b", mem, [(1.0, dmm)], w["mem_ln_g"])
    dr2, dr2b, g["ln2_g"], g["ln2_b"] = ln_bwd("ln2_b", r2, [(ALPHA, dr3), (1.0, dx2)], w["ln2_g"])
    io.grads_in("ffn2", dr2)

    dmerged = mm("mix_dm", dr2b, w["w_mix_out"], tb=True)
    g["w_mix_out"] = mm("mix_dw", merged, dr2b, ta=True)
    d_ydn, d_ypool, d_gates = merge_bwd("merge_b", gates, ydn, ypool, dmerged)
    g["w_dn_branch"] = mm("dn_dw", d_ydn, on, ta=True)
    d_on = mm("dn_dx", d_ydn, w["w_dn_branch"])
    g["w_pool_branch"] = mm("pool_dw", d_ypool, po, ta=True)
    d_po = mm("pool_dx", d_ypool, w["w_pool_branch"])
    dp, g["pool_w"], g["pool_scale"] = pool_bwd("pool_b", p, d_po, w["pool_w"], w["pool_scale"])
    d_o, dz, g["dn_norm_w"] = onorm_bwd("onorm_b", o, z, d_on, w["dn_norm_w"])
    dqd, dkd, du, dw_, dattn3, dgl = delta_scan_bwd("dscan_b", d_o, qd, kd, wd_, attn3, vn, st, gcb)
    dq, dk, dv, dgb, dbb = delta_prep_bwd("dprep_b", q, k, v, gb, bb, t3, du, dw_, dqd, dkd, dattn3, dgl)
    dpre, dc0, dc1, dc2, dc3 = conv_bwd("conv_b", pre, dq, dk, dv, taps)
    g["conv_w"] = jnp.concatenate([dc0, dc1, dc2, dc3], axis=0)
    d_ab, dalog, ddtb = gates_bwd("gates_b", ab, dgb, dbb, alog, dtb)
    g["a_log"] = dalog[:, :DN_HEADS]
    g["dt_bias"] = ddtb[:, :DN_HEADS]
    g["in_qkv"], g["in_z"], g["in_ab"] = mm_fan_t("in_dw_a", [dpre, dz, d_ab], x1b)
    g["in_gates"], g["in_p"] = mm_fan_t("in_dw_b", [d_gates, dp], x1b)
    io.grads_in("xa", g["in_ab"])
    dx1 = mm_sum("in_dx", [(dpre, w["in_qkv"]), (dz, w["in_z"]), (d_gates, w["in_gates"]), (dp, w["in_p"]),
                           (d_ab, w["in_ab"])], deps=io.grads_out("mixer", g))
    dr1, dr1b, g["ln1_g"], g["ln1_b"] = ln_bwd("ln1_b", r1, [(ALPHA, dr2), (1.0, dx1)], w["ln1_g"])

    def on_dw(which, dw):
        name = "ffn1_w_" + which
        small = io.small_out(dict(g, loss=loss[0, :1])) if which == "down" else ()
        return small + io.grads_out(name, {name: dw})

    grad_x, g["ffn1_w_gate"], g["ffn1_w_up"], g["ffn1_w_down"] = ffn_bwd(
        "ffn1b", x, res1, dr1b, w["ffn1_w_gate"], w["ffn1_w_up"], w["ffn1_w_down"], on_dw=on_dw, also=(ALPHA, dr1))
    return loss, grad_x, g


WEIGHT_NAMES = ['ffn1_w_gate', 'ffn1_w_up', 'ffn1_w_down', 'ln1_g', 'ln1_b', 'w_in', 'conv_w', 'a_log', 'dt_bias',
                'dn_norm_w', 'w_dn_branch', 'pool_w', 'pool_scale', 'w_pool_branch', 'w_mix_out', 'ln2_g', 'ln2_b',
                'mem_ln_g', 'mem_ln_b', 'xa_wq', 'xa_wk', 'xa_wv', 'xa_wo', 'ln3_g', 'ln3_b', 'ffn2_w_gate',
                'ffn2_w_up', 'ffn2_w_down', 'ln4_g', 'ln4_b']
SHARDED = [
    ("ffn1_w_gate", "cols", (1024, 352)), ("ffn1_w_up", "cols", (1024, 352)), ("ffn1_w_down", "rows", (352, 1024)),
    ("w_in", "cols", (1024, 577)), ("conv_w", "flat", (4, 192)), ("w_dn_branch", "cols", (512, 128)),
    ("w_pool_branch", "cols", (512, 128)), ("w_mix_out", "rows", (128, 1024)), ("xa_wq", "rows", (128, 1024)),
    ("xa_wk", "rows", (128, 1024)), ("xa_wv", "rows", (128, 1024)), ("xa_wo", "rows", (128, 1024)),
    ("ffn2_w_gate", "cols", (1024, 352)), ("ffn2_w_up", "cols", (1024, 352)), ("ffn2_w_down", "rows", (352, 1024)),
]
REPLICATED = [n for n in WEIGHT_NAMES if n not in {s[0] for s in SHARDED}]
ROW_ALIGN = 16
ROW_BLOCKS = (512, 384, 352, 256, 192, 176, 128)
GROUPS = {"ffn1_gu": ("ffn1_w_gate", "ffn1_w_up"), "ffn1_d": ("ffn1_w_down",),
          "ffn1_w_gate": ("ffn1_w_gate",), "ffn1_w_up": ("ffn1_w_up",), "ffn1_w_down": ("ffn1_w_down",),
          "mixer": ("w_in", "conv_w", "w_dn_branch", "w_pool_branch", "w_mix_out"),
          "xa": ("xa_wq", "xa_wk", "xa_wv", "xa_wo"),
          "ffn2": ("ffn2_w_gate", "ffn2_w_up", "ffn2_w_down")}
W_IN_COLS = 577
W_IN_PIECES = (("in_qkv", 0, 1536), ("in_z", 1536, 2048), ("in_ab", 2048, 2056), ("in_p", 2056, 2568),
               ("in_gates", 2568, 4616))


def _round_up(n, m):
    return -(-n // m) * m


def _layout():
    off, table = 0, {}
    for name, form, shape in SHARDED:
        valid = {"rows": shape[0], "cols": shape[1], "flat": 2}[form]
        width = {"rows": shape[1], "cols": shape[0], "flat": shape[0] * shape[1]}[form]
        rows = _round_up(valid, ROW_ALIGN)
        table[name] = (off, rows, valid, width, form, shape)
        off += rows
    return table


LAYOUT = _layout()


def _group_span(names):
    base = LAYOUT[names[0]][0]
    rows = LAYOUT[names[-1]][0] + LAYOUT[names[-1]][1] - base
    while not any(rows % b == 0 for b in ROW_BLOCKS):
        rows += ROW_ALIGN
    return base, rows


def _row_block(rows):
    return _pick(rows, ROW_BLOCKS)


def _pad_block(blk, rows):
    return jnp.pad(blk, ((0, rows - blk.shape[0]), (0, LANES - blk.shape[1])))


def pack_weight_shards(shards, names):
    parts, used = [], 0
    for name in names:
        off, rows, valid, width, form, _ = LAYOUT[name]
        s = shards[name]
        if form == "flat":
            flat = s.reshape(1, -1)
            hi = flat.astype(BF16)
            blk = jnp.concatenate([hi, (flat - hi.astype(F32)).astype(BF16)], axis=0)
        else:
            blk = (s.T if form == "cols" else s).astype(BF16)
        parts.append(_pad_block(blk, rows))
        used += rows
    if _group_span(names)[1] > used:
        parts.append(jnp.zeros((_group_span(names)[1] - used, LANES), BF16))
    return jnp.concatenate(parts, axis=0)


IN_AB_ROWS = 128


def _w_in_segments(first, last):
    segs = []
    for k in range(N_DEV):
        lo, hi = max(first, k * W_IN_COLS), min(last, (k + 1) * W_IN_COLS)
        if lo < hi:
            segs.append((k, lo - k * W_IN_COLS, lo - first, hi - lo))
    return segs


def w_in_pieces(name, gathered, off, rows):
    assert off % rows == 0
    sizes = [IN_AB_ROWS if piece == "in_ab" else last - first for piece, first, last in W_IN_PIECES]

    def body(src_ref, *outs):
        for o_ref, (piece, first, last) in zip(outs, W_IN_PIECES):
            if piece == "in_ab":
                o_ref[...] = jnp.zeros_like(o_ref)
            for k, src, dst, count in _w_in_segments(first, last):
                o_ref[pl.ds(dst, count), :] = src_ref[k, pl.ds(src, count), :]

    outs = pl.pallas_call(
        body, name=name, grid=(1,), in_specs=[pl.BlockSpec((N_DEV, rows, LANES), lambda i: (0, off // rows, 0))],
        out_specs=[pl.BlockSpec((n, LANES), lambda i: (0, 0)) for n in sizes],
        out_shape=[jax.ShapeDtypeStruct((n, LANES), gathered.dtype) for n in sizes],
        compiler_params=_params(("arbitrary",)),
    )(gathered)
    return {piece: o for (piece, _, _), o in zip(W_IN_PIECES, outs)}


def unpack_full_weights(gathered, names):
    out, base = {}, _group_span(names)[0]
    for name in names:
        off, rows, valid, width, form, shape = LAYOUT[name]
        seg = gathered[:, off - base:off - base + rows]
        if form == "flat":
            flat = seg[:, 0, :width].astype(F32) + seg[:, 1, :width].astype(F32)
            out[name] = flat.reshape((N_DEV,) + shape).transpose(1, 0, 2).reshape(shape[0], N_DEV * shape[1])
        elif name == "w_in":
            out.update(w_in_pieces("w_in_pieces", gathered, off - base, rows))
        else:
            out[name] = seg[:, :valid, :width].reshape(N_DEV * valid, width)
    return out


def pack_full_grads(grads, names, me):
    wire, own, used = [], [], 0
    for name in names:
        off, rows, valid, width, form, shape = LAYOUT[name]
        if form == "flat":
            full = grads[name].reshape(shape[0], N_DEV, shape[1]).transpose(1, 0, 2).reshape(N_DEV, 1, width)
        elif name == "w_in":
            full = jnp.concatenate([grads[piece][:last - first] for piece, first, last in W_IN_PIECES], axis=0)
            full = full.reshape(N_DEV, valid, width)
        else:
            full = grads[name].reshape(N_DEV, valid, width)
        pad = ((0, rows - full.shape[1]), (0, LANES - width))
        wire.append(jnp.pad(full.astype(WIRE), ((0, 0),) + pad))
        own.append(jnp.pad(lax.dynamic_index_in_dim(full, me, 0, keepdims=False), pad))
        used += rows
    if _group_span(names)[1] > used:
        wire.append(jnp.zeros((N_DEV, _group_span(names)[1] - used, LANES), WIRE))
        own.append(jnp.zeros((_group_span(names)[1] - used, LANES), F32))
    return jnp.concatenate(wire, axis=1), jnp.concatenate(own, axis=0)


TRANSPOSED = ("ffn1_w_gate", "ffn1_w_up", "ffn2_w_gate", "ffn2_w_up", "w_in")


def unpack_grad_shards(packed, names):
    out, base = {}, _group_span(names)[0]
    for name in names:
        off, rows, valid, width, form, shape = LAYOUT[name]
        off -= base
        if form == "flat":
            out[name] = packed[off, :width].reshape(shape)
        elif name in TRANSPOSED:
            out[name] = packed[off:off + valid, :width]
        elif form == "cols":
            out[name] = packed[off:off + valid, :width].T
        else:
            out[name] = packed[off:off + valid, :width]
    return out


SMALL_SHAPES = {n: (1024,) for n in REPLICATED}
SMALL_SHAPES.update(pool_w=(4, 128, 128), pool_scale=(512,), dn_norm_w=(128,), a_log=(4,), dt_bias=(4,))


SMALL_SHAPES["loss"] = (1,)
SMALL_NAMES = REPLICATED + ["loss"]


def _small_layout():
    off, table = 0, {}
    for name in SMALL_NAMES:
        numel = 1
        for d in SMALL_SHAPES[name]:
            numel *= d
        rows = _round_up(-(-numel // LANES), 8)
        table[name] = (off, rows, numel)
        off += rows
    return table, off


SMALL_LAYOUT, SMALL_ROWS = _small_layout()


def _to_rows(flat, rows):
    return jnp.pad(flat, (0, rows * LANES - flat.shape[0])).reshape(rows, LANES)


def pack_small(values):
    return jnp.concatenate([_to_rows(values[name].reshape(-1), SMALL_LAYOUT[name][1]) for name in SMALL_NAMES], axis=0)


def unpack_small(packed):
    out = {}
    for name in SMALL_NAMES:
        off, rows, numel = SMALL_LAYOUT[name]
        out[name] = packed[off:off + rows].reshape(-1)[:numel].reshape(SMALL_SHAPES[name])
    return out


MESH = pl.DeviceIdType.MESH


def _position():
    return lax.axis_index("x"), lax.axis_index("y"), lax.axis_index("c")


def _other_chips(x, y):
    return [(1 - x, y), (x, 1 - y), (1 - x, 1 - y)]


def all_gather(name, block):
    rows, n = block.shape

    def body(x_ref, out_ref, send_sems, recv_sems, local_sem):
        x, y, c = _position()
        me, sibling = (x, y, c), (x, y, 1 - c)
        chips = _other_chips(x, y)

        def slot(px, py, pc):
            return out_ref.at[4 * px + 2 * py + pc]

        def copy(k, blk, to, src=None):
            return pltpu.make_async_remote_copy(
                src_ref=slot(*blk) if src is None else src, dst_ref=slot(*blk),
                send_sem=send_sems.at[k], recv_sem=recv_sems.at[k], device_id=to, device_id_type=MESH)

        mine = pltpu.make_async_copy(x_ref, slot(*me), local_sem)
        mine.start()
        first = [copy(0, me, sibling, src=x_ref)]
        first += [copy(1 + j, me, (*chip, c), src=x_ref) for j, chip in enumerate(chips)]
        for cp in first:
            cp.start()
        passed = [copy(4 + j, (*chip, c), sibling) for j, chip in enumerate(chips)]
        for j, chip in enumerate(chips):
            copy(1 + j, (*chip, c), me).wait_recv()
            passed[j].start()
        copy(0, sibling, me).wait_recv()
        for j, chip in enumerate(chips):
            copy(4 + j, (*chip, 1 - c), me).wait_recv()
        for cp in first + passed:
            cp.wait_send()
        mine.wait()

    return pl.pallas_call(
        body, name=name, out_shape=jax.ShapeDtypeStruct((N_DEV, rows, n), block.dtype),
        in_specs=[ANY], out_specs=ANY,
        scratch_shapes=[pltpu.SemaphoreType.DMA((7,)), pltpu.SemaphoreType.DMA((7,)), pltpu.SemaphoreType.DMA(())],
    )(block)


HBM = pl.BlockSpec(memory_space=pltpu.HBM)
SEM = pl.BlockSpec(memory_space=pltpu.SEMAPHORE)
EFFECT = pltpu.SideEffectType.DATAFLOW_SIDE_EFFECTING


def _remote(src, dst, send_sem, recv_sem, to):
    return pltpu.make_async_remote_copy(src_ref=src, dst_ref=dst, send_sem=send_sem, recv_sem=recv_sem,
                                        device_id=to, device_id_type=MESH)


def split_start(name, bufs, n, make_copies):
    nb = len(bufs)

    def body(*refs):
        for out_cp, _ in make_copies(refs[:nb], refs[nb:nb + n], refs[nb + n:nb + 2 * n]):
            out_cp.start()
        refs[-1][...] = jnp.zeros_like(refs[-1])

    outs = pl.pallas_call(
        body, name=name,
        out_shape=tuple([pltpu.SemaphoreType.DMA(())] * (2 * n)) + tuple(pltpu.HBM(b.shape, b.dtype) for b in bufs)
        + (jax.ShapeDtypeStruct((8, 128), F32),),
        in_specs=[HBM] * nb,
        out_specs=tuple([SEM] * (2 * n) + [HBM] * nb + [pl.BlockSpec(memory_space=pltpu.VMEM)]),
        input_output_aliases={i: 2 * n + i for i in range(nb)},
        compiler_params=pltpu.CompilerParams(has_side_effects=EFFECT),
    )(*[pltpu.with_memory_space_constraint(b, pltpu.HBM) for b in bufs])
    return list(outs[:2 * n]), list(outs[2 * n:2 * n + nb]), outs[-1]


def split_wait(name, bufs, sems, n, make_copies, after):
    nb = len(bufs)

    def body(*refs):
        for out_cp, in_cp in make_copies(refs[:nb], refs[nb:nb + n], refs[nb + n:nb + 2 * n]):
            out_cp.wait_send()
            in_cp.wait_recv()

    outs = pl.pallas_call(
        body, name=name, out_shape=tuple(pltpu.HBM(b.shape, b.dtype) for b in bufs),
        in_specs=[HBM] * nb + [SEM] * (2 * n) + [ANY], out_specs=tuple([HBM] * nb),
        input_output_aliases={i: i for i in range(nb)},
        compiler_params=pltpu.CompilerParams(has_side_effects=EFFECT),
    )(*bufs, *sems, after)
    return list(outs)


def _gather_stage1(refs, send, recv):
    src, land = refs
    x, y, c = _position()
    peers = [(x, y, 1 - c)] + [(*chip, c) for chip in _other_chips(x, y)]
    return [(_remote(src, land.at[4 * x + 2 * y + c], send[k], recv[k], p),
             _remote(src, land.at[4 * p[0] + 2 * p[1] + p[2]], send[k], recv[k], p)) for k, p in enumerate(peers)]


def _gather_stage2(refs, send, recv):
    (land,) = refs
    x, y, c = _position()
    out = []
    for j, (px, py) in enumerate(_other_chips(x, y)):
        mine, theirs = land.at[4 * px + 2 * py + c], land.at[4 * px + 2 * py + 1 - c]
        out.append((_remote(mine, mine, send[j], recv[j], (x, y, 1 - c)),
                    _remote(theirs, theirs, send[j], recv[j], (x, y, 1 - c))))
    return out


def _flips():
    return [(a, b, d) for a in (0, 1) for b in (0, 1) for d in (0, 1) if a | b | d]


def _gather_direct(refs, send, recv):
    src, land = refs
    x, y, c = _position()
    out = []
    for k, (fx, fy, fc) in enumerate(_flips()):
        p = (1 - x if fx else x, 1 - y if fy else y, 1 - c if fc else c)
        out.append((_remote(src, land.at[4 * x + 2 * y + c], send[k], recv[k], p),
                    _remote(src, land.at[4 * p[0] + 2 * p[1] + p[2]], send[k], recv[k], p)))
    return out


def _scatter_direct(refs, send, recv):
    sendbuf, land = refs
    x, y, c = _position()
    out = []
    for k, (fx, fy, fc) in enumerate(_flips()):
        p = (1 - x if fx else x, 1 - y if fy else y, 1 - c if fc else c)
        cp = _remote(sendbuf.at[4 * p[0] + 2 * p[1] + p[2]], land.at[k], send[k], recv[k], p)
        out.append((cp, cp))
    return out


def _own_plus_slots(name, own, landed):
    n, rows, _ = landed.shape
    tr = _row_block(rows)

    def body(g_ref, l_ref, o_ref):
        acc = g_ref[...]
        for j in range(n):
            acc = acc + l_ref[j].astype(F32)
        o_ref[...] = acc

    return pl.pallas_call(
        body, name=name, grid=(rows // tr,),
        in_specs=[pl.BlockSpec((tr, LANES), lambda i: (i, 0)), pl.BlockSpec((n, tr, LANES), lambda i: (0, i, 0))],
        out_specs=pl.BlockSpec((tr, LANES), lambda i: (i, 0)),
        out_shape=jax.ShapeDtypeStruct((rows, LANES), F32), compiler_params=_params(("parallel",)),
    )(own, landed)


def _sum_slots(name, stack):
    n, rows, _ = stack.shape

    def body(s_ref, o_ref):
        acc = s_ref[0]
        for j in range(1, n):
            acc = acc + s_ref[j]
        o_ref[...] = acc

    return pl.pallas_call(
        body, name=name, in_specs=[pl.BlockSpec(stack.shape, lambda: (0, 0, 0))],
        out_specs=pl.BlockSpec((rows, LANES), lambda: (0, 0)), out_shape=jax.ShapeDtypeStruct((rows, LANES), F32),
    )(stack)


def adamw(name, w, g, m, v):
    shape = w.shape
    last = shape[-1]
    w2, g2, m2, v2 = [a.reshape(-1, last) for a in (w, g, m, v)]
    rows = w2.shape[0]
    tr = _pick(rows, (256, 176, 128))

    def body(w_ref, g_ref, m_ref, v_ref, d_ref, nm_ref, nv_ref):
        gg = g_ref[...]
        nm = ADAM_B1 * m_ref[...] + (1.0 - ADAM_B1) * gg
        nv = ADAM_B2 * v_ref[...] + (1.0 - ADAM_B2) * (gg * gg)
        m_hat = nm / (1.0 - ADAM_B1 ** ADAM_STEP)
        v_hat = nv / (1.0 - ADAM_B2 ** ADAM_STEP)
        d_ref[...] = -ADAM_LR * (m_hat / (jnp.sqrt(v_hat) + ADAM_EPS) + ADAM_WD * w_ref[...])
        nm_ref[...] = nm
        nv_ref[...] = nv

    spec = pl.BlockSpec((tr, last), lambda i: (i, 0))
    outs = pl.pallas_call(
        body, name=name, grid=(rows // tr,), in_specs=[spec] * 4, out_specs=[spec] * 3,
        out_shape=[jax.ShapeDtypeStruct((rows, last), F32)] * 3, compiler_params=_params(("parallel",)),
    )(w2, g2, m2, v2)
    return [o.reshape(shape) for o in outs]


def _landing(block_shape, dtype, own):
    x, y, c = _position()
    return lax.dynamic_update_slice(lax.empty((N_DEV,) + block_shape, dtype), own[None], (4 * x + 2 * y + c, 0, 0))


class _Exchanges:
    def __init__(self, shards):
        self.shards = shards
        self.pending = {}
        self.reduced = {}

    def first_weights(self):
        names = GROUPS["ffn1_gu"]
        return unpack_full_weights(all_gather("ag_ffn1_gu", pack_weight_shards(self.shards, names)), names)

    def rest_started(self):
        tokens = []
        block = pack_weight_shards(self.shards, GROUPS["ffn1_d"])
        sems, bufs, token = split_start("ag_ffn1_d_s", [block, _landing(block.shape, block.dtype, block)], N_DEV - 1,
                                        _gather_direct)
        self.pending["ffn1_d"] = (sems, bufs)
        tokens.append(token)
        for key in ("mixer", "xa", "ffn2"):
            block = pack_weight_shards(self.shards, GROUPS[key])
            sems, bufs, token = split_start(f"ag_{key}_s1", [block, _landing(block.shape, block.dtype, block)], 4,
                                            _gather_stage1)
            self.pending[key] = (sems, bufs)
            tokens.append(token)
        return tuple(tokens)

    def ffn1_down(self, after):
        sems, bufs = self.pending.pop("ffn1_d")
        _, gathered = split_wait("ag_ffn1_d_w", bufs, sems, N_DEV - 1, _gather_direct, after)
        return unpack_full_weights(gathered, GROUPS["ffn1_d"])["ffn1_w_down"]

    def halfway(self, key, after):
        sems, bufs = self.pending.pop(key)
        _, land = split_wait(f"ag_{key}_w1", bufs, sems, 4, _gather_stage1, after)
        sems, bufs, token = split_start(f"ag_{key}_s2", [land], 3, _gather_stage2)
        self.pending[key] = (sems, bufs)
        return (token,)

    def weights(self, key, after):
        sems, bufs = self.pending.pop(key)
        (gathered,) = split_wait(f"ag_{key}_w2", bufs, sems, 3, _gather_stage2, after)
        return unpack_full_weights(gathered, GROUPS[key])

    def grads_out(self, key, grads):
        x, y, c = _position()
        wire, own = pack_full_grads(grads, GROUPS[key], 4 * x + 2 * y + c)
        land = lax.empty((N_DEV - 1,) + wire.shape[1:], WIRE)
        sems, bufs, token = split_start(f"rs_{key}_start", [wire, land], N_DEV - 1, _scatter_direct)
        self.pending[key] = (sems, bufs, own)
        return (token,)

    def grads_in(self, key, after):
        sems, bufs, own = self.pending.pop(key)
        _, landed = split_wait(f"rs_{key}_wait", bufs, sems, N_DEV - 1, _scatter_direct, after)
        self.reduced.update(unpack_grad_shards(_own_plus_slots(f"rs_{key}_sum", own, landed), GROUPS[key]))

    def small_out(self, values):
        block = pack_small(values)
        sems, bufs, token = split_start("ag_small_s", [block, _landing(block.shape, block.dtype, block)], N_DEV - 1,
                                        _gather_direct)
        self.pending["small"] = (sems, bufs)
        return (token,)

    def small_in(self, after):
        sems, bufs = self.pending.pop("small")
        _, gathered = split_wait("ag_small_w", bufs, sems, N_DEV - 1, _gather_direct, after)
        return unpack_small(_sum_slots("small_sum", gathered))


def kernel(x, mem, ffn1_w_gate, ffn1_w_up, ffn1_w_down, ln1_g, ln1_b, w_in, conv_w, a_log, dt_bias, dn_norm_w, w_dn_branch, pool_w, pool_scale, w_pool_branch, w_mix_out, ln2_g, ln2_b, mem_ln_g, mem_ln_b, xa_wq, xa_wk, xa_wv, xa_wo, ln3_g, ln3_b, ffn2_w_gate, ffn2_w_up, ffn2_w_down, ln4_g, ln4_b, loss_target, m_ffn1_w_gate, m_ffn1_w_up, m_ffn1_w_down, m_ln1_g, m_ln1_b, m_w_in, m_conv_w, m_a_log, m_dt_bias, m_dn_norm_w, m_w_dn_branch, m_pool_w, m_pool_scale, m_w_pool_branch, m_w_mix_out, m_ln2_g, m_ln2_b, m_mem_ln_g, m_mem_ln_b, m_xa_wq, m_xa_wk, m_xa_wv, m_xa_wo, m_ln3_g, m_ln3_b, m_ffn2_w_gate, m_ffn2_w_up, m_ffn2_w_down, m_ln4_g, m_ln4_b, v_ffn1_w_gate, v_ffn1_w_up, v_ffn1_w_down, v_ln1_g, v_ln1_b, v_w_in, v_conv_w, v_a_log, v_dt_bias, v_dn_norm_w, v_w_dn_branch, v_pool_w, v_pool_scale, v_w_pool_branch, v_w_mix_out, v_ln2_g, v_ln2_b, v_mem_ln_g, v_mem_ln_b, v_xa_wq, v_xa_wk, v_xa_wv, v_xa_wo, v_ln3_g, v_ln3_b, v_ffn2_w_gate, v_ffn2_w_up, v_ffn2_w_down, v_ln4_g, v_ln4_b):
    given = dict(locals())
    shards = {n: given[n] for n in WEIGHT_NAMES}
    io = _Exchanges({n: shards[n][0] for n, _, _ in SHARDED})
    w = io.first_weights()
    for n in REPLICATED:
        w[n] = shards[n][0] if n == "pool_w" else shards[n]
    loss_part, grad_x, g = local_step(x[0], mem[0], loss_target[0], w, io)

    grad, updates = {}, {}

    def update(names, reduced):
        for n in names:
            if n in TRANSPOSED:
                outs = adamw("adamw_" + n, shards[n][0].T, reduced[n], given["m_" + n][0].T, given["v_" + n][0].T)
                grad[n], updates[n] = reduced[n].T[None], [o.T[None] for o in outs]
            else:
                grad[n] = reduced[n].reshape(shards[n].shape)
                updates[n] = adamw("adamw_" + n, shards[n], grad[n], given["m_" + n], given["v_" + n])
        return updates[names[-1]][0]

    update(GROUPS["ffn2"] + GROUPS["xa"], io.reduced)
    io.grads_in("mixer", grad_x)
    done = update(GROUPS["mixer"], io.reduced)
    small = io.small_in(done)
    loss = small.pop("loss")[0]
    done = update(REPLICATED, small)
    for n in ("ffn1_w_down", "ffn1_w_gate", "ffn1_w_up"):
        io.grads_in(n, done)
        done = update(GROUPS[n], io.reduced)
    return (loss, grad_x[None], *[grad[n] for n in WEIGHT_NAMES], *[updates[n][0] for n in WEIGHT_NAMES],
            *[updates[n][1] for n in WEIGHT_NAMES], *[updates[n][2] for n in WEIGHT_NAMES])
```
